```python
import math
import jax, jax.numpy as jnp
from jax import lax
import numpy as np

D_MODEL = 1024
BATCH = 16
SEQ = 2048
DEPTH = 1

HEAD_DIM = 64
SWA_Q_HEADS = 8
SWA_KV_HEADS = 2
SB_HEADS = 8
WINDOW = 128
BLOCK = 128
ROPE_THETA = 10000.0
EPS = 1e-6
SWA_WIDTH = SWA_Q_HEADS * HEAD_DIM
SWA_KV_WIDTH = SWA_KV_HEADS * HEAD_DIM
SB_WIDTH = SB_HEADS * HEAD_DIM
MIX_WIDTH = SWA_WIDTH + SB_WIDTH
SPLITS = (SWA_WIDTH, SWA_KV_WIDTH, SWA_KV_WIDTH, SWA_WIDTH, SB_WIDTH, SB_WIDTH, SB_WIDTH, SB_WIDTH)
IN_WIDTH = sum(SPLITS)

kernel_name = "hybrid_swa_sink_stickbreaking_heads"


def rmsnorm(x, g):
    xf = x.astype(jnp.float32)
    y = xf * lax.rsqrt(jnp.mean(xf * xf, axis=-1, keepdims=True) + EPS)
    return (y * g.astype(jnp.float32)).astype(x.dtype)


def rope(x, positions):
    half = x.shape[-1] // 2
    inv_freq = ROPE_THETA ** (-jnp.arange(half, dtype=jnp.float32) * 2.0 / x.shape[-1])
    ang = positions.astype(jnp.float32)[..., None] * inv_freq
    cos = jnp.cos(ang)[:, :, None, :]
    sin = jnp.sin(ang)[:, :, None, :]
    xf = x.astype(jnp.float32)
    x1, x2 = xf[..., :half], xf[..., half:]
    out = jnp.concatenate([x1 * cos - x2 * sin, x2 * cos + x1 * sin], axis=-1)
    return out.astype(x.dtype)


def swa_sink_attention(q, k, v, sinks):
    B, S, Hq, d = q.shape
    Hkv = k.shape[2]
    G = Hq // Hkv
    nb = S // BLOCK
    scale = 1.0 / math.sqrt(d)
    qb = q.reshape(B, nb, BLOCK, Hkv, G, d)
    kb = k.reshape(B, nb, BLOCK, Hkv, d)
    vb = v.reshape(B, nb, BLOCK, Hkv, d)
    kprev = jnp.concatenate([jnp.zeros_like(kb[:, :1]), kb[:, :-1]], axis=1)
    vprev = jnp.concatenate([jnp.zeros_like(vb[:, :1]), vb[:, :-1]], axis=1)
    kw = jnp.concatenate([kprev, kb], axis=2)
    vw = jnp.concatenate([vprev, vb], axis=2)
    scores = jnp.einsum('bnqhgd,bnkhd->bnhgqk', qb, kw).astype(jnp.float32) * scale
    q_idx = jnp.arange(nb)[:, None] * BLOCK + jnp.arange(BLOCK)[None, :]
    k_idx = jnp.arange(nb)[:, None] * BLOCK - BLOCK + jnp.arange(2 * BLOCK)[None, :]
    diff = q_idx[:, :, None] - k_idx[:, None, :]
    mask = (diff >= 0) & (diff < WINDOW) & (k_idx[:, None, :] >= 0)
    scores = jnp.where(mask[None, :, None, None], scores, -jnp.inf)
    sink = sinks.astype(jnp.float32).reshape(Hkv, G)[None, None, :, :, None, None]
    m = jnp.maximum(jnp.max(scores, axis=-1, keepdims=True), sink)
    p = jnp.exp(scores - m)
    probs = p / (jnp.sum(p, axis=-1, keepdims=True) + jnp.exp(sink - m))
    out = jnp.einsum('bnhgqk,bnkhd->bnqhgd', probs.astype(v.dtype), vw)
    return out.reshape(B, S, Hq * d)


def stick_breaking_attention(q, k, v):
    B, S, H, d = q.shape
    nb = S // BLOCK
    scale = 1.0 / math.sqrt(d)
    outs = []
    for i in range(nb):
        t0 = i * BLOCK
        L = t0 + BLOCK
        qi = q[:, t0:L]
        kp = k[:, :L]
        vp = v[:, :L]
        z = jnp.einsum('bqhd,bkhd->bhqk', qi, kp).astype(jnp.float32) * scale
        t_idx = t0 + jnp.arange(BLOCK)[:, None]
        s_idx = jnp.arange(L)[None, :]
        causal = s_idx < t_idx
        log_fail = jnp.where(causal, jax.nn.log_sigmoid(-z), 0.0)
        after = lax.cumsum(log_fail, axis=3, reverse=True) - log_fail
        w = jnp.where(causal, jnp.exp(jax.nn.log_sigmoid(z) + after), 0.0)
        outs.append(jnp.einsum('bhqk,bkhd->bqhd', w.astype(v.dtype), vp))
    out = jnp.concatenate(outs, axis=1)
    return out.reshape(B, S, H * d)


def _fwd_setup_inputs(seed: int = 0) -> dict:
    key = jax.random.key(seed)
    ks = jax.random.split(key, 8)
    x = jax.random.normal(ks[0], (BATCH, SEQ, D_MODEL), jnp.float32)
    positions = jnp.broadcast_to(jnp.arange(SEQ, dtype=jnp.int32)[None, :], (BATCH, SEQ)).astype(jnp.int32)
    norm_gain = 1.0 + 0.02 * jax.random.normal(ks[1], (DEPTH, D_MODEL), jnp.float32)
    w_in = jax.random.normal(ks[2], (DEPTH, D_MODEL, IN_WIDTH), jnp.float32) * D_MODEL ** -0.5
    q_norm_gain = 1.0 + 0.02 * jax.random.normal(ks[3], (DEPTH, HEAD_DIM), jnp.float32)
    k_norm_gain = 1.0 + 0.02 * jax.random.normal(ks[4], (DEPTH, HEAD_DIM), jnp.float32)
    sinks = 0.5 * jax.random.normal(ks[5], (DEPTH, SWA_Q_HEADS), jnp.float32)
    w_out = jax.random.normal(ks[6], (DEPTH, MIX_WIDTH, D_MODEL), jnp.float32) * MIX_WIDTH ** -0.5
    return {"x": x, "positions": positions, "norm_gain": norm_gain, "w_in": w_in,
            "q_norm_gain": q_norm_gain, "k_norm_gain": k_norm_gain, "sinks": sinks, "w_out": w_out}


def _fwd_reference(x, positions, norm_gain, w_in, q_norm_gain, k_norm_gain, sinks, w_out):
    B, S, _ = x.shape
    split_points = list(np.cumsum(SPLITS)[:-1])
    for l in range(DEPTH):
        h = rmsnorm(x, norm_gain[l])
        proj = jnp.einsum('bsd,de->bse', h, w_in[l])
        qa, ka, va, ga, qb, kb, vb, gb = jnp.split(proj, split_points, axis=-1)
        qa = rope(rmsnorm(qa.reshape(B, S, SWA_Q_HEADS, HEAD_DIM), q_norm_gain[l]), positions)
        ka = rope(rmsnorm(ka.reshape(B, S, SWA_KV_HEADS, HEAD_DIM), k_norm_gain[l]), positions)
        va = va.reshape(B, S, SWA_KV_HEADS, HEAD_DIM)
        ya = swa_sink_attention(qa, ka, va, sinks[l]) * jax.nn.silu(ga)
        qb = qb.reshape(B, S, SB_HEADS, HEAD_DIM)
        kb = kb.reshape(B, S, SB_HEADS, HEAD_DIM)
        vb = vb.reshape(B, S, SB_HEADS, HEAD_DIM)
        yb = stick_breaking_attention(qb, kb, vb) * jax.nn.silu(gb)
        y = jnp.concatenate([ya, yb], axis=-1)
        x = x + jnp.einsum('bse,ed->bsd', y, w_out[l])
    return x


import jax as _jax
import jax.numpy as _jnp

TWIN_FORMAT = 'train_step'
FWD_PARAMS = ['x', 'positions', 'norm_gain', 'w_in', 'q_norm_gain', 'k_norm_gain', 'sinks', 'w_out']
TWIN_WEIGHTS = ['norm_gain', 'w_in', 'q_norm_gain', 'k_norm_gain', 'sinks', 'w_out']
TWIN_DIFF_INPUT = 'x'
TWIN_INPUTS = ['x', 'positions', 'norm_gain', 'w_in', 'q_norm_gain', 'k_norm_gain', 'sinks', 'w_out', 'loss_target', 'm_norm_gain', 'm_w_in', 'm_q_norm_gain', 'm_k_norm_gain', 'm_sinks', 'm_w_out', 'v_norm_gain', 'v_w_in', 'v_q_norm_gain', 'v_k_norm_gain', 'v_sinks', 'v_w_out']
TWIN_OUTPUTS = ['loss', 'grad_x', 'grad_norm_gain', 'grad_w_in', 'grad_q_norm_gain', 'grad_k_norm_gain', 'grad_sinks', 'grad_w_out', 'delta_norm_gain', 'delta_w_in', 'delta_q_norm_gain', 'delta_k_norm_gain', 'delta_sinks', 'delta_w_out', 'new_m_norm_gain', 'new_m_w_in', 'new_m_q_norm_gain', 'new_m_k_norm_gain', 'new_m_sinks', 'new_m_w_out', 'new_v_norm_gain', 'new_v_w_in', 'new_v_q_norm_gain', 'new_v_k_norm_gain', 'new_v_sinks', 'new_v_w_out']
TWIN_LEAF_KINDS = {'loss': 'loss', 'grad_x': 'grad_x', 'grad_norm_gain': 'grad_w', 'grad_w_in': 'grad_w', 'grad_q_norm_gain': 'grad_w', 'grad_k_norm_gain': 'grad_w', 'grad_sinks': 'grad_w', 'grad_w_out': 'grad_w', 'delta_norm_gain': 'delta_w', 'delta_w_in': 'delta_w', 'delta_q_norm_gain': 'delta_w', 'delta_k_norm_gain': 'delta_w', 'delta_sinks': 'delta_w', 'delta_w_out': 'delta_w', 'new_m_norm_gain': 'new_m', 'new_m_w_in': 'new_m', 'new_m_q_norm_gain': 'new_m', 'new_m_k_norm_gain': 'new_m', 'new_m_sinks': 'new_m', 'new_m_w_out': 'new_m', 'new_v_norm_gain': 'new_v', 'new_v_w_in': 'new_v', 'new_v_q_norm_gain': 'new_v', 'new_v_k_norm_gain': 'new_v', 'new_v_sinks': 'new_v', 'new_v_w_out': 'new_v'}


def _forward(args):
    return _fwd_reference(*[args[k] for k in FWD_PARAMS])


def _output_shape():
    out = _jax.eval_shape(lambda: _forward(_fwd_setup_inputs(0)))
    return out.shape, out.dtype

N_MICROBATCH = 1
ADAM_LR = 0.001
ADAM_B1 = 0.9
ADAM_B2 = 0.999
ADAM_EPS = 1e-08
ADAM_WD = 0.01
ADAM_STEP = 10
PER_EXAMPLE_BATCH_AXIS = {'x': 0, 'positions': 0, 'loss_target': 0}
SHARED_INPUTS = []
_WEIGHT_DTYPES = {'norm_gain': _jnp.float32, 'w_in': _jnp.float32, 'q_norm_gain': _jnp.float32, 'k_norm_gain': _jnp.float32, 'sinks': _jnp.float32, 'w_out': _jnp.float32}
MOMENT_SCALE = {'norm_gain': 5.657210e+00, 'w_in': 9.153622e-02, 'q_norm_gain': 1.059892e+00, 'k_norm_gain': 1.061043e+00, 'sinks': 2.602511e-01, 'w_out': 8.780075e-02}


def _to_microbatches(a, axis):
    t = _jnp.moveaxis(a, axis, 0)
    t = t.reshape((N_MICROBATCH, t.shape[0] // N_MICROBATCH) + t.shape[1:])
    return _jnp.moveaxis(t, 1, axis + 1)


def setup_inputs(seed: int = 0) -> dict:
    inp = _fwd_setup_inputs(seed)
    key = _jax.random.fold_in(_jax.random.key(seed), 7919)
    shape, _ = _output_shape()
    out = dict(inp)
    out["loss_target"] = _jax.random.normal(_jax.random.fold_in(key, 0), shape, _jnp.float32)
    for i, name in enumerate(TWIN_WEIGHTS):
        w = inp[name].astype(_jnp.float32)
        if MOMENT_SCALE is None:
            s = _jnp.sqrt(_jnp.mean(_jnp.square(w)) + 1e-30)
        else:
            s = MOMENT_SCALE[name]
        km, kv = _jax.random.split(_jax.random.fold_in(key, i + 1))
        out[name] = w
        out["m_" + name] = s * _jax.random.normal(km, w.shape, _jnp.float32)
        out["v_" + name] = (s * s) * _jax.random.uniform(kv, w.shape, _jnp.float32, 0.5, 1.5)
    if N_MICROBATCH > 1:
        for name, axis in PER_EXAMPLE_BATCH_AXIS.items():
            out[name] = _to_microbatches(out[name], axis)
    return {'x': out['x'], 'positions': out['positions'], 'norm_gain': out['norm_gain'], 'w_in': out['w_in'], 'q_norm_gain': out['q_norm_gain'], 'k_norm_gain': out['k_norm_gain'], 'sinks': out['sinks'], 'w_out': out['w_out'], 'loss_target': out['loss_target'], 'm_norm_gain': out['m_norm_gain'], 'm_w_in': out['m_w_in'], 'm_q_norm_gain': out['m_q_norm_gain'], 'm_k_norm_gain': out['m_k_norm_gain'], 'm_sinks': out['m_sinks'], 'm_w_out': out['m_w_out'], 'v_norm_gain': out['v_norm_gain'], 'v_w_in': out['v_w_in'], 'v_q_norm_gain': out['v_q_norm_gain'], 'v_k_norm_gain': out['v_k_norm_gain'], 'v_sinks': out['v_sinks'], 'v_w_out': out['v_w_out']}


def _loss(weights, diff, rest, loss_target):
    with _jax.named_scope("forward"):
        args = {**rest, TWIN_DIFF_INPUT: diff, **{k: w.astype(_WEIGHT_DTYPES[k]) for k, w in weights.items()}}
        y = _forward(args)
    with _jax.named_scope("loss_head"):
        err = _jnp.square(y.astype(_jnp.float32) - loss_target)
        return 0.5 * _jnp.sum(_jnp.mean(err, axis=-1)) if err.ndim else 0.5 * err


def _adamw(w, g, m, v):
    m = ADAM_B1 * m + (1.0 - ADAM_B1) * g
    v = ADAM_B2 * v + (1.0 - ADAM_B2) * _jnp.square(g)
    m_hat = m / (1.0 - ADAM_B1 ** ADAM_STEP)
    v_hat = v / (1.0 - ADAM_B2 ** ADAM_STEP)
    delta = -ADAM_LR * (m_hat / (_jnp.sqrt(v_hat) + ADAM_EPS) + ADAM_WD * w)
    return delta, m, v


def reference(x, positions, norm_gain, w_in, q_norm_gain, k_norm_gain, sinks, w_out, loss_target, m_norm_gain, m_w_in, m_q_norm_gain, m_k_norm_gain, m_sinks, m_w_out, v_norm_gain, v_w_in, v_q_norm_gain, v_k_norm_gain, v_sinks, v_w_out):
    given = dict(x=x, positions=positions, norm_gain=norm_gain, w_in=w_in, q_norm_gain=q_norm_gain, k_norm_gain=k_norm_gain, sinks=sinks, w_out=w_out, loss_target=loss_target, m_norm_gain=m_norm_gain, m_w_in=m_w_in, m_q_norm_gain=m_q_norm_gain, m_k_norm_gain=m_k_norm_gain, m_sinks=m_sinks, m_w_out=m_w_out, v_norm_gain=v_norm_gain, v_w_in=v_w_in, v_q_norm_gain=v_q_norm_gain, v_k_norm_gain=v_k_norm_gain, v_sinks=v_sinks, v_w_out=v_w_out)
    weights = {n: given[n] for n in TWIN_WEIGHTS}
    shared = {n: given[n] for n in SHARED_INPUTS}
    per_example = {n: given[n] for n in ['x', 'positions']}
    grad_fn = _jax.value_and_grad(_loss, argnums=(0, 1))

    def one_microbatch(ex, loss_target):
        ex = dict(ex)
        diff = ex.pop(TWIN_DIFF_INPUT)
        return grad_fn(weights, diff, {**shared, **ex}, loss_target)

    if N_MICROBATCH == 1:
        loss, (grad_w, grad_x) = one_microbatch(per_example, given["loss_target"])
    else:
        def body(carry, xs):
            loss_sum, grad_sum = carry
            l_k, (gw_k, gx_k) = one_microbatch(xs[0], xs[1])
            with _jax.named_scope("update"):
                return (loss_sum + l_k, _jax.tree.map(_jnp.add, grad_sum, gw_k)), gx_k

        init = (_jnp.zeros((), _jnp.float32), _jax.tree.map(_jnp.zeros_like, weights))
        (loss, grad_w), grad_x = _jax.lax.scan(body, init, (per_example, given["loss_target"]))
    with _jax.named_scope("update"):
        delta_w, new_m, new_v = {}, {}, {}
        for n in TWIN_WEIGHTS:
            delta_w[n], new_m[n], new_v[n] = _adamw(weights[n], grad_w[n], given["m_" + n], given["v_" + n])
    return (loss, grad_x, *[grad_w[n] for n in TWIN_WEIGHTS], *[delta_w[n] for n in TWIN_WEIGHTS],
            *[new_m[n] for n in TWIN_WEIGHTS], *[new_v[n] for n in TWIN_WEIGHTS])
```

```python
import functools
import math

import jax
import jax.numpy as jnp
from jax import lax
from jax.experimental import pallas as pl
from jax.experimental.pallas import tpu as pltpu

F32 = jnp.float32
BF16 = jnp.bfloat16

D_MODEL = 1024
HEAD_DIM = 64
BLOCK = 128
ROPE_THETA = 10000.0
EPS = 1e-6
QA, KA, VA, GA, QB, KB, VB, GB = 0, 512, 640, 768, 1280, 1792, 2304, 2816
IN_WIDTH = 3328
SHARD_IN = IN_WIDTH // 4
SHARD_OUT = D_MODEL // 4
SCALE = 1.0 / math.sqrt(HEAD_DIM)
NEG = -1e30

ADAM_LR, ADAM_B1, ADAM_B2, ADAM_EPS, ADAM_WD, ADAM_STEP = 0.001, 0.9, 0.999, 1e-08, 0.01, 10

TM = 256
VMEM_LIMIT = 56 * 1024 * 1024
MESH = pl.DeviceIdType.MESH


def _dot(a, b):
    return jnp.dot(a, b, preferred_element_type=F32)


def _dot_nt(a, b):
    return lax.dot_general(a, b, (((1,), (1,)), ((), ())), preferred_element_type=F32)


def _dot_tn(a, b):
    return lax.dot_general(a, b, (((0,), (0,)), ((), ())), preferred_element_type=F32)


def _dot_hl(a, m):
    hi = a.astype(BF16)
    lo = (a - hi.astype(F32)).astype(BF16)
    return _dot(hi, m) + _dot(lo, m)


def _params(n_axes=None, vmem=VMEM_LIMIT):
    sem = None if n_axes is None else ("arbitrary",) * n_axes
    return pltpu.CompilerParams(dimension_semantics=sem, vmem_limit_bytes=vmem)


def _const_spec(shape):
    nd = len(shape)
    return pl.BlockSpec(shape, lambda *_: (0,) * nd)


def _rope_fwd(x, cos, sa, sb):
    return x * cos + pltpu.roll(x, 96, 1) * sa + pltpu.roll(x, 32, 1) * sb


def _rope_bwd(d, cos, sa, sb):
    return d * cos - pltpu.roll(d, 96, 1) * sa - pltpu.roll(d, 32, 1) * sb


def _lane_mask(rows, dtype=F32):
    lane = lax.broadcasted_iota(jnp.int32, (rows, 128), 1)
    return jnp.where(lane < HEAD_DIM, 1.0, 0.0).astype(dtype)


def _norm_proj(x2, ng, w_bf, qg512, kg128, g512, cos, sa, sb):
    T = x2.shape[0]

    def body(x_ref, ng_ref, w_ref, qg_ref, kg_ref, g_ref, cos_ref, sa_ref, sb_ref,
             h_ref, qraw_ref, kraw_ref, qrot_ref, k2_ref, v2_ref, ga_ref, qb_ref, kb_ref, vb_ref, gb_ref):
        xb = x_ref[...]
        r = lax.rsqrt(jnp.mean(xb * xb, axis=-1, keepdims=True) + EPS)
        h = (xb * r * ng_ref[...]).astype(BF16)
        h_ref[...] = h
        cosv, sav, sbv = cos_ref[...], sa_ref[...], sb_ref[...]
        m0 = _lane_mask(TM)

        def dup(v):
            v0 = v * m0
            v1 = v - v0
            return v0 + pltpu.roll(v0, 64, 1), v1 + pltpu.roll(v1, 64, 1)

        qa = _dot(h, w_ref[:, QA:KA])
        qraw_ref[...] = qa
        qn = qa * lax.rsqrt(_dot_hl(qa * qa, g_ref[...]) + EPS) * qg_ref[...]
        for s in range(4):
            qrot_ref[:, s * 128:(s + 1) * 128] = _rope_fwd(qn[:, s * 128:(s + 1) * 128], cosv, sav, sbv).astype(BF16)
        ka = _dot(h, w_ref[:, KA:VA])
        kraw_ref[...] = ka
        kn = ka * lax.rsqrt(_dot_hl(ka * ka, g_ref[0:128, 0:128]) + EPS) * kg_ref[...]
        k0, k1 = dup(_rope_fwd(kn, cosv, sav, sbv))
        k2_ref[:, 0:128] = k0.astype(BF16)
        k2_ref[:, 128:256] = k1.astype(BF16)
        v0, v1 = dup(_dot(h, w_ref[:, VA:GA]))
        v2_ref[:, 0:128] = v0.astype(BF16)
        v2_ref[:, 128:256] = v1.astype(BF16)
        ga_ref[...] = _dot(h, w_ref[:, GA:QB])
        qb_ref[...] = _dot(h, w_ref[:, QB:KB]).astype(BF16)
        kb_ref[...] = _dot(h, w_ref[:, KB:VB]).astype(BF16)
        vb_ref[...] = _dot(h, w_ref[:, VB:GB]).astype(BF16)
        gb_ref[...] = _dot(h, w_ref[:, GB:IN_WIDTH])

    def rows(w):
        return pl.BlockSpec((TM, w), lambda i: (i, 0))

    outs = [(D_MODEL, BF16), (512, F32), (128, F32), (512, BF16), (256, BF16), (256, BF16), (512, F32),
            (512, BF16), (512, BF16), (512, BF16), (512, F32)]
    return pl.pallas_call(
        body, name="norm_proj", grid=(T // TM,),
        in_specs=[rows(D_MODEL), _const_spec((1, D_MODEL)), _const_spec((D_MODEL, IN_WIDTH)), _const_spec((1, 512)),
                  _const_spec((1, 128)), _const_spec((512, 512)), rows(128), rows(128), rows(128)],
        out_specs=[rows(w) for w, _ in outs],
        out_shape=[jax.ShapeDtypeStruct((T, w), dt) for w, dt in outs],
        compiler_params=_params(1),
    )(x2, ng, w_bf, qg512, kg128, g512, cos, sa, sb)


def _swa_scores(q, kst, mask, sink_ref, kv, pr):
    s_all = _dot_nt(q, kst) * SCALE
    probs, stats = [], []
    for hh in range(2):
        sink = sink_ref[kv * 4 + pr * 2 + hh]
        s = jnp.where(mask, s_all[:, hh * 256:(hh + 1) * 256], NEG)
        m = jnp.maximum(jnp.max(s, axis=1, keepdims=True), sink)
        pe = jnp.exp(s - m)
        inv = 1.0 / (jnp.sum(pe, axis=1, keepdims=True) + jnp.exp(sink - m))
        probs.append(pe * inv)
        stats.append(jnp.exp(sink - m) * inv)
    return probs, stats


def _swa_mask(i):
    r = lax.broadcasted_iota(jnp.int32, (128, 256), 0)
    c = lax.broadcasted_iota(jnp.int32, (128, 256), 1)
    return (c > r) & (c <= r + 128) & ((c >= 128) | (i > 0))


def _stack_pair(prev_ref, cur_ref, m0b):
    kc = jnp.concatenate([prev_ref[...], cur_ref[...]], axis=0)
    k0 = kc * m0b
    return jnp.concatenate([k0, kc - k0], axis=0)


def _swa_fwd(sinks, qrot, k2, v2, nbatch, seq):
    nb = seq // BLOCK
    T = nbatch * seq

    def body(sink_ref, q_ref, kp_ref, kc_ref, vp_ref, vc_ref, o_ref):
        i, kv = pl.program_id(1), pl.program_id(2)
        m0b = _lane_mask(256, BF16)
        mask = _swa_mask(i)
        kst = _stack_pair(kp_ref, kc_ref, m0b)
        vst = _stack_pair(vp_ref, vc_ref, m0b)
        for pr in range(2):
            probs, _ = _swa_scores(q_ref[:, pr * 128:(pr + 1) * 128], kst, mask, sink_ref, kv, pr)
            o_ref[:, pr * 128:(pr + 1) * 128] = _dot(jnp.concatenate(probs, axis=1).astype(BF16), vst)

    cur = lambda b, i, kv: (b * nb + i, kv)
    prev = lambda b, i, kv: (b * nb + jnp.maximum(i - 1, 0), kv)
    return pl.pallas_call(
        body, name="swa_fwd", grid=(nbatch, nb, 2),
        in_specs=[pl.BlockSpec(memory_space=pltpu.SMEM), pl.BlockSpec((128, 256), cur),
                  pl.BlockSpec((128, 128), prev), pl.BlockSpec((128, 128), cur),
                  pl.BlockSpec((128, 128), prev), pl.BlockSpec((128, 128), cur)],
        out_specs=pl.BlockSpec((128, 256), cur),
        out_shape=jax.ShapeDtypeStruct((T, 512), F32),
        compiler_params=_params(3),
    )(sinks, qrot, k2, k2, v2, v2)


def _swa_bwd(sinks, qrot, k2, v2, doa, nbatch, seq):
    nb = seq // BLOCK
    T = nbatch * seq

    def body(sink_ref, q_ref, kp_ref, kc_ref, vp_ref, vc_ref, do_ref, dq_ref, dk_ref, dv_ref, ds_ref, dkc, dvc):
        b, kv, i = pl.program_id(0), pl.program_id(1), pl.program_id(2)

        @pl.when((b == 0) & (kv == 0) & (i == 0))
        def _():
            ds_ref[...] = jnp.zeros_like(ds_ref)

        @pl.when(i == 0)
        def _():
            dkc[...] = jnp.zeros_like(dkc)
            dvc[...] = jnp.zeros_like(dvc)

        @pl.when(i < nb)
        def _():
            m0b = _lane_mask(256, BF16)
            m0 = _lane_mask(128) > 0.5
            mask = _swa_mask(i)
            kst = _stack_pair(kp_ref, kc_ref, m0b)
            vst = _stack_pair(vp_ref, vc_ref, m0b)
            dkst = jnp.zeros((512, 128), F32)
            dvst = jnp.zeros((512, 128), F32)
            for pr in range(2):
                q = q_ref[:, pr * 128:(pr + 1) * 128]
                do = do_ref[:, pr * 128:(pr + 1) * 128]
                probs, psink = _swa_scores(q, kst, mask, sink_ref, kv, pr)
                dp_all = _dot_nt(do, vst)
                ds_parts = []
                for hh in range(2):
                    dp = dp_all[:, hh * 256:(hh + 1) * 256]
                    delta = jnp.sum(probs[hh] * dp, axis=1, keepdims=True)
                    ds_parts.append(probs[hh] * (dp - delta))
                    row = pl.ds(kv * 4 + pr * 2 + hh, 1)
                    ds_ref[row, :] = ds_ref[row, :] - jnp.sum(psink[hh] * delta)
                ds_all = jnp.concatenate(ds_parts, axis=1).astype(BF16)
                p_all = jnp.concatenate(probs, axis=1).astype(BF16)
                dq_ref[:, pr * 128:(pr + 1) * 128] = _dot(ds_all, kst) * SCALE
                dkst = dkst + _dot_tn(ds_all, q) * SCALE
                dvst = dvst + _dot_tn(p_all, do)
            dk_ref[...] = dkc[...] + jnp.where(m0, dkst[0:128], dkst[256:384])
            dv_ref[...] = dvc[...] + jnp.where(m0, dvst[0:128], dvst[256:384])
            dkc[...] = jnp.where(m0, dkst[128:256], dkst[384:512])
            dvc[...] = jnp.where(m0, dvst[128:256], dvst[384:512])

        @pl.when(i == nb)
        def _():
            dk_ref[...] = dkc[...]
            dv_ref[...] = dvc[...]

    cur = lambda b, kv, i: (b * nb + jnp.minimum(i, nb - 1), kv)
    prev = lambda b, kv, i: (b * nb + jnp.maximum(jnp.minimum(i, nb - 1) - 1, 0), kv)
    done = lambda b, kv, i: (b * nb + jnp.maximum(i - 1, 0), kv)
    return pl.pallas_call(
        body, name="swa_bwd", grid=(nbatch, 2, nb + 1),
        in_specs=[pl.BlockSpec(memory_space=pltpu.SMEM), pl.BlockSpec((128, 256), cur),
                  pl.BlockSpec((128, 128), prev), pl.BlockSpec((128, 128), cur),
                  pl.BlockSpec((128, 128), prev), pl.BlockSpec((128, 128), cur),
                  pl.BlockSpec((128, 256), cur)],
        out_specs=[pl.BlockSpec((128, 256), cur), pl.BlockSpec((128, 128), done), pl.BlockSpec((128, 128), done),
                   _const_spec((8, 128))],
        out_shape=[jax.ShapeDtypeStruct((T, 512), F32), jax.ShapeDtypeStruct((T, 256), F32),
                   jax.ShapeDtypeStruct((T, 256), F32), jax.ShapeDtypeStruct((8, 128), F32)],
        scratch_shapes=[pltpu.VMEM((128, 128), F32), pltpu.VMEM((128, 128), F32)],
        compiler_params=_params(3),
    )(sinks, qrot, k2, k2, v2, v2, doa)


def _sb_masks():
    r = lax.broadcasted_iota(jnp.int32, (128, 256), 0)
    c = lax.broadcasted_iota(jnp.int32, (128, 256), 1)
    return (c & 127) < r


def _sb_logits(q, kst):
    z = _dot_nt(q, kst) * SCALE
    t = jnp.exp(-jnp.abs(z))
    sp = jnp.maximum(z, 0.0) + jnp.log(1.0 + t)
    return z, t, -sp, z - sp


def _pair_rows(ref, j, m0b):
    kj = ref[pl.ds(pl.multiple_of(j * 128, 128), 128), :]
    k0 = kj * m0b
    return jnp.concatenate([k0, kj - k0], axis=0)


def _bcast2(c0, c1):
    return jnp.concatenate([jnp.broadcast_to(c0, (128, 128)), jnp.broadcast_to(c1, (128, 128))], axis=1)


def _rowsum2(x):
    return jnp.sum(x[:, 0:128], axis=1, keepdims=True), jnp.sum(x[:, 128:256], axis=1, keepdims=True)


def _sb_fwd(qb, kb, vb, ublk, nbatch, seq):
    nb = seq // BLOCK
    T = nbatch * seq

    def body(q_ref, k_ref, v_ref, u_ref, o_ref, ctab_ref):
        i = pl.program_id(2)
        q = q_ref[...]
        m0b = _lane_mask(128, BF16)
        u = u_ref[...]
        lane = lax.broadcasted_iota(jnp.int32, (128, 128), 1)
        ctab_ref[...] = jnp.zeros_like(ctab_ref)

        def tile(j, carry, diag):
            c0, c1, acc = carry
            if not diag:
                ctab_ref[...] = jnp.where(lane == j, c0, jnp.where(lane == nb + j, c1, ctab_ref[...]))
            kst = _pair_rows(k_ref, j, m0b)
            vst = _pair_rows(v_ref, j, m0b)
            z, _, lb, la = _sb_logits(q, kst)
            if diag:
                mask = _sb_masks()
                lb = jnp.where(mask, lb, 0.0)
            w = jnp.exp(la + _dot_hl(lb, u) + _bcast2(c0, c1))
            if diag:
                w = jnp.where(mask, w, 0.0)
            r0, r1 = _rowsum2(lb)
            return c0 + r0, c1 + r1, acc + _dot(w.astype(BF16), vst)

        zc = jnp.zeros((128, 1), F32)
        carry = tile(i, (zc, zc, jnp.zeros((128, 128), F32)), True)
        carry = lax.fori_loop(0, i, lambda jj, cr: tile(i - 1 - jj, cr, False), carry)
        o_ref[...] = carry[2]

    blk = lambda b, p, i: (b * nb + i, p)
    full = lambda b, p, i: (b, p)
    tab = lambda b, p, i: ((b * 4 + p) * nb + i, 0)
    return pl.pallas_call(
        body, name="sb_fwd", grid=(nbatch, 4, nb),
        in_specs=[pl.BlockSpec((128, 128), blk), pl.BlockSpec((seq, 128), full), pl.BlockSpec((seq, 128), full),
                  _const_spec((256, 256))],
        out_specs=[pl.BlockSpec((128, 128), blk), pl.BlockSpec((128, 128), tab)],
        out_shape=[jax.ShapeDtypeStruct((T, 512), F32), jax.ShapeDtypeStruct((nbatch * 4 * nb * 128, 128), F32)],
        compiler_params=_params(3),
    )(qb, kb, vb, ublk)


def _sb_bwd(qb, kb, vb, dob, ctab, ublk, pblk, nbatch, seq):
    nb = seq // BLOCK
    T = nbatch * seq

    def body(q_ref, k_ref, v_ref, do_ref, ctab_ref, u_ref, up_ref, dq_ref, dk_ref, dv_ref):
        i = pl.program_id(2)

        @pl.when(i == 0)
        def _():
            dk_ref[...] = jnp.zeros_like(dk_ref)
            dv_ref[...] = jnp.zeros_like(dv_ref)

        q = q_ref[...]
        do = do_ref[...]
        m0b = _lane_mask(128, BF16)
        m0 = _lane_mask(128) > 0.5
        u, up = u_ref[...], up_ref[...]
        lane = lax.broadcasted_iota(jnp.int32, (128, 128), 1)

        def tile(j, carry, diag):
            s0, s1, dq = carry
            kst = _pair_rows(k_ref, j, m0b)
            vst = _pair_rows(v_ref, j, m0b)
            z, t, lb, la = _sb_logits(q, kst)
            if diag:
                mask = _sb_masks()
                lb = jnp.where(mask, lb, 0.0)
                w = jnp.where(mask, jnp.exp(la + _dot_hl(lb, u)), 0.0)
            else:
                ct = ctab_ref[...]
                c0 = jnp.sum(jnp.where(lane == j, ct, 0.0), axis=1, keepdims=True)
                c1 = jnp.sum(jnp.where(lane == nb + j, ct, 0.0), axis=1, keepdims=True)
                w = jnp.exp(la + _dot_hl(lb, u) + _bcast2(c0, c1))
            e = _dot_nt(do, vst) * w
            dlb = _bcast2(s0, s1) + _dot_hl(e, up)
            inv = 1.0 / (1.0 + t)
            pos = z >= 0.0
            dz = e * (jnp.where(pos, t, 1.0) * inv) - dlb * (jnp.where(pos, 1.0, t) * inv)
            if diag:
                dz = jnp.where(mask, dz, 0.0)
            dzb = (dz * SCALE).astype(BF16)
            dkst = _dot_tn(dzb, q)
            dvst = _dot_tn(w.astype(BF16), do)
            rows = pl.ds(pl.multiple_of(j * 128, 128), 128)
            dk_ref[rows, :] = dk_ref[rows, :] + jnp.where(m0, dkst[0:128], dkst[128:256])
            dv_ref[rows, :] = dv_ref[rows, :] + jnp.where(m0, dvst[0:128], dvst[128:256])
            x0, x1 = _rowsum2(e)
            return s0 + x0, s1 + x1, dq + _dot(dzb, kst)

        zc = jnp.zeros((128, 1), F32)
        carry = lax.fori_loop(0, i, lambda j, cr: tile(j, cr, False), (zc, zc, jnp.zeros((128, 128), F32)))
        dq_ref[...] = tile(i, carry, True)[2]

    blk = lambda b, p, i: (b * nb + i, p)
    full = lambda b, p, i: (b, p)
    tab = lambda b, p, i: ((b * 4 + p) * nb + i, 0)
    return pl.pallas_call(
        body, name="sb_bwd", grid=(nbatch, 4, nb),
        in_specs=[pl.BlockSpec((128, 128), blk), pl.BlockSpec((seq, 128), full), pl.BlockSpec((seq, 128), full),
                  pl.BlockSpec((128, 128), blk), pl.BlockSpec((128, 128), tab),
                  _const_spec((256, 256)), _const_spec((256, 256))],
        out_specs=[pl.BlockSpec((128, 128), blk), pl.BlockSpec((seq, 128), full), pl.BlockSpec((seq, 128), full)],
        out_shape=[jax.ShapeDtypeStruct((T, 512), F32)] * 3,
        compiler_params=_params(3),
    )(qb, kb, vb, dob, ctab, ublk, pblk)


def _sigmoid(g):
    return 1.0 / (1.0 + jnp.exp(-g))


def _out_proj(oa, ob, ga, gb, x2, tgt, wout_bf):
    T = x2.shape[0]

    def body(oa_ref, ob_ref, ga_ref, gb_ref, x_ref, t_ref, w_ref, y_ref, dout_ref, loss_ref):
        @pl.when(pl.program_id(0) == 0)
        def _():
            loss_ref[...] = jnp.zeros_like(loss_ref)

        ga, gb = ga_ref[...], gb_ref[...]
        ya = (oa_ref[...] * (ga * _sigmoid(ga))).astype(BF16)
        yb = (ob_ref[...] * (gb * _sigmoid(gb))).astype(BF16)
        y_ref[:, 0:512] = ya
        y_ref[:, 512:1024] = yb
        out = x_ref[...] + _dot(ya, w_ref[0:512, :]) + _dot(yb, w_ref[512:1024, :])
        diff = out - t_ref[...]
        dout_ref[...] = diff * (1.0 / D_MODEL)
        loss_ref[...] = loss_ref[...] + jnp.sum(diff * diff) * (0.5 / D_MODEL)

    rows = lambda w: pl.BlockSpec((TM, w), lambda i: (i, 0))
    return pl.pallas_call(
        body, name="out_proj", grid=(T // TM,),
        in_specs=[rows(512), rows(512), rows(512), rows(512), rows(D_MODEL), rows(D_MODEL),
                  _const_spec((D_MODEL, D_MODEL))],
        out_specs=[rows(D_MODEL), rows(D_MODEL), _const_spec((8, 128))],
        out_shape=[jax.ShapeDtypeStruct((T, D_MODEL), BF16), jax.ShapeDtypeStruct((T, D_MODEL), F32),
                   jax.ShapeDtypeStruct((8, 128), F32)],
        compiler_params=_params(1),
    )(oa, ob, ga, gb, x2, tgt, wout_bf)


def _out_proj_bwd(dout, y, oa, ob, ga, gb, wout_bf):
    T = dout.shape[0]

    def body(dout_ref, y_ref, oa_ref, ob_ref, ga_ref, gb_ref, w_ref, doa_ref, dob_ref, dga_ref, dgb_ref, dw_ref):
        @pl.when(pl.program_id(0) == 0)
        def _():
            dw_ref[...] = jnp.zeros_like(dw_ref)

        db = dout_ref[...].astype(BF16)
        dw_ref[...] = dw_ref[...] + _dot_tn(y_ref[...], db)
        for o_ref, g_ref, do_ref, dg_ref, lo in ((oa_ref, ga_ref, doa_ref, dga_ref, 0), (ob_ref, gb_ref, dob_ref, dgb_ref, 512)):
            dy = _dot_nt(db, w_ref[lo:lo + 512, :])
            g = g_ref[...]
            sg = _sigmoid(g)
            do_ref[...] = (dy * (g * sg)).astype(BF16)
            dg_ref[...] = (dy * o_ref[...] * (sg * (1.0 + g * (1.0 - sg)))).astype(BF16)

    rows = lambda w: pl.BlockSpec((TM, w), lambda i: (i, 0))
    return pl.pallas_call(
        body, name="out_proj_bwd", grid=(T // TM,),
        in_specs=[rows(D_MODEL), rows(D_MODEL), rows(512), rows(512), rows(512), rows(512),
                  _const_spec((D_MODEL, D_MODEL))],
        out_specs=[rows(512)] * 4 + [_const_spec((D_MODEL, D_MODEL))],
        out_shape=[jax.ShapeDtypeStruct((T, 512), BF16)] * 4 + [jax.ShapeDtypeStruct((D_MODEL, D_MODEL), F32)],
        compiler_params=_params(1),
    )(dout, y, oa, ob, ga, gb, wout_bf)


def _qk_grad(qraw, kraw, dqrot, dk2, dv2, qg512, kg128, g512, cos, sa, sb):
    T = qraw.shape[0]

    def body(qraw_ref, kraw_ref, dqrot_ref, dk2_ref, dv2_ref, qg_ref, kg_ref, g_ref, cos_ref, sa_ref, sb_ref,
             dqa_ref, dkv_ref, dqg_ref, dkg_ref):
        @pl.when(pl.program_id(0) == 0)
        def _():
            dqg_ref[...] = jnp.zeros_like(dqg_ref)
            dkg_ref[...] = jnp.zeros_like(dkg_ref)

        cosv, sav, sbv = cos_ref[...], sa_ref[...], sb_ref[...]
        m0 = _lane_mask(TM) > 0.5

        def head_norm_bwd(raw, dn_rot, gmat, gain):
            r = lax.rsqrt(_dot_hl(raw * raw, gmat) + EPS)
            n = raw * r
            dn = dn_rot * gain
            return r * (dn - n * _dot_hl(dn * n, gmat)), jnp.sum(dn_rot * n, axis=0, keepdims=True)

        def fold(ref):
            a, b = ref[:, 0:128], ref[:, 128:256]
            return jnp.where(m0, a + pltpu.roll(a, 64, 1), b + pltpu.roll(b, 64, 1))

        dqn = jnp.concatenate([_rope_bwd(dqrot_ref[:, s * 128:(s + 1) * 128], cosv, sav, sbv) for s in range(4)], axis=1)
        dqa, dqg = head_norm_bwd(qraw_ref[...], dqn, g_ref[...], qg_ref[...])
        dka, dkg = head_norm_bwd(kraw_ref[...], _rope_bwd(fold(dk2_ref), cosv, sav, sbv), g_ref[0:128, 0:128], kg_ref[...])
        dqa_ref[...] = dqa.astype(BF16)
        dkv_ref[:, 0:128] = dka.astype(BF16)
        dkv_ref[:, 128:256] = fold(dv2_ref).astype(BF16)
        dqg_ref[...] = dqg_ref[...] + dqg
        dkg_ref[...] = dkg_ref[...] + dkg

    rows = lambda w: pl.BlockSpec((TM, w), lambda i: (i, 0))
    return pl.pallas_call(
        body, name="qk_grad", grid=(T // TM,),
        in_specs=[rows(512), rows(128), rows(512), rows(256), rows(256), _const_spec((1, 512)), _const_spec((1, 128)),
                  _const_spec((512, 512)), rows(128), rows(128), rows(128)],
        out_specs=[rows(512), rows(256), _const_spec((1, 512)), _const_spec((1, 128))],
        out_shape=[jax.ShapeDtypeStruct((T, 512), BF16), jax.ShapeDtypeStruct((T, 256), BF16),
                   jax.ShapeDtypeStruct((1, 512), F32), jax.ShapeDtypeStruct((1, 128), F32)],
        compiler_params=_params(1),
    )(qraw, kraw, dqrot, dk2, dv2, qg512, kg128, g512, cos, sa, sb)


_PIECES = ((QA, 512), (KA, 256), (GA, 512), (QB, 512), (KB, 512), (VB, 512), (GB, 512))


def _w_in_grad(h, pieces):
    T = h.shape[0]

    def body(h_ref, *refs):
        dw_ref = refs[-1]

        @pl.when(pl.program_id(0) == 0)
        def _():
            dw_ref[...] = jnp.zeros_like(dw_ref)

        hb = h_ref[...]
        for (lo, width), p_ref in zip(_PIECES, refs[:-1]):
            dw_ref[:, lo:lo + width] = dw_ref[:, lo:lo + width] + _dot_tn(hb, p_ref[...].astype(BF16))

    rows = lambda w: pl.BlockSpec((TM, w), lambda i: (i, 0))
    return pl.pallas_call(
        body, name="w_in_grad", grid=(T // TM,),
        in_specs=[rows(D_MODEL)] + [rows(w) for _, w in _PIECES],
        out_specs=_const_spec((D_MODEL, IN_WIDTH)),
        out_shape=jax.ShapeDtypeStruct((D_MODEL, IN_WIDTH), F32),
        compiler_params=_params(1),
    )(h, *pieces)


def _x_grad(x2, dout, pieces, w_bf, ng):
    T = x2.shape[0]
    npc = len(_PIECES)

    def body(x_ref, dout_ref, *refs):
        w_ref, ng_ref, gx_ref, dng_ref = refs[npc:]

        @pl.when(pl.program_id(0) == 0)
        def _():
            dng_ref[...] = jnp.zeros_like(dng_ref)

        dh = jnp.zeros((TM, D_MODEL), F32)
        for (lo, width), p_ref in zip(_PIECES, refs[:npc]):
            dh = dh + _dot_nt(p_ref[...].astype(BF16), w_ref[:, lo:lo + width])
        xb = x_ref[...]
        r = lax.rsqrt(jnp.mean(xb * xb, axis=-1, keepdims=True) + EPS)
        n = xb * r
        dn = dh * ng_ref[...]
        gx_ref[...] = dout_ref[...] + r * (dn - n * jnp.mean(dn * n, axis=-1, keepdims=True))
        dng_ref[...] = dng_ref[...] + jnp.sum(dh * n, axis=0, keepdims=True)

    rows = lambda w: pl.BlockSpec((TM, w), lambda i: (i, 0))
    return pl.pallas_call(
        body, name="x_grad", grid=(T // TM,),
        in_specs=[rows(D_MODEL), rows(D_MODEL)] + [rows(w) for _, w in _PIECES]
        + [_const_spec((D_MODEL, IN_WIDTH)), _const_spec((1, D_MODEL))],
        out_specs=[rows(D_MODEL), _const_spec((1, D_MODEL))],
        out_shape=[jax.ShapeDtypeStruct((T, D_MODEL), F32), jax.ShapeDtypeStruct((1, D_MODEL), F32)],
        compiler_params=_params(1),
    )(x2, dout, *pieces, w_bf, ng)


HBM = pl.BlockSpec(memory_space=pl.ANY)


def _place():
    x, y, c = lax.axis_index("x"), lax.axis_index("y"), lax.axis_index("c")
    chips = [(1 - x, y), (x, 1 - y), (1 - x, 1 - y)]
    return x, y, c, chips


def _remote(src, dst, sems, k, to):
    return pltpu.make_async_remote_copy(src_ref=src, dst_ref=dst, send_sem=sems[0].at[k], recv_sem=sems[1].at[k],
                                        device_id=to, device_id_type=MESH)


def _gather_weights(win_b, wout_b):
    hi, ho = D_MODEL // 2, SHARD_OUT // 2

    def body(win_ref, wout_ref, ain_ref, aout_ref, ssem, rsem, lsem):
        x, y, c, chips = _place()
        me, sib = 2 * x + y, (x, y, 1 - c)
        sems = (ssem, rsem)
        rin, rout = pl.ds(c * hi, hi), pl.ds(c * ho, ho)
        oin, oout = pl.ds((1 - c) * hi, hi), pl.ds((1 - c) * ho, ho)
        own = [pltpu.make_async_copy(win_ref, ain_ref.at[me], lsem.at[0]),
               pltpu.make_async_copy(wout_ref, aout_ref.at[me], lsem.at[1])]
        for cp in own:
            cp.start()
        sent = []
        for j, (cx, cy) in enumerate(chips):
            sent.append(_remote(win_ref.at[rin], ain_ref.at[me, rin], sems, 2 * j, (cx, cy, c)))
            sent.append(_remote(wout_ref.at[rout], aout_ref.at[me, rout], sems, 2 * j + 1, (cx, cy, c)))
        for cp in sent:
            cp.start()
        for j, (cx, cy) in enumerate(chips):
            k = 2 * cx + cy
            _remote(win_ref.at[rin], ain_ref.at[k, rin], sems, 2 * j, sib).wait_recv()
            fwd = _remote(ain_ref.at[k, rin], ain_ref.at[k, rin], sems, 6 + 2 * j, sib)
            fwd.start()
            sent.append(fwd)
            _remote(wout_ref.at[rout], aout_ref.at[k, rout], sems, 2 * j + 1, sib).wait_recv()
            fwd = _remote(aout_ref.at[k, rout], aout_ref.at[k, rout], sems, 7 + 2 * j, sib)
            fwd.start()
            sent.append(fwd)
        for j, (cx, cy) in enumerate(chips):
            k = 2 * cx + cy
            _remote(win_ref.at[oin], ain_ref.at[k, oin], sems, 6 + 2 * j, sib).wait_recv()
            _remote(wout_ref.at[oout], aout_ref.at[k, oout], sems, 7 + 2 * j, sib).wait_recv()
        for cp in sent:
            cp.wait_send()
        for cp in own:
            cp.wait()

    return pl.pallas_call(
        body, name="gather_weights", in_specs=[HBM, HBM], out_specs=[HBM, HBM],
        out_shape=[jax.ShapeDtypeStruct((4, D_MODEL, SHARD_IN), BF16), jax.ShapeDtypeStruct((4, SHARD_OUT, D_MODEL), BF16)],
        scratch_shapes=[pltpu.SemaphoreType.DMA((12,)), pltpu.SemaphoreType.DMA((12,)), pltpu.SemaphoreType.DMA((2,))],
    )(win_b, wout_b)


def _pair_exchange(gin4, gout4, small):
    hi, ho = D_MODEL // 2, SHARD_OUT // 2

    def body(gin_ref, gout_ref, small_ref, rin_ref, rout_ref, sall_ref, ssem, rsem, lsem):
        x, y, c, _ = _place()
        sems = (ssem, rsem)
        sib = (x, y, 1 - c)
        me = 4 * x + 2 * y + c
        own = pltpu.make_async_copy(small_ref, sall_ref.at[me], lsem.at[0])
        own.start()
        sent = [_remote(gin_ref.at[:, pl.ds((1 - c) * hi, hi), :], rin_ref, sems, 0, sib),
                _remote(gout_ref.at[:, pl.ds((1 - c) * ho, ho), :], rout_ref, sems, 1, sib)]
        flips = [(fx, fy, fc) for fx in (0, 1) for fy in (0, 1) for fc in (0, 1)][1:]
        for n, (fx, fy, fc) in enumerate(flips):
            peer = (x ^ fx, y ^ fy, c ^ fc)
            sent.append(_remote(small_ref, sall_ref.at[me], sems, 2 + n, peer))
        for cp in sent:
            cp.start()
        _remote(gin_ref.at[:, pl.ds(c * hi, hi), :], rin_ref, sems, 0, sib).wait_recv()
        _remote(gout_ref.at[:, pl.ds(c * ho, ho), :], rout_ref, sems, 1, sib).wait_recv()
        for n, (fx, fy, fc) in enumerate(flips):
            src = 4 * (x ^ fx) + 2 * (y ^ fy) + (c ^ fc)
            _remote(small_ref, sall_ref.at[src], sems, 2 + n, sib).wait_recv()
        for cp in sent:
            cp.wait_send()
        own.wait()

    return pl.pallas_call(
        body, name="pair_exchange", in_specs=[HBM, HBM, HBM], out_specs=[HBM, HBM, HBM],
        out_shape=[jax.ShapeDtypeStruct((4, hi, SHARD_IN), F32), jax.ShapeDtypeStruct((4, ho, D_MODEL), F32),
                   jax.ShapeDtypeStruct((8,) + small.shape, F32)],
        scratch_shapes=[pltpu.SemaphoreType.DMA((9,)), pltpu.SemaphoreType.DMA((9,)), pltpu.SemaphoreType.DMA((1,))],
    )(gin4, gout4, small)


def _chip_exchange(cin4, cout4):
    def body(cin_ref, cout_ref, rin_ref, rout_ref, ssem, rsem, lsem):
        x, y, c, chips = _place()
        sems = (ssem, rsem)
        me = 2 * x + y
        own = [pltpu.make_async_copy(cin_ref.at[me], rin_ref.at[me], lsem.at[0]),
               pltpu.make_async_copy(cout_ref.at[me], rout_ref.at[me], lsem.at[1])]
        for cp in own:
            cp.start()
        sent = []
        for j, (cx, cy) in enumerate(chips):
            k = 2 * cx + cy
            sent.append(_remote(cin_ref.at[k], rin_ref.at[me], sems, 2 * j, (cx, cy, c)))
            sent.append(_remote(cout_ref.at[k], rout_ref.at[me], sems, 2 * j + 1, (cx, cy, c)))
        for cp in sent:
            cp.start()
        for j, (cx, cy) in enumerate(chips):
            k = 2 * cx + cy
            _remote(cin_ref.at[k], rin_ref.at[k], sems, 2 * j, (cx, cy, c)).wait_recv()
            _remote(cout_ref.at[k], rout_ref.at[k], sems, 2 * j + 1, (cx, cy, c)).wait_recv()
        for cp in sent:
            cp.wait_send()
        for cp in own:
            cp.wait()

    return pl.pallas_call(
        body, name="chip_exchange", in_specs=[HBM, HBM], out_specs=[HBM, HBM],
        out_shape=[jax.ShapeDtypeStruct(cin4.shape, F32), jax.ShapeDtypeStruct(cout4.shape, F32)],
        scratch_shapes=[pltpu.SemaphoreType.DMA((6,)), pltpu.SemaphoreType.DMA((6,)), pltpu.SemaphoreType.DMA((2,))],
    )(cin4, cout4)


def _share_halves(hin, hout):
    hi, ho = D_MODEL // 2, SHARD_OUT // 2

    def body(hin_ref, hout_ref, fin_ref, fout_ref, ssem, rsem, lsem):
        x, y, c, _ = _place()
        sems = (ssem, rsem)
        sib = (x, y, 1 - c)
        mine_in, mine_out = pl.ds(c * hi, hi), pl.ds(c * ho, ho)
        own = [pltpu.make_async_copy(hin_ref, fin_ref.at[mine_in], lsem.at[0]),
               pltpu.make_async_copy(hout_ref, fout_ref.at[mine_out], lsem.at[1])]
        for cp in own:
            cp.start()
        sent = [_remote(hin_ref, fin_ref.at[mine_in], sems, 0, sib), _remote(hout_ref, fout_ref.at[mine_out], sems, 1, sib)]
        for cp in sent:
            cp.start()
        _remote(hin_ref, fin_ref.at[pl.ds((1 - c) * hi, hi)], sems, 0, sib).wait_recv()
        _remote(hout_ref, fout_ref.at[pl.ds((1 - c) * ho, ho)], sems, 1, sib).wait_recv()
        for cp in sent:
            cp.wait_send()
        for cp in own:
            cp.wait()

    return pl.pallas_call(
        body, name="share_halves", in_specs=[HBM, HBM], out_specs=[HBM, HBM],
        out_shape=[jax.ShapeDtypeStruct((D_MODEL, SHARD_IN), F32), jax.ShapeDtypeStruct((SHARD_OUT, D_MODEL), F32)],
        scratch_shapes=[pltpu.SemaphoreType.DMA((2,)), pltpu.SemaphoreType.DMA((2,)), pltpu.SemaphoreType.DMA((2,))],
    )(hin, hout)


def _add_half(cidx, full4, recv4, half):
    nblk, _, width = recv4.shape

    def body(c_ref, a_ref, b_ref, o_ref):
        o_ref[...] = a_ref[...] + b_ref[...]

    return pl.pallas_call(
        body, name=f"add_half_{width}",
        grid_spec=pltpu.PrefetchScalarGridSpec(
            num_scalar_prefetch=1, grid=(nblk,),
            in_specs=[pl.BlockSpec((1, half, width), lambda k, c: (k, c[0], 0)),
                      pl.BlockSpec((1, half, width), lambda k, c: (k, 0, 0))],
            out_specs=pl.BlockSpec((1, half, width), lambda k, c: (k, 0, 0))),
        out_shape=jax.ShapeDtypeStruct(recv4.shape, F32),
        compiler_params=_params(1),
    )(cidx, full4, recv4)


def _sum_slots(r4):
    n, rows, width = r4.shape

    def body(r_ref, o_ref):
        acc = r_ref[0]
        for s in range(1, n):
            acc = acc + r_ref[s]
        o_ref[...] = acc

    rb = min(rows, 128)
    return pl.pallas_call(
        body, name=f"sum_slots_{n}_{rows}_{width}", grid=(rows // rb,),
        in_specs=[pl.BlockSpec((n, rb, width), lambda i: (0, i, 0))],
        out_specs=pl.BlockSpec((rb, width), lambda i: (i, 0)),
        out_shape=jax.ShapeDtypeStruct((rows, width), F32),
        compiler_params=_params(1),
    )(r4)


def _adamw(w, g, m, v):
    rows, width = w.shape
    c1 = 1.0 - ADAM_B1 ** ADAM_STEP
    c2 = 1.0 - ADAM_B2 ** ADAM_STEP

    def body(w_ref, g_ref, m_ref, v_ref, d_ref, nm_ref, nv_ref):
        g = g_ref[...]
        nm = ADAM_B1 * m_ref[...] + (1.0 - ADAM_B1) * g
        nv = ADAM_B2 * v_ref[...] + (1.0 - ADAM_B2) * (g * g)
        d_ref[...] = -ADAM_LR * ((nm / c1) / (jnp.sqrt(nv / c2) + ADAM_EPS) + ADAM_WD * w_ref[...])
        nm_ref[...] = nm
        nv_ref[...] = nv

    rb = min(rows, 256)
    spec = pl.BlockSpec((rb, width), lambda i: (i, 0))
    return pl.pallas_call(
        body, name=f"adamw_{rows}_{width}", grid=(rows // rb,),
        in_specs=[spec] * 4, out_specs=[spec] * 3,
        out_shape=[jax.ShapeDtypeStruct((rows, width), F32)] * 3,
        compiler_params=_params(1),
    )(w, g, m, v)


def _rope_tables(positions):
    half = HEAD_DIM // 2
    inv_freq = ROPE_THETA ** (-jnp.arange(half, dtype=F32) * 2.0 / HEAD_DIM)
    ang = positions.astype(F32).reshape(-1, 1) * inv_freq
    cos, sin, zero = jnp.cos(ang), jnp.sin(ang), jnp.zeros_like(ang)
    return (jnp.concatenate([cos] * 4, axis=1), jnp.concatenate([-sin, zero] * 2, axis=1),
            jnp.concatenate([zero, sin] * 2, axis=1))


def _constants():
    idx = jnp.arange(512)
    g512 = jnp.where(idx[:, None] // HEAD_DIM == idx[None, :] // HEAD_DIM, 1.0 / HEAD_DIM, 0.0).astype(BF16)
    j = jnp.arange(256)
    same = j[:, None] // 128 == j[None, :] // 128
    ublk = jnp.where(same & (j[:, None] > j[None, :]), 1.0, 0.0).astype(BF16)
    pblk = jnp.where(same & (j[:, None] < j[None, :]), 1.0, 0.0).astype(BF16)
    return g512, ublk, pblk


def _pad_rows(v, width):
    return jnp.pad(v, ((0, 0), (0, width - v.shape[1])))


def _local_grads(x2, tgt, positions, norm_gain, q_norm_gain, k_norm_gain, sinks, w_bf, wout_bf, nbatch, seq):
    cos, sa, sb = _rope_tables(positions)
    g512, ublk, pblk = _constants()
    qg512 = jnp.tile(q_norm_gain, (1, 8))
    kg128 = jnp.tile(k_norm_gain, (1, 2))
    sink1 = sinks.reshape(8)

    h, qraw, kraw, qrot, k2, v2, ga, qb, kb, vb, gb = _norm_proj(x2, norm_gain, w_bf, qg512, kg128, g512, cos, sa, sb)
    oa = _swa_fwd(sink1, qrot, k2, v2, nbatch, seq)
    ob, ctab = _sb_fwd(qb, kb, vb, ublk, nbatch, seq)
    y, dout, loss_acc = _out_proj(oa, ob, ga, gb, x2, tgt, wout_bf)

    doa, dob, dga, dgb, dwout = _out_proj_bwd(dout, y, oa, ob, ga, gb, wout_bf)
    dqrot, dk2, dv2, dsink = _swa_bwd(sink1, qrot, k2, v2, doa, nbatch, seq)
    dqb, dkb, dvb = _sb_bwd(qb, kb, vb, dob, ctab, ublk, pblk, nbatch, seq)
    dqa, dkv, dqg, dkg = _qk_grad(qraw, kraw, dqrot, dk2, dv2, qg512, kg128, g512, cos, sa, sb)
    pieces = (dqa, dkv, dga, dqb, dkb, dvb, dgb)
    dwin = _w_in_grad(h, pieces)
    gx, dng = _x_grad(x2, dout, pieces, w_bf, norm_gain)
    dqg64 = dqg.reshape(8, HEAD_DIM).sum(axis=0, keepdims=True)
    dkg64 = dkg.reshape(2, HEAD_DIM).sum(axis=0, keepdims=True)
    return loss_acc[0, 0], gx, dwin, dwout, dng, dqg64, dkg64, dsink[:, 0].reshape(1, 8)


def kernel(x, positions, norm_gain, w_in, q_norm_gain, k_norm_gain, sinks, w_out, loss_target, m_norm_gain, m_w_in, m_q_norm_gain, m_k_norm_gain, m_sinks, m_w_out, v_norm_gain, v_w_in, v_q_norm_gain, v_k_norm_gain, v_sinks, v_w_out):
    nbatch, seq, _ = x.shape
    T = nbatch * seq
    x2 = x.reshape(T, D_MODEL)
    tgt = loss_target.reshape(T, D_MODEL)

    ain, aout = _gather_weights(w_in[0].astype(BF16), w_out[0].astype(BF16))
    w_bf = jnp.transpose(ain, (1, 0, 2)).reshape(D_MODEL, IN_WIDTH)
    wout_bf = aout.reshape(D_MODEL, D_MODEL)

    loss_local, gx, dwin, dwout, dng, dqg64, dkg64, dsink8 = _local_grads(
        x2, tgt, positions, norm_gain, q_norm_gain, k_norm_gain, sinks, w_bf, wout_bf, nbatch, seq)
    loss = lax.psum(loss_local, ("x", "y", "c"))

    small = jnp.concatenate([dng, _pad_rows(dqg64, D_MODEL), _pad_rows(dkg64, D_MODEL),
                             _pad_rows(dsink8, D_MODEL), jnp.zeros((4, D_MODEL), F32)], axis=0)
    gin4 = jnp.transpose(dwin.reshape(D_MODEL, 4, SHARD_IN), (1, 0, 2))
    gout4 = dwout.reshape(4, SHARD_OUT, D_MODEL)
    rin, rout, small_all = _pair_exchange(gin4, gout4, small)
    cidx = lax.axis_index("c").astype(jnp.int32).reshape(1)
    cin4 = _add_half(cidx, gin4, rin, D_MODEL // 2)
    cout4 = _add_half(cidx, gout4, rout, SHARD_OUT // 2)
    sin4, sout4 = _chip_exchange(cin4, cout4)
    g_in, g_out = _share_halves(_sum_slots(sin4), _sum_slots(sout4))
    g_small = _sum_slots(small_all)

    d_in, nm_in, nv_in = _adamw(w_in[0], g_in, m_w_in[0], v_w_in[0])
    d_out, nm_out, nv_out = _adamw(w_out[0], g_out, m_w_out[0], v_w_out[0])
    pack = lambda a, b, c_, d: jnp.concatenate(
        [a, _pad_rows(b, D_MODEL), _pad_rows(c_, D_MODEL), _pad_rows(d, D_MODEL), jnp.zeros((4, D_MODEL), F32)], axis=0)
    w_s = pack(norm_gain, q_norm_gain, k_norm_gain, sinks)
    m_s = pack(m_norm_gain, m_q_norm_gain, m_k_norm_gain, m_sinks)
    v_s = pack(v_norm_gain, v_q_norm_gain, v_k_norm_gain, v_sinks)
    d_s, nm_s, nv_s = _adamw(w_s, g_small, m_s, v_s)
    unpack = lambda a: (a[0:1, :], a[1:2, 0:HEAD_DIM], a[2:3, 0:HEAD_DIM], a[3:4, 0:8])

    g_ng, g_qg, g_kg, g_sk = unpack(g_small)
    d_ng, d_qg, d_kg, d_sk = unpack(d_s)
    m_ng, m_qg, m_kg, m_sk = unpack(nm_s)
    v_ng, v_qg, v_kg, v_sk = unpack(nv_s)
    return (loss, gx.reshape(nbatch, seq, D_MODEL),
            g_ng, g_in[None], g_qg, g_kg, g_sk, g_out[None],
            d_ng, d_in[None], d_qg, d_kg, d_sk, d_out[None],
            m_ng, nm_in[None], m_qg, m_kg, m_sk, nm_out[None],
            v_ng, nv_in[None], v_qg, v_kg, v_sk, nv_out[None])
```

```python
import functools
import math

import jax
import jax.numpy as jnp
from jax import lax
from jax.experimental import pallas as pl
from jax.experimental.pallas import tpu as pltpu

F32 = jnp.float32
BF16 = jnp.bfloat16

D_MODEL = 1024
HEAD_DIM = 64
BLOCK = 128
ROPE_THETA = 10000.0
EPS = 1e-6
QA, KA, VA, GA, QB, KB, VB, GB = 0, 512, 640, 768, 1280, 1792, 2304, 2816
IN_WIDTH = 3328
SHARD_IN = IN_WIDTH // 4
SHARD_OUT = D_MODEL // 4
SCALE = 1.0 / math.sqrt(HEAD_DIM)
NEG = -1e30

ADAM_LR, ADAM_B1, ADAM_B2, ADAM_EPS, ADAM_WD, ADAM_STEP = 0.001, 0.9, 0.999, 1e-08, 0.01, 10

TM = 256
VMEM_LIMIT = 56 * 1024 * 1024
MESH = pl.DeviceIdType.MESH


def _dot(a, b):
    return jnp.dot(a, b, preferred_element_type=F32)


def _dot_nt(a, b):
    return lax.dot_general(a, b, (((1,), (1,)), ((), ())), preferred_element_type=F32)


def _dot_tn(a, b):
    return lax.dot_general(a, b, (((0,), (0,)), ((), ())), preferred_element_type=F32)


def _dot_hl(a, m):
    hi = a.astype(BF16)
    lo = (a - hi.astype(F32)).astype(BF16)
    return _dot(hi, m) + _dot(lo, m)


def _params(n_axes=None, vmem=VMEM_LIMIT):
    sem = None if n_axes is None else ("arbitrary",) * n_axes
    return pltpu.CompilerParams(dimension_semantics=sem, vmem_limit_bytes=vmem)


def _const_spec(shape):
    nd = len(shape)
    return pl.BlockSpec(shape, lambda *_: (0,) * nd)


def _rope_fwd(x, cos, sa, sb):
    return x * cos + pltpu.roll(x, 96, 1) * sa + pltpu.roll(x, 32, 1) * sb


def _rope_bwd(d, cos, sa, sb):
    return d * cos - pltpu.roll(d, 96, 1) * sa - pltpu.roll(d, 32, 1) * sb


def _lane_mask(rows, dtype=F32):
    lane = lax.broadcasted_iota(jnp.int32, (rows, 128), 1)
    return jnp.where(lane < HEAD_DIM, 1.0, 0.0).astype(dtype)


def _norm_proj(x2, ng, w_bf, qg512, kg128, g512, cos, sa, sb):
    T = x2.shape[0]

    def body(x_ref, ng_ref, w_ref, qg_ref, kg_ref, g_ref, cos_ref, sa_ref, sb_ref,
             h_ref, qraw_ref, kraw_ref, qrot_ref, k2_ref, v2_ref, ga_ref, qb_ref, kb_ref, vb_ref, gb_ref):
        xb = x_ref[...]
        r = lax.rsqrt(jnp.mean(xb * xb, axis=-1, keepdims=True) + EPS)
        h = (xb * r * ng_ref[...]).astype(BF16)
        h_ref[...] = h
        cosv, sav, sbv = cos_ref[...], sa_ref[...], sb_ref[...]
        m0 = _lane_mask(TM)

        def dup(v):
            v0 = v * m0
            v1 = v - v0
            return v0 + pltpu.roll(v0, 64, 1), v1 + pltpu.roll(v1, 64, 1)

        qa = _dot(h, w_ref[:, QA:KA])
        qraw_ref[...] = qa
        qn = qa * lax.rsqrt(_dot_hl(qa * qa, g_ref[...]) + EPS) * qg_ref[...]
        for s in range(4):
            qrot_ref[:, s * 128:(s + 1) * 128] = _rope_fwd(qn[:, s * 128:(s + 1) * 128], cosv, sav, sbv).astype(BF16)
        ka = _dot(h, w_ref[:, KA:VA])
        kraw_ref[...] = ka
        kn = ka * lax.rsqrt(_dot_hl(ka * ka, g_ref[0:128, 0:128]) + EPS) * kg_ref[...]
        k0, k1 = dup(_rope_fwd(kn, cosv, sav, sbv))
        k2_ref[:, 0:128] = k0.astype(BF16)
        k2_ref[:, 128:256] = k1.astype(BF16)
        v0, v1 = dup(_dot(h, w_ref[:, VA:GA]))
        v2_ref[:, 0:128] = v0.astype(BF16)
        v2_ref[:, 128:256] = v1.astype(BF16)
        ga_ref[...] = _dot(h, w_ref[:, GA:QB])
        qb_ref[...] = _dot(h, w_ref[:, QB:KB]).astype(BF16)
        kb_ref[...] = _dot(h, w_ref[:, KB:VB]).astype(BF16)
        vb_ref[...] = _dot(h, w_ref[:, VB:GB]).astype(BF16)
        gb_ref[...] = _dot(h, w_ref[:, GB:IN_WIDTH])

    def rows(w):
        return pl.BlockSpec((TM, w), lambda i: (i, 0))

    outs = [(D_MODEL, BF16), (512, F32), (128, F32), (512, BF16), (256, BF16), (256, BF16), (512, F32),
            (512, BF16), (512, BF16), (512, BF16), (512, F32)]
    return pl.pallas_call(
        body, name="norm_proj", grid=(T // TM,),
        in_specs=[rows(D_MODEL), _const_spec((1, D_MODEL)), _const_spec((D_MODEL, IN_WIDTH)), _const_spec((1, 512)),
                  _const_spec((1, 128)), _const_spec((512, 512)), rows(128), rows(128), rows(128)],
        out_specs=[rows(w) for w, _ in outs],
        out_shape=[jax.ShapeDtypeStruct((T, w), dt) for w, dt in outs],
        compiler_params=_params(1),
    )(x2, ng, w_bf, qg512, kg128, g512, cos, sa, sb)


def _swa_scores(q, kst, mask, sink_ref, kv, pr):
    s_all = _dot_nt(q, kst) * SCALE
    probs, stats = [], []
    for hh in range(2):
        sink = sink_ref[kv * 4 + pr * 2 + hh]
        s = jnp.where(mask, s_all[:, hh * 256:(hh + 1) * 256], NEG)
        m = jnp.maximum(jnp.max(s, axis=1, keepdims=True), sink)
        pe = jnp.exp(s - m)
        inv = 1.0 / (jnp.sum(pe, axis=1, keepdims=True) + jnp.exp(sink - m))
        probs.append(pe * inv)
        stats.append(jnp.exp(sink - m) * inv)
    return probs, stats


def _swa_mask(i):
    r = lax.broadcasted_iota(jnp.int32, (128, 256), 0)
    c = lax.broadcasted_iota(jnp.int32, (128, 256), 1)
    return (c > r) & (c <= r + 128) & ((c >= 128) | (i > 0))


def _stack_pair(prev_ref, cur_ref, m0b):
    kc = jnp.concatenate([prev_ref[...], cur_ref[...]], axis=0)
    k0 = kc * m0b
    return jnp.concatenate([k0, kc - k0], axis=0)


def _swa_fwd(sinks, qrot, k2, v2, nbatch, seq):
    nb = seq // BLOCK
    T = nbatch * seq

    def body(sink_ref, q_ref, kp_ref, kc_ref, vp_ref, vc_ref, o_ref):
        i, kv = pl.program_id(1), pl.program_id(2)
        m0b = _lane_mask(256, BF16)
        mask = _swa_mask(i)
        kst = _stack_pair(kp_ref, kc_ref, m0b)
        vst = _stack_pair(vp_ref, vc_ref, m0b)
        for pr in range(2):
            probs, _ = _swa_scores(q_ref[:, pr * 128:(pr + 1) * 128], kst, mask, sink_ref, kv, pr)
            o_ref[:, pr * 128:(pr + 1) * 128] = _dot(jnp.concatenate(probs, axis=1).astype(BF16), vst)

    cur = lambda b, i, kv: (b * nb + i, kv)
    prev = lambda b, i, kv: (b * nb + jnp.maximum(i - 1, 0), kv)
    return pl.pallas_call(
        body, name="swa_fwd", grid=(nbatch, nb, 2),
        in_specs=[pl.BlockSpec(memory_space=pltpu.SMEM), pl.BlockSpec((128, 256), cur),
                  pl.BlockSpec((128, 128), prev), pl.BlockSpec((128, 128), cur),
                  pl.BlockSpec((128, 128), prev), pl.BlockSpec((128, 128), cur)],
        out_specs=pl.BlockSpec((128, 256), cur),
        out_shape=jax.ShapeDtypeStruct((T, 512), F32),
        compiler_params=_params(3),
    )(sinks, qrot, k2, k2, v2, v2)


def _swa_bwd(sinks, qrot, k2, v2, doa, nbatch, seq):
    nb = seq // BLOCK
    T = nbatch * seq

    def body(sink_ref, q_ref, kp_ref, kc_ref, vp_ref, vc_ref, do_ref, dq_ref, dk_ref, dv_ref, ds_ref, dkc, dvc):
        b, kv, i = pl.program_id(0), pl.program_id(1), pl.program_id(2)

        @pl.when((b == 0) & (kv == 0) & (i == 0))
        def _():
            ds_ref[...] = jnp.zeros_like(ds_ref)

        @pl.when(i == 0)
        def _():
            dkc[...] = jnp.zeros_like(dkc)
            dvc[...] = jnp.zeros_like(dvc)

        @pl.when(i < nb)
        def _():
            m0b = _lane_mask(256, BF16)
            m0 = _lane_mask(128) > 0.5
            mask = _swa_mask(i)
            kst = _stack_pair(kp_ref, kc_ref, m0b)
            vst = _stack_pair(vp_ref, vc_ref, m0b)
            dkst = jnp.zeros((512, 128), F32)
            dvst = jnp.zeros((512, 128), F32)
            for pr in range(2):
                q = q_ref[:, pr * 128:(pr + 1) * 128]
                do = do_ref[:, pr * 128:(pr + 1) * 128]
                probs, psink = _swa_scores(q, kst, mask, sink_ref, kv, pr)
                dp_all = _dot_nt(do, vst)
                ds_parts = []
                for hh in range(2):
                    dp = dp_all[:, hh * 256:(hh + 1) * 256]
                    delta = jnp.sum(probs[hh] * dp, axis=1, keepdims=True)
                    ds_parts.append(probs[hh] * (dp - delta))
                    row = pl.ds(kv * 4 + pr * 2 + hh, 1)
                    ds_ref[row, :] = ds_ref[row, :] - jnp.sum(psink[hh] * delta)
                ds_all = jnp.concatenate(ds_parts, axis=1).astype(BF16)
                p_all = jnp.concatenate(probs, axis=1).astype(BF16)
                dq_ref[:, pr * 128:(pr + 1) * 128] = _dot(ds_all, kst) * SCALE
                dkst = dkst + _dot_tn(ds_all, q) * SCALE
                dvst = dvst + _dot_tn(p_all, do)
            dk_ref[...] = dkc[...] + jnp.where(m0, dkst[0:128], dkst[256:384])
            dv_ref[...] = dvc[...] + jnp.where(m0, dvst[0:128], dvst[256:384])
            dkc[...] = jnp.where(m0, dkst[128:256], dkst[384:512])
            dvc[...] = jnp.where(m0, dvst[128:256], dvst[384:512])

        @pl.when(i == nb)
        def _():
            dk_ref[...] = dkc[...]
            dv_ref[...] = dvc[...]

    cur = lambda b, kv, i: (b * nb + jnp.minimum(i, nb - 1), kv)
    prev = lambda b, kv, i: (b * nb + jnp.maximum(jnp.minimum(i, nb - 1) - 1, 0), kv)
    done = lambda b, kv, i: (b * nb + jnp.maximum(i - 1, 0), kv)
    return pl.pallas_call(
        body, name="swa_bwd", grid=(nbatch, 2, nb + 1),
        in_specs=[pl.BlockSpec(memory_space=pltpu.SMEM), pl.BlockSpec((128, 256), cur),
                  pl.BlockSpec((128, 128), prev), pl.BlockSpec((128, 128), cur),
                  pl.BlockSpec((128, 128), prev), pl.BlockSpec((128, 128), cur),
                  pl.BlockSpec((128, 256), cur)],
        out_specs=[pl.BlockSpec((128, 256), cur), pl.BlockSpec((128, 128), done), pl.BlockSpec((128, 128), done),
                   _const_spec((8, 128))],
        out_shape=[jax.ShapeDtypeStruct((T, 512), F32), jax.ShapeDtypeStruct((T, 256), F32),
                   jax.ShapeDtypeStruct((T, 256), F32), jax.ShapeDtypeStruct((8, 128), F32)],
        scratch_shapes=[pltpu.VMEM((128, 128), F32), pltpu.VMEM((128, 128), F32)],
        compiler_params=_params(3),
    )(sinks, qrot, k2, k2, v2, v2, doa)


SB_T = 256


def _sb_masks():
    r = lax.broadcasted_iota(jnp.int32, (SB_T, 2 * SB_T), 0)
    c = lax.broadcasted_iota(jnp.int32, (SB_T, 2 * SB_T), 1)
    return (c & (SB_T - 1)) < r


def _sb_logits(q, kst):
    z = _dot_nt(q, kst) * SCALE
    t = jnp.exp(-jnp.abs(z))
    sp = jnp.maximum(z, 0.0) + jnp.log(1.0 + t)
    return z, t, -sp, z - sp


def _pair_rows(ref, j, m0b):
    kj = ref[pl.ds(pl.multiple_of(j * SB_T, SB_T), SB_T), :]
    k0 = kj * m0b
    return jnp.concatenate([k0, kj - k0], axis=0)


def _bcast2(c0, c1):
    return jnp.concatenate([jnp.broadcast_to(c0, (SB_T, SB_T)), jnp.broadcast_to(c1, (SB_T, SB_T))], axis=1)


def _rowsum2(x):
    return jnp.sum(x[:, 0:SB_T], axis=1, keepdims=True), jnp.sum(x[:, SB_T:2 * SB_T], axis=1, keepdims=True)


def _scan2(x, tri):
    return jnp.concatenate([_dot_hl(x[:, 0:SB_T], tri), _dot_hl(x[:, SB_T:2 * SB_T], tri)], axis=1)


def _sb_fwd(qb, kb, vb, ublk, nbatch, seq):
    nb = seq // SB_T
    T = nbatch * seq

    def body(q_ref, k_ref, v_ref, u_ref, o_ref, ctab_ref):
        i = pl.program_id(2)
        q = q_ref[...]
        m0b = _lane_mask(SB_T, BF16)
        u = u_ref[...]
        lane = lax.broadcasted_iota(jnp.int32, (SB_T, 128), 1)
        ctab_ref[...] = jnp.zeros_like(ctab_ref)

        def tile(j, carry, diag):
            c0, c1, acc = carry
            if not diag:
                ctab_ref[...] = jnp.where(lane == j, c0, jnp.where(lane == nb + j, c1, ctab_ref[...]))
            kst = _pair_rows(k_ref, j, m0b)
            vst = _pair_rows(v_ref, j, m0b)
            z, _, lb, la = _sb_logits(q, kst)
            if diag:
                mask = _sb_masks()
                lb = jnp.where(mask, lb, 0.0)
            w = jnp.exp(la + _scan2(lb, u) + _bcast2(c0, c1))
            if diag:
                w = jnp.where(mask, w, 0.0)
            r0, r1 = _rowsum2(lb)
            return c0 + r0, c1 + r1, acc + _dot(w.astype(BF16), vst)

        zc = jnp.zeros((SB_T, 1), F32)
        carry = tile(i, (zc, zc, jnp.zeros((SB_T, 128), F32)), True)
        carry = lax.fori_loop(0, i, lambda jj, cr: tile(i - 1 - jj, cr, False), carry)
        o_ref[...] = carry[2]

    blk = lambda b, p, i: (b * nb + i, p)
    full = lambda b, p, i: (b, p)
    tab = lambda b, p, i: ((b * 4 + p) * nb + i, 0)
    return pl.pallas_call(
        body, name="sb_fwd", grid=(nbatch, 4, nb),
        in_specs=[pl.BlockSpec((SB_T, 128), blk), pl.BlockSpec((seq, 128), full), pl.BlockSpec((seq, 128), full),
                  _const_spec((SB_T, SB_T))],
        out_specs=[pl.BlockSpec((SB_T, 128), blk), pl.BlockSpec((SB_T, 128), tab)],
        out_shape=[jax.ShapeDtypeStruct((T, 512), F32), jax.ShapeDtypeStruct((nbatch * 4 * nb * SB_T, 128), F32)],
        compiler_params=_params(3),
    )(qb, kb, vb, ublk)


def _sb_bwd(qb, kb, vb, dob, ctab, ublk, pblk, nbatch, seq):
    nb = seq // SB_T
    T = nbatch * seq

    def body(q_ref, k_ref, v_ref, do_ref, ctab_ref, u_ref, up_ref, dq_ref, dk_ref, dv_ref):
        i = pl.program_id(2)

        @pl.when(i == 0)
        def _():
            dk_ref[...] = jnp.zeros_like(dk_ref)
            dv_ref[...] = jnp.zeros_like(dv_ref)

        q = q_ref[...]
        do = do_ref[...]
        m0b = _lane_mask(SB_T, BF16)
        m0 = _lane_mask(SB_T) > 0.5
        u, up = u_ref[...], up_ref[...]
        lane = lax.broadcasted_iota(jnp.int32, (SB_T, 128), 1)

        def tile(j, carry, diag):
            s0, s1, dq = carry
            kst = _pair_rows(k_ref, j, m0b)
            vst = _pair_rows(v_ref, j, m0b)
            z, t, lb, la = _sb_logits(q, kst)
            if diag:
                mask = _sb_masks()
                lb = jnp.where(mask, lb, 0.0)
                w = jnp.where(mask, jnp.exp(la + _scan2(lb, u)), 0.0)
            else:
                ct = ctab_ref[...]
                c0 = jnp.sum(jnp.where(lane == j, ct, 0.0), axis=1, keepdims=True)
                c1 = jnp.sum(jnp.where(lane == nb + j, ct, 0.0), axis=1, keepdims=True)
                w = jnp.exp(la + _scan2(lb, u) + _bcast2(c0, c1))
            e = _dot_nt(do, vst) * w
            dlb = _bcast2(s0, s1) + _scan2(e, up)
            inv = 1.0 / (1.0 + t)
            pos = z >= 0.0
            dz = e * (jnp.where(pos, t, 1.0) * inv) - dlb * (jnp.where(pos, 1.0, t) * inv)
            if diag:
                dz = jnp.where(mask, dz, 0.0)
            dzb = (dz * SCALE).astype(BF16)
            dkst = _dot_tn(dzb, q)
            dvst = _dot_tn(w.astype(BF16), do)
            rows = pl.ds(pl.multiple_of(j * SB_T, SB_T), SB_T)
            dk_ref[rows, :] = dk_ref[rows, :] + jnp.where(m0, dkst[0:SB_T], dkst[SB_T:2 * SB_T])
            dv_ref[rows, :] = dv_ref[rows, :] + jnp.where(m0, dvst[0:SB_T], dvst[SB_T:2 * SB_T])
            x0, x1 = _rowsum2(e)
            return s0 + x0, s1 + x1, dq + _dot(dzb, kst)

        zc = jnp.zeros((SB_T, 1), F32)
        carry = lax.fori_loop(0, i, lambda j, cr: tile(j, cr, False), (zc, zc, jnp.zeros((SB_T, 128), F32)))
        dq_ref[...] = tile(i, carry, True)[2]

    blk = lambda b, p, i: (b * nb + i, p)
    full = lambda b, p, i: (b, p)
    tab = lambda b, p, i: ((b * 4 + p) * nb + i, 0)
    return pl.pallas_call(
        body, name="sb_bwd", grid=(nbatch, 4, nb),
        in_specs=[pl.BlockSpec((SB_T, 128), blk), pl.BlockSpec((seq, 128), full), pl.BlockSpec((seq, 128), full),
                  pl.BlockSpec((SB_T, 128), blk), pl.BlockSpec((SB_T, 128), tab),
                  _const_spec((SB_T, SB_T)), _const_spec((SB_T, SB_T))],
        out_specs=[pl.BlockSpec((SB_T, 128), blk), pl.BlockSpec((seq, 128), full), pl.BlockSpec((seq, 128), full)],
        out_shape=[jax.ShapeDtypeStruct((T, 512), F32)] * 3,
        compiler_params=_params(3),
    )(qb, kb, vb, dob, ctab, ublk, pblk)


def _sigmoid(g):
    return 1.0 / (1.0 + jnp.exp(-g))


def _out_proj(oa, ob, ga, gb, x2, tgt, wout_bf):
    T = x2.shape[0]

    def body(oa_ref, ob_ref, ga_ref, gb_ref, x_ref, t_ref, w_ref, y_ref, dout_ref, loss_ref):
        @pl.when(pl.program_id(0) == 0)
        def _():
            loss_ref[...] = jnp.zeros_like(loss_ref)

        ga, gb = ga_ref[...], gb_ref[...]
        ya = (oa_ref[...] * (ga * _sigmoid(ga))).astype(BF16)
        yb = (ob_ref[...] * (gb * _sigmoid(gb))).astype(BF16)
        y_ref[:, 0:512] = ya
        y_ref[:, 512:1024] = yb
        out = x_ref[...] + _dot(ya, w_ref[0:512, :]) + _dot(yb, w_ref[512:1024, :])
        diff = out - t_ref[...]
        dout_ref[...] = diff * (1.0 / D_MODEL)
        loss_ref[...] = loss_ref[...] + jnp.sum(diff * diff) * (0.5 / D_MODEL)

    rows = lambda w: pl.BlockSpec((TM, w), lambda i: (i, 0))
    return pl.pallas_call(
        body, name="out_proj", grid=(T // TM,),
        in_specs=[rows(512), rows(512), rows(512), rows(512), rows(D_MODEL), rows(D_MODEL),
                  _const_spec((D_MODEL, D_MODEL))],
        out_specs=[rows(D_MODEL), rows(D_MODEL), _const_spec((8, 128))],
        out_shape=[jax.ShapeDtypeStruct((T, D_MODEL), BF16), jax.ShapeDtypeStruct((T, D_MODEL), F32),
                   jax.ShapeDtypeStruct((8, 128), F32)],
        compiler_params=_params(1),
    )(oa, ob, ga, gb, x2, tgt, wout_bf)


def _out_proj_bwd(dout, y, oa, ob, ga, gb, wout_bf):
    T = dout.shape[0]

    def body(dout_ref, y_ref, oa_ref, ob_ref, ga_ref, gb_ref, w_ref, doa_ref, dob_ref, dga_ref, dgb_ref, dw_ref):
        @pl.when(pl.program_id(0) == 0)
        def _():
            dw_ref[...] = jnp.zeros_like(dw_ref)

        db = dout_ref[...].astype(BF16)
        dw_ref[...] = dw_ref[...] + _dot_tn(y_ref[...], db)
        for o_ref, g_ref, do_ref, dg_ref, lo in ((oa_ref, ga_ref, doa_ref, dga_ref, 0), (ob_ref, gb_ref, dob_ref, dgb_ref, 512)):
            dy = _dot_nt(db, w_ref[lo:lo + 512, :])
            g = g_ref[...]
            sg = _sigmoid(g)
            do_ref[...] = (dy * (g * sg)).astype(BF16)
            dg_ref[...] = (dy * o_ref[...] * (sg * (1.0 + g * (1.0 - sg)))).astype(BF16)

    rows = lambda w: pl.BlockSpec((TM, w), lambda i: (i, 0))
    return pl.pallas_call(
        body, name="out_proj_bwd", grid=(T // TM,),
        in_specs=[rows(D_MODEL), rows(D_MODEL), rows(512), rows(512), rows(512), rows(512),
                  _const_spec((D_MODEL, D_MODEL))],
        out_specs=[rows(512)] * 4 + [_const_spec((D_MODEL, D_MODEL))],
        out_shape=[jax.ShapeDtypeStruct((T, 512), BF16)] * 4 + [jax.ShapeDtypeStruct((D_MODEL, D_MODEL), F32)],
        compiler_params=_params(1),
    )(dout, y, oa, ob, ga, gb, wout_bf)


def _qk_grad(qraw, kraw, dqrot, dk2, dv2, qg512, kg128, g512, cos, sa, sb):
    T = qraw.shape[0]

    def body(qraw_ref, kraw_ref, dqrot_ref, dk2_ref, dv2_ref, qg_ref, kg_ref, g_ref, cos_ref, sa_ref, sb_ref,
             dqa_ref, dkv_ref, dqg_ref, dkg_ref):
        @pl.when(pl.program_id(0) == 0)
        def _():
            dqg_ref[...] = jnp.zeros_like(dqg_ref)
            dkg_ref[...] = jnp.zeros_like(dkg_ref)

        cosv, sav, sbv = cos_ref[...], sa_ref[...], sb_ref[...]
        m0 = _lane_mask(TM) > 0.5

        def head_norm_bwd(raw, dn_rot, gmat, gain):
            r = lax.rsqrt(_dot_hl(raw * raw, gmat) + EPS)
            n = raw * r
            dn = dn_rot * gain
            return r * (dn - n * _dot_hl(dn * n, gmat)), jnp.sum(dn_rot * n, axis=0, keepdims=True)

        def fold(ref):
            a, b = ref[:, 0:128], ref[:, 128:256]
            return jnp.where(m0, a + pltpu.roll(a, 64, 1), b + pltpu.roll(b, 64, 1))

        dqn = jnp.concatenate([_rope_bwd(dqrot_ref[:, s * 128:(s + 1) * 128], cosv, sav, sbv) for s in range(4)], axis=1)
        dqa, dqg = head_norm_bwd(qraw_ref[...], dqn, g_ref[...], qg_ref[...])
        dka, dkg = head_norm_bwd(kraw_ref[...], _rope_bwd(fold(dk2_ref), cosv, sav, sbv), g_ref[0:128, 0:128], kg_ref[...])
        dqa_ref[...] = dqa.astype(BF16)
        dkv_ref[:, 0:128] = dka.astype(BF16)
        dkv_ref[:, 128:256] = fold(dv2_ref).astype(BF16)
        dqg_ref[...] = dqg_ref[...] + dqg
        dkg_ref[...] = dkg_ref[...] + dkg

    rows = lambda w: pl.BlockSpec((TM, w), lambda i: (i, 0))
    return pl.pallas_call(
        body, name="qk_grad", grid=(T // TM,),
        in_specs=[rows(512), rows(128), rows(512), rows(256), rows(256), _const_spec((1, 512)), _const_spec((1, 128)),
                  _const_spec((512, 512)), rows(128), rows(128), rows(128)],
        out_specs=[rows(512), rows(256), _const_spec((1, 512)), _const_spec((1, 128))],
        out_shape=[jax.ShapeDtypeStruct((T, 512), BF16), jax.ShapeDtypeStruct((T, 256), BF16),
                   jax.ShapeDtypeStruct((1, 512), F32), jax.ShapeDtypeStruct((1, 128), F32)],
        compiler_params=_params(1),
    )(qraw, kraw, dqrot, dk2, dv2, qg512, kg128, g512, cos, sa, sb)


_PIECES = ((QA, 512), (KA, 256), (GA, 512), (QB, 512), (KB, 512), (VB, 512), (GB, 512))


def _w_in_grad(h, pieces):
    T = h.shape[0]

    def body(h_ref, *refs):
        dw_ref = refs[-1]

        @pl.when(pl.program_id(0) == 0)
        def _():
            dw_ref[...] = jnp.zeros_like(dw_ref)

        hb = h_ref[...]
        for (lo, width), p_ref in zip(_PIECES, refs[:-1]):
            dw_ref[:, lo:lo + width] = dw_ref[:, lo:lo + width] + _dot_tn(hb, p_ref[...].astype(BF16))

    rows = lambda w: pl.BlockSpec((TM, w), lambda i: (i, 0))
    return pl.pallas_call(
        body, name="w_in_grad", grid=(T // TM,),
        in_specs=[rows(D_MODEL)] + [rows(w) for _, w in _PIECES],
        out_specs=_const_spec((D_MODEL, IN_WIDTH)),
        out_shape=jax.ShapeDtypeStruct((D_MODEL, IN_WIDTH), F32),
        compiler_params=_params(1),
    )(h, *pieces)


def _x_grad(x2, dout, pieces, w_bf, ng):
    T = x2.shape[0]
    npc = len(_PIECES)

    def body(x_ref, dout_ref, *refs):
        w_ref, ng_ref, gx_ref, dng_ref = refs[npc:]

        @pl.when(pl.program_id(0) == 0)
        def _():
            dng_ref[...] = jnp.zeros_like(dng_ref)

        dh = jnp.zeros((TM, D_MODEL), F32)
        for (lo, width), p_ref in zip(_PIECES, refs[:npc]):
            dh = dh + _dot_nt(p_ref[...].astype(BF16), w_ref[:, lo:lo + width])
        xb = x_ref[...]
        r = lax.rsqrt(jnp.mean(xb * xb, axis=-1, keepdims=True) + EPS)
        n = xb * r
        dn = dh * ng_ref[...]
        gx_ref[...] = dout_ref[...] + r * (dn - n * jnp.mean(dn * n, axis=-1, keepdims=True))
        dng_ref[...] = dng_ref[...] + jnp.sum(dh * n, axis=0, keepdims=True)

    rows = lambda w: pl.BlockSpec((TM, w), lambda i: (i, 0))
    return pl.pallas_call(
        body, name="x_grad", grid=(T // TM,),
        in_specs=[rows(D_MODEL), rows(D_MODEL)] + [rows(w) for _, w in _PIECES]
        + [_const_spec((D_MODEL, IN_WIDTH)), _const_spec((1, D_MODEL))],
        out_specs=[rows(D_MODEL), _const_spec((1, D_MODEL))],
        out_shape=[jax.ShapeDtypeStruct((T, D_MODEL), F32), jax.ShapeDtypeStruct((1, D_MODEL), F32)],
        compiler_params=_params(1),
    )(x2, dout, *pieces, w_bf, ng)


HBM = pl.BlockSpec(memory_space=pl.ANY)


def _place():
    x, y, c = lax.axis_index("x"), lax.axis_index("y"), lax.axis_index("c")
    chips = [(1 - x, y), (x, 1 - y), (1 - x, 1 - y)]
    return x, y, c, chips


def _remote(src, dst, sems, k, to):
    return pltpu.make_async_remote_copy(src_ref=src, dst_ref=dst, send_sem=sems[0].at[k], recv_sem=sems[1].at[k],
                                        device_id=to, device_id_type=MESH)


def _gather_weights(win_b, wout_b):
    hi, ho = D_MODEL // 2, SHARD_OUT // 2

    def body(win_ref, wout_ref, ain_ref, aout_ref, ssem, rsem, lsem):
        x, y, c, chips = _place()
        me, sib = 2 * x + y, (x, y, 1 - c)
        sems = (ssem, rsem)
        rin, rout = pl.ds(c * hi, hi), pl.ds(c * ho, ho)
        oin, oout = pl.ds((1 - c) * hi, hi), pl.ds((1 - c) * ho, ho)
        own = [pltpu.make_async_copy(win_ref, ain_ref.at[me], lsem.at[0]),
               pltpu.make_async_copy(wout_ref, aout_ref.at[me], lsem.at[1])]
        for cp in own:
            cp.start()
        sent = []
        for j, (cx, cy) in enumerate(chips):
            sent.append(_remote(win_ref.at[rin], ain_ref.at[me, rin], sems, 2 * j, (cx, cy, c)))
            sent.append(_remote(wout_ref.at[rout], aout_ref.at[me, rout], sems, 2 * j + 1, (cx, cy, c)))
        for cp in sent:
            cp.start()
        for j, (cx, cy) in enumerate(chips):
            k = 2 * cx + cy
            _remote(win_ref.at[rin], ain_ref.at[k, rin], sems, 2 * j, sib).wait_recv()
            fwd = _remote(ain_ref.at[k, rin], ain_ref.at[k, rin], sems, 6 + 2 * j, sib)
            fwd.start()
            sent.append(fwd)
            _remote(wout_ref.at[rout], aout_ref.at[k, rout], sems, 2 * j + 1, sib).wait_recv()
            fwd = _remote(aout_ref.at[k, rout], aout_ref.at[k, rout], sems, 7 + 2 * j, sib)
            fwd.start()
            sent.append(fwd)
        for j, (cx, cy) in enumerate(chips):
            k = 2 * cx + cy
            _remote(win_ref.at[oin], ain_ref.at[k, oin], sems, 6 + 2 * j, sib).wait_recv()
            _remote(wout_ref.at[oout], aout_ref.at[k, oout], sems, 7 + 2 * j, sib).wait_recv()
        for cp in sent:
            cp.wait_send()
        for cp in own:
            cp.wait()

    return pl.pallas_call(
        body, name="gather_weights", in_specs=[HBM, HBM], out_specs=[HBM, HBM],
        out_shape=[jax.ShapeDtypeStruct((4, D_MODEL, SHARD_IN), BF16), jax.ShapeDtypeStruct((4, SHARD_OUT, D_MODEL), BF16)],
        scratch_shapes=[pltpu.SemaphoreType.DMA((12,)), pltpu.SemaphoreType.DMA((12,)), pltpu.SemaphoreType.DMA((2,))],
    )(win_b, wout_b)


def _pair_exchange(gin4, gout4, small):
    hi, ho = D_MODEL // 2, SHARD_OUT // 2

    def body(gin_ref, gout_ref, small_ref, rin_ref, rout_ref, sall_ref, ssem, rsem, lsem):
        x, y, c, _ = _place()
        sems = (ssem, rsem)
        sib = (x, y, 1 - c)
        me = 4 * x + 2 * y + c
        own = pltpu.make_async_copy(small_ref, sall_ref.at[me], lsem.at[0])
        own.start()
        sent = [_remote(gin_ref.at[:, pl.ds((1 - c) * hi, hi), :], rin_ref, sems, 0, sib),
                _remote(gout_ref.at[:, pl.ds((1 - c) * ho, ho), :], rout_ref, sems, 1, sib)]
        flips = [(fx, fy, fc) for fx in (0, 1) for fy in (0, 1) for fc in (0, 1)][1:]
        for n, (fx, fy, fc) in enumerate(flips):
            peer = (x ^ fx, y ^ fy, c ^ fc)
            sent.append(_remote(small_ref, sall_ref.at[me], sems, 2 + n, peer))
        for cp in sent:
            cp.start()
        _remote(gin_ref.at[:, pl.ds(c * hi, hi), :], rin_ref, sems, 0, sib).wait_recv()
        _remote(gout_ref.at[:, pl.ds(c * ho, ho), :], rout_ref, sems, 1, sib).wait_recv()
        for n, (fx, fy, fc) in enumerate(flips):
            src = 4 * (x ^ fx) + 2 * (y ^ fy) + (c ^ fc)
            _remote(small_ref, sall_ref.at[src], sems, 2 + n, sib).wait_recv()
        for cp in sent:
            cp.wait_send()
        own.wait()

    return pl.pallas_call(
        body, name="pair_exchange", in_specs=[HBM, HBM, HBM], out_specs=[HBM, HBM, HBM],
        out_shape=[jax.ShapeDtypeStruct((4, hi, SHARD_IN), F32), jax.ShapeDtypeStruct((4, ho, D_MODEL), F32),
                   jax.ShapeDtypeStruct((8,) + small.shape, F32)],
        scratch_shapes=[pltpu.SemaphoreType.DMA((9,)), pltpu.SemaphoreType.DMA((9,)), pltpu.SemaphoreType.DMA((1,))],
    )(gin4, gout4, small)


def _chip_exchange(cin4, cout4):
    def body(cin_ref, cout_ref, rin_ref, rout_ref, ssem, rsem, lsem):
        x, y, c, chips = _place()
        sems = (ssem, rsem)
        me = 2 * x + y
        own = [pltpu.make_async_copy(cin_ref.at[me], rin_ref.at[me], lsem.at[0]),
               pltpu.make_async_copy(cout_ref.at[me], rout_ref.at[me], lsem.at[1])]
        for cp in own:
            cp.start()
        sent = []
        for j, (cx, cy) in enumerate(chips):
            k = 2 * cx + cy
            sent.append(_remote(cin_ref.at[k], rin_ref.at[me], sems, 2 * j, (cx, cy, c)))
            sent.append(_remote(cout_ref.at[k], rout_ref.at[me], sems, 2 * j + 1, (cx, cy, c)))
        for cp in sent:
            cp.start()
        for j, (cx, cy) in enumerate(chips):
            k = 2 * cx + cy
            _remote(cin_ref.at[k], rin_ref.at[k], sems, 2 * j, (cx, cy, c)).wait_recv()
            _remote(cout_ref.at[k], rout_ref.at[k], sems, 2 * j + 1, (cx, cy, c)).wait_recv()
        for cp in sent:
            cp.wait_send()
        for cp in own:
            cp.wait()

    return pl.pallas_call(
        body, name="chip_exchange", in_specs=[HBM, HBM], out_specs=[HBM, HBM],
        out_shape=[jax.ShapeDtypeStruct(cin4.shape, F32), jax.ShapeDtypeStruct(cout4.shape, F32)],
        scratch_shapes=[pltpu.SemaphoreType.DMA((6,)), pltpu.SemaphoreType.DMA((6,)), pltpu.SemaphoreType.DMA((2,))],
    )(cin4, cout4)


def _share_halves(hin, hout):
    hi, ho = D_MODEL // 2, SHARD_OUT // 2

    def body(hin_ref, hout_ref, fin_ref, fout_ref, ssem, rsem, lsem):
        x, y, c, _ = _place()
        sems = (ssem, rsem)
        sib = (x, y, 1 - c)
        mine_in, mine_out = pl.ds(c * hi, hi), pl.ds(c * ho, ho)
        own = [pltpu.make_async_copy(hin_ref, fin_ref.at[mine_in], lsem.at[0]),
               pltpu.make_async_copy(hout_ref, fout_ref.at[mine_out], lsem.at[1])]
        for cp in own:
            cp.start()
        sent = [_remote(hin_ref, fin_ref.at[mine_in], sems, 0, sib), _remote(hout_ref, fout_ref.at[mine_out], sems, 1, sib)]
        for cp in sent:
            cp.start()
        _remote(hin_ref, fin_ref.at[pl.ds((1 - c) * hi, hi)], sems, 0, sib).wait_recv()
        _remote(hout_ref, fout_ref.at[pl.ds((1 - c) * ho, ho)], sems, 1, sib).wait_recv()
        for cp in sent:
            cp.wait_send()
        for cp in own:
            cp.wait()

    return pl.pallas_call(
        body, name="share_halves", in_specs=[HBM, HBM], out_specs=[HBM, HBM],
        out_shape=[jax.ShapeDtypeStruct((D_MODEL, SHARD_IN), F32), jax.ShapeDtypeStruct((SHARD_OUT, D_MODEL), F32)],
        scratch_shapes=[pltpu.SemaphoreType.DMA((2,)), pltpu.SemaphoreType.DMA((2,)), pltpu.SemaphoreType.DMA((2,))],
    )(hin, hout)


def _add_half(cidx, full4, recv4, half):
    nblk, _, width = recv4.shape

    def body(c_ref, a_ref, b_ref, o_ref):
        o_ref[...] = a_ref[...] + b_ref[...]

    return pl.pallas_call(
        body, name=f"add_half_{width}",
        grid_spec=pltpu.PrefetchScalarGridSpec(
            num_scalar_prefetch=1, grid=(nblk,),
            in_specs=[pl.BlockSpec((1, half, width), lambda k, c: (k, c[0], 0)),
                      pl.BlockSpec((1, half, width), lambda k, c: (k, 0, 0))],
            out_specs=pl.BlockSpec((1, half, width), lambda k, c: (k, 0, 0))),
        out_shape=jax.ShapeDtypeStruct(recv4.shape, F32),
        compiler_params=_params(1),
    )(cidx, full4, recv4)


def _sum_slots(r4):
    n, rows, width = r4.shape

    def body(r_ref, o_ref):
        acc = r_ref[0]
        for s in range(1, n):
            acc = acc + r_ref[s]
        o_ref[...] = acc

    rb = min(rows, 128)
    return pl.pallas_call(
        body, name=f"sum_slots_{n}_{rows}_{width}", grid=(rows // rb,),
        in_specs=[pl.BlockSpec((n, rb, width), lambda i: (0, i, 0))],
        out_specs=pl.BlockSpec((rb, width), lambda i: (i, 0)),
        out_shape=jax.ShapeDtypeStruct((rows, width), F32),
        compiler_params=_params(1),
    )(r4)


def _adamw(w, g, m, v):
    rows, width = w.shape
    c1 = 1.0 - ADAM_B1 ** ADAM_STEP
    c2 = 1.0 - ADAM_B2 ** ADAM_STEP

    def body(w_ref, g_ref, m_ref, v_ref, d_ref, nm_ref, nv_ref):
        g = g_ref[...]
        nm = ADAM_B1 * m_ref[...] + (1.0 - ADAM_B1) * g
        nv = ADAM_B2 * v_ref[...] + (1.0 - ADAM_B2) * (g * g)
        d_ref[...] = -ADAM_LR * ((nm / c1) / (jnp.sqrt(nv / c2) + ADAM_EPS) + ADAM_WD * w_ref[...])
        nm_ref[...] = nm
        nv_ref[...] = nv

    rb = min(rows, 256)
    spec = pl.BlockSpec((rb, width), lambda i: (i, 0))
    return pl.pallas_call(
        body, name=f"adamw_{rows}_{width}", grid=(rows // rb,),
        in_specs=[spec] * 4, out_specs=[spec] * 3,
        out_shape=[jax.ShapeDtypeStruct((rows, width), F32)] * 3,
        compiler_params=_params(1),
    )(w, g, m, v)


def _rope_tables(positions):
    half = HEAD_DIM // 2
    inv_freq = ROPE_THETA ** (-jnp.arange(half, dtype=F32) * 2.0 / HEAD_DIM)
    ang = positions.astype(F32).reshape(-1, 1) * inv_freq
    cos, sin, zero = jnp.cos(ang), jnp.sin(ang), jnp.zeros_like(ang)
    return (jnp.concatenate([cos] * 4, axis=1), jnp.concatenate([-sin, zero] * 2, axis=1),
            jnp.concatenate([zero, sin] * 2, axis=1))


def _constants():
    idx = jnp.arange(512)
    g512 = jnp.where(idx[:, None] // HEAD_DIM == idx[None, :] // HEAD_DIM, 1.0 / HEAD_DIM, 0.0).astype(BF16)
    j = jnp.arange(SB_T)
    ublk = jnp.where(j[:, None] > j[None, :], 1.0, 0.0).astype(BF16)
    pblk = jnp.where(j[:, None] < j[None, :], 1.0, 0.0).astype(BF16)
    return g512, ublk, pblk


def _pad_rows(v, width):
    return jnp.pad(v, ((0, 0), (0, width - v.shape[1])))


def _local_grads(x2, tgt, positions, norm_gain, q_norm_gain, k_norm_gain, sinks, w_bf, wout_bf, nbatch, seq):
    cos, sa, sb = _rope_tables(positions)
    g512, ublk, pblk = _constants()
    qg512 = jnp.tile(q_norm_gain, (1, 8))
    kg128 = jnp.tile(k_norm_gain, (1, 2))
    sink1 = sinks.reshape(8)

    h, qraw, kraw, qrot, k2, v2, ga, qb, kb, vb, gb = _norm_proj(x2, norm_gain, w_bf, qg512, kg128, g512, cos, sa, sb)
    oa = _swa_fwd(sink1, qrot, k2, v2, nbatch, seq)
    ob, ctab = _sb_fwd(qb, kb, vb, ublk, nbatch, seq)
    y, dout, loss_acc = _out_proj(oa, ob, ga, gb, x2, tgt, wout_bf)

    doa, dob, dga, dgb, dwout = _out_proj_bwd(dout, y, oa, ob, ga, gb, wout_bf)
    dqrot, dk2, dv2, dsink = _swa_bwd(sink1, qrot, k2, v2, doa, nbatch, seq)
    dqb, dkb, dvb = _sb_bwd(qb, kb, vb, dob, ctab, ublk, pblk, nbatch, seq)
    dqa, dkv, dqg, dkg = _qk_grad(qraw, kraw, dqrot, dk2, dv2, qg512, kg128, g512, cos, sa, sb)
    pieces = (dqa, dkv, dga, dqb, dkb, dvb, dgb)
    dwin = _w_in_grad(h, pieces)
    gx, dng = _x_grad(x2, dout, pieces, w_bf, norm_gain)
    dqg64 = dqg.reshape(8, HEAD_DIM).sum(axis=0, keepdims=True)
    dkg64 = dkg.reshape(2, HEAD_DIM).sum(axis=0, keepdims=True)
    return loss_acc[0, 0], gx, dwin, dwout, dng, dqg64, dkg64, dsink[:, 0].reshape(1, 8)


def kernel(x, positions, norm_gain, w_in, q_norm_gain, k_norm_gain, sinks, w_out, loss_target, m_norm_gain, m_w_in, m_q_norm_gain, m_k_norm_gain, m_sinks, m_w_out, v_norm_gain, v_w_in, v_q_norm_gain, v_k_norm_gain, v_sinks, v_w_out):
    nbatch, seq, _ = x.shape
    T = nbatch * seq
    x2 = x.reshape(T, D_MODEL)
    tgt = loss_target.reshape(T, D_MODEL)

    ain, aout = _gather_weights(w_in[0].astype(BF16), w_out[0].astype(BF16))
    w_bf = jnp.transpose(ain, (1, 0, 2)).reshape(D_MODEL, IN_WIDTH)
    wout_bf = aout.reshape(D_MODEL, D_MODEL)

    loss_local, gx, dwin, dwout, dng, dqg64, dkg64, dsink8 = _local_grads(
        x2, tgt, positions, norm_gain, q_norm_gain, k_norm_gain, sinks, w_bf, wout_bf, nbatch, seq)
    loss = lax.psum(loss_local, ("x", "y", "c"))

    small = jnp.concatenate([dng, _pad_rows(dqg64, D_MODEL), _pad_rows(dkg64, D_MODEL),
                             _pad_rows(dsink8, D_MODEL), jnp.zeros((4, D_MODEL), F32)], axis=0)
    gin4 = jnp.transpose(dwin.reshape(D_MODEL, 4, SHARD_IN), (1, 0, 2))
    gout4 = dwout.reshape(4, SHARD_OUT, D_MODEL)
    rin, rout, small_all = _pair_exchange(gin4, gout4, small)
    cidx = lax.axis_index("c").astype(jnp.int32).reshape(1)
    cin4 = _add_half(cidx, gin4, rin, D_MODEL // 2)
    cout4 = _add_half(cidx, gout4, rout, SHARD_OUT // 2)
    sin4, sout4 = _chip_exchange(cin4, cout4)
    g_in, g_out = _share_halves(_sum_slots(sin4), _sum_slots(sout4))
    g_small = _sum_slots(small_all)

    d_in, nm_in, nv_in = _adamw(w_in[0], g_in, m_w_in[0], v_w_in[0])
    d_out, nm_out, nv_out = _adamw(w_out[0], g_out, m_w_out[0], v_w_out[0])
    pack = lambda a, b, c_, d: jnp.concatenate(
        [a, _pad_rows(b, D_MODEL), _pad_rows(c_, D_MODEL), _pad_rows(d, D_MODEL), jnp.zeros((4, D_MODEL), F32)], axis=0)
    w_s = pack(norm_gain, q_norm_gain, k_norm_gain, sinks)
    m_s = pack(m_norm_gain, m_q_norm_gain, m_k_norm_gain, m_sinks)
    v_s = pack(v_norm_gain, v_q_norm_gain, v_k_norm_gain, v_sinks)
    d_s, nm_s, nv_s = _adamw(w_s, g_small, m_s, v_s)
    unpack = lambda a: (a[0:1, :], a[1:2, 0:HEAD_DIM], a[2:3, 0:HEAD_DIM], a[3:4, 0:8])

    g_ng, g_qg, g_kg, g_sk = unpack(g_small)
    d_ng, d_qg, d_kg, d_sk = unpack(d_s)
    m_ng, m_qg, m_kg, m_sk = unpack(nm_s)
    v_ng, v_qg, v_kg, v_sk = unpack(nv_s)
    return (loss, gx.reshape(nbatch, seq, D_MODEL),
            g_ng, g_in[None], g_qg, g_kg, g_sk, g_out[None],
            d_ng, d_in[None], d_qg, d_kg, d_sk, d_out[None],
            m_ng, nm_in[None], m_qg, m_kg, m_sk, nm_out[None],
            v_ng, nv_in[None], v_qg, v_kg, v_sk, nv_out[None])
```

```python
import functools
import math

import jax
import jax.numpy as jnp
from jax import lax
from jax.experimental import pallas as pl
from jax.experimental.pallas import tpu as pltpu

F32 = jnp.float32
BF16 = jnp.bfloat16

D_MODEL = 1024
HEAD_DIM = 64
BLOCK = 128
ROPE_THETA = 10000.0
EPS = 1e-6
QA, KA, VA, GA, QB, KB, VB, GB = 0, 512, 640, 768, 1280, 1792, 2304, 2816
IN_WIDTH = 3328
SHARD_IN = IN_WIDTH // 4
SHARD_OUT = D_MODEL // 4
SCALE = 1.0 / math.sqrt(HEAD_DIM)
NEG = -1e30

ADAM_LR, ADAM_B1, ADAM_B2, ADAM_EPS, ADAM_WD, ADAM_STEP = 0.001, 0.9, 0.999, 1e-08, 0.01, 10

TM = 256
VMEM_LIMIT = 56 * 1024 * 1024
MESH = pl.DeviceIdType.MESH


def _dot(a, b):
    return jnp.dot(a, b, preferred_element_type=F32)


def _dot_nt(a, b):
    return lax.dot_general(a, b, (((1,), (1,)), ((), ())), preferred_element_type=F32)


def _dot_tn(a, b):
    return lax.dot_general(a, b, (((0,), (0,)), ((), ())), preferred_element_type=F32)


def _dot_hl(a, m):
    hi = a.astype(BF16)
    lo = (a - hi.astype(F32)).astype(BF16)
    return _dot(hi, m) + _dot(lo, m)


def _params(n_axes=None, vmem=VMEM_LIMIT):
    sem = None if n_axes is None else ("arbitrary",) * n_axes
    return pltpu.CompilerParams(dimension_semantics=sem, vmem_limit_bytes=vmem)


def _const_spec(shape):
    nd = len(shape)
    return pl.BlockSpec(shape, lambda *_: (0,) * nd)


def _rope_fwd(x, cos, sa, sb):
    return x * cos + pltpu.roll(x, 96, 1) * sa + pltpu.roll(x, 32, 1) * sb


def _rope_bwd(d, cos, sa, sb):
    return d * cos - pltpu.roll(d, 96, 1) * sa - pltpu.roll(d, 32, 1) * sb


def _lane_mask(rows, dtype=F32):
    lane = lax.broadcasted_iota(jnp.int32, (rows, 128), 1)
    return jnp.where(lane < HEAD_DIM, 1.0, 0.0).astype(dtype)


def _norm_proj(x2, ng, w_bf, qg512, kg128, g512, cos, sa, sb):
    T = x2.shape[0]

    def body(x_ref, ng_ref, w_ref, qg_ref, kg_ref, g_ref, cos_ref, sa_ref, sb_ref,
             h_ref, qraw_ref, kraw_ref, qrot_ref, k2_ref, v2_ref, ga_ref, qb_ref, kb_ref, vb_ref, gb_ref):
        xb = x_ref[...]
        r = lax.rsqrt(jnp.mean(xb * xb, axis=-1, keepdims=True) + EPS)
        h = (xb * r * ng_ref[...]).astype(BF16)
        h_ref[...] = h
        cosv, sav, sbv = cos_ref[...], sa_ref[...], sb_ref[...]
        m0 = _lane_mask(TM)

        def dup(v):
            v0 = v * m0
            v1 = v - v0
            return v0 + pltpu.roll(v0, 64, 1), v1 + pltpu.roll(v1, 64, 1)

        qa = _dot(h, w_ref[:, QA:KA])
        qraw_ref[...] = qa
        qn = qa * lax.rsqrt(_dot_hl(qa * qa, g_ref[...]) + EPS) * qg_ref[...]
        for s in range(4):
            qrot_ref[:, s * 128:(s + 1) * 128] = _rope_fwd(qn[:, s * 128:(s + 1) * 128], cosv, sav, sbv).astype(BF16)
        ka = _dot(h, w_ref[:, KA:VA])
        kraw_ref[...] = ka
        kn = ka * lax.rsqrt(_dot_hl(ka * ka, g_ref[0:128, 0:128]) + EPS) * kg_ref[...]
        k0, k1 = dup(_rope_fwd(kn, cosv, sav, sbv))
        k2_ref[:, 0:128] = k0.astype(BF16)
        k2_ref[:, 128:256] = k1.astype(BF16)
        v0, v1 = dup(_dot(h, w_ref[:, VA:GA]))
        v2_ref[:, 0:128] = v0.astype(BF16)
        v2_ref[:, 128:256] = v1.astype(BF16)
        ga_ref[...] = _dot(h, w_ref[:, GA:QB])
        qb_ref[...] = _dot(h, w_ref[:, QB:KB]).astype(BF16)
        kb_ref[...] = _dot(h, w_ref[:, KB:VB]).astype(BF16)
        vb_ref[...] = _dot(h, w_ref[:, VB:GB]).astype(BF16)
        gb_ref[...] = _dot(h, w_ref[:, GB:IN_WIDTH])

    def rows(w):
        return pl.BlockSpec((TM, w), lambda i: (i, 0))

    outs = [(D_MODEL, BF16), (512, F32), (128, F32), (512, BF16), (256, BF16), (256, BF16), (512, F32),
            (512, BF16), (512, BF16), (512, BF16), (512, F32)]
    return pl.pallas_call(
        body, name="norm_proj", grid=(T // TM,),
        in_specs=[rows(D_MODEL), _const_spec((1, D_MODEL)), _const_spec((D_MODEL, IN_WIDTH)), _const_spec((1, 512)),
                  _const_spec((1, 128)), _const_spec((512, 512)), rows(128), rows(128), rows(128)],
        out_specs=[rows(w) for w, _ in outs],
        out_shape=[jax.ShapeDtypeStruct((T, w), dt) for w, dt in outs],
        compiler_params=_params(1),
    )(x2, ng, w_bf, qg512, kg128, g512, cos, sa, sb)


SWA_Q = 512
SWA_SUB = SWA_Q // BLOCK


def _swa_scores(q, kst, mask, sink_ref, pair):
    s_all = _dot_nt(q, kst) * SCALE
    probs, stats = [], []
    for hh in range(2):
        sink = sink_ref[pair * 2 + hh]
        s = jnp.where(mask, s_all[:, hh * 256:(hh + 1) * 256], NEG)
        m = jnp.maximum(jnp.max(s, axis=1, keepdims=True), sink)
        pe = jnp.exp(s - m)
        inv = 1.0 / (jnp.sum(pe, axis=1, keepdims=True) + jnp.exp(sink - m))
        probs.append(pe * inv)
        stats.append(jnp.exp(sink - m) * inv)
    return probs, stats


def _swa_mask(has_prev):
    r = lax.broadcasted_iota(jnp.int32, (128, 256), 0)
    c = lax.broadcasted_iota(jnp.int32, (128, 256), 1)
    band = (c > r) & (c <= r + 128)
    return band if has_prev is True else band & ((c >= 128) | has_prev)


def _swa_keys(prev_ref, main_ref, s, kv, m0b):
    cols = slice(kv * 128, (kv + 1) * 128)
    prev = prev_ref[:, cols] if s == 0 else main_ref[(s - 1) * 128:s * 128, cols]
    kc = jnp.concatenate([prev, main_ref[s * 128:(s + 1) * 128, cols]], axis=0)
    k0 = kc * m0b
    return jnp.concatenate([k0, kc - k0], axis=0)


def _swa_fwd(sinks, qrot, k2, v2, nbatch, seq):
    ni = seq // SWA_Q
    T = nbatch * seq

    def body(sink_ref, q_ref, kp_ref, km_ref, vp_ref, vm_ref, o_ref):
        i = pl.program_id(1)
        m0b = _lane_mask(256, BF16)
        for s in range(SWA_SUB):
            mask = _swa_mask(True if s else i > 0)
            rows = slice(s * 128, (s + 1) * 128)
            for kv in range(2):
                kst = _swa_keys(kp_ref, km_ref, s, kv, m0b)
                vst = _swa_keys(vp_ref, vm_ref, s, kv, m0b)
                for pr in range(2):
                    pair = kv * 2 + pr
                    cols = slice(pair * 128, (pair + 1) * 128)
                    probs, _ = _swa_scores(q_ref[rows, cols], kst, mask, sink_ref, pair)
                    o_ref[rows, cols] = _dot(jnp.concatenate(probs, axis=1).astype(BF16), vst)

    main = lambda b, i: (b * ni + i, 0)
    prev = lambda b, i: ((b * ni + i) * SWA_SUB - jnp.where(i > 0, 1, 0), 0)
    return pl.pallas_call(
        body, name="swa_fwd", grid=(nbatch, ni),
        in_specs=[pl.BlockSpec(memory_space=pltpu.SMEM), pl.BlockSpec((SWA_Q, 512), main),
                  pl.BlockSpec((128, 256), prev), pl.BlockSpec((SWA_Q, 256), main),
                  pl.BlockSpec((128, 256), prev), pl.BlockSpec((SWA_Q, 256), main)],
        out_specs=pl.BlockSpec((SWA_Q, 512), main),
        out_shape=jax.ShapeDtypeStruct((T, 512), F32),
        compiler_params=_params(2),
    )(sinks, qrot, k2, k2, v2, v2)


def _swa_bwd(sinks, qrot, k2, v2, doa, nbatch, seq):
    ni = seq // SWA_Q
    T = nbatch * seq

    def body(sink_ref, q_ref, kp_ref, km_ref, vp_ref, vm_ref, do_ref, dq_ref, dk_ref, dv_ref, ds_ref, dkc, dvc):
        b, i = pl.program_id(0), pl.program_id(1)

        @pl.when((b == 0) & (i == 0))
        def _():
            ds_ref[...] = jnp.zeros_like(ds_ref)

        @pl.when(i == 0)
        def _():
            dkc[...] = jnp.zeros_like(dkc)
            dvc[...] = jnp.zeros_like(dvc)

        @pl.when(i < ni)
        def _():
            m0b = _lane_mask(256, BF16)
            m0 = _lane_mask(128) > 0.5
            for kv in range(2):
                kcols = slice(kv * 128, (kv + 1) * 128)
                dk_own, dv_own = dkc[:, kcols], dvc[:, kcols]
                for s in range(SWA_SUB):
                    mask = _swa_mask(True if s else i > 0)
                    rows = slice(s * 128, (s + 1) * 128)
                    kst = _swa_keys(kp_ref, km_ref, s, kv, m0b)
                    vst = _swa_keys(vp_ref, vm_ref, s, kv, m0b)
                    dkst = jnp.zeros((512, 128), F32)
                    dvst = jnp.zeros((512, 128), F32)
                    for pr in range(2):
                        pair = kv * 2 + pr
                        cols = slice(pair * 128, (pair + 1) * 128)
                        q, do = q_ref[rows, cols], do_ref[rows, cols]
                        probs, psink = _swa_scores(q, kst, mask, sink_ref, pair)
                        dp_all = _dot_nt(do, vst)
                        ds_parts = []
                        for hh in range(2):
                            dp = dp_all[:, hh * 256:(hh + 1) * 256]
                            delta = jnp.sum(probs[hh] * dp, axis=1, keepdims=True)
                            ds_parts.append(probs[hh] * (dp - delta))
                            h = pair * 2 + hh
                            ds_ref[h:h + 1, :] = ds_ref[h:h + 1, :] - jnp.sum(psink[hh] * delta)
                        ds_all = jnp.concatenate(ds_parts, axis=1).astype(BF16)
                        p_all = jnp.concatenate(probs, axis=1).astype(BF16)
                        dq_ref[rows, cols] = _dot(ds_all, kst) * SCALE
                        dkst = dkst + _dot_tn(ds_all, q) * SCALE
                        dvst = dvst + _dot_tn(p_all, do)
                    dk_ref[rows, kcols] = dk_own + jnp.where(m0, dkst[0:128], dkst[256:384])
                    dv_ref[rows, kcols] = dv_own + jnp.where(m0, dvst[0:128], dvst[256:384])
                    dk_own = jnp.where(m0, dkst[128:256], dkst[384:512])
                    dv_own = jnp.where(m0, dvst[128:256], dvst[384:512])
                dkc[:, kcols] = dk_own
                dvc[:, kcols] = dv_own

        @pl.when(i == ni)
        def _():
            dk_ref[...] = jnp.zeros_like(dk_ref)
            dv_ref[...] = jnp.zeros_like(dv_ref)
            dk_ref[0:128, :] = dkc[...]
            dv_ref[0:128, :] = dvc[...]

    main = lambda b, i: (b * ni + jnp.minimum(i, ni - 1), 0)
    prev = lambda b, i: ((b * ni + jnp.minimum(i, ni - 1)) * SWA_SUB - jnp.where(i > 0, 1, 0), 0)
    shifted = lambda b, i: (b * (ni + 1) + i, 0)
    tpad = nbatch * (ni + 1) * SWA_Q
    return pl.pallas_call(
        body, name="swa_bwd", grid=(nbatch, ni + 1),
        in_specs=[pl.BlockSpec(memory_space=pltpu.SMEM), pl.BlockSpec((SWA_Q, 512), main),
                  pl.BlockSpec((128, 256), prev), pl.BlockSpec((SWA_Q, 256), main),
                  pl.BlockSpec((128, 256), prev), pl.BlockSpec((SWA_Q, 256), main),
                  pl.BlockSpec((SWA_Q, 512), main)],
        out_specs=[pl.BlockSpec((SWA_Q, 512), main), pl.BlockSpec((SWA_Q, 256), shifted),
                   pl.BlockSpec((SWA_Q, 256), shifted), _const_spec((8, 128))],
        out_shape=[jax.ShapeDtypeStruct((T, 512), F32), jax.ShapeDtypeStruct((tpad, 256), F32),
                   jax.ShapeDtypeStruct((tpad, 256), F32), jax.ShapeDtypeStruct((8, 128), F32)],
        scratch_shapes=[pltpu.VMEM((128, 256), F32), pltpu.VMEM((128, 256), F32)],
        compiler_params=_params(2),
    )(sinks, qrot, k2, k2, v2, v2, doa)


def _unshift(dkpad, nbatch, seq):
    return dkpad.reshape(nbatch, seq + SWA_Q, 256)[:, BLOCK:BLOCK + seq].reshape(nbatch * seq, 256)


SB_T = 256


def _sb_masks():
    r = lax.broadcasted_iota(jnp.int32, (SB_T, 2 * SB_T), 0)
    c = lax.broadcasted_iota(jnp.int32, (SB_T, 2 * SB_T), 1)
    return (c & (SB_T - 1)) < r


def _sb_logits(q, kst):
    z = _dot_nt(q, kst) * SCALE
    t = jnp.exp(-jnp.abs(z))
    sp = jnp.maximum(z, 0.0) + jnp.log(1.0 + t)
    return z, t, -sp, z - sp


def _pair_rows(ref, j, m0b):
    kj = ref[pl.ds(pl.multiple_of(j * SB_T, SB_T), SB_T), :]
    k0 = kj * m0b
    return jnp.concatenate([k0, kj - k0], axis=0)


def _bcast2(c0, c1):
    return jnp.concatenate([jnp.broadcast_to(c0, (SB_T, SB_T)), jnp.broadcast_to(c1, (SB_T, SB_T))], axis=1)


def _rowsum2(x):
    return jnp.sum(x[:, 0:SB_T], axis=1, keepdims=True), jnp.sum(x[:, SB_T:2 * SB_T], axis=1, keepdims=True)


def _scan2(x, tri):
    return jnp.concatenate([_dot_hl(x[:, 0:SB_T], tri), _dot_hl(x[:, SB_T:2 * SB_T], tri)], axis=1)


def _sb_fwd(qb, kb, vb, ublk, nbatch, seq):
    nb = seq // SB_T
    T = nbatch * seq

    def body(q_ref, k_ref, v_ref, u_ref, o_ref, ctab_ref):
        i = pl.program_id(2)
        q = q_ref[...]
        m0b = _lane_mask(SB_T, BF16)
        u = u_ref[...]
        lane = lax.broadcasted_iota(jnp.int32, (SB_T, 128), 1)
        ctab_ref[...] = jnp.zeros_like(ctab_ref)

        def tile(j, carry, diag):
            c0, c1, acc = carry
            if not diag:
                ctab_ref[...] = jnp.where(lane == j, c0, jnp.where(lane == nb + j, c1, ctab_ref[...]))
            kst = _pair_rows(k_ref, j, m0b)
            vst = _pair_rows(v_ref, j, m0b)
            z, _, lb, la = _sb_logits(q, kst)
            if diag:
                mask = _sb_masks()
                lb = jnp.where(mask, lb, 0.0)
            w = jnp.exp(la + _scan2(lb, u) + _bcast2(c0, c1))
            if diag:
                w = jnp.where(mask, w, 0.0)
            r0, r1 = _rowsum2(lb)
            return c0 + r0, c1 + r1, acc + _dot(w.astype(BF16), vst)

        zc = jnp.zeros((SB_T, 1), F32)
        carry = tile(i, (zc, zc, jnp.zeros((SB_T, 128), F32)), True)
        carry = lax.fori_loop(0, i, lambda jj, cr: tile(i - 1 - jj, cr, False), carry)
        o_ref[...] = carry[2]

    blk = lambda b, p, i: (b * nb + i, p)
    full = lambda b, p, i: (b, p)
    tab = lambda b, p, i: ((b * 4 + p) * nb + i, 0)
    return pl.pallas_call(
        body, name="sb_fwd", grid=(nbatch, 4, nb),
        in_specs=[pl.BlockSpec((SB_T, 128), blk), pl.BlockSpec((seq, 128), full), pl.BlockSpec((seq, 128), full),
                  _const_spec((SB_T, SB_T))],
        out_specs=[pl.BlockSpec((SB_T, 128), blk), pl.BlockSpec((SB_T, 128), tab)],
        out_shape=[jax.ShapeDtypeStruct((T, 512), F32), jax.ShapeDtypeStruct((nbatch * 4 * nb * SB_T, 128), F32)],
        compiler_params=_params(3),
    )(qb, kb, vb, ublk)


def _sb_bwd(qb, kb, vb, dob, ctab, ublk, pblk, nbatch, seq):
    nb = seq // SB_T
    T = nbatch * seq

    def body(q_ref, k_ref, v_ref, do_ref, ctab_ref, u_ref, up_ref, dq_ref, dk_ref, dv_ref):
        i = pl.program_id(2)

        @pl.when(i == 0)
        def _():
            dk_ref[...] = jnp.zeros_like(dk_ref)
            dv_ref[...] = jnp.zeros_like(dv_ref)

        q = q_ref[...]
        do = do_ref[...]
        m0b = _lane_mask(SB_T, BF16)
        m0 = _lane_mask(SB_T) > 0.5
        u, up = u_ref[...], up_ref[...]
        lane = lax.broadcasted_iota(jnp.int32, (SB_T, 128), 1)

        def tile(j, carry, diag):
            s0, s1, dq = carry
            kst = _pair_rows(k_ref, j, m0b)
            vst = _pair_rows(v_ref, j, m0b)
            z, t, lb, la = _sb_logits(q, kst)
            if diag:
                mask = _sb_masks()
                lb = jnp.where(mask, lb, 0.0)
                w = jnp.where(mask, jnp.exp(la + _scan2(lb, u)), 0.0)
            else:
                ct = ctab_ref[...]
                c0 = jnp.sum(jnp.where(lane == j, ct, 0.0), axis=1, keepdims=True)
                c1 = jnp.sum(jnp.where(lane == nb + j, ct, 0.0), axis=1, keepdims=True)
                w = jnp.exp(la + _scan2(lb, u) + _bcast2(c0, c1))
            e = _dot_nt(do, vst) * w
            dlb = _bcast2(s0, s1) + _scan2(e, up)
            inv = 1.0 / (1.0 + t)
            pos = z >= 0.0
            dz = e * (jnp.where(pos, t, 1.0) * inv) - dlb * (jnp.where(pos, 1.0, t) * inv)
            if diag:
                dz = jnp.where(mask, dz, 0.0)
            dzb = (dz * SCALE).astype(BF16)
            dkst = _dot_tn(dzb, q)
            dvst = _dot_tn(w.astype(BF16), do)
            rows = pl.ds(pl.multiple_of(j * SB_T, SB_T), SB_T)
            dk_ref[rows, :] = dk_ref[rows, :] + jnp.where(m0, dkst[0:SB_T], dkst[SB_T:2 * SB_T])
            dv_ref[rows, :] = dv_ref[rows, :] + jnp.where(m0, dvst[0:SB_T], dvst[SB_T:2 * SB_T])
            x0, x1 = _rowsum2(e)
            return s0 + x0, s1 + x1, dq + _dot(dzb, kst)

        zc = jnp.zeros((SB_T, 1), F32)
        carry = lax.fori_loop(0, i, lambda j, cr: tile(j, cr, False), (zc, zc, jnp.zeros((SB_T, 128), F32)))
        dq_ref[...] = tile(i, carry, True)[2]

    blk = lambda b, p, i: (b * nb + i, p)
    full = lambda b, p, i: (b, p)
    tab = lambda b, p, i: ((b * 4 + p) * nb + i, 0)
    return pl.pallas_call(
        body, name="sb_bwd", grid=(nbatch, 4, nb),
        in_specs=[pl.BlockSpec((SB_T, 128), blk), pl.BlockSpec((seq, 128), full), pl.BlockSpec((seq, 128), full),
                  pl.BlockSpec((SB_T, 128), blk), pl.BlockSpec((SB_T, 128), tab),
                  _const_spec((SB_T, SB_T)), _const_spec((SB_T, SB_T))],
        out_specs=[pl.BlockSpec((SB_T, 128), blk), pl.BlockSpec((seq, 128), full), pl.BlockSpec((seq, 128), full)],
        out_shape=[jax.ShapeDtypeStruct((T, 512), F32)] * 3,
        compiler_params=_params(3),
    )(qb, kb, vb, dob, ctab, ublk, pblk)


def _sigmoid(g):
    return 1.0 / (1.0 + jnp.exp(-g))


def _out_proj(oa, ob, ga, gb, x2, tgt, wout_bf):
    T = x2.shape[0]

    def body(oa_ref, ob_ref, ga_ref, gb_ref, x_ref, t_ref, w_ref, y_ref, dout_ref, loss_ref):
        @pl.when(pl.program_id(0) == 0)
        def _():
            loss_ref[...] = jnp.zeros_like(loss_ref)

        ga, gb = ga_ref[...], gb_ref[...]
        ya = (oa_ref[...] * (ga * _sigmoid(ga))).astype(BF16)
        yb = (ob_ref[...] * (gb * _sigmoid(gb))).astype(BF16)
        y_ref[:, 0:512] = ya
        y_ref[:, 512:1024] = yb
        out = x_ref[...] + _dot(ya, w_ref[0:512, :]) + _dot(yb, w_ref[512:1024, :])
        diff = out - t_ref[...]
        dout_ref[...] = diff * (1.0 / D_MODEL)
        loss_ref[...] = loss_ref[...] + jnp.sum(diff * diff) * (0.5 / D_MODEL)

    rows = lambda w: pl.BlockSpec((TM, w), lambda i: (i, 0))
    return pl.pallas_call(
        body, name="out_proj", grid=(T // TM,),
        in_specs=[rows(512), rows(512), rows(512), rows(512), rows(D_MODEL), rows(D_MODEL),
                  _const_spec((D_MODEL, D_MODEL))],
        out_specs=[rows(D_MODEL), rows(D_MODEL), _const_spec((8, 128))],
        out_shape=[jax.ShapeDtypeStruct((T, D_MODEL), BF16), jax.ShapeDtypeStruct((T, D_MODEL), F32),
                   jax.ShapeDtypeStruct((8, 128), F32)],
        compiler_params=_params(1),
    )(oa, ob, ga, gb, x2, tgt, wout_bf)


def _out_proj_bwd(dout, y, oa, ob, ga, gb, wout_bf):
    T = dout.shape[0]

    def body(dout_ref, y_ref, oa_ref, ob_ref, ga_ref, gb_ref, w_ref, doa_ref, dob_ref, dga_ref, dgb_ref, dw_ref):
        @pl.when(pl.program_id(0) == 0)
        def _():
            dw_ref[...] = jnp.zeros_like(dw_ref)

        db = dout_ref[...].astype(BF16)
        dw_ref[...] = dw_ref[...] + _dot_tn(y_ref[...], db)
        for o_ref, g_ref, do_ref, dg_ref, lo in ((oa_ref, ga_ref, doa_ref, dga_ref, 0), (ob_ref, gb_ref, dob_ref, dgb_ref, 512)):
            dy = _dot_nt(db, w_ref[lo:lo + 512, :])
            g = g_ref[...]
            sg = _sigmoid(g)
            do_ref[...] = (dy * (g * sg)).astype(BF16)
            dg_ref[...] = (dy * o_ref[...] * (sg * (1.0 + g * (1.0 - sg)))).astype(BF16)

    rows = lambda w: pl.BlockSpec((TM, w), lambda i: (i, 0))
    return pl.pallas_call(
        body, name="out_proj_bwd", grid=(T // TM,),
        in_specs=[rows(D_MODEL), rows(D_MODEL), rows(512), rows(512), rows(512), rows(512),
                  _const_spec((D_MODEL, D_MODEL))],
        out_specs=[rows(512)] * 4 + [_const_spec((D_MODEL, D_MODEL))],
        out_shape=[jax.ShapeDtypeStruct((T, 512), BF16)] * 4 + [jax.ShapeDtypeStruct((D_MODEL, D_MODEL), F32)],
        compiler_params=_params(1),
    )(dout, y, oa, ob, ga, gb, wout_bf)


def _qk_grad(qraw, kraw, dqrot, dk2, dv2, qg512, kg128, g512, cos, sa, sb):
    T = qraw.shape[0]

    def body(qraw_ref, kraw_ref, dqrot_ref, dk2_ref, dv2_ref, qg_ref, kg_ref, g_ref, cos_ref, sa_ref, sb_ref,
             dqa_ref, dkv_ref, dqg_ref, dkg_ref):
        @pl.when(pl.program_id(0) == 0)
        def _():
            dqg_ref[...] = jnp.zeros_like(dqg_ref)
            dkg_ref[...] = jnp.zeros_like(dkg_ref)

        cosv, sav, sbv = cos_ref[...], sa_ref[...], sb_ref[...]
        m0 = _lane_mask(TM) > 0.5

        def head_norm_bwd(raw, dn_rot, gmat, gain):
            r = lax.rsqrt(_dot_hl(raw * raw, gmat) + EPS)
            n = raw * r
            dn = dn_rot * gain
            return r * (dn - n * _dot_hl(dn * n, gmat)), jnp.sum(dn_rot * n, axis=0, keepdims=True)

        def fold(ref):
            a, b = ref[:, 0:128], ref[:, 128:256]
            return jnp.where(m0, a + pltpu.roll(a, 64, 1), b + pltpu.roll(b, 64, 1))

        dqn = jnp.concatenate([_rope_bwd(dqrot_ref[:, s * 128:(s + 1) * 128], cosv, sav, sbv) for s in range(4)], axis=1)
        dqa, dqg = head_norm_bwd(qraw_ref[...], dqn, g_ref[...], qg_ref[...])
        dka, dkg = head_norm_bwd(kraw_ref[...], _rope_bwd(fold(dk2_ref), cosv, sav, sbv), g_ref[0:128, 0:128], kg_ref[...])
        dqa_ref[...] = dqa.astype(BF16)
        dkv_ref[:, 0:128] = dka.astype(BF16)
        dkv_ref[:, 128:256] = fold(dv2_ref).astype(BF16)
        dqg_ref[...] = dqg_ref[...] + dqg
        dkg_ref[...] = dkg_ref[...] + dkg

    rows = lambda w: pl.BlockSpec((TM, w), lambda i: (i, 0))
    return pl.pallas_call(
        body, name="qk_grad", grid=(T // TM,),
        in_specs=[rows(512), rows(128), rows(512), rows(256), rows(256), _const_spec((1, 512)), _const_spec((1, 128)),
                  _const_spec((512, 512)), rows(128), rows(128), rows(128)],
        out_specs=[rows(512), rows(256), _const_spec((1, 512)), _const_spec((1, 128))],
        out_shape=[jax.ShapeDtypeStruct((T, 512), BF16), jax.ShapeDtypeStruct((T, 256), BF16),
                   jax.ShapeDtypeStruct((1, 512), F32), jax.ShapeDtypeStruct((1, 128), F32)],
        compiler_params=_params(1),
    )(qraw, kraw, dqrot, dk2, dv2, qg512, kg128, g512, cos, sa, sb)


_PIECES = ((QA, 512), (KA, 256), (GA, 512), (QB, 512), (KB, 512), (VB, 512), (GB, 512))


def _w_in_grad(h, pieces):
    T = h.shape[0]

    def body(h_ref, *refs):
        dw_ref = refs[-1]

        @pl.when(pl.program_id(0) == 0)
        def _():
            dw_ref[...] = jnp.zeros_like(dw_ref)

        hb = h_ref[...]
        for (lo, width), p_ref in zip(_PIECES, refs[:-1]):
            dw_ref[:, lo:lo + width] = dw_ref[:, lo:lo + width] + _dot_tn(hb, p_ref[...].astype(BF16))

    rows = lambda w: pl.BlockSpec((TM, w), lambda i: (i, 0))
    return pl.pallas_call(
        body, name="w_in_grad", grid=(T // TM,),
        in_specs=[rows(D_MODEL)] + [rows(w) for _, w in _PIECES],
        out_specs=_const_spec((D_MODEL, IN_WIDTH)),
        out_shape=jax.ShapeDtypeStruct((D_MODEL, IN_WIDTH), F32),
        compiler_params=_params(1),
    )(h, *pieces)


def _x_grad(x2, dout, pieces, w_bf, ng):
    T = x2.shape[0]
    npc = len(_PIECES)

    def body(x_ref, dout_ref, *refs):
        w_ref, ng_ref, gx_ref, dng_ref = refs[npc:]

        @pl.when(pl.program_id(0) == 0)
        def _():
            dng_ref[...] = jnp.zeros_like(dng_ref)

        dh = jnp.zeros((TM, D_MODEL), F32)
        for (lo, width), p_ref in zip(_PIECES, refs[:npc]):
            dh = dh + _dot_nt(p_ref[...].astype(BF16), w_ref[:, lo:lo + width])
        xb = x_ref[...]
        r = lax.rsqrt(jnp.mean(xb * xb, axis=-1, keepdims=True) + EPS)
        n = xb * r
        dn = dh * ng_ref[...]
        gx_ref[...] = dout_ref[...] + r * (dn - n * jnp.mean(dn * n, axis=-1, keepdims=True))
        dng_ref[...] = dng_ref[...] + jnp.sum(dh * n, axis=0, keepdims=True)

    rows = lambda w: pl.BlockSpec((TM, w), lambda i: (i, 0))
    return pl.pallas_call(
        body, name="x_grad", grid=(T // TM,),
        in_specs=[rows(D_MODEL), rows(D_MODEL)] + [rows(w) for _, w in _PIECES]
        + [_const_spec((D_MODEL, IN_WIDTH)), _const_spec((1, D_MODEL))],
        out_specs=[rows(D_MODEL), _const_spec((1, D_MODEL))],
        out_shape=[jax.ShapeDtypeStruct((T, D_MODEL), F32), jax.ShapeDtypeStruct((1, D_MODEL), F32)],
        compiler_params=_params(1),
    )(x2, dout, *pieces, w_bf, ng)


HBM = pl.BlockSpec(memory_space=pl.ANY)


def _place():
    x, y, c = lax.axis_index("x"), lax.axis_index("y"), lax.axis_index("c")
    chips = [(1 - x, y), (x, 1 - y), (1 - x, 1 - y)]
    return x, y, c, chips


def _remote(src, dst, sems, k, to):
    return pltpu.make_async_remote_copy(src_ref=src, dst_ref=dst, send_sem=sems[0].at[k], recv_sem=sems[1].at[k],
                                        device_id=to, device_id_type=MESH)


def _gather_weights(win_b, wout_b):
    hi, ho = D_MODEL // 2, SHARD_OUT // 2

    def body(win_ref, wout_ref, ain_ref, aout_ref, ssem, rsem):
        x, y, c, chips = _place()
        me, sib = 2 * x + y, (x, y, 1 - c)
        sems = (ssem, rsem)
        rin, rout = pl.ds(c * hi, hi), pl.ds(c * ho, ho)
        oin, oout = pl.ds((1 - c) * hi, hi), pl.ds((1 - c) * ho, ho)
        sent = []
        for j, (cx, cy) in enumerate(chips):
            sent.append(_remote(win_ref.at[rin], ain_ref.at[me, rin], sems, 2 * j, (cx, cy, c)))
            sent.append(_remote(wout_ref.at[rout], aout_ref.at[me, rout], sems, 2 * j + 1, (cx, cy, c)))
        for cp in sent:
            cp.start()
        for j, (cx, cy) in enumerate(chips):
            k = 2 * cx + cy
            _remote(win_ref.at[rin], ain_ref.at[k, rin], sems, 2 * j, sib).wait_recv()
            fwd = _remote(ain_ref.at[k, rin], ain_ref.at[k, rin], sems, 6 + 2 * j, sib)
            fwd.start()
            sent.append(fwd)
            _remote(wout_ref.at[rout], aout_ref.at[k, rout], sems, 2 * j + 1, sib).wait_recv()
            fwd = _remote(aout_ref.at[k, rout], aout_ref.at[k, rout], sems, 7 + 2 * j, sib)
            fwd.start()
            sent.append(fwd)
        for j, (cx, cy) in enumerate(chips):
            k = 2 * cx + cy
            _remote(win_ref.at[oin], ain_ref.at[k, oin], sems, 6 + 2 * j, sib).wait_recv()
            _remote(wout_ref.at[oout], aout_ref.at[k, oout], sems, 7 + 2 * j, sib).wait_recv()
        for cp in sent:
            cp.wait_send()

    ain, aout = pl.pallas_call(
        body, name="gather_weights", in_specs=[HBM, HBM], out_specs=[HBM, HBM],
        out_shape=[jax.ShapeDtypeStruct((4, D_MODEL, SHARD_IN), BF16), jax.ShapeDtypeStruct((4, SHARD_OUT, D_MODEL), BF16)],
        scratch_shapes=[pltpu.SemaphoreType.DMA((12,)), pltpu.SemaphoreType.DMA((12,))],
    )(win_b, wout_b)
    me = 2 * lax.axis_index("x") + lax.axis_index("y")
    w_bf = jnp.concatenate([jnp.where(me == k, win_b, ain[k]) for k in range(4)], axis=1)
    wout_bf = jnp.concatenate([jnp.where(me == k, wout_b, aout[k]) for k in range(4)], axis=0)
    return w_bf, wout_bf


def _pair_exchange(gin4, gout4, small):
    hi, ho = D_MODEL // 2, SHARD_OUT // 2

    def body(gin_ref, gout_ref, small_ref, rin_ref, rout_ref, sall_ref, ssem, rsem, lsem):
        x, y, c, _ = _place()
        sems = (ssem, rsem)
        sib = (x, y, 1 - c)
        me = 4 * x + 2 * y + c
        own = pltpu.make_async_copy(small_ref, sall_ref.at[me], lsem.at[0])
        own.start()
        sent = [_remote(gin_ref.at[:, pl.ds((1 - c) * hi, hi), :], rin_ref, sems, 0, sib),
                _remote(gout_ref.at[:, pl.ds((1 - c) * ho, ho), :], rout_ref, sems, 1, sib)]
        flips = [(fx, fy, fc) for fx in (0, 1) for fy in (0, 1) for fc in (0, 1)][1:]
        for n, (fx, fy, fc) in enumerate(flips):
            peer = (x ^ fx, y ^ fy, c ^ fc)
            sent.append(_remote(small_ref, sall_ref.at[me], sems, 2 + n, peer))
        for cp in sent:
            cp.start()
        _remote(gin_ref.at[:, pl.ds(c * hi, hi), :], rin_ref, sems, 0, sib).wait_recv()
        _remote(gout_ref.at[:, pl.ds(c * ho, ho), :], rout_ref, sems, 1, sib).wait_recv()
        for n, (fx, fy, fc) in enumerate(flips):
            src = 4 * (x ^ fx) + 2 * (y ^ fy) + (c ^ fc)
            _remote(small_ref, sall_ref.at[src], sems, 2 + n, sib).wait_recv()
        for cp in sent:
            cp.wait_send()
        own.wait()

    return pl.pallas_call(
        body, name="pair_exchange", in_specs=[HBM, HBM, HBM], out_specs=[HBM, HBM, HBM],
        out_shape=[jax.ShapeDtypeStruct((4, hi, SHARD_IN), gin4.dtype), jax.ShapeDtypeStruct((4, ho, D_MODEL), gout4.dtype),
                   jax.ShapeDtypeStruct((8,) + small.shape, F32)],
        scratch_shapes=[pltpu.SemaphoreType.DMA((9,)), pltpu.SemaphoreType.DMA((9,)), pltpu.SemaphoreType.DMA((1,))],
    )(gin4, gout4, small)


def _chip_exchange(cin4, cout4):
    def body(cin_ref, cout_ref, rin_ref, rout_ref, ssem, rsem):
        x, y, c, chips = _place()
        sems = (ssem, rsem)
        sent = []
        for j, (cx, cy) in enumerate(chips):
            k = 2 * cx + cy
            sent.append(_remote(cin_ref.at[k], rin_ref.at[j], sems, 2 * j, (cx, cy, c)))
            sent.append(_remote(cout_ref.at[k], rout_ref.at[j], sems, 2 * j + 1, (cx, cy, c)))
        for cp in sent:
            cp.start()
        for j, (cx, cy) in enumerate(chips):
            _remote(cin_ref.at[0], rin_ref.at[j], sems, 2 * j, (cx, cy, c)).wait_recv()
            _remote(cout_ref.at[0], rout_ref.at[j], sems, 2 * j + 1, (cx, cy, c)).wait_recv()
        for cp in sent:
            cp.wait_send()

    return pl.pallas_call(
        body, name="chip_exchange", in_specs=[HBM, HBM], out_specs=[HBM, HBM],
        out_shape=[jax.ShapeDtypeStruct((3,) + cin4.shape[1:], cin4.dtype),
                   jax.ShapeDtypeStruct((3,) + cout4.shape[1:], cout4.dtype)],
        scratch_shapes=[pltpu.SemaphoreType.DMA((6,)), pltpu.SemaphoreType.DMA((6,))],
    )(cin4, cout4)


def _share_halves(hin, hout):
    def body(hin_ref, hout_ref, oin_ref, oout_ref, ssem, rsem):
        x, y, c, _ = _place()
        sems = (ssem, rsem)
        sib = (x, y, 1 - c)
        sent = [_remote(hin_ref, oin_ref, sems, 0, sib), _remote(hout_ref, oout_ref, sems, 1, sib)]
        for cp in sent:
            cp.start()
        for cp in sent:
            cp.wait()

    return pl.pallas_call(
        body, name="share_halves", in_specs=[HBM, HBM], out_specs=[HBM, HBM],
        out_shape=[jax.ShapeDtypeStruct(hin.shape, F32), jax.ShapeDtypeStruct(hout.shape, F32)],
        scratch_shapes=[pltpu.SemaphoreType.DMA((2,)), pltpu.SemaphoreType.DMA((2,))],
    )(hin, hout)


def _add_half(cidx, full4, recv4, half):
    nblk, _, width = recv4.shape

    def body(c_ref, a_ref, b_ref, o_ref):
        o_ref[...] = (a_ref[...] + b_ref[...].astype(F32)).astype(BF16)

    return pl.pallas_call(
        body, name=f"add_half_{width}",
        grid_spec=pltpu.PrefetchScalarGridSpec(
            num_scalar_prefetch=1, grid=(nblk,),
            in_specs=[pl.BlockSpec((1, half, width), lambda k, c: (k, c[0], 0)),
                      pl.BlockSpec((1, half, width), lambda k, c: (k, 0, 0))],
            out_specs=pl.BlockSpec((1, half, width), lambda k, c: (k, 0, 0))),
        out_shape=jax.ShapeDtypeStruct(recv4.shape, BF16),
        compiler_params=_params(1),
    )(cidx, full4, recv4)


def _sum_chips(chip, own4, recv3):
    _, rows, width = recv3.shape
    rb = min(rows, 128)

    def body(k_ref, a_ref, r_ref, o_ref):
        acc = a_ref[0].astype(F32)
        for s in range(3):
            acc = acc + r_ref[s].astype(F32)
        o_ref[...] = acc

    return pl.pallas_call(
        body, name=f"sum_chips_{width}",
        grid_spec=pltpu.PrefetchScalarGridSpec(
            num_scalar_prefetch=1, grid=(rows // rb,),
            in_specs=[pl.BlockSpec((1, rb, width), lambda i, k: (k[0], i, 0)),
                      pl.BlockSpec((3, rb, width), lambda i, k: (0, i, 0))],
            out_specs=pl.BlockSpec((rb, width), lambda i, k: (i, 0))),
        out_shape=jax.ShapeDtypeStruct((rows, width), F32),
        compiler_params=_params(1),
    )(chip, own4, recv3)


def _sum_slots(r4):
    n, rows, width = r4.shape

    def body(r_ref, o_ref):
        acc = r_ref[0]
        for s in range(1, n):
            acc = acc + r_ref[s]
        o_ref[...] = acc

    return pl.pallas_call(
        body, name=f"sum_slots_{n}_{rows}_{width}", grid=(1,),
        in_specs=[pl.BlockSpec((n, rows, width), lambda i: (0, 0, 0))],
        out_specs=pl.BlockSpec((rows, width), lambda i: (0, 0)),
        out_shape=jax.ShapeDtypeStruct((rows, width), F32),
        compiler_params=_params(1),
    )(r4)


def _adam_math(w, g, m, v):
    c1 = 1.0 - ADAM_B1 ** ADAM_STEP
    c2 = 1.0 - ADAM_B2 ** ADAM_STEP
    nm = ADAM_B1 * m + (1.0 - ADAM_B1) * g
    nv = ADAM_B2 * v + (1.0 - ADAM_B2) * (g * g)
    return -ADAM_LR * ((nm / c1) / (jnp.sqrt(nv / c2) + ADAM_EPS) + ADAM_WD * w), nm, nv


def _adamw(w, g, m, v):
    rows, width = w.shape

    def body(w_ref, g_ref, m_ref, v_ref, d_ref, nm_ref, nv_ref):
        d_ref[...], nm_ref[...], nv_ref[...] = _adam_math(w_ref[...], g_ref[...], m_ref[...], v_ref[...])

    spec = pl.BlockSpec((rows, width), lambda i: (0, 0))
    return pl.pallas_call(
        body, name=f"adamw_{rows}_{width}", grid=(1,),
        in_specs=[spec] * 4, out_specs=[spec] * 3,
        out_shape=[jax.ShapeDtypeStruct((rows, width), F32)] * 3,
        compiler_params=_params(1),
    )(w, g, m, v)


def _adamw_halves(cidx, w, own, recv, m, v):
    rows, width = w.shape
    rb = rows // 4

    def body(c_ref, w_ref, own_ref, recv_ref, m_ref, v_ref, g_ref, d_ref, nm_ref, nv_ref):
        mine = (pl.program_id(0) // 2) == c_ref[0]
        g = jnp.where(mine, own_ref[...], recv_ref[...])
        g_ref[...] = g
        d_ref[...], nm_ref[...], nv_ref[...] = _adam_math(w_ref[...], g, m_ref[...], v_ref[...])

    full = pl.BlockSpec((rb, width), lambda i, c: (i, 0))
    half = pl.BlockSpec((rb, width), lambda i, c: (i % 2, 0))
    return pl.pallas_call(
        body, name=f"adamw_halves_{width}",
        grid_spec=pltpu.PrefetchScalarGridSpec(
            num_scalar_prefetch=1, grid=(4,),
            in_specs=[full, half, half, full, full], out_specs=[full] * 4),
        out_shape=[jax.ShapeDtypeStruct((rows, width), F32)] * 4,
        compiler_params=_params(1),
    )(cidx, w, own, recv, m, v)


def _rope_tables(positions):
    half = HEAD_DIM // 2
    inv_freq = ROPE_THETA ** (-jnp.arange(half, dtype=F32) * 2.0 / HEAD_DIM)
    ang = positions.astype(F32).reshape(-1, 1) * inv_freq
    cos, sin, zero = jnp.cos(ang), jnp.sin(ang), jnp.zeros_like(ang)
    return (jnp.concatenate([cos] * 4, axis=1), jnp.concatenate([-sin, zero] * 2, axis=1),
            jnp.concatenate([zero, sin] * 2, axis=1))


def _constants():
    idx = jnp.arange(512)
    g512 = jnp.where(idx[:, None] // HEAD_DIM == idx[None, :] // HEAD_DIM, 1.0 / HEAD_DIM, 0.0).astype(BF16)
    j = jnp.arange(SB_T)
    ublk = jnp.where(j[:, None] > j[None, :], 1.0, 0.0).astype(BF16)
    pblk = jnp.where(j[:, None] < j[None, :], 1.0, 0.0).astype(BF16)
    return g512, ublk, pblk


def _pad_rows(v, width):
    return jnp.pad(v, ((0, 0), (0, width - v.shape[1])))


def _local_grads(x2, tgt, positions, norm_gain, q_norm_gain, k_norm_gain, sinks, w_bf, wout_bf, nbatch, seq):
    cos, sa, sb = _rope_tables(positions)
    g512, ublk, pblk = _constants()
    qg512 = jnp.tile(q_norm_gain, (1, 8))
    kg128 = jnp.tile(k_norm_gain, (1, 2))
    sink1 = sinks.reshape(8)

    h, qraw, kraw, qrot, k2, v2, ga, qb, kb, vb, gb = _norm_proj(x2, norm_gain, w_bf, qg512, kg128, g512, cos, sa, sb)
    oa = _swa_fwd(sink1, qrot, k2, v2, nbatch, seq)
    ob, ctab = _sb_fwd(qb, kb, vb, ublk, nbatch, seq)
    y, dout, loss_acc = _out_proj(oa, ob, ga, gb, x2, tgt, wout_bf)

    doa, dob, dga, dgb, dwout = _out_proj_bwd(dout, y, oa, ob, ga, gb, wout_bf)
    dqrot, dk2, dv2, dsink = _swa_bwd(sink1, qrot, k2, v2, doa, nbatch, seq)
    dqb, dkb, dvb = _sb_bwd(qb, kb, vb, dob, ctab, ublk, pblk, nbatch, seq)
    dk2, dv2 = _unshift(dk2, nbatch, seq), _unshift(dv2, nbatch, seq)
    dqa, dkv, dqg, dkg = _qk_grad(qraw, kraw, dqrot, dk2, dv2, qg512, kg128, g512, cos, sa, sb)
    pieces = (dqa, dkv, dga, dqb, dkb, dvb, dgb)
    dwin = _w_in_grad(h, pieces)
    gx, dng = _x_grad(x2, dout, pieces, w_bf, norm_gain)
    dqg64 = dqg.reshape(8, HEAD_DIM).sum(axis=0, keepdims=True)
    dkg64 = dkg.reshape(2, HEAD_DIM).sum(axis=0, keepdims=True)
    return loss_acc[0, 0], gx, dwin, dwout, dng, dqg64, dkg64, dsink[:, 0].reshape(1, 8)


def _reduce_grads(dwin, dwout, small):
    gin4 = jnp.transpose(dwin.reshape(D_MODEL, 4, SHARD_IN), (1, 0, 2))
    gout4 = dwout.reshape(4, SHARD_OUT, D_MODEL)
    rin, rout, small_all = _pair_exchange(gin4.astype(BF16), gout4.astype(BF16), small)
    cidx = lax.axis_index("c").astype(jnp.int32).reshape(1)
    chip = (2 * lax.axis_index("x") + lax.axis_index("y")).astype(jnp.int32).reshape(1)
    cin4 = _add_half(cidx, gin4, rin, D_MODEL // 2)
    cout4 = _add_half(cidx, gout4, rout, SHARD_OUT // 2)
    rin3, rout3 = _chip_exchange(cin4, cout4)
    own_in, own_out = _sum_chips(chip, cin4, rin3), _sum_chips(chip, cout4, rout3)
    sib_in, sib_out = _share_halves(own_in, own_out)
    return cidx, own_in, sib_in, own_out, sib_out, _sum_slots(small_all)


def kernel(x, positions, norm_gain, w_in, q_norm_gain, k_norm_gain, sinks, w_out, loss_target, m_norm_gain, m_w_in, m_q_norm_gain, m_k_norm_gain, m_sinks, m_w_out, v_norm_gain, v_w_in, v_q_norm_gain, v_k_norm_gain, v_sinks, v_w_out):
    nbatch, seq, _ = x.shape
    T = nbatch * seq
    x2 = x.reshape(T, D_MODEL)
    tgt = loss_target.reshape(T, D_MODEL)

    w_bf, wout_bf = _gather_weights(w_in[0].astype(BF16), w_out[0].astype(BF16))

    loss_local, gx, dwin, dwout, dng, dqg64, dkg64, dsink8 = _local_grads(
        x2, tgt, positions, norm_gain, q_norm_gain, k_norm_gain, sinks, w_bf, wout_bf, nbatch, seq)
    loss = lax.psum(loss_local, ("x", "y", "c"))

    small = jnp.concatenate([dng, _pad_rows(dqg64, D_MODEL), _pad_rows(dkg64, D_MODEL),
                             _pad_rows(dsink8, D_MODEL), jnp.zeros((4, D_MODEL), F32)], axis=0)
    cidx, own_in, sib_in, own_out, sib_out, g_small = _reduce_grads(dwin, dwout, small)

    g_in, d_in, nm_in, nv_in = _adamw_halves(cidx, w_in[0], own_in, sib_in, m_w_in[0], v_w_in[0])
    g_out, d_out, nm_out, nv_out = _adamw_halves(cidx, w_out[0], own_out, sib_out, m_w_out[0], v_w_out[0])
    pack = lambda a, b, c_, d: jnp.concatenate(
        [a, _pad_rows(b, D_MODEL), _pad_rows(c_, D_MODEL), _pad_rows(d, D_MODEL), jnp.zeros((4, D_MODEL), F32)], axis=0)
    w_s = pack(norm_gain, q_norm_gain, k_norm_gain, sinks)
    m_s = pack(m_norm_gain, m_q_norm_gain, m_k_norm_gain, m_sinks)
    v_s = pack(v_norm_gain, v_q_norm_gain, v_k_norm_gain, v_sinks)
    d_s, nm_s, nv_s = _adamw(w_s, g_small, m_s, v_s)
    unpack = lambda a: (a[0:1, :], a[1:2, 0:HEAD_DIM], a[2:3, 0:HEAD_DIM], a[3:4, 0:8])

    g_ng, g_qg, g_kg, g_sk = unpack(g_small)
    d_ng, d_qg, d_kg, d_sk = unpack(d_s)
    m_ng, m_qg, m_kg, m_sk = unpack(nm_s)
    v_ng, v_qg, v_kg, v_sk = unpack(nv_s)
    return (loss, gx.reshape(nbatch, seq, D_MODEL),
            g_ng, g_in[None], g_qg, g_kg, g_sk, g_out[None],
            d_ng, d_in[None], d_qg, d_kg, d_sk, d_out[None],
            m_ng, nm_in[None], m_qg, m_kg, m_sk, nm_out[None],
            v_ng, nv_in[None], v_qg, v_kg, v_sk, nv_out[None])
```

```python
import functools
import math

import jax
import jax.numpy as jnp
from jax import lax
from jax.experimental import pallas as pl
from jax.experimental.pallas import tpu as pltpu

F32 = jnp.float32
BF16 = jnp.bfloat16

D_MODEL = 1024
HEAD_DIM = 64
BLOCK = 128
ROPE_THETA = 10000.0
EPS = 1e-6
QA, KA, VA, GA, QB, KB, VB, GB = 0, 512, 640, 768, 1280, 1792, 2304, 2816
IN_WIDTH = 3328
SHARD_IN = IN_WIDTH // 4
SHARD_OUT = D_MODEL // 4
SCALE = 1.0 / math.sqrt(HEAD_DIM)
NEG = -1e30

ADAM_LR, ADAM_B1, ADAM_B2, ADAM_EPS, ADAM_WD, ADAM_STEP = 0.001, 0.9, 0.999, 1e-08, 0.01, 10

TM = 256
VMEM_LIMIT = 56 * 1024 * 1024
MESH = pl.DeviceIdType.MESH


def _dot(a, b):
    return jnp.dot(a, b, preferred_element_type=F32)


def _dot_nt(a, b):
    return lax.dot_general(a, b, (((1,), (1,)), ((), ())), preferred_element_type=F32)


def _dot_tn(a, b):
    return lax.dot_general(a, b, (((0,), (0,)), ((), ())), preferred_element_type=F32)


def _dot_hl(a, m):
    hi = a.astype(BF16)
    lo = (a - hi.astype(F32)).astype(BF16)
    return _dot(hi, m) + _dot(lo, m)


def _params(n_axes=None, vmem=VMEM_LIMIT):
    sem = None if n_axes is None else ("arbitrary",) * n_axes
    return pltpu.CompilerParams(dimension_semantics=sem, vmem_limit_bytes=vmem)


def _const_spec(shape):
    nd = len(shape)
    return pl.BlockSpec(shape, lambda *_: (0,) * nd)


def _rope_fwd(x, cos, sa, sb):
    return x * cos + pltpu.roll(x, 96, 1) * sa + pltpu.roll(x, 32, 1) * sb


def _rope_bwd(d, cos, sa, sb):
    return d * cos - pltpu.roll(d, 96, 1) * sa - pltpu.roll(d, 32, 1) * sb


def _lane_mask(rows, dtype=F32):
    lane = lax.broadcasted_iota(jnp.int32, (rows, 128), 1)
    return jnp.where(lane < HEAD_DIM, 1.0, 0.0).astype(dtype)


def _norm_proj(x2, ng, w_bf, qg512, kg128, g512, cos, sa, sb):
    T = x2.shape[0]

    def body(x_ref, ng_ref, w_ref, qg_ref, kg_ref, g_ref, cos_ref, sa_ref, sb_ref,
             h_ref, qraw_ref, kraw_ref, qrot_ref, k2_ref, v2_ref, ga_ref, qb_ref, kb_ref, vb_ref, gb_ref):
        xb = x_ref[...]
        r = lax.rsqrt(jnp.mean(xb * xb, axis=-1, keepdims=True) + EPS)
        h = (xb * r * ng_ref[...]).astype(BF16)
        h_ref[...] = h
        cosv, sav, sbv = cos_ref[...], sa_ref[...], sb_ref[...]
        m0 = _lane_mask(TM)

        def dup(v):
            v0 = v * m0
            v1 = v - v0
            return v0 + pltpu.roll(v0, 64, 1), v1 + pltpu.roll(v1, 64, 1)

        qa = _dot(h, w_ref[:, QA:KA])
        qraw_ref[...] = qa
        qn = qa * lax.rsqrt(_dot_hl(qa * qa, g_ref[...]) + EPS) * qg_ref[...]
        for s in range(4):
            qs = _rope_fwd(qn[:, s * 128:(s + 1) * 128], cosv, sav, sbv)
            qrot_ref[:, s * 128:(s + 1) * 128] = (qs * SCALE).astype(BF16)
        ka = _dot(h, w_ref[:, KA:VA])
        kraw_ref[...] = ka
        kn = ka * lax.rsqrt(_dot_hl(ka * ka, g_ref[0:128, 0:128]) + EPS) * kg_ref[...]
        k0, k1 = dup(_rope_fwd(kn, cosv, sav, sbv))
        k2_ref[:, 0:128] = k0.astype(BF16)
        k2_ref[:, 128:256] = k1.astype(BF16)
        v0, v1 = dup(_dot(h, w_ref[:, VA:GA]))
        v2_ref[:, 0:128] = v0.astype(BF16)
        v2_ref[:, 128:256] = v1.astype(BF16)
        ga_ref[...] = _dot(h, w_ref[:, GA:QB])
        qb_ref[...] = (_dot(h, w_ref[:, QB:KB]) * SCALE).astype(BF16)
        kb_ref[...] = _dot(h, w_ref[:, KB:VB]).astype(BF16)
        vb_ref[...] = _dot(h, w_ref[:, VB:GB]).astype(BF16)
        gb_ref[...] = _dot(h, w_ref[:, GB:IN_WIDTH])

    def rows(w):
        return pl.BlockSpec((TM, w), lambda i: (i, 0))

    outs = [(D_MODEL, BF16), (512, F32), (128, F32), (512, BF16), (256, BF16), (256, BF16), (512, F32),
            (512, BF16), (512, BF16), (512, BF16), (512, F32)]
    return pl.pallas_call(
        body, name="norm_proj", grid=(T // TM,),
        in_specs=[rows(D_MODEL), _const_spec((1, D_MODEL)), _const_spec((D_MODEL, IN_WIDTH)), _const_spec((1, 512)),
                  _const_spec((1, 128)), _const_spec((512, 512)), rows(128), rows(128), rows(128)],
        out_specs=[rows(w) for w, _ in outs],
        out_shape=[jax.ShapeDtypeStruct((T, w), dt) for w, dt in outs],
        compiler_params=_params(1),
    )(x2, ng, w_bf, qg512, kg128, g512, cos, sa, sb)


SWA_Q = 512
SWA_SUB = SWA_Q // BLOCK


def _swa_scores(q, kst, mask, sink_ref, pair):
    s_all = _dot_nt(q, kst)
    probs, stats = [], []
    for hh in range(2):
        sink = sink_ref[pair * 2 + hh]
        s = jnp.where(mask, s_all[:, hh * 256:(hh + 1) * 256], NEG)
        m = jnp.maximum(jnp.max(s, axis=1, keepdims=True), sink)
        pe = jnp.exp(s - m)
        inv = 1.0 / (jnp.sum(pe, axis=1, keepdims=True) + jnp.exp(sink - m))
        probs.append(pe * inv)
        stats.append(jnp.exp(sink - m) * inv)
    return probs, stats


def _swa_mask(has_prev):
    r = lax.broadcasted_iota(jnp.int32, (128, 256), 0)
    c = lax.broadcasted_iota(jnp.int32, (128, 256), 1)
    band = (c > r) & (c <= r + 128)
    return band if has_prev is True else band & ((c >= 128) | has_prev)


def _swa_keys(prev_ref, main_ref, s, kv, m0b):
    cols = slice(kv * 128, (kv + 1) * 128)
    prev = prev_ref[:, cols] if s == 0 else main_ref[(s - 1) * 128:s * 128, cols]
    kc = jnp.concatenate([prev, main_ref[s * 128:(s + 1) * 128, cols]], axis=0)
    k0 = kc * m0b
    return jnp.concatenate([k0, kc - k0], axis=0)


def _swa_fwd(sinks, qrot, k2, v2, nbatch, seq):
    ni = seq // SWA_Q
    T = nbatch * seq

    def body(sink_ref, q_ref, kp_ref, km_ref, vp_ref, vm_ref, o_ref):
        i = pl.program_id(1)
        m0b = _lane_mask(256, BF16)
        for s in range(SWA_SUB):
            mask = _swa_mask(True if s else i > 0)
            rows = slice(s * 128, (s + 1) * 128)
            for kv in range(2):
                kst = _swa_keys(kp_ref, km_ref, s, kv, m0b)
                vst = _swa_keys(vp_ref, vm_ref, s, kv, m0b)
                for pr in range(2):
                    pair = kv * 2 + pr
                    cols = slice(pair * 128, (pair + 1) * 128)
                    probs, _ = _swa_scores(q_ref[rows, cols], kst, mask, sink_ref, pair)
                    o_ref[rows, cols] = _dot(jnp.concatenate(probs, axis=1).astype(BF16), vst)

    main = lambda b, i: (b * ni + i, 0)
    prev = lambda b, i: ((b * ni + i) * SWA_SUB - jnp.where(i > 0, 1, 0), 0)
    return pl.pallas_call(
        body, name="swa_fwd", grid=(nbatch, ni),
        in_specs=[pl.BlockSpec(memory_space=pltpu.SMEM), pl.BlockSpec((SWA_Q, 512), main),
                  pl.BlockSpec((128, 256), prev), pl.BlockSpec((SWA_Q, 256), main),
                  pl.BlockSpec((128, 256), prev), pl.BlockSpec((SWA_Q, 256), main)],
        out_specs=pl.BlockSpec((SWA_Q, 512), main),
        out_shape=jax.ShapeDtypeStruct((T, 512), F32),
        compiler_params=_params(2),
    )(sinks, qrot, k2, k2, v2, v2)


def _swa_bwd(sinks, qrot, k2, v2, doa, nbatch, seq):
    ni = seq // SWA_Q
    T = nbatch * seq

    def body(sink_ref, q_ref, kp_ref, km_ref, vp_ref, vm_ref, do_ref, dq_ref, dk_ref, dv_ref, ds_ref, dkc, dvc):
        b, i = pl.program_id(0), pl.program_id(1)

        @pl.when((b == 0) & (i == 0))
        def _():
            ds_ref[...] = jnp.zeros_like(ds_ref)

        @pl.when(i == 0)
        def _():
            dkc[...] = jnp.zeros_like(dkc)
            dvc[...] = jnp.zeros_like(dvc)

        @pl.when(i < ni)
        def _():
            m0b = _lane_mask(256, BF16)
            m0 = _lane_mask(128) > 0.5
            for kv in range(2):
                kcols = slice(kv * 128, (kv + 1) * 128)
                dk_own, dv_own = dkc[:, kcols], dvc[:, kcols]
                for s in range(SWA_SUB):
                    mask = _swa_mask(True if s else i > 0)
                    rows = slice(s * 128, (s + 1) * 128)
                    kst = _swa_keys(kp_ref, km_ref, s, kv, m0b)
                    vst = _swa_keys(vp_ref, vm_ref, s, kv, m0b)
                    dkst = jnp.zeros((512, 128), F32)
                    dvst = jnp.zeros((512, 128), F32)
                    for pr in range(2):
                        pair = kv * 2 + pr
                        cols = slice(pair * 128, (pair + 1) * 128)
                        q, do = q_ref[rows, cols], do_ref[rows, cols]
                        probs, psink = _swa_scores(q, kst, mask, sink_ref, pair)
                        dp_all = _dot_nt(do, vst)
                        ds_parts = []
                        for hh in range(2):
                            dp = dp_all[:, hh * 256:(hh + 1) * 256]
                            delta = jnp.sum(probs[hh] * dp, axis=1, keepdims=True)
                            ds_parts.append(probs[hh] * (dp - delta))
                            h = pair * 2 + hh
                            ds_ref[h:h + 1, :] = ds_ref[h:h + 1, :] - jnp.sum(psink[hh] * delta)
                        ds_all = jnp.concatenate(ds_parts, axis=1).astype(BF16)
                        p_all = jnp.concatenate(probs, axis=1).astype(BF16)
                        dq_ref[rows, cols] = _dot(ds_all, kst) * SCALE
                        dkst = dkst + _dot_tn(ds_all, q)
                        dvst = dvst + _dot_tn(p_all, do)
                    dk_ref[rows, kcols] = dk_own + jnp.where(m0, dkst[0:128], dkst[256:384])
                    dv_ref[rows, kcols] = dv_own + jnp.where(m0, dvst[0:128], dvst[256:384])
                    dk_own = jnp.where(m0, dkst[128:256], dkst[384:512])
                    dv_own = jnp.where(m0, dvst[128:256], dvst[384:512])
                dkc[:, kcols] = dk_own
                dvc[:, kcols] = dv_own

        @pl.when(i == ni)
        def _():
            dk_ref[...] = jnp.zeros_like(dk_ref)
            dv_ref[...] = jnp.zeros_like(dv_ref)
            dk_ref[0:128, :] = dkc[...]
            dv_ref[0:128, :] = dvc[...]

    main = lambda b, i: (b * ni + jnp.minimum(i, ni - 1), 0)
    prev = lambda b, i: ((b * ni + jnp.minimum(i, ni - 1)) * SWA_SUB - jnp.where(i > 0, 1, 0), 0)
    shifted = lambda b, i: (b * (ni + 1) + i, 0)
    tpad = nbatch * (ni + 1) * SWA_Q
    return pl.pallas_call(
        body, name="swa_bwd", grid=(nbatch, ni + 1),
        in_specs=[pl.BlockSpec(memory_space=pltpu.SMEM), pl.BlockSpec((SWA_Q, 512), main),
                  pl.BlockSpec((128, 256), prev), pl.BlockSpec((SWA_Q, 256), main),
                  pl.BlockSpec((128, 256), prev), pl.BlockSpec((SWA_Q, 256), main),
                  pl.BlockSpec((SWA_Q, 512), main)],
        out_specs=[pl.BlockSpec((SWA_Q, 512), main), pl.BlockSpec((SWA_Q, 256), shifted),
                   pl.BlockSpec((SWA_Q, 256), shifted), _const_spec((8, 128))],
        out_shape=[jax.ShapeDtypeStruct((T, 512), F32), jax.ShapeDtypeStruct((tpad, 256), F32),
                   jax.ShapeDtypeStruct((tpad, 256), F32), jax.ShapeDtypeStruct((8, 128), F32)],
        scratch_shapes=[pltpu.VMEM((128, 256), F32), pltpu.VMEM((128, 256), F32)],
        compiler_params=_params(2),
    )(sinks, qrot, k2, k2, v2, v2, doa)


def _unshift(dkpad, nbatch, seq):
    return dkpad.reshape(nbatch, seq + SWA_Q, 256)[:, BLOCK:BLOCK + seq].reshape(nbatch * seq, 256)


SB_T = 256


def _sb_masks():
    r = lax.broadcasted_iota(jnp.int32, (SB_T, 2 * SB_T), 0)
    c = lax.broadcasted_iota(jnp.int32, (SB_T, 2 * SB_T), 1)
    return (c & (SB_T - 1)) < r


def _sb_logits(q, kst):
    z = _dot_nt(q, kst)
    sign = jnp.uint32(0x80000000)
    neg_abs = lax.bitcast_convert_type(lax.bitcast_convert_type(z, jnp.uint32) | sign, F32)
    return z, -(jnp.maximum(z, 0.0) + jnp.log(1.0 + jnp.exp(neg_abs)))


SB_NP = 2


def _pair_rows(ref, j, cols, m0b):
    kj = ref[pl.ds(pl.multiple_of(j * SB_T, SB_T), SB_T), cols]
    k0 = kj * m0b
    return jnp.concatenate([k0, kj - k0], axis=0)


def _bcast2(c0, c1):
    return jnp.concatenate([jnp.broadcast_to(c0, (SB_T, SB_T)), jnp.broadcast_to(c1, (SB_T, SB_T))], axis=1)


def _rowsum2(x):
    return jnp.sum(x[:, 0:SB_T], axis=1, keepdims=True), jnp.sum(x[:, SB_T:2 * SB_T], axis=1, keepdims=True)


def _scan2(x, tri2):
    outs = []
    for h in range(2):
        xh = x[:, h * SB_T:(h + 1) * SB_T]
        hi = xh.astype(BF16)
        lo = (xh - hi.astype(F32)).astype(BF16)
        outs.append(_dot(jnp.concatenate([hi, lo], axis=1), tri2))
    return jnp.concatenate(outs, axis=1)


def _sb_fwd(qb, kb, vb, ublk, nbatch, seq):
    nb = seq // SB_T
    T = nbatch * seq

    def body(q_ref, k_ref, v_ref, u_ref, o_ref, ctab_ref):
        i = pl.program_id(2)
        m0b = _lane_mask(SB_T, BF16)
        u = u_ref[...]
        lane = lax.broadcasted_iota(jnp.int32, (SB_T, 128), 1)
        ctab_ref[...] = jnp.zeros_like(ctab_ref)

        def tile(j, carries, diag):
            out = []
            for pp, (c0, c1, acc) in enumerate(carries):
                cols = slice(pp * 128, (pp + 1) * 128)
                if not diag:
                    ctab_ref[:, cols] = jnp.where(lane == j, c0, jnp.where(lane == nb + j, c1, ctab_ref[:, cols]))
                kst = _pair_rows(k_ref, j, cols, m0b)
                vst = _pair_rows(v_ref, j, cols, m0b)
                z, lb = _sb_logits(q_ref[:, cols], kst)
                if diag:
                    mask = _sb_masks()
                    lb = jnp.where(mask, lb, 0.0)
                incl = _scan2(lb, u)
                w = jnp.exp(z + incl + _bcast2(c0, c1))
                if diag:
                    w = jnp.where(mask, w, 0.0)
                out.append((c0 + incl[:, 0:1], c1 + incl[:, SB_T:SB_T + 1], acc + _dot(w.astype(BF16), vst)))
            return tuple(out)

        zc = jnp.zeros((SB_T, 1), F32)
        carries = tile(i, ((zc, zc, jnp.zeros((SB_T, 128), F32)),) * SB_NP, True)
        carries = lax.fori_loop(0, i, lambda jj, cr: tile(i - 1 - jj, cr, False), carries)
        for pp in range(SB_NP):
            o_ref[:, pp * 128:(pp + 1) * 128] = carries[pp][2]

    wide = 128 * SB_NP
    blk = lambda b, g, i: (b * nb + i, g)
    full = lambda b, g, i: (b, g)
    return pl.pallas_call(
        body, name="sb_fwd", grid=(nbatch, 4 // SB_NP, nb),
        in_specs=[pl.BlockSpec((SB_T, wide), blk), pl.BlockSpec((seq, wide), full), pl.BlockSpec((seq, wide), full),
                  _const_spec((2 * SB_T, SB_T))],
        out_specs=[pl.BlockSpec((SB_T, wide), blk), pl.BlockSpec((SB_T, wide), blk)],
        out_shape=[jax.ShapeDtypeStruct((T, 512), F32), jax.ShapeDtypeStruct((T, 512), F32)],
        compiler_params=_params(3),
    )(qb, kb, vb, ublk)


def _sb_bwd(qb, kb, vb, dob, ctab, ublk, pblk, nbatch, seq):
    nb = seq // SB_T
    T = nbatch * seq

    def body(q_ref, k_ref, v_ref, do_ref, ctab_ref, u_ref, up_ref, dq_ref, dk_ref, dv_ref):
        i = pl.program_id(2)

        @pl.when(i == 0)
        def _():
            dk_ref[...] = jnp.zeros_like(dk_ref)
            dv_ref[...] = jnp.zeros_like(dv_ref)

        m0b = _lane_mask(SB_T, BF16)
        m0 = _lane_mask(SB_T) > 0.5
        u, up = u_ref[...], up_ref[...]
        lane = lax.broadcasted_iota(jnp.int32, (SB_T, 128), 1)

        def tile(j, carries, diag):
            out = []
            for pp, (s0, s1, dq) in enumerate(carries):
                cols = slice(pp * 128, (pp + 1) * 128)
                q, do = q_ref[:, cols], do_ref[:, cols]
                kst = _pair_rows(k_ref, j, cols, m0b)
                vst = _pair_rows(v_ref, j, cols, m0b)
                z, lb = _sb_logits(q, kst)
                one_minus_beta = jnp.exp(lb)
                if diag:
                    mask = _sb_masks()
                    lb = jnp.where(mask, lb, 0.0)
                    w = jnp.where(mask, jnp.exp(z + _scan2(lb, u)), 0.0)
                else:
                    ct = ctab_ref[:, cols]
                    c0 = jnp.sum(jnp.where(lane == j, ct, 0.0), axis=1, keepdims=True)
                    c1 = jnp.sum(jnp.where(lane == nb + j, ct, 0.0), axis=1, keepdims=True)
                    w = jnp.exp(z + _scan2(lb, u) + _bcast2(c0, c1))
                e = _dot_nt(do, vst) * w
                dlb = _bcast2(s0, s1) + _scan2(e, up)
                dz = (e + dlb) * one_minus_beta - dlb
                if diag:
                    dz = jnp.where(mask, dz, 0.0)
                dzb = dz.astype(BF16)
                dkst = _dot_tn(dzb, q)
                dvst = _dot_tn(w.astype(BF16), do)
                rows = pl.ds(pl.multiple_of(j * SB_T, SB_T), SB_T)
                dk_ref[rows, cols] = dk_ref[rows, cols] + jnp.where(m0, dkst[0:SB_T], dkst[SB_T:2 * SB_T])
                dv_ref[rows, cols] = dv_ref[rows, cols] + jnp.where(m0, dvst[0:SB_T], dvst[SB_T:2 * SB_T])
                x0, x1 = _rowsum2(e)
                out.append((s0 + x0, s1 + x1, dq + _dot(dzb, kst)))
            return tuple(out)

        zc = jnp.zeros((SB_T, 1), F32)
        carries = lax.fori_loop(0, i, lambda j, cr: tile(j, cr, False), ((zc, zc, jnp.zeros((SB_T, 128), F32)),) * SB_NP)
        carries = tile(i, carries, True)
        for pp in range(SB_NP):
            dq_ref[:, pp * 128:(pp + 1) * 128] = carries[pp][2] * SCALE

    wide = 128 * SB_NP
    blk = lambda b, g, i: (b * nb + i, g)
    full = lambda b, g, i: (b, g)
    return pl.pallas_call(
        body, name="sb_bwd", grid=(nbatch, 4 // SB_NP, nb),
        in_specs=[pl.BlockSpec((SB_T, wide), blk), pl.BlockSpec((seq, wide), full), pl.BlockSpec((seq, wide), full),
                  pl.BlockSpec((SB_T, wide), blk), pl.BlockSpec((SB_T, wide), blk),
                  _const_spec((2 * SB_T, SB_T)), _const_spec((2 * SB_T, SB_T))],
        out_specs=[pl.BlockSpec((SB_T, wide), blk), pl.BlockSpec((seq, wide), full), pl.BlockSpec((seq, wide), full)],
        out_shape=[jax.ShapeDtypeStruct((T, 512), F32)] * 3,
        compiler_params=_params(3),
    )(qb, kb, vb, dob, ctab, ublk, pblk)


def _sigmoid(g):
    return 1.0 / (1.0 + jnp.exp(-g))


def _out_proj(oa, ob, ga, gb, x2, tgt, wout_bf):
    T = x2.shape[0]

    def body(oa_ref, ob_ref, ga_ref, gb_ref, x_ref, t_ref, w_ref, y_ref, dout_ref, loss_ref):
        @pl.when(pl.program_id(0) == 0)
        def _():
            loss_ref[...] = jnp.zeros_like(loss_ref)

        ga, gb = ga_ref[...], gb_ref[...]
        ya = (oa_ref[...] * (ga * _sigmoid(ga))).astype(BF16)
        yb = (ob_ref[...] * (gb * _sigmoid(gb))).astype(BF16)
        y_ref[:, 0:512] = ya
        y_ref[:, 512:1024] = yb
        out = x_ref[...] + _dot(ya, w_ref[0:512, :]) + _dot(yb, w_ref[512:1024, :])
        diff = out - t_ref[...]
        dout_ref[...] = diff * (1.0 / D_MODEL)
        loss_ref[...] = loss_ref[...] + jnp.sum(diff * diff) * (0.5 / D_MODEL)

    rows = lambda w: pl.BlockSpec((TM, w), lambda i: (i, 0))
    return pl.pallas_call(
        body, name="out_proj", grid=(T // TM,),
        in_specs=[rows(512), rows(512), rows(512), rows(512), rows(D_MODEL), rows(D_MODEL),
                  _const_spec((D_MODEL, D_MODEL))],
        out_specs=[rows(D_MODEL), rows(D_MODEL), _const_spec((8, 128))],
        out_shape=[jax.ShapeDtypeStruct((T, D_MODEL), BF16), jax.ShapeDtypeStruct((T, D_MODEL), F32),
                   jax.ShapeDtypeStruct((8, 128), F32)],
        compiler_params=_params(1),
    )(oa, ob, ga, gb, x2, tgt, wout_bf)


def _out_proj_bwd(dout, y, oa, ob, ga, gb, wout_bf):
    T = dout.shape[0]

    def body(dout_ref, y_ref, oa_ref, ob_ref, ga_ref, gb_ref, w_ref, doa_ref, dob_ref, dga_ref, dgb_ref, dw_ref):
        @pl.when(pl.program_id(0) == 0)
        def _():
            dw_ref[...] = jnp.zeros_like(dw_ref)

        db = dout_ref[...].astype(BF16)
        dw_ref[...] = dw_ref[...] + _dot_tn(y_ref[...], db)
        for o_ref, g_ref, do_ref, dg_ref, lo in ((oa_ref, ga_ref, doa_ref, dga_ref, 0), (ob_ref, gb_ref, dob_ref, dgb_ref, 512)):
            dy = _dot_nt(db, w_ref[lo:lo + 512, :])
            g = g_ref[...]
            sg = _sigmoid(g)
            do_ref[...] = (dy * (g * sg)).astype(BF16)
            dg_ref[...] = (dy * o_ref[...] * (sg * (1.0 + g * (1.0 - sg)))).astype(BF16)

    rows = lambda w: pl.BlockSpec((TM, w), lambda i: (i, 0))
    return pl.pallas_call(
        body, name="out_proj_bwd", grid=(T // TM,),
        in_specs=[rows(D_MODEL), rows(D_MODEL), rows(512), rows(512), rows(512), rows(512),
                  _const_spec((D_MODEL, D_MODEL))],
        out_specs=[rows(512)] * 4 + [_const_spec((D_MODEL, D_MODEL))],
        out_shape=[jax.ShapeDtypeStruct((T, 512), BF16)] * 4 + [jax.ShapeDtypeStruct((D_MODEL, D_MODEL), F32)],
        compiler_params=_params(1),
    )(dout, y, oa, ob, ga, gb, wout_bf)


def _qk_grad(qraw, kraw, dqrot, dk2, dv2, qg512, kg128, g512, cos, sa, sb):
    T = qraw.shape[0]

    def body(qraw_ref, kraw_ref, dqrot_ref, dk2_ref, dv2_ref, qg_ref, kg_ref, g_ref, cos_ref, sa_ref, sb_ref,
             dqa_ref, dkv_ref, dqg_ref, dkg_ref):
        @pl.when(pl.program_id(0) == 0)
        def _():
            dqg_ref[...] = jnp.zeros_like(dqg_ref)
            dkg_ref[...] = jnp.zeros_like(dkg_ref)

        cosv, sav, sbv = cos_ref[...], sa_ref[...], sb_ref[...]
        m0 = _lane_mask(TM) > 0.5

        def head_norm_bwd(raw, dn_rot, gmat, gain):
            r = lax.rsqrt(_dot_hl(raw * raw, gmat) + EPS)
            n = raw * r
            dn = dn_rot * gain
            return r * (dn - n * _dot_hl(dn * n, gmat)), jnp.sum(dn_rot * n, axis=0, keepdims=True)

        def fold(ref):
            a, b = ref[:, 0:128], ref[:, 128:256]
            return jnp.where(m0, a + pltpu.roll(a, 64, 1), b + pltpu.roll(b, 64, 1))

        dqn = jnp.concatenate([_rope_bwd(dqrot_ref[:, s * 128:(s + 1) * 128], cosv, sav, sbv) for s in range(4)], axis=1)
        dqa, dqg = head_norm_bwd(qraw_ref[...], dqn, g_ref[...], qg_ref[...])
        dka, dkg = head_norm_bwd(kraw_ref[...], _rope_bwd(fold(dk2_ref), cosv, sav, sbv), g_ref[0:128, 0:128], kg_ref[...])
        dqa_ref[...] = dqa.astype(BF16)
        dkv_ref[:, 0:128] = dka.astype(BF16)
        dkv_ref[:, 128:256] = fold(dv2_ref).astype(BF16)
        dqg_ref[...] = dqg_ref[...] + dqg
        dkg_ref[...] = dkg_ref[...] + dkg

    rows = lambda w: pl.BlockSpec((TM, w), lambda i: (i, 0))
    return pl.pallas_call(
        body, name="qk_grad", grid=(T // TM,),
        in_specs=[rows(512), rows(128), rows(512), rows(256), rows(256), _const_spec((1, 512)), _const_spec((1, 128)),
                  _const_spec((512, 512)), rows(128), rows(128), rows(128)],
        out_specs=[rows(512), rows(256), _const_spec((1, 512)), _const_spec((1, 128))],
        out_shape=[jax.ShapeDtypeStruct((T, 512), BF16), jax.ShapeDtypeStruct((T, 256), BF16),
                   jax.ShapeDtypeStruct((1, 512), F32), jax.ShapeDtypeStruct((1, 128), F32)],
        compiler_params=_params(1),
    )(qraw, kraw, dqrot, dk2, dv2, qg512, kg128, g512, cos, sa, sb)


_PIECES = ((QA, 512), (KA, 256), (GA, 512), (QB, 512), (KB, 512), (VB, 512), (GB, 512))


def _w_in_grad(h, pieces):
    T = h.shape[0]

    def body(h_ref, *refs):
        dw_ref = refs[-1]

        @pl.when(pl.program_id(0) == 0)
        def _():
            dw_ref[...] = jnp.zeros_like(dw_ref)

        hb = h_ref[...]
        for (lo, width), p_ref in zip(_PIECES, refs[:-1]):
            dw_ref[:, lo:lo + width] = dw_ref[:, lo:lo + width] + _dot_tn(hb, p_ref[...].astype(BF16))

    rows = lambda w: pl.BlockSpec((TM, w), lambda i: (i, 0))
    return pl.pallas_call(
        body, name="w_in_grad", grid=(T // TM,),
        in_specs=[rows(D_MODEL)] + [rows(w) for _, w in _PIECES],
        out_specs=_const_spec((D_MODEL, IN_WIDTH)),
        out_shape=jax.ShapeDtypeStruct((D_MODEL, IN_WIDTH), F32),
        compiler_params=_params(1),
    )(h, *pieces)


def _x_grad(x2, dout, pieces, w_bf, ng):
    T = x2.shape[0]
    npc = len(_PIECES)

    def body(x_ref, dout_ref, *refs):
        w_ref, ng_ref, gx_ref, dng_ref = refs[npc:]

        @pl.when(pl.program_id(0) == 0)
        def _():
            dng_ref[...] = jnp.zeros_like(dng_ref)

        dh = jnp.zeros((TM, D_MODEL), F32)
        for (lo, width), p_ref in zip(_PIECES, refs[:npc]):
            dh = dh + _dot_nt(p_ref[...].astype(BF16), w_ref[:, lo:lo + width])
        xb = x_ref[...]
        r = lax.rsqrt(jnp.mean(xb * xb, axis=-1, keepdims=True) + EPS)
        n = xb * r
        dn = dh * ng_ref[...]
        gx_ref[...] = dout_ref[...] + r * (dn - n * jnp.mean(dn * n, axis=-1, keepdims=True))
        dng_ref[...] = dng_ref[...] + jnp.sum(dh * n, axis=0, keepdims=True)

    rows = lambda w: pl.BlockSpec((TM, w), lambda i: (i, 0))
    return pl.pallas_call(
        body, name="x_grad", grid=(T // TM,),
        in_specs=[rows(D_MODEL), rows(D_MODEL)] + [rows(w) for _, w in _PIECES]
        + [_const_spec((D_MODEL, IN_WIDTH)), _const_spec((1, D_MODEL))],
        out_specs=[rows(D_MODEL), _const_spec((1, D_MODEL))],
        out_shape=[jax.ShapeDtypeStruct((T, D_MODEL), F32), jax.ShapeDtypeStruct((1, D_MODEL), F32)],
        compiler_params=_params(1),
    )(x2, dout, *pieces, w_bf, ng)


HBM = pl.BlockSpec(memory_space=pl.ANY)


def _place():
    x, y, c = lax.axis_index("x"), lax.axis_index("y"), lax.axis_index("c")
    chips = [(1 - x, y), (x, 1 - y), (1 - x, 1 - y)]
    return x, y, c, chips


def _remote(src, dst, sems, k, to):
    return pltpu.make_async_remote_copy(src_ref=src, dst_ref=dst, send_sem=sems[0].at[k], recv_sem=sems[1].at[k],
                                        device_id=to, device_id_type=MESH)


def _gather_weights(win_b, wout_b):
    hi, ho = D_MODEL // 2, SHARD_OUT // 2

    def body(win_ref, wout_ref, ain_ref, aout_ref, ssem, rsem):
        x, y, c, chips = _place()
        me, sib = 2 * x + y, (x, y, 1 - c)
        sems = (ssem, rsem)
        rin, rout = pl.ds(c * hi, hi), pl.ds(c * ho, ho)
        oin, oout = pl.ds((1 - c) * hi, hi), pl.ds((1 - c) * ho, ho)
        sent = []
        for j, (cx, cy) in enumerate(chips):
            sent.append(_remote(win_ref.at[rin], ain_ref.at[me, rin], sems, 2 * j, (cx, cy, c)))
            sent.append(_remote(wout_ref.at[rout], aout_ref.at[me, rout], sems, 2 * j + 1, (cx, cy, c)))
        for cp in sent:
            cp.start()
        for j, (cx, cy) in enumerate(chips):
            k = 2 * cx + cy
            _remote(win_ref.at[rin], ain_ref.at[k, rin], sems, 2 * j, sib).wait_recv()
            fwd = _remote(ain_ref.at[k, rin], ain_ref.at[k, rin], sems, 6 + 2 * j, sib)
            fwd.start()
            sent.append(fwd)
            _remote(wout_ref.at[rout], aout_ref.at[k, rout], sems, 2 * j + 1, sib).wait_recv()
            fwd = _remote(aout_ref.at[k, rout], aout_ref.at[k, rout], sems, 7 + 2 * j, sib)
            fwd.start()
            sent.append(fwd)
        for j, (cx, cy) in enumerate(chips):
            k = 2 * cx + cy
            _remote(win_ref.at[oin], ain_ref.at[k, oin], sems, 6 + 2 * j, sib).wait_recv()
            _remote(wout_ref.at[oout], aout_ref.at[k, oout], sems, 7 + 2 * j, sib).wait_recv()
        for cp in sent:
            cp.wait_send()

    ain, aout = pl.pallas_call(
        body, name="gather_weights", in_specs=[HBM, HBM], out_specs=[HBM, HBM],
        out_shape=[jax.ShapeDtypeStruct((4, D_MODEL, SHARD_IN), BF16), jax.ShapeDtypeStruct((4, SHARD_OUT, D_MODEL), BF16)],
        scratch_shapes=[pltpu.SemaphoreType.DMA((12,)), pltpu.SemaphoreType.DMA((12,))],
    )(win_b, wout_b)
    me = 2 * lax.axis_index("x") + lax.axis_index("y")
    w_bf = jnp.concatenate([jnp.where(me == k, win_b, ain[k]) for k in range(4)], axis=1)
    wout_bf = jnp.concatenate([jnp.where(me == k, wout_b, aout[k]) for k in range(4)], axis=0)
    return w_bf, wout_bf


def _pair_exchange(gin4, gout4, small):
    hi, ho = D_MODEL // 2, SHARD_OUT // 2

    def body(gin_ref, gout_ref, small_ref, rin_ref, rout_ref, sall_ref, ssem, rsem, lsem):
        x, y, c, _ = _place()
        sems = (ssem, rsem)
        sib = (x, y, 1 - c)
        me = 4 * x + 2 * y + c
        own = pltpu.make_async_copy(small_ref, sall_ref.at[me], lsem.at[0])
        own.start()
        sent = [_remote(gin_ref.at[:, pl.ds((1 - c) * hi, hi), :], rin_ref, sems, 0, sib),
                _remote(gout_ref.at[:, pl.ds((1 - c) * ho, ho), :], rout_ref, sems, 1, sib)]
        flips = [(fx, fy, fc) for fx in (0, 1) for fy in (0, 1) for fc in (0, 1)][1:]
        for n, (fx, fy, fc) in enumerate(flips):
            peer = (x ^ fx, y ^ fy, c ^ fc)
            sent.append(_remote(small_ref, sall_ref.at[me], sems, 2 + n, peer))
        for cp in sent:
            cp.start()
        _remote(gin_ref.at[:, pl.ds(c * hi, hi), :], rin_ref, sems, 0, sib).wait_recv()
        _remote(gout_ref.at[:, pl.ds(c * ho, ho), :], rout_ref, sems, 1, sib).wait_recv()
        for n, (fx, fy, fc) in enumerate(flips):
            src = 4 * (x ^ fx) + 2 * (y ^ fy) + (c ^ fc)
            _remote(small_ref, sall_ref.at[src], sems, 2 + n, sib).wait_recv()
        for cp in sent:
            cp.wait_send()
        own.wait()

    return pl.pallas_call(
        body, name="pair_exchange", in_specs=[HBM, HBM, HBM], out_specs=[HBM, HBM, HBM],
        out_shape=[jax.ShapeDtypeStruct((4, hi, SHARD_IN), gin4.dtype), jax.ShapeDtypeStruct((4, ho, D_MODEL), gout4.dtype),
                   jax.ShapeDtypeStruct((8,) + small.shape, F32)],
        scratch_shapes=[pltpu.SemaphoreType.DMA((9,)), pltpu.SemaphoreType.DMA((9,)), pltpu.SemaphoreType.DMA((1,))],
    )(gin4, gout4, small)


def _chip_exchange(cin4, cout4):
    def body(cin_ref, cout_ref, rin_ref, rout_ref, ssem, rsem):
        x, y, c, chips = _place()
        sems = (ssem, rsem)
        sent = []
        for j, (cx, cy) in enumerate(chips):
            k = 2 * cx + cy
            sent.append(_remote(cin_ref.at[k], rin_ref.at[j], sems, 2 * j, (cx, cy, c)))
            sent.append(_remote(cout_ref.at[k], rout_ref.at[j], sems, 2 * j + 1, (cx, cy, c)))
        for cp in sent:
            cp.start()
        for j, (cx, cy) in enumerate(chips):
            _remote(cin_ref.at[0], rin_ref.at[j], sems, 2 * j, (cx, cy, c)).wait_recv()
            _remote(cout_ref.at[0], rout_ref.at[j], sems, 2 * j + 1, (cx, cy, c)).wait_recv()
        for cp in sent:
            cp.wait_send()

    return pl.pallas_call(
        body, name="chip_exchange", in_specs=[HBM, HBM], out_specs=[HBM, HBM],
        out_shape=[jax.ShapeDtypeStruct((3,) + cin4.shape[1:], cin4.dtype),
                   jax.ShapeDtypeStruct((3,) + cout4.shape[1:], cout4.dtype)],
        scratch_shapes=[pltpu.SemaphoreType.DMA((6,)), pltpu.SemaphoreType.DMA((6,))],
    )(cin4, cout4)


def _share_halves(hin, hout):
    def body(hin_ref, hout_ref, oin_ref, oout_ref, ssem, rsem):
        x, y, c, _ = _place()
        sems = (ssem, rsem)
        sib = (x, y, 1 - c)
        sent = [_remote(hin_ref, oin_ref, sems, 0, sib), _remote(hout_ref, oout_ref, sems, 1, sib)]
        for cp in sent:
            cp.start()
        for cp in sent:
            cp.wait()

    return pl.pallas_call(
        body, name="share_halves", in_specs=[HBM, HBM], out_specs=[HBM, HBM],
        out_shape=[jax.ShapeDtypeStruct(hin.shape, F32), jax.ShapeDtypeStruct(hout.shape, F32)],
        scratch_shapes=[pltpu.SemaphoreType.DMA((2,)), pltpu.SemaphoreType.DMA((2,))],
    )(hin, hout)


def _add_half(cidx, full4, recv4, half):
    nblk, _, width = recv4.shape

    def body(c_ref, a_ref, b_ref, o_ref):
        o_ref[...] = (a_ref[...] + b_ref[...].astype(F32)).astype(BF16)

    return pl.pallas_call(
        body, name=f"add_half_{width}",
        grid_spec=pltpu.PrefetchScalarGridSpec(
            num_scalar_prefetch=1, grid=(nblk,),
            in_specs=[pl.BlockSpec((1, half, width), lambda k, c: (k, c[0], 0)),
                      pl.BlockSpec((1, half, width), lambda k, c: (k, 0, 0))],
            out_specs=pl.BlockSpec((1, half, width), lambda k, c: (k, 0, 0))),
        out_shape=jax.ShapeDtypeStruct(recv4.shape, BF16),
        compiler_params=_params(1),
    )(cidx, full4, recv4)


def _sum_chips(chip, own4, recv3):
    _, rows, width = recv3.shape
    rb = min(rows, 128)

    def body(k_ref, a_ref, r_ref, o_ref):
        acc = a_ref[0].astype(F32)
        for s in range(3):
            acc = acc + r_ref[s].astype(F32)
        o_ref[...] = acc

    return pl.pallas_call(
        body, name=f"sum_chips_{width}",
        grid_spec=pltpu.PrefetchScalarGridSpec(
            num_scalar_prefetch=1, grid=(rows // rb,),
            in_specs=[pl.BlockSpec((1, rb, width), lambda i, k: (k[0], i, 0)),
                      pl.BlockSpec((3, rb, width), lambda i, k: (0, i, 0))],
            out_specs=pl.BlockSpec((rb, width), lambda i, k: (i, 0))),
        out_shape=jax.ShapeDtypeStruct((rows, width), F32),
        compiler_params=_params(1),
    )(chip, own4, recv3)


def _sum_slots(r4):
    n, rows, width = r4.shape

    def body(r_ref, o_ref):
        acc = r_ref[0]
        for s in range(1, n):
            acc = acc + r_ref[s]
        o_ref[...] = acc

    return pl.pallas_call(
        body, name=f"sum_slots_{n}_{rows}_{width}", grid=(1,),
        in_specs=[pl.BlockSpec((n, rows, width), lambda i: (0, 0, 0))],
        out_specs=pl.BlockSpec((rows, width), lambda i: (0, 0)),
        out_shape=jax.ShapeDtypeStruct((rows, width), F32),
        compiler_params=_params(1),
    )(r4)


def _adam_math(w, g, m, v):
    c1 = 1.0 - ADAM_B1 ** ADAM_STEP
    c2 = 1.0 - ADAM_B2 ** ADAM_STEP
    nm = ADAM_B1 * m + (1.0 - ADAM_B1) * g
    nv = ADAM_B2 * v + (1.0 - ADAM_B2) * (g * g)
    return -ADAM_LR * ((nm / c1) / (jnp.sqrt(nv / c2) + ADAM_EPS) + ADAM_WD * w), nm, nv


def _adamw(w, g, m, v):
    rows, width = w.shape

    def body(w_ref, g_ref, m_ref, v_ref, d_ref, nm_ref, nv_ref):
        d_ref[...], nm_ref[...], nv_ref[...] = _adam_math(w_ref[...], g_ref[...], m_ref[...], v_ref[...])

    spec = pl.BlockSpec((rows, width), lambda i: (0, 0))
    return pl.pallas_call(
        body, name=f"adamw_{rows}_{width}", grid=(1,),
        in_specs=[spec] * 4, out_specs=[spec] * 3,
        out_shape=[jax.ShapeDtypeStruct((rows, width), F32)] * 3,
        compiler_params=_params(1),
    )(w, g, m, v)


def _adamw_halves(cidx, w, own, recv, m, v):
    rows, width = w.shape
    rb = rows // 4

    def body(c_ref, w_ref, own_ref, recv_ref, m_ref, v_ref, g_ref, d_ref, nm_ref, nv_ref):
        mine = (pl.program_id(0) // 2) == c_ref[0]
        g = jnp.where(mine, own_ref[...], recv_ref[...])
        g_ref[...] = g
        d_ref[...], nm_ref[...], nv_ref[...] = _adam_math(w_ref[...], g, m_ref[...], v_ref[...])

    full = pl.BlockSpec((rb, width), lambda i, c: (i, 0))
    half = pl.BlockSpec((rb, width), lambda i, c: (i % 2, 0))
    return pl.pallas_call(
        body, name=f"adamw_halves_{width}",
        grid_spec=pltpu.PrefetchScalarGridSpec(
            num_scalar_prefetch=1, grid=(4,),
            in_specs=[full, half, half, full, full], out_specs=[full] * 4),
        out_shape=[jax.ShapeDtypeStruct((rows, width), F32)] * 4,
        compiler_params=_params(1),
    )(cidx, w, own, recv, m, v)


def _rope_tables(positions):
    half = HEAD_DIM // 2
    inv_freq = ROPE_THETA ** (-jnp.arange(half, dtype=F32) * 2.0 / HEAD_DIM)
    ang = positions.astype(F32).reshape(-1, 1) * inv_freq
    cos, sin, zero = jnp.cos(ang), jnp.sin(ang), jnp.zeros_like(ang)
    return (jnp.concatenate([cos] * 4, axis=1), jnp.concatenate([-sin, zero] * 2, axis=1),
            jnp.concatenate([zero, sin] * 2, axis=1))


def _constants():
    idx = jnp.arange(512)
    g512 = jnp.where(idx[:, None] // HEAD_DIM == idx[None, :] // HEAD_DIM, 1.0 / HEAD_DIM, 0.0).astype(BF16)
    j = jnp.arange(SB_T)
    ublk = jnp.where(j[:, None] >= j[None, :], 1.0, 0.0).astype(BF16)
    pblk = jnp.where(j[:, None] < j[None, :], 1.0, 0.0).astype(BF16)
    return g512, jnp.concatenate([ublk, ublk], axis=0), jnp.concatenate([pblk, pblk], axis=0)


def _pad_rows(v, width):
    return jnp.pad(v, ((0, 0), (0, width - v.shape[1])))


def _local_grads(x2, tgt, positions, norm_gain, q_norm_gain, k_norm_gain, sinks, w_bf, wout_bf, nbatch, seq):
    cos, sa, sb = _rope_tables(positions)
    g512, ublk, pblk = _constants()
    qg512 = jnp.tile(q_norm_gain, (1, 8))
    kg128 = jnp.tile(k_norm_gain, (1, 2))
    sink1 = sinks.reshape(8)

    h, qraw, kraw, qrot, k2, v2, ga, qb, kb, vb, gb = _norm_proj(x2, norm_gain, w_bf, qg512, kg128, g512, cos, sa, sb)
    oa = _swa_fwd(sink1, qrot, k2, v2, nbatch, seq)
    ob, ctab = _sb_fwd(qb, kb, vb, ublk, nbatch, seq)
    y, dout, loss_acc = _out_proj(oa, ob, ga, gb, x2, tgt, wout_bf)

    doa, dob, dga, dgb, dwout = _out_proj_bwd(dout, y, oa, ob, ga, gb, wout_bf)
    dqrot, dk2, dv2, dsink = _swa_bwd(sink1, qrot, k2, v2, doa, nbatch, seq)
    dqb, dkb, dvb = _sb_bwd(qb, kb, vb, dob, ctab, ublk, pblk, nbatch, seq)
    dk2, dv2 = _unshift(dk2, nbatch, seq), _unshift(dv2, nbatch, seq)
    dqa, dkv, dqg, dkg = _qk_grad(qraw, kraw, dqrot, dk2, dv2, qg512, kg128, g512, cos, sa, sb)
    pieces = (dqa, dkv, dga, dqb, dkb, dvb, dgb)
    dwin = _w_in_grad(h, pieces)
    gx, dng = _x_grad(x2, dout, pieces, w_bf, norm_gain)
    dqg64 = dqg.reshape(8, HEAD_DIM).sum(axis=0, keepdims=True)
    dkg64 = dkg.reshape(2, HEAD_DIM).sum(axis=0, keepdims=True)
    return loss_acc[0, 0], gx, dwin, dwout, dng, dqg64, dkg64, dsink[:, 0].reshape(1, 8)


def _reduce_grads(dwin, dwout, small):
    gin4 = jnp.transpose(dwin.reshape(D_MODEL, 4, SHARD_IN), (1, 0, 2))
    gout4 = dwout.reshape(4, SHARD_OUT, D_MODEL)
    rin, rout, small_all = _pair_exchange(gin4.astype(BF16), gout4.astype(BF16), small)
    cidx = lax.axis_index("c").astype(jnp.int32).reshape(1)
    chip = (2 * lax.axis_index("x") + lax.axis_index("y")).astype(jnp.int32).reshape(1)
    cin4 = _add_half(cidx, gin4, rin, D_MODEL // 2)
    cout4 = _add_half(cidx, gout4, rout, SHARD_OUT // 2)
    rin3, rout3 = _chip_exchange(cin4, cout4)
    own_in, own_out = _sum_chips(chip, cin4, rin3), _sum_chips(chip, cout4, rout3)
    sib_in, sib_out = _share_halves(own_in, own_out)
    return cidx, own_in, sib_in, own_out, sib_out, _sum_slots(small_all)


def kernel(x, positions, norm_gain, w_in, q_norm_gain, k_norm_gain, sinks, w_out, loss_target, m_norm_gain, m_w_in, m_q_norm_gain, m_k_norm_gain, m_sinks, m_w_out, v_norm_gain, v_w_in, v_q_norm_gain, v_k_norm_gain, v_sinks, v_w_out):
    nbatch, seq, _ = x.shape
    T = nbatch * seq
    x2 = x.reshape(T, D_MODEL)
    tgt = loss_target.reshape(T, D_MODEL)

    w_bf, wout_bf = _gather_weights(w_in[0].astype(BF16), w_out[0].astype(BF16))

    loss_local, gx, dwin, dwout, dng, dqg64, dkg64, dsink8 = _local_grads(
        x2, tgt, positions, norm_gain, q_norm_gain, k_norm_gain, sinks, w_bf, wout_bf, nbatch, seq)
    loss = lax.psum(loss_local, ("x", "y", "c"))

    small = jnp.concatenate([dng, _pad_rows(dqg64, D_MODEL), _pad_rows(dkg64, D_MODEL),
                             _pad_rows(dsink8, D_MODEL), jnp.zeros((4, D_MODEL), F32)], axis=0)
    cidx, own_in, sib_in, own_out, sib_out, g_small = _reduce_grads(dwin, dwout, small)

    g_in, d_in, nm_in, nv_in = _adamw_halves(cidx, w_in[0], own_in, sib_in, m_w_in[0], v_w_in[0])
    g_out, d_out, nm_out, nv_out = _adamw_halves(cidx, w_out[0], own_out, sib_out, m_w_out[0], v_w_out[0])
    pack = lambda a, b, c_, d: jnp.concatenate(
        [a, _pad_rows(b, D_MODEL), _pad_rows(c_, D_MODEL), _pad_rows(d, D_MODEL), jnp.zeros((4, D_MODEL), F32)], axis=0)
    w_s = pack(norm_gain, q_norm_gain, k_norm_gain, sinks)
    m_s = pack(m_norm_gain, m_q_norm_gain, m_k_norm_gain, m_sinks)
    v_s = pack(v_norm_gain, v_q_norm_gain, v_k_norm_gain, v_sinks)
    d_s, nm_s, nv_s = _adamw(w_s, g_small, m_s, v_s)
    unpack = lambda a: (a[0:1, :], a[1:2, 0:HEAD_DIM], a[2:3, 0:HEAD_DIM], a[3:4, 0:8])

    g_ng, g_qg, g_kg, g_sk = unpack(g_small)
    d_ng, d_qg, d_kg, d_sk = unpack(d_s)
    m_ng, m_qg, m_kg, m_sk = unpack(nm_s)
    v_ng, v_qg, v_kg, v_sk = unpack(nv_s)
    return (loss, gx.reshape(nbatch, seq, D_MODEL),
            g_ng, g_in[None], g_qg, g_kg, g_sk, g_out[None],
            d_ng, d_in[None], d_qg, d_kg, d_sk, d_out[None],
            m_ng, nm_in[None], m_qg, m_kg, m_sk, nm_out[None],
            v_ng, nv_in[None], v_qg, v_kg, v_sk, nv_out[None])
```

```python
import functools
import math

import jax
import jax.numpy as jnp
from jax import lax
from jax.experimental import pallas as pl
from jax.experimental.pallas import tpu as pltpu

F32 = jnp.float32
BF16 = jnp.bfloat16

D_MODEL = 1024
HEAD_DIM = 64
BLOCK = 128
ROPE_THETA = 10000.0
EPS = 1e-6
QA, KA, VA, GA, QB, KB, VB, GB = 0, 512, 640, 768, 1280, 1792, 2304, 2816
IN_WIDTH = 3328
SHARD_IN = IN_WIDTH // 4
SHARD_OUT = D_MODEL // 4
SCALE = 1.0 / math.sqrt(HEAD_DIM)
NEG = -1e30

ADAM_LR, ADAM_B1, ADAM_B2, ADAM_EPS, ADAM_WD, ADAM_STEP = 0.001, 0.9, 0.999, 1e-08, 0.01, 10

TM = 256
VMEM_LIMIT = 56 * 1024 * 1024
MESH = pl.DeviceIdType.MESH


def _dot(a, b):
    return jnp.dot(a, b, preferred_element_type=F32)


def _dot_nt(a, b):
    return lax.dot_general(a, b, (((1,), (1,)), ((), ())), preferred_element_type=F32)


def _dot_tn(a, b):
    return lax.dot_general(a, b, (((0,), (0,)), ((), ())), preferred_element_type=F32)


def _dot_hl(a, m):
    hi = a.astype(BF16)
    lo = (a - hi.astype(F32)).astype(BF16)
    return _dot(hi, m) + _dot(lo, m)


def _params(n_axes=None, vmem=VMEM_LIMIT):
    sem = None if n_axes is None else ("arbitrary",) * n_axes
    return pltpu.CompilerParams(dimension_semantics=sem, vmem_limit_bytes=vmem)


def _const_spec(shape):
    nd = len(shape)
    return pl.BlockSpec(shape, lambda *_: (0,) * nd)


def _rope_fwd(x, cos, sa, sb):
    return x * cos + pltpu.roll(x, 96, 1) * sa + pltpu.roll(x, 32, 1) * sb


def _rope_bwd(d, cos, sa, sb):
    return d * cos - pltpu.roll(d, 96, 1) * sa - pltpu.roll(d, 32, 1) * sb


def _lane_mask(rows, dtype=F32):
    lane = lax.broadcasted_iota(jnp.int32, (rows, 128), 1)
    return jnp.where(lane < HEAD_DIM, 1.0, 0.0).astype(dtype)


def _norm_proj(x2, ng, w_bf, qg512, kg128, g512, cos, sa, sb):
    T = x2.shape[0]

    def body(x_ref, ng_ref, w_ref, qg_ref, kg_ref, g_ref, cos_ref, sa_ref, sb_ref,
             h_ref, qraw_ref, kraw_ref, qrot_ref, k2_ref, v2_ref, ga_ref, qb_ref, kb_ref, vb_ref, gb_ref):
        xb = x_ref[...]
        r = lax.rsqrt(jnp.mean(xb * xb, axis=-1, keepdims=True) + EPS)
        h = (xb * r * ng_ref[...]).astype(BF16)
        h_ref[...] = h
        cosv, sav, sbv = cos_ref[...], sa_ref[...], sb_ref[...]
        m0 = _lane_mask(TM)

        def dup(v):
            v0 = v * m0
            v1 = v - v0
            return v0 + pltpu.roll(v0, 64, 1), v1 + pltpu.roll(v1, 64, 1)

        qa = _dot_nt(h, w_ref[QA:KA, :])
        qraw_ref[...] = qa
        qn = qa * lax.rsqrt(_dot_hl(qa * qa, g_ref[...]) + EPS) * qg_ref[...]
        for s in range(4):
            qs = _rope_fwd(qn[:, s * 128:(s + 1) * 128], cosv, sav, sbv)
            qrot_ref[:, s * 128:(s + 1) * 128] = (qs * SCALE).astype(BF16)
        ka = _dot_nt(h, w_ref[KA:VA, :])
        kraw_ref[...] = ka
        kn = ka * lax.rsqrt(_dot_hl(ka * ka, g_ref[0:128, 0:128]) + EPS) * kg_ref[...]
        k0, k1 = dup(_rope_fwd(kn, cosv, sav, sbv))
        k2_ref[:, 0:128] = k0.astype(BF16)
        k2_ref[:, 128:256] = k1.astype(BF16)
        v0, v1 = dup(_dot_nt(h, w_ref[VA:GA, :]))
        v2_ref[:, 0:128] = v0.astype(BF16)
        v2_ref[:, 128:256] = v1.astype(BF16)
        ga_ref[...] = _dot_nt(h, w_ref[GA:QB, :])
        qb_ref[...] = (_dot_nt(h, w_ref[QB:KB, :]) * SCALE).astype(BF16)
        kb_ref[...] = _dot_nt(h, w_ref[KB:VB, :]).astype(BF16)
        vb_ref[...] = _dot_nt(h, w_ref[VB:GB, :]).astype(BF16)
        gb_ref[...] = _dot_nt(h, w_ref[GB:IN_WIDTH, :])

    def rows(w):
        return pl.BlockSpec((TM, w), lambda i: (i, 0))

    outs = [(D_MODEL, BF16), (512, F32), (128, F32), (512, BF16), (256, BF16), (256, BF16), (512, F32),
            (512, BF16), (512, BF16), (512, BF16), (512, F32)]
    return pl.pallas_call(
        body, name="norm_proj", grid=(T // TM,),
        in_specs=[rows(D_MODEL), _const_spec((1, D_MODEL)), _const_spec((IN_WIDTH, D_MODEL)), _const_spec((1, 512)),
                  _const_spec((1, 128)), _const_spec((512, 512)), rows(128), rows(128), rows(128)],
        out_specs=[rows(w) for w, _ in outs],
        out_shape=[jax.ShapeDtypeStruct((T, w), dt) for w, dt in outs],
        compiler_params=_params(1),
    )(x2, ng, w_bf, qg512, kg128, g512, cos, sa, sb)


SWA_Q = 512
SWA_SUB = SWA_Q // BLOCK


def _swa_scores(q, kst, mask, sink_ref, pair):
    s_all = _dot_nt(q, kst)
    probs, stats = [], []
    for hh in range(2):
        sink = sink_ref[pair * 2 + hh]
        s = jnp.where(mask, s_all[:, hh * 256:(hh + 1) * 256], NEG)
        m = jnp.maximum(jnp.max(s, axis=1, keepdims=True), sink)
        pe = jnp.exp(s - m)
        inv = 1.0 / (jnp.sum(pe, axis=1, keepdims=True) + jnp.exp(sink - m))
        probs.append(pe * inv)
        stats.append(jnp.exp(sink - m) * inv)
    return probs, stats


def _swa_mask(has_prev):
    r = lax.broadcasted_iota(jnp.int32, (128, 256), 0)
    c = lax.broadcasted_iota(jnp.int32, (128, 256), 1)
    band = (c > r) & (c <= r + 128)
    return band if has_prev is True else band & ((c >= 128) | has_prev)


def _swa_keys(prev_ref, main_ref, s, kv, m0b):
    cols = slice(kv * 128, (kv + 1) * 128)
    prev = prev_ref[:, cols] if s == 0 else main_ref[(s - 1) * 128:s * 128, cols]
    kc = jnp.concatenate([prev, main_ref[s * 128:(s + 1) * 128, cols]], axis=0)
    k0 = kc * m0b
    return jnp.concatenate([k0, kc - k0], axis=0)


def _swa_fwd(sinks, qrot, k2, v2, nbatch, seq):
    ni = seq // SWA_Q
    T = nbatch * seq

    def body(sink_ref, q_ref, kp_ref, km_ref, vp_ref, vm_ref, o_ref):
        i = pl.program_id(1)
        m0b = _lane_mask(256, BF16)
        for s in range(SWA_SUB):
            mask = _swa_mask(True if s else i > 0)
            rows = slice(s * 128, (s + 1) * 128)
            for kv in range(2):
                kst = _swa_keys(kp_ref, km_ref, s, kv, m0b)
                vst = _swa_keys(vp_ref, vm_ref, s, kv, m0b)
                for pr in range(2):
                    pair = kv * 2 + pr
                    cols = slice(pair * 128, (pair + 1) * 128)
                    probs, _ = _swa_scores(q_ref[rows, cols], kst, mask, sink_ref, pair)
                    o_ref[rows, cols] = _dot(jnp.concatenate(probs, axis=1).astype(BF16), vst)

    main = lambda b, i: (b * ni + i, 0)
    prev = lambda b, i: ((b * ni + i) * SWA_SUB - jnp.where(i > 0, 1, 0), 0)
    return pl.pallas_call(
        body, name="swa_fwd", grid=(nbatch, ni),
        in_specs=[pl.BlockSpec(memory_space=pltpu.SMEM), pl.BlockSpec((SWA_Q, 512), main),
                  pl.BlockSpec((128, 256), prev), pl.BlockSpec((SWA_Q, 256), main),
                  pl.BlockSpec((128, 256), prev), pl.BlockSpec((SWA_Q, 256), main)],
        out_specs=pl.BlockSpec((SWA_Q, 512), main),
        out_shape=jax.ShapeDtypeStruct((T, 512), F32),
        compiler_params=_params(2),
    )(sinks, qrot, k2, k2, v2, v2)


def _swa_bwd(sinks, qrot, k2, v2, doa, nbatch, seq):
    ni = seq // SWA_Q
    T = nbatch * seq

    def body(sink_ref, q_ref, kp_ref, km_ref, vp_ref, vm_ref, do_ref, dq_ref, dk_ref, dv_ref, ds_ref, dkc, dvc):
        b, i = pl.program_id(0), pl.program_id(1)

        @pl.when((b == 0) & (i == 0))
        def _():
            ds_ref[...] = jnp.zeros_like(ds_ref)

        @pl.when(i == 0)
        def _():
            dkc[...] = jnp.zeros_like(dkc)
            dvc[...] = jnp.zeros_like(dvc)

        @pl.when(i < ni)
        def _():
            m0b = _lane_mask(256, BF16)
            m0 = _lane_mask(128) > 0.5
            for kv in range(2):
                kcols = slice(kv * 128, (kv + 1) * 128)
                dk_own, dv_own = dkc[:, kcols], dvc[:, kcols]
                for s in range(SWA_SUB):
                    mask = _swa_mask(True if s else i > 0)
                    rows = slice(s * 128, (s + 1) * 128)
                    kst = _swa_keys(kp_ref, km_ref, s, kv, m0b)
                    vst = _swa_keys(vp_ref, vm_ref, s, kv, m0b)
                    dkst = jnp.zeros((512, 128), F32)
                    dvst = jnp.zeros((512, 128), F32)
                    for pr in range(2):
                        pair = kv * 2 + pr
                        cols = slice(pair * 128, (pair + 1) * 128)
                        q, do = q_ref[rows, cols], do_ref[rows, cols]
                        probs, psink = _swa_scores(q, kst, mask, sink_ref, pair)
                        dp_all = _dot_nt(do, vst)
                        ds_parts = []
                        for hh in range(2):
                            dp = dp_all[:, hh * 256:(hh + 1) * 256]
                            delta = jnp.sum(probs[hh] * dp, axis=1, keepdims=True)
                            ds_parts.append(probs[hh] * (dp - delta))
                            h = pair * 2 + hh
                            ds_ref[h:h + 1, :] = ds_ref[h:h + 1, :] - jnp.sum(psink[hh] * delta)
                        ds_all = jnp.concatenate(ds_parts, axis=1).astype(BF16)
                        p_all = jnp.concatenate(probs, axis=1).astype(BF16)
                        dq_ref[rows, cols] = _dot(ds_all, kst) * SCALE
                        dkst = dkst + _dot_tn(ds_all, q)
                        dvst = dvst + _dot_tn(p_all, do)
                    dk_ref[rows, kcols] = dk_own + jnp.where(m0, dkst[0:128], dkst[256:384])
                    dv_ref[rows, kcols] = dv_own + jnp.where(m0, dvst[0:128], dvst[256:384])
                    dk_own = jnp.where(m0, dkst[128:256], dkst[384:512])
                    dv_own = jnp.where(m0, dvst[128:256], dvst[384:512])
                dkc[:, kcols] = dk_own
                dvc[:, kcols] = dv_own

        @pl.when(i == ni)
        def _():
            dk_ref[...] = jnp.zeros_like(dk_ref)
            dv_ref[...] = jnp.zeros_like(dv_ref)
            dk_ref[0:128, :] = dkc[...]
            dv_ref[0:128, :] = dvc[...]

    main = lambda b, i: (b * ni + jnp.minimum(i, ni - 1), 0)
    prev = lambda b, i: ((b * ni + jnp.minimum(i, ni - 1)) * SWA_SUB - jnp.where(jnp.minimum(i, ni - 1) > 0, 1, 0), 0)
    shifted = lambda b, i: (b * (ni + 1) + i, 0)
    tpad = nbatch * (ni + 1) * SWA_Q
    return pl.pallas_call(
        body, name="swa_bwd", grid=(nbatch, ni + 1),
        in_specs=[pl.BlockSpec(memory_space=pltpu.SMEM), pl.BlockSpec((SWA_Q, 512), main),
                  pl.BlockSpec((128, 256), prev), pl.BlockSpec((SWA_Q, 256), main),
                  pl.BlockSpec((128, 256), prev), pl.BlockSpec((SWA_Q, 256), main),
                  pl.BlockSpec((SWA_Q, 512), main)],
        out_specs=[pl.BlockSpec((SWA_Q, 512), main), pl.BlockSpec((SWA_Q, 256), shifted),
                   pl.BlockSpec((SWA_Q, 256), shifted), _const_spec((8, 128))],
        out_shape=[jax.ShapeDtypeStruct((T, 512), F32), jax.ShapeDtypeStruct((tpad, 256), F32),
                   jax.ShapeDtypeStruct((tpad, 256), F32), jax.ShapeDtypeStruct((8, 128), F32)],
        scratch_shapes=[pltpu.VMEM((128, 256), F32), pltpu.VMEM((128, 256), F32)],
        compiler_params=_params(2),
    )(sinks, qrot, k2, k2, v2, v2, doa)


def _unshift(dkpad, nbatch, seq):
    return dkpad.reshape(nbatch, seq + SWA_Q, 256)[:, BLOCK:BLOCK + seq].reshape(nbatch * seq, 256)


SB_T = 256


def _sb_masks():
    r = lax.broadcasted_iota(jnp.int32, (SB_T, 2 * SB_T), 0)
    c = lax.broadcasted_iota(jnp.int32, (SB_T, 2 * SB_T), 1)
    return (c & (SB_T - 1)) < r


def _sb_logits(q, kst):
    z = _dot_nt(q, kst)
    sign = jnp.uint32(0x80000000)
    neg_abs = lax.bitcast_convert_type(lax.bitcast_convert_type(z, jnp.uint32) | sign, F32)
    return z, -(jnp.maximum(z, 0.0) + jnp.log(1.0 + jnp.exp(neg_abs)))


SB_NP = 2


def _pair_rows(ref, j, cols, m0b):
    kj = ref[pl.ds(pl.multiple_of(j * SB_T, SB_T), SB_T), cols]
    k0 = kj * m0b
    return jnp.concatenate([k0, kj - k0], axis=0)


def _bcast2(c0, c1):
    return jnp.concatenate([jnp.broadcast_to(c0, (SB_T, SB_T)), jnp.broadcast_to(c1, (SB_T, SB_T))], axis=1)


def _rowsum2(x):
    return jnp.sum(x[:, 0:SB_T], axis=1, keepdims=True), jnp.sum(x[:, SB_T:2 * SB_T], axis=1, keepdims=True)


def _scan2(x, tri2):
    outs = []
    for h in range(2):
        xh = x[:, h * SB_T:(h + 1) * SB_T]
        hi = xh.astype(BF16)
        lo = (xh - hi.astype(F32)).astype(BF16)
        outs.append(_dot(jnp.concatenate([hi, lo], axis=1), tri2))
    return jnp.concatenate(outs, axis=1)


def _sb_fwd(qb, kb, vb, ublk, nbatch, seq):
    nb = seq // SB_T
    T = nbatch * seq

    def body(q_ref, k_ref, v_ref, u_ref, o_ref, ctab_ref):
        i = pl.program_id(2)
        m0b = _lane_mask(SB_T, BF16)
        u = u_ref[...]
        lane = lax.broadcasted_iota(jnp.int32, (SB_T, 128), 1)
        ctab_ref[...] = jnp.zeros_like(ctab_ref)

        def tile(j, carries, diag):
            out = []
            for pp, (c0, c1, acc) in enumerate(carries):
                cols = slice(pp * 128, (pp + 1) * 128)
                if not diag:
                    ctab_ref[:, cols] = jnp.where(lane == j, c0, jnp.where(lane == nb + j, c1, ctab_ref[:, cols]))
                kst = _pair_rows(k_ref, j, cols, m0b)
                vst = _pair_rows(v_ref, j, cols, m0b)
                z, lb = _sb_logits(q_ref[:, cols], kst)
                if diag:
                    mask = _sb_masks()
                    lb = jnp.where(mask, lb, 0.0)
                incl = _scan2(lb, u)
                w = jnp.exp(z + incl + _bcast2(c0, c1))
                if diag:
                    w = jnp.where(mask, w, 0.0)
                out.append((c0 + incl[:, 0:1], c1 + incl[:, SB_T:SB_T + 1], acc + _dot(w.astype(BF16), vst)))
            return tuple(out)

        zc = jnp.zeros((SB_T, 1), F32)
        carries = tile(i, ((zc, zc, jnp.zeros((SB_T, 128), F32)),) * SB_NP, True)
        carries = lax.fori_loop(0, i, lambda jj, cr: tile(i - 1 - jj, cr, False), carries)
        for pp in range(SB_NP):
            o_ref[:, pp * 128:(pp + 1) * 128] = carries[pp][2]

    wide = 128 * SB_NP
    blk = lambda b, g, i: (b * nb + i, g)
    full = lambda b, g, i: (b, g)
    return pl.pallas_call(
        body, name="sb_fwd", grid=(nbatch, 4 // SB_NP, nb),
        in_specs=[pl.BlockSpec((SB_T, wide), blk), pl.BlockSpec((seq, wide), full), pl.BlockSpec((seq, wide), full),
                  _const_spec((2 * SB_T, SB_T))],
        out_specs=[pl.BlockSpec((SB_T, wide), blk), pl.BlockSpec((SB_T, wide), blk)],
        out_shape=[jax.ShapeDtypeStruct((T, 512), F32), jax.ShapeDtypeStruct((T, 512), F32)],
        compiler_params=_params(3),
    )(qb, kb, vb, ublk)


def _sb_bwd(qb, kb, vb, dob, ctab, ublk, pblk, nbatch, seq):
    nb = seq // SB_T
    T = nbatch * seq

    def body(q_ref, k_ref, v_ref, do_ref, ctab_ref, u_ref, up_ref, dq_ref, dk_ref, dv_ref):
        i = pl.program_id(2)

        @pl.when(i == 0)
        def _():
            dk_ref[...] = jnp.zeros_like(dk_ref)
            dv_ref[...] = jnp.zeros_like(dv_ref)

        m0b = _lane_mask(SB_T, BF16)
        m0 = _lane_mask(SB_T) > 0.5
        u, up = u_ref[...], up_ref[...]
        lane = lax.broadcasted_iota(jnp.int32, (SB_T, 128), 1)

        def tile(j, carries, diag):
            out = []
            for pp, (s0, s1, dq) in enumerate(carries):
                cols = slice(pp * 128, (pp + 1) * 128)
                q, do = q_ref[:, cols], do_ref[:, cols]
                kst = _pair_rows(k_ref, j, cols, m0b)
                vst = _pair_rows(v_ref, j, cols, m0b)
                z, lb = _sb_logits(q, kst)
                one_minus_beta = jnp.exp(lb)
                if diag:
                    mask = _sb_masks()
                    lb = jnp.where(mask, lb, 0.0)
                    w = jnp.where(mask, jnp.exp(z + _scan2(lb, u)), 0.0)
                else:
                    ct = ctab_ref[:, cols]
                    c0 = jnp.sum(jnp.where(lane == j, ct, 0.0), axis=1, keepdims=True)
                    c1 = jnp.sum(jnp.where(lane == nb + j, ct, 0.0), axis=1, keepdims=True)
                    w = jnp.exp(z + _scan2(lb, u) + _bcast2(c0, c1))
                e = _dot_nt(do, vst) * w
                dlb = _bcast2(s0, s1) + _scan2(e, up)
                dz = (e + dlb) * one_minus_beta - dlb
                if diag:
                    dz = jnp.where(mask, dz, 0.0)
                dzb = dz.astype(BF16)
                dkst = _dot_tn(dzb, q)
                dvst = _dot_tn(w.astype(BF16), do)
                rows = pl.ds(pl.multiple_of(j * SB_T, SB_T), SB_T)
                dk_ref[rows, cols] = dk_ref[rows, cols] + jnp.where(m0, dkst[0:SB_T], dkst[SB_T:2 * SB_T])
                dv_ref[rows, cols] = dv_ref[rows, cols] + jnp.where(m0, dvst[0:SB_T], dvst[SB_T:2 * SB_T])
                x0, x1 = _rowsum2(e)
                out.append((s0 + x0, s1 + x1, dq + _dot(dzb, kst)))
            return tuple(out)

        zc = jnp.zeros((SB_T, 1), F32)
        carries = lax.fori_loop(0, i, lambda j, cr: tile(j, cr, False), ((zc, zc, jnp.zeros((SB_T, 128), F32)),) * SB_NP)
        carries = tile(i, carries, True)
        for pp in range(SB_NP):
            dq_ref[:, pp * 128:(pp + 1) * 128] = carries[pp][2] * SCALE

    wide = 128 * SB_NP
    blk = lambda b, g, i: (b * nb + i, g)
    full = lambda b, g, i: (b, g)
    return pl.pallas_call(
        body, name="sb_bwd", grid=(nbatch, 4 // SB_NP, nb),
        in_specs=[pl.BlockSpec((SB_T, wide), blk), pl.BlockSpec((seq, wide), full), pl.BlockSpec((seq, wide), full),
                  pl.BlockSpec((SB_T, wide), blk), pl.BlockSpec((SB_T, wide), blk),
                  _const_spec((2 * SB_T, SB_T)), _const_spec((2 * SB_T, SB_T))],
        out_specs=[pl.BlockSpec((SB_T, wide), blk), pl.BlockSpec((seq, wide), full), pl.BlockSpec((seq, wide), full)],
        out_shape=[jax.ShapeDtypeStruct((T, 512), F32)] * 3,
        compiler_params=_params(3),
    )(qb, kb, vb, dob, ctab, ublk, pblk)


def _sigmoid(g):
    return 1.0 / (1.0 + jnp.exp(-g))


def _out_proj(oa, ob, ga, gb, x2, tgt, wout_bf):
    T = x2.shape[0]

    def body(oa_ref, ob_ref, ga_ref, gb_ref, x_ref, t_ref, w_ref, y_ref, dout_ref, loss_ref):
        @pl.when(pl.program_id(0) == 0)
        def _():
            loss_ref[...] = jnp.zeros_like(loss_ref)

        ga, gb = ga_ref[...], gb_ref[...]
        ya = (oa_ref[...] * (ga * _sigmoid(ga))).astype(BF16)
        yb = (ob_ref[...] * (gb * _sigmoid(gb))).astype(BF16)
        y_ref[:, 0:512] = ya
        y_ref[:, 512:1024] = yb
        out = x_ref[...] + _dot(ya, w_ref[0:512, :]) + _dot(yb, w_ref[512:1024, :])
        diff = out - t_ref[...]
        dout_ref[...] = diff * (1.0 / D_MODEL)
        loss_ref[...] = loss_ref[...] + jnp.sum(diff * diff) * (0.5 / D_MODEL)

    rows = lambda w: pl.BlockSpec((TM, w), lambda i: (i, 0))
    return pl.pallas_call(
        body, name="out_proj", grid=(T // TM,),
        in_specs=[rows(512), rows(512), rows(512), rows(512), rows(D_MODEL), rows(D_MODEL),
                  _const_spec((D_MODEL, D_MODEL))],
        out_specs=[rows(D_MODEL), rows(D_MODEL), _const_spec((8, 128))],
        out_shape=[jax.ShapeDtypeStruct((T, D_MODEL), BF16), jax.ShapeDtypeStruct((T, D_MODEL), F32),
                   jax.ShapeDtypeStruct((8, 128), F32)],
        compiler_params=_params(1),
    )(oa, ob, ga, gb, x2, tgt, wout_bf)


def _out_proj_bwd(dout, y, oa, ob, ga, gb, wout_bf):
    T = dout.shape[0]

    def body(dout_ref, y_ref, oa_ref, ob_ref, ga_ref, gb_ref, w_ref, doa_ref, dob_ref, dga_ref, dgb_ref, dw_ref):
        @pl.when(pl.program_id(0) == 0)
        def _():
            dw_ref[...] = jnp.zeros_like(dw_ref)

        db = dout_ref[...].astype(BF16)
        dw_ref[...] = dw_ref[...] + _dot_tn(y_ref[...], db)
        for o_ref, g_ref, do_ref, dg_ref, lo in ((oa_ref, ga_ref, doa_ref, dga_ref, 0), (ob_ref, gb_ref, dob_ref, dgb_ref, 512)):
            dy = _dot_nt(db, w_ref[lo:lo + 512, :])
            g = g_ref[...]
            sg = _sigmoid(g)
            do_ref[...] = (dy * (g * sg)).astype(BF16)
            dg_ref[...] = (dy * o_ref[...] * (sg * (1.0 + g * (1.0 - sg)))).astype(BF16)

    rows = lambda w: pl.BlockSpec((TM, w), lambda i: (i, 0))
    return pl.pallas_call(
        body, name="out_proj_bwd", grid=(T // TM,),
        in_specs=[rows(D_MODEL), rows(D_MODEL), rows(512), rows(512), rows(512), rows(512),
                  _const_spec((D_MODEL, D_MODEL))],
        out_specs=[rows(512)] * 4 + [_const_spec((D_MODEL, D_MODEL))],
        out_shape=[jax.ShapeDtypeStruct((T, 512), BF16)] * 4 + [jax.ShapeDtypeStruct((D_MODEL, D_MODEL), F32)],
        compiler_params=_params(1),
    )(dout, y, oa, ob, ga, gb, wout_bf)


def _qk_grad(qraw, kraw, dqrot, dk2, dv2, qg512, kg128, g512, cos, sa, sb):
    T = qraw.shape[0]

    def body(qraw_ref, kraw_ref, dqrot_ref, dk2_ref, dv2_ref, qg_ref, kg_ref, g_ref, cos_ref, sa_ref, sb_ref,
             dqa_ref, dkv_ref, dqg_ref, dkg_ref):
        @pl.when(pl.program_id(0) == 0)
        def _():
            dqg_ref[...] = jnp.zeros_like(dqg_ref)
            dkg_ref[...] = jnp.zeros_like(dkg_ref)

        cosv, sav, sbv = cos_ref[...], sa_ref[...], sb_ref[...]
        m0 = _lane_mask(TM) > 0.5

        def head_norm_bwd(raw, dn_rot, gmat, gain):
            r = lax.rsqrt(_dot_hl(raw * raw, gmat) + EPS)
            n = raw * r
            dn = dn_rot * gain
            return r * (dn - n * _dot_hl(dn * n, gmat)), jnp.sum(dn_rot * n, axis=0, keepdims=True)

        def fold(ref):
            a, b = ref[:, 0:128], ref[:, 128:256]
            return jnp.where(m0, a + pltpu.roll(a, 64, 1), b + pltpu.roll(b, 64, 1))

        dqn = jnp.concatenate([_rope_bwd(dqrot_ref[:, s * 128:(s + 1) * 128], cosv, sav, sbv) for s in range(4)], axis=1)
        dqa, dqg = head_norm_bwd(qraw_ref[...], dqn, g_ref[...], qg_ref[...])
        dka, dkg = head_norm_bwd(kraw_ref[...], _rope_bwd(fold(dk2_ref), cosv, sav, sbv), g_ref[0:128, 0:128], kg_ref[...])
        dqa_ref[...] = dqa.astype(BF16)
        dkv_ref[:, 0:128] = dka.astype(BF16)
        dkv_ref[:, 128:256] = fold(dv2_ref).astype(BF16)
        dqg_ref[...] = dqg_ref[...] + dqg
        dkg_ref[...] = dkg_ref[...] + dkg

    rows = lambda w: pl.BlockSpec((TM, w), lambda i: (i, 0))
    return pl.pallas_call(
        body, name="qk_grad", grid=(T // TM,),
        in_specs=[rows(512), rows(128), rows(512), rows(256), rows(256), _const_spec((1, 512)), _const_spec((1, 128)),
                  _const_spec((512, 512)), rows(128), rows(128), rows(128)],
        out_specs=[rows(512), rows(256), _const_spec((1, 512)), _const_spec((1, 128))],
        out_shape=[jax.ShapeDtypeStruct((T, 512), BF16), jax.ShapeDtypeStruct((T, 256), BF16),
                   jax.ShapeDtypeStruct((1, 512), F32), jax.ShapeDtypeStruct((1, 128), F32)],
        compiler_params=_params(1),
    )(qraw, kraw, dqrot, dk2, dv2, qg512, kg128, g512, cos, sa, sb)


_PIECES = ((QA, 512), (KA, 256), (GA, 512), (QB, 512), (KB, 512), (VB, 512), (GB, 512))


def _w_in_grad(h, pieces):
    T = h.shape[0]

    def body(h_ref, *refs):
        dw_ref = refs[-1]

        @pl.when(pl.program_id(0) == 0)
        def _():
            dw_ref[...] = jnp.zeros_like(dw_ref)

        hb = h_ref[...]
        for (lo, width), p_ref in zip(_PIECES, refs[:-1]):
            dw_ref[lo:lo + width, :] = dw_ref[lo:lo + width, :] + _dot_tn(p_ref[...].astype(BF16), hb)

    rows = lambda w: pl.BlockSpec((TM, w), lambda i: (i, 0))
    return pl.pallas_call(
        body, name="w_in_grad", grid=(T // TM,),
        in_specs=[rows(D_MODEL)] + [rows(w) for _, w in _PIECES],
        out_specs=_const_spec((IN_WIDTH, D_MODEL)),
        out_shape=jax.ShapeDtypeStruct((IN_WIDTH, D_MODEL), F32),
        compiler_params=_params(1),
    )(h, *pieces)


def _x_grad(x2, dout, pieces, w_bf, ng, cin4, cout4):
    T = x2.shape[0]
    npc = len(_PIECES)
    steps = T // TM

    def body(x_ref, dout_ref, *refs):
        w_ref, ng_ref, cin_ref, cout_ref, gx_ref, dng_ref, rin_ref, rout_ref, ssem, rsem = refs[npc:]
        step = pl.program_id(0)
        _, _, c, chips = _place()
        sems = (ssem, rsem)

        def copies():
            out = []
            for j, (cx, cy) in enumerate(chips):
                k = 2 * cx + cy
                out.append(_remote(cin_ref.at[k], rin_ref.at[j], sems, 2 * j, (cx, cy, c)))
                out.append(_remote(cout_ref.at[k], rout_ref.at[j], sems, 2 * j + 1, (cx, cy, c)))
            return out

        @pl.when(step == 0)
        def _():
            dng_ref[...] = jnp.zeros_like(dng_ref)
            for cp in copies():
                cp.start()

        dh = jnp.zeros((TM, D_MODEL), F32)
        for (lo, width), p_ref in zip(_PIECES, refs[:npc]):
            dh = dh + _dot(p_ref[...].astype(BF16), w_ref[lo:lo + width, :])
        xb = x_ref[...]
        r = lax.rsqrt(jnp.mean(xb * xb, axis=-1, keepdims=True) + EPS)
        n = xb * r
        dn = dh * ng_ref[...]
        gx_ref[...] = dout_ref[...] + r * (dn - n * jnp.mean(dn * n, axis=-1, keepdims=True))
        dng_ref[...] = dng_ref[...] + jnp.sum(dh * n, axis=0, keepdims=True)

        @pl.when(step == steps - 1)
        def _():
            for cp in copies():
                cp.wait_recv()
            for cp in copies():
                cp.wait_send()

    rows = lambda w: pl.BlockSpec((TM, w), lambda i: (i, 0))
    hbm = pl.BlockSpec(memory_space=pl.ANY)
    return pl.pallas_call(
        body, name="x_grad", grid=(steps,),
        in_specs=[rows(D_MODEL), rows(D_MODEL)] + [rows(w) for _, w in _PIECES]
        + [_const_spec((IN_WIDTH, D_MODEL)), _const_spec((1, D_MODEL)), hbm, hbm],
        out_specs=[rows(D_MODEL), _const_spec((1, D_MODEL)), hbm, hbm],
        out_shape=[jax.ShapeDtypeStruct((T, D_MODEL), F32), jax.ShapeDtypeStruct((1, D_MODEL), F32),
                   jax.ShapeDtypeStruct((3,) + cin4.shape[1:], cin4.dtype),
                   jax.ShapeDtypeStruct((3,) + cout4.shape[1:], cout4.dtype)],
        scratch_shapes=[pltpu.SemaphoreType.DMA((6,)), pltpu.SemaphoreType.DMA((6,))],
        compiler_params=_params(1),
    )(x2, dout, *pieces, w_bf, ng, cin4, cout4)


HBM = pl.BlockSpec(memory_space=pl.ANY)


def _place():
    x, y, c = lax.axis_index("x"), lax.axis_index("y"), lax.axis_index("c")
    chips = [(1 - x, y), (x, 1 - y), (1 - x, 1 - y)]
    return x, y, c, chips


def _remote(src, dst, sems, k, to):
    return pltpu.make_async_remote_copy(src_ref=src, dst_ref=dst, send_sem=sems[0].at[k], recv_sem=sems[1].at[k],
                                        device_id=to, device_id_type=MESH)


def _gather_weights(shards):
    n = len(shards)

    def body(*refs):
        srcs, dsts, (ssem, rsem) = refs[:n], refs[n:2 * n], refs[2 * n:]
        x, y, c, chips = _place()
        me, sib = 2 * x + y, (x, y, 1 - c)
        sems = (ssem, rsem)
        sent = []
        for j, (cx, cy) in enumerate(chips):
            for a in range(n):
                sent.append(_remote(srcs[a].at[c], dsts[a].at[me, c], sems, n * j + a, (cx, cy, c)))
        for cp in sent:
            cp.start()
        for j, (cx, cy) in enumerate(chips):
            k = 2 * cx + cy
            for a in range(n):
                _remote(srcs[a].at[c], dsts[a].at[k, c], sems, n * j + a, sib).wait_recv()
                fwd = _remote(dsts[a].at[k, c], dsts[a].at[k, c], sems, 3 * n + n * j + a, sib)
                fwd.start()
                sent.append(fwd)
        for j, (cx, cy) in enumerate(chips):
            k = 2 * cx + cy
            for a in range(n):
                _remote(srcs[a].at[c], dsts[a].at[k, 1 - c], sems, 3 * n + n * j + a, sib).wait_recv()
        for cp in sent:
            cp.wait_send()

    gathered = pl.pallas_call(
        body, name="gather_weights", in_specs=[HBM] * n, out_specs=[HBM] * n,
        out_shape=[jax.ShapeDtypeStruct((4,) + s.shape, s.dtype) for s in shards],
        scratch_shapes=[pltpu.SemaphoreType.DMA((6 * n,)), pltpu.SemaphoreType.DMA((6 * n,))],
    )(*shards)
    me = 2 * lax.axis_index("x") + lax.axis_index("y")
    return [lax.dynamic_update_slice(g, s[None], (me, 0, 0, 0)) for g, s in zip(gathered, shards)]


def _pair_exchange(grads):
    n = len(grads)

    def body(*refs):
        srcs, dsts, (ssem, rsem) = refs[:n], refs[n:2 * n], refs[2 * n:]
        x, y, c, _ = _place()
        sems = (ssem, rsem)
        sib = (x, y, 1 - c)
        sent = [_remote(srcs[a].at[:, pl.ds(1 - c, 1)], dsts[a], sems, a, sib) for a in range(n)]
        for cp in sent:
            cp.start()
        for cp in sent:
            cp.wait()

    return pl.pallas_call(
        body, name="pair_exchange", in_specs=[HBM] * n, out_specs=[HBM] * n,
        out_shape=[jax.ShapeDtypeStruct((4, 1) + g.shape[2:], g.dtype) for g in grads],
        scratch_shapes=[pltpu.SemaphoreType.DMA((n,)), pltpu.SemaphoreType.DMA((n,))],
    )(*grads)


def _share_halves(hin, hout, small):
    def body(hin_ref, hout_ref, small_ref, oin_ref, oout_ref, sall_ref, ssem, rsem, lsem):
        x, y, c, _ = _place()
        sems = (ssem, rsem)
        sib = (x, y, 1 - c)
        me = 4 * x + 2 * y + c
        own = pltpu.make_async_copy(small_ref, sall_ref.at[me], lsem.at[0])
        own.start()
        sent = [_remote(hin_ref, oin_ref, sems, 0, sib), _remote(hout_ref, oout_ref, sems, 1, sib)]
        flips = [(fx, fy, fc) for fx in (0, 1) for fy in (0, 1) for fc in (0, 1)][1:]
        for k, (fx, fy, fc) in enumerate(flips):
            sent.append(_remote(small_ref, sall_ref.at[me], sems, 2 + k, (x ^ fx, y ^ fy, c ^ fc)))
        for cp in sent:
            cp.start()
        _remote(hin_ref, oin_ref, sems, 0, sib).wait_recv()
        _remote(hout_ref, oout_ref, sems, 1, sib).wait_recv()
        for k, (fx, fy, fc) in enumerate(flips):
            src = 4 * (x ^ fx) + 2 * (y ^ fy) + (c ^ fc)
            _remote(small_ref, sall_ref.at[src], sems, 2 + k, sib).wait_recv()
        for cp in sent:
            cp.wait_send()
        own.wait()

    return pl.pallas_call(
        body, name="share_halves", in_specs=[HBM, HBM, HBM], out_specs=[HBM, HBM, HBM],
        out_shape=[jax.ShapeDtypeStruct(hin.shape, F32), jax.ShapeDtypeStruct(hout.shape, F32),
                   jax.ShapeDtypeStruct((8,) + small.shape, F32)],
        scratch_shapes=[pltpu.SemaphoreType.DMA((9,)), pltpu.SemaphoreType.DMA((9,)), pltpu.SemaphoreType.DMA((1,))],
    )(hin, hout, small)


def _add_half(cidx, full4, recv4):
    _, _, rows, width = full4.shape

    def body(c_ref, a_ref, b_ref, o_ref):
        o_ref[0] = (a_ref[0, 0] + b_ref[0, 0]).astype(BF16)

    return pl.pallas_call(
        body, name=f"add_half_{rows}",
        grid_spec=pltpu.PrefetchScalarGridSpec(
            num_scalar_prefetch=1, grid=(4,),
            in_specs=[pl.BlockSpec((1, 1, rows, width), lambda k, c: (k, c[0], 0, 0)),
                      pl.BlockSpec((1, 1, rows, width), lambda k, c: (k, 0, 0, 0))],
            out_specs=pl.BlockSpec((1, rows, width), lambda k, c: (k, 0, 0))),
        out_shape=jax.ShapeDtypeStruct((4, rows, width), BF16),
        compiler_params=_params(1),
    )(cidx, full4, recv4)


def _sum_chips(chip, own4, recv3):
    _, rows, width = recv3.shape
    rb = rows // 2

    def body(k_ref, a_ref, r_ref, o_ref):
        acc = a_ref[0].astype(F32)
        for s in range(3):
            acc = acc + r_ref[s].astype(F32)
        o_ref[...] = acc

    return pl.pallas_call(
        body, name=f"sum_chips_{rows}",
        grid_spec=pltpu.PrefetchScalarGridSpec(
            num_scalar_prefetch=1, grid=(rows // rb,),
            in_specs=[pl.BlockSpec((1, rb, width), lambda i, k: (k[0], i, 0)),
                      pl.BlockSpec((3, rb, width), lambda i, k: (0, i, 0))],
            out_specs=pl.BlockSpec((rb, width), lambda i, k: (i, 0))),
        out_shape=jax.ShapeDtypeStruct((rows, width), F32),
        compiler_params=_params(1),
    )(chip, own4, recv3)


def _sum_slots(r4):
    n, rows, width = r4.shape

    def body(r_ref, o_ref):
        acc = r_ref[0]
        for s in range(1, n):
            acc = acc + r_ref[s]
        o_ref[...] = acc

    return pl.pallas_call(
        body, name=f"sum_slots_{n}_{rows}_{width}", grid=(1,),
        in_specs=[pl.BlockSpec((n, rows, width), lambda i: (0, 0, 0))],
        out_specs=pl.BlockSpec((rows, width), lambda i: (0, 0)),
        out_shape=jax.ShapeDtypeStruct((rows, width), F32),
        compiler_params=_params(1),
    )(r4)


def _adam_math(w, g, m, v):
    c1 = 1.0 - ADAM_B1 ** ADAM_STEP
    c2 = 1.0 - ADAM_B2 ** ADAM_STEP
    nm = ADAM_B1 * m + (1.0 - ADAM_B1) * g
    nv = ADAM_B2 * v + (1.0 - ADAM_B2) * (g * g)
    return -ADAM_LR * ((nm / c1) / (jnp.sqrt(nv / c2) + ADAM_EPS) + ADAM_WD * w), nm, nv


def _adamw(w, g, m, v):
    rows, width = w.shape

    def body(w_ref, g_ref, m_ref, v_ref, d_ref, nm_ref, nv_ref):
        d_ref[...], nm_ref[...], nv_ref[...] = _adam_math(w_ref[...], g_ref[...], m_ref[...], v_ref[...])

    spec = pl.BlockSpec((rows, width), lambda i: (0, 0))
    return pl.pallas_call(
        body, name=f"adamw_{rows}_{width}", grid=(1,),
        in_specs=[spec] * 4, out_specs=[spec] * 3,
        out_shape=[jax.ShapeDtypeStruct((rows, width), F32)] * 3,
        compiler_params=_params(1),
    )(w, g, m, v)


def _adamw_halves(cidx, w, own, recv, m, v):
    rows, width = w.shape
    rb = rows // 4

    def body(c_ref, w_ref, own_ref, recv_ref, m_ref, v_ref, g_ref, d_ref, nm_ref, nv_ref):
        mine = (pl.program_id(0) // 2) == c_ref[0]
        g = jnp.where(mine, own_ref[...], recv_ref[...])
        g_ref[...] = g
        d_ref[...], nm_ref[...], nv_ref[...] = _adam_math(w_ref[...], g, m_ref[...], v_ref[...])

    full = pl.BlockSpec((rb, width), lambda i, c: (i, 0))
    half = pl.BlockSpec((rb, width), lambda i, c: (i % 2, 0))
    return pl.pallas_call(
        body, name=f"adamw_halves_{rows}",
        grid_spec=pltpu.PrefetchScalarGridSpec(
            num_scalar_prefetch=1, grid=(4,),
            in_specs=[full, half, half, full, full], out_specs=[full] * 4),
        out_shape=[jax.ShapeDtypeStruct((rows, width), F32)] * 4,
        compiler_params=_params(1),
    )(cidx, w, own, recv, m, v)


def _rope_tables(positions):
    half = HEAD_DIM // 2
    inv_freq = ROPE_THETA ** (-jnp.arange(half, dtype=F32) * 2.0 / HEAD_DIM)
    ang = positions.astype(F32).reshape(-1, 1) * inv_freq
    cos, sin, zero = jnp.cos(ang), jnp.sin(ang), jnp.zeros_like(ang)
    return (jnp.concatenate([cos] * 4, axis=1), jnp.concatenate([-sin, zero] * 2, axis=1),
            jnp.concatenate([zero, sin] * 2, axis=1))


def _constants():
    idx = jnp.arange(512)
    g512 = jnp.where(idx[:, None] // HEAD_DIM == idx[None, :] // HEAD_DIM, 1.0 / HEAD_DIM, 0.0).astype(BF16)
    j = jnp.arange(SB_T)
    ublk = jnp.where(j[:, None] >= j[None, :], 1.0, 0.0).astype(BF16)
    pblk = jnp.where(j[:, None] < j[None, :], 1.0, 0.0).astype(BF16)
    return g512, jnp.concatenate([ublk, ublk], axis=0), jnp.concatenate([pblk, pblk], axis=0)


def _pad_rows(v, width):
    return jnp.pad(v, ((0, 0), (0, width - v.shape[1])))


def _forward_backward(x2, tgt, positions, norm_gain, q_norm_gain, k_norm_gain, sinks, w_bf, wout_bf, nbatch, seq):
    cos, sa, sb = _rope_tables(positions)
    g512, ublk, pblk = _constants()
    qg512 = jnp.tile(q_norm_gain, (1, 8))
    kg128 = jnp.tile(k_norm_gain, (1, 2))
    sink1 = sinks.reshape(8)

    h, qraw, kraw, qrot, k2, v2, ga, qb, kb, vb, gb = _norm_proj(x2, norm_gain, w_bf, qg512, kg128, g512, cos, sa, sb)
    oa = _swa_fwd(sink1, qrot, k2, v2, nbatch, seq)
    ob, ctab = _sb_fwd(qb, kb, vb, ublk, nbatch, seq)
    y, dout, loss_acc = _out_proj(oa, ob, ga, gb, x2, tgt, wout_bf)

    doa, dob, dga, dgb, dwout = _out_proj_bwd(dout, y, oa, ob, ga, gb, wout_bf)
    dqrot, dk2, dv2, dsink = _swa_bwd(sink1, qrot, k2, v2, doa, nbatch, seq)
    dqb, dkb, dvb = _sb_bwd(qb, kb, vb, dob, ctab, ublk, pblk, nbatch, seq)
    dk2, dv2 = _unshift(dk2, nbatch, seq), _unshift(dv2, nbatch, seq)
    dqa, dkv, dqg, dkg = _qk_grad(qraw, kraw, dqrot, dk2, dv2, qg512, kg128, g512, cos, sa, sb)
    pieces = (dqa, dkv, dga, dqb, dkb, dvb, dgb)
    dwin_t = _w_in_grad(h, pieces)
    dqg64 = dqg.reshape(8, HEAD_DIM).sum(axis=0, keepdims=True)
    dkg64 = dkg.reshape(2, HEAD_DIM).sum(axis=0, keepdims=True)
    small = jnp.concatenate([_pad_rows(dqg64, D_MODEL), _pad_rows(dkg64, D_MODEL), _pad_rows(dsink[:, 0].reshape(1, 8), D_MODEL),
                             _pad_rows(loss_acc[0:1, 0:1], D_MODEL), jnp.zeros((3, D_MODEL), F32)], axis=0)
    return dwin_t, dwout, small, (x2, dout, pieces, w_bf, norm_gain)


def _reduce_grads(dwin_t, dwout, small7, x_grad_args):
    gin4 = dwin_t.reshape(4, 2, SHARD_IN // 2, D_MODEL)
    gout4 = dwout.reshape(4, 2, SHARD_OUT // 2, D_MODEL)
    rin, rout = _pair_exchange([gin4, gout4])
    cidx = lax.axis_index("c").astype(jnp.int32).reshape(1)
    chip = (2 * lax.axis_index("x") + lax.axis_index("y")).astype(jnp.int32).reshape(1)
    cin4, cout4 = _add_half(cidx, gin4, rin), _add_half(cidx, gout4, rout)
    gx, dng, rin3, rout3 = _x_grad(*x_grad_args, cin4, cout4)
    own_in, own_out = _sum_chips(chip, cin4, rin3), _sum_chips(chip, cout4, rout3)
    sib_in, sib_out, small_all = _share_halves(own_in, own_out, jnp.concatenate([dng, small7], axis=0))
    return gx, cidx, own_in, sib_in, own_out, sib_out, _sum_slots(small_all)


def kernel(x, positions, norm_gain, w_in, q_norm_gain, k_norm_gain, sinks, w_out, loss_target, m_norm_gain, m_w_in, m_q_norm_gain, m_k_norm_gain, m_sinks, m_w_out, v_norm_gain, v_w_in, v_q_norm_gain, v_k_norm_gain, v_sinks, v_w_out):
    nbatch, seq, _ = x.shape
    T = nbatch * seq
    x2 = x.reshape(T, D_MODEL)
    tgt = loss_target.reshape(T, D_MODEL)
    tr = lambda a: jnp.swapaxes(a[0], 0, 1)
    win_t, m_win_t, v_win_t = tr(w_in), tr(m_w_in), tr(v_w_in)

    ain, aout = _gather_weights([win_t.astype(BF16).reshape(2, SHARD_IN // 2, D_MODEL),
                                 w_out[0].astype(BF16).reshape(2, SHARD_OUT // 2, D_MODEL)])
    w_bf, wout_bf = ain.reshape(IN_WIDTH, D_MODEL), aout.reshape(D_MODEL, D_MODEL)

    dwin_t, dwout, small7, x_grad_args = _forward_backward(
        x2, tgt, positions, norm_gain, q_norm_gain, k_norm_gain, sinks, w_bf, wout_bf, nbatch, seq)

    gx, cidx, own_in, sib_in, own_out, sib_out, g_small = _reduce_grads(dwin_t, dwout, small7, x_grad_args)
    loss = g_small[4, 0]

    g_in, d_in, nm_in, nv_in = [jnp.swapaxes(a, 0, 1) for a in
                                _adamw_halves(cidx, win_t, own_in, sib_in, m_win_t, v_win_t)]
    g_out, d_out, nm_out, nv_out = _adamw_halves(cidx, w_out[0], own_out, sib_out, m_w_out[0], v_w_out[0])
    pack = lambda a, b, c_, d: jnp.concatenate(
        [a, _pad_rows(b, D_MODEL), _pad_rows(c_, D_MODEL), _pad_rows(d, D_MODEL), jnp.zeros((4, D_MODEL), F32)], axis=0)
    w_s = pack(norm_gain, q_norm_gain, k_norm_gain, sinks)
    m_s = pack(m_norm_gain, m_q_norm_gain, m_k_norm_gain, m_sinks)
    v_s = pack(v_norm_gain, v_q_norm_gain, v_k_norm_gain, v_sinks)
    d_s, nm_s, nv_s = _adamw(w_s, g_small, m_s, v_s)
    unpack = lambda a: (a[0:1, :], a[1:2, 0:HEAD_DIM], a[2:3, 0:HEAD_DIM], a[3:4, 0:8])

    g_ng, g_qg, g_kg, g_sk = unpack(g_small)
    d_ng, d_qg, d_kg, d_sk = unpack(d_s)
    m_ng, m_qg, m_kg, m_sk = unpack(nm_s)
    v_ng, v_qg, v_kg, v_sk = unpack(nv_s)
    return (loss, gx.reshape(nbatch, seq, D_MODEL),
            g_ng, g_in[None], g_qg, g_kg, g_sk, g_out[None],
            d_ng, d_in[None], d_qg, d_kg, d_sk, d_out[None],
            m_ng, nm_in[None], m_qg, m_kg, m_sk, nm_out[None],
            v_ng, nv_in[None], v_qg, v_kg, v_sk, nv_out[None])
```

```python
import functools
import math

import jax
import jax.numpy as jnp
from jax import lax
from jax.experimental import pallas as pl
from jax.experimental.pallas import tpu as pltpu

F32 = jnp.float32
BF16 = jnp.bfloat16

D_MODEL = 1024
HEAD_DIM = 64
BLOCK = 128
ROPE_THETA = 10000.0
EPS = 1e-6
QA, KA, VA, GA, QB, KB, VB, GB = 0, 512, 640, 768, 1280, 1792, 2304, 2816
IN_WIDTH = 3328
SHARD_IN = IN_WIDTH // 4
SHARD_OUT = D_MODEL // 4
SCALE = 1.0 / math.sqrt(HEAD_DIM)
NEG = -1e30

ADAM_LR, ADAM_B1, ADAM_B2, ADAM_EPS, ADAM_WD, ADAM_STEP = 0.001, 0.9, 0.999, 1e-08, 0.01, 10

TM = 256
VMEM_LIMIT = 56 * 1024 * 1024
MESH = pl.DeviceIdType.MESH


def _dot(a, b):
    return jnp.dot(a, b, preferred_element_type=F32)


def _dot_nt(a, b):
    return lax.dot_general(a, b, (((1,), (1,)), ((), ())), preferred_element_type=F32)


def _dot_tn(a, b):
    return lax.dot_general(a, b, (((0,), (0,)), ((), ())), preferred_element_type=F32)


def _dot_hl(a, m):
    hi = a.astype(BF16)
    lo = (a - hi.astype(F32)).astype(BF16)
    return _dot(hi, m) + _dot(lo, m)


def _params(n_axes=None, vmem=VMEM_LIMIT):
    sem = None if n_axes is None else ("arbitrary",) * n_axes
    return pltpu.CompilerParams(dimension_semantics=sem, vmem_limit_bytes=vmem)


def _const_spec(shape):
    nd = len(shape)
    return pl.BlockSpec(shape, lambda *_: (0,) * nd)


def _rope_fwd(x, cos, sa, sb):
    return x * cos + pltpu.roll(x, 96, 1) * sa + pltpu.roll(x, 32, 1) * sb


def _rope_bwd(d, cos, sa, sb):
    return d * cos - pltpu.roll(d, 96, 1) * sa - pltpu.roll(d, 32, 1) * sb


def _lane_mask(rows, dtype=F32):
    lane = lax.broadcasted_iota(jnp.int32, (rows, 128), 1)
    return jnp.where(lane < HEAD_DIM, 1.0, 0.0).astype(dtype)


def _norm_proj(x2, ng, w_bf, qg512, kg128, g512, cos, sa, sb):
    T = x2.shape[0]

    def body(x_ref, ng_ref, w_ref, qg_ref, kg_ref, g_ref, cos_ref, sa_ref, sb_ref,
             h_ref, qraw_ref, kraw_ref, qrot_ref, k2_ref, v2_ref, ga_ref, qb_ref, kb_ref, vb_ref, gb_ref):
        xb = x_ref[...]
        r = lax.rsqrt(jnp.mean(xb * xb, axis=-1, keepdims=True) + EPS)
        h = (xb * r * ng_ref[...]).astype(BF16)
        h_ref[...] = h
        cosv, sav, sbv = cos_ref[...], sa_ref[...], sb_ref[...]
        m0 = _lane_mask(TM)

        def dup(v):
            v0 = v * m0
            v1 = v - v0
            return v0 + pltpu.roll(v0, 64, 1), v1 + pltpu.roll(v1, 64, 1)

        qa = _dot_nt(h, w_ref[QA:KA, :])
        qraw_ref[...] = qa
        qn = qa * lax.rsqrt(_dot_hl(qa * qa, g_ref[...]) + EPS) * qg_ref[...]
        for s in range(4):
            qs = _rope_fwd(qn[:, s * 128:(s + 1) * 128], cosv, sav, sbv)
            qrot_ref[:, s * 128:(s + 1) * 128] = (qs * SCALE).astype(BF16)
        ka = _dot_nt(h, w_ref[KA:VA, :])
        kraw_ref[...] = ka
        kn = ka * lax.rsqrt(_dot_hl(ka * ka, g_ref[0:128, 0:128]) + EPS) * kg_ref[...]
        k0, k1 = dup(_rope_fwd(kn, cosv, sav, sbv))
        k2_ref[:, 0:128] = k0.astype(BF16)
        k2_ref[:, 128:256] = k1.astype(BF16)
        v0, v1 = dup(_dot_nt(h, w_ref[VA:GA, :]))
        v2_ref[:, 0:128] = v0.astype(BF16)
        v2_ref[:, 128:256] = v1.astype(BF16)
        ga_ref[...] = _dot_nt(h, w_ref[GA:QB, :])
        qb_ref[...] = (_dot_nt(h, w_ref[QB:KB, :]) * SCALE).astype(BF16)
        kb_ref[...] = _dot_nt(h, w_ref[KB:VB, :]).astype(BF16)
        vb_ref[...] = _dot_nt(h, w_ref[VB:GB, :]).astype(BF16)
        gb_ref[...] = _dot_nt(h, w_ref[GB:IN_WIDTH, :])

    def rows(w):
        return pl.BlockSpec((TM, w), lambda i: (i, 0))

    outs = [(D_MODEL, BF16), (512, F32), (128, F32), (512, BF16), (256, BF16), (256, BF16), (512, F32),
            (512, BF16), (512, BF16), (512, BF16), (512, F32)]
    return pl.pallas_call(
        body, name="norm_proj", grid=(T // TM,),
        in_specs=[rows(D_MODEL), _const_spec((1, D_MODEL)), _const_spec((IN_WIDTH, D_MODEL)), _const_spec((1, 512)),
                  _const_spec((1, 128)), _const_spec((512, 512)), rows(128), rows(128), rows(128)],
        out_specs=[rows(w) for w, _ in outs],
        out_shape=[jax.ShapeDtypeStruct((T, w), dt) for w, dt in outs],
        compiler_params=_params(1),
    )(x2, ng, w_bf, qg512, kg128, g512, cos, sa, sb)


SWA_Q = 512
SWA_SUB = SWA_Q // BLOCK


def _swa_scores(q, kst, mask, sink_ref, pair):
    s_all = _dot_nt(q, kst)
    probs, stats = [], []
    for hh in range(2):
        sink = sink_ref[pair * 2 + hh]
        s = jnp.where(mask, s_all[:, hh * 256:(hh + 1) * 256], NEG)
        m = jnp.maximum(jnp.max(s, axis=1, keepdims=True), sink)
        pe = jnp.exp(s - m)
        inv = 1.0 / (jnp.sum(pe, axis=1, keepdims=True) + jnp.exp(sink - m))
        probs.append(pe * inv)
        stats.append(jnp.exp(sink - m) * inv)
    return probs, stats


def _swa_mask(has_prev):
    r = lax.broadcasted_iota(jnp.int32, (128, 256), 0)
    c = lax.broadcasted_iota(jnp.int32, (128, 256), 1)
    band = (c > r) & (c <= r + 128)
    return band if has_prev is True else band & ((c >= 128) | has_prev)


def _swa_keys(prev_ref, main_ref, s, kv, m0b):
    cols = slice(kv * 128, (kv + 1) * 128)
    prev = prev_ref[:, cols] if s == 0 else main_ref[(s - 1) * 128:s * 128, cols]
    kc = jnp.concatenate([prev, main_ref[s * 128:(s + 1) * 128, cols]], axis=0)
    k0 = kc * m0b
    return jnp.concatenate([k0, kc - k0], axis=0)


def _swa_fwd(sinks, qrot, k2, v2, nbatch, seq):
    ni = seq // SWA_Q
    T = nbatch * seq

    def body(sink_ref, q_ref, kp_ref, km_ref, vp_ref, vm_ref, o_ref):
        i = pl.program_id(1)
        m0b = _lane_mask(256, BF16)
        for s in range(SWA_SUB):
            mask = _swa_mask(True if s else i > 0)
            rows = slice(s * 128, (s + 1) * 128)
            for kv in range(2):
                kst = _swa_keys(kp_ref, km_ref, s, kv, m0b)
                vst = _swa_keys(vp_ref, vm_ref, s, kv, m0b)
                for pr in range(2):
                    pair = kv * 2 + pr
                    cols = slice(pair * 128, (pair + 1) * 128)
                    probs, _ = _swa_scores(q_ref[rows, cols], kst, mask, sink_ref, pair)
                    o_ref[rows, cols] = _dot(jnp.concatenate(probs, axis=1).astype(BF16), vst)

    main = lambda b, i: (b * ni + i, 0)
    prev = lambda b, i: ((b * ni + i) * SWA_SUB - jnp.where(i > 0, 1, 0), 0)
    return pl.pallas_call(
        body, name="swa_fwd", grid=(nbatch, ni),
        in_specs=[pl.BlockSpec(memory_space=pltpu.SMEM), pl.BlockSpec((SWA_Q, 512), main),
                  pl.BlockSpec((128, 256), prev), pl.BlockSpec((SWA_Q, 256), main),
                  pl.BlockSpec((128, 256), prev), pl.BlockSpec((SWA_Q, 256), main)],
        out_specs=pl.BlockSpec((SWA_Q, 512), main),
        out_shape=jax.ShapeDtypeStruct((T, 512), F32),
        compiler_params=_params(2),
    )(sinks, qrot, k2, k2, v2, v2)


def _swa_bwd(sinks, qrot, k2, v2, doa, nbatch, seq):
    ni = seq // SWA_Q
    T = nbatch * seq

    def body(sink_ref, q_ref, kp_ref, km_ref, vp_ref, vm_ref, do_ref, dq_ref, dk_ref, dv_ref, ds_ref, dkc, dvc):
        b, i = pl.program_id(0), pl.program_id(1)

        @pl.when((b == 0) & (i == 0))
        def _():
            ds_ref[...] = jnp.zeros_like(ds_ref)

        @pl.when(i == 0)
        def _():
            dkc[...] = jnp.zeros_like(dkc)
            dvc[...] = jnp.zeros_like(dvc)

        @pl.when(i < ni)
        def _():
            m0b = _lane_mask(256, BF16)
            m0 = _lane_mask(128) > 0.5
            for kv in range(2):
                kcols = slice(kv * 128, (kv + 1) * 128)
                dk_own, dv_own = dkc[:, kcols], dvc[:, kcols]
                for s in range(SWA_SUB):
                    mask = _swa_mask(True if s else i > 0)
                    rows = slice(s * 128, (s + 1) * 128)
                    kst = _swa_keys(kp_ref, km_ref, s, kv, m0b)
                    vst = _swa_keys(vp_ref, vm_ref, s, kv, m0b)
                    dkst = jnp.zeros((512, 128), F32)
                    dvst = jnp.zeros((512, 128), F32)
                    for pr in range(2):
                        pair = kv * 2 + pr
                        cols = slice(pair * 128, (pair + 1) * 128)
                        q, do = q_ref[rows, cols], do_ref[rows, cols]
                        probs, psink = _swa_scores(q, kst, mask, sink_ref, pair)
                        dp_all = _dot_nt(do, vst)
                        ds_parts = []
                        for hh in range(2):
                            dp = dp_all[:, hh * 256:(hh + 1) * 256]
                            delta = jnp.sum(probs[hh] * dp, axis=1, keepdims=True)
                            ds_parts.append(probs[hh] * (dp - delta))
                            h = pair * 2 + hh
                            ds_ref[h:h + 1, :] = ds_ref[h:h + 1, :] - jnp.sum(psink[hh] * delta)
                        ds_all = jnp.concatenate(ds_parts, axis=1).astype(BF16)
                        p_all = jnp.concatenate(probs, axis=1).astype(BF16)
                        dq_ref[rows, cols] = _dot(ds_all, kst) * SCALE
                        dkst = dkst + _dot_tn(ds_all, q)
                        dvst = dvst + _dot_tn(p_all, do)
                    dk_ref[rows, kcols] = dk_own + jnp.where(m0, dkst[0:128], dkst[256:384])
                    dv_ref[rows, kcols] = dv_own + jnp.where(m0, dvst[0:128], dvst[256:384])
                    dk_own = jnp.where(m0, dkst[128:256], dkst[384:512])
                    dv_own = jnp.where(m0, dvst[128:256], dvst[384:512])
                dkc[:, kcols] = dk_own
                dvc[:, kcols] = dv_own

        @pl.when(i == ni)
        def _():
            dk_ref[...] = jnp.zeros_like(dk_ref)
            dv_ref[...] = jnp.zeros_like(dv_ref)
            dk_ref[0:128, :] = dkc[...]
            dv_ref[0:128, :] = dvc[...]

    main = lambda b, i: (b * ni + jnp.minimum(i, ni - 1), 0)
    prev = lambda b, i: ((b * ni + jnp.minimum(i, ni - 1)) * SWA_SUB - jnp.where(jnp.minimum(i, ni - 1) > 0, 1, 0), 0)
    shifted = lambda b, i: (b * (ni + 1) + i, 0)
    tpad = nbatch * (ni + 1) * SWA_Q
    return pl.pallas_call(
        body, name="swa_bwd", grid=(nbatch, ni + 1),
        in_specs=[pl.BlockSpec(memory_space=pltpu.SMEM), pl.BlockSpec((SWA_Q, 512), main),
                  pl.BlockSpec((128, 256), prev), pl.BlockSpec((SWA_Q, 256), main),
                  pl.BlockSpec((128, 256), prev), pl.BlockSpec((SWA_Q, 256), main),
                  pl.BlockSpec((SWA_Q, 512), main)],
        out_specs=[pl.BlockSpec((SWA_Q, 512), main), pl.BlockSpec((SWA_Q, 256), shifted),
                   pl.BlockSpec((SWA_Q, 256), shifted), _const_spec((8, 128))],
        out_shape=[jax.ShapeDtypeStruct((T, 512), F32), jax.ShapeDtypeStruct((tpad, 256), F32),
                   jax.ShapeDtypeStruct((tpad, 256), F32), jax.ShapeDtypeStruct((8, 128), F32)],
        scratch_shapes=[pltpu.VMEM((128, 256), F32), pltpu.VMEM((128, 256), F32)],
        compiler_params=_params(2),
    )(sinks, qrot, k2, k2, v2, v2, doa)


def _unshift(dkpad, nbatch, seq):
    return dkpad.reshape(nbatch, seq + SWA_Q, 256)[:, BLOCK:BLOCK + seq].reshape(nbatch * seq, 256)


SB_T = 256


def _sb_masks(r0, rows):
    r = lax.broadcasted_iota(jnp.int32, (rows, 2 * SB_T), 0) + r0
    c = lax.broadcasted_iota(jnp.int32, (rows, 2 * SB_T), 1)
    return (c & (SB_T - 1)) < r


def _sb_logits(neg_q, kst):
    nz = _dot_nt(neg_q, kst)
    sign = jnp.uint32(0x80000000)
    neg_abs = lax.bitcast_convert_type(lax.bitcast_convert_type(nz, jnp.uint32) | sign, F32)
    return nz, jnp.minimum(nz, 0.0) - jnp.log(1.0 + jnp.exp(neg_abs))


SB_NP = 4
SB_RC = 1
SB_RB = SB_T // SB_RC


def _sb_chains():
    return [(slice(pp * 128, (pp + 1) * 128), slice(rc * SB_RB, (rc + 1) * SB_RB), rc * SB_RB)
            for pp in range(SB_NP) for rc in range(SB_RC)]


def _pair_rows(ref, j, cols, m0b):
    kj = ref[pl.ds(pl.multiple_of(j * SB_T, SB_T), SB_T), cols]
    k0 = kj * m0b
    return jnp.concatenate([k0, kj - k0], axis=0)


def _bcast2(c0, c1):
    rows = c0.shape[0]
    return jnp.concatenate([jnp.broadcast_to(c0, (rows, SB_T)), jnp.broadcast_to(c1, (rows, SB_T))], axis=1)


def _rowsum2(x):
    return jnp.sum(x[:, 0:SB_T], axis=1, keepdims=True), jnp.sum(x[:, SB_T:2 * SB_T], axis=1, keepdims=True)


def _scan2(x, tri2):
    outs = []
    for h in range(2):
        xh = x[:, h * SB_T:(h + 1) * SB_T]
        hi = xh.astype(BF16)
        lo = (xh - hi.astype(F32)).astype(BF16)
        outs.append(_dot(jnp.concatenate([hi, lo], axis=1), tri2))
    return jnp.concatenate(outs, axis=1)


def _scan1(x, tri):
    xb = x.astype(BF16)
    return jnp.concatenate([_dot(xb[:, h * SB_T:(h + 1) * SB_T], tri) for h in range(2)], axis=1)


def _sb_fwd(qb, kb, vb, ublk, nbatch, seq):
    nb = seq // SB_T
    T = nbatch * seq

    def body(q_ref, k_ref, v_ref, u_ref, o_ref, ctab_ref):
        i = pl.program_id(2)
        m0b = _lane_mask(SB_T, BF16)
        u = u_ref[...]
        lane = lax.broadcasted_iota(jnp.int32, (SB_RB, 128), 1)
        ctab_ref[...] = jnp.zeros_like(ctab_ref)
        chains = _sb_chains()
        neg_qs = [-q_ref[rows, cols] for cols, rows, _ in chains]

        def tile(j, carries, diag):
            out = []
            for (cols, rows, r0), neg_q, (c0, c1, acc) in zip(chains, neg_qs, carries):
                if not diag:
                    ctab_ref[rows, cols] = jnp.where(lane == j, c0, jnp.where(lane == nb + j, c1, ctab_ref[rows, cols]))
                kst = _pair_rows(k_ref, j, cols, m0b)
                vst = _pair_rows(v_ref, j, cols, m0b)
                nz, lb = _sb_logits(neg_q, kst)
                if diag:
                    mask = _sb_masks(r0, SB_RB)
                    lb = jnp.where(mask, lb, 0.0)
                incl = _scan2(lb, u)
                w = jnp.exp(incl + _bcast2(c0, c1) - nz)
                if diag:
                    w = jnp.where(mask, w, 0.0)
                out.append((c0 + incl[:, 0:1], c1 + incl[:, SB_T:SB_T + 1], acc + _dot(w.astype(BF16), vst)))
            return tuple(out)

        zc = jnp.zeros((SB_RB, 1), F32)
        carries = tile(i, ((zc, zc, jnp.zeros((SB_RB, 128), F32)),) * len(chains), True)
        carries = lax.fori_loop(0, i, lambda jj, cr: tile(i - 1 - jj, cr, False), carries)
        for (cols, rows, _), carry in zip(chains, carries):
            o_ref[rows, cols] = carry[2]

    wide = 128 * SB_NP
    blk = lambda b, g, i: (b * nb + i, g)
    full = lambda b, g, i: (b, g)
    return pl.pallas_call(
        body, name="sb_fwd", grid=(nbatch, 4 // SB_NP, nb),
        in_specs=[pl.BlockSpec((SB_T, wide), blk), pl.BlockSpec((seq, wide), full), pl.BlockSpec((seq, wide), full),
                  _const_spec((2 * SB_T, SB_T))],
        out_specs=[pl.BlockSpec((SB_T, wide), blk), pl.BlockSpec((SB_T, wide), blk)],
        out_shape=[jax.ShapeDtypeStruct((T, 512), F32), jax.ShapeDtypeStruct((T, 512), F32)],
        compiler_params=_params(3),
    )(qb, kb, vb, ublk)


def _sb_bwd(qb, kb, vb, dob, ctab, ublk, pblk, nbatch, seq):
    nb = seq // SB_T
    T = nbatch * seq

    def body(q_ref, k_ref, v_ref, do_ref, ctab_ref, u_ref, up_ref, dq_ref, dk_ref, dv_ref):
        i = pl.program_id(2)

        @pl.when(i == 0)
        def _():
            dk_ref[...] = jnp.zeros_like(dk_ref)
            dv_ref[...] = jnp.zeros_like(dv_ref)

        m0b = _lane_mask(SB_T, BF16)
        m0 = _lane_mask(SB_T) > 0.5
        u, up = u_ref[...], up_ref[...]
        lane = lax.broadcasted_iota(jnp.int32, (SB_RB, 128), 1)
        chains = _sb_chains()
        neg_qs = [-q_ref[rows, cols] for cols, rows, _ in chains]

        def tile(j, carries, diag):
            out = []
            for (cols, rows, r0), neg_q, (s0, s1, dq) in zip(chains, neg_qs, carries):
                q, do = q_ref[rows, cols], do_ref[rows, cols]
                kst = _pair_rows(k_ref, j, cols, m0b)
                vst = _pair_rows(v_ref, j, cols, m0b)
                nz, lb = _sb_logits(neg_q, kst)
                one_minus_beta = jnp.exp(lb)
                if diag:
                    mask = _sb_masks(r0, SB_RB)
                    lb = jnp.where(mask, lb, 0.0)
                    w = jnp.where(mask, jnp.exp(_scan2(lb, u) - nz), 0.0)
                else:
                    ct = ctab_ref[rows, cols]
                    c0 = jnp.sum(jnp.where(lane == j, ct, 0.0), axis=1, keepdims=True)
                    c1 = jnp.sum(jnp.where(lane == nb + j, ct, 0.0), axis=1, keepdims=True)
                    w = jnp.exp(_scan2(lb, u) + _bcast2(c0, c1) - nz)
                e = _dot_nt(do, vst) * w
                dlb = _bcast2(s0, s1) + _scan1(e, up)
                dz = (e + dlb) * one_minus_beta - dlb
                if diag:
                    dz = jnp.where(mask, dz, 0.0)
                dzb = dz.astype(BF16)
                dkst = _dot_tn(dzb, q)
                dvst = _dot_tn(w.astype(BF16), do)
                keys = pl.ds(pl.multiple_of(j * SB_T, SB_T), SB_T)
                dk_ref[keys, cols] = dk_ref[keys, cols] + jnp.where(m0, dkst[0:SB_T], dkst[SB_T:2 * SB_T])
                dv_ref[keys, cols] = dv_ref[keys, cols] + jnp.where(m0, dvst[0:SB_T], dvst[SB_T:2 * SB_T])
                x0, x1 = _rowsum2(e)
                out.append((s0 + x0, s1 + x1, dq + _dot(dzb, kst)))
            return tuple(out)

        zc = jnp.zeros((SB_RB, 1), F32)
        carries = lax.fori_loop(0, i, lambda j, cr: tile(j, cr, False),
                                ((zc, zc, jnp.zeros((SB_RB, 128), F32)),) * len(chains))
        carries = tile(i, carries, True)
        for (cols, rows, _), carry in zip(chains, carries):
            dq_ref[rows, cols] = carry[2] * SCALE

    wide = 128 * SB_NP
    blk = lambda b, g, i: (b * nb + i, g)
    full = lambda b, g, i: (b, g)
    return pl.pallas_call(
        body, name="sb_bwd", grid=(nbatch, 4 // SB_NP, nb),
        in_specs=[pl.BlockSpec((SB_T, wide), blk), pl.BlockSpec((seq, wide), full), pl.BlockSpec((seq, wide), full),
                  pl.BlockSpec((SB_T, wide), blk), pl.BlockSpec((SB_T, wide), blk),
                  _const_spec((2 * SB_T, SB_T)), _const_spec((SB_T, SB_T))],
        out_specs=[pl.BlockSpec((SB_T, wide), blk), pl.BlockSpec((seq, wide), full), pl.BlockSpec((seq, wide), full)],
        out_shape=[jax.ShapeDtypeStruct((T, 512), F32)] * 3,
        compiler_params=_params(3),
    )(qb, kb, vb, dob, ctab, ublk, pblk)


def _sigmoid(g):
    return 1.0 / (1.0 + jnp.exp(-g))


def _out_proj(oa, ob, ga, gb, x2, tgt, wout_bf):
    T = x2.shape[0]

    def body(oa_ref, ob_ref, ga_ref, gb_ref, x_ref, t_ref, w_ref, y_ref, dout_ref, loss_ref):
        @pl.when(pl.program_id(0) == 0)
        def _():
            loss_ref[...] = jnp.zeros_like(loss_ref)

        ga, gb = ga_ref[...], gb_ref[...]
        ya = (oa_ref[...] * (ga * _sigmoid(ga))).astype(BF16)
        yb = (ob_ref[...] * (gb * _sigmoid(gb))).astype(BF16)
        y_ref[:, 0:512] = ya
        y_ref[:, 512:1024] = yb
        out = x_ref[...] + _dot(ya, w_ref[0:512, :]) + _dot(yb, w_ref[512:1024, :])
        diff = out - t_ref[...]
        dout_ref[...] = diff * (1.0 / D_MODEL)
        loss_ref[...] = loss_ref[...] + jnp.sum(diff * diff) * (0.5 / D_MODEL)

    rows = lambda w: pl.BlockSpec((TM, w), lambda i: (i, 0))
    return pl.pallas_call(
        body, name="out_proj", grid=(T // TM,),
        in_specs=[rows(512), rows(512), rows(512), rows(512), rows(D_MODEL), rows(D_MODEL),
                  _const_spec((D_MODEL, D_MODEL))],
        out_specs=[rows(D_MODEL), rows(D_MODEL), _const_spec((8, 128))],
        out_shape=[jax.ShapeDtypeStruct((T, D_MODEL), BF16), jax.ShapeDtypeStruct((T, D_MODEL), F32),
                   jax.ShapeDtypeStruct((8, 128), F32)],
        compiler_params=_params(1),
    )(oa, ob, ga, gb, x2, tgt, wout_bf)


def _out_proj_bwd(dout, y, oa, ob, ga, gb, wout_bf):
    T = dout.shape[0]

    def body(dout_ref, y_ref, oa_ref, ob_ref, ga_ref, gb_ref, w_ref, doa_ref, dob_ref, dga_ref, dgb_ref, dw_ref):
        @pl.when(pl.program_id(0) == 0)
        def _():
            dw_ref[...] = jnp.zeros_like(dw_ref)

        db = dout_ref[...].astype(BF16)
        dw_ref[...] = dw_ref[...] + _dot_tn(y_ref[...], db)
        for o_ref, g_ref, do_ref, dg_ref, lo in ((oa_ref, ga_ref, doa_ref, dga_ref, 0), (ob_ref, gb_ref, dob_ref, dgb_ref, 512)):
            dy = _dot_nt(db, w_ref[lo:lo + 512, :])
            g = g_ref[...]
            sg = _sigmoid(g)
            do_ref[...] = (dy * (g * sg)).astype(BF16)
            dg_ref[...] = (dy * o_ref[...] * (sg * (1.0 + g * (1.0 - sg)))).astype(BF16)

    rows = lambda w: pl.BlockSpec((TM, w), lambda i: (i, 0))
    return pl.pallas_call(
        body, name="out_proj_bwd", grid=(T // TM,),
        in_specs=[rows(D_MODEL), rows(D_MODEL), rows(512), rows(512), rows(512), rows(512),
                  _const_spec((D_MODEL, D_MODEL))],
        out_specs=[rows(512)] * 4 + [_const_spec((D_MODEL, D_MODEL))],
        out_shape=[jax.ShapeDtypeStruct((T, 512), BF16)] * 4 + [jax.ShapeDtypeStruct((D_MODEL, D_MODEL), F32)],
        compiler_params=_params(1),
    )(dout, y, oa, ob, ga, gb, wout_bf)


def _qk_grad(qraw, kraw, dqrot, dk2, dv2, qg512, kg128, g512, cos, sa, sb):
    T = qraw.shape[0]

    def body(qraw_ref, kraw_ref, dqrot_ref, dk2_ref, dv2_ref, qg_ref, kg_ref, g_ref, cos_ref, sa_ref, sb_ref,
             dqa_ref, dkv_ref, dqg_ref, dkg_ref):
        @pl.when(pl.program_id(0) == 0)
        def _():
            dqg_ref[...] = jnp.zeros_like(dqg_ref)
            dkg_ref[...] = jnp.zeros_like(dkg_ref)

        cosv, sav, sbv = cos_ref[...], sa_ref[...], sb_ref[...]
        m0 = _lane_mask(TM) > 0.5

        def head_norm_bwd(raw, dn_rot, gmat, gain):
            r = lax.rsqrt(_dot_hl(raw * raw, gmat) + EPS)
            n = raw * r
            dn = dn_rot * gain
            return r * (dn - n * _dot_hl(dn * n, gmat)), jnp.sum(dn_rot * n, axis=0, keepdims=True)

        def fold(ref):
            a, b = ref[:, 0:128], ref[:, 128:256]
            return jnp.where(m0, a + pltpu.roll(a, 64, 1), b + pltpu.roll(b, 64, 1))

        dqn = jnp.concatenate([_rope_bwd(dqrot_ref[:, s * 128:(s + 1) * 128], cosv, sav, sbv) for s in range(4)], axis=1)
        dqa, dqg = head_norm_bwd(qraw_ref[...], dqn, g_ref[...], qg_ref[...])
        dka, dkg = head_norm_bwd(kraw_ref[...], _rope_bwd(fold(dk2_ref), cosv, sav, sbv), g_ref[0:128, 0:128], kg_ref[...])
        dqa_ref[...] = dqa.astype(BF16)
        dkv_ref[:, 0:128] = dka.astype(BF16)
        dkv_ref[:, 128:256] = fold(dv2_ref).astype(BF16)
        dqg_ref[...] = dqg_ref[...] + dqg
        dkg_ref[...] = dkg_ref[...] + dkg

    rows = lambda w: pl.BlockSpec((TM, w), lambda i: (i, 0))
    return pl.pallas_call(
        body, name="qk_grad", grid=(T // TM,),
        in_specs=[rows(512), rows(128), rows(512), rows(256), rows(256), _const_spec((1, 512)), _const_spec((1, 128)),
                  _const_spec((512, 512)), rows(128), rows(128), rows(128)],
        out_specs=[rows(512), rows(256), _const_spec((1, 512)), _const_spec((1, 128))],
        out_shape=[jax.ShapeDtypeStruct((T, 512), BF16), jax.ShapeDtypeStruct((T, 256), BF16),
                   jax.ShapeDtypeStruct((1, 512), F32), jax.ShapeDtypeStruct((1, 128), F32)],
        compiler_params=_params(1),
    )(qraw, kraw, dqrot, dk2, dv2, qg512, kg128, g512, cos, sa, sb)


_PIECES = ((QA, 512), (KA, 256), (GA, 512), (QB, 512), (KB, 512), (VB, 512), (GB, 512))


def _w_in_grad(h, pieces):
    T = h.shape[0]

    def body(h_ref, *refs):
        dw_ref = refs[-1]

        @pl.when(pl.program_id(0) == 0)
        def _():
            dw_ref[...] = jnp.zeros_like(dw_ref)

        hb = h_ref[...]
        for (lo, width), p_ref in zip(_PIECES, refs[:-1]):
            dw_ref[lo:lo + width, :] = dw_ref[lo:lo + width, :] + _dot_tn(p_ref[...].astype(BF16), hb)

    rows = lambda w: pl.BlockSpec((TM, w), lambda i: (i, 0))
    return pl.pallas_call(
        body, name="w_in_grad", grid=(T // TM,),
        in_specs=[rows(D_MODEL)] + [rows(w) for _, w in _PIECES],
        out_specs=_const_spec((IN_WIDTH, D_MODEL)),
        out_shape=jax.ShapeDtypeStruct((IN_WIDTH, D_MODEL), F32),
        compiler_params=_params(1),
    )(h, *pieces)


def _x_grad(x2, dout, pieces, w_bf, ng, cin4, cout4):
    T = x2.shape[0]
    npc = len(_PIECES)
    steps = T // TM

    def body(x_ref, dout_ref, *refs):
        w_ref, ng_ref, cin_ref, cout_ref, gx_ref, dng_ref, rin_ref, rout_ref, ssem, rsem = refs[npc:]
        step = pl.program_id(0)
        _, _, c, chips = _place()
        sems = (ssem, rsem)

        def copies():
            out = []
            for j, (cx, cy) in enumerate(chips):
                k = 2 * cx + cy
                out.append(_remote(cin_ref.at[k], rin_ref.at[j], sems, 2 * j, (cx, cy, c)))
                out.append(_remote(cout_ref.at[k], rout_ref.at[j], sems, 2 * j + 1, (cx, cy, c)))
            return out

        @pl.when(step == 0)
        def _():
            dng_ref[...] = jnp.zeros_like(dng_ref)
            for cp in copies():
                cp.start()

        dh = jnp.zeros((TM, D_MODEL), F32)
        for (lo, width), p_ref in zip(_PIECES, refs[:npc]):
            dh = dh + _dot(p_ref[...].astype(BF16), w_ref[lo:lo + width, :])
        xb = x_ref[...]
        r = lax.rsqrt(jnp.mean(xb * xb, axis=-1, keepdims=True) + EPS)
        n = xb * r
        dn = dh * ng_ref[...]
        gx_ref[...] = dout_ref[...] + r * (dn - n * jnp.mean(dn * n, axis=-1, keepdims=True))
        dng_ref[...] = dng_ref[...] + jnp.sum(dh * n, axis=0, keepdims=True)

        @pl.when(step == steps - 1)
        def _():
            for cp in copies():
                cp.wait_recv()
            for cp in copies():
                cp.wait_send()

    rows = lambda w: pl.BlockSpec((TM, w), lambda i: (i, 0))
    hbm = pl.BlockSpec(memory_space=pl.ANY)
    return pl.pallas_call(
        body, name="x_grad", grid=(steps,),
        in_specs=[rows(D_MODEL), rows(D_MODEL)] + [rows(w) for _, w in _PIECES]
        + [_const_spec((IN_WIDTH, D_MODEL)), _const_spec((1, D_MODEL)), hbm, hbm],
        out_specs=[rows(D_MODEL), _const_spec((1, D_MODEL)), hbm, hbm],
        out_shape=[jax.ShapeDtypeStruct((T, D_MODEL), F32), jax.ShapeDtypeStruct((1, D_MODEL), F32),
                   jax.ShapeDtypeStruct((3,) + cin4.shape[1:], cin4.dtype),
                   jax.ShapeDtypeStruct((3,) + cout4.shape[1:], cout4.dtype)],
        scratch_shapes=[pltpu.SemaphoreType.DMA((6,)), pltpu.SemaphoreType.DMA((6,))],
        compiler_params=_params(1),
    )(x2, dout, *pieces, w_bf, ng, cin4, cout4)


HBM = pl.BlockSpec(memory_space=pl.ANY)


def _place():
    x, y, c = lax.axis_index("x"), lax.axis_index("y"), lax.axis_index("c")
    chips = [(1 - x, y), (x, 1 - y), (1 - x, 1 - y)]
    return x, y, c, chips


def _remote(src, dst, sems, k, to):
    return pltpu.make_async_remote_copy(src_ref=src, dst_ref=dst, send_sem=sems[0].at[k], recv_sem=sems[1].at[k],
                                        device_id=to, device_id_type=MESH)


def _gather_weights(shards):
    n = len(shards)

    def body(*refs):
        srcs, dsts, (ssem, rsem) = refs[:n], refs[n:2 * n], refs[2 * n:]
        x, y, c, chips = _place()
        me, sib = 2 * x + y, (x, y, 1 - c)
        sems = (ssem, rsem)
        sent = []
        for j, (cx, cy) in enumerate(chips):
            for a in range(n):
                sent.append(_remote(srcs[a].at[c], dsts[a].at[me, c], sems, n * j + a, (cx, cy, c)))
        for cp in sent:
            cp.start()
        for j, (cx, cy) in enumerate(chips):
            k = 2 * cx + cy
            for a in range(n):
                _remote(srcs[a].at[c], dsts[a].at[k, c], sems, n * j + a, sib).wait_recv()
                fwd = _remote(dsts[a].at[k, c], dsts[a].at[k, c], sems, 3 * n + n * j + a, sib)
                fwd.start()
                sent.append(fwd)
        for j, (cx, cy) in enumerate(chips):
            k = 2 * cx + cy
            for a in range(n):
                _remote(srcs[a].at[c], dsts[a].at[k, 1 - c], sems, 3 * n + n * j + a, sib).wait_recv()
        for cp in sent:
            cp.wait_send()

    gathered = pl.pallas_call(
        body, name="gather_weights", in_specs=[HBM] * n, out_specs=[HBM] * n,
        out_shape=[jax.ShapeDtypeStruct((4,) + s.shape, s.dtype) for s in shards],
        scratch_shapes=[pltpu.SemaphoreType.DMA((6 * n,)), pltpu.SemaphoreType.DMA((6 * n,))],
    )(*shards)
    me = 2 * lax.axis_index("x") + lax.axis_index("y")
    return [lax.dynamic_update_slice(g, s[None], (me, 0, 0, 0)) for g, s in zip(gathered, shards)]


def _pair_exchange(grads):
    n = len(grads)

    def body(*refs):
        srcs, dsts, (ssem, rsem) = refs[:n], refs[n:2 * n], refs[2 * n:]
        x, y, c, _ = _place()
        sems = (ssem, rsem)
        sib = (x, y, 1 - c)
        sent = [_remote(srcs[a].at[:, pl.ds(1 - c, 1)], dsts[a], sems, a, sib) for a in range(n)]
        for cp in sent:
            cp.start()
        for cp in sent:
            cp.wait()

    return pl.pallas_call(
        body, name="pair_exchange", in_specs=[HBM] * n, out_specs=[HBM] * n,
        out_shape=[jax.ShapeDtypeStruct((4, 1) + g.shape[2:], g.dtype) for g in grads],
        scratch_shapes=[pltpu.SemaphoreType.DMA((n,)), pltpu.SemaphoreType.DMA((n,))],
    )(*grads)


def _share_halves(hin, hout, small):
    def body(hin_ref, hout_ref, small_ref, oin_ref, oout_ref, sall_ref, ssem, rsem, lsem):
        x, y, c, _ = _place()
        sems = (ssem, rsem)
        sib = (x, y, 1 - c)
        me = 4 * x + 2 * y + c
        own = pltpu.make_async_copy(small_ref, sall_ref.at[me], lsem.at[0])
        own.start()
        sent = [_remote(hin_ref, oin_ref, sems, 0, sib), _remote(hout_ref, oout_ref, sems, 1, sib)]
        flips = [(fx, fy, fc) for fx in (0, 1) for fy in (0, 1) for fc in (0, 1)][1:]
        for k, (fx, fy, fc) in enumerate(flips):
            sent.append(_remote(small_ref, sall_ref.at[me], sems, 2 + k, (x ^ fx, y ^ fy, c ^ fc)))
        for cp in sent:
            cp.start()
        _remote(hin_ref, oin_ref, sems, 0, sib).wait_recv()
        _remote(hout_ref, oout_ref, sems, 1, sib).wait_recv()
        for k, (fx, fy, fc) in enumerate(flips):
            src = 4 * (x ^ fx) + 2 * (y ^ fy) + (c ^ fc)
            _remote(small_ref, sall_ref.at[src], sems, 2 + k, sib).wait_recv()
        for cp in sent:
            cp.wait_send()
        own.wait()

    return pl.pallas_call(
        body, name="share_halves", in_specs=[HBM, HBM, HBM], out_specs=[HBM, HBM, HBM],
        out_shape=[jax.ShapeDtypeStruct(hin.shape, F32), jax.ShapeDtypeStruct(hout.shape, F32),
                   jax.ShapeDtypeStruct((8,) + small.shape, F32)],
        scratch_shapes=[pltpu.SemaphoreType.DMA((9,)), pltpu.SemaphoreType.DMA((9,)), pltpu.SemaphoreType.DMA((1,))],
    )(hin, hout, small)


def _add_half(cidx, full4, recv4):
    _, _, rows, width = full4.shape

    def body(c_ref, a_ref, b_ref, o_ref):
        o_ref[0] = (a_ref[0, 0] + b_ref[0, 0]).astype(BF16)

    return pl.pallas_call(
        body, name=f"add_half_{rows}",
        grid_spec=pltpu.PrefetchScalarGridSpec(
            num_scalar_prefetch=1, grid=(4,),
            in_specs=[pl.BlockSpec((1, 1, rows, width), lambda k, c: (k, c[0], 0, 0)),
                      pl.BlockSpec((1, 1, rows, width), lambda k, c: (k, 0, 0, 0))],
            out_specs=pl.BlockSpec((1, rows, width), lambda k, c: (k, 0, 0))),
        out_shape=jax.ShapeDtypeStruct((4, rows, width), BF16),
        compiler_params=_params(1),
    )(cidx, full4, recv4)


def _sum_chips(chip, own4, recv3):
    _, rows, width = recv3.shape
    rb = rows // 2

    def body(k_ref, a_ref, r_ref, o_ref):
        acc = a_ref[0].astype(F32)
        for s in range(3):
            acc = acc + r_ref[s].astype(F32)
        o_ref[...] = acc

    return pl.pallas_call(
        body, name=f"sum_chips_{rows}",
        grid_spec=pltpu.PrefetchScalarGridSpec(
            num_scalar_prefetch=1, grid=(rows // rb,),
            in_specs=[pl.BlockSpec((1, rb, width), lambda i, k: (k[0], i, 0)),
                      pl.BlockSpec((3, rb, width), lambda i, k: (0, i, 0))],
            out_specs=pl.BlockSpec((rb, width), lambda i, k: (i, 0))),
        out_shape=jax.ShapeDtypeStruct((rows, width), F32),
        compiler_params=_params(1),
    )(chip, own4, recv3)


def _sum_slots(r4):
    n, rows, width = r4.shape

    def body(r_ref, o_ref):
        acc = r_ref[0]
        for s in range(1, n):
            acc = acc + r_ref[s]
        o_ref[...] = acc

    return pl.pallas_call(
        body, name=f"sum_slots_{n}_{rows}_{width}", grid=(1,),
        in_specs=[pl.BlockSpec((n, rows, width), lambda i: (0, 0, 0))],
        out_specs=pl.BlockSpec((rows, width), lambda i: (0, 0)),
        out_shape=jax.ShapeDtypeStruct((rows, width), F32),
        compiler_params=_params(1),
    )(r4)


def _adam_math(w, g, m, v):
    c1 = 1.0 - ADAM_B1 ** ADAM_STEP
    c2 = 1.0 - ADAM_B2 ** ADAM_STEP
    nm = ADAM_B1 * m + (1.0 - ADAM_B1) * g
    nv = ADAM_B2 * v + (1.0 - ADAM_B2) * (g * g)
    return -ADAM_LR * ((nm / c1) / (jnp.sqrt(nv / c2) + ADAM_EPS) + ADAM_WD * w), nm, nv


def _adamw(w, g, m, v):
    rows, width = w.shape

    def body(w_ref, g_ref, m_ref, v_ref, d_ref, nm_ref, nv_ref):
        d_ref[...], nm_ref[...], nv_ref[...] = _adam_math(w_ref[...], g_ref[...], m_ref[...], v_ref[...])

    spec = pl.BlockSpec((rows, width), lambda i: (0, 0))
    return pl.pallas_call(
        body, name=f"adamw_{rows}_{width}", grid=(1,),
        in_specs=[spec] * 4, out_specs=[spec] * 3,
        out_shape=[jax.ShapeDtypeStruct((rows, width), F32)] * 3,
        compiler_params=_params(1),
    )(w, g, m, v)


def _adamw_halves(cidx, w, own, recv, m, v):
    rows, width = w.shape
    rb = rows // 4

    def body(c_ref, w_ref, own_ref, recv_ref, m_ref, v_ref, g_ref, d_ref, nm_ref, nv_ref):
        mine = (pl.program_id(0) // 2) == c_ref[0]
        g = jnp.where(mine, own_ref[...], recv_ref[...])
        g_ref[...] = g
        d_ref[...], nm_ref[...], nv_ref[...] = _adam_math(w_ref[...], g, m_ref[...], v_ref[...])

    full = pl.BlockSpec((rb, width), lambda i, c: (i, 0))
    half = pl.BlockSpec((rb, width), lambda i, c: (i % 2, 0))
    return pl.pallas_call(
        body, name=f"adamw_halves_{rows}",
        grid_spec=pltpu.PrefetchScalarGridSpec(
            num_scalar_prefetch=1, grid=(4,),
            in_specs=[full, half, half, full, full], out_specs=[full] * 4),
        out_shape=[jax.ShapeDtypeStruct((rows, width), F32)] * 4,
        compiler_params=_params(1),
    )(cidx, w, own, recv, m, v)


def _rope_tables(positions):
    half = HEAD_DIM // 2
    inv_freq = ROPE_THETA ** (-jnp.arange(half, dtype=F32) * 2.0 / HEAD_DIM)
    ang = positions.astype(F32).reshape(-1, 1) * inv_freq
    cos, sin, zero = jnp.cos(ang), jnp.sin(ang), jnp.zeros_like(ang)
    return (jnp.concatenate([cos] * 4, axis=1), jnp.concatenate([-sin, zero] * 2, axis=1),
            jnp.concatenate([zero, sin] * 2, axis=1))


def _constants():
    idx = jnp.arange(512)
    g512 = jnp.where(idx[:, None] // HEAD_DIM == idx[None, :] // HEAD_DIM, 1.0 / HEAD_DIM, 0.0).astype(BF16)
    j = jnp.arange(SB_T)
    ublk = jnp.where(j[:, None] >= j[None, :], 1.0, 0.0).astype(BF16)
    pblk = jnp.where(j[:, None] < j[None, :], 1.0, 0.0).astype(BF16)
    return g512, jnp.concatenate([ublk, ublk], axis=0), pblk


def _pad_rows(v, width):
    return jnp.pad(v, ((0, 0), (0, width - v.shape[1])))


def _forward_backward(x2, tgt, positions, norm_gain, q_norm_gain, k_norm_gain, sinks, w_bf, wout_bf, nbatch, seq):
    cos, sa, sb = _rope_tables(positions)
    g512, ublk, pblk = _constants()
    qg512 = jnp.tile(q_norm_gain, (1, 8))
    kg128 = jnp.tile(k_norm_gain, (1, 2))
    sink1 = sinks.reshape(8)

    h, qraw, kraw, qrot, k2, v2, ga, qb, kb, vb, gb = _norm_proj(x2, norm_gain, w_bf, qg512, kg128, g512, cos, sa, sb)
    oa = _swa_fwd(sink1, qrot, k2, v2, nbatch, seq)
    ob, ctab = _sb_fwd(qb, kb, vb, ublk, nbatch, seq)
    y, dout, loss_acc = _out_proj(oa, ob, ga, gb, x2, tgt, wout_bf)

    doa, dob, dga, dgb, dwout = _out_proj_bwd(dout, y, oa, ob, ga, gb, wout_bf)
    dqrot, dk2, dv2, dsink = _swa_bwd(sink1, qrot, k2, v2, doa, nbatch, seq)
    dqb, dkb, dvb = _sb_bwd(qb, kb, vb, dob, ctab, ublk, pblk, nbatch, seq)
    dk2, dv2 = _unshift(dk2, nbatch, seq), _unshift(dv2, nbatch, seq)
    dqa, dkv, dqg, dkg = _qk_grad(qraw, kraw, dqrot, dk2, dv2, qg512, kg128, g512, cos, sa, sb)
    pieces = (dqa, dkv, dga, dqb, dkb, dvb, dgb)
    dwin_t = _w_in_grad(h, pieces)
    dqg64 = dqg.reshape(8, HEAD_DIM).sum(axis=0, keepdims=True)
    dkg64 = dkg.reshape(2, HEAD_DIM).sum(axis=0, keepdims=True)
    small = jnp.concatenate([_pad_rows(dqg64, D_MODEL), _pad_rows(dkg64, D_MODEL), _pad_rows(dsink[:, 0].reshape(1, 8), D_MODEL),
                             _pad_rows(loss_acc[0:1, 0:1], D_MODEL), jnp.zeros((3, D_MODEL), F32)], axis=0)
    return dwin_t, dwout, small, (x2, dout, pieces, w_bf, norm_gain)


def _reduce_grads(dwin_t, dwout, small7, x_grad_args):
    gin4 = dwin_t.reshape(4, 2, SHARD_IN // 2, D_MODEL)
    gout4 = dwout.reshape(4, 2, SHARD_OUT // 2, D_MODEL)
    rin, rout = _pair_exchange([gin4, gout4])
    cidx = lax.axis_index("c").astype(jnp.int32).reshape(1)
    chip = (2 * lax.axis_index("x") + lax.axis_index("y")).astype(jnp.int32).reshape(1)
    cin4, cout4 = _add_half(cidx, gin4, rin), _add_half(cidx, gout4, rout)
    gx, dng, rin3, rout3 = _x_grad(*x_grad_args, cin4, cout4)
    own_in, own_out = _sum_chips(chip, cin4, rin3), _sum_chips(chip, cout4, rout3)
    sib_in, sib_out, small_all = _share_halves(own_in, own_out, jnp.concatenate([dng, small7], axis=0))
    return gx, cidx, own_in, sib_in, own_out, sib_out, _sum_slots(small_all)


def kernel(x, positions, norm_gain, w_in, q_norm_gain, k_norm_gain, sinks, w_out, loss_target, m_norm_gain, m_w_in, m_q_norm_gain, m_k_norm_gain, m_sinks, m_w_out, v_norm_gain, v_w_in, v_q_norm_gain, v_k_norm_gain, v_sinks, v_w_out):
    nbatch, seq, _ = x.shape
    T = nbatch * seq
    x2 = x.reshape(T, D_MODEL)
    tgt = loss_target.reshape(T, D_MODEL)
    tr = lambda a: jnp.swapaxes(a[0], 0, 1)
    win_t, m_win_t, v_win_t = tr(w_in), tr(m_w_in), tr(v_w_in)

    ain, aout = _gather_weights([win_t.astype(BF16).reshape(2, SHARD_IN // 2, D_MODEL),
                                 w_out[0].astype(BF16).reshape(2, SHARD_OUT // 2, D_MODEL)])
    w_bf, wout_bf = ain.reshape(IN_WIDTH, D_MODEL), aout.reshape(D_MODEL, D_MODEL)

    dwin_t, dwout, small7, x_grad_args = _forward_backward(
        x2, tgt, positions, norm_gain, q_norm_gain, k_norm_gain, sinks, w_bf, wout_bf, nbatch, seq)

    gx, cidx, own_in, sib_in, own_out, sib_out, g_small = _reduce_grads(dwin_t, dwout, small7, x_grad_args)
    loss = g_small[4, 0]

    g_in, d_in, nm_in, nv_in = [jnp.swapaxes(a, 0, 1) for a in
                                _adamw_halves(cidx, win_t, own_in, sib_in, m_win_t, v_win_t)]
    g_out, d_out, nm_out, nv_out = _adamw_halves(cidx, w_out[0], own_out, sib_out, m_w_out[0], v_w_out[0])
    pack = lambda a, b, c_, d: jnp.concatenate(
        [a, _pad_rows(b, D_MODEL), _pad_rows(c_, D_MODEL), _pad_rows(d, D_MODEL), jnp.zeros((4, D_MODEL), F32)], axis=0)
    w_s = pack(norm_gain, q_norm_gain, k_norm_gain, sinks)
    m_s = pack(m_norm_gain, m_q_norm_gain, m_k_norm_gain, m_sinks)
    v_s = pack(v_norm_gain, v_q_norm_gain, v_k_norm_gain, v_sinks)
    d_s, nm_s, nv_s = _adamw(w_s, g_small, m_s, v_s)
    unpack = lambda a: (a[0:1, :], a[1:2, 0:HEAD_DIM], a[2:3, 0:HEAD_DIM], a[3:4, 0:8])

    g_ng, g_qg, g_kg, g_sk = unpack(g_small)
    d_ng, d_qg, d_kg, d_sk = unpack(d_s)
    m_ng, m_qg, m_kg, m_sk = unpack(nm_s)
    v_ng, v_qg, v_kg, v_sk = unpack(nv_s)
    return (loss, gx.reshape(nbatch, seq, D_MODEL),
            g_ng, g_in[None], g_qg, g_kg, g_sk, g_out[None],
            d_ng, d_in[None], d_qg, d_kg, d_sk, d_out[None],
            m_ng, nm_in[None], m_qg, m_kg, m_sk, nm_out[None],
            v_ng, nv_in[None], v_qg, v_kg, v_sk, nv_out[None])
```

```python
import functools
import math

import jax
import jax.numpy as jnp
from jax import lax
from jax.experimental import pallas as pl
from jax.experimental.pallas import tpu as pltpu

F32 = jnp.float32
BF16 = jnp.bfloat16

D_MODEL = 1024
HEAD_DIM = 64
BLOCK = 128
ROPE_THETA = 10000.0
EPS = 1e-6
QA, KA, VA, GA, QB, KB, VB, GB = 0, 512, 640, 768, 1280, 1792, 2304, 2816
IN_WIDTH = 3328
SHARD_IN = IN_WIDTH // 4
SHARD_OUT = D_MODEL // 4
SCALE = 1.0 / math.sqrt(HEAD_DIM)
NEG = -1e30

ADAM_LR, ADAM_B1, ADAM_B2, ADAM_EPS, ADAM_WD, ADAM_STEP = 0.001, 0.9, 0.999, 1e-08, 0.01, 10

TM = 256
VMEM_LIMIT = 56 * 1024 * 1024
MESH = pl.DeviceIdType.MESH


def _dot(a, b):
    return jnp.dot(a, b, preferred_element_type=F32)


def _dot_nt(a, b):
    return lax.dot_general(a, b, (((1,), (1,)), ((), ())), preferred_element_type=F32)


def _dot_tn(a, b):
    return lax.dot_general(a, b, (((0,), (0,)), ((), ())), preferred_element_type=F32)


def _dot_hl(a, m):
    hi = a.astype(BF16)
    lo = (a - hi.astype(F32)).astype(BF16)
    return _dot(hi, m) + _dot(lo, m)


def _params(n_axes=None, vmem=VMEM_LIMIT):
    sem = None if n_axes is None else ("arbitrary",) * n_axes
    return pltpu.CompilerParams(dimension_semantics=sem, vmem_limit_bytes=vmem)


def _const_spec(shape):
    nd = len(shape)
    return pl.BlockSpec(shape, lambda *_: (0,) * nd)


def _rope_fwd(x, cos, sa, sb):
    return x * cos + pltpu.roll(x, 96, 1) * sa + pltpu.roll(x, 32, 1) * sb


def _rope_bwd(d, cos, sa, sb):
    return d * cos - pltpu.roll(d, 96, 1) * sa - pltpu.roll(d, 32, 1) * sb


def _lane_mask(rows, dtype=F32):
    lane = lax.broadcasted_iota(jnp.int32, (rows, 128), 1)
    return jnp.where(lane < HEAD_DIM, 1.0, 0.0).astype(dtype)


def _norm_proj(x2, ng, w_bf, qg512, kg128, g512, cos, sa, sb):
    T = x2.shape[0]

    def body(x_ref, ng_ref, w_ref, qg_ref, kg_ref, g_ref, cos_ref, sa_ref, sb_ref,
             h_ref, qraw_ref, kraw_ref, qrot_ref, k2_ref, v2_ref, ga_ref, qb_ref, kb_ref, vb_ref, gb_ref):
        xb = x_ref[...]
        r = lax.rsqrt(jnp.mean(xb * xb, axis=-1, keepdims=True) + EPS)
        h = (xb * r * ng_ref[...]).astype(BF16)
        h_ref[...] = h
        cosv, sav, sbv = cos_ref[...], sa_ref[...], sb_ref[...]
        m0 = _lane_mask(TM)

        def dup(v):
            v0 = v * m0
            v1 = v - v0
            return v0 + pltpu.roll(v0, 64, 1), v1 + pltpu.roll(v1, 64, 1)

        qa = _dot_nt(h, w_ref[QA:KA, :])
        qraw_ref[...] = qa
        qn = qa * lax.rsqrt(_dot_hl(qa * qa, g_ref[...]) + EPS) * qg_ref[...]
        for s in range(4):
            qs = _rope_fwd(qn[:, s * 128:(s + 1) * 128], cosv, sav, sbv)
            qrot_ref[:, s * 128:(s + 1) * 128] = (qs * SCALE).astype(BF16)
        ka = _dot_nt(h, w_ref[KA:VA, :])
        kraw_ref[...] = ka
        kn = ka * lax.rsqrt(_dot_hl(ka * ka, g_ref[0:128, 0:128]) + EPS) * kg_ref[...]
        k0, k1 = dup(_rope_fwd(kn, cosv, sav, sbv))
        k2_ref[:, 0:128] = k0.astype(BF16)
        k2_ref[:, 128:256] = k1.astype(BF16)
        v0, v1 = dup(_dot_nt(h, w_ref[VA:GA, :]))
        v2_ref[:, 0:128] = v0.astype(BF16)
        v2_ref[:, 128:256] = v1.astype(BF16)
        ga_ref[...] = _dot_nt(h, w_ref[GA:QB, :])
        qb_ref[...] = (_dot_nt(h, w_ref[QB:KB, :]) * SCALE).astype(BF16)
        kb_ref[...] = _dot_nt(h, w_ref[KB:VB, :]).astype(BF16)
        vb_ref[...] = _dot_nt(h, w_ref[VB:GB, :]).astype(BF16)
        gb_ref[...] = _dot_nt(h, w_ref[GB:IN_WIDTH, :])

    def rows(w):
        return pl.BlockSpec((TM, w), lambda i: (i, 0))

    outs = [(D_MODEL, BF16), (512, F32), (128, F32), (512, BF16), (256, BF16), (256, BF16), (512, F32),
            (512, BF16), (512, BF16), (512, BF16), (512, F32)]
    return pl.pallas_call(
        body, name="norm_proj", grid=(T // TM,),
        in_specs=[rows(D_MODEL), _const_spec((1, D_MODEL)), _const_spec((IN_WIDTH, D_MODEL)), _const_spec((1, 512)),
                  _const_spec((1, 128)), _const_spec((512, 512)), rows(128), rows(128), rows(128)],
        out_specs=[rows(w) for w, _ in outs],
        out_shape=[jax.ShapeDtypeStruct((T, w), dt) for w, dt in outs],
        compiler_params=_params(1),
    )(x2, ng, w_bf, qg512, kg128, g512, cos, sa, sb)


SWA_Q = 512
SWA_SUB = SWA_Q // BLOCK


def _swa_scores(q, kst, mask, sink_ref, pair):
    s_all = _dot_nt(q, kst)
    probs, stats = [], []
    for hh in range(2):
        sink = sink_ref[pair * 2 + hh]
        s = jnp.where(mask, s_all[:, hh * 256:(hh + 1) * 256], NEG)
        m = jnp.maximum(jnp.max(s, axis=1, keepdims=True), sink)
        pe = jnp.exp(s - m)
        inv = 1.0 / (jnp.sum(pe, axis=1, keepdims=True) + jnp.exp(sink - m))
        probs.append(pe * inv)
        stats.append(jnp.exp(sink - m) * inv)
    return probs, stats


def _swa_mask(has_prev):
    r = lax.broadcasted_iota(jnp.int32, (128, 256), 0)
    c = lax.broadcasted_iota(jnp.int32, (128, 256), 1)
    band = (c > r) & (c <= r + 128)
    return band if has_prev is True else band & ((c >= 128) | has_prev)


def _swa_keys(prev_ref, main_ref, s, kv, m0b):
    cols = slice(kv * 128, (kv + 1) * 128)
    prev = prev_ref[:, cols] if s == 0 else main_ref[(s - 1) * 128:s * 128, cols]
    kc = jnp.concatenate([prev, main_ref[s * 128:(s + 1) * 128, cols]], axis=0)
    k0 = kc * m0b
    return jnp.concatenate([k0, kc - k0], axis=0)


def _swa_fwd(sinks, qrot, k2, v2, nbatch, seq):
    ni = seq // SWA_Q
    T = nbatch * seq

    def body(sink_ref, q_ref, kp_ref, km_ref, vp_ref, vm_ref, o_ref):
        i = pl.program_id(1)
        m0b = _lane_mask(256, BF16)
        for s in range(SWA_SUB):
            mask = _swa_mask(True if s else i > 0)
            rows = slice(s * 128, (s + 1) * 128)
            for kv in range(2):
                kst = _swa_keys(kp_ref, km_ref, s, kv, m0b)
                vst = _swa_keys(vp_ref, vm_ref, s, kv, m0b)
                for pr in range(2):
                    pair = kv * 2 + pr
                    cols = slice(pair * 128, (pair + 1) * 128)
                    probs, _ = _swa_scores(q_ref[rows, cols], kst, mask, sink_ref, pair)
                    o_ref[rows, cols] = _dot(jnp.concatenate(probs, axis=1).astype(BF16), vst)

    main = lambda b, i: (b * ni + i, 0)
    prev = lambda b, i: ((b * ni + i) * SWA_SUB - jnp.where(i > 0, 1, 0), 0)
    return pl.pallas_call(
        body, name="swa_fwd", grid=(nbatch, ni),
        in_specs=[pl.BlockSpec(memory_space=pltpu.SMEM), pl.BlockSpec((SWA_Q, 512), main),
                  pl.BlockSpec((128, 256), prev), pl.BlockSpec((SWA_Q, 256), main),
                  pl.BlockSpec((128, 256), prev), pl.BlockSpec((SWA_Q, 256), main)],
        out_specs=pl.BlockSpec((SWA_Q, 512), main),
        out_shape=jax.ShapeDtypeStruct((T, 512), F32),
        compiler_params=_params(2),
    )(sinks, qrot, k2, k2, v2, v2)


def _swa_bwd(sinks, qrot, k2, v2, doa, nbatch, seq):
    ni = seq // SWA_Q
    T = nbatch * seq

    def body(sink_ref, q_ref, kp_ref, km_ref, vp_ref, vm_ref, do_ref, dq_ref, dk_ref, dv_ref, ds_ref, dkc, dvc):
        b, i = pl.program_id(0), pl.program_id(1)

        @pl.when((b == 0) & (i == 0))
        def _():
            ds_ref[...] = jnp.zeros_like(ds_ref)

        @pl.when(i == 0)
        def _():
            dkc[...] = jnp.zeros_like(dkc)
            dvc[...] = jnp.zeros_like(dvc)

        @pl.when(i < ni)
        def _():
            m0b = _lane_mask(256, BF16)
            m0 = _lane_mask(128) > 0.5
            for kv in range(2):
                kcols = slice(kv * 128, (kv + 1) * 128)
                dk_own, dv_own = dkc[:, kcols], dvc[:, kcols]
                for s in range(SWA_SUB):
                    mask = _swa_mask(True if s else i > 0)
                    rows = slice(s * 128, (s + 1) * 128)
                    kst = _swa_keys(kp_ref, km_ref, s, kv, m0b)
                    vst = _swa_keys(vp_ref, vm_ref, s, kv, m0b)
                    dkst = jnp.zeros((512, 128), F32)
                    dvst = jnp.zeros((512, 128), F32)
                    for pr in range(2):
                        pair = kv * 2 + pr
                        cols = slice(pair * 128, (pair + 1) * 128)
                        q, do = q_ref[rows, cols], do_ref[rows, cols]
                        probs, psink = _swa_scores(q, kst, mask, sink_ref, pair)
                        dp_all = _dot_nt(do, vst)
                        ds_parts = []
                        for hh in range(2):
                            dp = dp_all[:, hh * 256:(hh + 1) * 256]
                            delta = jnp.sum(probs[hh] * dp, axis=1, keepdims=True)
                            ds_parts.append(probs[hh] * (dp - delta))
                            h = pair * 2 + hh
                            ds_ref[h:h + 1, :] = ds_ref[h:h + 1, :] - jnp.sum(psink[hh] * delta)
                        ds_all = jnp.concatenate(ds_parts, axis=1).astype(BF16)
                        p_all = jnp.concatenate(probs, axis=1).astype(BF16)
                        dq_ref[rows, cols] = _dot(ds_all, kst) * SCALE
                        dkst = dkst + _dot_tn(ds_all, q)
                        dvst = dvst + _dot_tn(p_all, do)
                    dk_ref[rows, kcols] = dk_own + jnp.where(m0, dkst[0:128], dkst[256:384])
                    dv_ref[rows, kcols] = dv_own + jnp.where(m0, dvst[0:128], dvst[256:384])
                    dk_own = jnp.where(m0, dkst[128:256], dkst[384:512])
                    dv_own = jnp.where(m0, dvst[128:256], dvst[384:512])
                dkc[:, kcols] = dk_own
                dvc[:, kcols] = dv_own

        @pl.when(i == ni)
        def _():
            dk_ref[...] = jnp.zeros_like(dk_ref)
            dv_ref[...] = jnp.zeros_like(dv_ref)
            dk_ref[0:128, :] = dkc[...]
            dv_ref[0:128, :] = dvc[...]

    main = lambda b, i: (b * ni + jnp.minimum(i, ni - 1), 0)
    prev = lambda b, i: ((b * ni + jnp.minimum(i, ni - 1)) * SWA_SUB - jnp.where(jnp.minimum(i, ni - 1) > 0, 1, 0), 0)
    shifted = lambda b, i: (b * (ni + 1) + i, 0)
    tpad = nbatch * (ni + 1) * SWA_Q
    return pl.pallas_call(
        body, name="swa_bwd", grid=(nbatch, ni + 1),
        in_specs=[pl.BlockSpec(memory_space=pltpu.SMEM), pl.BlockSpec((SWA_Q, 512), main),
                  pl.BlockSpec((128, 256), prev), pl.BlockSpec((SWA_Q, 256), main),
                  pl.BlockSpec((128, 256), prev), pl.BlockSpec((SWA_Q, 256), main),
                  pl.BlockSpec((SWA_Q, 512), main)],
        out_specs=[pl.BlockSpec((SWA_Q, 512), main), pl.BlockSpec((SWA_Q, 256), shifted),
                   pl.BlockSpec((SWA_Q, 256), shifted), _const_spec((8, 128))],
        out_shape=[jax.ShapeDtypeStruct((T, 512), F32), jax.ShapeDtypeStruct((tpad, 256), F32),
                   jax.ShapeDtypeStruct((tpad, 256), F32), jax.ShapeDtypeStruct((8, 128), F32)],
        scratch_shapes=[pltpu.VMEM((128, 256), F32), pltpu.VMEM((128, 256), F32)],
        compiler_params=_params(2),
    )(sinks, qrot, k2, k2, v2, v2, doa)


def _unshift(dkpad, nbatch, seq):
    return dkpad.reshape(nbatch, seq + SWA_Q, 256)[:, BLOCK:BLOCK + seq].reshape(nbatch * seq, 256)


SB_T = 256
SB_TQ = 2 * SB_T


def _sb_mask(kind):
    if kind == "full":
        return None
    rows = SB_T if kind == "B" else SB_TQ
    r = lax.broadcasted_iota(jnp.int32, (rows, 2 * SB_T), 0)
    c = lax.broadcasted_iota(jnp.int32, (rows, 2 * SB_T), 1)
    causal = (c & (SB_T - 1)) < r
    return causal | (r >= SB_T) if kind == "A" else causal


def _sb_rows(kind):
    return slice(SB_T, SB_TQ) if kind == "B" else slice(0, SB_TQ)


def _lower(x):
    return jnp.concatenate([jnp.zeros_like(x), x], axis=0)


def _sb_logits(neg_q, kst):
    nz = _dot_nt(neg_q, kst)
    sign = jnp.uint32(0x80000000)
    neg_abs = lax.bitcast_convert_type(lax.bitcast_convert_type(nz, jnp.uint32) | sign, F32)
    return nz, jnp.minimum(nz, 0.0) - jnp.log(1.0 + jnp.exp(neg_abs))


SB_NP = 4


def _pair_rows(ref, j, cols, m0b):
    kj = ref[pl.ds(pl.multiple_of(j * SB_T, SB_T), SB_T), cols]
    k0 = kj * m0b
    return jnp.concatenate([k0, kj - k0], axis=0)


def _bcast2(c0, c1):
    rows = c0.shape[0]
    return jnp.concatenate([jnp.broadcast_to(c0, (rows, SB_T)), jnp.broadcast_to(c1, (rows, SB_T))], axis=1)


def _rowsum2(x):
    return jnp.sum(x[:, 0:SB_T], axis=1, keepdims=True), jnp.sum(x[:, SB_T:2 * SB_T], axis=1, keepdims=True)


def _scan2(x, tri2):
    outs = []
    for h in range(2):
        xh = x[:, h * SB_T:(h + 1) * SB_T]
        hi = xh.astype(BF16)
        lo = (xh - hi.astype(F32)).astype(BF16)
        outs.append(_dot(jnp.concatenate([hi, lo], axis=1), tri2))
    return jnp.concatenate(outs, axis=1)


def _scan1(x, tri):
    xb = x.astype(BF16)
    return jnp.concatenate([_dot(xb[:, h * SB_T:(h + 1) * SB_T], tri) for h in range(2)], axis=1)


def _sb_fwd(qb, kb, vb, ublk, nbatch, seq):
    nq, nk = seq // SB_TQ, seq // SB_T
    T = nbatch * seq

    def body(q_ref, k_ref, v_ref, u_ref, o_ref, ctab_ref):
        i = pl.program_id(2)
        m0b = _lane_mask(SB_T, BF16)
        u = u_ref[...]
        lane = lax.broadcasted_iota(jnp.int32, (SB_TQ, 128), 1)
        ctab_ref[...] = jnp.zeros_like(ctab_ref)
        pairs = [slice(pp * 128, (pp + 1) * 128) for pp in range(SB_NP)]
        neg_qs = [-q_ref[:, cols] for cols in pairs]

        def tile(j, carries, kind):
            rows, mask = _sb_rows(kind), _sb_mask(kind)
            out = []
            for cols, neg_q, (c0, c1, acc) in zip(pairs, neg_qs, carries):
                if kind != "B":
                    ctab_ref[:, cols] = jnp.where(lane == j, c0, jnp.where(lane == nk + j, c1, ctab_ref[:, cols]))
                kst = _pair_rows(k_ref, j, cols, m0b)
                vst = _pair_rows(v_ref, j, cols, m0b)
                nz, lb = _sb_logits(neg_q[rows], kst)
                if mask is not None:
                    lb = jnp.where(mask, lb, 0.0)
                incl = _scan2(lb, u)
                w = jnp.exp(incl - nz if kind == "B" else incl + _bcast2(c0, c1) - nz)
                if mask is not None:
                    w = jnp.where(mask, w, 0.0)
                d0, d1, da = incl[:, 0:1], incl[:, SB_T:SB_T + 1], _dot(w.astype(BF16), vst)
                if kind == "B":
                    d0, d1, da = _lower(d0), _lower(d1), _lower(da)
                out.append((c0 + d0, c1 + d1, acc + da))
            return tuple(out)

        zc = jnp.zeros((SB_TQ, 1), F32)
        carries = tile(2 * i + 1, ((zc, zc, jnp.zeros((SB_TQ, 128), F32)),) * SB_NP, "B")
        carries = tile(2 * i, carries, "A")
        carries = lax.fori_loop(0, 2 * i, lambda jj, cr: tile(2 * i - 1 - jj, cr, "full"), carries)
        for cols, carry in zip(pairs, carries):
            o_ref[:, cols] = carry[2]

    wide = 128 * SB_NP
    blk = lambda b, g, i: (b * nq + i, g)
    full = lambda b, g, i: (b, g)
    return pl.pallas_call(
        body, name="sb_fwd", grid=(nbatch, 4 // SB_NP, nq),
        in_specs=[pl.BlockSpec((SB_TQ, wide), blk), pl.BlockSpec((seq, wide), full), pl.BlockSpec((seq, wide), full),
                  _const_spec((2 * SB_T, SB_T))],
        out_specs=[pl.BlockSpec((SB_TQ, wide), blk), pl.BlockSpec((SB_TQ, wide), blk)],
        out_shape=[jax.ShapeDtypeStruct((T, 512), F32), jax.ShapeDtypeStruct((T, 512), F32)],
        compiler_params=_params(3),
    )(qb, kb, vb, ublk)


def _sb_bwd(qb, kb, vb, dob, ctab, ublk, pblk, nbatch, seq):
    nq, nk = seq // SB_TQ, seq // SB_T
    T = nbatch * seq

    def body(q_ref, k_ref, v_ref, do_ref, ctab_ref, u_ref, up_ref, dq_ref, dk_ref, dv_ref):
        i = pl.program_id(2)

        @pl.when(i == 0)
        def _():
            dk_ref[...] = jnp.zeros_like(dk_ref)
            dv_ref[...] = jnp.zeros_like(dv_ref)

        m0b = _lane_mask(SB_T, BF16)
        m0 = _lane_mask(SB_T) > 0.5
        u, up = u_ref[...], up_ref[...]
        lane = lax.broadcasted_iota(jnp.int32, (SB_TQ, 128), 1)
        pairs = [slice(pp * 128, (pp + 1) * 128) for pp in range(SB_NP)]
        neg_qs = [-q_ref[:, cols] for cols in pairs]

        def tile(j, carries, kind):
            rows, mask = _sb_rows(kind), _sb_mask(kind)
            out = []
            for cols, neg_q, (s0, s1, dq) in zip(pairs, neg_qs, carries):
                q, do = q_ref[rows, cols], do_ref[rows, cols]
                kst = _pair_rows(k_ref, j, cols, m0b)
                vst = _pair_rows(v_ref, j, cols, m0b)
                nz, lb = _sb_logits(neg_q[rows], kst)
                one_minus_beta = jnp.exp(lb)
                if mask is not None:
                    lb = jnp.where(mask, lb, 0.0)
                if kind == "B":
                    w = jnp.exp(_scan2(lb, u) - nz)
                else:
                    ct = ctab_ref[:, cols]
                    c0 = jnp.sum(jnp.where(lane == j, ct, 0.0), axis=1, keepdims=True)
                    c1 = jnp.sum(jnp.where(lane == nk + j, ct, 0.0), axis=1, keepdims=True)
                    w = jnp.exp(_scan2(lb, u) + _bcast2(c0, c1) - nz)
                if mask is not None:
                    w = jnp.where(mask, w, 0.0)
                e = _dot_nt(do, vst) * w
                dlb = _bcast2(s0[rows], s1[rows]) + _scan1(e, up)
                dz = (e + dlb) * one_minus_beta - dlb
                if mask is not None:
                    dz = jnp.where(mask, dz, 0.0)
                dzb = dz.astype(BF16)
                dkst = _dot_tn(dzb, q)
                dvst = _dot_tn(w.astype(BF16), do)
                keys = pl.ds(pl.multiple_of(j * SB_T, SB_T), SB_T)
                dk_ref[keys, cols] = dk_ref[keys, cols] + jnp.where(m0, dkst[0:SB_T], dkst[SB_T:2 * SB_T])
                dv_ref[keys, cols] = dv_ref[keys, cols] + jnp.where(m0, dvst[0:SB_T], dvst[SB_T:2 * SB_T])
                x0, x1 = _rowsum2(e)
                ddq = _dot(dzb, kst)
                if kind == "B":
                    x0, x1, ddq = _lower(x0), _lower(x1), _lower(ddq)
                out.append((s0 + x0, s1 + x1, dq + ddq))
            return tuple(out)

        zc = jnp.zeros((SB_TQ, 1), F32)
        carries = lax.fori_loop(0, 2 * i, lambda j, cr: tile(j, cr, "full"),
                                ((zc, zc, jnp.zeros((SB_TQ, 128), F32)),) * SB_NP)
        carries = tile(2 * i, carries, "A")
        carries = tile(2 * i + 1, carries, "B")
        for cols, carry in zip(pairs, carries):
            dq_ref[:, cols] = carry[2] * SCALE

    wide = 128 * SB_NP
    blk = lambda b, g, i: (b * nq + i, g)
    full = lambda b, g, i: (b, g)
    return pl.pallas_call(
        body, name="sb_bwd", grid=(nbatch, 4 // SB_NP, nq),
        in_specs=[pl.BlockSpec((SB_TQ, wide), blk), pl.BlockSpec((seq, wide), full), pl.BlockSpec((seq, wide), full),
                  pl.BlockSpec((SB_TQ, wide), blk), pl.BlockSpec((SB_TQ, wide), blk),
                  _const_spec((2 * SB_T, SB_T)), _const_spec((SB_T, SB_T))],
        out_specs=[pl.BlockSpec((SB_TQ, wide), blk), pl.BlockSpec((seq, wide), full), pl.BlockSpec((seq, wide), full)],
        out_shape=[jax.ShapeDtypeStruct((T, 512), F32)] * 3,
        compiler_params=_params(3),
    )(qb, kb, vb, dob, ctab, ublk, pblk)


def _sigmoid(g):
    return 1.0 / (1.0 + jnp.exp(-g))


def _out_proj(oa, ob, ga, gb, x2, tgt, wout_bf):
    T = x2.shape[0]

    def body(oa_ref, ob_ref, ga_ref, gb_ref, x_ref, t_ref, w_ref, y_ref, dout_ref, loss_ref):
        @pl.when(pl.program_id(0) == 0)
        def _():
            loss_ref[...] = jnp.zeros_like(loss_ref)

        ga, gb = ga_ref[...], gb_ref[...]
        ya = (oa_ref[...] * (ga * _sigmoid(ga))).astype(BF16)
        yb = (ob_ref[...] * (gb * _sigmoid(gb))).astype(BF16)
        y_ref[:, 0:512] = ya
        y_ref[:, 512:1024] = yb
        out = x_ref[...] + _dot(ya, w_ref[0:512, :]) + _dot(yb, w_ref[512:1024, :])
        diff = out - t_ref[...]
        dout_ref[...] = diff * (1.0 / D_MODEL)
        loss_ref[...] = loss_ref[...] + jnp.sum(diff * diff) * (0.5 / D_MODEL)

    rows = lambda w: pl.BlockSpec((TM, w), lambda i: (i, 0))
    return pl.pallas_call(
        body, name="out_proj", grid=(T // TM,),
        in_specs=[rows(512), rows(512), rows(512), rows(512), rows(D_MODEL), rows(D_MODEL),
                  _const_spec((D_MODEL, D_MODEL))],
        out_specs=[rows(D_MODEL), rows(D_MODEL), _const_spec((8, 128))],
        out_shape=[jax.ShapeDtypeStruct((T, D_MODEL), BF16), jax.ShapeDtypeStruct((T, D_MODEL), F32),
                   jax.ShapeDtypeStruct((8, 128), F32)],
        compiler_params=_params(1),
    )(oa, ob, ga, gb, x2, tgt, wout_bf)


def _out_proj_bwd(dout, y, oa, ob, ga, gb, wout_bf):
    T = dout.shape[0]

    def body(dout_ref, y_ref, oa_ref, ob_ref, ga_ref, gb_ref, w_ref, doa_ref, dob_ref, dga_ref, dgb_ref, dw_ref):
        @pl.when(pl.program_id(0) == 0)
        def _():
            dw_ref[...] = jnp.zeros_like(dw_ref)

        db = dout_ref[...].astype(BF16)
        dw_ref[...] = dw_ref[...] + _dot_tn(y_ref[...], db)
        for o_ref, g_ref, do_ref, dg_ref, lo in ((oa_ref, ga_ref, doa_ref, dga_ref, 0), (ob_ref, gb_ref, dob_ref, dgb_ref, 512)):
            dy = _dot_nt(db, w_ref[lo:lo + 512, :])
            g = g_ref[...]
            sg = _sigmoid(g)
            do_ref[...] = (dy * (g * sg)).astype(BF16)
            dg_ref[...] = (dy * o_ref[...] * (sg * (1.0 + g * (1.0 - sg)))).astype(BF16)

    rows = lambda w: pl.BlockSpec((TM, w), lambda i: (i, 0))
    return pl.pallas_call(
        body, name="out_proj_bwd", grid=(T // TM,),
        in_specs=[rows(D_MODEL), rows(D_MODEL), rows(512), rows(512), rows(512), rows(512),
                  _const_spec((D_MODEL, D_MODEL))],
        out_specs=[rows(512)] * 4 + [_const_spec((D_MODEL, D_MODEL))],
        out_shape=[jax.ShapeDtypeStruct((T, 512), BF16)] * 4 + [jax.ShapeDtypeStruct((D_MODEL, D_MODEL), F32)],
        compiler_params=_params(1),
    )(dout, y, oa, ob, ga, gb, wout_bf)


def _qk_grad(qraw, kraw, dqrot, dk2, dv2, qg512, kg128, g512, cos, sa, sb):
    T = qraw.shape[0]

    def body(qraw_ref, kraw_ref, dqrot_ref, dk2_ref, dv2_ref, qg_ref, kg_ref, g_ref, cos_ref, sa_ref, sb_ref,
             dqa_ref, dkv_ref, dqg_ref, dkg_ref):
        @pl.when(pl.program_id(0) == 0)
        def _():
            dqg_ref[...] = jnp.zeros_like(dqg_ref)
            dkg_ref[...] = jnp.zeros_like(dkg_ref)

        cosv, sav, sbv = cos_ref[...], sa_ref[...], sb_ref[...]
        m0 = _lane_mask(TM) > 0.5

        def head_norm_bwd(raw, dn_rot, gmat, gain):
            r = lax.rsqrt(_dot_hl(raw * raw, gmat) + EPS)
            n = raw * r
            dn = dn_rot * gain
            return r * (dn - n * _dot_hl(dn * n, gmat)), jnp.sum(dn_rot * n, axis=0, keepdims=True)

        def fold(ref):
            a, b = ref[:, 0:128], ref[:, 128:256]
            return jnp.where(m0, a + pltpu.roll(a, 64, 1), b + pltpu.roll(b, 64, 1))

        dqn = jnp.concatenate([_rope_bwd(dqrot_ref[:, s * 128:(s + 1) * 128], cosv, sav, sbv) for s in range(4)], axis=1)
        dqa, dqg = head_norm_bwd(qraw_ref[...], dqn, g_ref[...], qg_ref[...])
        dka, dkg = head_norm_bwd(kraw_ref[...], _rope_bwd(fold(dk2_ref), cosv, sav, sbv), g_ref[0:128, 0:128], kg_ref[...])
        dqa_ref[...] = dqa.astype(BF16)
        dkv_ref[:, 0:128] = dka.astype(BF16)
        dkv_ref[:, 128:256] = fold(dv2_ref).astype(BF16)
        dqg_ref[...] = dqg_ref[...] + dqg
        dkg_ref[...] = dkg_ref[...] + dkg

    rows = lambda w: pl.BlockSpec((TM, w), lambda i: (i, 0))
    return pl.pallas_call(
        body, name="qk_grad", grid=(T // TM,),
        in_specs=[rows(512), rows(128), rows(512), rows(256), rows(256), _const_spec((1, 512)), _const_spec((1, 128)),
                  _const_spec((512, 512)), rows(128), rows(128), rows(128)],
        out_specs=[rows(512), rows(256), _const_spec((1, 512)), _const_spec((1, 128))],
        out_shape=[jax.ShapeDtypeStruct((T, 512), BF16), jax.ShapeDtypeStruct((T, 256), BF16),
                   jax.ShapeDtypeStruct((1, 512), F32), jax.ShapeDtypeStruct((1, 128), F32)],
        compiler_params=_params(1),
    )(qraw, kraw, dqrot, dk2, dv2, qg512, kg128, g512, cos, sa, sb)


_PIECES = ((QA, 512), (KA, 256), (GA, 512), (QB, 512), (KB, 512), (VB, 512), (GB, 512))


def _w_in_grad(h, pieces):
    T = h.shape[0]

    def body(h_ref, *refs):
        dw_ref = refs[-1]

        @pl.when(pl.program_id(0) == 0)
        def _():
            dw_ref[...] = jnp.zeros_like(dw_ref)

        hb = h_ref[...]
        for (lo, width), p_ref in zip(_PIECES, refs[:-1]):
            dw_ref[lo:lo + width, :] = dw_ref[lo:lo + width, :] + _dot_tn(p_ref[...].astype(BF16), hb)

    rows = lambda w: pl.BlockSpec((TM, w), lambda i: (i, 0))
    return pl.pallas_call(
        body, name="w_in_grad", grid=(T // TM,),
        in_specs=[rows(D_MODEL)] + [rows(w) for _, w in _PIECES],
        out_specs=_const_spec((IN_WIDTH, D_MODEL)),
        out_shape=jax.ShapeDtypeStruct((IN_WIDTH, D_MODEL), F32),
        compiler_params=_params(1),
    )(h, *pieces)


def _x_grad(x2, dout, pieces, w_bf, ng, cin4, cout4):
    T = x2.shape[0]
    npc = len(_PIECES)
    steps = T // TM

    def body(x_ref, dout_ref, *refs):
        w_ref, ng_ref, cin_ref, cout_ref, gx_ref, dng_ref, rin_ref, rout_ref, ssem, rsem = refs[npc:]
        step = pl.program_id(0)
        _, _, c, chips = _place()
        sems = (ssem, rsem)

        def copies():
            out = []
            for j, (cx, cy) in enumerate(chips):
                k = 2 * cx + cy
                out.append(_remote(cin_ref.at[k], rin_ref.at[j], sems, 2 * j, (cx, cy, c)))
                out.append(_remote(cout_ref.at[k], rout_ref.at[j], sems, 2 * j + 1, (cx, cy, c)))
            return out

        @pl.when(step == 0)
        def _():
            dng_ref[...] = jnp.zeros_like(dng_ref)
            for cp in copies():
                cp.start()

        dh = jnp.zeros((TM, D_MODEL), F32)
        for (lo, width), p_ref in zip(_PIECES, refs[:npc]):
            dh = dh + _dot(p_ref[...].astype(BF16), w_ref[lo:lo + width, :])
        xb = x_ref[...]
        r = lax.rsqrt(jnp.mean(xb * xb, axis=-1, keepdims=True) + EPS)
        n = xb * r
        dn = dh * ng_ref[...]
        gx_ref[...] = dout_ref[...] + r * (dn - n * jnp.mean(dn * n, axis=-1, keepdims=True))
        dng_ref[...] = dng_ref[...] + jnp.sum(dh * n, axis=0, keepdims=True)

        @pl.when(step == steps - 1)
        def _():
            for cp in copies():
                cp.wait_recv()
            for cp in copies():
                cp.wait_send()

    rows = lambda w: pl.BlockSpec((TM, w), lambda i: (i, 0))
    hbm = pl.BlockSpec(memory_space=pl.ANY)
    return pl.pallas_call(
        body, name="x_grad", grid=(steps,),
        in_specs=[rows(D_MODEL), rows(D_MODEL)] + [rows(w) for _, w in _PIECES]
        + [_const_spec((IN_WIDTH, D_MODEL)), _const_spec((1, D_MODEL)), hbm, hbm],
        out_specs=[rows(D_MODEL), _const_spec((1, D_MODEL)), hbm, hbm],
        out_shape=[jax.ShapeDtypeStruct((T, D_MODEL), F32), jax.ShapeDtypeStruct((1, D_MODEL), F32),
                   jax.ShapeDtypeStruct((3,) + cin4.shape[1:], cin4.dtype),
                   jax.ShapeDtypeStruct((3,) + cout4.shape[1:], cout4.dtype)],
        scratch_shapes=[pltpu.SemaphoreType.DMA((6,)), pltpu.SemaphoreType.DMA((6,))],
        compiler_params=_params(1),
    )(x2, dout, *pieces, w_bf, ng, cin4, cout4)


HBM = pl.BlockSpec(memory_space=pl.ANY)


def _place():
    x, y, c = lax.axis_index("x"), lax.axis_index("y"), lax.axis_index("c")
    chips = [(1 - x, y), (x, 1 - y), (1 - x, 1 - y)]
    return x, y, c, chips


def _remote(src, dst, sems, k, to):
    return pltpu.make_async_remote_copy(src_ref=src, dst_ref=dst, send_sem=sems[0].at[k], recv_sem=sems[1].at[k],
                                        device_id=to, device_id_type=MESH)


def _gather_weights(shards):
    n = len(shards)

    def body(*refs):
        srcs, dsts, (ssem, rsem) = refs[:n], refs[n:2 * n], refs[2 * n:]
        x, y, c, chips = _place()
        me, sib = 2 * x + y, (x, y, 1 - c)
        sems = (ssem, rsem)
        sent = []
        for j, (cx, cy) in enumerate(chips):
            for a in range(n):
                sent.append(_remote(srcs[a].at[c], dsts[a].at[me, c], sems, n * j + a, (cx, cy, c)))
        for cp in sent:
            cp.start()
        for j, (cx, cy) in enumerate(chips):
            k = 2 * cx + cy
            for a in range(n):
                _remote(srcs[a].at[c], dsts[a].at[k, c], sems, n * j + a, sib).wait_recv()
                fwd = _remote(dsts[a].at[k, c], dsts[a].at[k, c], sems, 3 * n + n * j + a, sib)
                fwd.start()
                sent.append(fwd)
        for j, (cx, cy) in enumerate(chips):
            k = 2 * cx + cy
            for a in range(n):
                _remote(srcs[a].at[c], dsts[a].at[k, 1 - c], sems, 3 * n + n * j + a, sib).wait_recv()
        for cp in sent:
            cp.wait_send()

    gathered = pl.pallas_call(
        body, name="gather_weights", in_specs=[HBM] * n, out_specs=[HBM] * n,
        out_shape=[jax.ShapeDtypeStruct((4,) + s.shape, s.dtype) for s in shards],
        scratch_shapes=[pltpu.SemaphoreType.DMA((6 * n,)), pltpu.SemaphoreType.DMA((6 * n,))],
    )(*shards)
    me = 2 * lax.axis_index("x") + lax.axis_index("y")
    return [lax.dynamic_update_slice(g, s[None], (me, 0, 0, 0)) for g, s in zip(gathered, shards)]


def _pair_exchange(grads):
    n = len(grads)

    def body(*refs):
        srcs, dsts, (ssem, rsem) = refs[:n], refs[n:2 * n], refs[2 * n:]
        x, y, c, _ = _place()
        sems = (ssem, rsem)
        sib = (x, y, 1 - c)
        sent = [_remote(srcs[a].at[:, pl.ds(1 - c, 1)], dsts[a], sems, a, sib) for a in range(n)]
        for cp in sent:
            cp.start()
        for cp in sent:
            cp.wait()

    return pl.pallas_call(
        body, name="pair_exchange", in_specs=[HBM] * n, out_specs=[HBM] * n,
        out_shape=[jax.ShapeDtypeStruct((4, 1) + g.shape[2:], g.dtype) for g in grads],
        scratch_shapes=[pltpu.SemaphoreType.DMA((n,)), pltpu.SemaphoreType.DMA((n,))],
    )(*grads)


def _share_halves(hin, hout, small):
    def body(hin_ref, hout_ref, small_ref, oin_ref, oout_ref, sall_ref, ssem, rsem, lsem):
        x, y, c, _ = _place()
        sems = (ssem, rsem)
        sib = (x, y, 1 - c)
        me = 4 * x + 2 * y + c
        own = pltpu.make_async_copy(small_ref, sall_ref.at[me], lsem.at[0])
        own.start()
        sent = [_remote(hin_ref, oin_ref, sems, 0, sib), _remote(hout_ref, oout_ref, sems, 1, sib)]
        flips = [(fx, fy, fc) for fx in (0, 1) for fy in (0, 1) for fc in (0, 1)][1:]
        for k, (fx, fy, fc) in enumerate(flips):
            sent.append(_remote(small_ref, sall_ref.at[me], sems, 2 + k, (x ^ fx, y ^ fy, c ^ fc)))
        for cp in sent:
            cp.start()
        _remote(hin_ref, oin_ref, sems, 0, sib).wait_recv()
        _remote(hout_ref, oout_ref, sems, 1, sib).wait_recv()
        for k, (fx, fy, fc) in enumerate(flips):
            src = 4 * (x ^ fx) + 2 * (y ^ fy) + (c ^ fc)
            _remote(small_ref, sall_ref.at[src], sems, 2 + k, sib).wait_recv()
        for cp in sent:
            cp.wait_send()
        own.wait()

    return pl.pallas_call(
        body, name="share_halves", in_specs=[HBM, HBM, HBM], out_specs=[HBM, HBM, HBM],
        out_shape=[jax.ShapeDtypeStruct(hin.shape, F32), jax.ShapeDtypeStruct(hout.shape, F32),
                   jax.ShapeDtypeStruct((8,) + small.shape, F32)],
        scratch_shapes=[pltpu.SemaphoreType.DMA((9,)), pltpu.SemaphoreType.DMA((9,)), pltpu.SemaphoreType.DMA((1,))],
    )(hin, hout, small)


def _add_half(cidx, full4, recv4):
    _, _, rows, width = full4.shape

    def body(c_ref, a_ref, b_ref, o_ref):
        o_ref[0] = (a_ref[0, 0] + b_ref[0, 0]).astype(BF16)

    return pl.pallas_call(
        body, name=f"add_half_{rows}",
        grid_spec=pltpu.PrefetchScalarGridSpec(
            num_scalar_prefetch=1, grid=(4,),
            in_specs=[pl.BlockSpec((1, 1, rows, width), lambda k, c: (k, c[0], 0, 0)),
                      pl.BlockSpec((1, 1, rows, width), lambda k, c: (k, 0, 0, 0))],
            out_specs=pl.BlockSpec((1, rows, width), lambda k, c: (k, 0, 0))),
        out_shape=jax.ShapeDtypeStruct((4, rows, width), BF16),
        compiler_params=_params(1),
    )(cidx, full4, recv4)


def _sum_chips(chip, own4, recv3):
    _, rows, width = recv3.shape
    rb = rows // 2

    def body(k_ref, a_ref, r_ref, o_ref):
        acc = a_ref[0].astype(F32)
        for s in range(3):
            acc = acc + r_ref[s].astype(F32)
        o_ref[...] = acc

    return pl.pallas_call(
        body, name=f"sum_chips_{rows}",
        grid_spec=pltpu.PrefetchScalarGridSpec(
            num_scalar_prefetch=1, grid=(rows // rb,),
            in_specs=[pl.BlockSpec((1, rb, width), lambda i, k: (k[0], i, 0)),
                      pl.BlockSpec((3, rb, width), lambda i, k: (0, i, 0))],
            out_specs=pl.BlockSpec((rb, width), lambda i, k: (i, 0))),
        out_shape=jax.ShapeDtypeStruct((rows, width), F32),
        compiler_params=_params(1),
    )(chip, own4, recv3)


def _sum_slots(r4):
    n, rows, width = r4.shape

    def body(r_ref, o_ref):
        acc = r_ref[0]
        for s in range(1, n):
            acc = acc + r_ref[s]
        o_ref[...] = acc

    return pl.pallas_call(
        body, name=f"sum_slots_{n}_{rows}_{width}", grid=(1,),
        in_specs=[pl.BlockSpec((n, rows, width), lambda i: (0, 0, 0))],
        out_specs=pl.BlockSpec((rows, width), lambda i: (0, 0)),
        out_shape=jax.ShapeDtypeStruct((rows, width), F32),
        compiler_params=_params(1),
    )(r4)


def _adam_math(w, g, m, v):
    c1 = 1.0 - ADAM_B1 ** ADAM_STEP
    c2 = 1.0 - ADAM_B2 ** ADAM_STEP
    nm = ADAM_B1 * m + (1.0 - ADAM_B1) * g
    nv = ADAM_B2 * v + (1.0 - ADAM_B2) * (g * g)
    return -ADAM_LR * ((nm / c1) / (jnp.sqrt(nv / c2) + ADAM_EPS) + ADAM_WD * w), nm, nv


def _adamw(w, g, m, v):
    rows, width = w.shape

    def body(w_ref, g_ref, m_ref, v_ref, d_ref, nm_ref, nv_ref):
        d_ref[...], nm_ref[...], nv_ref[...] = _adam_math(w_ref[...], g_ref[...], m_ref[...], v_ref[...])

    spec = pl.BlockSpec((rows, width), lambda i: (0, 0))
    return pl.pallas_call(
        body, name=f"adamw_{rows}_{width}", grid=(1,),
        in_specs=[spec] * 4, out_specs=[spec] * 3,
        out_shape=[jax.ShapeDtypeStruct((rows, width), F32)] * 3,
        compiler_params=_params(1),
    )(w, g, m, v)


def _adamw_halves(cidx, w, own, recv, m, v):
    rows, width = w.shape
    rb = rows // 4

    def body(c_ref, w_ref, own_ref, recv_ref, m_ref, v_ref, g_ref, d_ref, nm_ref, nv_ref):
        mine = (pl.program_id(0) // 2) == c_ref[0]
        g = jnp.where(mine, own_ref[...], recv_ref[...])
        g_ref[...] = g
        d_ref[...], nm_ref[...], nv_ref[...] = _adam_math(w_ref[...], g, m_ref[...], v_ref[...])

    full = pl.BlockSpec((rb, width), lambda i, c: (i, 0))
    half = pl.BlockSpec((rb, width), lambda i, c: (i % 2, 0))
    return pl.pallas_call(
        body, name=f"adamw_halves_{rows}",
        grid_spec=pltpu.PrefetchScalarGridSpec(
            num_scalar_prefetch=1, grid=(4,),
            in_specs=[full, half, half, full, full], out_specs=[full] * 4),
        out_shape=[jax.ShapeDtypeStruct((rows, width), F32)] * 4,
        compiler_params=_params(1),
    )(cidx, w, own, recv, m, v)


def _rope_tables(positions):
    half = HEAD_DIM // 2
    inv_freq = ROPE_THETA ** (-jnp.arange(half, dtype=F32) * 2.0 / HEAD_DIM)
    ang = positions.astype(F32).reshape(-1, 1) * inv_freq
    cos, sin, zero = jnp.cos(ang), jnp.sin(ang), jnp.zeros_like(ang)
    return (jnp.concatenate([cos] * 4, axis=1), jnp.concatenate([-sin, zero] * 2, axis=1),
            jnp.concatenate([zero, sin] * 2, axis=1))


def _constants():
    idx = jnp.arange(512)
    g512 = jnp.where(idx[:, None] // HEAD_DIM == idx[None, :] // HEAD_DIM, 1.0 / HEAD_DIM, 0.0).astype(BF16)
    j = jnp.arange(SB_T)
    ublk = jnp.where(j[:, None] >= j[None, :], 1.0, 0.0).astype(BF16)
    pblk = jnp.where(j[:, None] < j[None, :], 1.0, 0.0).astype(BF16)
    return g512, jnp.concatenate([ublk, ublk], axis=0), pblk


def _pad_rows(v, width):
    return jnp.pad(v, ((0, 0), (0, width - v.shape[1])))


def _forward_backward(x2, tgt, positions, norm_gain, q_norm_gain, k_norm_gain, sinks, w_bf, wout_bf, nbatch, seq):
    cos, sa, sb = _rope_tables(positions)
    g512, ublk, pblk = _constants()
    qg512 = jnp.tile(q_norm_gain, (1, 8))
    kg128 = jnp.tile(k_norm_gain, (1, 2))
    sink1 = sinks.reshape(8)

    h, qraw, kraw, qrot, k2, v2, ga, qb, kb, vb, gb = _norm_proj(x2, norm_gain, w_bf, qg512, kg128, g512, cos, sa, sb)
    oa = _swa_fwd(sink1, qrot, k2, v2, nbatch, seq)
    ob, ctab = _sb_fwd(qb, kb, vb, ublk, nbatch, seq)
    y, dout, loss_acc = _out_proj(oa, ob, ga, gb, x2, tgt, wout_bf)

    doa, dob, dga, dgb, dwout = _out_proj_bwd(dout, y, oa, ob, ga, gb, wout_bf)
    dqrot, dk2, dv2, dsink = _swa_bwd(sink1, qrot, k2, v2, doa, nbatch, seq)
    dqb, dkb, dvb = _sb_bwd(qb, kb, vb, dob, ctab, ublk, pblk, nbatch, seq)
    dk2, dv2 = _unshift(dk2, nbatch, seq), _unshift(dv2, nbatch, seq)
    dqa, dkv, dqg, dkg = _qk_grad(qraw, kraw, dqrot, dk2, dv2, qg512, kg128, g512, cos, sa, sb)
    pieces = (dqa, dkv, dga, dqb, dkb, dvb, dgb)
    dwin_t = _w_in_grad(h, pieces)
    dqg64 = dqg.reshape(8, HEAD_DIM).sum(axis=0, keepdims=True)
    dkg64 = dkg.reshape(2, HEAD_DIM).sum(axis=0, keepdims=True)
    small = jnp.concatenate([_pad_rows(dqg64, D_MODEL), _pad_rows(dkg64, D_MODEL), _pad_rows(dsink[:, 0].reshape(1, 8), D_MODEL),
                             _pad_rows(loss_acc[0:1, 0:1], D_MODEL), jnp.zeros((3, D_MODEL), F32)], axis=0)
    return dwin_t, dwout, small, (x2, dout, pieces, w_bf, norm_gain)


def _reduce_grads(dwin_t, dwout, small7, x_grad_args):
    gin4 = dwin_t.reshape(4, 2, SHARD_IN // 2, D_MODEL)
    gout4 = dwout.reshape(4, 2, SHARD_OUT // 2, D_MODEL)
    rin, rout = _pair_exchange([gin4, gout4])
    cidx = lax.axis_index("c").astype(jnp.int32).reshape(1)
    chip = (2 * lax.axis_index("x") + lax.axis_index("y")).astype(jnp.int32).reshape(1)
    cin4, cout4 = _add_half(cidx, gin4, rin), _add_half(cidx, gout4, rout)
    gx, dng, rin3, rout3 = _x_grad(*x_grad_args, cin4, cout4)
    own_in, own_out = _sum_chips(chip, cin4, rin3), _sum_chips(chip, cout4, rout3)
    sib_in, sib_out, small_all = _share_halves(own_in, own_out, jnp.concatenate([dng, small7], axis=0))
    return gx, cidx, own_in, sib_in, own_out, sib_out, _sum_slots(small_all)


def kernel(x, positions, norm_gain, w_in, q_norm_gain, k_norm_gain, sinks, w_out, loss_target, m_norm_gain, m_w_in, m_q_norm_gain, m_k_norm_gain, m_sinks, m_w_out, v_norm_gain, v_w_in, v_q_norm_gain, v_k_norm_gain, v_sinks, v_w_out):
    nbatch, seq, _ = x.shape
    T = nbatch * seq
    x2 = x.reshape(T, D_MODEL)
    tgt = loss_target.reshape(T, D_MODEL)
    tr = lambda a: jnp.swapaxes(a[0], 0, 1)
    win_t, m_win_t, v_win_t = tr(w_in), tr(m_w_in), tr(v_w_in)

    ain, aout = _gather_weights([win_t.astype(BF16).reshape(2, SHARD_IN // 2, D_MODEL),
                                 w_out[0].astype(BF16).reshape(2, SHARD_OUT // 2, D_MODEL)])
    w_bf, wout_bf = ain.reshape(IN_WIDTH, D_MODEL), aout.reshape(D_MODEL, D_MODEL)

    dwin_t, dwout, small7, x_grad_args = _forward_backward(
        x2, tgt, positions, norm_gain, q_norm_gain, k_norm_gain, sinks, w_bf, wout_bf, nbatch, seq)

    gx, cidx, own_in, sib_in, own_out, sib_out, g_small = _reduce_grads(dwin_t, dwout, small7, x_grad_args)
    loss = g_small[4, 0]

    g_in, d_in, nm_in, nv_in = [jnp.swapaxes(a, 0, 1) for a in
                                _adamw_halves(cidx, win_t, own_in, sib_in, m_win_t, v_win_t)]
    g_out, d_out, nm_out, nv_out = _adamw_halves(cidx, w_out[0], own_out, sib_out, m_w_out[0], v_w_out[0])
    pack = lambda a, b, c_, d: jnp.concatenate(
        [a, _pad_rows(b, D_MODEL), _pad_rows(c_, D_MODEL), _pad_rows(d, D_MODEL), jnp.zeros((4, D_MODEL), F32)], axis=0)
    w_s = pack(norm_gain, q_norm_gain, k_norm_gain, sinks)
    m_s = pack(m_norm_gain, m_q_norm_gain, m_k_norm_gain, m_sinks)
    v_s = pack(v_norm_gain, v_q_norm_gain, v_k_norm_gain, v_sinks)
    d_s, nm_s, nv_s = _adamw(w_s, g_small, m_s, v_s)
    unpack = lambda a: (a[0:1, :], a[1:2, 0:HEAD_DIM], a[2:3, 0:HEAD_DIM], a[3:4, 0:8])

    g_ng, g_qg, g_kg, g_sk = unpack(g_small)
    d_ng, d_qg, d_kg, d_sk = unpack(d_s)
    m_ng, m_qg, m_kg, m_sk = unpack(nm_s)
    v_ng, v_qg, v_kg, v_sk = unpack(nv_s)
    return (loss, gx.reshape(nbatch, seq, D_MODEL),
            g_ng, g_in[None], g_qg, g_kg, g_sk, g_out[None],
            d_ng, d_in[None], d_qg, d_kg, d_sk, d_out[None],
            m_ng, nm_in[None], m_qg, m_kg, m_sk, nm_out[None],
            v_ng, nv_in[None], v_qg, v_kg, v_sk, nv_out[None])
```

```python
import functools
import math

import jax
import jax.numpy as jnp
from jax import lax
from jax.experimental import pallas as pl
from jax.experimental.pallas import tpu as pltpu

F32 = jnp.float32
BF16 = jnp.bfloat16

D_MODEL = 1024
HEAD_DIM = 64
BLOCK = 128
ROPE_THETA = 10000.0
EPS = 1e-6
QA, KA, VA, GA, QB, KB, VB, GB = 0, 512, 640, 768, 1280, 1792, 2304, 2816
IN_WIDTH = 3328
SHARD_IN = IN_WIDTH // 4
SHARD_OUT = D_MODEL // 4
SCALE = 1.0 / math.sqrt(HEAD_DIM)
NEG = -1e30

ADAM_LR, ADAM_B1, ADAM_B2, ADAM_EPS, ADAM_WD, ADAM_STEP = 0.001, 0.9, 0.999, 1e-08, 0.01, 10

TM = 512
VMEM_LIMIT = 56 * 1024 * 1024
MESH = pl.DeviceIdType.MESH


def _dot(a, b):
    return jnp.dot(a, b, preferred_element_type=F32)


def _dot_nt(a, b):
    return lax.dot_general(a, b, (((1,), (1,)), ((), ())), preferred_element_type=F32)


def _dot_tn(a, b):
    return lax.dot_general(a, b, (((0,), (0,)), ((), ())), preferred_element_type=F32)


def _dot_hl(a, m):
    hi = a.astype(BF16)
    lo = (a - hi.astype(F32)).astype(BF16)
    return _dot(hi, m) + _dot(lo, m)


def _params(n_axes=None, vmem=VMEM_LIMIT):
    sem = None if n_axes is None else ("arbitrary",) * n_axes
    return pltpu.CompilerParams(dimension_semantics=sem, vmem_limit_bytes=vmem)


def _const_spec(shape):
    nd = len(shape)
    return pl.BlockSpec(shape, lambda *_: (0,) * nd)


def _rope_fwd(x, cos, sa, sb):
    return x * cos + pltpu.roll(x, 96, 1) * sa + pltpu.roll(x, 32, 1) * sb


def _rope_bwd(d, cos, sa, sb):
    return d * cos - pltpu.roll(d, 96, 1) * sa - pltpu.roll(d, 32, 1) * sb


def _lane_mask(rows, dtype=F32):
    lane = lax.broadcasted_iota(jnp.int32, (rows, 128), 1)
    return jnp.where(lane < HEAD_DIM, 1.0, 0.0).astype(dtype)


def _norm_proj(x2, ng, w_bf, qg512, kg128, g512, cos, sa, sb):
    T = x2.shape[0]

    def body(x_ref, ng_ref, w_ref, qg_ref, kg_ref, g_ref, cos_ref, sa_ref, sb_ref,
             h_ref, qraw_ref, kraw_ref, qrot_ref, k2_ref, v2_ref, ga_ref, qb_ref, kb_ref, vb_ref, gb_ref):
        xb = x_ref[...]
        r = lax.rsqrt(jnp.mean(xb * xb, axis=-1, keepdims=True) + EPS)
        h = (xb * r * ng_ref[...]).astype(BF16)
        h_ref[...] = h
        cosv, sav, sbv = cos_ref[...], sa_ref[...], sb_ref[...]
        m0 = _lane_mask(TM)

        def dup(v):
            v0 = v * m0
            v1 = v - v0
            return v0 + pltpu.roll(v0, 64, 1), v1 + pltpu.roll(v1, 64, 1)

        qa = _dot_nt(h, w_ref[QA:KA, :])
        qraw_ref[...] = qa
        qn = qa * lax.rsqrt(_dot_hl(qa * qa, g_ref[...]) + EPS) * qg_ref[...]
        for s in range(4):
            qs = _rope_fwd(qn[:, s * 128:(s + 1) * 128], cosv, sav, sbv)
            qrot_ref[:, s * 128:(s + 1) * 128] = (qs * SCALE).astype(BF16)
        ka = _dot_nt(h, w_ref[KA:VA, :])
        kraw_ref[...] = ka
        kn = ka * lax.rsqrt(_dot_hl(ka * ka, g_ref[0:128, 0:128]) + EPS) * kg_ref[...]
        k0, k1 = dup(_rope_fwd(kn, cosv, sav, sbv))
        k2_ref[:, 0:128] = k0.astype(BF16)
        k2_ref[:, 128:256] = k1.astype(BF16)
        v0, v1 = dup(_dot_nt(h, w_ref[VA:GA, :]))
        v2_ref[:, 0:128] = v0.astype(BF16)
        v2_ref[:, 128:256] = v1.astype(BF16)
        ga_ref[...] = _dot_nt(h, w_ref[GA:QB, :])
        qb_ref[...] = (_dot_nt(h, w_ref[QB:KB, :]) * SCALE).astype(BF16)
        kb_ref[...] = _dot_nt(h, w_ref[KB:VB, :]).astype(BF16)
        vb_ref[...] = _dot_nt(h, w_ref[VB:GB, :]).astype(BF16)
        gb_ref[...] = _dot_nt(h, w_ref[GB:IN_WIDTH, :])

    def rows(w):
        return pl.BlockSpec((TM, w), lambda i: (i, 0))

    outs = [(D_MODEL, BF16), (512, F32), (128, F32), (512, BF16), (256, BF16), (256, BF16), (512, F32),
            (512, BF16), (512, BF16), (512, BF16), (512, F32)]
    return pl.pallas_call(
        body, name="norm_proj", grid=(T // TM,),
        in_specs=[rows(D_MODEL), _const_spec((1, D_MODEL)), _const_spec((IN_WIDTH, D_MODEL)), _const_spec((1, 512)),
                  _const_spec((1, 128)), _const_spec((512, 512)), rows(128), rows(128), rows(128)],
        out_specs=[rows(w) for w, _ in outs],
        out_shape=[jax.ShapeDtypeStruct((T, w), dt) for w, dt in outs],
        compiler_params=_params(1),
    )(x2, ng, w_bf, qg512, kg128, g512, cos, sa, sb)


SWA_Q = 512
SWA_SUB = SWA_Q // BLOCK


def _swa_scores(q, kst, mask, sink_ref, kv):
    s_all = _dot_nt(q, kst)
    first = lax.broadcasted_iota(jnp.int32, (256, 1), 0) < 128
    probs, stats = [], []
    for hh in range(2):
        sink = jnp.where(first, sink_ref[kv * 4 + hh], sink_ref[kv * 4 + 2 + hh])
        s = jnp.where(mask, s_all[:, hh * 256:(hh + 1) * 256], NEG)
        m = jnp.maximum(jnp.max(s, axis=1, keepdims=True), sink)
        pe = jnp.exp(s - m)
        inv = 1.0 / (jnp.sum(pe, axis=1, keepdims=True) + jnp.exp(sink - m))
        probs.append(pe * inv)
        stats.append(jnp.exp(sink - m) * inv)
    return probs, stats


def _swa_mask(has_prev):
    r = lax.broadcasted_iota(jnp.int32, (256, 256), 0) & 127
    c = lax.broadcasted_iota(jnp.int32, (256, 256), 1)
    band = (c > r) & (c <= r + 128)
    return band if has_prev is True else band & ((c >= 128) | has_prev)


def _stack_pairs(ref, rows, kv):
    return jnp.concatenate([ref[rows, (2 * kv) * 128:(2 * kv + 1) * 128], ref[rows, (2 * kv + 1) * 128:(2 * kv + 2) * 128]], axis=0)


def _swa_keys(prev_ref, main_ref, s, kv, m0b):
    cols = slice(kv * 128, (kv + 1) * 128)
    prev = prev_ref[:, cols] if s == 0 else main_ref[(s - 1) * 128:s * 128, cols]
    kc = jnp.concatenate([prev, main_ref[s * 128:(s + 1) * 128, cols]], axis=0)
    k0 = kc * m0b
    return jnp.concatenate([k0, kc - k0], axis=0)


def _swa_fwd(sinks, qrot, k2, v2, nbatch, seq):
    ni = seq // SWA_Q
    T = nbatch * seq

    def body(sink_ref, q_ref, kp_ref, km_ref, vp_ref, vm_ref, o_ref):
        i = pl.program_id(1)
        m0b = _lane_mask(256, BF16)
        for s in range(SWA_SUB):
            mask = _swa_mask(True if s else i > 0)
            rows = slice(s * 128, (s + 1) * 128)
            for kv in range(2):
                kst = _swa_keys(kp_ref, km_ref, s, kv, m0b)
                vst = _swa_keys(vp_ref, vm_ref, s, kv, m0b)
                probs, _ = _swa_scores(_stack_pairs(q_ref, rows, kv), kst, mask, sink_ref, kv)
                o2 = _dot(jnp.concatenate(probs, axis=1).astype(BF16), vst)
                o_ref[rows, kv * 256:kv * 256 + 128] = o2[0:128]
                o_ref[rows, kv * 256 + 128:(kv + 1) * 256] = o2[128:256]

    main = lambda b, i: (b * ni + i, 0)
    prev = lambda b, i: ((b * ni + i) * SWA_SUB - jnp.where(i > 0, 1, 0), 0)
    return pl.pallas_call(
        body, name="swa_fwd", grid=(nbatch, ni),
        in_specs=[pl.BlockSpec(memory_space=pltpu.SMEM), pl.BlockSpec((SWA_Q, 512), main),
                  pl.BlockSpec((128, 256), prev), pl.BlockSpec((SWA_Q, 256), main),
                  pl.BlockSpec((128, 256), prev), pl.BlockSpec((SWA_Q, 256), main)],
        out_specs=pl.BlockSpec((SWA_Q, 512), main),
        out_shape=jax.ShapeDtypeStruct((T, 512), F32),
        compiler_params=_params(2),
    )(sinks, qrot, k2, k2, v2, v2)


def _swa_bwd(sinks, qrot, k2, v2, doa, nbatch, seq):
    ni = seq // SWA_Q
    T = nbatch * seq

    def body(sink_ref, q_ref, kp_ref, km_ref, vp_ref, vm_ref, do_ref, dq_ref, dk_ref, dv_ref, ds_ref, dkc, dvc):
        b, i = pl.program_id(0), pl.program_id(1)

        @pl.when((b == 0) & (i == 0))
        def _():
            ds_ref[...] = jnp.zeros_like(ds_ref)

        @pl.when(i == 0)
        def _():
            dkc[...] = jnp.zeros_like(dkc)
            dvc[...] = jnp.zeros_like(dvc)

        @pl.when(i < ni)
        def _():
            m0b = _lane_mask(256, BF16)
            m0 = _lane_mask(128) > 0.5
            for kv in range(2):
                kcols = slice(kv * 128, (kv + 1) * 128)
                dk_own, dv_own = dkc[:, kcols], dvc[:, kcols]
                for s in range(SWA_SUB):
                    mask = _swa_mask(True if s else i > 0)
                    rows = slice(s * 128, (s + 1) * 128)
                    kst = _swa_keys(kp_ref, km_ref, s, kv, m0b)
                    vst = _swa_keys(vp_ref, vm_ref, s, kv, m0b)
                    q, do = _stack_pairs(q_ref, rows, kv), _stack_pairs(do_ref, rows, kv)
                    probs, psink = _swa_scores(q, kst, mask, sink_ref, kv)
                    dp_all = _dot_nt(do, vst)
                    ds_parts = []
                    for hh in range(2):
                        dp = dp_all[:, hh * 256:(hh + 1) * 256]
                        delta = jnp.sum(probs[hh] * dp, axis=1, keepdims=True)
                        ds_parts.append(probs[hh] * (dp - delta))
                        dsink = psink[hh] * delta
                        for pr in range(2):
                            h = kv * 4 + pr * 2 + hh
                            ds_ref[h:h + 1, :] = ds_ref[h:h + 1, :] - jnp.sum(dsink[pr * 128:(pr + 1) * 128])
                    ds_all = jnp.concatenate(ds_parts, axis=1).astype(BF16)
                    p_all = jnp.concatenate(probs, axis=1).astype(BF16)
                    dq2 = _dot(ds_all, kst) * SCALE
                    dq_ref[rows, kv * 256:kv * 256 + 128] = dq2[0:128]
                    dq_ref[rows, kv * 256 + 128:(kv + 1) * 256] = dq2[128:256]
                    dkst = _dot_tn(ds_all, q)
                    dvst = _dot_tn(p_all, do)
                    dk_ref[rows, kcols] = dk_own + jnp.where(m0, dkst[0:128], dkst[256:384])
                    dv_ref[rows, kcols] = dv_own + jnp.where(m0, dvst[0:128], dvst[256:384])
                    dk_own = jnp.where(m0, dkst[128:256], dkst[384:512])
                    dv_own = jnp.where(m0, dvst[128:256], dvst[384:512])
                dkc[:, kcols] = dk_own
                dvc[:, kcols] = dv_own

        @pl.when(i == ni)
        def _():
            dk_ref[...] = jnp.zeros_like(dk_ref)
            dv_ref[...] = jnp.zeros_like(dv_ref)
            dk_ref[0:128, :] = dkc[...]
            dv_ref[0:128, :] = dvc[...]

    main = lambda b, i: (b * ni + jnp.minimum(i, ni - 1), 0)
    prev = lambda b, i: ((b * ni + jnp.minimum(i, ni - 1)) * SWA_SUB - jnp.where(jnp.minimum(i, ni - 1) > 0, 1, 0), 0)
    shifted = lambda b, i: (b * (ni + 1) + i, 0)
    tpad = nbatch * (ni + 1) * SWA_Q
    return pl.pallas_call(
        body, name="swa_bwd", grid=(nbatch, ni + 1),
        in_specs=[pl.BlockSpec(memory_space=pltpu.SMEM), pl.BlockSpec((SWA_Q, 512), main),
                  pl.BlockSpec((128, 256), prev), pl.BlockSpec((SWA_Q, 256), main),
                  pl.BlockSpec((128, 256), prev), pl.BlockSpec((SWA_Q, 256), main),
                  pl.BlockSpec((SWA_Q, 512), main)],
        out_specs=[pl.BlockSpec((SWA_Q, 512), main), pl.BlockSpec((SWA_Q, 256), shifted),
                   pl.BlockSpec((SWA_Q, 256), shifted), _const_spec((8, 128))],
        out_shape=[jax.ShapeDtypeStruct((T, 512), F32), jax.ShapeDtypeStruct((tpad, 256), F32),
                   jax.ShapeDtypeStruct((tpad, 256), F32), jax.ShapeDtypeStruct((8, 128), F32)],
        scratch_shapes=[pltpu.VMEM((128, 256), F32), pltpu.VMEM((128, 256), F32)],
        compiler_params=_params(2),
    )(sinks, qrot, k2, k2, v2, v2, doa)


def _unshift(dkpad, nbatch, seq):
    return dkpad.reshape(nbatch, seq + SWA_Q, 256)[:, BLOCK:BLOCK + seq].reshape(nbatch * seq, 256)


SB_T = 256
SB_TQ = 2 * SB_T


def _sb_mask(kind):
    if kind == "full":
        return None
    rows = SB_T if kind == "B" else SB_TQ
    r = lax.broadcasted_iota(jnp.int32, (rows, 2 * SB_T), 0)
    c = lax.broadcasted_iota(jnp.int32, (rows, 2 * SB_T), 1)
    causal = (c & (SB_T - 1)) < r
    return causal | (r >= SB_T) if kind == "A" else causal


def _sb_rows(kind):
    return slice(SB_T, SB_TQ) if kind == "B" else slice(0, SB_TQ)


def _lower(x):
    return jnp.concatenate([jnp.zeros_like(x), x], axis=0)


def _sb_logits(neg_q, kst):
    nz = _dot_nt(neg_q, kst)
    sign = jnp.uint32(0x80000000)
    neg_abs = lax.bitcast_convert_type(lax.bitcast_convert_type(nz, jnp.uint32) | sign, F32)
    return nz, jnp.minimum(nz, 0.0) - jnp.log(1.0 + jnp.exp(neg_abs))


SB_NP = 4


def _pair_rows(ref, j, cols, m0b):
    kj = ref[pl.ds(pl.multiple_of(j * SB_T, SB_T), SB_T), cols]
    k0 = kj * m0b
    return jnp.concatenate([k0, kj - k0], axis=0)


def _bcast2(c0, c1):
    rows = c0.shape[0]
    return jnp.concatenate([jnp.broadcast_to(c0, (rows, SB_T)), jnp.broadcast_to(c1, (rows, SB_T))], axis=1)


def _rowsum2(x):
    return jnp.sum(x[:, 0:SB_T], axis=1, keepdims=True), jnp.sum(x[:, SB_T:2 * SB_T], axis=1, keepdims=True)


def _scan2(x, tri2):
    outs = []
    for h in range(2):
        xh = x[:, h * SB_T:(h + 1) * SB_T]
        hi = xh.astype(BF16)
        lo = (xh - hi.astype(F32)).astype(BF16)
        outs.append(_dot(jnp.concatenate([hi, lo], axis=1), tri2))
    return jnp.concatenate(outs, axis=1)


def _scan1(x, tri):
    xb = x.astype(BF16)
    return jnp.concatenate([_dot(xb[:, h * SB_T:(h + 1) * SB_T], tri) for h in range(2)], axis=1)


def _sb_fwd(qb, kb, vb, ublk, nbatch, seq):
    nq, nk = seq // SB_TQ, seq // SB_T
    T = nbatch * seq

    def body(q_ref, k_ref, v_ref, u_ref, o_ref, ctab_ref):
        i = pl.program_id(2)
        m0b = _lane_mask(SB_T, BF16)
        u = u_ref[...]
        lane = lax.broadcasted_iota(jnp.int32, (SB_TQ, 128), 1)
        ctab_ref[...] = jnp.zeros_like(ctab_ref)
        pairs = [slice(pp * 128, (pp + 1) * 128) for pp in range(SB_NP)]
        neg_qs = [-q_ref[:, cols] for cols in pairs]

        def tile(j, carries, kind):
            rows, mask = _sb_rows(kind), _sb_mask(kind)
            out = []
            for cols, neg_q, (c0, c1, acc) in zip(pairs, neg_qs, carries):
                if kind != "B":
                    ctab_ref[:, cols] = jnp.where(lane == j, c0, jnp.where(lane == nk + j, c1, ctab_ref[:, cols]))
                kst = _pair_rows(k_ref, j, cols, m0b)
                vst = _pair_rows(v_ref, j, cols, m0b)
                nz, lb = _sb_logits(neg_q[rows], kst)
                if mask is not None:
                    lb = jnp.where(mask, lb, 0.0)
                incl = _scan2(lb, u)
                w = jnp.exp(incl - nz if kind == "B" else incl + _bcast2(c0, c1) - nz)
                if mask is not None:
                    w = jnp.where(mask, w, 0.0)
                d0, d1, da = incl[:, 0:1], incl[:, SB_T:SB_T + 1], _dot(w.astype(BF16), vst)
                if kind == "B":
                    d0, d1, da = _lower(d0), _lower(d1), _lower(da)
                out.append((c0 + d0, c1 + d1, acc + da))
            return tuple(out)

        zc = jnp.zeros((SB_TQ, 1), F32)
        carries = tile(2 * i + 1, ((zc, zc, jnp.zeros((SB_TQ, 128), F32)),) * SB_NP, "B")
        carries = tile(2 * i, carries, "A")
        carries = lax.fori_loop(0, 2 * i, lambda jj, cr: tile(2 * i - 1 - jj, cr, "full"), carries)
        for cols, carry in zip(pairs, carries):
            o_ref[:, cols] = carry[2]

    wide = 128 * SB_NP
    blk = lambda b, g, i: (b * nq + i, g)
    full = lambda b, g, i: (b, g)
    return pl.pallas_call(
        body, name="sb_fwd", grid=(nbatch, 4 // SB_NP, nq),
        in_specs=[pl.BlockSpec((SB_TQ, wide), blk), pl.BlockSpec((seq, wide), full), pl.BlockSpec((seq, wide), full),
                  _const_spec((2 * SB_T, SB_T))],
        out_specs=[pl.BlockSpec((SB_TQ, wide), blk), pl.BlockSpec((SB_TQ, wide), blk)],
        out_shape=[jax.ShapeDtypeStruct((T, 512), F32), jax.ShapeDtypeStruct((T, 512), F32)],
        compiler_params=_params(3),
    )(qb, kb, vb, ublk)


def _sb_bwd(qb, kb, vb, dob, ctab, ublk, pblk, nbatch, seq):
    nq, nk = seq // SB_TQ, seq // SB_T
    T = nbatch * seq

    def body(q_ref, k_ref, v_ref, do_ref, ctab_ref, u_ref, up_ref, dq_ref, dk_ref, dv_ref):
        i = pl.program_id(2)

        @pl.when(i == 0)
        def _():
            dk_ref[...] = jnp.zeros_like(dk_ref)
            dv_ref[...] = jnp.zeros_like(dv_ref)

        m0b = _lane_mask(SB_T, BF16)
        m0 = _lane_mask(SB_T) > 0.5
        u, up = u_ref[...], up_ref[...]
        lane = lax.broadcasted_iota(jnp.int32, (SB_TQ, 128), 1)
        pairs = [slice(pp * 128, (pp + 1) * 128) for pp in range(SB_NP)]
        neg_qs = [-q_ref[:, cols] for cols in pairs]

        def tile(j, carries, kind):
            rows, mask = _sb_rows(kind), _sb_mask(kind)
            out = []
            for cols, neg_q, (s0, s1, dq) in zip(pairs, neg_qs, carries):
                q, do = q_ref[rows, cols], do_ref[rows, cols]
                kst = _pair_rows(k_ref, j, cols, m0b)
                vst = _pair_rows(v_ref, j, cols, m0b)
                nz, lb = _sb_logits(neg_q[rows], kst)
                one_minus_beta = jnp.exp(lb)
                if mask is not None:
                    lb = jnp.where(mask, lb, 0.0)
                if kind == "B":
                    w = jnp.exp(_scan2(lb, u) - nz)
                else:
                    ct = ctab_ref[:, cols]
                    c0 = jnp.sum(jnp.where(lane == j, ct, 0.0), axis=1, keepdims=True)
                    c1 = jnp.sum(jnp.where(lane == nk + j, ct, 0.0), axis=1, keepdims=True)
                    w = jnp.exp(_scan2(lb, u) + _bcast2(c0, c1) - nz)
                if mask is not None:
                    w = jnp.where(mask, w, 0.0)
                e = _dot_nt(do, vst) * w
                dlb = _bcast2(s0[rows], s1[rows]) + _scan1(e, up)
                dz = (e + dlb) * one_minus_beta - dlb
                if mask is not None:
                    dz = jnp.where(mask, dz, 0.0)
                dzb = dz.astype(BF16)
                dkst = _dot_tn(dzb, q)
                dvst = _dot_tn(w.astype(BF16), do)
                keys = pl.ds(pl.multiple_of(j * SB_T, SB_T), SB_T)
                dk_ref[keys, cols] = dk_ref[keys, cols] + jnp.where(m0, dkst[0:SB_T], dkst[SB_T:2 * SB_T])
                dv_ref[keys, cols] = dv_ref[keys, cols] + jnp.where(m0, dvst[0:SB_T], dvst[SB_T:2 * SB_T])
                x0, x1 = _rowsum2(e)
                ddq = _dot(dzb, kst)
                if kind == "B":
                    x0, x1, ddq = _lower(x0), _lower(x1), _lower(ddq)
                out.append((s0 + x0, s1 + x1, dq + ddq))
            return tuple(out)

        zc = jnp.zeros((SB_TQ, 1), F32)
        carries = lax.fori_loop(0, 2 * i, lambda j, cr: tile(j, cr, "full"),
                                ((zc, zc, jnp.zeros((SB_TQ, 128), F32)),) * SB_NP)
        carries = tile(2 * i, carries, "A")
        carries = tile(2 * i + 1, carries, "B")
        for cols, carry in zip(pairs, carries):
            dq_ref[:, cols] = carry[2] * SCALE

    wide = 128 * SB_NP
    blk = lambda b, g, i: (b * nq + i, g)
    full = lambda b, g, i: (b, g)
    return pl.pallas_call(
        body, name="sb_bwd", grid=(nbatch, 4 // SB_NP, nq),
        in_specs=[pl.BlockSpec((SB_TQ, wide), blk), pl.BlockSpec((seq, wide), full), pl.BlockSpec((seq, wide), full),
                  pl.BlockSpec((SB_TQ, wide), blk), pl.BlockSpec((SB_TQ, wide), blk),
                  _const_spec((2 * SB_T, SB_T)), _const_spec((SB_T, SB_T))],
        out_specs=[pl.BlockSpec((SB_TQ, wide), blk), pl.BlockSpec((seq, wide), full), pl.BlockSpec((seq, wide), full)],
        out_shape=[jax.ShapeDtypeStruct((T, 512), F32)] * 3,
        compiler_params=_params(3),
    )(qb, kb, vb, dob, ctab, ublk, pblk)


def _sigmoid(g):
    return 1.0 / (1.0 + jnp.exp(-g))


def _out_proj(oa, ob, ga, gb, x2, tgt, wout_bf):
    T = x2.shape[0]

    def body(oa_ref, ob_ref, ga_ref, gb_ref, x_ref, t_ref, w_ref,
             dout_ref, doa_ref, dob_ref, dga_ref, dgb_ref, dw_ref, loss_ref):
        @pl.when(pl.program_id(0) == 0)
        def _():
            loss_ref[...] = jnp.zeros_like(loss_ref)
            dw_ref[...] = jnp.zeros_like(dw_ref)

        halves = ((oa_ref, ga_ref, doa_ref, dga_ref, 0), (ob_ref, gb_ref, dob_ref, dgb_ref, 512))
        out = x_ref[...]
        gated = []
        for o_ref, g_ref, _, _, lo in halves:
            g = g_ref[...]
            sg = _sigmoid(g)
            y = (o_ref[...] * (g * sg)).astype(BF16)
            out = out + _dot(y, w_ref[lo:lo + 512, :])
            gated.append((y, g, sg))
        diff = out - t_ref[...]
        dout = diff * (1.0 / D_MODEL)
        dout_ref[...] = dout
        loss_ref[...] = loss_ref[...] + jnp.sum(diff * diff) * (0.5 / D_MODEL)
        db = dout.astype(BF16)
        for (o_ref, _, do_ref, dg_ref, lo), (y, g, sg) in zip(halves, gated):
            dw_ref[lo:lo + 512, :] = dw_ref[lo:lo + 512, :] + _dot_tn(y, db)
            dy = _dot_nt(db, w_ref[lo:lo + 512, :])
            do_ref[...] = (dy * (g * sg)).astype(BF16)
            dg_ref[...] = (dy * o_ref[...] * (sg * (1.0 + g * (1.0 - sg)))).astype(BF16)

    rows = lambda w: pl.BlockSpec((TM, w), lambda i: (i, 0))
    return pl.pallas_call(
        body, name="out_proj", grid=(T // TM,),
        in_specs=[rows(512), rows(512), rows(512), rows(512), rows(D_MODEL), rows(D_MODEL),
                  _const_spec((D_MODEL, D_MODEL))],
        out_specs=[rows(D_MODEL)] + [rows(512)] * 4 + [_const_spec((D_MODEL, D_MODEL)), _const_spec((8, 128))],
        out_shape=[jax.ShapeDtypeStruct((T, D_MODEL), F32)] + [jax.ShapeDtypeStruct((T, 512), BF16)] * 4
        + [jax.ShapeDtypeStruct((D_MODEL, D_MODEL), F32), jax.ShapeDtypeStruct((8, 128), F32)],
        compiler_params=_params(1),
    )(oa, ob, ga, gb, x2, tgt, wout_bf)


def _qk_grad(qraw, kraw, dqrot, dk2, dv2, qg512, kg128, g512, cos, sa, sb):
    T = qraw.shape[0]

    def body(qraw_ref, kraw_ref, dqrot_ref, dk2_ref, dv2_ref, qg_ref, kg_ref, g_ref, cos_ref, sa_ref, sb_ref,
             dqa_ref, dkv_ref, dqg_ref, dkg_ref):
        @pl.when(pl.program_id(0) == 0)
        def _():
            dqg_ref[...] = jnp.zeros_like(dqg_ref)
            dkg_ref[...] = jnp.zeros_like(dkg_ref)

        cosv, sav, sbv = cos_ref[...], sa_ref[...], sb_ref[...]
        m0 = _lane_mask(TM) > 0.5

        def head_norm_bwd(raw, dn_rot, gmat, gain):
            r = lax.rsqrt(_dot_hl(raw * raw, gmat) + EPS)
            n = raw * r
            dn = dn_rot * gain
            return r * (dn - n * _dot_hl(dn * n, gmat)), jnp.sum(dn_rot * n, axis=0, keepdims=True)

        def fold(ref):
            a, b = ref[:, 0:128], ref[:, 128:256]
            return jnp.where(m0, a + pltpu.roll(a, 64, 1), b + pltpu.roll(b, 64, 1))

        dqn = jnp.concatenate([_rope_bwd(dqrot_ref[:, s * 128:(s + 1) * 128], cosv, sav, sbv) for s in range(4)], axis=1)
        dqa, dqg = head_norm_bwd(qraw_ref[...], dqn, g_ref[...], qg_ref[...])
        dka, dkg = head_norm_bwd(kraw_ref[...], _rope_bwd(fold(dk2_ref), cosv, sav, sbv), g_ref[0:128, 0:128], kg_ref[...])
        dqa_ref[...] = dqa.astype(BF16)
        dkv_ref[:, 0:128] = dka.astype(BF16)
        dkv_ref[:, 128:256] = fold(dv2_ref).astype(BF16)
        dqg_ref[...] = dqg_ref[...] + dqg
        dkg_ref[...] = dkg_ref[...] + dkg

    rows = lambda w: pl.BlockSpec((TM, w), lambda i: (i, 0))
    return pl.pallas_call(
        body, name="qk_grad", grid=(T // TM,),
        in_specs=[rows(512), rows(128), rows(512), rows(256), rows(256), _const_spec((1, 512)), _const_spec((1, 128)),
                  _const_spec((512, 512)), rows(128), rows(128), rows(128)],
        out_specs=[rows(512), rows(256), _const_spec((1, 512)), _const_spec((1, 128))],
        out_shape=[jax.ShapeDtypeStruct((T, 512), BF16), jax.ShapeDtypeStruct((T, 256), BF16),
                   jax.ShapeDtypeStruct((1, 512), F32), jax.ShapeDtypeStruct((1, 128), F32)],
        compiler_params=_params(1),
    )(qraw, kraw, dqrot, dk2, dv2, qg512, kg128, g512, cos, sa, sb)


_PIECES = ((QA, 512), (KA, 256), (GA, 512), (QB, 512), (KB, 512), (VB, 512), (GB, 512))


def _w_in_grad(h, pieces):
    T = h.shape[0]

    def body(h_ref, *refs):
        dw_ref = refs[-1]

        @pl.when(pl.program_id(0) == 0)
        def _():
            dw_ref[...] = jnp.zeros_like(dw_ref)

        hb = h_ref[...]
        for (lo, width), p_ref in zip(_PIECES, refs[:-1]):
            dw_ref[lo:lo + width, :] = dw_ref[lo:lo + width, :] + _dot_tn(p_ref[...].astype(BF16), hb)

    rows = lambda w: pl.BlockSpec((TM, w), lambda i: (i, 0))
    return pl.pallas_call(
        body, name="w_in_grad", grid=(T // TM,),
        in_specs=[rows(D_MODEL)] + [rows(w) for _, w in _PIECES],
        out_specs=_const_spec((IN_WIDTH, D_MODEL)),
        out_shape=jax.ShapeDtypeStruct((IN_WIDTH, D_MODEL), F32),
        compiler_params=_params(1),
    )(h, *pieces)


def _x_grad(x2, dout, pieces, w_bf, ng, cin4, cout4):
    T = x2.shape[0]
    npc = len(_PIECES)
    steps = T // TM

    def body(x_ref, dout_ref, *refs):
        w_ref, ng_ref, cin_ref, cout_ref, gx_ref, dng_ref, rin_ref, rout_ref, ssem, rsem = refs[npc:]
        step = pl.program_id(0)
        _, _, c, chips = _place()
        sems = (ssem, rsem)

        def copies():
            out = []
            for j, (cx, cy) in enumerate(chips):
                k = 2 * cx + cy
                out.append(_remote(cin_ref.at[k], rin_ref.at[j], sems, 2 * j, (cx, cy, c)))
                out.append(_remote(cout_ref.at[k], rout_ref.at[j], sems, 2 * j + 1, (cx, cy, c)))
            return out

        @pl.when(step == 0)
        def _():
            dng_ref[...] = jnp.zeros_like(dng_ref)
            for cp in copies():
                cp.start()

        dh = jnp.zeros((TM, D_MODEL), F32)
        for (lo, width), p_ref in zip(_PIECES, refs[:npc]):
            dh = dh + _dot(p_ref[...].astype(BF16), w_ref[lo:lo + width, :])
        xb = x_ref[...]
        r = lax.rsqrt(jnp.mean(xb * xb, axis=-1, keepdims=True) + EPS)
        n = xb * r
        dn = dh * ng_ref[...]
        gx_ref[...] = dout_ref[...] + r * (dn - n * jnp.mean(dn * n, axis=-1, keepdims=True))
        dng_ref[...] = dng_ref[...] + jnp.sum(dh * n, axis=0, keepdims=True)

        @pl.when(step == steps - 1)
        def _():
            for cp in copies():
                cp.wait_recv()
            for cp in copies():
                cp.wait_send()

    rows = lambda w: pl.BlockSpec((TM, w), lambda i: (i, 0))
    hbm = pl.BlockSpec(memory_space=pl.ANY)
    return pl.pallas_call(
        body, name="x_grad", grid=(steps,),
        in_specs=[rows(D_MODEL), rows(D_MODEL)] + [rows(w) for _, w in _PIECES]
        + [_const_spec((IN_WIDTH, D_MODEL)), _const_spec((1, D_MODEL)), hbm, hbm],
        out_specs=[rows(D_MODEL), _const_spec((1, D_MODEL)), hbm, hbm],
        out_shape=[jax.ShapeDtypeStruct((T, D_MODEL), F32), jax.ShapeDtypeStruct((1, D_MODEL), F32),
                   jax.ShapeDtypeStruct((3,) + cin4.shape[1:], cin4.dtype),
                   jax.ShapeDtypeStruct((3,) + cout4.shape[1:], cout4.dtype)],
        scratch_shapes=[pltpu.SemaphoreType.DMA((6,)), pltpu.SemaphoreType.DMA((6,))],
        compiler_params=_params(1),
    )(x2, dout, *pieces, w_bf, ng, cin4, cout4)


HBM = pl.BlockSpec(memory_space=pl.ANY)


def _place():
    x, y, c = lax.axis_index("x"), lax.axis_index("y"), lax.axis_index("c")
    chips = [(1 - x, y), (x, 1 - y), (1 - x, 1 - y)]
    return x, y, c, chips


def _remote(src, dst, sems, k, to):
    return pltpu.make_async_remote_copy(src_ref=src, dst_ref=dst, send_sem=sems[0].at[k], recv_sem=sems[1].at[k],
                                        device_id=to, device_id_type=MESH)


def _gather_weights(shards):
    n = len(shards)

    def body(*refs):
        srcs, dsts, (ssem, rsem) = refs[:n], refs[n:2 * n], refs[2 * n:]
        x, y, c, chips = _place()
        me, sib = 2 * x + y, (x, y, 1 - c)
        sems = (ssem, rsem)
        sent = []
        for j, (cx, cy) in enumerate(chips):
            for a in range(n):
                sent.append(_remote(srcs[a].at[c], dsts[a].at[me, c], sems, n * j + a, (cx, cy, c)))
        for cp in sent:
            cp.start()
        for j, (cx, cy) in enumerate(chips):
            k = 2 * cx + cy
            for a in range(n):
                _remote(srcs[a].at[c], dsts[a].at[k, c], sems, n * j + a, sib).wait_recv()
                fwd = _remote(dsts[a].at[k, c], dsts[a].at[k, c], sems, 3 * n + n * j + a, sib)
                fwd.start()
                sent.append(fwd)
        for j, (cx, cy) in enumerate(chips):
            k = 2 * cx + cy
            for a in range(n):
                _remote(srcs[a].at[c], dsts[a].at[k, 1 - c], sems, 3 * n + n * j + a, sib).wait_recv()
        for cp in sent:
            cp.wait_send()

    gathered = pl.pallas_call(
        body, name="gather_weights", in_specs=[HBM] * n, out_specs=[HBM] * n,
        out_shape=[jax.ShapeDtypeStruct((4,) + s.shape, s.dtype) for s in shards],
        scratch_shapes=[pltpu.SemaphoreType.DMA((6 * n,)), pltpu.SemaphoreType.DMA((6 * n,))],
    )(*shards)
    me = 2 * lax.axis_index("x") + lax.axis_index("y")
    return [lax.dynamic_update_slice(g, s[None], (me, 0, 0, 0)) for g, s in zip(gathered, shards)]


def _pair_exchange(grads):
    n = len(grads)

    def body(*refs):
        srcs, dsts, (ssem, rsem) = refs[:n], refs[n:2 * n], refs[2 * n:]
        x, y, c, _ = _place()
        sems = (ssem, rsem)
        sib = (x, y, 1 - c)
        sent = [_remote(srcs[a].at[:, pl.ds(1 - c, 1)], dsts[a], sems, a, sib) for a in range(n)]
        for cp in sent:
            cp.start()
        for cp in sent:
            cp.wait()

    return pl.pallas_call(
        body, name="pair_exchange", in_specs=[HBM] * n, out_specs=[HBM] * n,
        out_shape=[jax.ShapeDtypeStruct((4, 1) + g.shape[2:], g.dtype) for g in grads],
        scratch_shapes=[pltpu.SemaphoreType.DMA((n,)), pltpu.SemaphoreType.DMA((n,))],
    )(*grads)


def _share_halves(hin, hout, small):
    def body(hin_ref, hout_ref, small_ref, oin_ref, oout_ref, sall_ref, ssem, rsem, lsem):
        x, y, c, _ = _place()
        sems = (ssem, rsem)
        sib = (x, y, 1 - c)
        me = 4 * x + 2 * y + c
        own = pltpu.make_async_copy(small_ref, sall_ref.at[me], lsem.at[0])
        own.start()
        sent = [_remote(hin_ref, oin_ref, sems, 0, sib), _remote(hout_ref, oout_ref, sems, 1, sib)]
        flips = [(fx, fy, fc) for fx in (0, 1) for fy in (0, 1) for fc in (0, 1)][1:]
        for k, (fx, fy, fc) in enumerate(flips):
            sent.append(_remote(small_ref, sall_ref.at[me], sems, 2 + k, (x ^ fx, y ^ fy, c ^ fc)))
        for cp in sent:
            cp.start()
        _remote(hin_ref, oin_ref, sems, 0, sib).wait_recv()
        _remote(hout_ref, oout_ref, sems, 1, sib).wait_recv()
        for k, (fx, fy, fc) in enumerate(flips):
            src = 4 * (x ^ fx) + 2 * (y ^ fy) + (c ^ fc)
            _remote(small_ref, sall_ref.at[src], sems, 2 + k, sib).wait_recv()
        for cp in sent:
            cp.wait_send()
        own.wait()

    return pl.pallas_call(
        body, name="share_halves", in_specs=[HBM, HBM, HBM], out_specs=[HBM, HBM, HBM],
        out_shape=[jax.ShapeDtypeStruct(hin.shape, F32), jax.ShapeDtypeStruct(hout.shape, F32),
                   jax.ShapeDtypeStruct((8,) + small.shape, F32)],
        scratch_shapes=[pltpu.SemaphoreType.DMA((9,)), pltpu.SemaphoreType.DMA((9,)), pltpu.SemaphoreType.DMA((1,))],
    )(hin, hout, small)


def _add_half(cidx, full4, recv4):
    _, _, rows, width = full4.shape

    def body(c_ref, a_ref, b_ref, o_ref):
        o_ref[0] = (a_ref[0, 0] + b_ref[0, 0]).astype(BF16)

    return pl.pallas_call(
        body, name=f"add_half_{rows}",
        grid_spec=pltpu.PrefetchScalarGridSpec(
            num_scalar_prefetch=1, grid=(4,),
            in_specs=[pl.BlockSpec((1, 1, rows, width), lambda k, c: (k, c[0], 0, 0)),
                      pl.BlockSpec((1, 1, rows, width), lambda k, c: (k, 0, 0, 0))],
            out_specs=pl.BlockSpec((1, rows, width), lambda k, c: (k, 0, 0))),
        out_shape=jax.ShapeDtypeStruct((4, rows, width), BF16),
        compiler_params=_params(1),
    )(cidx, full4, recv4)


def _sum_chips(chip, own4, recv3):
    _, rows, width = recv3.shape
    rb = rows // 2

    def body(k_ref, a_ref, r_ref, o_ref):
        acc = a_ref[0].astype(F32)
        for s in range(3):
            acc = acc + r_ref[s].astype(F32)
        o_ref[...] = acc

    return pl.pallas_call(
        body, name=f"sum_chips_{rows}",
        grid_spec=pltpu.PrefetchScalarGridSpec(
            num_scalar_prefetch=1, grid=(rows // rb,),
            in_specs=[pl.BlockSpec((1, rb, width), lambda i, k: (k[0], i, 0)),
                      pl.BlockSpec((3, rb, width), lambda i, k: (0, i, 0))],
            out_specs=pl.BlockSpec((rb, width), lambda i, k: (i, 0))),
        out_shape=jax.ShapeDtypeStruct((rows, width), F32),
        compiler_params=_params(1),
    )(chip, own4, recv3)


def _sum_slots(r4):
    n, rows, width = r4.shape

    def body(r_ref, o_ref):
        acc = r_ref[0]
        for s in range(1, n):
            acc = acc + r_ref[s]
        o_ref[...] = acc

    return pl.pallas_call(
        body, name=f"sum_slots_{n}_{rows}_{width}", grid=(1,),
        in_specs=[pl.BlockSpec((n, rows, width), lambda i: (0, 0, 0))],
        out_specs=pl.BlockSpec((rows, width), lambda i: (0, 0)),
        out_shape=jax.ShapeDtypeStruct((rows, width), F32),
        compiler_params=_params(1),
    )(r4)


def _adam_math(w, g, m, v):
    c1 = 1.0 - ADAM_B1 ** ADAM_STEP
    c2 = 1.0 - ADAM_B2 ** ADAM_STEP
    nm = ADAM_B1 * m + (1.0 - ADAM_B1) * g
    nv = ADAM_B2 * v + (1.0 - ADAM_B2) * (g * g)
    return -ADAM_LR * ((nm / c1) / (jnp.sqrt(nv / c2) + ADAM_EPS) + ADAM_WD * w), nm, nv


def _adamw(w, g, m, v):
    rows, width = w.shape

    def body(w_ref, g_ref, m_ref, v_ref, d_ref, nm_ref, nv_ref):
        d_ref[...], nm_ref[...], nv_ref[...] = _adam_math(w_ref[...], g_ref[...], m_ref[...], v_ref[...])

    spec = pl.BlockSpec((rows, width), lambda i: (0, 0))
    return pl.pallas_call(
        body, name=f"adamw_{rows}_{width}", grid=(1,),
        in_specs=[spec] * 4, out_specs=[spec] * 3,
        out_shape=[jax.ShapeDtypeStruct((rows, width), F32)] * 3,
        compiler_params=_params(1),
    )(w, g, m, v)


def _adamw_halves(cidx, w, own, recv, m, v):
    rows, width = w.shape
    rb = rows // 4

    def body(c_ref, w_ref, own_ref, recv_ref, m_ref, v_ref, g_ref, d_ref, nm_ref, nv_ref):
        mine = (pl.program_id(0) // 2) == c_ref[0]
        g = jnp.where(mine, own_ref[...], recv_ref[...])
        g_ref[...] = g
        d_ref[...], nm_ref[...], nv_ref[...] = _adam_math(w_ref[...], g, m_ref[...], v_ref[...])

    full = pl.BlockSpec((rb, width), lambda i, c: (i, 0))
    half = pl.BlockSpec((rb, width), lambda i, c: (i % 2, 0))
    return pl.pallas_call(
        body, name=f"adamw_halves_{rows}",
        grid_spec=pltpu.PrefetchScalarGridSpec(
            num_scalar_prefetch=1, grid=(4,),
            in_specs=[full, half, half, full, full], out_specs=[full] * 4),
        out_shape=[jax.ShapeDtypeStruct((rows, width), F32)] * 4,
        compiler_params=_params(1),
    )(cidx, w, own, recv, m, v)


def _rope_tables(positions):
    half = HEAD_DIM // 2
    inv_freq = ROPE_THETA ** (-jnp.arange(half, dtype=F32) * 2.0 / HEAD_DIM)
    ang = positions.astype(F32).reshape(-1, 1) * inv_freq
    cos, sin, zero = jnp.cos(ang), jnp.sin(ang), jnp.zeros_like(ang)
    return (jnp.concatenate([cos] * 4, axis=1), jnp.concatenate([-sin, zero] * 2, axis=1),
            jnp.concatenate([zero, sin] * 2, axis=1))


def _constants():
    idx = jnp.arange(512)
    g512 = jnp.where(idx[:, None] // HEAD_DIM == idx[None, :] // HEAD_DIM, 1.0 / HEAD_DIM, 0.0).astype(BF16)
    j = jnp.arange(SB_T)
    ublk = jnp.where(j[:, None] >= j[None, :], 1.0, 0.0).astype(BF16)
    pblk = jnp.where(j[:, None] < j[None, :], 1.0, 0.0).astype(BF16)
    return g512, jnp.concatenate([ublk, ublk], axis=0), pblk


def _pad_rows(v, width):
    return jnp.pad(v, ((0, 0), (0, width - v.shape[1])))


def _forward_backward(x2, tgt, positions, norm_gain, q_norm_gain, k_norm_gain, sinks, w_bf, wout_bf, nbatch, seq):
    cos, sa, sb = _rope_tables(positions)
    g512, ublk, pblk = _constants()
    qg512 = jnp.tile(q_norm_gain, (1, 8))
    kg128 = jnp.tile(k_norm_gain, (1, 2))
    sink1 = sinks.reshape(8)

    h, qraw, kraw, qrot, k2, v2, ga, qb, kb, vb, gb = _norm_proj(x2, norm_gain, w_bf, qg512, kg128, g512, cos, sa, sb)
    oa = _swa_fwd(sink1, qrot, k2, v2, nbatch, seq)
    ob, ctab = _sb_fwd(qb, kb, vb, ublk, nbatch, seq)
    dout, doa, dob, dga, dgb, dwout, loss_acc = _out_proj(oa, ob, ga, gb, x2, tgt, wout_bf)

    dqrot, dk2, dv2, dsink = _swa_bwd(sink1, qrot, k2, v2, doa, nbatch, seq)
    dqb, dkb, dvb = _sb_bwd(qb, kb, vb, dob, ctab, ublk, pblk, nbatch, seq)
    dk2, dv2 = _unshift(dk2, nbatch, seq), _unshift(dv2, nbatch, seq)
    dqa, dkv, dqg, dkg = _qk_grad(qraw, kraw, dqrot, dk2, dv2, qg512, kg128, g512, cos, sa, sb)
    pieces = (dqa, dkv, dga, dqb, dkb, dvb, dgb)
    dwin_t = _w_in_grad(h, pieces)
    dqg64 = dqg.reshape(8, HEAD_DIM).sum(axis=0, keepdims=True)
    dkg64 = dkg.reshape(2, HEAD_DIM).sum(axis=0, keepdims=True)
    small = jnp.concatenate([_pad_rows(dqg64, D_MODEL), _pad_rows(dkg64, D_MODEL), _pad_rows(dsink[:, 0].reshape(1, 8), D_MODEL),
                             _pad_rows(loss_acc[0:1, 0:1], D_MODEL), jnp.zeros((3, D_MODEL), F32)], axis=0)
    return dwin_t, dwout, small, (x2, dout, pieces, w_bf, norm_gain)


def _reduce_grads(dwin_t, dwout, small7, x_grad_args):
    gin4 = dwin_t.reshape(4, 2, SHARD_IN // 2, D_MODEL)
    gout4 = dwout.reshape(4, 2, SHARD_OUT // 2, D_MODEL)
    rin, rout = _pair_exchange([gin4, gout4])
    cidx = lax.axis_index("c").astype(jnp.int32).reshape(1)
    chip = (2 * lax.axis_index("x") + lax.axis_index("y")).astype(jnp.int32).reshape(1)
    cin4, cout4 = _add_half(cidx, gin4, rin), _add_half(cidx, gout4, rout)
    gx, dng, rin3, rout3 = _x_grad(*x_grad_args, cin4, cout4)
    own_in, own_out = _sum_chips(chip, cin4, rin3), _sum_chips(chip, cout4, rout3)
    sib_in, sib_out, small_all = _share_halves(own_in, own_out, jnp.concatenate([dng, small7], axis=0))
    return gx, cidx, own_in, sib_in, own_out, sib_out, _sum_slots(small_all)


def kernel(x, positions, norm_gain, w_in, q_norm_gain, k_norm_gain, sinks, w_out, loss_target, m_norm_gain, m_w_in, m_q_norm_gain, m_k_norm_gain, m_sinks, m_w_out, v_norm_gain, v_w_in, v_q_norm_gain, v_k_norm_gain, v_sinks, v_w_out):
    nbatch, seq, _ = x.shape
    T = nbatch * seq
    x2 = x.reshape(T, D_MODEL)
    tgt = loss_target.reshape(T, D_MODEL)
    tr = lambda a: jnp.swapaxes(a[0], 0, 1)
    win_t, m_win_t, v_win_t = tr(w_in), tr(m_w_in), tr(v_w_in)

    ain, aout = _gather_weights([win_t.astype(BF16).reshape(2, SHARD_IN // 2, D_MODEL),
                                 w_out[0].astype(BF16).reshape(2, SHARD_OUT // 2, D_MODEL)])
    w_bf, wout_bf = ain.reshape(IN_WIDTH, D_MODEL), aout.reshape(D_MODEL, D_MODEL)

    dwin_t, dwout, small7, x_grad_args = _forward_backward(
        x2, tgt, positions, norm_gain, q_norm_gain, k_norm_gain, sinks, w_bf, wout_bf, nbatch, seq)

    gx, cidx, own_in, sib_in, own_out, sib_out, g_small = _reduce_grads(dwin_t, dwout, small7, x_grad_args)
    loss = g_small[4, 0]

    g_in, d_in, nm_in, nv_in = [jnp.swapaxes(a, 0, 1) for a in
                                _adamw_halves(cidx, win_t, own_in, sib_in, m_win_t, v_win_t)]
    g_out, d_out, nm_out, nv_out = _adamw_halves(cidx, w_out[0], own_out, sib_out, m_w_out[0], v_w_out[0])
    pack = lambda a, b, c_, d: jnp.concatenate(
        [a, _pad_rows(b, D_MODEL), _pad_rows(c_, D_MODEL), _pad_rows(d, D_MODEL), jnp.zeros((4, D_MODEL), F32)], axis=0)
    w_s = pack(norm_gain, q_norm_gain, k_norm_gain, sinks)
    m_s = pack(m_norm_gain, m_q_norm_gain, m_k_norm_gain, m_sinks)
    v_s = pack(v_norm_gain, v_q_norm_gain, v_k_norm_gain, v_sinks)
    d_s, nm_s, nv_s = _adamw(w_s, g_small, m_s, v_s)
    unpack = lambda a: (a[0:1, :], a[1:2, 0:HEAD_DIM], a[2:3, 0:HEAD_DIM], a[3:4, 0:8])

    g_ng, g_qg, g_kg, g_sk = unpack(g_small)
    d_ng, d_qg, d_kg, d_sk = unpack(d_s)
    m_ng, m_qg, m_kg, m_sk = unpack(nm_s)
    v_ng, v_qg, v_kg, v_sk = unpack(nv_s)
    return (loss, gx.reshape(nbatch, seq, D_MODEL),
            g_ng, g_in[None], g_qg, g_kg, g_sk, g_out[None],
            d_ng, d_in[None], d_qg, d_kg, d_sk, d_out[None],
            m_ng, nm_in[None], m_qg, m_kg, m_sk, nm_out[None],
            v_ng, nv_in[None], v_qg, v_kg, v_sk, nv_out[None])
```

```python
import functools
import math

import jax
import jax.numpy as jnp
from jax import lax
from jax.experimental import pallas as pl
from jax.experimental.pallas import tpu as pltpu

F32 = jnp.float32
BF16 = jnp.bfloat16

D_MODEL = 1024
HEAD_DIM = 64
BLOCK = 128
ROPE_THETA = 10000.0
EPS = 1e-6
QA, KA, VA, GA, QB, KB, VB, GB = 0, 512, 640, 768, 1280, 1792, 2304, 2816
IN_WIDTH = 3328
SHARD_IN = IN_WIDTH // 4
SHARD_OUT = D_MODEL // 4
SCALE = 1.0 / math.sqrt(HEAD_DIM)
NEG = -1e30

ADAM_LR, ADAM_B1, ADAM_B2, ADAM_EPS, ADAM_WD, ADAM_STEP = 0.001, 0.9, 0.999, 1e-08, 0.01, 10

TM = 512
VMEM_LIMIT = 56 * 1024 * 1024
MESH = pl.DeviceIdType.MESH


def _dot(a, b):
    return jnp.dot(a, b, preferred_element_type=F32)


def _dot_nt(a, b):
    return lax.dot_general(a, b, (((1,), (1,)), ((), ())), preferred_element_type=F32)


def _dot_tn(a, b):
    return lax.dot_general(a, b, (((0,), (0,)), ((), ())), preferred_element_type=F32)


def _dot_hl(a, m):
    hi = a.astype(BF16)
    lo = (a - hi.astype(F32)).astype(BF16)
    return _dot(hi, m) + _dot(lo, m)


def _params(n_axes=None, vmem=VMEM_LIMIT):
    sem = None if n_axes is None else ("arbitrary",) * n_axes
    return pltpu.CompilerParams(dimension_semantics=sem, vmem_limit_bytes=vmem)


def _const_spec(shape):
    nd = len(shape)
    return pl.BlockSpec(shape, lambda *_: (0,) * nd)


def _rope_fwd(x, cos, sa, sb):
    return x * cos + pltpu.roll(x, 96, 1) * sa + pltpu.roll(x, 32, 1) * sb


def _rope_bwd(d, cos, sa, sb):
    return d * cos - pltpu.roll(d, 96, 1) * sa - pltpu.roll(d, 32, 1) * sb


def _lane_mask(rows, dtype=F32):
    lane = lax.broadcasted_iota(jnp.int32, (rows, 128), 1)
    return jnp.where(lane < HEAD_DIM, 1.0, 0.0).astype(dtype)


def _norm_proj(x2, ng, w_bf, qg512, kg128, g512, cos, sa, sb):
    T = x2.shape[0]

    def body(x_ref, ng_ref, w_ref, qg_ref, kg_ref, g_ref, cos_ref, sa_ref, sb_ref,
             h_ref, qraw_ref, kraw_ref, qrot_ref, k2_ref, v2_ref, ga_ref, qb_ref, kb_ref, vb_ref, gb_ref):
        xb = x_ref[...]
        r = lax.rsqrt(jnp.mean(xb * xb, axis=-1, keepdims=True) + EPS)
        h = (xb * r * ng_ref[...]).astype(BF16)
        h_ref[...] = h
        cosv, sav, sbv = cos_ref[...], sa_ref[...], sb_ref[...]
        m0 = _lane_mask(TM)

        def dup(v):
            v0 = v * m0
            v1 = v - v0
            return v0 + pltpu.roll(v0, 64, 1), v1 + pltpu.roll(v1, 64, 1)

        qa = _dot_nt(h, w_ref[QA:KA, :])
        qraw_ref[...] = qa
        qn = qa * lax.rsqrt(_dot_hl(qa * qa, g_ref[...]) + EPS) * qg_ref[...]
        for s in range(4):
            qs = _rope_fwd(qn[:, s * 128:(s + 1) * 128], cosv, sav, sbv)
            qrot_ref[:, s * 128:(s + 1) * 128] = (qs * SCALE).astype(BF16)
        ka = _dot_nt(h, w_ref[KA:VA, :])
        kraw_ref[...] = ka
        kn = ka * lax.rsqrt(_dot_hl(ka * ka, g_ref[0:128, 0:128]) + EPS) * kg_ref[...]
        k0, k1 = dup(_rope_fwd(kn, cosv, sav, sbv))
        k2_ref[:, 0:128] = k0.astype(BF16)
        k2_ref[:, 128:256] = k1.astype(BF16)
        v0, v1 = dup(_dot_nt(h, w_ref[VA:GA, :]))
        v2_ref[:, 0:128] = v0.astype(BF16)
        v2_ref[:, 128:256] = v1.astype(BF16)
        ga_ref[...] = _dot_nt(h, w_ref[GA:QB, :])
        qb_ref[...] = (_dot_nt(h, w_ref[QB:KB, :]) * SCALE).astype(BF16)
        kb_ref[...] = _dot_nt(h, w_ref[KB:VB, :]).astype(BF16)
        vb_ref[...] = _dot_nt(h, w_ref[VB:GB, :]).astype(BF16)
        gb_ref[...] = _dot_nt(h, w_ref[GB:IN_WIDTH, :])

    def rows(w):
        return pl.BlockSpec((TM, w), lambda i: (i, 0))

    outs = [(D_MODEL, BF16), (512, F32), (128, F32), (512, BF16), (256, BF16), (256, BF16), (512, F32),
            (512, BF16), (512, BF16), (512, BF16), (512, F32)]
    return pl.pallas_call(
        body, name="norm_proj", grid=(T // TM,),
        in_specs=[rows(D_MODEL), _const_spec((1, D_MODEL)), _const_spec((IN_WIDTH, D_MODEL)), _const_spec((1, 512)),
                  _const_spec((1, 128)), _const_spec((512, 512)), rows(128), rows(128), rows(128)],
        out_specs=[rows(w) for w, _ in outs],
        out_shape=[jax.ShapeDtypeStruct((T, w), dt) for w, dt in outs],
        compiler_params=_params(1),
    )(x2, ng, w_bf, qg512, kg128, g512, cos, sa, sb)


SWA_Q = 512
SWA_SUB = SWA_Q // BLOCK


def _swa_scores(q, kst, mask, sink_ref, kv):
    s_all = _dot_nt(q, kst)
    first = lax.broadcasted_iota(jnp.int32, (256, 1), 0) < 128
    probs, stats = [], []
    for hh in range(2):
        sink = jnp.where(first, sink_ref[kv * 4 + hh], sink_ref[kv * 4 + 2 + hh])
        s = jnp.where(mask, s_all[:, hh * 256:(hh + 1) * 256], NEG)
        m = jnp.maximum(jnp.max(s, axis=1, keepdims=True), sink)
        pe = jnp.exp(s - m)
        inv = 1.0 / (jnp.sum(pe, axis=1, keepdims=True) + jnp.exp(sink - m))
        probs.append(pe * inv)
        stats.append(jnp.exp(sink - m) * inv)
    return probs, stats


def _swa_mask(has_prev):
    r = lax.broadcasted_iota(jnp.int32, (256, 256), 0) & 127
    c = lax.broadcasted_iota(jnp.int32, (256, 256), 1)
    band = (c > r) & (c <= r + 128)
    return band if has_prev is True else band & ((c >= 128) | has_prev)


def _stack_pairs(ref, rows, kv):
    return jnp.concatenate([ref[rows, (2 * kv) * 128:(2 * kv + 1) * 128], ref[rows, (2 * kv + 1) * 128:(2 * kv + 2) * 128]], axis=0)


def _swa_keys(prev_ref, main_ref, s, kv, m0b):
    cols = slice(kv * 128, (kv + 1) * 128)
    prev = prev_ref[:, cols] if s == 0 else main_ref[(s - 1) * 128:s * 128, cols]
    kc = jnp.concatenate([prev, main_ref[s * 128:(s + 1) * 128, cols]], axis=0)
    k0 = kc * m0b
    return jnp.concatenate([k0, kc - k0], axis=0)


def _swa_fwd(sinks, qrot, k2, v2, nbatch, seq):
    ni = seq // SWA_Q
    T = nbatch * seq

    def body(sink_ref, q_ref, kp_ref, km_ref, vp_ref, vm_ref, o_ref):
        i = pl.program_id(1)
        m0b = _lane_mask(256, BF16)
        for s in range(SWA_SUB):
            mask = _swa_mask(True if s else i > 0)
            rows = slice(s * 128, (s + 1) * 128)
            for kv in range(2):
                kst = _swa_keys(kp_ref, km_ref, s, kv, m0b)
                vst = _swa_keys(vp_ref, vm_ref, s, kv, m0b)
                probs, _ = _swa_scores(_stack_pairs(q_ref, rows, kv), kst, mask, sink_ref, kv)
                o2 = _dot(jnp.concatenate(probs, axis=1).astype(BF16), vst)
                o_ref[rows, kv * 256:kv * 256 + 128] = o2[0:128]
                o_ref[rows, kv * 256 + 128:(kv + 1) * 256] = o2[128:256]

    main = lambda b, i: (b * ni + i, 0)
    prev = lambda b, i: ((b * ni + i) * SWA_SUB - jnp.where(i > 0, 1, 0), 0)
    return pl.pallas_call(
        body, name="swa_fwd", grid=(nbatch, ni),
        in_specs=[pl.BlockSpec(memory_space=pltpu.SMEM), pl.BlockSpec((SWA_Q, 512), main),
                  pl.BlockSpec((128, 256), prev), pl.BlockSpec((SWA_Q, 256), main),
                  pl.BlockSpec((128, 256), prev), pl.BlockSpec((SWA_Q, 256), main)],
        out_specs=pl.BlockSpec((SWA_Q, 512), main),
        out_shape=jax.ShapeDtypeStruct((T, 512), F32),
        compiler_params=_params(2),
    )(sinks, qrot, k2, k2, v2, v2)


def _swa_bwd(sinks, qrot, k2, v2, doa, nbatch, seq):
    ni = seq // SWA_Q
    T = nbatch * seq

    def body(sink_ref, q_ref, kp_ref, km_ref, vp_ref, vm_ref, do_ref, dq_ref, dk_ref, dv_ref, ds_ref, dkc, dvc):
        b, i = pl.program_id(0), pl.program_id(1)

        @pl.when((b == 0) & (i == 0))
        def _():
            ds_ref[...] = jnp.zeros_like(ds_ref)

        @pl.when(i == 0)
        def _():
            dkc[...] = jnp.zeros_like(dkc)
            dvc[...] = jnp.zeros_like(dvc)

        @pl.when(i < ni)
        def _():
            m0b = _lane_mask(256, BF16)
            m0 = _lane_mask(128) > 0.5
            for kv in range(2):
                kcols = slice(kv * 128, (kv + 1) * 128)
                dk_own, dv_own = dkc[:, kcols], dvc[:, kcols]
                for s in range(SWA_SUB):
                    mask = _swa_mask(True if s else i > 0)
                    rows = slice(s * 128, (s + 1) * 128)
                    kst = _swa_keys(kp_ref, km_ref, s, kv, m0b)
                    vst = _swa_keys(vp_ref, vm_ref, s, kv, m0b)
                    q, do = _stack_pairs(q_ref, rows, kv), _stack_pairs(do_ref, rows, kv)
                    probs, psink = _swa_scores(q, kst, mask, sink_ref, kv)
                    dp_all = _dot_nt(do, vst)
                    ds_parts = []
                    for hh in range(2):
                        dp = dp_all[:, hh * 256:(hh + 1) * 256]
                        delta = jnp.sum(probs[hh] * dp, axis=1, keepdims=True)
                        ds_parts.append(probs[hh] * (dp - delta))
                        dsink = psink[hh] * delta
                        for pr in range(2):
                            h = kv * 4 + pr * 2 + hh
                            ds_ref[h:h + 1, :] = ds_ref[h:h + 1, :] - jnp.sum(dsink[pr * 128:(pr + 1) * 128])
                    ds_all = jnp.concatenate(ds_parts, axis=1).astype(BF16)
                    p_all = jnp.concatenate(probs, axis=1).astype(BF16)
                    dq2 = _dot(ds_all, kst) * SCALE
                    dq_ref[rows, kv * 256:kv * 256 + 128] = dq2[0:128]
                    dq_ref[rows, kv * 256 + 128:(kv + 1) * 256] = dq2[128:256]
                    dkst = _dot_tn(ds_all, q)
                    dvst = _dot_tn(p_all, do)
                    dk_ref[rows, kcols] = dk_own + jnp.where(m0, dkst[0:128], dkst[256:384])
                    dv_ref[rows, kcols] = dv_own + jnp.where(m0, dvst[0:128], dvst[256:384])
                    dk_own = jnp.where(m0, dkst[128:256], dkst[384:512])
                    dv_own = jnp.where(m0, dvst[128:256], dvst[384:512])
                dkc[:, kcols] = dk_own
                dvc[:, kcols] = dv_own

        @pl.when(i == ni)
        def _():
            dk_ref[...] = jnp.zeros_like(dk_ref)
            dv_ref[...] = jnp.zeros_like(dv_ref)
            dk_ref[0:128, :] = dkc[...]
            dv_ref[0:128, :] = dvc[...]

    main = lambda b, i: (b * ni + jnp.minimum(i, ni - 1), 0)
    prev = lambda b, i: ((b * ni + jnp.minimum(i, ni - 1)) * SWA_SUB - jnp.where(jnp.minimum(i, ni - 1) > 0, 1, 0), 0)
    shifted = lambda b, i: (b * (ni + 1) + i, 0)
    tpad = nbatch * (ni + 1) * SWA_Q
    return pl.pallas_call(
        body, name="swa_bwd", grid=(nbatch, ni + 1),
        in_specs=[pl.BlockSpec(memory_space=pltpu.SMEM), pl.BlockSpec((SWA_Q, 512), main),
                  pl.BlockSpec((128, 256), prev), pl.BlockSpec((SWA_Q, 256), main),
                  pl.BlockSpec((128, 256), prev), pl.BlockSpec((SWA_Q, 256), main),
                  pl.BlockSpec((SWA_Q, 512), main)],
        out_specs=[pl.BlockSpec((SWA_Q, 512), main), pl.BlockSpec((SWA_Q, 256), shifted),
                   pl.BlockSpec((SWA_Q, 256), shifted), _const_spec((8, 128))],
        out_shape=[jax.ShapeDtypeStruct((T, 512), F32), jax.ShapeDtypeStruct((tpad, 256), F32),
                   jax.ShapeDtypeStruct((tpad, 256), F32), jax.ShapeDtypeStruct((8, 128), F32)],
        scratch_shapes=[pltpu.VMEM((128, 256), F32), pltpu.VMEM((128, 256), F32)],
        compiler_params=_params(2),
    )(sinks, qrot, k2, k2, v2, v2, doa)


def _unshift(dkpad, nbatch, seq):
    return dkpad.reshape(nbatch, seq + SWA_Q, 256)[:, BLOCK:BLOCK + seq].reshape(nbatch * seq, 256)


SB_T = 256
SB_TQ = 2 * SB_T


def _sb_mask(kind):
    if kind == "full":
        return None
    rows = SB_T if kind == "B" else SB_TQ
    r = lax.broadcasted_iota(jnp.int32, (rows, 2 * SB_T), 0)
    c = lax.broadcasted_iota(jnp.int32, (rows, 2 * SB_T), 1)
    causal = (c & (SB_T - 1)) < r
    return causal | (r >= SB_T) if kind == "A" else causal


def _sb_rows(kind):
    return slice(SB_T, SB_TQ) if kind == "B" else slice(0, SB_TQ)


def _lower(x):
    return jnp.concatenate([jnp.zeros_like(x), x], axis=0)


def _sb_logits(neg_q, kst):
    nz = _dot_nt(neg_q, kst)
    sign = jnp.uint32(0x80000000)
    neg_abs = lax.bitcast_convert_type(lax.bitcast_convert_type(nz, jnp.uint32) | sign, F32)
    return nz, jnp.minimum(nz, 0.0) - jnp.log(1.0 + jnp.exp(neg_abs))


SB_NP = 4


def _pair_rows(ref, j, cols, m0b):
    kj = ref[pl.ds(pl.multiple_of(j * SB_T, SB_T), SB_T), cols]
    k0 = kj * m0b
    return jnp.concatenate([k0, kj - k0], axis=0)


def _bcast2(c0, c1):
    rows = c0.shape[0]
    return jnp.concatenate([jnp.broadcast_to(c0, (rows, SB_T)), jnp.broadcast_to(c1, (rows, SB_T))], axis=1)


def _rowsum2(x):
    return jnp.sum(x[:, 0:SB_T], axis=1, keepdims=True), jnp.sum(x[:, SB_T:2 * SB_T], axis=1, keepdims=True)


def _scan2(x, tri2):
    outs = []
    for h in range(2):
        xh = x[:, h * SB_T:(h + 1) * SB_T]
        hi = xh.astype(BF16)
        lo = (xh - hi.astype(F32)).astype(BF16)
        outs.append(_dot(jnp.concatenate([hi, lo], axis=1), tri2))
    return jnp.concatenate(outs, axis=1)


def _scan1(x, tri):
    xb = x.astype(BF16)
    return jnp.concatenate([_dot(xb[:, h * SB_T:(h + 1) * SB_T], tri) for h in range(2)], axis=1)


def _sb_fwd(qb, kb, vb, ublk, nbatch, seq):
    nq, nk = seq // SB_TQ, seq // SB_T
    T = nbatch * seq

    def body(q_ref, k_ref, v_ref, u_ref, o_ref, wst_ref, ost_ref, wbuf, obuf, sems):
        b, i = pl.program_id(0), pl.program_id(2)
        m0b = _lane_mask(SB_T, BF16)
        u = u_ref[...]
        pairs = [slice(pp * 128, (pp + 1) * 128) for pp in range(SB_NP)]
        neg_qs = [-q_ref[:, cols] for cols in pairs]

        def stores(j, slot):
            tix = (b * nq + i) * nk + j
            return (pltpu.make_async_copy(wbuf.at[slot], wst_ref.at[tix], sems.at[0, slot]),
                    pltpu.make_async_copy(obuf.at[slot], ost_ref.at[tix], sems.at[1, slot]))

        def tile(j, carries, kind, slot, wait):
            rows, mask = _sb_rows(kind), _sb_mask(kind)
            if wait:
                for cp in stores(j, slot):
                    cp.wait()
            out = []
            for pp, (cols, neg_q, (c0, c1, acc)) in enumerate(zip(pairs, neg_qs, carries)):
                kst = _pair_rows(k_ref, j, cols, m0b)
                vst = _pair_rows(v_ref, j, cols, m0b)
                nz, lb = _sb_logits(neg_q[rows], kst)
                if mask is not None:
                    lb = jnp.where(mask, lb, 0.0)
                incl = _scan2(lb, u)
                w = jnp.exp(incl - nz if kind == "B" else incl + _bcast2(c0, c1) - nz)
                if mask is not None:
                    w = jnp.where(mask, w, 0.0)
                wb = w.astype(BF16)
                wbuf[slot, pp, rows, :] = wb
                obuf[slot, pp, rows, :] = jnp.exp(lb).astype(BF16)
                if kind == "B":
                    wbuf[slot, pp, 0:SB_T, :] = jnp.zeros((SB_T, 2 * SB_T), BF16)
                    obuf[slot, pp, 0:SB_T, :] = jnp.zeros((SB_T, 2 * SB_T), BF16)
                d0, d1, da = incl[:, 0:1], incl[:, SB_T:SB_T + 1], _dot(wb, vst)
                if kind == "B":
                    d0, d1, da = _lower(d0), _lower(d1), _lower(da)
                out.append((c0 + d0, c1 + d1, acc + da))
            for cp in stores(j, slot):
                cp.start()
            return tuple(out)

        zc = jnp.zeros((SB_TQ, 1), F32)
        carries = tile(2 * i + 1, ((zc, zc, jnp.zeros((SB_TQ, 128), F32)),) * SB_NP, "B", 0, False)
        carries = tile(2 * i, carries, "A", 1, False)

        def two(jj, cr):
            cr = tile(2 * i - 1 - 2 * jj, cr, "full", 0, True)
            return tile(2 * i - 2 - 2 * jj, cr, "full", 1, True)

        carries = lax.fori_loop(0, i, two, carries)
        for cols, carry in zip(pairs, carries):
            o_ref[:, cols] = carry[2]
        for slot in range(2):
            for cp in stores(0, slot):
                cp.wait()

    wide = 128 * SB_NP
    blk = lambda b, g, i: (b * nq + i, g)
    full = lambda b, g, i: (b, g)
    hbm = pl.BlockSpec(memory_space=pl.ANY)
    tiles = jax.ShapeDtypeStruct((nbatch * nq * nk, SB_NP, SB_TQ, 2 * SB_T), BF16)
    return pl.pallas_call(
        body, name="sb_fwd", grid=(nbatch, 4 // SB_NP, nq),
        in_specs=[pl.BlockSpec((SB_TQ, wide), blk), pl.BlockSpec((seq, wide), full), pl.BlockSpec((seq, wide), full),
                  _const_spec((2 * SB_T, SB_T))],
        out_specs=[pl.BlockSpec((SB_TQ, wide), blk), hbm, hbm],
        out_shape=[jax.ShapeDtypeStruct((T, 512), F32), tiles, tiles],
        scratch_shapes=[pltpu.VMEM((2, SB_NP, SB_TQ, 2 * SB_T), BF16), pltpu.VMEM((2, SB_NP, SB_TQ, 2 * SB_T), BF16),
                        pltpu.SemaphoreType.DMA((2, 2))],
        compiler_params=_params(3),
    )(qb, kb, vb, ublk)


def _sb_bwd(qb, kb, vb, dob, wst, ost, pblk, nbatch, seq):
    nq, nk = seq // SB_TQ, seq // SB_T
    T = nbatch * seq

    def body(q_ref, k_ref, v_ref, do_ref, wst_ref, ost_ref, up_ref, dq_ref, dk_ref, dv_ref, wbuf, obuf, sems):
        b, i = pl.program_id(0), pl.program_id(2)

        @pl.when(i == 0)
        def _():
            dk_ref[...] = jnp.zeros_like(dk_ref)
            dv_ref[...] = jnp.zeros_like(dv_ref)

        m0b = _lane_mask(SB_T, BF16)
        m0 = _lane_mask(SB_T) > 0.5
        up = up_ref[...]
        pairs = [slice(pp * 128, (pp + 1) * 128) for pp in range(SB_NP)]

        def loads(j, slot):
            tix = (b * nq + i) * nk + j
            return (pltpu.make_async_copy(wst_ref.at[tix], wbuf.at[slot], sems.at[0, slot]),
                    pltpu.make_async_copy(ost_ref.at[tix], obuf.at[slot], sems.at[1, slot]))

        def tile(j, carries, kind, slot, fetch_next):
            rows, mask = _sb_rows(kind), _sb_mask(kind)
            if fetch_next:
                for cp in loads(j + 1, 1 - slot):
                    cp.start()
            for cp in loads(j, slot):
                cp.wait()
            out = []
            for pp, (cols, (s0, s1, dq)) in enumerate(zip(pairs, carries)):
                q, do = q_ref[rows, cols], do_ref[rows, cols]
                kst = _pair_rows(k_ref, j, cols, m0b)
                vst = _pair_rows(v_ref, j, cols, m0b)
                wb = wbuf[slot, pp, rows, :]
                e = _dot_nt(do, vst) * wb.astype(F32)
                dlb = _bcast2(s0[rows], s1[rows]) + _scan1(e, up)
                dz = (e + dlb) * obuf[slot, pp, rows, :].astype(F32) - dlb
                if mask is not None:
                    dz = jnp.where(mask, dz, 0.0)
                dzb = dz.astype(BF16)
                dkst = _dot_tn(dzb, q)
                dvst = _dot_tn(wb, do)
                keys = pl.ds(pl.multiple_of(j * SB_T, SB_T), SB_T)
                dk_ref[keys, cols] = dk_ref[keys, cols] + jnp.where(m0, dkst[0:SB_T], dkst[SB_T:2 * SB_T])
                dv_ref[keys, cols] = dv_ref[keys, cols] + jnp.where(m0, dvst[0:SB_T], dvst[SB_T:2 * SB_T])
                x0, x1 = _rowsum2(e)
                ddq = _dot(dzb, kst)
                if kind == "B":
                    x0, x1, ddq = _lower(x0), _lower(x1), _lower(ddq)
                out.append((s0 + x0, s1 + x1, dq + ddq))
            return tuple(out)

        for cp in loads(0, 0):
            cp.start()

        def two(jj, cr):
            cr = tile(2 * jj, cr, "full", 0, True)
            return tile(2 * jj + 1, cr, "full", 1, True)

        zc = jnp.zeros((SB_TQ, 1), F32)
        carries = lax.fori_loop(0, i, two, ((zc, zc, jnp.zeros((SB_TQ, 128), F32)),) * SB_NP)
        carries = tile(2 * i, carries, "A", 0, True)
        carries = tile(2 * i + 1, carries, "B", 1, False)
        for cols, carry in zip(pairs, carries):
            dq_ref[:, cols] = carry[2] * SCALE

    wide = 128 * SB_NP
    blk = lambda b, g, i: (b * nq + i, g)
    full = lambda b, g, i: (b, g)
    hbm = pl.BlockSpec(memory_space=pl.ANY)
    return pl.pallas_call(
        body, name="sb_bwd", grid=(nbatch, 4 // SB_NP, nq),
        in_specs=[pl.BlockSpec((SB_TQ, wide), blk), pl.BlockSpec((seq, wide), full), pl.BlockSpec((seq, wide), full),
                  pl.BlockSpec((SB_TQ, wide), blk), hbm, hbm, _const_spec((SB_T, SB_T))],
        out_specs=[pl.BlockSpec((SB_TQ, wide), blk), pl.BlockSpec((seq, wide), full), pl.BlockSpec((seq, wide), full)],
        out_shape=[jax.ShapeDtypeStruct((T, 512), F32)] * 3,
        scratch_shapes=[pltpu.VMEM((2, SB_NP, SB_TQ, 2 * SB_T), BF16), pltpu.VMEM((2, SB_NP, SB_TQ, 2 * SB_T), BF16),
                        pltpu.SemaphoreType.DMA((2, 2))],
        compiler_params=_params(3),
    )(qb, kb, vb, dob, wst, ost, pblk)


def _sigmoid(g):
    return 1.0 / (1.0 + jnp.exp(-g))


def _out_proj(oa, ob, ga, gb, x2, tgt, wout_bf):
    T = x2.shape[0]

    def body(oa_ref, ob_ref, ga_ref, gb_ref, x_ref, t_ref, w_ref,
             dout_ref, doa_ref, dob_ref, dga_ref, dgb_ref, dw_ref, loss_ref):
        @pl.when(pl.program_id(0) == 0)
        def _():
            loss_ref[...] = jnp.zeros_like(loss_ref)
            dw_ref[...] = jnp.zeros_like(dw_ref)

        halves = ((oa_ref, ga_ref, doa_ref, dga_ref, 0), (ob_ref, gb_ref, dob_ref, dgb_ref, 512))
        out = x_ref[...]
        gated = []
        for o_ref, g_ref, _, _, lo in halves:
            g = g_ref[...]
            sg = _sigmoid(g)
            y = (o_ref[...] * (g * sg)).astype(BF16)
            out = out + _dot(y, w_ref[lo:lo + 512, :])
            gated.append((y, g, sg))
        diff = out - t_ref[...]
        dout = diff * (1.0 / D_MODEL)
        dout_ref[...] = dout
        loss_ref[...] = loss_ref[...] + jnp.sum(diff * diff) * (0.5 / D_MODEL)
        db = dout.astype(BF16)
        for (o_ref, _, do_ref, dg_ref, lo), (y, g, sg) in zip(halves, gated):
            dw_ref[lo:lo + 512, :] = dw_ref[lo:lo + 512, :] + _dot_tn(y, db)
            dy = _dot_nt(db, w_ref[lo:lo + 512, :])
            do_ref[...] = (dy * (g * sg)).astype(BF16)
            dg_ref[...] = (dy * o_ref[...] * (sg * (1.0 + g * (1.0 - sg)))).astype(BF16)

    rows = lambda w: pl.BlockSpec((TM, w), lambda i: (i, 0))
    return pl.pallas_call(
        body, name="out_proj", grid=(T // TM,),
        in_specs=[rows(512), rows(512), rows(512), rows(512), rows(D_MODEL), rows(D_MODEL),
                  _const_spec((D_MODEL, D_MODEL))],
        out_specs=[rows(D_MODEL)] + [rows(512)] * 4 + [_const_spec((D_MODEL, D_MODEL)), _const_spec((8, 128))],
        out_shape=[jax.ShapeDtypeStruct((T, D_MODEL), F32)] + [jax.ShapeDtypeStruct((T, 512), BF16)] * 4
        + [jax.ShapeDtypeStruct((D_MODEL, D_MODEL), F32), jax.ShapeDtypeStruct((8, 128), F32)],
        compiler_params=_params(1),
    )(oa, ob, ga, gb, x2, tgt, wout_bf)


def _qk_grad(qraw, kraw, dqrot, dk2, dv2, qg512, kg128, g512, cos, sa, sb):
    T = qraw.shape[0]

    def body(qraw_ref, kraw_ref, dqrot_ref, dk2_ref, dv2_ref, qg_ref, kg_ref, g_ref, cos_ref, sa_ref, sb_ref,
             dqa_ref, dkv_ref, dqg_ref, dkg_ref):
        @pl.when(pl.program_id(0) == 0)
        def _():
            dqg_ref[...] = jnp.zeros_like(dqg_ref)
            dkg_ref[...] = jnp.zeros_like(dkg_ref)

        cosv, sav, sbv = cos_ref[...], sa_ref[...], sb_ref[...]
        m0 = _lane_mask(TM) > 0.5

        def head_norm_bwd(raw, dn_rot, gmat, gain):
            r = lax.rsqrt(_dot_hl(raw * raw, gmat) + EPS)
            n = raw * r
            dn = dn_rot * gain
            return r * (dn - n * _dot_hl(dn * n, gmat)), jnp.sum(dn_rot * n, axis=0, keepdims=True)

        def fold(ref):
            a, b = ref[:, 0:128], ref[:, 128:256]
            return jnp.where(m0, a + pltpu.roll(a, 64, 1), b + pltpu.roll(b, 64, 1))

        dqn = jnp.concatenate([_rope_bwd(dqrot_ref[:, s * 128:(s + 1) * 128], cosv, sav, sbv) for s in range(4)], axis=1)
        dqa, dqg = head_norm_bwd(qraw_ref[...], dqn, g_ref[...], qg_ref[...])
        dka, dkg = head_norm_bwd(kraw_ref[...], _rope_bwd(fold(dk2_ref), cosv, sav, sbv), g_ref[0:128, 0:128], kg_ref[...])
        dqa_ref[...] = dqa.astype(BF16)
        dkv_ref[:, 0:128] = dka.astype(BF16)
        dkv_ref[:, 128:256] = fold(dv2_ref).astype(BF16)
        dqg_ref[...] = dqg_ref[...] + dqg
        dkg_ref[...] = dkg_ref[...] + dkg

    rows = lambda w: pl.BlockSpec((TM, w), lambda i: (i, 0))
    return pl.pallas_call(
        body, name="qk_grad", grid=(T // TM,),
        in_specs=[rows(512), rows(128), rows(512), rows(256), rows(256), _const_spec((1, 512)), _const_spec((1, 128)),
                  _const_spec((512, 512)), rows(128), rows(128), rows(128)],
        out_specs=[rows(512), rows(256), _const_spec((1, 512)), _const_spec((1, 128))],
        out_shape=[jax.ShapeDtypeStruct((T, 512), BF16), jax.ShapeDtypeStruct((T, 256), BF16),
                   jax.ShapeDtypeStruct((1, 512), F32), jax.ShapeDtypeStruct((1, 128), F32)],
        compiler_params=_params(1),
    )(qraw, kraw, dqrot, dk2, dv2, qg512, kg128, g512, cos, sa, sb)


_PIECES = ((QA, 512), (KA, 256), (GA, 512), (QB, 512), (KB, 512), (VB, 512), (GB, 512))


def _w_in_grad(h, pieces):
    T = h.shape[0]

    def body(h_ref, *refs):
        dw_ref = refs[-1]

        @pl.when(pl.program_id(0) == 0)
        def _():
            dw_ref[...] = jnp.zeros_like(dw_ref)

        hb = h_ref[...]
        for (lo, width), p_ref in zip(_PIECES, refs[:-1]):
            dw_ref[lo:lo + width, :] = dw_ref[lo:lo + width, :] + _dot_tn(p_ref[...].astype(BF16), hb)

    rows = lambda w: pl.BlockSpec((TM, w), lambda i: (i, 0))
    return pl.pallas_call(
        body, name="w_in_grad", grid=(T // TM,),
        in_specs=[rows(D_MODEL)] + [rows(w) for _, w in _PIECES],
        out_specs=_const_spec((IN_WIDTH, D_MODEL)),
        out_shape=jax.ShapeDtypeStruct((IN_WIDTH, D_MODEL), F32),
        compiler_params=_params(1),
    )(h, *pieces)


def _x_grad(x2, dout, pieces, w_bf, ng, cin4, cout4):
    T = x2.shape[0]
    npc = len(_PIECES)
    steps = T // TM

    def body(x_ref, dout_ref, *refs):
        w_ref, ng_ref, cin_ref, cout_ref, gx_ref, dng_ref, rin_ref, rout_ref, ssem, rsem = refs[npc:]
        step = pl.program_id(0)
        _, _, c, chips = _place()
        sems = (ssem, rsem)

        def copies():
            out = []
            for j, (cx, cy) in enumerate(chips):
                k = 2 * cx + cy
                out.append(_remote(cin_ref.at[k], rin_ref.at[j], sems, 2 * j, (cx, cy, c)))
                out.append(_remote(cout_ref.at[k], rout_ref.at[j], sems, 2 * j + 1, (cx, cy, c)))
            return out

        @pl.when(step == 0)
        def _():
            dng_ref[...] = jnp.zeros_like(dng_ref)
            for cp in copies():
                cp.start()

        dh = jnp.zeros((TM, D_MODEL), F32)
        for (lo, width), p_ref in zip(_PIECES, refs[:npc]):
            dh = dh + _dot(p_ref[...].astype(BF16), w_ref[lo:lo + width, :])
        xb = x_ref[...]
        r = lax.rsqrt(jnp.mean(xb * xb, axis=-1, keepdims=True) + EPS)
        n = xb * r
        dn = dh * ng_ref[...]
        gx_ref[...] = dout_ref[...] + r * (dn - n * jnp.mean(dn * n, axis=-1, keepdims=True))
        dng_ref[...] = dng_ref[...] + jnp.sum(dh * n, axis=0, keepdims=True)

        @pl.when(step == steps - 1)
        def _():
            for cp in copies():
                cp.wait_recv()
            for cp in copies():
                cp.wait_send()

    rows = lambda w: pl.BlockSpec((TM, w), lambda i: (i, 0))
    hbm = pl.BlockSpec(memory_space=pl.ANY)
    return pl.pallas_call(
        body, name="x_grad", grid=(steps,),
        in_specs=[rows(D_MODEL), rows(D_MODEL)] + [rows(w) for _, w in _PIECES]
        + [_const_spec((IN_WIDTH, D_MODEL)), _const_spec((1, D_MODEL)), hbm, hbm],
        out_specs=[rows(D_MODEL), _const_spec((1, D_MODEL)), hbm, hbm],
        out_shape=[jax.ShapeDtypeStruct((T, D_MODEL), F32), jax.ShapeDtypeStruct((1, D_MODEL), F32),
                   jax.ShapeDtypeStruct((3,) + cin4.shape[1:], cin4.dtype),
                   jax.ShapeDtypeStruct((3,) + cout4.shape[1:], cout4.dtype)],
        scratch_shapes=[pltpu.SemaphoreType.DMA((6,)), pltpu.SemaphoreType.DMA((6,))],
        compiler_params=_params(1),
    )(x2, dout, *pieces, w_bf, ng, cin4, cout4)


HBM = pl.BlockSpec(memory_space=pl.ANY)


def _place():
    x, y, c = lax.axis_index("x"), lax.axis_index("y"), lax.axis_index("c")
    chips = [(1 - x, y), (x, 1 - y), (1 - x, 1 - y)]
    return x, y, c, chips


def _remote(src, dst, sems, k, to):
    return pltpu.make_async_remote_copy(src_ref=src, dst_ref=dst, send_sem=sems[0].at[k], recv_sem=sems[1].at[k],
                                        device_id=to, device_id_type=MESH)


def _gather_weights(shards):
    n = len(shards)

    def body(*refs):
        srcs, dsts, (ssem, rsem) = refs[:n], refs[n:2 * n], refs[2 * n:]
        x, y, c, chips = _place()
        me, sib = 2 * x + y, (x, y, 1 - c)
        sems = (ssem, rsem)
        sent = []
        for j, (cx, cy) in enumerate(chips):
            for a in range(n):
                sent.append(_remote(srcs[a].at[c], dsts[a].at[me, c], sems, n * j + a, (cx, cy, c)))
        for cp in sent:
            cp.start()
        for j, (cx, cy) in enumerate(chips):
            k = 2 * cx + cy
            for a in range(n):
                _remote(srcs[a].at[c], dsts[a].at[k, c], sems, n * j + a, sib).wait_recv()
                fwd = _remote(dsts[a].at[k, c], dsts[a].at[k, c], sems, 3 * n + n * j + a, sib)
                fwd.start()
                sent.append(fwd)
        for j, (cx, cy) in enumerate(chips):
            k = 2 * cx + cy
            for a in range(n):
                _remote(srcs[a].at[c], dsts[a].at[k, 1 - c], sems, 3 * n + n * j + a, sib).wait_recv()
        for cp in sent:
            cp.wait_send()

    gathered = pl.pallas_call(
        body, name="gather_weights", in_specs=[HBM] * n, out_specs=[HBM] * n,
        out_shape=[jax.ShapeDtypeStruct((4,) + s.shape, s.dtype) for s in shards],
        scratch_shapes=[pltpu.SemaphoreType.DMA((6 * n,)), pltpu.SemaphoreType.DMA((6 * n,))],
    )(*shards)
    me = 2 * lax.axis_index("x") + lax.axis_index("y")
    return [lax.dynamic_update_slice(g, s[None], (me, 0, 0, 0)) for g, s in zip(gathered, shards)]


def _pair_exchange(grads):
    n = len(grads)

    def body(*refs):
        srcs, dsts, (ssem, rsem) = refs[:n], refs[n:2 * n], refs[2 * n:]
        x, y, c, _ = _place()
        sems = (ssem, rsem)
        sib = (x, y, 1 - c)
        sent = [_remote(srcs[a].at[:, pl.ds(1 - c, 1)], dsts[a], sems, a, sib) for a in range(n)]
        for cp in sent:
            cp.start()
        for cp in sent:
            cp.wait()

    return pl.pallas_call(
        body, name="pair_exchange", in_specs=[HBM] * n, out_specs=[HBM] * n,
        out_shape=[jax.ShapeDtypeStruct((4, 1) + g.shape[2:], g.dtype) for g in grads],
        scratch_shapes=[pltpu.SemaphoreType.DMA((n,)), pltpu.SemaphoreType.DMA((n,))],
    )(*grads)


def _share_halves(hin, hout, small):
    def body(hin_ref, hout_ref, small_ref, oin_ref, oout_ref, sall_ref, ssem, rsem, lsem):
        x, y, c, _ = _place()
        sems = (ssem, rsem)
        sib = (x, y, 1 - c)
        me = 4 * x + 2 * y + c
        own = pltpu.make_async_copy(small_ref, sall_ref.at[me], lsem.at[0])
        own.start()
        sent = [_remote(hin_ref, oin_ref, sems, 0, sib), _remote(hout_ref, oout_ref, sems, 1, sib)]
        flips = [(fx, fy, fc) for fx in (0, 1) for fy in (0, 1) for fc in (0, 1)][1:]
        for k, (fx, fy, fc) in enumerate(flips):
            sent.append(_remote(small_ref, sall_ref.at[me], sems, 2 + k, (x ^ fx, y ^ fy, c ^ fc)))
        for cp in sent:
            cp.start()
        _remote(hin_ref, oin_ref, sems, 0, sib).wait_recv()
        _remote(hout_ref, oout_ref, sems, 1, sib).wait_recv()
        for k, (fx, fy, fc) in enumerate(flips):
            src = 4 * (x ^ fx) + 2 * (y ^ fy) + (c ^ fc)
            _remote(small_ref, sall_ref.at[src], sems, 2 + k, sib).wait_recv()
        for cp in sent:
            cp.wait_send()
        own.wait()

    return pl.pallas_call(
        body, name="share_halves", in_specs=[HBM, HBM, HBM], out_specs=[HBM, HBM, HBM],
        out_shape=[jax.ShapeDtypeStruct(hin.shape, F32), jax.ShapeDtypeStruct(hout.shape, F32),
                   jax.ShapeDtypeStruct((8,) + small.shape, F32)],
        scratch_shapes=[pltpu.SemaphoreType.DMA((9,)), pltpu.SemaphoreType.DMA((9,)), pltpu.SemaphoreType.DMA((1,))],
    )(hin, hout, small)


def _add_half(cidx, full4, recv4):
    _, _, rows, width = full4.shape

    def body(c_ref, a_ref, b_ref, o_ref):
        o_ref[0] = (a_ref[0, 0] + b_ref[0, 0]).astype(BF16)

    return pl.pallas_call(
        body, name=f"add_half_{rows}",
        grid_spec=pltpu.PrefetchScalarGridSpec(
            num_scalar_prefetch=1, grid=(4,),
            in_specs=[pl.BlockSpec((1, 1, rows, width), lambda k, c: (k, c[0], 0, 0)),
                      pl.BlockSpec((1, 1, rows, width), lambda k, c: (k, 0, 0, 0))],
            out_specs=pl.BlockSpec((1, rows, width), lambda k, c: (k, 0, 0))),
        out_shape=jax.ShapeDtypeStruct((4, rows, width), BF16),
        compiler_params=_params(1),
    )(cidx, full4, recv4)


def _sum_chips(chip, own4, recv3):
    _, rows, width = recv3.shape
    rb = rows // 2

    def body(k_ref, a_ref, r_ref, o_ref):
        acc = a_ref[0].astype(F32)
        for s in range(3):
            acc = acc + r_ref[s].astype(F32)
        o_ref[...] = acc

    return pl.pallas_call(
        body, name=f"sum_chips_{rows}",
        grid_spec=pltpu.PrefetchScalarGridSpec(
            num_scalar_prefetch=1, grid=(rows // rb,),
            in_specs=[pl.BlockSpec((1, rb, width), lambda i, k: (k[0], i, 0)),
                      pl.BlockSpec((3, rb, width), lambda i, k: (0, i, 0))],
            out_specs=pl.BlockSpec((rb, width), lambda i, k: (i, 0))),
        out_shape=jax.ShapeDtypeStruct((rows, width), F32),
        compiler_params=_params(1),
    )(chip, own4, recv3)


def _sum_slots(r4):
    n, rows, width = r4.shape

    def body(r_ref, o_ref):
        acc = r_ref[0]
        for s in range(1, n):
            acc = acc + r_ref[s]
        o_ref[...] = acc

    return pl.pallas_call(
        body, name=f"sum_slots_{n}_{rows}_{width}", grid=(1,),
        in_specs=[pl.BlockSpec((n, rows, width), lambda i: (0, 0, 0))],
        out_specs=pl.BlockSpec((rows, width), lambda i: (0, 0)),
        out_shape=jax.ShapeDtypeStruct((rows, width), F32),
        compiler_params=_params(1),
    )(r4)


def _adam_math(w, g, m, v):
    c1 = 1.0 - ADAM_B1 ** ADAM_STEP
    c2 = 1.0 - ADAM_B2 ** ADAM_STEP
    nm = ADAM_B1 * m + (1.0 - ADAM_B1) * g
    nv = ADAM_B2 * v + (1.0 - ADAM_B2) * (g * g)
    return -ADAM_LR * ((nm / c1) / (jnp.sqrt(nv / c2) + ADAM_EPS) + ADAM_WD * w), nm, nv


def _adamw(w, g, m, v):
    rows, width = w.shape

    def body(w_ref, g_ref, m_ref, v_ref, d_ref, nm_ref, nv_ref):
        d_ref[...], nm_ref[...], nv_ref[...] = _adam_math(w_ref[...], g_ref[...], m_ref[...], v_ref[...])

    spec = pl.BlockSpec((rows, width), lambda i: (0, 0))
    return pl.pallas_call(
        body, name=f"adamw_{rows}_{width}", grid=(1,),
        in_specs=[spec] * 4, out_specs=[spec] * 3,
        out_shape=[jax.ShapeDtypeStruct((rows, width), F32)] * 3,
        compiler_params=_params(1),
    )(w, g, m, v)


def _adamw_halves(cidx, w, own, recv, m, v):
    rows, width = w.shape
    rb = rows // 4

    def body(c_ref, w_ref, own_ref, recv_ref, m_ref, v_ref, g_ref, d_ref, nm_ref, nv_ref):
        mine = (pl.program_id(0) // 2) == c_ref[0]
        g = jnp.where(mine, own_ref[...], recv_ref[...])
        g_ref[...] = g
        d_ref[...], nm_ref[...], nv_ref[...] = _adam_math(w_ref[...], g, m_ref[...], v_ref[...])

    full = pl.BlockSpec((rb, width), lambda i, c: (i, 0))
    half = pl.BlockSpec((rb, width), lambda i, c: (i % 2, 0))
    return pl.pallas_call(
        body, name=f"adamw_halves_{rows}",
        grid_spec=pltpu.PrefetchScalarGridSpec(
            num_scalar_prefetch=1, grid=(4,),
            in_specs=[full, half, half, full, full], out_specs=[full] * 4),
        out_shape=[jax.ShapeDtypeStruct((rows, width), F32)] * 4,
        compiler_params=_params(1),
    )(cidx, w, own, recv, m, v)


def _rope_tables(positions):
    half = HEAD_DIM // 2
    inv_freq = ROPE_THETA ** (-jnp.arange(half, dtype=F32) * 2.0 / HEAD_DIM)
    ang = positions.astype(F32).reshape(-1, 1) * inv_freq
    cos, sin, zero = jnp.cos(ang), jnp.sin(ang), jnp.zeros_like(ang)
    return (jnp.concatenate([cos] * 4, axis=1), jnp.concatenate([-sin, zero] * 2, axis=1),
            jnp.concatenate([zero, sin] * 2, axis=1))


def _constants():
    idx = jnp.arange(512)
    g512 = jnp.where(idx[:, None] // HEAD_DIM == idx[None, :] // HEAD_DIM, 1.0 / HEAD_DIM, 0.0).astype(BF16)
    j = jnp.arange(SB_T)
    ublk = jnp.where(j[:, None] >= j[None, :], 1.0, 0.0).astype(BF16)
    pblk = jnp.where(j[:, None] < j[None, :], 1.0, 0.0).astype(BF16)
    return g512, jnp.concatenate([ublk, ublk], axis=0), pblk


def _pad_rows(v, width):
    return jnp.pad(v, ((0, 0), (0, width - v.shape[1])))


def _forward_backward(x2, tgt, positions, norm_gain, q_norm_gain, k_norm_gain, sinks, w_bf, wout_bf, nbatch, seq):
    cos, sa, sb = _rope_tables(positions)
    g512, ublk, pblk = _constants()
    qg512 = jnp.tile(q_norm_gain, (1, 8))
    kg128 = jnp.tile(k_norm_gain, (1, 2))
    sink1 = sinks.reshape(8)

    h, qraw, kraw, qrot, k2, v2, ga, qb, kb, vb, gb = _norm_proj(x2, norm_gain, w_bf, qg512, kg128, g512, cos, sa, sb)
    oa = _swa_fwd(sink1, qrot, k2, v2, nbatch, seq)
    ob, wst, ost = _sb_fwd(qb, kb, vb, ublk, nbatch, seq)
    dout, doa, dob, dga, dgb, dwout, loss_acc = _out_proj(oa, ob, ga, gb, x2, tgt, wout_bf)

    dqrot, dk2, dv2, dsink = _swa_bwd(sink1, qrot, k2, v2, doa, nbatch, seq)
    dqb, dkb, dvb = _sb_bwd(qb, kb, vb, dob, wst, ost, pblk, nbatch, seq)
    dk2, dv2 = _unshift(dk2, nbatch, seq), _unshift(dv2, nbatch, seq)
    dqa, dkv, dqg, dkg = _qk_grad(qraw, kraw, dqrot, dk2, dv2, qg512, kg128, g512, cos, sa, sb)
    pieces = (dqa, dkv, dga, dqb, dkb, dvb, dgb)
    dwin_t = _w_in_grad(h, pieces)
    dqg64 = dqg.reshape(8, HEAD_DIM).sum(axis=0, keepdims=True)
    dkg64 = dkg.reshape(2, HEAD_DIM).sum(axis=0, keepdims=True)
    small = jnp.concatenate([_pad_rows(dqg64, D_MODEL), _pad_rows(dkg64, D_MODEL), _pad_rows(dsink[:, 0].reshape(1, 8), D_MODEL),
                             _pad_rows(loss_acc[0:1, 0:1], D_MODEL), jnp.zeros((3, D_MODEL), F32)], axis=0)
    return dwin_t, dwout, small, (x2, dout, pieces, w_bf, norm_gain)


def _reduce_grads(dwin_t, dwout, small7, x_grad_args):
    gin4 = dwin_t.reshape(4, 2, SHARD_IN // 2, D_MODEL)
    gout4 = dwout.reshape(4, 2, SHARD_OUT // 2, D_MODEL)
    rin, rout = _pair_exchange([gin4, gout4])
    cidx = lax.axis_index("c").astype(jnp.int32).reshape(1)
    chip = (2 * lax.axis_index("x") + lax.axis_index("y")).astype(jnp.int32).reshape(1)
    cin4, cout4 = _add_half(cidx, gin4, rin), _add_half(cidx, gout4, rout)
    gx, dng, rin3, rout3 = _x_grad(*x_grad_args, cin4, cout4)
    own_in, own_out = _sum_chips(chip, cin4, rin3), _sum_chips(chip, cout4, rout3)
    sib_in, sib_out, small_all = _share_halves(own_in, own_out, jnp.concatenate([dng, small7], axis=0))
    return gx, cidx, own_in, sib_in, own_out, sib_out, _sum_slots(small_all)


def kernel(x, positions, norm_gain, w_in, q_norm_gain, k_norm_gain, sinks, w_out, loss_target, m_norm_gain, m_w_in, m_q_norm_gain, m_k_norm_gain, m_sinks, m_w_out, v_norm_gain, v_w_in, v_q_norm_gain, v_k_norm_gain, v_sinks, v_w_out):
    nbatch, seq, _ = x.shape
    T = nbatch * seq
    x2 = x.reshape(T, D_MODEL)
    tgt = loss_target.reshape(T, D_MODEL)
    tr = lambda a: jnp.swapaxes(a[0], 0, 1)
    win_t, m_win_t, v_win_t = tr(w_in), tr(m_w_in), tr(v_w_in)

    ain, aout = _gather_weights([win_t.astype(BF16).reshape(2, SHARD_IN // 2, D_MODEL),
                                 w_out[0].astype(BF16).reshape(2, SHARD_OUT // 2, D_MODEL)])
    w_bf, wout_bf = ain.reshape(IN_WIDTH, D_MODEL), aout.reshape(D_MODEL, D_MODEL)

    dwin_t, dwout, small7, x_grad_args = _forward_backward(
        x2, tgt, positions, norm_gain, q_norm_gain, k_norm_gain, sinks, w_bf, wout_bf, nbatch, seq)

    gx, cidx, own_in, sib_in, own_out, sib_out, g_small = _reduce_grads(dwin_t, dwout, small7, x_grad_args)
    loss = g_small[4, 0]

    g_in, d_in, nm_in, nv_in = [jnp.swapaxes(a, 0, 1) for a in
                                _adamw_halves(cidx, win_t, own_in, sib_in, m_win_t, v_win_t)]
    g_out, d_out, nm_out, nv_out = _adamw_halves(cidx, w_out[0], own_out, sib_out, m_w_out[0], v_w_out[0])
    pack = lambda a, b, c_, d: jnp.concatenate(
        [a, _pad_rows(b, D_MODEL), _pad_rows(c_, D_MODEL), _pad_rows(d, D_MODEL), jnp.zeros((4, D_MODEL), F32)], axis=0)
    w_s = pack(norm_gain, q_norm_gain, k_norm_gain, sinks)
    m_s = pack(m_norm_gain, m_q_norm_gain, m_k_norm_gain, m_sinks)
    v_s = pack(v_norm_gain, v_q_norm_gain, v_k_norm_gain, v_sinks)
    d_s, nm_s, nv_s = _adamw(w_s, g_small, m_s, v_s)
    unpack = lambda a: (a[0:1, :], a[1:2, 0:HEAD_DIM], a[2:3, 0:HEAD_DIM], a[3:4, 0:8])

    g_ng, g_qg, g_kg, g_sk = unpack(g_small)
    d_ng, d_qg, d_kg, d_sk = unpack(d_s)
    m_ng, m_qg, m_kg, m_sk = unpack(nm_s)
    v_ng, v_qg, v_kg, v_sk = unpack(nv_s)
    return (loss, gx.reshape(nbatch, seq, D_MODEL),
            g_ng, g_in[None], g_qg, g_kg, g_sk, g_out[None],
            d_ng, d_in[None], d_qg, d_kg, d_sk, d_out[None],
            m_ng, nm_in[None], m_qg, m_kg, m_sk, nm_out[None],
            v_ng, nv_in[None], v_qg, v_kg, v_sk, nv_out[None])
```

```python
import functools
import math

import jax
import jax.numpy as jnp
from jax import lax
from jax.experimental import pallas as pl
from jax.experimental.pallas import tpu as pltpu

F32 = jnp.float32
BF16 = jnp.bfloat16

D_MODEL = 1024
HEAD_DIM = 64
BLOCK = 128
ROPE_THETA = 10000.0
EPS = 1e-6
QA, KA, VA, GA, QB, KB, VB, GB = 0, 512, 640, 768, 1280, 1792, 2304, 2816
IN_WIDTH = 3328
SHARD_IN = IN_WIDTH // 4
SHARD_OUT = D_MODEL // 4
SCALE = 1.0 / math.sqrt(HEAD_DIM)
NEG = -1e30

ADAM_LR, ADAM_B1, ADAM_B2, ADAM_EPS, ADAM_WD, ADAM_STEP = 0.001, 0.9, 0.999, 1e-08, 0.01, 10

TM = 512
VMEM_LIMIT = 56 * 1024 * 1024
MESH = pl.DeviceIdType.MESH


def _dot(a, b):
    return jnp.dot(a, b, preferred_element_type=F32)


def _dot_nt(a, b):
    return lax.dot_general(a, b, (((1,), (1,)), ((), ())), preferred_element_type=F32)


def _dot_tn(a, b):
    return lax.dot_general(a, b, (((0,), (0,)), ((), ())), preferred_element_type=F32)


def _dot_hl(a, m):
    hi = a.astype(BF16)
    lo = (a - hi.astype(F32)).astype(BF16)
    return _dot(hi, m) + _dot(lo, m)


def _params(n_axes=None, vmem=VMEM_LIMIT):
    sem = None if n_axes is None else ("arbitrary",) * n_axes
    return pltpu.CompilerParams(dimension_semantics=sem, vmem_limit_bytes=vmem)


def _const_spec(shape):
    nd = len(shape)
    return pl.BlockSpec(shape, lambda *_: (0,) * nd)


def _rope_fwd(x, cos, sa, sb):
    return x * cos + pltpu.roll(x, 96, 1) * sa + pltpu.roll(x, 32, 1) * sb


def _rope_bwd(d, cos, sa, sb):
    return d * cos - pltpu.roll(d, 96, 1) * sa - pltpu.roll(d, 32, 1) * sb


def _lane_mask(rows, dtype=F32):
    lane = lax.broadcasted_iota(jnp.int32, (rows, 128), 1)
    return jnp.where(lane < HEAD_DIM, 1.0, 0.0).astype(dtype)


def _norm_proj(x2, ng, w_bf, qg512, kg128, g512, cos, sa, sb, wout_shard):
    T = x2.shape[0]
    steps = T // TM

    def body(x_ref, ng_ref, w_ref, qg_ref, kg_ref, g_ref, cos_ref, sa_ref, sb_ref, wo_ref,
             h_ref, qraw_ref, kraw_ref, qrot_ref, k2_ref, v2_ref, ga_ref, qb_ref, kb_ref, vb_ref, gb_ref, ao_ref,
             ssem, rsem):
        @pl.when(pl.program_id(0) == 0)
        def _():
            _gather_start(wo_ref, ao_ref, (ssem, rsem))

        xb = x_ref[...]
        r = lax.rsqrt(jnp.mean(xb * xb, axis=-1, keepdims=True) + EPS)
        h = (xb * r * ng_ref[...]).astype(BF16)
        h_ref[...] = h
        cosv, sav, sbv = cos_ref[...], sa_ref[...], sb_ref[...]
        m0 = _lane_mask(TM)

        def dup(v):
            v0 = v * m0
            v1 = v - v0
            return v0 + pltpu.roll(v0, 64, 1), v1 + pltpu.roll(v1, 64, 1)

        qa = _dot_nt(h, w_ref[QA:KA, :])
        qraw_ref[...] = qa
        qn = qa * lax.rsqrt(_dot_hl(qa * qa, g_ref[...]) + EPS) * qg_ref[...]
        for s in range(4):
            qs = _rope_fwd(qn[:, s * 128:(s + 1) * 128], cosv, sav, sbv)
            qrot_ref[:, s * 128:(s + 1) * 128] = (qs * SCALE).astype(BF16)
        ka = _dot_nt(h, w_ref[KA:VA, :])
        kraw_ref[...] = ka
        kn = ka * lax.rsqrt(_dot_hl(ka * ka, g_ref[0:128, 0:128]) + EPS) * kg_ref[...]
        k0, k1 = dup(_rope_fwd(kn, cosv, sav, sbv))
        k2_ref[:, 0:128] = k0.astype(BF16)
        k2_ref[:, 128:256] = k1.astype(BF16)
        v0, v1 = dup(_dot_nt(h, w_ref[VA:GA, :]))
        v2_ref[:, 0:128] = v0.astype(BF16)
        v2_ref[:, 128:256] = v1.astype(BF16)
        ga_ref[...] = _dot_nt(h, w_ref[GA:QB, :])
        qb_ref[...] = (_dot_nt(h, w_ref[QB:KB, :]) * SCALE).astype(BF16)
        kb_ref[...] = _dot_nt(h, w_ref[KB:VB, :]).astype(BF16)
        vb_ref[...] = _dot_nt(h, w_ref[VB:GB, :]).astype(BF16)
        gb_ref[...] = _dot_nt(h, w_ref[GB:IN_WIDTH, :])

        @pl.when(pl.program_id(0) == steps - 1)
        def _():
            _gather_finish(wo_ref, ao_ref, (ssem, rsem))

    def rows(w):
        return pl.BlockSpec((TM, w), lambda i: (i, 0))

    outs = [(D_MODEL, BF16), (512, F32), (128, F32), (512, BF16), (256, BF16), (256, BF16), (512, F32),
            (512, BF16), (512, BF16), (512, BF16), (512, F32)]
    hbm = pl.BlockSpec(memory_space=pl.ANY)
    res = pl.pallas_call(
        body, name="norm_proj", grid=(steps,),
        in_specs=[rows(D_MODEL), _const_spec((1, D_MODEL)), _const_spec((IN_WIDTH, D_MODEL)), _const_spec((1, 512)),
                  _const_spec((1, 128)), _const_spec((512, 512)), rows(128), rows(128), rows(128), hbm],
        out_specs=[rows(w) for w, _ in outs] + [hbm],
        out_shape=[jax.ShapeDtypeStruct((T, w), dt) for w, dt in outs]
        + [jax.ShapeDtypeStruct((4,) + wout_shard.shape, wout_shard.dtype)],
        scratch_shapes=[pltpu.SemaphoreType.DMA((6,)), pltpu.SemaphoreType.DMA((6,))],
        compiler_params=_params(1),
    )(x2, ng, w_bf, qg512, kg128, g512, cos, sa, sb, wout_shard)
    return res[:-1], _own_slot(res[-1], wout_shard).reshape(D_MODEL, D_MODEL)


SWA_Q = 512
SWA_SUB = SWA_Q // BLOCK


def _swa_scores(q, kst, mask, sink_ref, kv):
    s_all = _dot_nt(q, kst)
    first = lax.broadcasted_iota(jnp.int32, (256, 1), 0) < 128
    probs, stats = [], []
    for hh in range(2):
        sink = jnp.where(first, sink_ref[kv * 4 + hh], sink_ref[kv * 4 + 2 + hh])
        s = jnp.where(mask, s_all[:, hh * 256:(hh + 1) * 256], NEG)
        m = jnp.maximum(jnp.max(s, axis=1, keepdims=True), sink)
        pe = jnp.exp(s - m)
        inv = 1.0 / (jnp.sum(pe, axis=1, keepdims=True) + jnp.exp(sink - m))
        probs.append(pe * inv)
        stats.append(jnp.exp(sink - m) * inv)
    return probs, stats


def _swa_mask(has_prev):
    r = lax.broadcasted_iota(jnp.int32, (256, 256), 0) & 127
    c = lax.broadcasted_iota(jnp.int32, (256, 256), 1)
    band = (c > r) & (c <= r + 128)
    return band if has_prev is True else band & ((c >= 128) | has_prev)


def _stack_pairs(ref, rows, kv):
    return jnp.concatenate([ref[rows, (2 * kv) * 128:(2 * kv + 1) * 128], ref[rows, (2 * kv + 1) * 128:(2 * kv + 2) * 128]], axis=0)


def _swa_keys(prev_ref, main_ref, s, kv, m0b):
    cols = slice(kv * 128, (kv + 1) * 128)
    prev = prev_ref[:, cols] if s == 0 else main_ref[(s - 1) * 128:s * 128, cols]
    kc = jnp.concatenate([prev, main_ref[s * 128:(s + 1) * 128, cols]], axis=0)
    k0 = kc * m0b
    return jnp.concatenate([k0, kc - k0], axis=0)


def _swa_fwd(sinks, qrot, k2, v2, nbatch, seq):
    ni = seq // SWA_Q
    T = nbatch * seq

    def body(sink_ref, q_ref, kp_ref, km_ref, vp_ref, vm_ref, o_ref):
        i = pl.program_id(1)
        m0b = _lane_mask(256, BF16)
        for s in range(SWA_SUB):
            mask = _swa_mask(True if s else i > 0)
            rows = slice(s * 128, (s + 1) * 128)
            for kv in range(2):
                kst = _swa_keys(kp_ref, km_ref, s, kv, m0b)
                vst = _swa_keys(vp_ref, vm_ref, s, kv, m0b)
                probs, _ = _swa_scores(_stack_pairs(q_ref, rows, kv), kst, mask, sink_ref, kv)
                o2 = _dot(jnp.concatenate(probs, axis=1).astype(BF16), vst)
                o_ref[rows, kv * 256:kv * 256 + 128] = o2[0:128]
                o_ref[rows, kv * 256 + 128:(kv + 1) * 256] = o2[128:256]

    main = lambda b, i: (b * ni + i, 0)
    prev = lambda b, i: ((b * ni + i) * SWA_SUB - jnp.where(i > 0, 1, 0), 0)
    return pl.pallas_call(
        body, name="swa_fwd", grid=(nbatch, ni),
        in_specs=[pl.BlockSpec(memory_space=pltpu.SMEM), pl.BlockSpec((SWA_Q, 512), main),
                  pl.BlockSpec((128, 256), prev), pl.BlockSpec((SWA_Q, 256), main),
                  pl.BlockSpec((128, 256), prev), pl.BlockSpec((SWA_Q, 256), main)],
        out_specs=pl.BlockSpec((SWA_Q, 512), main),
        out_shape=jax.ShapeDtypeStruct((T, 512), F32),
        compiler_params=_params(2),
    )(sinks, qrot, k2, k2, v2, v2)


def _swa_bwd(sinks, qrot, k2, v2, doa, cout4, nbatch, seq):
    ni = seq // SWA_Q
    T = nbatch * seq

    def body(sink_ref, q_ref, kp_ref, km_ref, vp_ref, vm_ref, do_ref, cout_ref,
             dq_ref, dk_ref, dv_ref, ds_ref, rout_ref, dkc, dvc, ssem, rsem):
        b, i = pl.program_id(0), pl.program_id(1)
        copies = lambda: _chip_exchange(cout_ref, rout_ref, (ssem, rsem))

        @pl.when((b == 0) & (i == 0))
        def _():
            ds_ref[...] = jnp.zeros_like(ds_ref)
            for cp in copies():
                cp.start()

        @pl.when((b == nbatch - 1) & (i == ni))
        def _():
            for cp in copies():
                cp.wait_recv()
            for cp in copies():
                cp.wait_send()

        @pl.when(i == 0)
        def _():
            dkc[...] = jnp.zeros_like(dkc)
            dvc[...] = jnp.zeros_like(dvc)

        @pl.when(i < ni)
        def _():
            m0b = _lane_mask(256, BF16)
            m0 = _lane_mask(128) > 0.5
            for kv in range(2):
                kcols = slice(kv * 128, (kv + 1) * 128)
                dk_own, dv_own = dkc[:, kcols], dvc[:, kcols]
                for s in range(SWA_SUB):
                    mask = _swa_mask(True if s else i > 0)
                    rows = slice(s * 128, (s + 1) * 128)
                    kst = _swa_keys(kp_ref, km_ref, s, kv, m0b)
                    vst = _swa_keys(vp_ref, vm_ref, s, kv, m0b)
                    q, do = _stack_pairs(q_ref, rows, kv), _stack_pairs(do_ref, rows, kv)
                    probs, psink = _swa_scores(q, kst, mask, sink_ref, kv)
                    dp_all = _dot_nt(do, vst)
                    ds_parts = []
                    for hh in range(2):
                        dp = dp_all[:, hh * 256:(hh + 1) * 256]
                        delta = jnp.sum(probs[hh] * dp, axis=1, keepdims=True)
                        ds_parts.append(probs[hh] * (dp - delta))
                        dsink = psink[hh] * delta
                        for pr in range(2):
                            h = kv * 4 + pr * 2 + hh
                            ds_ref[h:h + 1, :] = ds_ref[h:h + 1, :] - jnp.sum(dsink[pr * 128:(pr + 1) * 128])
                    ds_all = jnp.concatenate(ds_parts, axis=1).astype(BF16)
                    p_all = jnp.concatenate(probs, axis=1).astype(BF16)
                    dq2 = _dot(ds_all, kst) * SCALE
                    dq_ref[rows, kv * 256:kv * 256 + 128] = dq2[0:128]
                    dq_ref[rows, kv * 256 + 128:(kv + 1) * 256] = dq2[128:256]
                    dkst = _dot_tn(ds_all, q)
                    dvst = _dot_tn(p_all, do)
                    dk_ref[rows, kcols] = dk_own + jnp.where(m0, dkst[0:128], dkst[256:384])
                    dv_ref[rows, kcols] = dv_own + jnp.where(m0, dvst[0:128], dvst[256:384])
                    dk_own = jnp.where(m0, dkst[128:256], dkst[384:512])
                    dv_own = jnp.where(m0, dvst[128:256], dvst[384:512])
                dkc[:, kcols] = dk_own
                dvc[:, kcols] = dv_own

        @pl.when(i == ni)
        def _():
            dk_ref[...] = jnp.zeros_like(dk_ref)
            dv_ref[...] = jnp.zeros_like(dv_ref)
            dk_ref[0:128, :] = dkc[...]
            dv_ref[0:128, :] = dvc[...]

    main = lambda b, i: (b * ni + jnp.minimum(i, ni - 1), 0)
    prev = lambda b, i: ((b * ni + jnp.minimum(i, ni - 1)) * SWA_SUB - jnp.where(jnp.minimum(i, ni - 1) > 0, 1, 0), 0)
    shifted = lambda b, i: (b * (ni + 1) + i, 0)
    tpad = nbatch * (ni + 1) * SWA_Q
    return pl.pallas_call(
        body, name="swa_bwd", grid=(nbatch, ni + 1),
        in_specs=[pl.BlockSpec(memory_space=pltpu.SMEM), pl.BlockSpec((SWA_Q, 512), main),
                  pl.BlockSpec((128, 256), prev), pl.BlockSpec((SWA_Q, 256), main),
                  pl.BlockSpec((128, 256), prev), pl.BlockSpec((SWA_Q, 256), main),
                  pl.BlockSpec((SWA_Q, 512), main), pl.BlockSpec(memory_space=pl.ANY)],
        out_specs=[pl.BlockSpec((SWA_Q, 512), main), pl.BlockSpec((SWA_Q, 256), shifted),
                   pl.BlockSpec((SWA_Q, 256), shifted), _const_spec((8, 128)), pl.BlockSpec(memory_space=pl.ANY)],
        out_shape=[jax.ShapeDtypeStruct((T, 512), F32), jax.ShapeDtypeStruct((tpad, 256), F32),
                   jax.ShapeDtypeStruct((tpad, 256), F32), jax.ShapeDtypeStruct((8, 128), F32),
                   jax.ShapeDtypeStruct((3,) + cout4.shape[1:], cout4.dtype)],
        scratch_shapes=[pltpu.VMEM((128, 256), F32), pltpu.VMEM((128, 256), F32),
                        pltpu.SemaphoreType.DMA((3,)), pltpu.SemaphoreType.DMA((3,))],
        compiler_params=_params(2),
    )(sinks, qrot, k2, k2, v2, v2, doa, cout4)


def _unshift(dkpad, nbatch, seq):
    return dkpad.reshape(nbatch, seq + SWA_Q, 256)[:, BLOCK:BLOCK + seq].reshape(nbatch * seq, 256)


SB_T = 256
SB_TQ = 2 * SB_T


def _sb_mask(kind):
    if kind == "full":
        return None
    rows = SB_T if kind == "B" else SB_TQ
    r = lax.broadcasted_iota(jnp.int32, (rows, 2 * SB_T), 0)
    c = lax.broadcasted_iota(jnp.int32, (rows, 2 * SB_T), 1)
    causal = (c & (SB_T - 1)) < r
    return causal | (r >= SB_T) if kind == "A" else causal


def _sb_rows(kind):
    return slice(SB_T, SB_TQ) if kind == "B" else slice(0, SB_TQ)


def _lower(x):
    return jnp.concatenate([jnp.zeros_like(x), x], axis=0)


def _sb_logits(neg_q, kst):
    nz = _dot_nt(neg_q, kst)
    sign = jnp.uint32(0x80000000)
    neg_abs = lax.bitcast_convert_type(lax.bitcast_convert_type(nz, jnp.uint32) | sign, F32)
    return nz, jnp.minimum(nz, 0.0) - jnp.log(1.0 + jnp.exp(neg_abs))


SB_NP = 4


def _pair_rows(ref, j, cols, m0b):
    kj = ref[pl.ds(pl.multiple_of(j * SB_T, SB_T), SB_T), cols]
    k0 = kj * m0b
    return jnp.concatenate([k0, kj - k0], axis=0)


def _bcast2(c0, c1):
    rows = c0.shape[0]
    return jnp.concatenate([jnp.broadcast_to(c0, (rows, SB_T)), jnp.broadcast_to(c1, (rows, SB_T))], axis=1)


def _rowsum2(x):
    return jnp.sum(x[:, 0:SB_T], axis=1, keepdims=True), jnp.sum(x[:, SB_T:2 * SB_T], axis=1, keepdims=True)


def _scan2(x, tri2):
    outs = []
    for h in range(2):
        xh = x[:, h * SB_T:(h + 1) * SB_T]
        hi = xh.astype(BF16)
        lo = (xh - hi.astype(F32)).astype(BF16)
        outs.append(_dot(jnp.concatenate([hi, lo], axis=1), tri2))
    return jnp.concatenate(outs, axis=1)


def _scan1(x, tri):
    xb = x.astype(BF16)
    return jnp.concatenate([_dot(xb[:, h * SB_T:(h + 1) * SB_T], tri) for h in range(2)], axis=1)


def _sb_fwd(qb, kb, vb, ublk, nbatch, seq):
    nq, nk = seq // SB_TQ, seq // SB_T
    T = nbatch * seq

    def body(q_ref, k_ref, v_ref, u_ref, o_ref, wst_ref, ost_ref, wbuf, obuf, sems):
        b, i = pl.program_id(0), pl.program_id(2)
        m0b = _lane_mask(SB_T, BF16)
        u = u_ref[...]
        pairs = [slice(pp * 128, (pp + 1) * 128) for pp in range(SB_NP)]
        neg_qs = [-q_ref[:, cols] for cols in pairs]

        def stores(j, slot):
            tix = (b * nq + i) * nk + j
            return (pltpu.make_async_copy(wbuf.at[slot], wst_ref.at[tix], sems.at[0, slot]),
                    pltpu.make_async_copy(obuf.at[slot], ost_ref.at[tix], sems.at[1, slot]))

        def tile(j, carries, kind, slot, wait):
            rows, mask = _sb_rows(kind), _sb_mask(kind)
            if wait:
                for cp in stores(j, slot):
                    cp.wait()
            out = []
            for pp, (cols, neg_q, (c0, c1, acc)) in enumerate(zip(pairs, neg_qs, carries)):
                kst = _pair_rows(k_ref, j, cols, m0b)
                vst = _pair_rows(v_ref, j, cols, m0b)
                nz, lb = _sb_logits(neg_q[rows], kst)
                if mask is not None:
                    lb = jnp.where(mask, lb, 0.0)
                incl = _scan2(lb, u)
                w = jnp.exp(incl - nz if kind == "B" else incl + _bcast2(c0, c1) - nz)
                if mask is not None:
                    w = jnp.where(mask, w, 0.0)
                wb = w.astype(BF16)
                wbuf[slot, pp, rows, :] = wb
                obuf[slot, pp, rows, :] = jnp.exp(lb).astype(BF16)
                if kind == "B":
                    wbuf[slot, pp, 0:SB_T, :] = jnp.zeros((SB_T, 2 * SB_T), BF16)
                    obuf[slot, pp, 0:SB_T, :] = jnp.zeros((SB_T, 2 * SB_T), BF16)
                d0, d1, da = incl[:, 0:1], incl[:, SB_T:SB_T + 1], _dot(wb, vst)
                if kind == "B":
                    d0, d1, da = _lower(d0), _lower(d1), _lower(da)
                out.append((c0 + d0, c1 + d1, acc + da))
            for cp in stores(j, slot):
                cp.start()
            return tuple(out)

        zc = jnp.zeros((SB_TQ, 1), F32)
        carries = tile(2 * i + 1, ((zc, zc, jnp.zeros((SB_TQ, 128), F32)),) * SB_NP, "B", 0, False)
        carries = tile(2 * i, carries, "A", 1, False)

        def two(jj, cr):
            cr = tile(2 * i - 1 - 2 * jj, cr, "full", 0, True)
            return tile(2 * i - 2 - 2 * jj, cr, "full", 1, True)

        carries = lax.fori_loop(0, i, two, carries)
        for cols, carry in zip(pairs, carries):
            o_ref[:, cols] = carry[2]
        for slot in range(2):
            for cp in stores(0, slot):
                cp.wait()

    wide = 128 * SB_NP
    blk = lambda b, g, i: (b * nq + i, g)
    full = lambda b, g, i: (b, g)
    hbm = pl.BlockSpec(memory_space=pl.ANY)
    tiles = jax.ShapeDtypeStruct((nbatch * nq * nk, SB_NP, SB_TQ, 2 * SB_T), BF16)
    return pl.pallas_call(
        body, name="sb_fwd", grid=(nbatch, 4 // SB_NP, nq),
        in_specs=[pl.BlockSpec((SB_TQ, wide), blk), pl.BlockSpec((seq, wide), full), pl.BlockSpec((seq, wide), full),
                  _const_spec((2 * SB_T, SB_T))],
        out_specs=[pl.BlockSpec((SB_TQ, wide), blk), hbm, hbm],
        out_shape=[jax.ShapeDtypeStruct((T, 512), F32), tiles, tiles],
        scratch_shapes=[pltpu.VMEM((2, SB_NP, SB_TQ, 2 * SB_T), BF16), pltpu.VMEM((2, SB_NP, SB_TQ, 2 * SB_T), BF16),
                        pltpu.SemaphoreType.DMA((2, 2))],
        compiler_params=_params(3),
    )(qb, kb, vb, ublk)


def _sb_bwd(qb, kb, vb, dob, wst, ost, pblk, nbatch, seq):
    nq, nk = seq // SB_TQ, seq // SB_T
    T = nbatch * seq

    def body(q_ref, k_ref, v_ref, do_ref, wst_ref, ost_ref, up_ref, dq_ref, dk_ref, dv_ref, wbuf, obuf, sems):
        b, i = pl.program_id(0), pl.program_id(2)

        @pl.when(i == 0)
        def _():
            dk_ref[...] = jnp.zeros_like(dk_ref)
            dv_ref[...] = jnp.zeros_like(dv_ref)

        m0b = _lane_mask(SB_T, BF16)
        m0 = _lane_mask(SB_T) > 0.5
        up = up_ref[...]
        pairs = [slice(pp * 128, (pp + 1) * 128) for pp in range(SB_NP)]

        def loads(j, slot):
            tix = (b * nq + i) * nk + j
            return (pltpu.make_async_copy(wst_ref.at[tix], wbuf.at[slot], sems.at[0, slot]),
                    pltpu.make_async_copy(ost_ref.at[tix], obuf.at[slot], sems.at[1, slot]))

        def tile(j, carries, kind, slot, fetch_next):
            rows, mask = _sb_rows(kind), _sb_mask(kind)
            if fetch_next:
                for cp in loads(j + 1, 1 - slot):
                    cp.start()
            for cp in loads(j, slot):
                cp.wait()
            out = []
            for pp, (cols, (s0, s1, dq)) in enumerate(zip(pairs, carries)):
                q, do = q_ref[rows, cols], do_ref[rows, cols]
                kst = _pair_rows(k_ref, j, cols, m0b)
                vst = _pair_rows(v_ref, j, cols, m0b)
                wb = wbuf[slot, pp, rows, :]
                e = _dot_nt(do, vst) * wb.astype(F32)
                dlb = _bcast2(s0[rows], s1[rows]) + _scan1(e, up)
                dz = (e + dlb) * obuf[slot, pp, rows, :].astype(F32) - dlb
                if mask is not None:
                    dz = jnp.where(mask, dz, 0.0)
                dzb = dz.astype(BF16)
                dkst = _dot_tn(dzb, q)
                dvst = _dot_tn(wb, do)
                keys = pl.ds(pl.multiple_of(j * SB_T, SB_T), SB_T)
                dk_ref[keys, cols] = dk_ref[keys, cols] + jnp.where(m0, dkst[0:SB_T], dkst[SB_T:2 * SB_T])
                dv_ref[keys, cols] = dv_ref[keys, cols] + jnp.where(m0, dvst[0:SB_T], dvst[SB_T:2 * SB_T])
                x0, x1 = _rowsum2(e)
                ddq = _dot(dzb, kst)
                if kind == "B":
                    x0, x1, ddq = _lower(x0), _lower(x1), _lower(ddq)
                out.append((s0 + x0, s1 + x1, dq + ddq))
            return tuple(out)

        for cp in loads(0, 0):
            cp.start()

        def two(jj, cr):
            cr = tile(2 * jj, cr, "full", 0, True)
            return tile(2 * jj + 1, cr, "full", 1, True)

        zc = jnp.zeros((SB_TQ, 1), F32)
        carries = lax.fori_loop(0, i, two, ((zc, zc, jnp.zeros((SB_TQ, 128), F32)),) * SB_NP)
        carries = tile(2 * i, carries, "A", 0, True)
        carries = tile(2 * i + 1, carries, "B", 1, False)
        for cols, carry in zip(pairs, carries):
            dq_ref[:, cols] = carry[2] * SCALE

    wide = 128 * SB_NP
    blk = lambda b, g, i: (b * nq + i, g)
    full = lambda b, g, i: (b, g)
    hbm = pl.BlockSpec(memory_space=pl.ANY)
    return pl.pallas_call(
        body, name="sb_bwd", grid=(nbatch, 4 // SB_NP, nq),
        in_specs=[pl.BlockSpec((SB_TQ, wide), blk), pl.BlockSpec((seq, wide), full), pl.BlockSpec((seq, wide), full),
                  pl.BlockSpec((SB_TQ, wide), blk), hbm, hbm, _const_spec((SB_T, SB_T))],
        out_specs=[pl.BlockSpec((SB_TQ, wide), blk), pl.BlockSpec((seq, wide), full), pl.BlockSpec((seq, wide), full)],
        out_shape=[jax.ShapeDtypeStruct((T, 512), F32)] * 3,
        scratch_shapes=[pltpu.VMEM((2, SB_NP, SB_TQ, 2 * SB_T), BF16), pltpu.VMEM((2, SB_NP, SB_TQ, 2 * SB_T), BF16),
                        pltpu.SemaphoreType.DMA((2, 2))],
        compiler_params=_params(3),
    )(qb, kb, vb, dob, wst, ost, pblk)


def _sigmoid(g):
    return 1.0 / (1.0 + jnp.exp(-g))


def _out_proj(oa, ob, ga, gb, x2, tgt, wout_bf):
    T = x2.shape[0]

    def body(oa_ref, ob_ref, ga_ref, gb_ref, x_ref, t_ref, w_ref,
             dout_ref, doa_ref, dob_ref, dga_ref, dgb_ref, dw_ref, loss_ref):
        @pl.when(pl.program_id(0) == 0)
        def _():
            loss_ref[...] = jnp.zeros_like(loss_ref)
            dw_ref[...] = jnp.zeros_like(dw_ref)

        halves = ((oa_ref, ga_ref, doa_ref, dga_ref, 0), (ob_ref, gb_ref, dob_ref, dgb_ref, 512))
        out = x_ref[...]
        gated = []
        for o_ref, g_ref, _, _, lo in halves:
            g = g_ref[...]
            sg = _sigmoid(g)
            y = (o_ref[...] * (g * sg)).astype(BF16)
            out = out + _dot(y, w_ref[lo:lo + 512, :])
            gated.append((y, g, sg))
        diff = out - t_ref[...]
        dout = diff * (1.0 / D_MODEL)
        dout_ref[...] = dout
        loss_ref[...] = loss_ref[...] + jnp.sum(diff * diff) * (0.5 / D_MODEL)
        db = dout.astype(BF16)
        for (o_ref, _, do_ref, dg_ref, lo), (y, g, sg) in zip(halves, gated):
            dw_ref[lo:lo + 512, :] = dw_ref[lo:lo + 512, :] + _dot_tn(y, db)
            dy = _dot_nt(db, w_ref[lo:lo + 512, :])
            do_ref[...] = (dy * (g * sg)).astype(BF16)
            dg_ref[...] = (dy * o_ref[...] * (sg * (1.0 + g * (1.0 - sg)))).astype(BF16)

    rows = lambda w: pl.BlockSpec((TM, w), lambda i: (i, 0))
    return pl.pallas_call(
        body, name="out_proj", grid=(T // TM,),
        in_specs=[rows(512), rows(512), rows(512), rows(512), rows(D_MODEL), rows(D_MODEL),
                  _const_spec((D_MODEL, D_MODEL))],
        out_specs=[rows(D_MODEL)] + [rows(512)] * 4 + [_const_spec((D_MODEL, D_MODEL)), _const_spec((8, 128))],
        out_shape=[jax.ShapeDtypeStruct((T, D_MODEL), F32)] + [jax.ShapeDtypeStruct((T, 512), BF16)] * 4
        + [jax.ShapeDtypeStruct((D_MODEL, D_MODEL), F32), jax.ShapeDtypeStruct((8, 128), F32)],
        compiler_params=_params(1),
    )(oa, ob, ga, gb, x2, tgt, wout_bf)


def _qk_grad(qraw, kraw, dqrot, dk2, dv2, qg512, kg128, g512, cos, sa, sb):
    T = qraw.shape[0]

    def body(qraw_ref, kraw_ref, dqrot_ref, dk2_ref, dv2_ref, qg_ref, kg_ref, g_ref, cos_ref, sa_ref, sb_ref,
             dqa_ref, dkv_ref, dqg_ref, dkg_ref):
        @pl.when(pl.program_id(0) == 0)
        def _():
            dqg_ref[...] = jnp.zeros_like(dqg_ref)
            dkg_ref[...] = jnp.zeros_like(dkg_ref)

        cosv, sav, sbv = cos_ref[...], sa_ref[...], sb_ref[...]
        m0 = _lane_mask(TM) > 0.5

        def head_norm_bwd(raw, dn_rot, gmat, gain):
            r = lax.rsqrt(_dot_hl(raw * raw, gmat) + EPS)
            n = raw * r
            dn = dn_rot * gain
            return r * (dn - n * _dot_hl(dn * n, gmat)), jnp.sum(dn_rot * n, axis=0, keepdims=True)

        def fold(ref):
            a, b = ref[:, 0:128], ref[:, 128:256]
            return jnp.where(m0, a + pltpu.roll(a, 64, 1), b + pltpu.roll(b, 64, 1))

        dqn = jnp.concatenate([_rope_bwd(dqrot_ref[:, s * 128:(s + 1) * 128], cosv, sav, sbv) for s in range(4)], axis=1)
        dqa, dqg = head_norm_bwd(qraw_ref[...], dqn, g_ref[...], qg_ref[...])
        dka, dkg = head_norm_bwd(kraw_ref[...], _rope_bwd(fold(dk2_ref), cosv, sav, sbv), g_ref[0:128, 0:128], kg_ref[...])
        dqa_ref[...] = dqa.astype(BF16)
        dkv_ref[:, 0:128] = dka.astype(BF16)
        dkv_ref[:, 128:256] = fold(dv2_ref).astype(BF16)
        dqg_ref[...] = dqg_ref[...] + dqg
        dkg_ref[...] = dkg_ref[...] + dkg

    rows = lambda w: pl.BlockSpec((TM, w), lambda i: (i, 0))
    return pl.pallas_call(
        body, name="qk_grad", grid=(T // TM,),
        in_specs=[rows(512), rows(128), rows(512), rows(256), rows(256), _const_spec((1, 512)), _const_spec((1, 128)),
                  _const_spec((512, 512)), rows(128), rows(128), rows(128)],
        out_specs=[rows(512), rows(256), _const_spec((1, 512)), _const_spec((1, 128))],
        out_shape=[jax.ShapeDtypeStruct((T, 512), BF16), jax.ShapeDtypeStruct((T, 256), BF16),
                   jax.ShapeDtypeStruct((1, 512), F32), jax.ShapeDtypeStruct((1, 128), F32)],
        compiler_params=_params(1),
    )(qraw, kraw, dqrot, dk2, dv2, qg512, kg128, g512, cos, sa, sb)


_PIECES = ((QA, 512), (KA, 256), (GA, 512), (QB, 512), (KB, 512), (VB, 512), (GB, 512))


def _w_in_grad(h, pieces):
    T = h.shape[0]

    def body(h_ref, *refs):
        dw_ref = refs[-1]

        @pl.when(pl.program_id(0) == 0)
        def _():
            dw_ref[...] = jnp.zeros_like(dw_ref)

        hb = h_ref[...]
        for (lo, width), p_ref in zip(_PIECES, refs[:-1]):
            dw_ref[lo:lo + width, :] = dw_ref[lo:lo + width, :] + _dot_tn(p_ref[...].astype(BF16), hb)

    rows = lambda w: pl.BlockSpec((TM, w), lambda i: (i, 0))
    return pl.pallas_call(
        body, name="w_in_grad", grid=(T // TM,),
        in_specs=[rows(D_MODEL)] + [rows(w) for _, w in _PIECES],
        out_specs=_const_spec((IN_WIDTH, D_MODEL)),
        out_shape=jax.ShapeDtypeStruct((IN_WIDTH, D_MODEL), F32),
        compiler_params=_params(1),
    )(h, *pieces)


def _chip_exchange(src_ref, dst_ref, sems):
    _, _, c, chips = _place()
    return [_remote(src_ref.at[2 * cx + cy], dst_ref.at[j], sems, j, (cx, cy, c)) for j, (cx, cy) in enumerate(chips)]


def _x_grad(x2, dout, pieces, w_bf, ng, cin4):
    T = x2.shape[0]
    npc = len(_PIECES)
    steps = T // TM

    def body(x_ref, dout_ref, *refs):
        w_ref, ng_ref, cin_ref, gx_ref, dng_ref, rin_ref, ssem, rsem = refs[npc:]
        step = pl.program_id(0)
        copies = lambda: _chip_exchange(cin_ref, rin_ref, (ssem, rsem))

        @pl.when(step == 0)
        def _():
            dng_ref[...] = jnp.zeros_like(dng_ref)
            for cp in copies():
                cp.start()

        dh = jnp.zeros((TM, D_MODEL), F32)
        for (lo, width), p_ref in zip(_PIECES, refs[:npc]):
            dh = dh + _dot(p_ref[...].astype(BF16), w_ref[lo:lo + width, :])
        xb = x_ref[...]
        r = lax.rsqrt(jnp.mean(xb * xb, axis=-1, keepdims=True) + EPS)
        n = xb * r
        dn = dh * ng_ref[...]
        gx_ref[...] = dout_ref[...] + r * (dn - n * jnp.mean(dn * n, axis=-1, keepdims=True))
        dng_ref[...] = dng_ref[...] + jnp.sum(dh * n, axis=0, keepdims=True)

        @pl.when(step == steps - 1)
        def _():
            for cp in copies():
                cp.wait_recv()
            for cp in copies():
                cp.wait_send()

    rows = lambda w: pl.BlockSpec((TM, w), lambda i: (i, 0))
    hbm = pl.BlockSpec(memory_space=pl.ANY)
    return pl.pallas_call(
        body, name="x_grad", grid=(steps,),
        in_specs=[rows(D_MODEL), rows(D_MODEL)] + [rows(w) for _, w in _PIECES]
        + [_const_spec((IN_WIDTH, D_MODEL)), _const_spec((1, D_MODEL)), hbm],
        out_specs=[rows(D_MODEL), _const_spec((1, D_MODEL)), hbm],
        out_shape=[jax.ShapeDtypeStruct((T, D_MODEL), F32), jax.ShapeDtypeStruct((1, D_MODEL), F32),
                   jax.ShapeDtypeStruct((3,) + cin4.shape[1:], cin4.dtype)],
        scratch_shapes=[pltpu.SemaphoreType.DMA((3,)), pltpu.SemaphoreType.DMA((3,))],
        compiler_params=_params(1),
    )(x2, dout, *pieces, w_bf, ng, cin4)


HBM = pl.BlockSpec(memory_space=pl.ANY)


def _place():
    x, y, c = lax.axis_index("x"), lax.axis_index("y"), lax.axis_index("c")
    chips = [(1 - x, y), (x, 1 - y), (1 - x, 1 - y)]
    return x, y, c, chips


def _remote(src, dst, sems, k, to):
    return pltpu.make_async_remote_copy(src_ref=src, dst_ref=dst, send_sem=sems[0].at[k], recv_sem=sems[1].at[k],
                                        device_id=to, device_id_type=MESH)


def _gather_start(src, dst, sems):
    x, y, c, chips = _place()
    for j, (cx, cy) in enumerate(chips):
        _remote(src.at[c], dst.at[2 * x + y, c], sems, j, (cx, cy, c)).start()


def _gather_finish(src, dst, sems):
    x, y, c, chips = _place()
    sib = (x, y, 1 - c)
    fwds = []
    for j, (cx, cy) in enumerate(chips):
        k = 2 * cx + cy
        _remote(src.at[c], dst.at[k, c], sems, j, sib).wait_recv()
        fwds.append(_remote(dst.at[k, c], dst.at[k, c], sems, 3 + j, sib))
        fwds[-1].start()
    for j, (cx, cy) in enumerate(chips):
        _remote(src.at[c], dst.at[2 * cx + cy, 1 - c], sems, 3 + j, sib).wait_recv()
    for j, (cx, cy) in enumerate(chips):
        _remote(src.at[c], dst.at[2 * x + y, c], sems, j, (cx, cy, c)).wait_send()
    for cp in fwds:
        cp.wait_send()


def _own_slot(gathered, shard):
    me = 2 * lax.axis_index("x") + lax.axis_index("y")
    return lax.dynamic_update_slice(gathered, shard[None], (me, 0, 0, 0))


def _gather_weights(shard):
    def body(src, dst, ssem, rsem):
        _gather_start(src, dst, (ssem, rsem))
        _gather_finish(src, dst, (ssem, rsem))

    gathered = pl.pallas_call(
        body, name="gather_weights", in_specs=[HBM], out_specs=HBM,
        out_shape=jax.ShapeDtypeStruct((4,) + shard.shape, shard.dtype),
        scratch_shapes=[pltpu.SemaphoreType.DMA((6,)), pltpu.SemaphoreType.DMA((6,))],
    )(shard)
    return _own_slot(gathered, shard)


def _pair_exchange(g4):
    def body(src, dst, ssem, rsem):
        x, y, c, _ = _place()
        cp = _remote(src.at[:, pl.ds(1 - c, 1)], dst, (ssem, rsem), 0, (x, y, 1 - c))
        cp.start()
        cp.wait()

    return pl.pallas_call(
        body, name=f"pair_exchange_{g4.shape[2]}", in_specs=[HBM], out_specs=HBM,
        out_shape=jax.ShapeDtypeStruct((4, 1) + g4.shape[2:], g4.dtype),
        scratch_shapes=[pltpu.SemaphoreType.DMA((1,)), pltpu.SemaphoreType.DMA((1,))],
    )(g4)


def _share_halves(hin, hout, small):
    def body(hin_ref, hout_ref, small_ref, oin_ref, oout_ref, sall_ref, ssem, rsem, lsem):
        x, y, c, _ = _place()
        sems = (ssem, rsem)
        sib = (x, y, 1 - c)
        me = 4 * x + 2 * y + c
        own = pltpu.make_async_copy(small_ref, sall_ref.at[me], lsem.at[0])
        own.start()
        sent = [_remote(hin_ref, oin_ref, sems, 0, sib), _remote(hout_ref, oout_ref, sems, 1, sib)]
        flips = [(fx, fy, fc) for fx in (0, 1) for fy in (0, 1) for fc in (0, 1)][1:]
        for k, (fx, fy, fc) in enumerate(flips):
            sent.append(_remote(small_ref, sall_ref.at[me], sems, 2 + k, (x ^ fx, y ^ fy, c ^ fc)))
        for cp in sent:
            cp.start()
        _remote(hin_ref, oin_ref, sems, 0, sib).wait_recv()
        _remote(hout_ref, oout_ref, sems, 1, sib).wait_recv()
        for k, (fx, fy, fc) in enumerate(flips):
            src = 4 * (x ^ fx) + 2 * (y ^ fy) + (c ^ fc)
            _remote(small_ref, sall_ref.at[src], sems, 2 + k, sib).wait_recv()
        for cp in sent:
            cp.wait_send()
        own.wait()

    return pl.pallas_call(
        body, name="share_halves", in_specs=[HBM, HBM, HBM], out_specs=[HBM, HBM, HBM],
        out_shape=[jax.ShapeDtypeStruct(hin.shape, F32), jax.ShapeDtypeStruct(hout.shape, F32),
                   jax.ShapeDtypeStruct((8,) + small.shape, F32)],
        scratch_shapes=[pltpu.SemaphoreType.DMA((9,)), pltpu.SemaphoreType.DMA((9,)), pltpu.SemaphoreType.DMA((1,))],
    )(hin, hout, small)


def _add_half(cidx, full4, recv4):
    _, _, rows, width = full4.shape

    def body(c_ref, a_ref, b_ref, o_ref):
        o_ref[0] = (a_ref[0, 0] + b_ref[0, 0]).astype(BF16)

    return pl.pallas_call(
        body, name=f"add_half_{rows}",
        grid_spec=pltpu.PrefetchScalarGridSpec(
            num_scalar_prefetch=1, grid=(4,),
            in_specs=[pl.BlockSpec((1, 1, rows, width), lambda k, c: (k, c[0], 0, 0)),
                      pl.BlockSpec((1, 1, rows, width), lambda k, c: (k, 0, 0, 0))],
            out_specs=pl.BlockSpec((1, rows, width), lambda k, c: (k, 0, 0))),
        out_shape=jax.ShapeDtypeStruct((4, rows, width), BF16),
        compiler_params=_params(1),
    )(cidx, full4, recv4)


def _sum_chips(chip, own4, recv3):
    _, rows, width = recv3.shape
    rb = rows // 2

    def body(k_ref, a_ref, r_ref, o_ref):
        acc = a_ref[0].astype(F32)
        for s in range(3):
            acc = acc + r_ref[s].astype(F32)
        o_ref[...] = acc

    return pl.pallas_call(
        body, name=f"sum_chips_{rows}",
        grid_spec=pltpu.PrefetchScalarGridSpec(
            num_scalar_prefetch=1, grid=(rows // rb,),
            in_specs=[pl.BlockSpec((1, rb, width), lambda i, k: (k[0], i, 0)),
                      pl.BlockSpec((3, rb, width), lambda i, k: (0, i, 0))],
            out_specs=pl.BlockSpec((rb, width), lambda i, k: (i, 0))),
        out_shape=jax.ShapeDtypeStruct((rows, width), F32),
        compiler_params=_params(1),
    )(chip, own4, recv3)


def _sum_slots(r4):
    n, rows, width = r4.shape

    def body(r_ref, o_ref):
        acc = r_ref[0]
        for s in range(1, n):
            acc = acc + r_ref[s]
        o_ref[...] = acc

    return pl.pallas_call(
        body, name=f"sum_slots_{n}_{rows}_{width}", grid=(1,),
        in_specs=[pl.BlockSpec((n, rows, width), lambda i: (0, 0, 0))],
        out_specs=pl.BlockSpec((rows, width), lambda i: (0, 0)),
        out_shape=jax.ShapeDtypeStruct((rows, width), F32),
        compiler_params=_params(1),
    )(r4)


def _adam_math(w, g, m, v):
    c1 = 1.0 - ADAM_B1 ** ADAM_STEP
    c2 = 1.0 - ADAM_B2 ** ADAM_STEP
    nm = ADAM_B1 * m + (1.0 - ADAM_B1) * g
    nv = ADAM_B2 * v + (1.0 - ADAM_B2) * (g * g)
    return -ADAM_LR * ((nm / c1) / (jnp.sqrt(nv / c2) + ADAM_EPS) + ADAM_WD * w), nm, nv


def _adamw(w, g, m, v):
    rows, width = w.shape

    def body(w_ref, g_ref, m_ref, v_ref, d_ref, nm_ref, nv_ref):
        d_ref[...], nm_ref[...], nv_ref[...] = _adam_math(w_ref[...], g_ref[...], m_ref[...], v_ref[...])

    spec = pl.BlockSpec((rows, width), lambda i: (0, 0))
    return pl.pallas_call(
        body, name=f"adamw_{rows}_{width}", grid=(1,),
        in_specs=[spec] * 4, out_specs=[spec] * 3,
        out_shape=[jax.ShapeDtypeStruct((rows, width), F32)] * 3,
        compiler_params=_params(1),
    )(w, g, m, v)


def _adamw_halves(cidx, w, own, recv, m, v):
    rows, width = w.shape
    rb = rows // 4

    def body(c_ref, w_ref, own_ref, recv_ref, m_ref, v_ref, g_ref, d_ref, nm_ref, nv_ref):
        mine = (pl.program_id(0) // 2) == c_ref[0]
        g = jnp.where(mine, own_ref[...], recv_ref[...])
        g_ref[...] = g
        d_ref[...], nm_ref[...], nv_ref[...] = _adam_math(w_ref[...], g, m_ref[...], v_ref[...])

    full = pl.BlockSpec((rb, width), lambda i, c: (i, 0))
    half = pl.BlockSpec((rb, width), lambda i, c: (i % 2, 0))
    return pl.pallas_call(
        body, name=f"adamw_halves_{rows}",
        grid_spec=pltpu.PrefetchScalarGridSpec(
            num_scalar_prefetch=1, grid=(4,),
            in_specs=[full, half, half, full, full], out_specs=[full] * 4),
        out_shape=[jax.ShapeDtypeStruct((rows, width), F32)] * 4,
        compiler_params=_params(1),
    )(cidx, w, own, recv, m, v)


def _rope_tables(positions):
    half = HEAD_DIM // 2
    inv_freq = ROPE_THETA ** (-jnp.arange(half, dtype=F32) * 2.0 / HEAD_DIM)
    ang = positions.astype(F32).reshape(-1, 1) * inv_freq
    cos, sin, zero = jnp.cos(ang), jnp.sin(ang), jnp.zeros_like(ang)
    return (jnp.concatenate([cos] * 4, axis=1), jnp.concatenate([-sin, zero] * 2, axis=1),
            jnp.concatenate([zero, sin] * 2, axis=1))


def _constants():
    idx = jnp.arange(512)
    g512 = jnp.where(idx[:, None] // HEAD_DIM == idx[None, :] // HEAD_DIM, 1.0 / HEAD_DIM, 0.0).astype(BF16)
    j = jnp.arange(SB_T)
    ublk = jnp.where(j[:, None] >= j[None, :], 1.0, 0.0).astype(BF16)
    pblk = jnp.where(j[:, None] < j[None, :], 1.0, 0.0).astype(BF16)
    return g512, jnp.concatenate([ublk, ublk], axis=0), pblk


def _pad_rows(v, width):
    return jnp.pad(v, ((0, 0), (0, width - v.shape[1])))


def _pair_sum(cidx, partial, rows):
    g4 = partial.reshape(4, 2, rows, D_MODEL)
    return _add_half(cidx, g4, _pair_exchange(g4))


def _step(x2, tgt, positions, norm_gain, q_norm_gain, k_norm_gain, sinks, w_bf, wout_shard, nbatch, seq):
    cos, sa, sb = _rope_tables(positions)
    g512, ublk, pblk = _constants()
    qg512 = jnp.tile(q_norm_gain, (1, 8))
    kg128 = jnp.tile(k_norm_gain, (1, 2))
    sink1 = sinks.reshape(8)
    cidx = lax.axis_index("c").astype(jnp.int32).reshape(1)
    chip = (2 * lax.axis_index("x") + lax.axis_index("y")).astype(jnp.int32).reshape(1)

    (h, qraw, kraw, qrot, k2, v2, ga, qb, kb, vb, gb), wout_bf = _norm_proj(
        x2, norm_gain, w_bf, qg512, kg128, g512, cos, sa, sb, wout_shard)
    oa = _swa_fwd(sink1, qrot, k2, v2, nbatch, seq)
    ob, wst, ost = _sb_fwd(qb, kb, vb, ublk, nbatch, seq)
    dout, doa, dob, dga, dgb, dwout, loss_acc = _out_proj(oa, ob, ga, gb, x2, tgt, wout_bf)

    cout4 = _pair_sum(cidx, dwout, SHARD_OUT // 2)
    dqrot, dk2, dv2, dsink, rout3 = _swa_bwd(sink1, qrot, k2, v2, doa, cout4, nbatch, seq)
    dqb, dkb, dvb = _sb_bwd(qb, kb, vb, dob, wst, ost, pblk, nbatch, seq)
    dk2, dv2 = _unshift(dk2, nbatch, seq), _unshift(dv2, nbatch, seq)
    dqa, dkv, dqg, dkg = _qk_grad(qraw, kraw, dqrot, dk2, dv2, qg512, kg128, g512, cos, sa, sb)
    pieces = (dqa, dkv, dga, dqb, dkb, dvb, dgb)
    cin4 = _pair_sum(cidx, _w_in_grad(h, pieces), SHARD_IN // 2)
    gx, dng, rin3 = _x_grad(x2, dout, pieces, w_bf, norm_gain, cin4)
    own_in, own_out = _sum_chips(chip, cin4, rin3), _sum_chips(chip, cout4, rout3)

    dqg64 = dqg.reshape(8, HEAD_DIM).sum(axis=0, keepdims=True)
    dkg64 = dkg.reshape(2, HEAD_DIM).sum(axis=0, keepdims=True)
    small = jnp.concatenate([dng, _pad_rows(dqg64, D_MODEL), _pad_rows(dkg64, D_MODEL),
                             _pad_rows(dsink[:, 0].reshape(1, 8), D_MODEL), _pad_rows(loss_acc[0:1, 0:1], D_MODEL),
                             jnp.zeros((3, D_MODEL), F32)], axis=0)
    sib_in, sib_out, small_all = _share_halves(own_in, own_out, small)
    return gx, cidx, (own_in, sib_in), (own_out, sib_out), _sum_slots(small_all)


def kernel(x, positions, norm_gain, w_in, q_norm_gain, k_norm_gain, sinks, w_out, loss_target, m_norm_gain, m_w_in, m_q_norm_gain, m_k_norm_gain, m_sinks, m_w_out, v_norm_gain, v_w_in, v_q_norm_gain, v_k_norm_gain, v_sinks, v_w_out):
    nbatch, seq, _ = x.shape
    T = nbatch * seq
    x2 = x.reshape(T, D_MODEL)
    tgt = loss_target.reshape(T, D_MODEL)
    tr = lambda a: jnp.swapaxes(a[0], 0, 1)
    win_t, m_win_t, v_win_t = tr(w_in), tr(m_w_in), tr(v_w_in)

    w_bf = _gather_weights(win_t.astype(BF16).reshape(2, SHARD_IN // 2, D_MODEL)).reshape(IN_WIDTH, D_MODEL)
    wout_shard = w_out[0].astype(BF16).reshape(2, SHARD_OUT // 2, D_MODEL)

    gx, cidx, g_in_halves, g_out_halves, g_small = _step(
        x2, tgt, positions, norm_gain, q_norm_gain, k_norm_gain, sinks, w_bf, wout_shard, nbatch, seq)
    loss = g_small[4, 0]

    g_in, d_in, nm_in, nv_in = [jnp.swapaxes(a, 0, 1) for a in
                                _adamw_halves(cidx, win_t, *g_in_halves, m_win_t, v_win_t)]
    g_out, d_out, nm_out, nv_out = _adamw_halves(cidx, w_out[0], *g_out_halves, m_w_out[0], v_w_out[0])
    pack = lambda a, b, c_, d: jnp.concatenate(
        [a, _pad_rows(b, D_MODEL), _pad_rows(c_, D_MODEL), _pad_rows(d, D_MODEL), jnp.zeros((4, D_MODEL), F32)], axis=0)
    w_s = pack(norm_gain, q_norm_gain, k_norm_gain, sinks)
    m_s = pack(m_norm_gain, m_q_norm_gain, m_k_norm_gain, m_sinks)
    v_s = pack(v_norm_gain, v_q_norm_gain, v_k_norm_gain, v_sinks)
    d_s, nm_s, nv_s = _adamw(w_s, g_small, m_s, v_s)
    unpack = lambda a: (a[0:1, :], a[1:2, 0:HEAD_DIM], a[2:3, 0:HEAD_DIM], a[3:4, 0:8])

    g_ng, g_qg, g_kg, g_sk = unpack(g_small)
    d_ng, d_qg, d_kg, d_sk = unpack(d_s)
    m_ng, m_qg, m_kg, m_sk = unpack(nm_s)
    v_ng, v_qg, v_kg, v_sk = unpack(nv_s)
    return (loss, gx.reshape(nbatch, seq, D_MODEL),
            g_ng, g_in[None], g_qg, g_kg, g_sk, g_out[None],
            d_ng, d_in[None], d_qg, d_kg, d_sk, d_out[None],
            m_ng, nm_in[None], m_qg, m_kg, m_sk, nm_out[None],
            v_ng, nv_in[None], v_qg, v_kg, v_sk, nv_out[None])
```

```python
import functools
import math

import jax
import jax.numpy as jnp
from jax import lax
from jax.experimental import pallas as pl
from jax.experimental.pallas import tpu as pltpu

F32 = jnp.float32
BF16 = jnp.bfloat16

D_MODEL = 1024
HEAD_DIM = 64
BLOCK = 128
ROPE_THETA = 10000.0
EPS = 1e-6
QA, KA, VA, GA, QB, KB, VB, GB = 0, 512, 640, 768, 1280, 1792, 2304, 2816
IN_WIDTH = 3328
SHARD_IN = IN_WIDTH // 4
SHARD_OUT = D_MODEL // 4
SCALE = 1.0 / math.sqrt(HEAD_DIM)
NEG = -1e30

ADAM_LR, ADAM_B1, ADAM_B2, ADAM_EPS, ADAM_WD, ADAM_STEP = 0.001, 0.9, 0.999, 1e-08, 0.01, 10

TM = 512
VMEM_LIMIT = 56 * 1024 * 1024
MESH = pl.DeviceIdType.MESH


def _dot(a, b):
    return jnp.dot(a, b, preferred_element_type=F32)


def _dot_nt(a, b):
    return lax.dot_general(a, b, (((1,), (1,)), ((), ())), preferred_element_type=F32)


def _dot_tn(a, b):
    return lax.dot_general(a, b, (((0,), (0,)), ((), ())), preferred_element_type=F32)


def _dot_hl(a, m):
    hi = a.astype(BF16)
    lo = (a - hi.astype(F32)).astype(BF16)
    return _dot(hi, m) + _dot(lo, m)


def _params(n_axes=None, vmem=VMEM_LIMIT):
    sem = None if n_axes is None else ("arbitrary",) * n_axes
    return pltpu.CompilerParams(dimension_semantics=sem, vmem_limit_bytes=vmem)


def _const_spec(shape):
    nd = len(shape)
    return pl.BlockSpec(shape, lambda *_: (0,) * nd)


def _rope_fwd(x, cos, sa, sb):
    return x * cos + pltpu.roll(x, 96, 1) * sa + pltpu.roll(x, 32, 1) * sb


def _rope_bwd(d, cos, sa, sb):
    return d * cos - pltpu.roll(d, 96, 1) * sa - pltpu.roll(d, 32, 1) * sb


def _lane_mask(rows, dtype=F32):
    lane = lax.broadcasted_iota(jnp.int32, (rows, 128), 1)
    return jnp.where(lane < HEAD_DIM, 1.0, 0.0).astype(dtype)


def _norm_proj(x2, ng, w_bf, qg512, kg128, g512, cos, sa, sb, wout_shard):
    T = x2.shape[0]
    steps = T // TM

    def body(x_ref, ng_ref, w_ref, qg_ref, kg_ref, g_ref, cos_ref, sa_ref, sb_ref, wo_ref,
             h_ref, qraw_ref, kraw_ref, qrot_ref, k2_ref, v2_ref, ga_ref, qb_ref, kb_ref, vb_ref, gb_ref, ao_ref,
             ssem, rsem):
        @pl.when(pl.program_id(0) == 0)
        def _():
            _gather_start(wo_ref, ao_ref, (ssem, rsem))

        xb = x_ref[...]
        r = lax.rsqrt(jnp.mean(xb * xb, axis=-1, keepdims=True) + EPS)
        h = (xb * r * ng_ref[...]).astype(BF16)
        h_ref[...] = h
        cosv, sav, sbv = cos_ref[...], sa_ref[...], sb_ref[...]
        m0 = _lane_mask(TM)

        def dup(v):
            v0 = v * m0
            v1 = v - v0
            return v0 + pltpu.roll(v0, 64, 1), v1 + pltpu.roll(v1, 64, 1)

        qa = _dot_nt(h, w_ref[QA:KA, :])
        qraw_ref[...] = qa
        qn = qa * lax.rsqrt(_dot_hl(qa * qa, g_ref[...]) + EPS) * qg_ref[...]
        for s in range(4):
            qs = _rope_fwd(qn[:, s * 128:(s + 1) * 128], cosv, sav, sbv)
            qrot_ref[:, s * 128:(s + 1) * 128] = (qs * SCALE).astype(BF16)
        ka = _dot_nt(h, w_ref[KA:VA, :])
        kraw_ref[...] = ka
        kn = ka * lax.rsqrt(_dot_hl(ka * ka, g_ref[0:128, 0:128]) + EPS) * kg_ref[...]
        k0, k1 = dup(_rope_fwd(kn, cosv, sav, sbv))
        k2_ref[:, 0:128] = k0.astype(BF16)
        k2_ref[:, 128:256] = k1.astype(BF16)
        v0, v1 = dup(_dot_nt(h, w_ref[VA:GA, :]))
        v2_ref[:, 0:128] = v0.astype(BF16)
        v2_ref[:, 128:256] = v1.astype(BF16)
        ga_ref[...] = _dot_nt(h, w_ref[GA:QB, :])
        qb_ref[...] = (_dot_nt(h, w_ref[QB:KB, :]) * SCALE).astype(BF16)
        kb_ref[...] = _dot_nt(h, w_ref[KB:VB, :]).astype(BF16)
        vb_ref[...] = _dot_nt(h, w_ref[VB:GB, :]).astype(BF16)
        gb_ref[...] = _dot_nt(h, w_ref[GB:IN_WIDTH, :])

        @pl.when(pl.program_id(0) == steps - 1)
        def _():
            _gather_finish(wo_ref, ao_ref, (ssem, rsem))

    def rows(w):
        return pl.BlockSpec((TM, w), lambda i: (i, 0))

    outs = [(D_MODEL, BF16), (512, F32), (128, F32), (512, BF16), (256, BF16), (256, BF16), (512, F32),
            (512, BF16), (512, BF16), (512, BF16), (512, F32)]
    hbm = pl.BlockSpec(memory_space=pl.ANY)
    res = pl.pallas_call(
        body, name="norm_proj", grid=(steps,),
        in_specs=[rows(D_MODEL), _const_spec((1, D_MODEL)), _const_spec((IN_WIDTH, D_MODEL)), _const_spec((1, 512)),
                  _const_spec((1, 128)), _const_spec((512, 512)), rows(128), rows(128), rows(128), hbm],
        out_specs=[rows(w) for w, _ in outs] + [hbm],
        out_shape=[jax.ShapeDtypeStruct((T, w), dt) for w, dt in outs]
        + [jax.ShapeDtypeStruct((4,) + wout_shard.shape, wout_shard.dtype)],
        scratch_shapes=[pltpu.SemaphoreType.DMA((6,)), pltpu.SemaphoreType.DMA((6,))],
        compiler_params=_params(1),
    )(x2, ng, w_bf, qg512, kg128, g512, cos, sa, sb, wout_shard)
    return res[:-1], _own_slot(res[-1], wout_shard).reshape(D_MODEL, D_MODEL)


SWA_Q = 512
SWA_SUB = SWA_Q // BLOCK


def _swa_scores(q, kst, mask, sink_ref, kv):
    s_all = _dot_nt(q, kst)
    first = lax.broadcasted_iota(jnp.int32, (256, 1), 0) < 128
    probs, stats = [], []
    for hh in range(2):
        sink = jnp.where(first, sink_ref[kv * 4 + hh], sink_ref[kv * 4 + 2 + hh])
        s = jnp.where(mask, s_all[:, hh * 256:(hh + 1) * 256], NEG)
        m = jnp.maximum(jnp.max(s, axis=1, keepdims=True), sink)
        pe = jnp.exp(s - m)
        inv = 1.0 / (jnp.sum(pe, axis=1, keepdims=True) + jnp.exp(sink - m))
        probs.append(pe * inv)
        stats.append(jnp.exp(sink - m) * inv)
    return probs, stats


def _swa_mask(has_prev):
    r = lax.broadcasted_iota(jnp.int32, (256, 256), 0) & 127
    c = lax.broadcasted_iota(jnp.int32, (256, 256), 1)
    band = (c > r) & (c <= r + 128)
    return band if has_prev is True else band & ((c >= 128) | has_prev)


def _stack_pairs(ref, rows, kv):
    return jnp.concatenate([ref[rows, (2 * kv) * 128:(2 * kv + 1) * 128], ref[rows, (2 * kv + 1) * 128:(2 * kv + 2) * 128]], axis=0)


def _swa_keys(prev_ref, main_ref, s, kv, m0b):
    cols = slice(kv * 128, (kv + 1) * 128)
    prev = prev_ref[:, cols] if s == 0 else main_ref[(s - 1) * 128:s * 128, cols]
    kc = jnp.concatenate([prev, main_ref[s * 128:(s + 1) * 128, cols]], axis=0)
    k0 = kc * m0b
    return jnp.concatenate([k0, kc - k0], axis=0)


def _swa_fwd(sinks, qrot, k2, v2, nbatch, seq):
    ni = seq // SWA_Q
    T = nbatch * seq

    def body(sink_ref, q_ref, kp_ref, km_ref, vp_ref, vm_ref, o_ref):
        i = pl.program_id(1)
        m0b = _lane_mask(256, BF16)
        for s in range(SWA_SUB):
            mask = _swa_mask(True if s else i > 0)
            rows = slice(s * 128, (s + 1) * 128)
            for kv in range(2):
                kst = _swa_keys(kp_ref, km_ref, s, kv, m0b)
                vst = _swa_keys(vp_ref, vm_ref, s, kv, m0b)
                probs, _ = _swa_scores(_stack_pairs(q_ref, rows, kv), kst, mask, sink_ref, kv)
                o2 = _dot(jnp.concatenate(probs, axis=1).astype(BF16), vst)
                o_ref[rows, kv * 256:kv * 256 + 128] = o2[0:128]
                o_ref[rows, kv * 256 + 128:(kv + 1) * 256] = o2[128:256]

    main = lambda b, i: (b * ni + i, 0)
    prev = lambda b, i: ((b * ni + i) * SWA_SUB - jnp.where(i > 0, 1, 0), 0)
    return pl.pallas_call(
        body, name="swa_fwd", grid=(nbatch, ni),
        in_specs=[pl.BlockSpec(memory_space=pltpu.SMEM), pl.BlockSpec((SWA_Q, 512), main),
                  pl.BlockSpec((128, 256), prev), pl.BlockSpec((SWA_Q, 256), main),
                  pl.BlockSpec((128, 256), prev), pl.BlockSpec((SWA_Q, 256), main)],
        out_specs=pl.BlockSpec((SWA_Q, 512), main),
        out_shape=jax.ShapeDtypeStruct((T, 512), F32),
        compiler_params=_params(2),
    )(sinks, qrot, k2, k2, v2, v2)


def _swa_bwd(sinks, qrot, k2, v2, doa, cout4, nbatch, seq):
    ni = seq // SWA_Q
    T = nbatch * seq

    def body(sink_ref, q_ref, kp_ref, km_ref, vp_ref, vm_ref, do_ref, cout_ref,
             dq_ref, dk_ref, dv_ref, ds_ref, rout_ref, dkc, dvc, ssem, rsem):
        b, i = pl.program_id(0), pl.program_id(1)
        copies = lambda: _chip_exchange(cout_ref, rout_ref, (ssem, rsem))

        @pl.when((b == 0) & (i == 0))
        def _():
            ds_ref[...] = jnp.zeros_like(ds_ref)
            for cp in copies():
                cp.start()

        @pl.when((b == nbatch - 1) & (i == ni))
        def _():
            for cp in copies():
                cp.wait_recv()
            for cp in copies():
                cp.wait_send()

        @pl.when(i == 0)
        def _():
            dkc[...] = jnp.zeros_like(dkc)
            dvc[...] = jnp.zeros_like(dvc)

        @pl.when(i < ni)
        def _():
            m0b = _lane_mask(256, BF16)
            m0 = _lane_mask(128) > 0.5
            for kv in range(2):
                kcols = slice(kv * 128, (kv + 1) * 128)
                dk_own, dv_own = dkc[:, kcols], dvc[:, kcols]
                for s in range(SWA_SUB):
                    mask = _swa_mask(True if s else i > 0)
                    rows = slice(s * 128, (s + 1) * 128)
                    kst = _swa_keys(kp_ref, km_ref, s, kv, m0b)
                    vst = _swa_keys(vp_ref, vm_ref, s, kv, m0b)
                    q, do = _stack_pairs(q_ref, rows, kv), _stack_pairs(do_ref, rows, kv)
                    probs, psink = _swa_scores(q, kst, mask, sink_ref, kv)
                    dp_all = _dot_nt(do, vst)
                    ds_parts = []
                    for hh in range(2):
                        dp = dp_all[:, hh * 256:(hh + 1) * 256]
                        delta = jnp.sum(probs[hh] * dp, axis=1, keepdims=True)
                        ds_parts.append(probs[hh] * (dp - delta))
                        dsink = psink[hh] * delta
                        for pr in range(2):
                            h = kv * 4 + pr * 2 + hh
                            ds_ref[h:h + 1, :] = ds_ref[h:h + 1, :] - jnp.sum(dsink[pr * 128:(pr + 1) * 128])
                    ds_all = jnp.concatenate(ds_parts, axis=1).astype(BF16)
                    p_all = jnp.concatenate(probs, axis=1).astype(BF16)
                    dq2 = _dot(ds_all, kst) * SCALE
                    dq_ref[rows, kv * 256:kv * 256 + 128] = dq2[0:128]
                    dq_ref[rows, kv * 256 + 128:(kv + 1) * 256] = dq2[128:256]
                    dkst = _dot_tn(ds_all, q)
                    dvst = _dot_tn(p_all, do)
                    dk_ref[rows, kcols] = dk_own + jnp.where(m0, dkst[0:128], dkst[256:384])
                    dv_ref[rows, kcols] = dv_own + jnp.where(m0, dvst[0:128], dvst[256:384])
                    dk_own = jnp.where(m0, dkst[128:256], dkst[384:512])
                    dv_own = jnp.where(m0, dvst[128:256], dvst[384:512])
                dkc[:, kcols] = dk_own
                dvc[:, kcols] = dv_own

        @pl.when(i == ni)
        def _():
            dk_ref[...] = jnp.zeros_like(dk_ref)
            dv_ref[...] = jnp.zeros_like(dv_ref)
            dk_ref[0:128, :] = dkc[...]
            dv_ref[0:128, :] = dvc[...]

    main = lambda b, i: (b * ni + jnp.minimum(i, ni - 1), 0)
    prev = lambda b, i: ((b * ni + jnp.minimum(i, ni - 1)) * SWA_SUB - jnp.where(jnp.minimum(i, ni - 1) > 0, 1, 0), 0)
    shifted = lambda b, i: (b * (ni + 1) + i, 0)
    tpad = nbatch * (ni + 1) * SWA_Q
    return pl.pallas_call(
        body, name="swa_bwd", grid=(nbatch, ni + 1),
        in_specs=[pl.BlockSpec(memory_space=pltpu.SMEM), pl.BlockSpec((SWA_Q, 512), main),
                  pl.BlockSpec((128, 256), prev), pl.BlockSpec((SWA_Q, 256), main),
                  pl.BlockSpec((128, 256), prev), pl.BlockSpec((SWA_Q, 256), main),
                  pl.BlockSpec((SWA_Q, 512), main), pl.BlockSpec(memory_space=pl.ANY)],
        out_specs=[pl.BlockSpec((SWA_Q, 512), main), pl.BlockSpec((SWA_Q, 256), shifted),
                   pl.BlockSpec((SWA_Q, 256), shifted), _const_spec((8, 128)), pl.BlockSpec(memory_space=pl.ANY)],
        out_shape=[jax.ShapeDtypeStruct((T, 512), F32), jax.ShapeDtypeStruct((tpad, 256), F32),
                   jax.ShapeDtypeStruct((tpad, 256), F32), jax.ShapeDtypeStruct((8, 128), F32),
                   jax.ShapeDtypeStruct((3,) + cout4.shape[1:], cout4.dtype)],
        scratch_shapes=[pltpu.VMEM((128, 256), F32), pltpu.VMEM((128, 256), F32),
                        pltpu.SemaphoreType.DMA((3,)), pltpu.SemaphoreType.DMA((3,))],
        compiler_params=_params(2),
    )(sinks, qrot, k2, k2, v2, v2, doa, cout4)


def _unshift(dkpad, nbatch, seq):
    return dkpad.reshape(nbatch, seq + SWA_Q, 256)[:, BLOCK:BLOCK + seq].reshape(nbatch * seq, 256)


SB_T = 256
SB_TQ = 2 * SB_T


def _sb_mask(kind):
    if kind == "full":
        return None
    rows = SB_T if kind == "B" else SB_TQ
    r = lax.broadcasted_iota(jnp.int32, (rows, 2 * SB_T), 0)
    c = lax.broadcasted_iota(jnp.int32, (rows, 2 * SB_T), 1)
    causal = (c & (SB_T - 1)) < r
    return causal | (r >= SB_T) if kind == "A" else causal


def _sb_rows(kind):
    return slice(SB_T, SB_TQ) if kind == "B" else slice(0, SB_TQ)


def _lower(x):
    return jnp.concatenate([jnp.zeros_like(x), x], axis=0)


def _sb_logits(neg_q, kst):
    nz = _dot_nt(neg_q, kst)
    sign = jnp.uint32(0x80000000)
    neg_abs = lax.bitcast_convert_type(lax.bitcast_convert_type(nz, jnp.uint32) | sign, F32)
    return nz, jnp.minimum(nz, 0.0) - jnp.log(1.0 + jnp.exp(neg_abs))


SB_NP = 4


def _pair_rows(ref, j, cols, m0b):
    kj = ref[pl.ds(pl.multiple_of(j * SB_T, SB_T), SB_T), cols]
    k0 = kj * m0b
    return jnp.concatenate([k0, kj - k0], axis=0)


def _bcast2(c0, c1):
    rows = c0.shape[0]
    return jnp.concatenate([jnp.broadcast_to(c0, (rows, SB_T)), jnp.broadcast_to(c1, (rows, SB_T))], axis=1)


def _rowsum2(x):
    return jnp.sum(x[:, 0:SB_T], axis=1, keepdims=True), jnp.sum(x[:, SB_T:2 * SB_T], axis=1, keepdims=True)


def _scan2(x, tri2):
    outs = []
    for h in range(2):
        xh = x[:, h * SB_T:(h + 1) * SB_T]
        hi = xh.astype(BF16)
        lo = (xh - hi.astype(F32)).astype(BF16)
        outs.append(_dot(jnp.concatenate([hi, lo], axis=1), tri2))
    return jnp.concatenate(outs, axis=1)


def _scan1(x, tri):
    xb = x.astype(BF16)
    return jnp.concatenate([_dot(xb[:, h * SB_T:(h + 1) * SB_T], tri) for h in range(2)], axis=1)


def _sb_fwd(qb, kb, vb, ublk, nbatch, seq):
    nq, nk = seq // SB_TQ, seq // SB_T
    T = nbatch * seq

    def body(q_ref, k_ref, v_ref, u_ref, o_ref, wst_ref, ost_ref, wbuf, obuf, sems):
        b, i = pl.program_id(0), pl.program_id(2)
        m0b = _lane_mask(SB_T, BF16)
        u = u_ref[...]
        pairs = [slice(pp * 128, (pp + 1) * 128) for pp in range(SB_NP)]
        neg_qs = [-q_ref[:, cols] for cols in pairs]

        def stores(j, slot):
            tix = (b * nq + i) * nk + j
            return (pltpu.make_async_copy(wbuf.at[slot], wst_ref.at[tix], sems.at[0, slot]),
                    pltpu.make_async_copy(obuf.at[slot], ost_ref.at[tix], sems.at[1, slot]))

        def tile(j, carries, kind, slot, wait):
            rows, mask = _sb_rows(kind), _sb_mask(kind)
            if wait:
                for cp in stores(j, slot):
                    cp.wait()
            out = []
            for pp, (cols, neg_q, (c0, c1, acc)) in enumerate(zip(pairs, neg_qs, carries)):
                kst = _pair_rows(k_ref, j, cols, m0b)
                vst = _pair_rows(v_ref, j, cols, m0b)
                nz, lb = _sb_logits(neg_q[rows], kst)
                if mask is not None:
                    lb = jnp.where(mask, lb, 0.0)
                incl = _scan2(lb, u)
                w = jnp.exp(incl - nz if kind == "B" else incl + _bcast2(c0, c1) - nz)
                if mask is not None:
                    w = jnp.where(mask, w, 0.0)
                wb = w.astype(BF16)
                wbuf[slot, pp, rows, :] = wb
                obuf[slot, pp, rows, :] = jnp.exp(lb).astype(BF16)
                if kind == "B":
                    wbuf[slot, pp, 0:SB_T, :] = jnp.zeros((SB_T, 2 * SB_T), BF16)
                    obuf[slot, pp, 0:SB_T, :] = jnp.zeros((SB_T, 2 * SB_T), BF16)
                d0, d1, da = incl[:, 0:1], incl[:, SB_T:SB_T + 1], _dot(wb, vst)
                if kind == "B":
                    d0, d1, da = _lower(d0), _lower(d1), _lower(da)
                out.append((c0 + d0, c1 + d1, acc + da))
            for cp in stores(j, slot):
                cp.start()
            return tuple(out)

        zc = jnp.zeros((SB_TQ, 1), F32)
        carries = tile(2 * i + 1, ((zc, zc, jnp.zeros((SB_TQ, 128), F32)),) * SB_NP, "B", 0, False)
        carries = tile(2 * i, carries, "A", 1, False)

        def two(jj, cr):
            cr = tile(2 * i - 1 - 2 * jj, cr, "full", 0, True)
            return tile(2 * i - 2 - 2 * jj, cr, "full", 1, True)

        carries = lax.fori_loop(0, i, two, carries)
        for cols, carry in zip(pairs, carries):
            o_ref[:, cols] = carry[2]
        for slot in range(2):
            for cp in stores(0, slot):
                cp.wait()

    wide = 128 * SB_NP
    blk = lambda b, g, i: (b * nq + i, g)
    full = lambda b, g, i: (b, g)
    hbm = pl.BlockSpec(memory_space=pl.ANY)
    tiles = jax.ShapeDtypeStruct((nbatch * nq * nk, SB_NP, SB_TQ, 2 * SB_T), BF16)
    return pl.pallas_call(
        body, name="sb_fwd", grid=(nbatch, 4 // SB_NP, nq),
        in_specs=[pl.BlockSpec((SB_TQ, wide), blk), pl.BlockSpec((seq, wide), full), pl.BlockSpec((seq, wide), full),
                  _const_spec((2 * SB_T, SB_T))],
        out_specs=[pl.BlockSpec((SB_TQ, wide), blk), hbm, hbm],
        out_shape=[jax.ShapeDtypeStruct((T, 512), F32), tiles, tiles],
        scratch_shapes=[pltpu.VMEM((2, SB_NP, SB_TQ, 2 * SB_T), BF16), pltpu.VMEM((2, SB_NP, SB_TQ, 2 * SB_T), BF16),
                        pltpu.SemaphoreType.DMA((2, 2))],
        compiler_params=_params(3),
    )(qb, kb, vb, ublk)


def _sb_bwd(qb, kb, vb, dob, wst, ost, pblk, nbatch, seq):
    nq, nk = seq // SB_TQ, seq // SB_T
    T = nbatch * seq

    def body(q_ref, k_ref, v_ref, do_ref, wst_ref, ost_ref, up_ref, dq_ref, dk_ref, dv_ref, wbuf, obuf, sems):
        b, i = pl.program_id(0), pl.program_id(2)

        @pl.when(i == 0)
        def _():
            dk_ref[...] = jnp.zeros_like(dk_ref)
            dv_ref[...] = jnp.zeros_like(dv_ref)

        m0b = _lane_mask(SB_T, BF16)
        m0 = _lane_mask(SB_T) > 0.5
        up = up_ref[...]
        pairs = [slice(pp * 128, (pp + 1) * 128) for pp in range(SB_NP)]

        def loads(j, slot):
            tix = (b * nq + i) * nk + j
            return (pltpu.make_async_copy(wst_ref.at[tix], wbuf.at[slot], sems.at[0, slot]),
                    pltpu.make_async_copy(ost_ref.at[tix], obuf.at[slot], sems.at[1, slot]))

        def tile(j, carries, kind, slot, fetch_next):
            rows, mask = _sb_rows(kind), _sb_mask(kind)
            if fetch_next:
                for cp in loads(j + 1, 1 - slot):
                    cp.start()
            for cp in loads(j, slot):
                cp.wait()
            out = []
            for pp, (cols, (s0, s1, dq)) in enumerate(zip(pairs, carries)):
                q, do = q_ref[rows, cols], do_ref[rows, cols]
                kst = _pair_rows(k_ref, j, cols, m0b)
                vst = _pair_rows(v_ref, j, cols, m0b)
                wb = wbuf[slot, pp, rows, :]
                e = _dot_nt(do, vst) * wb.astype(F32)
                dlb = _bcast2(s0[rows], s1[rows]) + _scan1(e, up)
                dz = (e + dlb) * obuf[slot, pp, rows, :].astype(F32) - dlb
                if mask is not None:
                    dz = jnp.where(mask, dz, 0.0)
                dzb = dz.astype(BF16)
                dkst = _dot_tn(dzb, q)
                dvst = _dot_tn(wb, do)
                keys = pl.ds(pl.multiple_of(j * SB_T, SB_T), SB_T)
                dk_ref[keys, cols] = dk_ref[keys, cols] + jnp.where(m0, dkst[0:SB_T], dkst[SB_T:2 * SB_T])
                dv_ref[keys, cols] = dv_ref[keys, cols] + jnp.where(m0, dvst[0:SB_T], dvst[SB_T:2 * SB_T])
                x0, x1 = _rowsum2(e)
                ddq = _dot(dzb, kst)
                if kind == "B":
                    x0, x1, ddq = _lower(x0), _lower(x1), _lower(ddq)
                out.append((s0 + x0, s1 + x1, dq + ddq))
            return tuple(out)

        for cp in loads(0, 0):
            cp.start()

        def two(jj, cr):
            cr = tile(2 * jj, cr, "full", 0, True)
            return tile(2 * jj + 1, cr, "full", 1, True)

        zc = jnp.zeros((SB_TQ, 1), F32)
        carries = lax.fori_loop(0, i, two, ((zc, zc, jnp.zeros((SB_TQ, 128), F32)),) * SB_NP)
        carries = tile(2 * i, carries, "A", 0, True)
        carries = tile(2 * i + 1, carries, "B", 1, False)
        for cols, carry in zip(pairs, carries):
            dq_ref[:, cols] = carry[2] * SCALE

    wide = 128 * SB_NP
    blk = lambda b, g, i: (b * nq + i, g)
    full = lambda b, g, i: (b, g)
    hbm = pl.BlockSpec(memory_space=pl.ANY)
    return pl.pallas_call(
        body, name="sb_bwd", grid=(nbatch, 4 // SB_NP, nq),
        in_specs=[pl.BlockSpec((SB_TQ, wide), blk), pl.BlockSpec((seq, wide), full), pl.BlockSpec((seq, wide), full),
                  pl.BlockSpec((SB_TQ, wide), blk), hbm, hbm, _const_spec((SB_T, SB_T))],
        out_specs=[pl.BlockSpec((SB_TQ, wide), blk), pl.BlockSpec((seq, wide), full), pl.BlockSpec((seq, wide), full)],
        out_shape=[jax.ShapeDtypeStruct((T, 512), F32)] * 3,
        scratch_shapes=[pltpu.VMEM((2, SB_NP, SB_TQ, 2 * SB_T), BF16), pltpu.VMEM((2, SB_NP, SB_TQ, 2 * SB_T), BF16),
                        pltpu.SemaphoreType.DMA((2, 2))],
        compiler_params=_params(3),
    )(qb, kb, vb, dob, wst, ost, pblk)


def _sigmoid(g):
    return 1.0 / (1.0 + jnp.exp(-g))


def _out_proj(oa, ob, ga, gb, x2, tgt, wout_bf):
    T = x2.shape[0]

    def body(oa_ref, ob_ref, ga_ref, gb_ref, x_ref, t_ref, w_ref,
             dout_ref, doa_ref, dob_ref, dga_ref, dgb_ref, dw_ref, loss_ref):
        @pl.when(pl.program_id(0) == 0)
        def _():
            loss_ref[...] = jnp.zeros_like(loss_ref)
            dw_ref[...] = jnp.zeros_like(dw_ref)

        halves = ((oa_ref, ga_ref, doa_ref, dga_ref, 0), (ob_ref, gb_ref, dob_ref, dgb_ref, 512))
        out = x_ref[...]
        gated = []
        for o_ref, g_ref, _, _, lo in halves:
            g = g_ref[...]
            sg = _sigmoid(g)
            y = (o_ref[...] * (g * sg)).astype(BF16)
            out = out + _dot(y, w_ref[lo:lo + 512, :])
            gated.append((y, g, sg))
        diff = out - t_ref[...]
        dout = diff * (1.0 / D_MODEL)
        dout_ref[...] = dout
        loss_ref[...] = loss_ref[...] + jnp.sum(diff * diff) * (0.5 / D_MODEL)
        db = dout.astype(BF16)
        for (o_ref, _, do_ref, dg_ref, lo), (y, g, sg) in zip(halves, gated):
            dw_ref[lo:lo + 512, :] = dw_ref[lo:lo + 512, :] + _dot_tn(y, db)
            dy = _dot_nt(db, w_ref[lo:lo + 512, :])
            do_ref[...] = (dy * (g * sg)).astype(BF16)
            dg_ref[...] = (dy * o_ref[...] * (sg * (1.0 + g * (1.0 - sg)))).astype(BF16)

    rows = lambda w: pl.BlockSpec((TM, w), lambda i: (i, 0))
    return pl.pallas_call(
        body, name="out_proj", grid=(T // TM,),
        in_specs=[rows(512), rows(512), rows(512), rows(512), rows(D_MODEL), rows(D_MODEL),
                  _const_spec((D_MODEL, D_MODEL))],
        out_specs=[rows(D_MODEL)] + [rows(512)] * 4 + [_const_spec((D_MODEL, D_MODEL)), _const_spec((8, 128))],
        out_shape=[jax.ShapeDtypeStruct((T, D_MODEL), F32)] + [jax.ShapeDtypeStruct((T, 512), BF16)] * 4
        + [jax.ShapeDtypeStruct((D_MODEL, D_MODEL), F32), jax.ShapeDtypeStruct((8, 128), F32)],
        compiler_params=_params(1),
    )(oa, ob, ga, gb, x2, tgt, wout_bf)


def _qk_grad(qraw, kraw, dqrot, dk2, dv2, qg512, kg128, g512, cos, sa, sb):
    T = qraw.shape[0]

    def body(qraw_ref, kraw_ref, dqrot_ref, dk2_ref, dv2_ref, qg_ref, kg_ref, g_ref, cos_ref, sa_ref, sb_ref,
             dqa_ref, dkv_ref, dqg_ref, dkg_ref):
        @pl.when(pl.program_id(0) == 0)
        def _():
            dqg_ref[...] = jnp.zeros_like(dqg_ref)
            dkg_ref[...] = jnp.zeros_like(dkg_ref)

        cosv, sav, sbv = cos_ref[...], sa_ref[...], sb_ref[...]
        m0 = _lane_mask(TM) > 0.5

        def head_norm_bwd(raw, dn_rot, gmat, gain):
            r = lax.rsqrt(_dot_hl(raw * raw, gmat) + EPS)
            n = raw * r
            dn = dn_rot * gain
            return r * (dn - n * _dot_hl(dn * n, gmat)), jnp.sum(dn_rot * n, axis=0, keepdims=True)

        def fold(ref):
            a, b = ref[:, 0:128], ref[:, 128:256]
            return jnp.where(m0, a + pltpu.roll(a, 64, 1), b + pltpu.roll(b, 64, 1))

        dqn = jnp.concatenate([_rope_bwd(dqrot_ref[:, s * 128:(s + 1) * 128], cosv, sav, sbv) for s in range(4)], axis=1)
        dqa, dqg = head_norm_bwd(qraw_ref[...], dqn, g_ref[...], qg_ref[...])
        dka, dkg = head_norm_bwd(kraw_ref[...], _rope_bwd(fold(dk2_ref), cosv, sav, sbv), g_ref[0:128, 0:128], kg_ref[...])
        dqa_ref[...] = dqa.astype(BF16)
        dkv_ref[:, 0:128] = dka.astype(BF16)
        dkv_ref[:, 128:256] = fold(dv2_ref).astype(BF16)
        dqg_ref[...] = dqg_ref[...] + dqg
        dkg_ref[...] = dkg_ref[...] + dkg

    rows = lambda w: pl.BlockSpec((TM, w), lambda i: (i, 0))
    return pl.pallas_call(
        body, name="qk_grad", grid=(T // TM,),
        in_specs=[rows(512), rows(128), rows(512), rows(256), rows(256), _const_spec((1, 512)), _const_spec((1, 128)),
                  _const_spec((512, 512)), rows(128), rows(128), rows(128)],
        out_specs=[rows(512), rows(256), _const_spec((1, 512)), _const_spec((1, 128))],
        out_shape=[jax.ShapeDtypeStruct((T, 512), BF16), jax.ShapeDtypeStruct((T, 256), BF16),
                   jax.ShapeDtypeStruct((1, 512), F32), jax.ShapeDtypeStruct((1, 128), F32)],
        compiler_params=_params(1),
    )(qraw, kraw, dqrot, dk2, dv2, qg512, kg128, g512, cos, sa, sb)


_PIECES = ((QA, 512), (KA, 256), (GA, 512), (QB, 512), (KB, 512), (VB, 512), (GB, 512))


def _w_in_grad(h, pieces):
    T = h.shape[0]

    def body(h_ref, *refs):
        dw_ref = refs[-1]

        @pl.when(pl.program_id(0) == 0)
        def _():
            dw_ref[...] = jnp.zeros_like(dw_ref)

        hb = h_ref[...]
        for (lo, width), p_ref in zip(_PIECES, refs[:-1]):
            dw_ref[lo:lo + width, :] = dw_ref[lo:lo + width, :] + _dot_tn(p_ref[...].astype(BF16), hb)

    rows = lambda w: pl.BlockSpec((TM, w), lambda i: (i, 0))
    return pl.pallas_call(
        body, name="w_in_grad", grid=(T // TM,),
        in_specs=[rows(D_MODEL)] + [rows(w) for _, w in _PIECES],
        out_specs=_const_spec((IN_WIDTH, D_MODEL)),
        out_shape=jax.ShapeDtypeStruct((IN_WIDTH, D_MODEL), F32),
        compiler_params=_params(1),
    )(h, *pieces)


def _chip_exchange(src_ref, dst_ref, sems):
    _, _, c, chips = _place()
    return [_remote(src_ref.at[2 * cx + cy], dst_ref.at[j], sems, j, (cx, cy, c)) for j, (cx, cy) in enumerate(chips)]


def _x_grad(x2, dout, pieces, w_bf, ng, cin4):
    T = x2.shape[0]
    npc = len(_PIECES)
    steps = T // TM

    def body(x_ref, dout_ref, *refs):
        w_ref, ng_ref, cin_ref, gx_ref, dng_ref, rin_ref, ssem, rsem = refs[npc:]
        step = pl.program_id(0)
        copies = lambda: _chip_exchange(cin_ref, rin_ref, (ssem, rsem))

        @pl.when(step == 0)
        def _():
            dng_ref[...] = jnp.zeros_like(dng_ref)
            for cp in copies():
                cp.start()

        dh = jnp.zeros((TM, D_MODEL), F32)
        for (lo, width), p_ref in zip(_PIECES, refs[:npc]):
            dh = dh + _dot(p_ref[...].astype(BF16), w_ref[lo:lo + width, :])
        xb = x_ref[...]
        r = lax.rsqrt(jnp.mean(xb * xb, axis=-1, keepdims=True) + EPS)
        n = xb * r
        dn = dh * ng_ref[...]
        gx_ref[...] = dout_ref[...] + r * (dn - n * jnp.mean(dn * n, axis=-1, keepdims=True))
        dng_ref[...] = dng_ref[...] + jnp.sum(dh * n, axis=0, keepdims=True)

        @pl.when(step == steps - 1)
        def _():
            for cp in copies():
                cp.wait_recv()
            for cp in copies():
                cp.wait_send()

    rows = lambda w: pl.BlockSpec((TM, w), lambda i: (i, 0))
    hbm = pl.BlockSpec(memory_space=pl.ANY)
    return pl.pallas_call(
        body, name="x_grad", grid=(steps,),
        in_specs=[rows(D_MODEL), rows(D_MODEL)] + [rows(w) for _, w in _PIECES]
        + [_const_spec((IN_WIDTH, D_MODEL)), _const_spec((1, D_MODEL)), hbm],
        out_specs=[rows(D_MODEL), _const_spec((1, D_MODEL)), hbm],
        out_shape=[jax.ShapeDtypeStruct((T, D_MODEL), F32), jax.ShapeDtypeStruct((1, D_MODEL), F32),
                   jax.ShapeDtypeStruct((3,) + cin4.shape[1:], cin4.dtype)],
        scratch_shapes=[pltpu.SemaphoreType.DMA((3,)), pltpu.SemaphoreType.DMA((3,))],
        compiler_params=_params(1),
    )(x2, dout, *pieces, w_bf, ng, cin4)


HBM = pl.BlockSpec(memory_space=pl.ANY)


def _place():
    x, y, c = lax.axis_index("x"), lax.axis_index("y"), lax.axis_index("c")
    chips = [(1 - x, y), (x, 1 - y), (1 - x, 1 - y)]
    return x, y, c, chips


def _remote(src, dst, sems, k, to):
    return pltpu.make_async_remote_copy(src_ref=src, dst_ref=dst, send_sem=sems[0].at[k], recv_sem=sems[1].at[k],
                                        device_id=to, device_id_type=MESH)


def _gather_start(src, dst, sems):
    x, y, c, chips = _place()
    for j, (cx, cy) in enumerate(chips):
        _remote(src.at[c], dst.at[2 * x + y, c], sems, j, (cx, cy, c)).start()


def _gather_finish(src, dst, sems):
    x, y, c, chips = _place()
    sib = (x, y, 1 - c)
    fwds = []
    for j, (cx, cy) in enumerate(chips):
        k = 2 * cx + cy
        _remote(src.at[c], dst.at[k, c], sems, j, sib).wait_recv()
        fwds.append(_remote(dst.at[k, c], dst.at[k, c], sems, 3 + j, sib))
        fwds[-1].start()
    for j, (cx, cy) in enumerate(chips):
        _remote(src.at[c], dst.at[2 * cx + cy, 1 - c], sems, 3 + j, sib).wait_recv()
    for j, (cx, cy) in enumerate(chips):
        _remote(src.at[c], dst.at[2 * x + y, c], sems, j, (cx, cy, c)).wait_send()
    for cp in fwds:
        cp.wait_send()


def _own_slot(gathered, shard):
    me = 2 * lax.axis_index("x") + lax.axis_index("y")
    return lax.dynamic_update_slice(gathered, shard[None], (me, 0, 0, 0))


def _gather_weights(shard):
    def body(src, dst, ssem, rsem):
        _gather_start(src, dst, (ssem, rsem))
        _gather_finish(src, dst, (ssem, rsem))

    gathered = pl.pallas_call(
        body, name="gather_weights", in_specs=[HBM], out_specs=HBM,
        out_shape=jax.ShapeDtypeStruct((4,) + shard.shape, shard.dtype),
        scratch_shapes=[pltpu.SemaphoreType.DMA((6,)), pltpu.SemaphoreType.DMA((6,))],
    )(shard)
    return _own_slot(gathered, shard)


def _pair_exchange(g4):
    def body(src, dst, ssem, rsem):
        x, y, c, _ = _place()
        cp = _remote(src.at[:, pl.ds(1 - c, 1)], dst, (ssem, rsem), 0, (x, y, 1 - c))
        cp.start()
        cp.wait()

    return pl.pallas_call(
        body, name=f"pair_exchange_{g4.shape[2]}", in_specs=[HBM], out_specs=HBM,
        out_shape=jax.ShapeDtypeStruct((4, 1) + g4.shape[2:], g4.dtype),
        scratch_shapes=[pltpu.SemaphoreType.DMA((1,)), pltpu.SemaphoreType.DMA((1,))],
    )(g4)


def _share_halves(hin, hout, small):
    def body(hin_ref, hout_ref, small_ref, oin_ref, oout_ref, sall_ref, ssem, rsem, lsem):
        x, y, c, _ = _place()
        sems = (ssem, rsem)
        sib = (x, y, 1 - c)
        me = 4 * x + 2 * y + c
        own = pltpu.make_async_copy(small_ref, sall_ref.at[me], lsem.at[0])
        own.start()
        sent = [_remote(hin_ref, oin_ref, sems, 0, sib), _remote(hout_ref, oout_ref, sems, 1, sib)]
        flips = [(fx, fy, fc) for fx in (0, 1) for fy in (0, 1) for fc in (0, 1)][1:]
        for k, (fx, fy, fc) in enumerate(flips):
            sent.append(_remote(small_ref, sall_ref.at[me], sems, 2 + k, (x ^ fx, y ^ fy, c ^ fc)))
        for cp in sent:
            cp.start()
        _remote(hin_ref, oin_ref, sems, 0, sib).wait_recv()
        _remote(hout_ref, oout_ref, sems, 1, sib).wait_recv()
        for k, (fx, fy, fc) in enumerate(flips):
            src = 4 * (x ^ fx) + 2 * (y ^ fy) + (c ^ fc)
            _remote(small_ref, sall_ref.at[src], sems, 2 + k, sib).wait_recv()
        for cp in sent:
            cp.wait_send()
        own.wait()

    return pl.pallas_call(
        body, name="share_halves", in_specs=[HBM, HBM, HBM], out_specs=[HBM, HBM, HBM],
        out_shape=[jax.ShapeDtypeStruct(hin.shape, F32), jax.ShapeDtypeStruct(hout.shape, F32),
                   jax.ShapeDtypeStruct((8,) + small.shape, F32)],
        scratch_shapes=[pltpu.SemaphoreType.DMA((9,)), pltpu.SemaphoreType.DMA((9,)), pltpu.SemaphoreType.DMA((1,))],
    )(hin, hout, small)


def _add_half(cidx, full4, recv4):
    _, _, rows, width = full4.shape

    def body(c_ref, a_ref, b_ref, o_ref):
        o_ref[0] = (a_ref[0, 0] + b_ref[0, 0]).astype(BF16)

    return pl.pallas_call(
        body, name=f"add_half_{rows}",
        grid_spec=pltpu.PrefetchScalarGridSpec(
            num_scalar_prefetch=1, grid=(4,),
            in_specs=[pl.BlockSpec((1, 1, rows, width), lambda k, c: (k, c[0], 0, 0)),
                      pl.BlockSpec((1, 1, rows, width), lambda k, c: (k, 0, 0, 0))],
            out_specs=pl.BlockSpec((1, rows, width), lambda k, c: (k, 0, 0))),
        out_shape=jax.ShapeDtypeStruct((4, rows, width), BF16),
        compiler_params=_params(1),
    )(cidx, full4, recv4)


def _sum_chips(chip, own4, recv3):
    _, rows, width = recv3.shape
    rb = rows // 2

    def body(k_ref, a_ref, r_ref, o_ref):
        acc = a_ref[0].astype(F32)
        for s in range(3):
            acc = acc + r_ref[s].astype(F32)
        o_ref[...] = acc

    return pl.pallas_call(
        body, name=f"sum_chips_{rows}",
        grid_spec=pltpu.PrefetchScalarGridSpec(
            num_scalar_prefetch=1, grid=(rows // rb,),
            in_specs=[pl.BlockSpec((1, rb, width), lambda i, k: (k[0], i, 0)),
                      pl.BlockSpec((3, rb, width), lambda i, k: (0, i, 0))],
            out_specs=pl.BlockSpec((rb, width), lambda i, k: (i, 0))),
        out_shape=jax.ShapeDtypeStruct((rows, width), F32),
        compiler_params=_params(1),
    )(chip, own4, recv3)


def _adam_math(w, g, m, v):
    c1 = 1.0 - ADAM_B1 ** ADAM_STEP
    c2 = 1.0 - ADAM_B2 ** ADAM_STEP
    nm = ADAM_B1 * m + (1.0 - ADAM_B1) * g
    nv = ADAM_B2 * v + (1.0 - ADAM_B2) * (g * g)
    return -ADAM_LR * ((nm / c1) / (jnp.sqrt(nv / c2) + ADAM_EPS) + ADAM_WD * w), nm, nv


def _small_update(small_all, params):
    n = len(params)
    flat = [a for p in params for a in p]

    def body(s_ref, *refs):
        ins, loss_ref, outs = refs[:3 * n], refs[3 * n], refs[3 * n + 1:]
        total = s_ref[0]
        for d in range(1, 8):
            total = total + s_ref[d]
        loss_ref[...] = total[n:n + 1, 0:1]
        for r in range(n):
            w_ref, m_ref, v_ref = ins[3 * r:3 * r + 3]
            g = total[r:r + 1, 0:w_ref.shape[1]]
            outs[4 * r][...] = g
            outs[4 * r + 1][...], outs[4 * r + 2][...], outs[4 * r + 3][...] = _adam_math(w_ref[...], g, m_ref[...], v_ref[...])

    whole = lambda a: pl.BlockSpec(a.shape, lambda i: (0,) * a.ndim)
    out_shape = [jax.ShapeDtypeStruct((1, 1), F32)] + [jax.ShapeDtypeStruct(p[0].shape, F32) for p in params for _ in range(4)]
    res = pl.pallas_call(
        body, name="small_update", grid=(1,),
        in_specs=[whole(small_all)] + [whole(a) for a in flat], out_specs=[whole(s) for s in out_shape],
        out_shape=out_shape, compiler_params=_params(1),
    )(small_all, *flat)
    return res[0], [tuple(res[1 + 4 * r:5 + 4 * r]) for r in range(n)]


def _adamw_halves(cidx, w, own, recv, m, v):
    rows, width = w.shape
    rb = rows // 4

    def body(c_ref, w_ref, own_ref, recv_ref, m_ref, v_ref, g_ref, d_ref, nm_ref, nv_ref):
        mine = (pl.program_id(0) // 2) == c_ref[0]
        g = jnp.where(mine, own_ref[...], recv_ref[...])
        g_ref[...] = g
        d_ref[...], nm_ref[...], nv_ref[...] = _adam_math(w_ref[...], g, m_ref[...], v_ref[...])

    full = pl.BlockSpec((rb, width), lambda i, c: (i, 0))
    half = pl.BlockSpec((rb, width), lambda i, c: (i % 2, 0))
    return pl.pallas_call(
        body, name=f"adamw_halves_{rows}",
        grid_spec=pltpu.PrefetchScalarGridSpec(
            num_scalar_prefetch=1, grid=(4,),
            in_specs=[full, half, half, full, full], out_specs=[full] * 4),
        out_shape=[jax.ShapeDtypeStruct((rows, width), F32)] * 4,
        compiler_params=_params(1),
    )(cidx, w, own, recv, m, v)


def _rope_tables(positions):
    half = HEAD_DIM // 2
    inv_freq = ROPE_THETA ** (-jnp.arange(half, dtype=F32) * 2.0 / HEAD_DIM)
    ang = positions.astype(F32).reshape(-1, 1) * inv_freq
    cos, sin, zero = jnp.cos(ang), jnp.sin(ang), jnp.zeros_like(ang)
    return (jnp.concatenate([cos] * 4, axis=1), jnp.concatenate([-sin, zero] * 2, axis=1),
            jnp.concatenate([zero, sin] * 2, axis=1))


def _constants():
    idx = jnp.arange(512)
    g512 = jnp.where(idx[:, None] // HEAD_DIM == idx[None, :] // HEAD_DIM, 1.0 / HEAD_DIM, 0.0).astype(BF16)
    j = jnp.arange(SB_T)
    ublk = jnp.where(j[:, None] >= j[None, :], 1.0, 0.0).astype(BF16)
    pblk = jnp.where(j[:, None] < j[None, :], 1.0, 0.0).astype(BF16)
    return g512, jnp.concatenate([ublk, ublk], axis=0), pblk


def _pad_rows(v, width):
    return jnp.pad(v, ((0, 0), (0, width - v.shape[1])))


def _pair_sum(cidx, partial, rows):
    g4 = partial.reshape(4, 2, rows, D_MODEL)
    return _add_half(cidx, g4, _pair_exchange(g4))


def _step(x2, tgt, positions, norm_gain, q_norm_gain, k_norm_gain, sinks, w_bf, wout_shard, nbatch, seq):
    cos, sa, sb = _rope_tables(positions)
    g512, ublk, pblk = _constants()
    qg512 = jnp.tile(q_norm_gain, (1, 8))
    kg128 = jnp.tile(k_norm_gain, (1, 2))
    sink1 = sinks.reshape(8)
    cidx = lax.axis_index("c").astype(jnp.int32).reshape(1)
    chip = (2 * lax.axis_index("x") + lax.axis_index("y")).astype(jnp.int32).reshape(1)

    (h, qraw, kraw, qrot, k2, v2, ga, qb, kb, vb, gb), wout_bf = _norm_proj(
        x2, norm_gain, w_bf, qg512, kg128, g512, cos, sa, sb, wout_shard)
    oa = _swa_fwd(sink1, qrot, k2, v2, nbatch, seq)
    ob, wst, ost = _sb_fwd(qb, kb, vb, ublk, nbatch, seq)
    dout, doa, dob, dga, dgb, dwout, loss_acc = _out_proj(oa, ob, ga, gb, x2, tgt, wout_bf)

    cout4 = _pair_sum(cidx, dwout, SHARD_OUT // 2)
    dqrot, dk2, dv2, dsink, rout3 = _swa_bwd(sink1, qrot, k2, v2, doa, cout4, nbatch, seq)
    dqb, dkb, dvb = _sb_bwd(qb, kb, vb, dob, wst, ost, pblk, nbatch, seq)
    dk2, dv2 = _unshift(dk2, nbatch, seq), _unshift(dv2, nbatch, seq)
    dqa, dkv, dqg, dkg = _qk_grad(qraw, kraw, dqrot, dk2, dv2, qg512, kg128, g512, cos, sa, sb)
    pieces = (dqa, dkv, dga, dqb, dkb, dvb, dgb)
    cin4 = _pair_sum(cidx, _w_in_grad(h, pieces), SHARD_IN // 2)
    gx, dng, rin3 = _x_grad(x2, dout, pieces, w_bf, norm_gain, cin4)
    own_in, own_out = _sum_chips(chip, cin4, rin3), _sum_chips(chip, cout4, rout3)

    dqg64 = dqg.reshape(8, HEAD_DIM).sum(axis=0, keepdims=True)
    dkg64 = dkg.reshape(2, HEAD_DIM).sum(axis=0, keepdims=True)
    small = jnp.concatenate([dng, _pad_rows(dqg64, D_MODEL), _pad_rows(dkg64, D_MODEL),
                             _pad_rows(dsink[:, 0].reshape(1, 8), D_MODEL), _pad_rows(loss_acc[0:1, 0:1], D_MODEL),
                             jnp.zeros((3, D_MODEL), F32)], axis=0)
    sib_in, sib_out, small_all = _share_halves(own_in, own_out, small)
    return gx, cidx, (own_in, sib_in), (own_out, sib_out), small_all


def kernel(x, positions, norm_gain, w_in, q_norm_gain, k_norm_gain, sinks, w_out, loss_target, m_norm_gain, m_w_in, m_q_norm_gain, m_k_norm_gain, m_sinks, m_w_out, v_norm_gain, v_w_in, v_q_norm_gain, v_k_norm_gain, v_sinks, v_w_out):
    nbatch, seq, _ = x.shape
    T = nbatch * seq
    x2 = x.reshape(T, D_MODEL)
    tgt = loss_target.reshape(T, D_MODEL)
    tr = lambda a: jnp.swapaxes(a[0], 0, 1)
    win_t, m_win_t, v_win_t = tr(w_in), tr(m_w_in), tr(v_w_in)

    w_bf = _gather_weights(win_t.astype(BF16).reshape(2, SHARD_IN // 2, D_MODEL)).reshape(IN_WIDTH, D_MODEL)
    wout_shard = w_out[0].astype(BF16).reshape(2, SHARD_OUT // 2, D_MODEL)

    gx, cidx, g_in_halves, g_out_halves, small_all = _step(
        x2, tgt, positions, norm_gain, q_norm_gain, k_norm_gain, sinks, w_bf, wout_shard, nbatch, seq)

    g_in, d_in, nm_in, nv_in = [jnp.swapaxes(a, 0, 1) for a in
                                _adamw_halves(cidx, win_t, *g_in_halves, m_win_t, v_win_t)]
    g_out, d_out, nm_out, nv_out = _adamw_halves(cidx, w_out[0], *g_out_halves, m_w_out[0], v_w_out[0])
    loss, small = _small_update(small_all, [(norm_gain, m_norm_gain, v_norm_gain), (q_norm_gain, m_q_norm_gain, v_q_norm_gain),
                                            (k_norm_gain, m_k_norm_gain, v_k_norm_gain), (sinks, m_sinks, v_sinks)])
    (g_ng, d_ng, m_ng, v_ng), (g_qg, d_qg, m_qg, v_qg), (g_kg, d_kg, m_kg, v_kg), (g_sk, d_sk, m_sk, v_sk) = small
    return (loss.reshape(()), gx.reshape(nbatch, seq, D_MODEL),
            g_ng, g_in[None], g_qg, g_kg, g_sk, g_out[None],
            d_ng, d_in[None], d_qg, d_kg, d_sk, d_out[None],
            m_ng, nm_in[None], m_qg, m_kg, m_sk, nm_out[None],
            v_ng, nv_in[None], v_qg, v_kg, v_sk, nv_out[None])
```

```python
import functools
import math

import jax
import jax.numpy as jnp
from jax import lax
from jax.experimental import pallas as pl
from jax.experimental.pallas import tpu as pltpu

F32 = jnp.float32
BF16 = jnp.bfloat16

D_MODEL = 1024
HEAD_DIM = 64
BLOCK = 128
ROPE_THETA = 10000.0
EPS = 1e-6
QA, KA, VA, GA, QB, KB, VB, GB = 0, 512, 640, 768, 1280, 1792, 2304, 2816
IN_WIDTH = 3328
SHARD_IN = IN_WIDTH // 4
SHARD_OUT = D_MODEL // 4
SCALE = 1.0 / math.sqrt(HEAD_DIM)
NEG = -1e30

ADAM_LR, ADAM_B1, ADAM_B2, ADAM_EPS, ADAM_WD, ADAM_STEP = 0.001, 0.9, 0.999, 1e-08, 0.01, 10

TM = 512
VMEM_LIMIT = 56 * 1024 * 1024
MESH = pl.DeviceIdType.MESH


def _dot(a, b):
    return jnp.dot(a, b, preferred_element_type=F32)


def _dot_nt(a, b):
    return lax.dot_general(a, b, (((1,), (1,)), ((), ())), preferred_element_type=F32)


def _dot_tn(a, b):
    return lax.dot_general(a, b, (((0,), (0,)), ((), ())), preferred_element_type=F32)


def _dot_hl(a, m):
    hi = a.astype(BF16)
    lo = (a - hi.astype(F32)).astype(BF16)
    return _dot(hi, m) + _dot(lo, m)


def _params(n_axes=None, vmem=VMEM_LIMIT):
    sem = None if n_axes is None else ("arbitrary",) * n_axes
    return pltpu.CompilerParams(dimension_semantics=sem, vmem_limit_bytes=vmem)


def _const_spec(shape):
    nd = len(shape)
    return pl.BlockSpec(shape, lambda *_: (0,) * nd)


def _rope_fwd(x, cos, sa, sb):
    return x * cos + pltpu.roll(x, 96, 1) * sa + pltpu.roll(x, 32, 1) * sb


def _rope_bwd(d, cos, sa, sb):
    return d * cos - pltpu.roll(d, 96, 1) * sa - pltpu.roll(d, 32, 1) * sb


def _lane_mask(rows, dtype=F32):
    lane = lax.broadcasted_iota(jnp.int32, (rows, 128), 1)
    return jnp.where(lane < HEAD_DIM, 1.0, 0.0).astype(dtype)


def _norm_proj(x2, ng, w_bf, qg512, kg128, g512, pos, wout_shard):
    T = x2.shape[0]
    steps = T // TM
    half = HEAD_DIM // 2
    inv_freq = jnp.tile(ROPE_THETA ** (-jnp.arange(half, dtype=F32) * 2.0 / HEAD_DIM), 4).reshape(1, 128)

    def body(x_ref, ng_ref, w_ref, qg_ref, kg_ref, g_ref, pos_ref, freq_ref, wo_ref,
             h_ref, qraw_ref, kraw_ref, qrot_ref, k2_ref, v2_ref, ga_ref, qb_ref, kb_ref, vb_ref, gb_ref,
             cos_ref, sa_ref, sb_ref, ao_ref, ssem, rsem):
        @pl.when(pl.program_id(0) == 0)
        def _():
            _gather_start(wo_ref, ao_ref, (ssem, rsem))

        xb = x_ref[...]
        r = lax.rsqrt(jnp.mean(xb * xb, axis=-1, keepdims=True) + EPS)
        h = (xb * r * ng_ref[...]).astype(BF16)
        h_ref[...] = h
        ang = pos_ref[...].astype(F32) * freq_ref[...]
        first = (lax.broadcasted_iota(jnp.int32, (TM, 128), 1) & (HEAD_DIM - 1)) < half
        cosv, sn = jnp.cos(ang), jnp.sin(ang)
        sav, sbv = jnp.where(first, -sn, 0.0), jnp.where(first, 0.0, sn)
        cos_ref[...], sa_ref[...], sb_ref[...] = cosv, sav, sbv
        m0 = _lane_mask(TM)

        def dup(v):
            v0 = v * m0
            v1 = v - v0
            return v0 + pltpu.roll(v0, 64, 1), v1 + pltpu.roll(v1, 64, 1)

        qa = _dot_nt(h, w_ref[QA:KA, :])
        qraw_ref[...] = qa
        qn = qa * lax.rsqrt(_dot_hl(qa * qa, g_ref[...]) + EPS) * qg_ref[...]
        for s in range(4):
            qs = _rope_fwd(qn[:, s * 128:(s + 1) * 128], cosv, sav, sbv)
            qrot_ref[:, s * 128:(s + 1) * 128] = (qs * SCALE).astype(BF16)
        ka = _dot_nt(h, w_ref[KA:VA, :])
        kraw_ref[...] = ka
        kn = ka * lax.rsqrt(_dot_hl(ka * ka, g_ref[0:128, 0:128]) + EPS) * kg_ref[...]
        k0, k1 = dup(_rope_fwd(kn, cosv, sav, sbv))
        k2_ref[:, 0:128] = k0.astype(BF16)
        k2_ref[:, 128:256] = k1.astype(BF16)
        v0, v1 = dup(_dot_nt(h, w_ref[VA:GA, :]))
        v2_ref[:, 0:128] = v0.astype(BF16)
        v2_ref[:, 128:256] = v1.astype(BF16)
        ga_ref[...] = _dot_nt(h, w_ref[GA:QB, :])
        qb_ref[...] = (_dot_nt(h, w_ref[QB:KB, :]) * SCALE).astype(BF16)
        kb_ref[...] = _dot_nt(h, w_ref[KB:VB, :]).astype(BF16)
        vb_ref[...] = _dot_nt(h, w_ref[VB:GB, :]).astype(BF16)
        gb_ref[...] = _dot_nt(h, w_ref[GB:IN_WIDTH, :])

        @pl.when(pl.program_id(0) == steps - 1)
        def _():
            _gather_finish(wo_ref, ao_ref, (ssem, rsem))

    def rows(w):
        return pl.BlockSpec((TM, w), lambda i: (i, 0))

    outs = [(D_MODEL, BF16), (512, F32), (128, F32), (512, BF16), (256, BF16), (256, BF16), (512, F32),
            (512, BF16), (512, BF16), (512, BF16), (512, F32), (128, F32), (128, F32), (128, F32)]
    hbm = pl.BlockSpec(memory_space=pl.ANY)
    res = pl.pallas_call(
        body, name="norm_proj", grid=(steps,),
        in_specs=[rows(D_MODEL), _const_spec((1, D_MODEL)), _const_spec((IN_WIDTH, D_MODEL)), _const_spec((1, 512)),
                  _const_spec((1, 128)), _const_spec((512, 512)), rows(1), _const_spec((1, 128)), hbm],
        out_specs=[rows(w) for w, _ in outs] + [hbm],
        out_shape=[jax.ShapeDtypeStruct((T, w), dt) for w, dt in outs]
        + [jax.ShapeDtypeStruct((4,) + wout_shard.shape, wout_shard.dtype)],
        scratch_shapes=[pltpu.SemaphoreType.DMA((6,)), pltpu.SemaphoreType.DMA((6,))],
        compiler_params=_params(1),
    )(x2, ng, w_bf, qg512, kg128, g512, pos, inv_freq, wout_shard)
    return res[:-4], res[-4:-1], _own_slot(res[-1], wout_shard).reshape(D_MODEL, D_MODEL)


SWA_Q = 512
SWA_SUB = SWA_Q // BLOCK


def _swa_scores(q, kst, mask, sink_ref, kv):
    s_all = _dot_nt(q, kst)
    first = lax.broadcasted_iota(jnp.int32, (256, 1), 0) < 128
    probs, stats = [], []
    for hh in range(2):
        sink = jnp.where(first, sink_ref[kv * 4 + hh], sink_ref[kv * 4 + 2 + hh])
        s = jnp.where(mask, s_all[:, hh * 256:(hh + 1) * 256], NEG)
        m = jnp.maximum(jnp.max(s, axis=1, keepdims=True), sink)
        pe = jnp.exp(s - m)
        inv = 1.0 / (jnp.sum(pe, axis=1, keepdims=True) + jnp.exp(sink - m))
        probs.append(pe * inv)
        stats.append(jnp.exp(sink - m) * inv)
    return probs, stats


def _swa_mask(has_prev):
    r = lax.broadcasted_iota(jnp.int32, (256, 256), 0) & 127
    c = lax.broadcasted_iota(jnp.int32, (256, 256), 1)
    band = (c > r) & (c <= r + 128)
    return band if has_prev is True else band & ((c >= 128) | has_prev)


def _stack_pairs(ref, rows, kv):
    return jnp.concatenate([ref[rows, (2 * kv) * 128:(2 * kv + 1) * 128], ref[rows, (2 * kv + 1) * 128:(2 * kv + 2) * 128]], axis=0)


def _swa_keys(prev_ref, main_ref, s, kv, m0b):
    cols = slice(kv * 128, (kv + 1) * 128)
    prev = prev_ref[:, cols] if s == 0 else main_ref[(s - 1) * 128:s * 128, cols]
    kc = jnp.concatenate([prev, main_ref[s * 128:(s + 1) * 128, cols]], axis=0)
    k0 = kc * m0b
    return jnp.concatenate([k0, kc - k0], axis=0)


def _swa_fwd(sinks, qrot, k2, v2, nbatch, seq):
    ni = seq // SWA_Q
    T = nbatch * seq

    def body(sink_ref, q_ref, kp_ref, km_ref, vp_ref, vm_ref, o_ref):
        i = pl.program_id(1)
        m0b = _lane_mask(256, BF16)
        for s in range(SWA_SUB):
            mask = _swa_mask(True if s else i > 0)
            rows = slice(s * 128, (s + 1) * 128)
            for kv in range(2):
                kst = _swa_keys(kp_ref, km_ref, s, kv, m0b)
                vst = _swa_keys(vp_ref, vm_ref, s, kv, m0b)
                probs, _ = _swa_scores(_stack_pairs(q_ref, rows, kv), kst, mask, sink_ref, kv)
                o2 = _dot(jnp.concatenate(probs, axis=1).astype(BF16), vst)
                o_ref[rows, kv * 256:kv * 256 + 128] = o2[0:128]
                o_ref[rows, kv * 256 + 128:(kv + 1) * 256] = o2[128:256]

    main = lambda b, i: (b * ni + i, 0)
    prev = lambda b, i: ((b * ni + i) * SWA_SUB - jnp.where(i > 0, 1, 0), 0)
    return pl.pallas_call(
        body, name="swa_fwd", grid=(nbatch, ni),
        in_specs=[pl.BlockSpec(memory_space=pltpu.SMEM), pl.BlockSpec((SWA_Q, 512), main),
                  pl.BlockSpec((128, 256), prev), pl.BlockSpec((SWA_Q, 256), main),
                  pl.BlockSpec((128, 256), prev), pl.BlockSpec((SWA_Q, 256), main)],
        out_specs=pl.BlockSpec((SWA_Q, 512), main),
        out_shape=jax.ShapeDtypeStruct((T, 512), F32),
        compiler_params=_params(2),
    )(sinks, qrot, k2, k2, v2, v2)


def _swa_bwd(sinks, qrot, k2, v2, doa, cout4, nbatch, seq):
    ni = seq // SWA_Q
    T = nbatch * seq

    def body(sink_ref, q_ref, kp_ref, km_ref, vp_ref, vm_ref, do_ref, cout_ref,
             dq_ref, dk_ref, dv_ref, ds_ref, rout_ref, dkc, dvc, ssem, rsem):
        b, i = pl.program_id(0), pl.program_id(1)
        copies = lambda: _chip_exchange(cout_ref, rout_ref, (ssem, rsem))

        @pl.when((b == 0) & (i == 0))
        def _():
            ds_ref[...] = jnp.zeros_like(ds_ref)
            for cp in copies():
                cp.start()

        @pl.when((b == nbatch - 1) & (i == ni))
        def _():
            for cp in copies():
                cp.wait_recv()
            for cp in copies():
                cp.wait_send()

        @pl.when(i == 0)
        def _():
            dkc[...] = jnp.zeros_like(dkc)
            dvc[...] = jnp.zeros_like(dvc)

        @pl.when(i < ni)
        def _():
            m0b = _lane_mask(256, BF16)
            m0 = _lane_mask(128) > 0.5
            for kv in range(2):
                kcols = slice(kv * 128, (kv + 1) * 128)
                dk_own, dv_own = dkc[:, kcols], dvc[:, kcols]
                for s in range(SWA_SUB):
                    mask = _swa_mask(True if s else i > 0)
                    rows = slice(s * 128, (s + 1) * 128)
                    kst = _swa_keys(kp_ref, km_ref, s, kv, m0b)
                    vst = _swa_keys(vp_ref, vm_ref, s, kv, m0b)
                    q, do = _stack_pairs(q_ref, rows, kv), _stack_pairs(do_ref, rows, kv)
                    probs, psink = _swa_scores(q, kst, mask, sink_ref, kv)
                    dp_all = _dot_nt(do, vst)
                    ds_parts = []
                    for hh in range(2):
                        dp = dp_all[:, hh * 256:(hh + 1) * 256]
                        delta = jnp.sum(probs[hh] * dp, axis=1, keepdims=True)
                        ds_parts.append(probs[hh] * (dp - delta))
                        dsink = psink[hh] * delta
                        for pr in range(2):
                            h = kv * 4 + pr * 2 + hh
                            ds_ref[h:h + 1, :] = ds_ref[h:h + 1, :] - jnp.sum(dsink[pr * 128:(pr + 1) * 128])
                    ds_all = jnp.concatenate(ds_parts, axis=1).astype(BF16)
                    p_all = jnp.concatenate(probs, axis=1).astype(BF16)
                    dq2 = _dot(ds_all, kst) * SCALE
                    dq_ref[rows, kv * 256:kv * 256 + 128] = dq2[0:128]
                    dq_ref[rows, kv * 256 + 128:(kv + 1) * 256] = dq2[128:256]
                    dkst = _dot_tn(ds_all, q)
                    dvst = _dot_tn(p_all, do)
                    dk_ref[rows, kcols] = dk_own + jnp.where(m0, dkst[0:128], dkst[256:384])
                    dv_ref[rows, kcols] = dv_own + jnp.where(m0, dvst[0:128], dvst[256:384])
                    dk_own = jnp.where(m0, dkst[128:256], dkst[384:512])
                    dv_own = jnp.where(m0, dvst[128:256], dvst[384:512])
                dkc[:, kcols] = dk_own
                dvc[:, kcols] = dv_own

        @pl.when(i == ni)
        def _():
            dk_ref[...] = jnp.zeros_like(dk_ref)
            dv_ref[...] = jnp.zeros_like(dv_ref)
            dk_ref[0:128, :] = dkc[...]
            dv_ref[0:128, :] = dvc[...]

    main = lambda b, i: (b * ni + jnp.minimum(i, ni - 1), 0)
    prev = lambda b, i: ((b * ni + jnp.minimum(i, ni - 1)) * SWA_SUB - jnp.where(jnp.minimum(i, ni - 1) > 0, 1, 0), 0)
    shifted = lambda b, i: (b * (ni + 1) + i, 0)
    tpad = nbatch * (ni + 1) * SWA_Q
    return pl.pallas_call(
        body, name="swa_bwd", grid=(nbatch, ni + 1),
        in_specs=[pl.BlockSpec(memory_space=pltpu.SMEM), pl.BlockSpec((SWA_Q, 512), main),
                  pl.BlockSpec((128, 256), prev), pl.BlockSpec((SWA_Q, 256), main),
                  pl.BlockSpec((128, 256), prev), pl.BlockSpec((SWA_Q, 256), main),
                  pl.BlockSpec((SWA_Q, 512), main), pl.BlockSpec(memory_space=pl.ANY)],
        out_specs=[pl.BlockSpec((SWA_Q, 512), main), pl.BlockSpec((SWA_Q, 256), shifted),
                   pl.BlockSpec((SWA_Q, 256), shifted), _const_spec((8, 128)), pl.BlockSpec(memory_space=pl.ANY)],
        out_shape=[jax.ShapeDtypeStruct((T, 512), F32), jax.ShapeDtypeStruct((tpad, 256), F32),
                   jax.ShapeDtypeStruct((tpad, 256), F32), jax.ShapeDtypeStruct((8, 128), F32),
                   jax.ShapeDtypeStruct((3,) + cout4.shape[1:], cout4.dtype)],
        scratch_shapes=[pltpu.VMEM((128, 256), F32), pltpu.VMEM((128, 256), F32),
                        pltpu.SemaphoreType.DMA((3,)), pltpu.SemaphoreType.DMA((3,))],
        compiler_params=_params(2),
    )(sinks, qrot, k2, k2, v2, v2, doa, cout4)


def _unshift(dkpad, nbatch, seq):
    return dkpad.reshape(nbatch, seq + SWA_Q, 256)[:, BLOCK:BLOCK + seq].reshape(nbatch * seq, 256)


SB_T = 256
SB_TQ = 2 * SB_T


def _sb_mask(kind):
    if kind == "full":
        return None
    rows = SB_T if kind == "B" else SB_TQ
    r = lax.broadcasted_iota(jnp.int32, (rows, 2 * SB_T), 0)
    c = lax.broadcasted_iota(jnp.int32, (rows, 2 * SB_T), 1)
    causal = (c & (SB_T - 1)) < r
    return causal | (r >= SB_T) if kind == "A" else causal


def _sb_rows(kind):
    return slice(SB_T, SB_TQ) if kind == "B" else slice(0, SB_TQ)


def _lower(x):
    return jnp.concatenate([jnp.zeros_like(x), x], axis=0)


def _sb_logits(neg_q, kst):
    nz = _dot_nt(neg_q, kst)
    sign = jnp.uint32(0x80000000)
    neg_abs = lax.bitcast_convert_type(lax.bitcast_convert_type(nz, jnp.uint32) | sign, F32)
    return nz, jnp.minimum(nz, 0.0) - jnp.log(1.0 + jnp.exp(neg_abs))


SB_NP = 4


def _pair_rows(ref, j, cols, m0b):
    kj = ref[pl.ds(pl.multiple_of(j * SB_T, SB_T), SB_T), cols]
    k0 = kj * m0b
    return jnp.concatenate([k0, kj - k0], axis=0)


def _bcast2(c0, c1):
    rows = c0.shape[0]
    return jnp.concatenate([jnp.broadcast_to(c0, (rows, SB_T)), jnp.broadcast_to(c1, (rows, SB_T))], axis=1)


def _rowsum2(x):
    return jnp.sum(x[:, 0:SB_T], axis=1, keepdims=True), jnp.sum(x[:, SB_T:2 * SB_T], axis=1, keepdims=True)


def _scan2(x, tri2):
    outs = []
    for h in range(2):
        xh = x[:, h * SB_T:(h + 1) * SB_T]
        hi = xh.astype(BF16)
        lo = (xh - hi.astype(F32)).astype(BF16)
        outs.append(_dot(jnp.concatenate([hi, lo], axis=1), tri2))
    return jnp.concatenate(outs, axis=1)


def _scan1(x, tri):
    xb = x.astype(BF16)
    return jnp.concatenate([_dot(xb[:, h * SB_T:(h + 1) * SB_T], tri) for h in range(2)], axis=1)


def _sb_fwd(qb, kb, vb, ublk, nbatch, seq):
    nq, nk = seq // SB_TQ, seq // SB_T
    T = nbatch * seq

    def body(q_ref, k_ref, v_ref, u_ref, o_ref, wst_ref, ost_ref, wbuf, obuf, sems):
        b, i = pl.program_id(0), pl.program_id(2)
        m0b = _lane_mask(SB_T, BF16)
        u = u_ref[...]
        pairs = [slice(pp * 128, (pp + 1) * 128) for pp in range(SB_NP)]
        neg_qs = [-q_ref[:, cols] for cols in pairs]

        def stores(j, slot):
            tix = (b * nq + i) * nk + j
            return (pltpu.make_async_copy(wbuf.at[slot], wst_ref.at[tix], sems.at[0, slot]),
                    pltpu.make_async_copy(obuf.at[slot], ost_ref.at[tix], sems.at[1, slot]))

        def tile(j, carries, kind, slot, wait):
            rows, mask = _sb_rows(kind), _sb_mask(kind)
            if wait:
                for cp in stores(j, slot):
                    cp.wait()
            out = []
            for pp, (cols, neg_q, (c0, c1, acc)) in enumerate(zip(pairs, neg_qs, carries)):
                kst = _pair_rows(k_ref, j, cols, m0b)
                vst = _pair_rows(v_ref, j, cols, m0b)
                nz, lb = _sb_logits(neg_q[rows], kst)
                if mask is not None:
                    lb = jnp.where(mask, lb, 0.0)
                incl = _scan2(lb, u)
                w = jnp.exp(incl - nz if kind == "B" else incl + _bcast2(c0, c1) - nz)
                if mask is not None:
                    w = jnp.where(mask, w, 0.0)
                wb = w.astype(BF16)
                wbuf[slot, pp, rows, :] = wb
                obuf[slot, pp, rows, :] = jnp.exp(lb).astype(BF16)
                if kind == "B":
                    wbuf[slot, pp, 0:SB_T, :] = jnp.zeros((SB_T, 2 * SB_T), BF16)
                    obuf[slot, pp, 0:SB_T, :] = jnp.zeros((SB_T, 2 * SB_T), BF16)
                d0, d1, da = incl[:, 0:1], incl[:, SB_T:SB_T + 1], _dot(wb, vst)
                if kind == "B":
                    d0, d1, da = _lower(d0), _lower(d1), _lower(da)
                out.append((c0 + d0, c1 + d1, acc + da))
            for cp in stores(j, slot):
                cp.start()
            return tuple(out)

        zc = jnp.zeros((SB_TQ, 1), F32)
        carries = tile(2 * i + 1, ((zc, zc, jnp.zeros((SB_TQ, 128), F32)),) * SB_NP, "B", 0, False)
        carries = tile(2 * i, carries, "A", 1, False)

        def two(jj, cr):
            cr = tile(2 * i - 1 - 2 * jj, cr, "full", 0, True)
            return tile(2 * i - 2 - 2 * jj, cr, "full", 1, True)

        carries = lax.fori_loop(0, i, two, carries)
        for cols, carry in zip(pairs, carries):
            o_ref[:, cols] = carry[2]
        for slot in range(2):
            for cp in stores(0, slot):
                cp.wait()

    wide = 128 * SB_NP
    blk = lambda b, g, i: (b * nq + i, g)
    full = lambda b, g, i: (b, g)
    hbm = pl.BlockSpec(memory_space=pl.ANY)
    tiles = jax.ShapeDtypeStruct((nbatch * nq * nk, SB_NP, SB_TQ, 2 * SB_T), BF16)
    return pl.pallas_call(
        body, name="sb_fwd", grid=(nbatch, 4 // SB_NP, nq),
        in_specs=[pl.BlockSpec((SB_TQ, wide), blk), pl.BlockSpec((seq, wide), full), pl.BlockSpec((seq, wide), full),
                  _const_spec((2 * SB_T, SB_T))],
        out_specs=[pl.BlockSpec((SB_TQ, wide), blk), hbm, hbm],
        out_shape=[jax.ShapeDtypeStruct((T, 512), F32), tiles, tiles],
        scratch_shapes=[pltpu.VMEM((2, SB_NP, SB_TQ, 2 * SB_T), BF16), pltpu.VMEM((2, SB_NP, SB_TQ, 2 * SB_T), BF16),
                        pltpu.SemaphoreType.DMA((2, 2))],
        compiler_params=_params(3),
    )(qb, kb, vb, ublk)


def _sb_bwd(qb, kb, vb, dob, wst, ost, pblk, nbatch, seq):
    nq, nk = seq // SB_TQ, seq // SB_T
    T = nbatch * seq

    def body(q_ref, k_ref, v_ref, do_ref, wst_ref, ost_ref, up_ref, dq_ref, dk_ref, dv_ref, wbuf, obuf, sems):
        b, i = pl.program_id(0), pl.program_id(2)

        @pl.when(i == 0)
        def _():
            dk_ref[...] = jnp.zeros_like(dk_ref)
            dv_ref[...] = jnp.zeros_like(dv_ref)

        m0b = _lane_mask(SB_T, BF16)
        m0 = _lane_mask(SB_T) > 0.5
        up = up_ref[...]
        pairs = [slice(pp * 128, (pp + 1) * 128) for pp in range(SB_NP)]

        def loads(j, slot):
            tix = (b * nq + i) * nk + j
            return (pltpu.make_async_copy(wst_ref.at[tix], wbuf.at[slot], sems.at[0, slot]),
                    pltpu.make_async_copy(ost_ref.at[tix], obuf.at[slot], sems.at[1, slot]))

        def tile(j, carries, kind, slot, fetch_next):
            rows, mask = _sb_rows(kind), _sb_mask(kind)
            if fetch_next:
                for cp in loads(j + 1, 1 - slot):
                    cp.start()
            for cp in loads(j, slot):
                cp.wait()
            out = []
            for pp, (cols, (s0, s1, dq)) in enumerate(zip(pairs, carries)):
                q, do = q_ref[rows, cols], do_ref[rows, cols]
                kst = _pair_rows(k_ref, j, cols, m0b)
                vst = _pair_rows(v_ref, j, cols, m0b)
                wb = wbuf[slot, pp, rows, :]
                e = _dot_nt(do, vst) * wb.astype(F32)
                dlb = _bcast2(s0[rows], s1[rows]) + _scan1(e, up)
                dz = (e + dlb) * obuf[slot, pp, rows, :].astype(F32) - dlb
                if mask is not None:
                    dz = jnp.where(mask, dz, 0.0)
                dzb = dz.astype(BF16)
                dkst = _dot_tn(dzb, q)
                dvst = _dot_tn(wb, do)
                keys = pl.ds(pl.multiple_of(j * SB_T, SB_T), SB_T)
                dk_ref[keys, cols] = dk_ref[keys, cols] + jnp.where(m0, dkst[0:SB_T], dkst[SB_T:2 * SB_T])
                dv_ref[keys, cols] = dv_ref[keys, cols] + jnp.where(m0, dvst[0:SB_T], dvst[SB_T:2 * SB_T])
                x0, x1 = _rowsum2(e)
                ddq = _dot(dzb, kst)
                if kind == "B":
                    x0, x1, ddq = _lower(x0), _lower(x1), _lower(ddq)
                out.append((s0 + x0, s1 + x1, dq + ddq))
            return tuple(out)

        for cp in loads(0, 0):
            cp.start()

        def two(jj, cr):
            cr = tile(2 * jj, cr, "full", 0, True)
            return tile(2 * jj + 1, cr, "full", 1, True)

        zc = jnp.zeros((SB_TQ, 1), F32)
        carries = lax.fori_loop(0, i, two, ((zc, zc, jnp.zeros((SB_TQ, 128), F32)),) * SB_NP)
        carries = tile(2 * i, carries, "A", 0, True)
        carries = tile(2 * i + 1, carries, "B", 1, False)
        for cols, carry in zip(pairs, carries):
            dq_ref[:, cols] = carry[2] * SCALE

    wide = 128 * SB_NP
    blk = lambda b, g, i: (b * nq + i, g)
    full = lambda b, g, i: (b, g)
    hbm = pl.BlockSpec(memory_space=pl.ANY)
    return pl.pallas_call(
        body, name="sb_bwd", grid=(nbatch, 4 // SB_NP, nq),
        in_specs=[pl.BlockSpec((SB_TQ, wide), blk), pl.BlockSpec((seq, wide), full), pl.BlockSpec((seq, wide), full),
                  pl.BlockSpec((SB_TQ, wide), blk), hbm, hbm, _const_spec((SB_T, SB_T))],
        out_specs=[pl.BlockSpec((SB_TQ, wide), blk), pl.BlockSpec((seq, wide), full), pl.BlockSpec((seq, wide), full)],
        out_shape=[jax.ShapeDtypeStruct((T, 512), F32)] * 3,
        scratch_shapes=[pltpu.VMEM((2, SB_NP, SB_TQ, 2 * SB_T), BF16), pltpu.VMEM((2, SB_NP, SB_TQ, 2 * SB_T), BF16),
                        pltpu.SemaphoreType.DMA((2, 2))],
        compiler_params=_params(3),
    )(qb, kb, vb, dob, wst, ost, pblk)


def _sigmoid(g):
    return 1.0 / (1.0 + jnp.exp(-g))


def _out_proj(oa, ob, ga, gb, x2, tgt, wout_bf):
    T = x2.shape[0]

    def body(oa_ref, ob_ref, ga_ref, gb_ref, x_ref, t_ref, w_ref,
             dout_ref, doa_ref, dob_ref, dga_ref, dgb_ref, dw_ref, loss_ref):
        @pl.when(pl.program_id(0) == 0)
        def _():
            loss_ref[...] = jnp.zeros_like(loss_ref)
            dw_ref[...] = jnp.zeros_like(dw_ref)

        halves = ((oa_ref, ga_ref, doa_ref, dga_ref, 0), (ob_ref, gb_ref, dob_ref, dgb_ref, 512))
        out = x_ref[...]
        gated = []
        for o_ref, g_ref, _, _, lo in halves:
            g = g_ref[...]
            sg = _sigmoid(g)
            y = (o_ref[...] * (g * sg)).astype(BF16)
            out = out + _dot(y, w_ref[lo:lo + 512, :])
            gated.append((y, g, sg))
        diff = out - t_ref[...]
        dout = diff * (1.0 / D_MODEL)
        dout_ref[...] = dout
        loss_ref[...] = loss_ref[...] + jnp.sum(diff * diff) * (0.5 / D_MODEL)
        db = dout.astype(BF16)
        for (o_ref, _, do_ref, dg_ref, lo), (y, g, sg) in zip(halves, gated):
            dw_ref[lo:lo + 512, :] = dw_ref[lo:lo + 512, :] + _dot_tn(y, db)
            dy = _dot_nt(db, w_ref[lo:lo + 512, :])
            do_ref[...] = (dy * (g * sg)).astype(BF16)
            dg_ref[...] = (dy * o_ref[...] * (sg * (1.0 + g * (1.0 - sg)))).astype(BF16)

    rows = lambda w: pl.BlockSpec((TM, w), lambda i: (i, 0))
    return pl.pallas_call(
        body, name="out_proj", grid=(T // TM,),
        in_specs=[rows(512), rows(512), rows(512), rows(512), rows(D_MODEL), rows(D_MODEL),
                  _const_spec((D_MODEL, D_MODEL))],
        out_specs=[rows(D_MODEL)] + [rows(512)] * 4 + [_const_spec((D_MODEL, D_MODEL)), _const_spec((8, 128))],
        out_shape=[jax.ShapeDtypeStruct((T, D_MODEL), F32)] + [jax.ShapeDtypeStruct((T, 512), BF16)] * 4
        + [jax.ShapeDtypeStruct((D_MODEL, D_MODEL), F32), jax.ShapeDtypeStruct((8, 128), F32)],
        compiler_params=_params(1),
    )(oa, ob, ga, gb, x2, tgt, wout_bf)


def _qk_grad(qraw, kraw, dqrot, dk2, dv2, qg512, kg128, g512, cos, sa, sb):
    T = qraw.shape[0]

    def body(qraw_ref, kraw_ref, dqrot_ref, dk2_ref, dv2_ref, qg_ref, kg_ref, g_ref, cos_ref, sa_ref, sb_ref,
             dqa_ref, dkv_ref, dqg_ref, dkg_ref):
        @pl.when(pl.program_id(0) == 0)
        def _():
            dqg_ref[...] = jnp.zeros_like(dqg_ref)
            dkg_ref[...] = jnp.zeros_like(dkg_ref)

        cosv, sav, sbv = cos_ref[...], sa_ref[...], sb_ref[...]
        m0 = _lane_mask(TM) > 0.5

        def head_norm_bwd(raw, dn_rot, gmat, gain):
            r = lax.rsqrt(_dot_hl(raw * raw, gmat) + EPS)
            n = raw * r
            dn = dn_rot * gain
            return r * (dn - n * _dot_hl(dn * n, gmat)), jnp.sum(dn_rot * n, axis=0, keepdims=True)

        def fold(ref):
            a, b = ref[:, 0:128], ref[:, 128:256]
            return jnp.where(m0, a + pltpu.roll(a, 64, 1), b + pltpu.roll(b, 64, 1))

        dqn = jnp.concatenate([_rope_bwd(dqrot_ref[:, s * 128:(s + 1) * 128], cosv, sav, sbv) for s in range(4)], axis=1)
        dqa, dqg = head_norm_bwd(qraw_ref[...], dqn, g_ref[...], qg_ref[...])
        dka, dkg = head_norm_bwd(kraw_ref[...], _rope_bwd(fold(dk2_ref), cosv, sav, sbv), g_ref[0:128, 0:128], kg_ref[...])
        dqa_ref[...] = dqa.astype(BF16)
        dkv_ref[:, 0:128] = dka.astype(BF16)
        dkv_ref[:, 128:256] = fold(dv2_ref).astype(BF16)
        dqg_ref[...] = dqg_ref[...] + dqg
        dkg_ref[...] = dkg_ref[...] + dkg

    rows = lambda w: pl.BlockSpec((TM, w), lambda i: (i, 0))
    return pl.pallas_call(
        body, name="qk_grad", grid=(T // TM,),
        in_specs=[rows(512), rows(128), rows(512), rows(256), rows(256), _const_spec((1, 512)), _const_spec((1, 128)),
                  _const_spec((512, 512)), rows(128), rows(128), rows(128)],
        out_specs=[rows(512), rows(256), _const_spec((1, 512)), _const_spec((1, 128))],
        out_shape=[jax.ShapeDtypeStruct((T, 512), BF16), jax.ShapeDtypeStruct((T, 256), BF16),
                   jax.ShapeDtypeStruct((1, 512), F32), jax.ShapeDtypeStruct((1, 128), F32)],
        compiler_params=_params(1),
    )(qraw, kraw, dqrot, dk2, dv2, qg512, kg128, g512, cos, sa, sb)


_PIECES = ((QA, 512), (KA, 256), (GA, 512), (QB, 512), (KB, 512), (VB, 512), (GB, 512))


def _w_in_grad(h, pieces):
    T = h.shape[0]

    def body(h_ref, *refs):
        dw_ref = refs[-1]

        @pl.when(pl.program_id(0) == 0)
        def _():
            dw_ref[...] = jnp.zeros_like(dw_ref)

        hb = h_ref[...]
        for (lo, width), p_ref in zip(_PIECES, refs[:-1]):
            dw_ref[lo:lo + width, :] = dw_ref[lo:lo + width, :] + _dot_tn(p_ref[...].astype(BF16), hb)

    rows = lambda w: pl.BlockSpec((TM, w), lambda i: (i, 0))
    return pl.pallas_call(
        body, name="w_in_grad", grid=(T // TM,),
        in_specs=[rows(D_MODEL)] + [rows(w) for _, w in _PIECES],
        out_specs=_const_spec((IN_WIDTH, D_MODEL)),
        out_shape=jax.ShapeDtypeStruct((IN_WIDTH, D_MODEL), F32),
        compiler_params=_params(1),
    )(h, *pieces)


def _chip_exchange(src_ref, dst_ref, sems):
    _, _, c, chips = _place()
    return [_remote(src_ref.at[2 * cx + cy], dst_ref.at[j], sems, j, (cx, cy, c)) for j, (cx, cy) in enumerate(chips)]


def _x_grad(x2, dout, pieces, w_bf, ng, cin4):
    T = x2.shape[0]
    npc = len(_PIECES)
    steps = T // TM

    def body(x_ref, dout_ref, *refs):
        w_ref, ng_ref, cin_ref, gx_ref, dng_ref, rin_ref, ssem, rsem = refs[npc:]
        step = pl.program_id(0)
        copies = lambda: _chip_exchange(cin_ref, rin_ref, (ssem, rsem))

        @pl.when(step == 0)
        def _():
            dng_ref[...] = jnp.zeros_like(dng_ref)
            for cp in copies():
                cp.start()

        dh = jnp.zeros((TM, D_MODEL), F32)
        for (lo, width), p_ref in zip(_PIECES, refs[:npc]):
            dh = dh + _dot(p_ref[...].astype(BF16), w_ref[lo:lo + width, :])
        xb = x_ref[...]
        r = lax.rsqrt(jnp.mean(xb * xb, axis=-1, keepdims=True) + EPS)
        n = xb * r
        dn = dh * ng_ref[...]
        gx_ref[...] = dout_ref[...] + r * (dn - n * jnp.mean(dn * n, axis=-1, keepdims=True))
        dng_ref[...] = dng_ref[...] + jnp.sum(dh * n, axis=0, keepdims=True)

        @pl.when(step == steps - 1)
        def _():
            for cp in copies():
                cp.wait_recv()
            for cp in copies():
                cp.wait_send()

    rows = lambda w: pl.BlockSpec((TM, w), lambda i: (i, 0))
    hbm = pl.BlockSpec(memory_space=pl.ANY)
    return pl.pallas_call(
        body, name="x_grad", grid=(steps,),
        in_specs=[rows(D_MODEL), rows(D_MODEL)] + [rows(w) for _, w in _PIECES]
        + [_const_spec((IN_WIDTH, D_MODEL)), _const_spec((1, D_MODEL)), hbm],
        out_specs=[rows(D_MODEL), _const_spec((1, D_MODEL)), hbm],
        out_shape=[jax.ShapeDtypeStruct((T, D_MODEL), F32), jax.ShapeDtypeStruct((1, D_MODEL), F32),
                   jax.ShapeDtypeStruct((3,) + cin4.shape[1:], cin4.dtype)],
        scratch_shapes=[pltpu.SemaphoreType.DMA((3,)), pltpu.SemaphoreType.DMA((3,))],
        compiler_params=_params(1),
    )(x2, dout, *pieces, w_bf, ng, cin4)


HBM = pl.BlockSpec(memory_space=pl.ANY)


def _place():
    x, y, c = lax.axis_index("x"), lax.axis_index("y"), lax.axis_index("c")
    chips = [(1 - x, y), (x, 1 - y), (1 - x, 1 - y)]
    return x, y, c, chips


def _remote(src, dst, sems, k, to):
    return pltpu.make_async_remote_copy(src_ref=src, dst_ref=dst, send_sem=sems[0].at[k], recv_sem=sems[1].at[k],
                                        device_id=to, device_id_type=MESH)


def _gather_start(src, dst, sems):
    x, y, c, chips = _place()
    for j, (cx, cy) in enumerate(chips):
        _remote(src.at[c], dst.at[2 * x + y, c], sems, j, (cx, cy, c)).start()


def _gather_finish(src, dst, sems):
    x, y, c, chips = _place()
    sib = (x, y, 1 - c)
    fwds = []
    for j, (cx, cy) in enumerate(chips):
        k = 2 * cx + cy
        _remote(src.at[c], dst.at[k, c], sems, j, sib).wait_recv()
        fwds.append(_remote(dst.at[k, c], dst.at[k, c], sems, 3 + j, sib))
        fwds[-1].start()
    for j, (cx, cy) in enumerate(chips):
        _remote(src.at[c], dst.at[2 * cx + cy, 1 - c], sems, 3 + j, sib).wait_recv()
    for j, (cx, cy) in enumerate(chips):
        _remote(src.at[c], dst.at[2 * x + y, c], sems, j, (cx, cy, c)).wait_send()
    for cp in fwds:
        cp.wait_send()


def _own_slot(gathered, shard):
    me = 2 * lax.axis_index("x") + lax.axis_index("y")
    return lax.dynamic_update_slice(gathered, shard[None], (me, 0, 0, 0))


def _gather_weights(shard):
    def body(src, dst, ssem, rsem):
        _gather_start(src, dst, (ssem, rsem))
        _gather_finish(src, dst, (ssem, rsem))

    gathered = pl.pallas_call(
        body, name="gather_weights", in_specs=[HBM], out_specs=HBM,
        out_shape=jax.ShapeDtypeStruct((4,) + shard.shape, shard.dtype),
        scratch_shapes=[pltpu.SemaphoreType.DMA((6,)), pltpu.SemaphoreType.DMA((6,))],
    )(shard)
    return _own_slot(gathered, shard)


def _pair_exchange(g4):
    def body(src, dst, ssem, rsem):
        x, y, c, _ = _place()
        cp = _remote(src.at[:, pl.ds(1 - c, 1)], dst, (ssem, rsem), 0, (x, y, 1 - c))
        cp.start()
        cp.wait()

    return pl.pallas_call(
        body, name=f"pair_exchange_{g4.shape[2]}", in_specs=[HBM], out_specs=HBM,
        out_shape=jax.ShapeDtypeStruct((4, 1) + g4.shape[2:], g4.dtype),
        scratch_shapes=[pltpu.SemaphoreType.DMA((1,)), pltpu.SemaphoreType.DMA((1,))],
    )(g4)


def _share_halves(hin, hout, small):
    def body(hin_ref, hout_ref, small_ref, oin_ref, oout_ref, sall_ref, ssem, rsem, lsem):
        x, y, c, _ = _place()
        sems = (ssem, rsem)
        sib = (x, y, 1 - c)
        me = 4 * x + 2 * y + c
        own = pltpu.make_async_copy(small_ref, sall_ref.at[me], lsem.at[0])
        own.start()
        sent = [_remote(hin_ref, oin_ref, sems, 0, sib), _remote(hout_ref, oout_ref, sems, 1, sib)]
        flips = [(fx, fy, fc) for fx in (0, 1) for fy in (0, 1) for fc in (0, 1)][1:]
        for k, (fx, fy, fc) in enumerate(flips):
            sent.append(_remote(small_ref, sall_ref.at[me], sems, 2 + k, (x ^ fx, y ^ fy, c ^ fc)))
        for cp in sent:
            cp.start()
        _remote(hin_ref, oin_ref, sems, 0, sib).wait_recv()
        _remote(hout_ref, oout_ref, sems, 1, sib).wait_recv()
        for k, (fx, fy, fc) in enumerate(flips):
            src = 4 * (x ^ fx) + 2 * (y ^ fy) + (c ^ fc)
            _remote(small_ref, sall_ref.at[src], sems, 2 + k, sib).wait_recv()
        for cp in sent:
            cp.wait_send()
        own.wait()

    return pl.pallas_call(
        body, name="share_halves", in_specs=[HBM, HBM, HBM], out_specs=[HBM, HBM, HBM],
        out_shape=[jax.ShapeDtypeStruct(hin.shape, F32), jax.ShapeDtypeStruct(hout.shape, F32),
                   jax.ShapeDtypeStruct((8,) + small.shape, F32)],
        scratch_shapes=[pltpu.SemaphoreType.DMA((9,)), pltpu.SemaphoreType.DMA((9,)), pltpu.SemaphoreType.DMA((1,))],
    )(hin, hout, small)


def _add_half(cidx, full4, recv4):
    _, _, rows, width = full4.shape

    def body(c_ref, a_ref, b_ref, o_ref):
        o_ref[0] = (a_ref[0, 0] + b_ref[0, 0]).astype(BF16)

    return pl.pallas_call(
        body, name=f"add_half_{rows}",
        grid_spec=pltpu.PrefetchScalarGridSpec(
            num_scalar_prefetch=1, grid=(4,),
            in_specs=[pl.BlockSpec((1, 1, rows, width), lambda k, c: (k, c[0], 0, 0)),
                      pl.BlockSpec((1, 1, rows, width), lambda k, c: (k, 0, 0, 0))],
            out_specs=pl.BlockSpec((1, rows, width), lambda k, c: (k, 0, 0))),
        out_shape=jax.ShapeDtypeStruct((4, rows, width), BF16),
        compiler_params=_params(1),
    )(cidx, full4, recv4)


def _sum_chips(chip, own4, recv3):
    _, rows, width = recv3.shape
    rb = rows // 2

    def body(k_ref, a_ref, r_ref, o_ref):
        acc = a_ref[0].astype(F32)
        for s in range(3):
            acc = acc + r_ref[s].astype(F32)
        o_ref[...] = acc

    return pl.pallas_call(
        body, name=f"sum_chips_{rows}",
        grid_spec=pltpu.PrefetchScalarGridSpec(
            num_scalar_prefetch=1, grid=(rows // rb,),
            in_specs=[pl.BlockSpec((1, rb, width), lambda i, k: (k[0], i, 0)),
                      pl.BlockSpec((3, rb, width), lambda i, k: (0, i, 0))],
            out_specs=pl.BlockSpec((rb, width), lambda i, k: (i, 0))),
        out_shape=jax.ShapeDtypeStruct((rows, width), F32),
        compiler_params=_params(1),
    )(chip, own4, recv3)


def _adam_math(w, g, m, v):
    c1 = 1.0 - ADAM_B1 ** ADAM_STEP
    c2 = 1.0 - ADAM_B2 ** ADAM_STEP
    nm = ADAM_B1 * m + (1.0 - ADAM_B1) * g
    nv = ADAM_B2 * v + (1.0 - ADAM_B2) * (g * g)
    return -ADAM_LR * ((nm / c1) / (jnp.sqrt(nv / c2) + ADAM_EPS) + ADAM_WD * w), nm, nv


def _small_update(small_all, params):
    n = len(params)
    flat = [a for p in params for a in p]

    def body(s_ref, *refs):
        ins, loss_ref, outs = refs[:3 * n], refs[3 * n], refs[3 * n + 1:]
        total = s_ref[0]
        for d in range(1, 8):
            total = total + s_ref[d]
        loss_ref[...] = total[n:n + 1, 0:1]
        for r in range(n):
            w_ref, m_ref, v_ref = ins[3 * r:3 * r + 3]
            g = total[r:r + 1, 0:w_ref.shape[1]]
            outs[4 * r][...] = g
            outs[4 * r + 1][...], outs[4 * r + 2][...], outs[4 * r + 3][...] = _adam_math(w_ref[...], g, m_ref[...], v_ref[...])

    whole = lambda a: pl.BlockSpec(a.shape, lambda i: (0,) * a.ndim)
    out_shape = [jax.ShapeDtypeStruct((1, 1), F32)] + [jax.ShapeDtypeStruct(p[0].shape, F32) for p in params for _ in range(4)]
    res = pl.pallas_call(
        body, name="small_update", grid=(1,),
        in_specs=[whole(small_all)] + [whole(a) for a in flat], out_specs=[whole(s) for s in out_shape],
        out_shape=out_shape, compiler_params=_params(1),
    )(small_all, *flat)
    return res[0], [tuple(res[1 + 4 * r:5 + 4 * r]) for r in range(n)]


def _adamw_halves(cidx, w, own, recv, m, v):
    rows, width = w.shape
    rb = rows // 4

    def body(c_ref, w_ref, own_ref, recv_ref, m_ref, v_ref, g_ref, d_ref, nm_ref, nv_ref):
        mine = (pl.program_id(0) // 2) == c_ref[0]
        g = jnp.where(mine, own_ref[...], recv_ref[...])
        g_ref[...] = g
        d_ref[...], nm_ref[...], nv_ref[...] = _adam_math(w_ref[...], g, m_ref[...], v_ref[...])

    full = pl.BlockSpec((rb, width), lambda i, c: (i, 0))
    half = pl.BlockSpec((rb, width), lambda i, c: (i % 2, 0))
    return pl.pallas_call(
        body, name=f"adamw_halves_{rows}",
        grid_spec=pltpu.PrefetchScalarGridSpec(
            num_scalar_prefetch=1, grid=(4,),
            in_specs=[full, half, half, full, full], out_specs=[full] * 4),
        out_shape=[jax.ShapeDtypeStruct((rows, width), F32)] * 4,
        compiler_params=_params(1),
    )(cidx, w, own, recv, m, v)


def _constants():
    idx = jnp.arange(512)
    g512 = jnp.where(idx[:, None] // HEAD_DIM == idx[None, :] // HEAD_DIM, 1.0 / HEAD_DIM, 0.0).astype(BF16)
    j = jnp.arange(SB_T)
    ublk = jnp.where(j[:, None] >= j[None, :], 1.0, 0.0).astype(BF16)
    pblk = jnp.where(j[:, None] < j[None, :], 1.0, 0.0).astype(BF16)
    return g512, jnp.concatenate([ublk, ublk], axis=0), pblk


def _pad_rows(v, width):
    return jnp.pad(v, ((0, 0), (0, width - v.shape[1])))


def _pair_sum(cidx, partial, rows):
    g4 = partial.reshape(4, 2, rows, D_MODEL)
    return _add_half(cidx, g4, _pair_exchange(g4))


def _step(x2, tgt, positions, norm_gain, q_norm_gain, k_norm_gain, sinks, w_bf, wout_shard, nbatch, seq):
    g512, ublk, pblk = _constants()
    qg512 = jnp.tile(q_norm_gain, (1, 8))
    kg128 = jnp.tile(k_norm_gain, (1, 2))
    sink1 = sinks.reshape(8)
    cidx = lax.axis_index("c").astype(jnp.int32).reshape(1)
    chip = (2 * lax.axis_index("x") + lax.axis_index("y")).astype(jnp.int32).reshape(1)

    (h, qraw, kraw, qrot, k2, v2, ga, qb, kb, vb, gb), (cos, sa, sb), wout_bf = _norm_proj(
        x2, norm_gain, w_bf, qg512, kg128, g512, positions.reshape(-1, 1), wout_shard)
    oa = _swa_fwd(sink1, qrot, k2, v2, nbatch, seq)
    ob, wst, ost = _sb_fwd(qb, kb, vb, ublk, nbatch, seq)
    dout, doa, dob, dga, dgb, dwout, loss_acc = _out_proj(oa, ob, ga, gb, x2, tgt, wout_bf)

    cout4 = _pair_sum(cidx, dwout, SHARD_OUT // 2)
    dqrot, dk2, dv2, dsink, rout3 = _swa_bwd(sink1, qrot, k2, v2, doa, cout4, nbatch, seq)
    dqb, dkb, dvb = _sb_bwd(qb, kb, vb, dob, wst, ost, pblk, nbatch, seq)
    dk2, dv2 = _unshift(dk2, nbatch, seq), _unshift(dv2, nbatch, seq)
    dqa, dkv, dqg, dkg = _qk_grad(qraw, kraw, dqrot, dk2, dv2, qg512, kg128, g512, cos, sa, sb)
    pieces = (dqa, dkv, dga, dqb, dkb, dvb, dgb)
    cin4 = _pair_sum(cidx, _w_in_grad(h, pieces), SHARD_IN // 2)
    gx, dng, rin3 = _x_grad(x2, dout, pieces, w_bf, norm_gain, cin4)
    own_in, own_out = _sum_chips(chip, cin4, rin3), _sum_chips(chip, cout4, rout3)

    dqg64 = dqg.reshape(8, HEAD_DIM).sum(axis=0, keepdims=True)
    dkg64 = dkg.reshape(2, HEAD_DIM).sum(axis=0, keepdims=True)
    small = jnp.concatenate([dng, _pad_rows(dqg64, D_MODEL), _pad_rows(dkg64, D_MODEL),
                             _pad_rows(dsink[:, 0].reshape(1, 8), D_MODEL), _pad_rows(loss_acc[0:1, 0:1], D_MODEL),
                             jnp.zeros((3, D_MODEL), F32)], axis=0)
    sib_in, sib_out, small_all = _share_halves(own_in, own_out, small)
    return gx, cidx, (own_in, sib_in), (own_out, sib_out), small_all


def kernel(x, positions, norm_gain, w_in, q_norm_gain, k_norm_gain, sinks, w_out, loss_target, m_norm_gain, m_w_in, m_q_norm_gain, m_k_norm_gain, m_sinks, m_w_out, v_norm_gain, v_w_in, v_q_norm_gain, v_k_norm_gain, v_sinks, v_w_out):
    nbatch, seq, _ = x.shape
    T = nbatch * seq
    x2 = x.reshape(T, D_MODEL)
    tgt = loss_target.reshape(T, D_MODEL)
    tr = lambda a: jnp.swapaxes(a[0], 0, 1)
    win_t, m_win_t, v_win_t = tr(w_in), tr(m_w_in), tr(v_w_in)

    w_bf = _gather_weights(win_t.astype(BF16).reshape(2, SHARD_IN // 2, D_MODEL)).reshape(IN_WIDTH, D_MODEL)
    wout_shard = w_out[0].astype(BF16).reshape(2, SHARD_OUT // 2, D_MODEL)

    gx, cidx, g_in_halves, g_out_halves, small_all = _step(
        x2, tgt, positions, norm_gain, q_norm_gain, k_norm_gain, sinks, w_bf, wout_shard, nbatch, seq)

    g_in, d_in, nm_in, nv_in = [jnp.swapaxes(a, 0, 1) for a in
                                _adamw_halves(cidx, win_t, *g_in_halves, m_win_t, v_win_t)]
    g_out, d_out, nm_out, nv_out = _adamw_halves(cidx, w_out[0], *g_out_halves, m_w_out[0], v_w_out[0])
    loss, small = _small_update(small_all, [(norm_gain, m_norm_gain, v_norm_gain), (q_norm_gain, m_q_norm_gain, v_q_norm_gain),
                                            (k_norm_gain, m_k_norm_gain, v_k_norm_gain), (sinks, m_sinks, v_sinks)])
    (g_ng, d_ng, m_ng, v_ng), (g_qg, d_qg, m_qg, v_qg), (g_kg, d_kg, m_kg, v_kg), (g_sk, d_sk, m_sk, v_sk) = small
    return (loss.reshape(()), gx.reshape(nbatch, seq, D_MODEL),
            g_ng, g_in[None], g_qg, g_kg, g_sk, g_out[None],
            d_ng, d_in[None], d_qg, d_kg, d_sk, d_out[None],
            m_ng, nm_in[None], m_qg, m_kg, m_sk, nm_out[None],
            v_ng, nv_in[None], v_qg, v_kg, v_sk, nv_out[None])
```

```python
import functools
import math

import jax
import jax.numpy as jnp
from jax import lax
from jax.experimental import pallas as pl
from jax.experimental.pallas import tpu as pltpu

F32 = jnp.float32
BF16 = jnp.bfloat16

D_MODEL = 1024
HEAD_DIM = 64
BLOCK = 128
ROPE_THETA = 10000.0
EPS = 1e-6
QA, KA, VA, GA, QB, KB, VB, GB = 0, 512, 640, 768, 1280, 1792, 2304, 2816
IN_WIDTH = 3328
SHARD_IN = IN_WIDTH // 4
SHARD_OUT = D_MODEL // 4
SCALE = 1.0 / math.sqrt(HEAD_DIM)
NEG = -1e30

ADAM_LR, ADAM_B1, ADAM_B2, ADAM_EPS, ADAM_WD, ADAM_STEP = 0.001, 0.9, 0.999, 1e-08, 0.01, 10

TM = 512
VMEM_LIMIT = 56 * 1024 * 1024
MESH = pl.DeviceIdType.MESH


def _dot(a, b):
    return jnp.dot(a, b, preferred_element_type=F32)


def _dot_nt(a, b):
    return lax.dot_general(a, b, (((1,), (1,)), ((), ())), preferred_element_type=F32)


def _dot_tn(a, b):
    return lax.dot_general(a, b, (((0,), (0,)), ((), ())), preferred_element_type=F32)


def _dot_b(a, m):
    return _dot(a.astype(BF16), m)


def _params(n_axes=None, vmem=VMEM_LIMIT):
    sem = None if n_axes is None else ("arbitrary",) * n_axes
    return pltpu.CompilerParams(dimension_semantics=sem, vmem_limit_bytes=vmem)


def _const_spec(shape):
    nd = len(shape)
    return pl.BlockSpec(shape, lambda *_: (0,) * nd)


def _rope_fwd(x, cos, sa, sb):
    return x * cos + pltpu.roll(x, 96, 1) * sa + pltpu.roll(x, 32, 1) * sb


def _rope_bwd(d, cos, sa, sb):
    return d * cos - pltpu.roll(d, 96, 1) * sa - pltpu.roll(d, 32, 1) * sb


def _lane_mask(rows, dtype=F32):
    lane = lax.broadcasted_iota(jnp.int32, (rows, 128), 1)
    return jnp.where(lane < HEAD_DIM, 1.0, 0.0).astype(dtype)


def _norm_proj(x2, ng, w_bf, qg512, kg128, g512, cos, sa, sb, wout_shard):
    T = x2.shape[0]
    steps = T // TM

    def body(x_ref, ng_ref, w_ref, qg_ref, kg_ref, g_ref, cos_ref, sa_ref, sb_ref, wo_ref,
             h_ref, qraw_ref, kraw_ref, qrot_ref, k2_ref, v2_ref, ga_ref, qb_ref, kb_ref, vb_ref, gb_ref, ao_ref,
             ssem, rsem):
        @pl.when(pl.program_id(0) == 0)
        def _():
            _gather_start(wo_ref, ao_ref, (ssem, rsem))

        xb = x_ref[...]
        r = lax.rsqrt(jnp.mean(xb * xb, axis=-1, keepdims=True) + EPS)
        h = (xb * r * ng_ref[...]).astype(BF16)
        h_ref[...] = h
        cosv, sav, sbv = cos_ref[...], sa_ref[...], sb_ref[...]
        m0 = _lane_mask(TM)

        def dup(v):
            v0 = v * m0
            v1 = v - v0
            return v0 + pltpu.roll(v0, 64, 1), v1 + pltpu.roll(v1, 64, 1)

        qa = _dot_nt(h, w_ref[QA:KA, :])
        qraw_ref[...] = qa
        qn = qa * lax.rsqrt(_dot_b(qa * qa, g_ref[...]) + EPS) * qg_ref[...]
        for s in range(4):
            qs = _rope_fwd(qn[:, s * 128:(s + 1) * 128], cosv, sav, sbv)
            qrot_ref[:, s * 128:(s + 1) * 128] = (qs * SCALE).astype(BF16)
        ka = _dot_nt(h, w_ref[KA:VA, :])
        kraw_ref[...] = ka
        kn = ka * lax.rsqrt(_dot_b(ka * ka, g_ref[0:128, 0:128]) + EPS) * kg_ref[...]
        k0, k1 = dup(_rope_fwd(kn, cosv, sav, sbv))
        k2_ref[:, 0:128] = k0.astype(BF16)
        k2_ref[:, 128:256] = k1.astype(BF16)
        v0, v1 = dup(_dot_nt(h, w_ref[VA:GA, :]))
        v2_ref[:, 0:128] = v0.astype(BF16)
        v2_ref[:, 128:256] = v1.astype(BF16)
        ga_ref[...] = _dot_nt(h, w_ref[GA:QB, :])
        qb_ref[...] = (_dot_nt(h, w_ref[QB:KB, :]) * SCALE).astype(BF16)
        kb_ref[...] = _dot_nt(h, w_ref[KB:VB, :]).astype(BF16)
        vb_ref[...] = _dot_nt(h, w_ref[VB:GB, :]).astype(BF16)
        gb_ref[...] = _dot_nt(h, w_ref[GB:IN_WIDTH, :])

        @pl.when(pl.program_id(0) == steps - 1)
        def _():
            _gather_finish(wo_ref, ao_ref, (ssem, rsem))

    def rows(w):
        return pl.BlockSpec((TM, w), lambda i: (i, 0))

    outs = [(D_MODEL, BF16), (512, F32), (128, F32), (512, BF16), (256, BF16), (256, BF16), (512, F32),
            (512, BF16), (512, BF16), (512, BF16), (512, F32)]
    hbm = pl.BlockSpec(memory_space=pl.ANY)
    res = pl.pallas_call(
        body, name="norm_proj", grid=(steps,),
        in_specs=[rows(D_MODEL), _const_spec((1, D_MODEL)), _const_spec((IN_WIDTH, D_MODEL)), _const_spec((1, 512)),
                  _const_spec((1, 128)), _const_spec((512, 512)), rows(128), rows(128), rows(128), hbm],
        out_specs=[rows(w) for w, _ in outs] + [hbm],
        out_shape=[jax.ShapeDtypeStruct((T, w), dt) for w, dt in outs]
        + [jax.ShapeDtypeStruct((4,) + wout_shard.shape, wout_shard.dtype)],
        scratch_shapes=[pltpu.SemaphoreType.DMA((6,)), pltpu.SemaphoreType.DMA((6,))],
        compiler_params=_params(1),
    )(x2, ng, w_bf, qg512, kg128, g512, cos, sa, sb, wout_shard)
    return res[:-1], _own_slot(res[-1], wout_shard).reshape(D_MODEL, D_MODEL)


SWA_Q = 512
SWA_SUB = SWA_Q // BLOCK


def _swa_scores(q, kst, mask, sink_ref, kv):
    s_all = _dot_nt(q, kst)
    first = lax.broadcasted_iota(jnp.int32, (256, 1), 0) < 128
    probs, stats = [], []
    for hh in range(2):
        sink = jnp.where(first, sink_ref[kv * 4 + hh], sink_ref[kv * 4 + 2 + hh])
        s = jnp.where(mask, s_all[:, hh * 256:(hh + 1) * 256], NEG)
        m = jnp.maximum(jnp.max(s, axis=1, keepdims=True), sink)
        pe = jnp.exp(s - m)
        inv = 1.0 / (jnp.sum(pe, axis=1, keepdims=True) + jnp.exp(sink - m))
        probs.append(pe * inv)
        stats.append(jnp.exp(sink - m) * inv)
    return probs, stats


def _swa_mask(has_prev):
    r = lax.broadcasted_iota(jnp.int32, (256, 256), 0) & 127
    c = lax.broadcasted_iota(jnp.int32, (256, 256), 1)
    band = (c > r) & (c <= r + 128)
    return band if has_prev is True else band & ((c >= 128) | has_prev)


def _stack_pairs(ref, rows, kv):
    return jnp.concatenate([ref[rows, (2 * kv) * 128:(2 * kv + 1) * 128], ref[rows, (2 * kv + 1) * 128:(2 * kv + 2) * 128]], axis=0)


def _swa_keys(prev_ref, main_ref, s, kv, m0b):
    cols = slice(kv * 128, (kv + 1) * 128)
    prev = prev_ref[:, cols] if s == 0 else main_ref[(s - 1) * 128:s * 128, cols]
    kc = jnp.concatenate([prev, main_ref[s * 128:(s + 1) * 128, cols]], axis=0)
    k0 = kc * m0b
    return jnp.concatenate([k0, kc - k0], axis=0)


def _swa_fwd(sinks, qrot, k2, v2, nbatch, seq):
    ni = seq // SWA_Q
    T = nbatch * seq

    def body(sink_ref, q_ref, kp_ref, km_ref, vp_ref, vm_ref, o_ref):
        i = pl.program_id(1)
        m0b = _lane_mask(256, BF16)
        for s in range(SWA_SUB):
            mask = _swa_mask(True if s else i > 0)
            rows = slice(s * 128, (s + 1) * 128)
            for kv in range(2):
                kst = _swa_keys(kp_ref, km_ref, s, kv, m0b)
                vst = _swa_keys(vp_ref, vm_ref, s, kv, m0b)
                probs, _ = _swa_scores(_stack_pairs(q_ref, rows, kv), kst, mask, sink_ref, kv)
                o2 = _dot(jnp.concatenate(probs, axis=1).astype(BF16), vst)
                o_ref[rows, kv * 256:kv * 256 + 128] = o2[0:128].astype(BF16)
                o_ref[rows, kv * 256 + 128:(kv + 1) * 256] = o2[128:256].astype(BF16)

    main = lambda b, i: (b * ni + i, 0)
    prev = lambda b, i: ((b * ni + i) * SWA_SUB - jnp.where(i > 0, 1, 0), 0)
    return pl.pallas_call(
        body, name="swa_fwd", grid=(nbatch, ni),
        in_specs=[pl.BlockSpec(memory_space=pltpu.SMEM), pl.BlockSpec((SWA_Q, 512), main),
                  pl.BlockSpec((128, 256), prev), pl.BlockSpec((SWA_Q, 256), main),
                  pl.BlockSpec((128, 256), prev), pl.BlockSpec((SWA_Q, 256), main)],
        out_specs=pl.BlockSpec((SWA_Q, 512), main),
        out_shape=jax.ShapeDtypeStruct((T, 512), BF16),
        compiler_params=_params(2),
    )(sinks, qrot, k2, k2, v2, v2)


def _swa_bwd(sinks, qrot, k2, v2, doa, cout4, nbatch, seq):
    ni = seq // SWA_Q
    T = nbatch * seq

    def body(sink_ref, q_ref, kp_ref, km_ref, vp_ref, vm_ref, do_ref, cout_ref,
             dq_ref, dk_ref, dv_ref, ds_ref, rout_ref, dkc, dvc, ssem, rsem):
        b, i = pl.program_id(0), pl.program_id(1)
        copies = lambda: _chip_exchange(cout_ref, rout_ref, (ssem, rsem))

        @pl.when((b == 0) & (i == 0))
        def _():
            ds_ref[...] = jnp.zeros_like(ds_ref)
            for cp in copies():
                cp.start()

        @pl.when((b == nbatch - 1) & (i == ni))
        def _():
            for cp in copies():
                cp.wait_recv()
            for cp in copies():
                cp.wait_send()

        @pl.when(i == 0)
        def _():
            dkc[...] = jnp.zeros_like(dkc)
            dvc[...] = jnp.zeros_like(dvc)

        @pl.when(i < ni)
        def _():
            m0b = _lane_mask(256, BF16)
            m0 = _lane_mask(128) > 0.5
            for kv in range(2):
                kcols = slice(kv * 128, (kv + 1) * 128)
                dk_own, dv_own = dkc[:, kcols], dvc[:, kcols]
                for s in range(SWA_SUB):
                    mask = _swa_mask(True if s else i > 0)
                    rows = slice(s * 128, (s + 1) * 128)
                    kst = _swa_keys(kp_ref, km_ref, s, kv, m0b)
                    vst = _swa_keys(vp_ref, vm_ref, s, kv, m0b)
                    q, do = _stack_pairs(q_ref, rows, kv), _stack_pairs(do_ref, rows, kv)
                    probs, psink = _swa_scores(q, kst, mask, sink_ref, kv)
                    dp_all = _dot_nt(do, vst)
                    ds_parts = []
                    for hh in range(2):
                        dp = dp_all[:, hh * 256:(hh + 1) * 256]
                        delta = jnp.sum(probs[hh] * dp, axis=1, keepdims=True)
                        ds_parts.append(probs[hh] * (dp - delta))
                        dsink = psink[hh] * delta
                        for pr in range(2):
                            h = kv * 4 + pr * 2 + hh
                            ds_ref[h:h + 1, :] = ds_ref[h:h + 1, :] - jnp.sum(dsink[pr * 128:(pr + 1) * 128])
                    ds_all = jnp.concatenate(ds_parts, axis=1).astype(BF16)
                    p_all = jnp.concatenate(probs, axis=1).astype(BF16)
                    dq2 = _dot(ds_all, kst) * SCALE
                    dq_ref[rows, kv * 256:kv * 256 + 128] = dq2[0:128]
                    dq_ref[rows, kv * 256 + 128:(kv + 1) * 256] = dq2[128:256]
                    dkst = _dot_tn(ds_all, q)
                    dvst = _dot_tn(p_all, do)
                    dk_ref[rows, kcols] = dk_own + jnp.where(m0, dkst[0:128], dkst[256:384])
                    dv_ref[rows, kcols] = dv_own + jnp.where(m0, dvst[0:128], dvst[256:384])
                    dk_own = jnp.where(m0, dkst[128:256], dkst[384:512])
                    dv_own = jnp.where(m0, dvst[128:256], dvst[384:512])
                dkc[:, kcols] = dk_own
                dvc[:, kcols] = dv_own

        @pl.when(i == ni)
        def _():
            dk_ref[...] = jnp.zeros_like(dk_ref)
            dv_ref[...] = jnp.zeros_like(dv_ref)
            dk_ref[0:128, :] = dkc[...]
            dv_ref[0:128, :] = dvc[...]

    main = lambda b, i: (b * ni + jnp.minimum(i, ni - 1), 0)
    prev = lambda b, i: ((b * ni + jnp.minimum(i, ni - 1)) * SWA_SUB - jnp.where(jnp.minimum(i, ni - 1) > 0, 1, 0), 0)
    shifted = lambda b, i: (b * (ni + 1) + i, 0)
    tpad = nbatch * (ni + 1) * SWA_Q
    return pl.pallas_call(
        body, name="swa_bwd", grid=(nbatch, ni + 1),
        in_specs=[pl.BlockSpec(memory_space=pltpu.SMEM), pl.BlockSpec((SWA_Q, 512), main),
                  pl.BlockSpec((128, 256), prev), pl.BlockSpec((SWA_Q, 256), main),
                  pl.BlockSpec((128, 256), prev), pl.BlockSpec((SWA_Q, 256), main),
                  pl.BlockSpec((SWA_Q, 512), main), pl.BlockSpec(memory_space=pl.ANY)],
        out_specs=[pl.BlockSpec((SWA_Q, 512), main), pl.BlockSpec((SWA_Q, 256), shifted),
                   pl.BlockSpec((SWA_Q, 256), shifted), _const_spec((8, 128)), pl.BlockSpec(memory_space=pl.ANY)],
        out_shape=[jax.ShapeDtypeStruct((T, 512), F32), jax.ShapeDtypeStruct((tpad, 256), F32),
                   jax.ShapeDtypeStruct((tpad, 256), F32), jax.ShapeDtypeStruct((8, 128), F32),
                   jax.ShapeDtypeStruct((3,) + cout4.shape[1:], cout4.dtype)],
        scratch_shapes=[pltpu.VMEM((128, 256), F32), pltpu.VMEM((128, 256), F32),
                        pltpu.SemaphoreType.DMA((3,)), pltpu.SemaphoreType.DMA((3,))],
        compiler_params=_params(2),
    )(sinks, qrot, k2, k2, v2, v2, doa, cout4)


def _unshift(dkpad, nbatch, seq):
    return dkpad.reshape(nbatch, seq + SWA_Q, 256)[:, BLOCK:BLOCK + seq].reshape(nbatch * seq, 256)


SB_T = 256
SB_TQ = 2 * SB_T


def _sb_mask(kind):
    if kind == "full":
        return None
    rows = SB_T if kind == "B" else SB_TQ
    r = lax.broadcasted_iota(jnp.int32, (rows, 2 * SB_T), 0)
    c = lax.broadcasted_iota(jnp.int32, (rows, 2 * SB_T), 1)
    causal = (c & (SB_T - 1)) < r
    return causal | (r >= SB_T) if kind == "A" else causal


def _sb_rows(kind):
    return slice(SB_T, SB_TQ) if kind == "B" else slice(0, SB_TQ)


def _lower(x):
    return jnp.concatenate([jnp.zeros_like(x), x], axis=0)


def _sb_logits(neg_q, kst):
    nz = _dot_nt(neg_q, kst)
    sign = jnp.uint32(0x80000000)
    neg_abs = lax.bitcast_convert_type(lax.bitcast_convert_type(nz, jnp.uint32) | sign, F32)
    return nz, jnp.minimum(nz, 0.0) - jnp.log(1.0 + jnp.exp(neg_abs))


SB_NP = 4


def _pair_rows(ref, j, cols, m0b):
    kj = ref[pl.ds(pl.multiple_of(j * SB_T, SB_T), SB_T), cols]
    k0 = kj * m0b
    return jnp.concatenate([k0, kj - k0], axis=0)


def _bcast2(c0, c1):
    rows = c0.shape[0]
    return jnp.concatenate([jnp.broadcast_to(c0, (rows, SB_T)), jnp.broadcast_to(c1, (rows, SB_T))], axis=1)


def _rowsum2(x):
    return jnp.sum(x[:, 0:SB_T], axis=1, keepdims=True), jnp.sum(x[:, SB_T:2 * SB_T], axis=1, keepdims=True)


def _scan2(x, tri2):
    outs = []
    for h in range(2):
        xh = x[:, h * SB_T:(h + 1) * SB_T]
        hi = xh.astype(BF16)
        lo = (xh - hi.astype(F32)).astype(BF16)
        outs.append(_dot(jnp.concatenate([hi, lo], axis=1), tri2))
    return jnp.concatenate(outs, axis=1)


def _scan1(x, tri):
    xb = x.astype(BF16)
    return jnp.concatenate([_dot(xb[:, h * SB_T:(h + 1) * SB_T], tri) for h in range(2)], axis=1)


def _sb_fwd(qb, kb, vb, ublk, nbatch, seq):
    nq, nk = seq // SB_TQ, seq // SB_T
    T = nbatch * seq

    def body(q_ref, k_ref, v_ref, u_ref, o_ref, wst_ref, ost_ref, wbuf, obuf, sems):
        b, i = pl.program_id(0), pl.program_id(2)
        m0b = _lane_mask(SB_T, BF16)
        u = u_ref[...]
        pairs = [slice(pp * 128, (pp + 1) * 128) for pp in range(SB_NP)]
        neg_qs = [-q_ref[:, cols] for cols in pairs]

        def stores(j, slot):
            tix = (b * nq + i) * nk + j
            return (pltpu.make_async_copy(wbuf.at[slot], wst_ref.at[tix], sems.at[0, slot]),
                    pltpu.make_async_copy(obuf.at[slot], ost_ref.at[tix], sems.at[1, slot]))

        def tile(j, carries, kind, slot, wait):
            rows, mask = _sb_rows(kind), _sb_mask(kind)
            if wait:
                for cp in stores(j, slot):
                    cp.wait()
            out = []
            for pp, (cols, neg_q, (c0, c1, acc)) in enumerate(zip(pairs, neg_qs, carries)):
                kst = _pair_rows(k_ref, j, cols, m0b)
                vst = _pair_rows(v_ref, j, cols, m0b)
                nz, lb = _sb_logits(neg_q[rows], kst)
                if mask is not None:
                    lb = jnp.where(mask, lb, 0.0)
                incl = _scan2(lb, u)
                w = jnp.exp(incl - nz if kind == "B" else incl + _bcast2(c0, c1) - nz)
                if mask is not None:
                    w = jnp.where(mask, w, 0.0)
                wb = w.astype(BF16)
                wbuf[slot, pp, rows, :] = wb
                obuf[slot, pp, rows, :] = lb.astype(BF16)
                if kind == "B":
                    wbuf[slot, pp, 0:SB_T, :] = jnp.zeros((SB_T, 2 * SB_T), BF16)
                    obuf[slot, pp, 0:SB_T, :] = jnp.zeros((SB_T, 2 * SB_T), BF16)
                d0, d1, da = incl[:, 0:1], incl[:, SB_T:SB_T + 1], _dot(wb, vst)
                if kind == "B":
                    d0, d1, da = _lower(d0), _lower(d1), _lower(da)
                out.append((c0 + d0, c1 + d1, acc + da))
            for cp in stores(j, slot):
                cp.start()
            return tuple(out)

        zc = jnp.zeros((SB_TQ, 1), F32)
        carries = tile(2 * i + 1, ((zc, zc, jnp.zeros((SB_TQ, 128), F32)),) * SB_NP, "B", 0, False)
        carries = tile(2 * i, carries, "A", 1, False)

        def two(jj, cr):
            cr = tile(2 * i - 1 - 2 * jj, cr, "full", 0, True)
            return tile(2 * i - 2 - 2 * jj, cr, "full", 1, True)

        carries = lax.fori_loop(0, i, two, carries)
        for cols, carry in zip(pairs, carries):
            o_ref[:, cols] = carry[2].astype(BF16)
        for slot in range(2):
            for cp in stores(0, slot):
                cp.wait()

    wide = 128 * SB_NP
    blk = lambda b, g, i: (b * nq + i, g)
    full = lambda b, g, i: (b, g)
    hbm = pl.BlockSpec(memory_space=pl.ANY)
    tiles = jax.ShapeDtypeStruct((nbatch * nq * nk, SB_NP, SB_TQ, 2 * SB_T), BF16)
    return pl.pallas_call(
        body, name="sb_fwd", grid=(nbatch, 4 // SB_NP, nq),
        in_specs=[pl.BlockSpec((SB_TQ, wide), blk), pl.BlockSpec((seq, wide), full), pl.BlockSpec((seq, wide), full),
                  _const_spec((2 * SB_T, SB_T))],
        out_specs=[pl.BlockSpec((SB_TQ, wide), blk), hbm, hbm],
        out_shape=[jax.ShapeDtypeStruct((T, 512), BF16), tiles, tiles],
        scratch_shapes=[pltpu.VMEM((2, SB_NP, SB_TQ, 2 * SB_T), BF16), pltpu.VMEM((2, SB_NP, SB_TQ, 2 * SB_T), BF16),
                        pltpu.SemaphoreType.DMA((2, 2))],
        compiler_params=_params(3),
    )(qb, kb, vb, ublk)


def _sb_bwd(qb, kb, vb, dob, wst, ost, pblk, nbatch, seq):
    nq, nk = seq // SB_TQ, seq // SB_T
    T = nbatch * seq

    def body(q_ref, k_ref, v_ref, do_ref, wst_ref, ost_ref, up_ref, dq_ref, dk_ref, dv_ref, wbuf, obuf, sems):
        b, i = pl.program_id(0), pl.program_id(2)

        @pl.when(i == 0)
        def _():
            dk_ref[...] = jnp.zeros_like(dk_ref)
            dv_ref[...] = jnp.zeros_like(dv_ref)

        m0b = _lane_mask(SB_T, BF16)
        m0 = _lane_mask(SB_T) > 0.5
        up = up_ref[...]
        pairs = [slice(pp * 128, (pp + 1) * 128) for pp in range(SB_NP)]

        def loads(j, slot):
            tix = (b * nq + i) * nk + j
            return (pltpu.make_async_copy(wst_ref.at[tix], wbuf.at[slot], sems.at[0, slot]),
                    pltpu.make_async_copy(ost_ref.at[tix], obuf.at[slot], sems.at[1, slot]))

        def tile(j, carries, kind, slot, fetch_next):
            rows, mask = _sb_rows(kind), _sb_mask(kind)
            if fetch_next:
                for cp in loads(j + 1, 1 - slot):
                    cp.start()
            for cp in loads(j, slot):
                cp.wait()
            out = []
            for pp, (cols, (s0, s1, dq)) in enumerate(zip(pairs, carries)):
                q, do = q_ref[rows, cols], do_ref[rows, cols]
                kst = _pair_rows(k_ref, j, cols, m0b)
                vst = _pair_rows(v_ref, j, cols, m0b)
                wb = wbuf[slot, pp, rows, :]
                e = _dot_nt(do, vst) * wb.astype(F32)
                dlb = _bcast2(s0[rows], s1[rows]) + _scan1(e, up)
                dz = (e + dlb) * jnp.exp(obuf[slot, pp, rows, :].astype(F32)) - dlb
                if mask is not None:
                    dz = jnp.where(mask, dz, 0.0)
                dzb = dz.astype(BF16)
                dkst = _dot_tn(dzb, q)
                dvst = _dot_tn(wb, do)
                keys = pl.ds(pl.multiple_of(j * SB_T, SB_T), SB_T)
                dk_ref[keys, cols] = dk_ref[keys, cols] + jnp.where(m0, dkst[0:SB_T], dkst[SB_T:2 * SB_T])
                dv_ref[keys, cols] = dv_ref[keys, cols] + jnp.where(m0, dvst[0:SB_T], dvst[SB_T:2 * SB_T])
                x0, x1 = _rowsum2(e)
                ddq = _dot(dzb, kst)
                if kind == "B":
                    x0, x1, ddq = _lower(x0), _lower(x1), _lower(ddq)
                out.append((s0 + x0, s1 + x1, dq + ddq))
            return tuple(out)

        for cp in loads(0, 0):
            cp.start()

        def two(jj, cr):
            cr = tile(2 * jj, cr, "full", 0, True)
            return tile(2 * jj + 1, cr, "full", 1, True)

        zc = jnp.zeros((SB_TQ, 1), F32)
        carries = lax.fori_loop(0, i, two, ((zc, zc, jnp.zeros((SB_TQ, 128), F32)),) * SB_NP)
        carries = tile(2 * i, carries, "A", 0, True)
        carries = tile(2 * i + 1, carries, "B", 1, False)
        for cols, carry in zip(pairs, carries):
            dq_ref[:, cols] = carry[2] * SCALE

    wide = 128 * SB_NP
    blk = lambda b, g, i: (b * nq + i, g)
    full = lambda b, g, i: (b, g)
    hbm = pl.BlockSpec(memory_space=pl.ANY)
    return pl.pallas_call(
        body, name="sb_bwd", grid=(nbatch, 4 // SB_NP, nq),
        in_specs=[pl.BlockSpec((SB_TQ, wide), blk), pl.BlockSpec((seq, wide), full), pl.BlockSpec((seq, wide), full),
                  pl.BlockSpec((SB_TQ, wide), blk), hbm, hbm, _const_spec((SB_T, SB_T))],
        out_specs=[pl.BlockSpec((SB_TQ, wide), blk), pl.BlockSpec((seq, wide), full), pl.BlockSpec((seq, wide), full)],
        out_shape=[jax.ShapeDtypeStruct((T, 512), F32)] * 3,
        scratch_shapes=[pltpu.VMEM((2, SB_NP, SB_TQ, 2 * SB_T), BF16), pltpu.VMEM((2, SB_NP, SB_TQ, 2 * SB_T), BF16),
                        pltpu.SemaphoreType.DMA((2, 2))],
        compiler_params=_params(3),
    )(qb, kb, vb, dob, wst, ost, pblk)


def _sigmoid(g):
    return 1.0 / (1.0 + jnp.exp(-g))


def _out_proj(oa, ob, ga, gb, x2, tgt, wout_bf):
    T = x2.shape[0]

    def body(oa_ref, ob_ref, ga_ref, gb_ref, x_ref, t_ref, w_ref,
             dout_ref, doa_ref, dob_ref, dga_ref, dgb_ref, dw_ref, loss_ref):
        @pl.when(pl.program_id(0) == 0)
        def _():
            loss_ref[...] = jnp.zeros_like(loss_ref)
            dw_ref[...] = jnp.zeros_like(dw_ref)

        halves = ((oa_ref, ga_ref, doa_ref, dga_ref, 0), (ob_ref, gb_ref, dob_ref, dgb_ref, 512))
        out = x_ref[...]
        gated = []
        for o_ref, g_ref, _, _, lo in halves:
            g = g_ref[...]
            sg = _sigmoid(g)
            y = (o_ref[...] * (g * sg)).astype(BF16)
            out = out + _dot(y, w_ref[lo:lo + 512, :])
            gated.append((y, g, sg))
        diff = out - t_ref[...]
        dout = diff * (1.0 / D_MODEL)
        dout_ref[...] = dout
        loss_ref[...] = loss_ref[...] + jnp.sum(diff * diff) * (0.5 / D_MODEL)
        db = dout.astype(BF16)
        for (o_ref, _, do_ref, dg_ref, lo), (y, g, sg) in zip(halves, gated):
            dw_ref[lo:lo + 512, :] = dw_ref[lo:lo + 512, :] + _dot_tn(y, db)
            dy = _dot_nt(db, w_ref[lo:lo + 512, :])
            do_ref[...] = (dy * (g * sg)).astype(BF16)
            dg_ref[...] = (dy * o_ref[...] * (sg * (1.0 + g * (1.0 - sg)))).astype(BF16)

    rows = lambda w: pl.BlockSpec((TM, w), lambda i: (i, 0))
    return pl.pallas_call(
        body, name="out_proj", grid=(T // TM,),
        in_specs=[rows(512), rows(512), rows(512), rows(512), rows(D_MODEL), rows(D_MODEL),
                  _const_spec((D_MODEL, D_MODEL))],
        out_specs=[rows(D_MODEL)] + [rows(512)] * 4 + [_const_spec((D_MODEL, D_MODEL)), _const_spec((8, 128))],
        out_shape=[jax.ShapeDtypeStruct((T, D_MODEL), F32)] + [jax.ShapeDtypeStruct((T, 512), BF16)] * 4
        + [jax.ShapeDtypeStruct((D_MODEL, D_MODEL), F32), jax.ShapeDtypeStruct((8, 128), F32)],
        compiler_params=_params(1),
    )(oa, ob, ga, gb, x2, tgt, wout_bf)


def _qk_grad(qraw, kraw, dqrot, dk2, dv2, qg512, kg128, g512, cos, sa, sb):
    T = qraw.shape[0]

    def body(qraw_ref, kraw_ref, dqrot_ref, dk2_ref, dv2_ref, qg_ref, kg_ref, g_ref, cos_ref, sa_ref, sb_ref,
             dqa_ref, dkv_ref, dqg_ref, dkg_ref):
        @pl.when(pl.program_id(0) == 0)
        def _():
            dqg_ref[...] = jnp.zeros_like(dqg_ref)
            dkg_ref[...] = jnp.zeros_like(dkg_ref)

        cosv, sav, sbv = cos_ref[...], sa_ref[...], sb_ref[...]
        m0 = _lane_mask(TM) > 0.5

        def head_norm_bwd(raw, dn_rot, gmat, gain):
            r = lax.rsqrt(_dot_b(raw * raw, gmat) + EPS)
            n = raw * r
            dn = dn_rot * gain
            return r * (dn - n * _dot_b(dn * n, gmat)), jnp.sum(dn_rot * n, axis=0, keepdims=True)

        def fold(ref):
            a, b = ref[:, 0:128], ref[:, 128:256]
            return jnp.where(m0, a + pltpu.roll(a, 64, 1), b + pltpu.roll(b, 64, 1))

        dqn = jnp.concatenate([_rope_bwd(dqrot_ref[:, s * 128:(s + 1) * 128], cosv, sav, sbv) for s in range(4)], axis=1)
        dqa, dqg = head_norm_bwd(qraw_ref[...], dqn, g_ref[...], qg_ref[...])
        dka, dkg = head_norm_bwd(kraw_ref[...], _rope_bwd(fold(dk2_ref), cosv, sav, sbv), g_ref[0:128, 0:128], kg_ref[...])
        dqa_ref[...] = dqa.astype(BF16)
        dkv_ref[:, 0:128] = dka.astype(BF16)
        dkv_ref[:, 128:256] = fold(dv2_ref).astype(BF16)
        dqg_ref[...] = dqg_ref[...] + dqg
        dkg_ref[...] = dkg_ref[...] + dkg

    rows = lambda w: pl.BlockSpec((TM, w), lambda i: (i, 0))
    return pl.pallas_call(
        body, name="qk_grad", grid=(T // TM,),
        in_specs=[rows(512), rows(128), rows(512), rows(256), rows(256), _const_spec((1, 512)), _const_spec((1, 128)),
                  _const_spec((512, 512)), rows(128), rows(128), rows(128)],
        out_specs=[rows(512), rows(256), _const_spec((1, 512)), _const_spec((1, 128))],
        out_shape=[jax.ShapeDtypeStruct((T, 512), BF16), jax.ShapeDtypeStruct((T, 256), BF16),
                   jax.ShapeDtypeStruct((1, 512), F32), jax.ShapeDtypeStruct((1, 128), F32)],
        compiler_params=_params(1),
    )(qraw, kraw, dqrot, dk2, dv2, qg512, kg128, g512, cos, sa, sb)


_PIECES = ((QA, 512), (KA, 256), (GA, 512), (QB, 512), (KB, 512), (VB, 512), (GB, 512))


def _w_in_grad(h, pieces):
    T = h.shape[0]

    def body(h_ref, *refs):
        dw_ref = refs[-1]

        @pl.when(pl.program_id(0) == 0)
        def _():
            dw_ref[...] = jnp.zeros_like(dw_ref)

        hb = h_ref[...]
        for (lo, width), p_ref in zip(_PIECES, refs[:-1]):
            dw_ref[lo:lo + width, :] = dw_ref[lo:lo + width, :] + _dot_tn(p_ref[...].astype(BF16), hb)

    rows = lambda w: pl.BlockSpec((TM, w), lambda i: (i, 0))
    return pl.pallas_call(
        body, name="w_in_grad", grid=(T // TM,),
        in_specs=[rows(D_MODEL)] + [rows(w) for _, w in _PIECES],
        out_specs=_const_spec((IN_WIDTH, D_MODEL)),
        out_shape=jax.ShapeDtypeStruct((IN_WIDTH, D_MODEL), F32),
        compiler_params=_params(1),
    )(h, *pieces)


def _chip_exchange(src_ref, dst_ref, sems):
    _, _, c, chips = _place()
    return [_remote(src_ref.at[2 * cx + cy], dst_ref.at[j], sems, j, (cx, cy, c)) for j, (cx, cy) in enumerate(chips)]


def _x_grad(x2, dout, pieces, w_bf, ng, cin4):
    T = x2.shape[0]
    npc = len(_PIECES)
    steps = T // TM

    def body(x_ref, dout_ref, *refs):
        w_ref, ng_ref, cin_ref, gx_ref, dng_ref, rin_ref, ssem, rsem = refs[npc:]
        step = pl.program_id(0)
        copies = lambda: _chip_exchange(cin_ref, rin_ref, (ssem, rsem))

        @pl.when(step == 0)
        def _():
            dng_ref[...] = jnp.zeros_like(dng_ref)
            for cp in copies():
                cp.start()

        dh = jnp.zeros((TM, D_MODEL), F32)
        for (lo, width), p_ref in zip(_PIECES, refs[:npc]):
            dh = dh + _dot(p_ref[...].astype(BF16), w_ref[lo:lo + width, :])
        xb = x_ref[...]
        r = lax.rsqrt(jnp.mean(xb * xb, axis=-1, keepdims=True) + EPS)
        n = xb * r
        dn = dh * ng_ref[...]
        gx_ref[...] = dout_ref[...] + r * (dn - n * jnp.mean(dn * n, axis=-1, keepdims=True))
        dng_ref[...] = dng_ref[...] + jnp.sum(dh * n, axis=0, keepdims=True)

        @pl.when(step == steps - 1)
        def _():
            for cp in copies():
                cp.wait_recv()
            for cp in copies():
                cp.wait_send()

    rows = lambda w: pl.BlockSpec((TM, w), lambda i: (i, 0))
    hbm = pl.BlockSpec(memory_space=pl.ANY)
    return pl.pallas_call(
        body, name="x_grad", grid=(steps,),
        in_specs=[rows(D_MODEL), rows(D_MODEL)] + [rows(w) for _, w in _PIECES]
        + [_const_spec((IN_WIDTH, D_MODEL)), _const_spec((1, D_MODEL)), hbm],
        out_specs=[rows(D_MODEL), _const_spec((1, D_MODEL)), hbm],
        out_shape=[jax.ShapeDtypeStruct((T, D_MODEL), F32), jax.ShapeDtypeStruct((1, D_MODEL), F32),
                   jax.ShapeDtypeStruct((3,) + cin4.shape[1:], cin4.dtype)],
        scratch_shapes=[pltpu.SemaphoreType.DMA((3,)), pltpu.SemaphoreType.DMA((3,))],
        compiler_params=_params(1),
    )(x2, dout, *pieces, w_bf, ng, cin4)


HBM = pl.BlockSpec(memory_space=pl.ANY)


def _place():
    x, y, c = lax.axis_index("x"), lax.axis_index("y"), lax.axis_index("c")
    chips = [(1 - x, y), (x, 1 - y), (1 - x, 1 - y)]
    return x, y, c, chips


def _remote(src, dst, sems, k, to):
    return pltpu.make_async_remote_copy(src_ref=src, dst_ref=dst, send_sem=sems[0].at[k], recv_sem=sems[1].at[k],
                                        device_id=to, device_id_type=MESH)


def _gather_start(src, dst, sems):
    x, y, c, chips = _place()
    for j, (cx, cy) in enumerate(chips):
        _remote(src.at[c], dst.at[2 * x + y, c], sems, j, (cx, cy, c)).start()


def _gather_finish(src, dst, sems):
    x, y, c, chips = _place()
    sib = (x, y, 1 - c)
    fwds = []
    for j, (cx, cy) in enumerate(chips):
        k = 2 * cx + cy
        _remote(src.at[c], dst.at[k, c], sems, j, sib).wait_recv()
        fwds.append(_remote(dst.at[k, c], dst.at[k, c], sems, 3 + j, sib))
        fwds[-1].start()
    for j, (cx, cy) in enumerate(chips):
        _remote(src.at[c], dst.at[2 * cx + cy, 1 - c], sems, 3 + j, sib).wait_recv()
    for j, (cx, cy) in enumerate(chips):
        _remote(src.at[c], dst.at[2 * x + y, c], sems, j, (cx, cy, c)).wait_send()
    for cp in fwds:
        cp.wait_send()


def _own_slot(gathered, shard):
    me = 2 * lax.axis_index("x") + lax.axis_index("y")
    return lax.dynamic_update_slice(gathered, shard[None], (me, 0, 0, 0))


def _gather_weights(shard):
    def body(src, dst, ssem, rsem):
        _gather_start(src, dst, (ssem, rsem))
        _gather_finish(src, dst, (ssem, rsem))

    gathered = pl.pallas_call(
        body, name="gather_weights", in_specs=[HBM], out_specs=HBM,
        out_shape=jax.ShapeDtypeStruct((4,) + shard.shape, shard.dtype),
        scratch_shapes=[pltpu.SemaphoreType.DMA((6,)), pltpu.SemaphoreType.DMA((6,))],
    )(shard)
    return _own_slot(gathered, shard)


def _pair_exchange(g4):
    def body(src, dst, ssem, rsem):
        x, y, c, _ = _place()
        cp = _remote(src.at[:, pl.ds(1 - c, 1)], dst, (ssem, rsem), 0, (x, y, 1 - c))
        cp.start()
        cp.wait()

    return pl.pallas_call(
        body, name=f"pair_exchange_{g4.shape[2]}", in_specs=[HBM], out_specs=HBM,
        out_shape=jax.ShapeDtypeStruct((4, 1) + g4.shape[2:], g4.dtype),
        scratch_shapes=[pltpu.SemaphoreType.DMA((1,)), pltpu.SemaphoreType.DMA((1,))],
    )(g4)


def _share_halves(hin, hout, small):
    def body(hin_ref, hout_ref, small_ref, oin_ref, oout_ref, sall_ref, ssem, rsem, lsem):
        x, y, c, _ = _place()
        sems = (ssem, rsem)
        sib = (x, y, 1 - c)
        me = 4 * x + 2 * y + c
        own = pltpu.make_async_copy(small_ref, sall_ref.at[me], lsem.at[0])
        own.start()
        sent = [_remote(hin_ref, oin_ref, sems, 0, sib), _remote(hout_ref, oout_ref, sems, 1, sib)]
        flips = [(fx, fy, fc) for fx in (0, 1) for fy in (0, 1) for fc in (0, 1)][1:]
        for k, (fx, fy, fc) in enumerate(flips):
            sent.append(_remote(small_ref, sall_ref.at[me], sems, 2 + k, (x ^ fx, y ^ fy, c ^ fc)))
        for cp in sent:
            cp.start()
        _remote(hin_ref, oin_ref, sems, 0, sib).wait_recv()
        _remote(hout_ref, oout_ref, sems, 1, sib).wait_recv()
        for k, (fx, fy, fc) in enumerate(flips):
            src = 4 * (x ^ fx) + 2 * (y ^ fy) + (c ^ fc)
            _remote(small_ref, sall_ref.at[src], sems, 2 + k, sib).wait_recv()
        for cp in sent:
            cp.wait_send()
        own.wait()

    return pl.pallas_call(
        body, name="share_halves", in_specs=[HBM, HBM, HBM], out_specs=[HBM, HBM, HBM],
        out_shape=[jax.ShapeDtypeStruct(hin.shape, F32), jax.ShapeDtypeStruct(hout.shape, F32),
                   jax.ShapeDtypeStruct((8,) + small.shape, F32)],
        scratch_shapes=[pltpu.SemaphoreType.DMA((9,)), pltpu.SemaphoreType.DMA((9,)), pltpu.SemaphoreType.DMA((1,))],
    )(hin, hout, small)


def _add_half(cidx, full4, recv4):
    _, _, rows, width = full4.shape

    def body(c_ref, a_ref, b_ref, o_ref):
        o_ref[0] = (a_ref[0, 0] + b_ref[0, 0]).astype(BF16)

    return pl.pallas_call(
        body, name=f"add_half_{rows}",
        grid_spec=pltpu.PrefetchScalarGridSpec(
            num_scalar_prefetch=1, grid=(4,),
            in_specs=[pl.BlockSpec((1, 1, rows, width), lambda k, c: (k, c[0], 0, 0)),
                      pl.BlockSpec((1, 1, rows, width), lambda k, c: (k, 0, 0, 0))],
            out_specs=pl.BlockSpec((1, rows, width), lambda k, c: (k, 0, 0))),
        out_shape=jax.ShapeDtypeStruct((4, rows, width), BF16),
        compiler_params=_params(1),
    )(cidx, full4, recv4)


def _sum_chips(chip, own4, recv3):
    _, rows, width = recv3.shape
    rb = rows // 2

    def body(k_ref, a_ref, r_ref, o_ref):
        acc = a_ref[0].astype(F32)
        for s in range(3):
            acc = acc + r_ref[s].astype(F32)
        o_ref[...] = acc

    return pl.pallas_call(
        body, name=f"sum_chips_{rows}",
        grid_spec=pltpu.PrefetchScalarGridSpec(
            num_scalar_prefetch=1, grid=(rows // rb,),
            in_specs=[pl.BlockSpec((1, rb, width), lambda i, k: (k[0], i, 0)),
                      pl.BlockSpec((3, rb, width), lambda i, k: (0, i, 0))],
            out_specs=pl.BlockSpec((rb, width), lambda i, k: (i, 0))),
        out_shape=jax.ShapeDtypeStruct((rows, width), F32),
        compiler_params=_params(1),
    )(chip, own4, recv3)


def _adam_math(w, g, m, v):
    c1 = 1.0 - ADAM_B1 ** ADAM_STEP
    c2 = 1.0 - ADAM_B2 ** ADAM_STEP
    nm = ADAM_B1 * m + (1.0 - ADAM_B1) * g
    nv = ADAM_B2 * v + (1.0 - ADAM_B2) * (g * g)
    return -ADAM_LR * ((nm / c1) / (jnp.sqrt(nv / c2) + ADAM_EPS) + ADAM_WD * w), nm, nv


def _small_update(small_all, params):
    n = len(params)
    flat = [a for p in params for a in p]

    def body(s_ref, *refs):
        ins, loss_ref, outs = refs[:3 * n], refs[3 * n], refs[3 * n + 1:]
        total = s_ref[0]
        for d in range(1, 8):
            total = total + s_ref[d]
        loss_ref[...] = total[n:n + 1, 0:1]
        for r in range(n):
            w_ref, m_ref, v_ref = ins[3 * r:3 * r + 3]
            g = total[r:r + 1, 0:w_ref.shape[1]]
            outs[4 * r][...] = g
            outs[4 * r + 1][...], outs[4 * r + 2][...], outs[4 * r + 3][...] = _adam_math(w_ref[...], g, m_ref[...], v_ref[...])

    whole = lambda a: pl.BlockSpec(a.shape, lambda i: (0,) * a.ndim)
    out_shape = [jax.ShapeDtypeStruct((1, 1), F32)] + [jax.ShapeDtypeStruct(p[0].shape, F32) for p in params for _ in range(4)]
    res = pl.pallas_call(
        body, name="small_update", grid=(1,),
        in_specs=[whole(small_all)] + [whole(a) for a in flat], out_specs=[whole(s) for s in out_shape],
        out_shape=out_shape, compiler_params=_params(1),
    )(small_all, *flat)
    return res[0], [tuple(res[1 + 4 * r:5 + 4 * r]) for r in range(n)]


def _adamw_halves(cidx, w, own, recv, m, v):
    rows, width = w.shape
    rb = rows // 4

    def body(c_ref, w_ref, own_ref, recv_ref, m_ref, v_ref, g_ref, d_ref, nm_ref, nv_ref):
        mine = (pl.program_id(0) // 2) == c_ref[0]
        g = jnp.where(mine, own_ref[...], recv_ref[...])
        g_ref[...] = g
        d_ref[...], nm_ref[...], nv_ref[...] = _adam_math(w_ref[...], g, m_ref[...], v_ref[...])

    full = pl.BlockSpec((rb, width), lambda i, c: (i, 0))
    half = pl.BlockSpec((rb, width), lambda i, c: (i % 2, 0))
    return pl.pallas_call(
        body, name=f"adamw_halves_{rows}",
        grid_spec=pltpu.PrefetchScalarGridSpec(
            num_scalar_prefetch=1, grid=(4,),
            in_specs=[full, half, half, full, full], out_specs=[full] * 4),
        out_shape=[jax.ShapeDtypeStruct((rows, width), F32)] * 4,
        compiler_params=_params(1),
    )(cidx, w, own, recv, m, v)


def _rope_tables(positions):
    half = HEAD_DIM // 2
    inv_freq = ROPE_THETA ** (-jnp.arange(half, dtype=F32) * 2.0 / HEAD_DIM)
    ang = positions.astype(F32).reshape(-1, 1) * inv_freq
    cos, sin, zero = jnp.cos(ang), jnp.sin(ang), jnp.zeros_like(ang)
    return (jnp.concatenate([cos] * 4, axis=1), jnp.concatenate([-sin, zero] * 2, axis=1),
            jnp.concatenate([zero, sin] * 2, axis=1))


def _constants():
    idx = jnp.arange(512)
    g512 = jnp.where(idx[:, None] // HEAD_DIM == idx[None, :] // HEAD_DIM, 1.0 / HEAD_DIM, 0.0).astype(BF16)
    j = jnp.arange(SB_T)
    ublk = jnp.where(j[:, None] >= j[None, :], 1.0, 0.0).astype(BF16)
    pblk = jnp.where(j[:, None] < j[None, :], 1.0, 0.0).astype(BF16)
    return g512, jnp.concatenate([ublk, ublk], axis=0), pblk


def _pad_rows(v, width):
    return jnp.pad(v, ((0, 0), (0, width - v.shape[1])))


def _pair_sum(cidx, partial, rows):
    g4 = partial.reshape(4, 2, rows, D_MODEL)
    return _add_half(cidx, g4, _pair_exchange(g4))


def _step(x2, tgt, positions, norm_gain, q_norm_gain, k_norm_gain, sinks, w_bf, wout_shard, nbatch, seq):
    cos, sa, sb = _rope_tables(positions)
    g512, ublk, pblk = _constants()
    qg512 = jnp.tile(q_norm_gain, (1, 8))
    kg128 = jnp.tile(k_norm_gain, (1, 2))
    sink1 = sinks.reshape(8)
    cidx = lax.axis_index("c").astype(jnp.int32).reshape(1)
    chip = (2 * lax.axis_index("x") + lax.axis_index("y")).astype(jnp.int32).reshape(1)

    (h, qraw, kraw, qrot, k2, v2, ga, qb, kb, vb, gb), wout_bf = _norm_proj(
        x2, norm_gain, w_bf, qg512, kg128, g512, cos, sa, sb, wout_shard)
    oa = _swa_fwd(sink1, qrot, k2, v2, nbatch, seq)
    ob, wst, ost = _sb_fwd(qb, kb, vb, ublk, nbatch, seq)
    dout, doa, dob, dga, dgb, dwout, loss_acc = _out_proj(oa, ob, ga, gb, x2, tgt, wout_bf)

    cout4 = _pair_sum(cidx, dwout, SHARD_OUT // 2)
    dqrot, dk2, dv2, dsink, rout3 = _swa_bwd(sink1, qrot, k2, v2, doa, cout4, nbatch, seq)
    dqb, dkb, dvb = _sb_bwd(qb, kb, vb, dob, wst, ost, pblk, nbatch, seq)
    dk2, dv2 = _unshift(dk2, nbatch, seq), _unshift(dv2, nbatch, seq)
    dqa, dkv, dqg, dkg = _qk_grad(qraw, kraw, dqrot, dk2, dv2, qg512, kg128, g512, cos, sa, sb)
    pieces = (dqa, dkv, dga, dqb, dkb, dvb, dgb)
    cin4 = _pair_sum(cidx, _w_in_grad(h, pieces), SHARD_IN // 2)
    gx, dng, rin3 = _x_grad(x2, dout, pieces, w_bf, norm_gain, cin4)
    own_in, own_out = _sum_chips(chip, cin4, rin3), _sum_chips(chip, cout4, rout3)

    dqg64 = dqg.reshape(8, HEAD_DIM).sum(axis=0, keepdims=True)
    dkg64 = dkg.reshape(2, HEAD_DIM).sum(axis=0, keepdims=True)
    small = jnp.concatenate([dng, _pad_rows(dqg64, D_MODEL), _pad_rows(dkg64, D_MODEL),
                             _pad_rows(dsink[:, 0].reshape(1, 8), D_MODEL), _pad_rows(loss_acc[0:1, 0:1], D_MODEL),
                             jnp.zeros((3, D_MODEL), F32)], axis=0)
    sib_in, sib_out, small_all = _share_halves(own_in, own_out, small)
    return gx, cidx, (own_in, sib_in), (own_out, sib_out), small_all


def kernel(x, positions, norm_gain, w_in, q_norm_gain, k_norm_gain, sinks, w_out, loss_target, m_norm_gain, m_w_in, m_q_norm_gain, m_k_norm_gain, m_sinks, m_w_out, v_norm_gain, v_w_in, v_q_norm_gain, v_k_norm_gain, v_sinks, v_w_out):
    nbatch, seq, _ = x.shape
    T = nbatch * seq
    x2 = x.reshape(T, D_MODEL)
    tgt = loss_target.reshape(T, D_MODEL)
    tr = lambda a: jnp.swapaxes(a[0], 0, 1)
    win_t, m_win_t, v_win_t = tr(w_in), tr(m_w_in), tr(v_w_in)

    w_bf = _gather_weights(win_t.astype(BF16).reshape(2, SHARD_IN // 2, D_MODEL)).reshape(IN_WIDTH, D_MODEL)
    wout_shard = w_out[0].astype(BF16).reshape(2, SHARD_OUT // 2, D_MODEL)

    gx, cidx, g_in_halves, g_out_halves, small_all = _step(
        x2, tgt, positions, norm_gain, q_norm_gain, k_norm_gain, sinks, w_bf, wout_shard, nbatch, seq)

    g_in, d_in, nm_in, nv_in = [jnp.swapaxes(a, 0, 1) for a in
                                _adamw_halves(cidx, win_t, *g_in_halves, m_win_t, v_win_t)]
    g_out, d_out, nm_out, nv_out = _adamw_halves(cidx, w_out[0], *g_out_halves, m_w_out[0], v_w_out[0])
    loss, small = _small_update(small_all, [(norm_gain, m_norm_gain, v_norm_gain), (q_norm_gain, m_q_norm_gain, v_q_norm_gain),
                                            (k_norm_gain, m_k_norm_gain, v_k_norm_gain), (sinks, m_sinks, v_sinks)])
    (g_ng, d_ng, m_ng, v_ng), (g_qg, d_qg, m_qg, v_qg), (g_kg, d_kg, m_kg, v_kg), (g_sk, d_sk, m_sk, v_sk) = small
    return (loss.reshape(()), gx.reshape(nbatch, seq, D_MODEL),
            g_ng, g_in[None], g_qg, g_kg, g_sk, g_out[None],
            d_ng, d_in[None], d_qg, d_kg, d_sk, d_out[None],
            m_ng, nm_in[None], m_qg, m_kg, m_sk, nm_out[None],
            v_ng, nv_in[None], v_qg, v_kg, v_sk, nv_out[None])
```

```python
import functools
import math

import jax
import jax.numpy as jnp
from jax import lax
from jax.experimental import pallas as pl
from jax.experimental.pallas import tpu as pltpu

F32 = jnp.float32
BF16 = jnp.bfloat16

D_MODEL = 1024
HEAD_DIM = 64
BLOCK = 128
ROPE_THETA = 10000.0
EPS = 1e-6
QA, KA, VA, GA, QB, KB, VB, GB = 0, 512, 640, 768, 1280, 1792, 2304, 2816
IN_WIDTH = 3328
SHARD_IN = IN_WIDTH // 4
SHARD_OUT = D_MODEL // 4
SCALE = 1.0 / math.sqrt(HEAD_DIM)
NEG = -1e30

ADAM_LR, ADAM_B1, ADAM_B2, ADAM_EPS, ADAM_WD, ADAM_STEP = 0.001, 0.9, 0.999, 1e-08, 0.01, 10

TM = 512
VMEM_LIMIT = 56 * 1024 * 1024
MESH = pl.DeviceIdType.MESH


def _dot(a, b):
    return jnp.dot(a, b, preferred_element_type=F32)


def _dot_nt(a, b):
    return lax.dot_general(a, b, (((1,), (1,)), ((), ())), preferred_element_type=F32)


def _dot_tn(a, b):
    return lax.dot_general(a, b, (((0,), (0,)), ((), ())), preferred_element_type=F32)


def _dot_b(a, m):
    return _dot(a.astype(BF16), m)


def _params(n_axes=None, vmem=VMEM_LIMIT):
    sem = None if n_axes is None else ("arbitrary",) * n_axes
    return pltpu.CompilerParams(dimension_semantics=sem, vmem_limit_bytes=vmem)


def _const_spec(shape):
    nd = len(shape)
    return pl.BlockSpec(shape, lambda *_: (0,) * nd)


def _rope_fwd(x, cos, sa, sb):
    return x * cos + pltpu.roll(x, 96, 1) * sa + pltpu.roll(x, 32, 1) * sb


def _rope_bwd(d, cos, sa, sb):
    return d * cos - pltpu.roll(d, 96, 1) * sa - pltpu.roll(d, 32, 1) * sb


def _lane_mask(rows, dtype=F32):
    lane = lax.broadcasted_iota(jnp.int32, (rows, 128), 1)
    return jnp.where(lane < HEAD_DIM, 1.0, 0.0).astype(dtype)


def _prologue(x2, ng, pos, win_shard):
    T = x2.shape[0]
    steps = T // TM
    half = HEAD_DIM // 2
    inv_freq = jnp.tile(ROPE_THETA ** (-jnp.arange(half, dtype=F32) * 2.0 / HEAD_DIM), 4).reshape(1, 128)

    def body(x_ref, ng_ref, pos_ref, freq_ref, wi_ref, h_ref, cos_ref, sa_ref, sb_ref, ai_ref, ssem, rsem):
        @pl.when(pl.program_id(0) == 0)
        def _():
            _gather_start(wi_ref, ai_ref, (ssem, rsem))

        xb = x_ref[...]
        r = lax.rsqrt(jnp.mean(xb * xb, axis=-1, keepdims=True) + EPS)
        h_ref[...] = (xb * r * ng_ref[...]).astype(BF16)
        ang = pos_ref[...].astype(F32) * freq_ref[...]
        first = (lax.broadcasted_iota(jnp.int32, (TM, 128), 1) & (HEAD_DIM - 1)) < half
        sn = jnp.sin(ang)
        cos_ref[...] = jnp.cos(ang)
        sa_ref[...] = jnp.where(first, -sn, 0.0)
        sb_ref[...] = jnp.where(first, 0.0, sn)

        @pl.when(pl.program_id(0) == steps - 1)
        def _():
            _gather_finish(wi_ref, ai_ref, (ssem, rsem))

    rows = lambda w: pl.BlockSpec((TM, w), lambda i: (i, 0))
    hbm = pl.BlockSpec(memory_space=pl.ANY)
    h, cos, sa, sb, gathered = pl.pallas_call(
        body, name="prologue", grid=(steps,),
        in_specs=[rows(D_MODEL), _const_spec((1, D_MODEL)), rows(1), _const_spec((1, 128)), hbm],
        out_specs=[rows(D_MODEL), rows(128), rows(128), rows(128), hbm],
        out_shape=[jax.ShapeDtypeStruct((T, D_MODEL), BF16)] + [jax.ShapeDtypeStruct((T, 128), F32)] * 3
        + [jax.ShapeDtypeStruct((4,) + win_shard.shape, win_shard.dtype)],
        scratch_shapes=[pltpu.SemaphoreType.DMA((6,)), pltpu.SemaphoreType.DMA((6,))],
        compiler_params=_params(1),
    )(x2, ng, pos, inv_freq, win_shard)
    return h, (cos, sa, sb), _own_slot(gathered, win_shard).reshape(IN_WIDTH, D_MODEL)


def _norm_proj(h, w_bf, qg512, kg128, g512, cos, sa, sb, wout_shard):
    T = h.shape[0]
    steps = T // TM

    def body(h_ref, w_ref, qg_ref, kg_ref, g_ref, cos_ref, sa_ref, sb_ref, wo_ref,
             qraw_ref, kraw_ref, qrot_ref, k2_ref, v2_ref, ga_ref, qb_ref, kb_ref, vb_ref, gb_ref, ao_ref,
             ssem, rsem):
        @pl.when(pl.program_id(0) == 0)
        def _():
            _gather_start(wo_ref, ao_ref, (ssem, rsem))

        h = h_ref[...]
        cosv, sav, sbv = cos_ref[...], sa_ref[...], sb_ref[...]
        m0 = _lane_mask(TM)

        def dup(v):
            v0 = v * m0
            v1 = v - v0
            return v0 + pltpu.roll(v0, 64, 1), v1 + pltpu.roll(v1, 64, 1)

        qa = _dot_nt(h, w_ref[QA:KA, :])
        qraw_ref[...] = qa
        qn = qa * lax.rsqrt(_dot_b(qa * qa, g_ref[...]) + EPS) * qg_ref[...]
        for s in range(4):
            qs = _rope_fwd(qn[:, s * 128:(s + 1) * 128], cosv, sav, sbv)
            qrot_ref[:, s * 128:(s + 1) * 128] = (qs * SCALE).astype(BF16)
        ka = _dot_nt(h, w_ref[KA:VA, :])
        kraw_ref[...] = ka
        kn = ka * lax.rsqrt(_dot_b(ka * ka, g_ref[0:128, 0:128]) + EPS) * kg_ref[...]
        k0, k1 = dup(_rope_fwd(kn, cosv, sav, sbv))
        k2_ref[:, 0:128] = k0.astype(BF16)
        k2_ref[:, 128:256] = k1.astype(BF16)
        v0, v1 = dup(_dot_nt(h, w_ref[VA:GA, :]))
        v2_ref[:, 0:128] = v0.astype(BF16)
        v2_ref[:, 128:256] = v1.astype(BF16)
        ga_ref[...] = _dot_nt(h, w_ref[GA:QB, :])
        qb_ref[...] = (_dot_nt(h, w_ref[QB:KB, :]) * SCALE).astype(BF16)
        kb_ref[...] = _dot_nt(h, w_ref[KB:VB, :]).astype(BF16)
        vb_ref[...] = _dot_nt(h, w_ref[VB:GB, :]).astype(BF16)
        gb_ref[...] = _dot_nt(h, w_ref[GB:IN_WIDTH, :])

        @pl.when(pl.program_id(0) == steps - 1)
        def _():
            _gather_finish(wo_ref, ao_ref, (ssem, rsem))

    def rows(w):
        return pl.BlockSpec((TM, w), lambda i: (i, 0))

    outs = [(512, F32), (128, F32), (512, BF16), (256, BF16), (256, BF16), (512, F32),
            (512, BF16), (512, BF16), (512, BF16), (512, F32)]
    hbm = pl.BlockSpec(memory_space=pl.ANY)
    res = pl.pallas_call(
        body, name="norm_proj", grid=(steps,),
        in_specs=[rows(D_MODEL), _const_spec((IN_WIDTH, D_MODEL)), _const_spec((1, 512)),
                  _const_spec((1, 128)), _const_spec((512, 512)), rows(128), rows(128), rows(128), hbm],
        out_specs=[rows(w) for w, _ in outs] + [hbm],
        out_shape=[jax.ShapeDtypeStruct((T, w), dt) for w, dt in outs]
        + [jax.ShapeDtypeStruct((4,) + wout_shard.shape, wout_shard.dtype)],
        scratch_shapes=[pltpu.SemaphoreType.DMA((6,)), pltpu.SemaphoreType.DMA((6,))],
        compiler_params=_params(1),
    )(h, w_bf, qg512, kg128, g512, cos, sa, sb, wout_shard)
    return res[:-1], _own_slot(res[-1], wout_shard).reshape(D_MODEL, D_MODEL)


SWA_Q = 512
SWA_SUB = SWA_Q // BLOCK


def _swa_scores(q, kst, mask, sink_ref, kv):
    s_all = _dot_nt(q, kst)
    first = lax.broadcasted_iota(jnp.int32, (256, 1), 0) < 128
    probs, stats = [], []
    for hh in range(2):
        sink = jnp.where(first, sink_ref[kv * 4 + hh], sink_ref[kv * 4 + 2 + hh])
        s = jnp.where(mask, s_all[:, hh * 256:(hh + 1) * 256], NEG)
        m = jnp.maximum(jnp.max(s, axis=1, keepdims=True), sink)
        pe = jnp.exp(s - m)
        inv = 1.0 / (jnp.sum(pe, axis=1, keepdims=True) + jnp.exp(sink - m))
        probs.append(pe * inv)
        stats.append(jnp.exp(sink - m) * inv)
    return probs, stats


def _swa_mask(has_prev):
    r = lax.broadcasted_iota(jnp.int32, (256, 256), 0) & 127
    c = lax.broadcasted_iota(jnp.int32, (256, 256), 1)
    band = (c > r) & (c <= r + 128)
    return band if has_prev is True else band & ((c >= 128) | has_prev)


def _stack_pairs(ref, rows, kv):
    return jnp.concatenate([ref[rows, (2 * kv) * 128:(2 * kv + 1) * 128], ref[rows, (2 * kv + 1) * 128:(2 * kv + 2) * 128]], axis=0)


def _swa_keys(prev_ref, main_ref, s, kv, m0b):
    cols = slice(kv * 128, (kv + 1) * 128)
    prev = prev_ref[:, cols] if s == 0 else main_ref[(s - 1) * 128:s * 128, cols]
    kc = jnp.concatenate([prev, main_ref[s * 128:(s + 1) * 128, cols]], axis=0)
    k0 = kc * m0b
    return jnp.concatenate([k0, kc - k0], axis=0)


def _swa_fwd(sinks, qrot, k2, v2, nbatch, seq):
    ni = seq // SWA_Q
    T = nbatch * seq

    def body(sink_ref, q_ref, kp_ref, km_ref, vp_ref, vm_ref, o_ref):
        i = pl.program_id(1)
        m0b = _lane_mask(256, BF16)
        for s in range(SWA_SUB):
            mask = _swa_mask(True if s else i > 0)
            rows = slice(s * 128, (s + 1) * 128)
            for kv in range(2):
                kst = _swa_keys(kp_ref, km_ref, s, kv, m0b)
                vst = _swa_keys(vp_ref, vm_ref, s, kv, m0b)
                probs, _ = _swa_scores(_stack_pairs(q_ref, rows, kv), kst, mask, sink_ref, kv)
                o2 = _dot(jnp.concatenate(probs, axis=1).astype(BF16), vst)
                o_ref[rows, kv * 256:kv * 256 + 128] = o2[0:128].astype(BF16)
                o_ref[rows, kv * 256 + 128:(kv + 1) * 256] = o2[128:256].astype(BF16)

    main = lambda b, i: (b * ni + i, 0)
    prev = lambda b, i: ((b * ni + i) * SWA_SUB - jnp.where(i > 0, 1, 0), 0)
    return pl.pallas_call(
        body, name="swa_fwd", grid=(nbatch, ni),
        in_specs=[pl.BlockSpec(memory_space=pltpu.SMEM), pl.BlockSpec((SWA_Q, 512), main),
                  pl.BlockSpec((128, 256), prev), pl.BlockSpec((SWA_Q, 256), main),
                  pl.BlockSpec((128, 256), prev), pl.BlockSpec((SWA_Q, 256), main)],
        out_specs=pl.BlockSpec((SWA_Q, 512), main),
        out_shape=jax.ShapeDtypeStruct((T, 512), BF16),
        compiler_params=_params(2),
    )(sinks, qrot, k2, k2, v2, v2)


def _swa_bwd(sinks, qrot, k2, v2, doa, cout4, nbatch, seq):
    ni = seq // SWA_Q
    T = nbatch * seq

    def body(sink_ref, q_ref, kp_ref, km_ref, vp_ref, vm_ref, do_ref, cout_ref,
             dq_ref, dk_ref, dv_ref, ds_ref, rout_ref, dkc, dvc, ssem, rsem):
        b, i = pl.program_id(0), pl.program_id(1)
        copies = lambda: _chip_exchange(cout_ref, rout_ref, (ssem, rsem))

        @pl.when((b == 0) & (i == 0))
        def _():
            ds_ref[...] = jnp.zeros_like(ds_ref)
            for cp in copies():
                cp.start()

        @pl.when((b == nbatch - 1) & (i == ni))
        def _():
            for cp in copies():
                cp.wait_recv()
            for cp in copies():
                cp.wait_send()

        @pl.when(i == 0)
        def _():
            dkc[...] = jnp.zeros_like(dkc)
            dvc[...] = jnp.zeros_like(dvc)

        @pl.when(i < ni)
        def _():
            m0b = _lane_mask(256, BF16)
            m0 = _lane_mask(128) > 0.5
            for kv in range(2):
                kcols = slice(kv * 128, (kv + 1) * 128)
                dk_own, dv_own = dkc[:, kcols], dvc[:, kcols]
                for s in range(SWA_SUB):
                    mask = _swa_mask(True if s else i > 0)
                    rows = slice(s * 128, (s + 1) * 128)
                    kst = _swa_keys(kp_ref, km_ref, s, kv, m0b)
                    vst = _swa_keys(vp_ref, vm_ref, s, kv, m0b)
                    q, do = _stack_pairs(q_ref, rows, kv), _stack_pairs(do_ref, rows, kv)
                    probs, psink = _swa_scores(q, kst, mask, sink_ref, kv)
                    dp_all = _dot_nt(do, vst)
                    ds_parts = []
                    for hh in range(2):
                        dp = dp_all[:, hh * 256:(hh + 1) * 256]
                        delta = jnp.sum(probs[hh] * dp, axis=1, keepdims=True)
                        ds_parts.append(probs[hh] * (dp - delta))
                        dsink = psink[hh] * delta
                        for pr in range(2):
                            h = kv * 4 + pr * 2 + hh
                            ds_ref[h:h + 1, :] = ds_ref[h:h + 1, :] - jnp.sum(dsink[pr * 128:(pr + 1) * 128])
                    ds_all = jnp.concatenate(ds_parts, axis=1).astype(BF16)
                    p_all = jnp.concatenate(probs, axis=1).astype(BF16)
                    dq2 = _dot(ds_all, kst) * SCALE
                    dq_ref[rows, kv * 256:kv * 256 + 128] = dq2[0:128]
                    dq_ref[rows, kv * 256 + 128:(kv + 1) * 256] = dq2[128:256]
                    dkst = _dot_tn(ds_all, q)
                    dvst = _dot_tn(p_all, do)
                    dk_ref[rows, kcols] = dk_own + jnp.where(m0, dkst[0:128], dkst[256:384])
                    dv_ref[rows, kcols] = dv_own + jnp.where(m0, dvst[0:128], dvst[256:384])
                    dk_own = jnp.where(m0, dkst[128:256], dkst[384:512])
                    dv_own = jnp.where(m0, dvst[128:256], dvst[384:512])
                dkc[:, kcols] = dk_own
                dvc[:, kcols] = dv_own

        @pl.when(i == ni)
        def _():
            dk_ref[...] = jnp.zeros_like(dk_ref)
            dv_ref[...] = jnp.zeros_like(dv_ref)
            dk_ref[0:128, :] = dkc[...]
            dv_ref[0:128, :] = dvc[...]

    main = lambda b, i: (b * ni + jnp.minimum(i, ni - 1), 0)
    prev = lambda b, i: ((b * ni + jnp.minimum(i, ni - 1)) * SWA_SUB - jnp.where(jnp.minimum(i, ni - 1) > 0, 1, 0), 0)
    shifted = lambda b, i: (b * (ni + 1) + i, 0)
    tpad = nbatch * (ni + 1) * SWA_Q
    return pl.pallas_call(
        body, name="swa_bwd", grid=(nbatch, ni + 1),
        in_specs=[pl.BlockSpec(memory_space=pltpu.SMEM), pl.BlockSpec((SWA_Q, 512), main),
                  pl.BlockSpec((128, 256), prev), pl.BlockSpec((SWA_Q, 256), main),
                  pl.BlockSpec((128, 256), prev), pl.BlockSpec((SWA_Q, 256), main),
                  pl.BlockSpec((SWA_Q, 512), main), pl.BlockSpec(memory_space=pl.ANY)],
        out_specs=[pl.BlockSpec((SWA_Q, 512), main), pl.BlockSpec((SWA_Q, 256), shifted),
                   pl.BlockSpec((SWA_Q, 256), shifted), _const_spec((8, 128)), pl.BlockSpec(memory_space=pl.ANY)],
        out_shape=[jax.ShapeDtypeStruct((T, 512), F32), jax.ShapeDtypeStruct((tpad, 256), F32),
                   jax.ShapeDtypeStruct((tpad, 256), F32), jax.ShapeDtypeStruct((8, 128), F32),
                   jax.ShapeDtypeStruct((3,) + cout4.shape[1:], cout4.dtype)],
        scratch_shapes=[pltpu.VMEM((128, 256), F32), pltpu.VMEM((128, 256), F32),
                        pltpu.SemaphoreType.DMA((3,)), pltpu.SemaphoreType.DMA((3,))],
        compiler_params=_params(2),
    )(sinks, qrot, k2, k2, v2, v2, doa, cout4)


def _unshift(dkpad, nbatch, seq):
    return dkpad.reshape(nbatch, seq + SWA_Q, 256)[:, BLOCK:BLOCK + seq].reshape(nbatch * seq, 256)


SB_T = 256
SB_TQ = 2 * SB_T


def _sb_mask(kind):
    if kind == "full":
        return None
    rows = SB_T if kind == "B" else SB_TQ
    r = lax.broadcasted_iota(jnp.int32, (rows, 2 * SB_T), 0)
    c = lax.broadcasted_iota(jnp.int32, (rows, 2 * SB_T), 1)
    causal = (c & (SB_T - 1)) < r
    return causal | (r >= SB_T) if kind == "A" else causal


def _sb_rows(kind):
    return slice(SB_T, SB_TQ) if kind == "B" else slice(0, SB_TQ)


def _lower(x):
    return jnp.concatenate([jnp.zeros_like(x), x], axis=0)


def _sb_logits(neg_q, kst):
    nz = _dot_nt(neg_q, kst)
    sign = jnp.uint32(0x80000000)
    neg_abs = lax.bitcast_convert_type(lax.bitcast_convert_type(nz, jnp.uint32) | sign, F32)
    return nz, jnp.minimum(nz, 0.0) - jnp.log(1.0 + jnp.exp(neg_abs))


SB_NP = 4


def _pair_rows(ref, j, cols, m0b):
    kj = ref[pl.ds(pl.multiple_of(j * SB_T, SB_T), SB_T), cols]
    k0 = kj * m0b
    return jnp.concatenate([k0, kj - k0], axis=0)


def _bcast2(c0, c1):
    rows = c0.shape[0]
    return jnp.concatenate([jnp.broadcast_to(c0, (rows, SB_T)), jnp.broadcast_to(c1, (rows, SB_T))], axis=1)


def _rowsum2(x):
    return jnp.sum(x[:, 0:SB_T], axis=1, keepdims=True), jnp.sum(x[:, SB_T:2 * SB_T], axis=1, keepdims=True)


def _scan2(x, tri2):
    outs = []
    for h in range(2):
        xh = x[:, h * SB_T:(h + 1) * SB_T]
        hi = xh.astype(BF16)
        lo = (xh - hi.astype(F32)).astype(BF16)
        outs.append(_dot(jnp.concatenate([hi, lo], axis=1), tri2))
    return jnp.concatenate(outs, axis=1)


def _scan1(x, tri):
    xb = x.astype(BF16)
    return jnp.concatenate([_dot(xb[:, h * SB_T:(h + 1) * SB_T], tri) for h in range(2)], axis=1)


def _sb_fwd(qb, kb, vb, ublk, nbatch, seq):
    nq, nk = seq // SB_TQ, seq // SB_T
    T = nbatch * seq

    def body(q_ref, k_ref, v_ref, u_ref, o_ref, wst_ref, ost_ref, wbuf, obuf, sems):
        b, i = pl.program_id(0), pl.program_id(2)
        m0b = _lane_mask(SB_T, BF16)
        u = u_ref[...]
        pairs = [slice(pp * 128, (pp + 1) * 128) for pp in range(SB_NP)]
        neg_qs = [-q_ref[:, cols] for cols in pairs]

        def stores(j, slot):
            tix = (b * nq + i) * nk + j
            return (pltpu.make_async_copy(wbuf.at[slot], wst_ref.at[tix], sems.at[0, slot]),
                    pltpu.make_async_copy(obuf.at[slot], ost_ref.at[tix], sems.at[1, slot]))

        def tile(j, carries, kind, slot, wait):
            rows, mask = _sb_rows(kind), _sb_mask(kind)
            if wait:
                for cp in stores(j, slot):
                    cp.wait()
            out = []
            for pp, (cols, neg_q, (c0, c1, acc)) in enumerate(zip(pairs, neg_qs, carries)):
                kst = _pair_rows(k_ref, j, cols, m0b)
                vst = _pair_rows(v_ref, j, cols, m0b)
                nz, lb = _sb_logits(neg_q[rows], kst)
                if mask is not None:
                    lb = jnp.where(mask, lb, 0.0)
                incl = _scan2(lb, u)
                w = jnp.exp(incl - nz if kind == "B" else incl + _bcast2(c0, c1) - nz)
                if mask is not None:
                    w = jnp.where(mask, w, 0.0)
                wb = w.astype(BF16)
                wbuf[slot, pp, rows, :] = wb
                obuf[slot, pp, rows, :] = lb.astype(BF16)
                if kind == "B":
                    wbuf[slot, pp, 0:SB_T, :] = jnp.zeros((SB_T, 2 * SB_T), BF16)
                    obuf[slot, pp, 0:SB_T, :] = jnp.zeros((SB_T, 2 * SB_T), BF16)
                d0, d1, da = incl[:, 0:1], incl[:, SB_T:SB_T + 1], _dot(wb, vst)
                if kind == "B":
                    d0, d1, da = _lower(d0), _lower(d1), _lower(da)
                out.append((c0 + d0, c1 + d1, acc + da))
            for cp in stores(j, slot):
                cp.start()
            return tuple(out)

        zc = jnp.zeros((SB_TQ, 1), F32)
        carries = tile(2 * i + 1, ((zc, zc, jnp.zeros((SB_TQ, 128), F32)),) * SB_NP, "B", 0, False)
        carries = tile(2 * i, carries, "A", 1, False)

        def two(jj, cr):
            cr = tile(2 * i - 1 - 2 * jj, cr, "full", 0, True)
            return tile(2 * i - 2 - 2 * jj, cr, "full", 1, True)

        carries = lax.fori_loop(0, i, two, carries)
        for cols, carry in zip(pairs, carries):
            o_ref[:, cols] = carry[2].astype(BF16)
        for slot in range(2):
            for cp in stores(0, slot):
                cp.wait()

    wide = 128 * SB_NP
    blk = lambda b, g, i: (b * nq + i, g)
    full = lambda b, g, i: (b, g)
    hbm = pl.BlockSpec(memory_space=pl.ANY)
    tiles = jax.ShapeDtypeStruct((nbatch * nq * nk, SB_NP, SB_TQ, 2 * SB_T), BF16)
    return pl.pallas_call(
        body, name="sb_fwd", grid=(nbatch, 4 // SB_NP, nq),
        in_specs=[pl.BlockSpec((SB_TQ, wide), blk), pl.BlockSpec((seq, wide), full), pl.BlockSpec((seq, wide), full),
                  _const_spec((2 * SB_T, SB_T))],
        out_specs=[pl.BlockSpec((SB_TQ, wide), blk), hbm, hbm],
        out_shape=[jax.ShapeDtypeStruct((T, 512), BF16), tiles, tiles],
        scratch_shapes=[pltpu.VMEM((2, SB_NP, SB_TQ, 2 * SB_T), BF16), pltpu.VMEM((2, SB_NP, SB_TQ, 2 * SB_T), BF16),
                        pltpu.SemaphoreType.DMA((2, 2))],
        compiler_params=_params(3),
    )(qb, kb, vb, ublk)


def _sb_bwd(qb, kb, vb, dob, wst, ost, pblk, nbatch, seq):
    nq, nk = seq // SB_TQ, seq // SB_T
    T = nbatch * seq

    def body(q_ref, k_ref, v_ref, do_ref, wst_ref, ost_ref, up_ref, dq_ref, dk_ref, dv_ref, wbuf, obuf, sems):
        b, i = pl.program_id(0), pl.program_id(2)

        @pl.when(i == 0)
        def _():
            dk_ref[...] = jnp.zeros_like(dk_ref)
            dv_ref[...] = jnp.zeros_like(dv_ref)

        m0b = _lane_mask(SB_T, BF16)
        m0 = _lane_mask(SB_T) > 0.5
        up = up_ref[...]
        pairs = [slice(pp * 128, (pp + 1) * 128) for pp in range(SB_NP)]

        def loads(j, slot):
            tix = (b * nq + i) * nk + j
            return (pltpu.make_async_copy(wst_ref.at[tix], wbuf.at[slot], sems.at[0, slot]),
                    pltpu.make_async_copy(ost_ref.at[tix], obuf.at[slot], sems.at[1, slot]))

        def tile(j, carries, kind, slot, fetch_next):
            rows, mask = _sb_rows(kind), _sb_mask(kind)
            if fetch_next:
                for cp in loads(j + 1, 1 - slot):
                    cp.start()
            for cp in loads(j, slot):
                cp.wait()
            out = []
            for pp, (cols, (s0, s1, dq)) in enumerate(zip(pairs, carries)):
                q, do = q_ref[rows, cols], do_ref[rows, cols]
                kst = _pair_rows(k_ref, j, cols, m0b)
                vst = _pair_rows(v_ref, j, cols, m0b)
                wb = wbuf[slot, pp, rows, :]
                e = _dot_nt(do, vst) * wb.astype(F32)
                dlb = _bcast2(s0[rows], s1[rows]) + _scan1(e, up)
                dz = (e + dlb) * jnp.exp(obuf[slot, pp, rows, :].astype(F32)) - dlb
                if mask is not None:
                    dz = jnp.where(mask, dz, 0.0)
                dzb = dz.astype(BF16)
                dkst = _dot_tn(dzb, q)
                dvst = _dot_tn(wb, do)
                keys = pl.ds(pl.multiple_of(j * SB_T, SB_T), SB_T)
                dk_ref[keys, cols] = dk_ref[keys, cols] + jnp.where(m0, dkst[0:SB_T], dkst[SB_T:2 * SB_T])
                dv_ref[keys, cols] = dv_ref[keys, cols] + jnp.where(m0, dvst[0:SB_T], dvst[SB_T:2 * SB_T])
                x0, x1 = _rowsum2(e)
                ddq = _dot(dzb, kst)
                if kind == "B":
                    x0, x1, ddq = _lower(x0), _lower(x1), _lower(ddq)
                out.append((s0 + x0, s1 + x1, dq + ddq))
            return tuple(out)

        for cp in loads(0, 0):
            cp.start()

        def two(jj, cr):
            cr = tile(2 * jj, cr, "full", 0, True)
            return tile(2 * jj + 1, cr, "full", 1, True)

        zc = jnp.zeros((SB_TQ, 1), F32)
        carries = lax.fori_loop(0, i, two, ((zc, zc, jnp.zeros((SB_TQ, 128), F32)),) * SB_NP)
        carries = tile(2 * i, carries, "A", 0, True)
        carries = tile(2 * i + 1, carries, "B", 1, False)
        for cols, carry in zip(pairs, carries):
            dq_ref[:, cols] = carry[2] * SCALE

    wide = 128 * SB_NP
    blk = lambda b, g, i: (b * nq + i, g)
    full = lambda b, g, i: (b, g)
    hbm = pl.BlockSpec(memory_space=pl.ANY)
    return pl.pallas_call(
        body, name="sb_bwd", grid=(nbatch, 4 // SB_NP, nq),
        in_specs=[pl.BlockSpec((SB_TQ, wide), blk), pl.BlockSpec((seq, wide), full), pl.BlockSpec((seq, wide), full),
                  pl.BlockSpec((SB_TQ, wide), blk), hbm, hbm, _const_spec((SB_T, SB_T))],
        out_specs=[pl.BlockSpec((SB_TQ, wide), blk), pl.BlockSpec((seq, wide), full), pl.BlockSpec((seq, wide), full)],
        out_shape=[jax.ShapeDtypeStruct((T, 512), F32)] * 3,
        scratch_shapes=[pltpu.VMEM((2, SB_NP, SB_TQ, 2 * SB_T), BF16), pltpu.VMEM((2, SB_NP, SB_TQ, 2 * SB_T), BF16),
                        pltpu.SemaphoreType.DMA((2, 2))],
        compiler_params=_params(3),
    )(qb, kb, vb, dob, wst, ost, pblk)


def _sigmoid(g):
    return 1.0 / (1.0 + jnp.exp(-g))


def _out_proj(oa, ob, ga, gb, x2, tgt, wout_bf):
    T = x2.shape[0]

    def body(oa_ref, ob_ref, ga_ref, gb_ref, x_ref, t_ref, w_ref,
             dout_ref, doa_ref, dob_ref, dga_ref, dgb_ref, dw_ref, loss_ref):
        @pl.when(pl.program_id(0) == 0)
        def _():
            loss_ref[...] = jnp.zeros_like(loss_ref)
            dw_ref[...] = jnp.zeros_like(dw_ref)

        halves = ((oa_ref, ga_ref, doa_ref, dga_ref, 0), (ob_ref, gb_ref, dob_ref, dgb_ref, 512))
        out = x_ref[...]
        gated = []
        for o_ref, g_ref, _, _, lo in halves:
            g = g_ref[...]
            sg = _sigmoid(g)
            y = (o_ref[...] * (g * sg)).astype(BF16)
            out = out + _dot(y, w_ref[lo:lo + 512, :])
            gated.append((y, g, sg))
        diff = out - t_ref[...]
        dout = diff * (1.0 / D_MODEL)
        dout_ref[...] = dout
        loss_ref[...] = loss_ref[...] + jnp.sum(diff * diff) * (0.5 / D_MODEL)
        db = dout.astype(BF16)
        for (o_ref, _, do_ref, dg_ref, lo), (y, g, sg) in zip(halves, gated):
            dw_ref[lo:lo + 512, :] = dw_ref[lo:lo + 512, :] + _dot_tn(y, db)
            dy = _dot_nt(db, w_ref[lo:lo + 512, :])
            do_ref[...] = (dy * (g * sg)).astype(BF16)
            dg_ref[...] = (dy * o_ref[...] * (sg * (1.0 + g * (1.0 - sg)))).astype(BF16)

    rows = lambda w: pl.BlockSpec((TM, w), lambda i: (i, 0))
    return pl.pallas_call(
        body, name="out_proj", grid=(T // TM,),
        in_specs=[rows(512), rows(512), rows(512), rows(512), rows(D_MODEL), rows(D_MODEL),
                  _const_spec((D_MODEL, D_MODEL))],
        out_specs=[rows(D_MODEL)] + [rows(512)] * 4 + [_const_spec((D_MODEL, D_MODEL)), _const_spec((8, 128))],
        out_shape=[jax.ShapeDtypeStruct((T, D_MODEL), F32)] + [jax.ShapeDtypeStruct((T, 512), BF16)] * 4
        + [jax.ShapeDtypeStruct((D_MODEL, D_MODEL), F32), jax.ShapeDtypeStruct((8, 128), F32)],
        compiler_params=_params(1),
    )(oa, ob, ga, gb, x2, tgt, wout_bf)


def _qk_grad(qraw, kraw, dqrot, dk2, dv2, qg512, kg128, g512, cos, sa, sb):
    T = qraw.shape[0]

    def body(qraw_ref, kraw_ref, dqrot_ref, dk2_ref, dv2_ref, qg_ref, kg_ref, g_ref, cos_ref, sa_ref, sb_ref,
             dqa_ref, dkv_ref, dqg_ref, dkg_ref):
        @pl.when(pl.program_id(0) == 0)
        def _():
            dqg_ref[...] = jnp.zeros_like(dqg_ref)
            dkg_ref[...] = jnp.zeros_like(dkg_ref)

        cosv, sav, sbv = cos_ref[...], sa_ref[...], sb_ref[...]
        m0 = _lane_mask(TM) > 0.5

        def head_norm_bwd(raw, dn_rot, gmat, gain):
            r = lax.rsqrt(_dot_b(raw * raw, gmat) + EPS)
            n = raw * r
            dn = dn_rot * gain
            return r * (dn - n * _dot_b(dn * n, gmat)), jnp.sum(dn_rot * n, axis=0, keepdims=True)

        def fold(ref):
            a, b = ref[:, 0:128], ref[:, 128:256]
            return jnp.where(m0, a + pltpu.roll(a, 64, 1), b + pltpu.roll(b, 64, 1))

        dqn = jnp.concatenate([_rope_bwd(dqrot_ref[:, s * 128:(s + 1) * 128], cosv, sav, sbv) for s in range(4)], axis=1)
        dqa, dqg = head_norm_bwd(qraw_ref[...], dqn, g_ref[...], qg_ref[...])
        dka, dkg = head_norm_bwd(kraw_ref[...], _rope_bwd(fold(dk2_ref), cosv, sav, sbv), g_ref[0:128, 0:128], kg_ref[...])
        dqa_ref[...] = dqa.astype(BF16)
        dkv_ref[:, 0:128] = dka.astype(BF16)
        dkv_ref[:, 128:256] = fold(dv2_ref).astype(BF16)
        dqg_ref[...] = dqg_ref[...] + dqg
        dkg_ref[...] = dkg_ref[...] + dkg

    rows = lambda w: pl.BlockSpec((TM, w), lambda i: (i, 0))
    return pl.pallas_call(
        body, name="qk_grad", grid=(T // TM,),
        in_specs=[rows(512), rows(128), rows(512), rows(256), rows(256), _const_spec((1, 512)), _const_spec((1, 128)),
                  _const_spec((512, 512)), rows(128), rows(128), rows(128)],
        out_specs=[rows(512), rows(256), _const_spec((1, 512)), _const_spec((1, 128))],
        out_shape=[jax.ShapeDtypeStruct((T, 512), BF16), jax.ShapeDtypeStruct((T, 256), BF16),
                   jax.ShapeDtypeStruct((1, 512), F32), jax.ShapeDtypeStruct((1, 128), F32)],
        compiler_params=_params(1),
    )(qraw, kraw, dqrot, dk2, dv2, qg512, kg128, g512, cos, sa, sb)


_PIECES = ((QA, 512), (KA, 256), (GA, 512), (QB, 512), (KB, 512), (VB, 512), (GB, 512))


def _w_in_grad(h, pieces):
    T = h.shape[0]

    def body(h_ref, *refs):
        dw_ref = refs[-1]

        @pl.when(pl.program_id(0) == 0)
        def _():
            dw_ref[...] = jnp.zeros_like(dw_ref)

        hb = h_ref[...]
        for (lo, width), p_ref in zip(_PIECES, refs[:-1]):
            dw_ref[lo:lo + width, :] = dw_ref[lo:lo + width, :] + _dot_tn(p_ref[...].astype(BF16), hb)

    rows = lambda w: pl.BlockSpec((TM, w), lambda i: (i, 0))
    return pl.pallas_call(
        body, name="w_in_grad", grid=(T // TM,),
        in_specs=[rows(D_MODEL)] + [rows(w) for _, w in _PIECES],
        out_specs=_const_spec((IN_WIDTH, D_MODEL)),
        out_shape=jax.ShapeDtypeStruct((IN_WIDTH, D_MODEL), F32),
        compiler_params=_params(1),
    )(h, *pieces)


def _chip_exchange(src_ref, dst_ref, sems):
    _, _, c, chips = _place()
    return [_remote(src_ref.at[2 * cx + cy], dst_ref.at[j], sems, j, (cx, cy, c)) for j, (cx, cy) in enumerate(chips)]


def _x_grad(x2, dout, pieces, w_bf, ng, cin4):
    T = x2.shape[0]
    npc = len(_PIECES)
    steps = T // TM

    def body(x_ref, dout_ref, *refs):
        w_ref, ng_ref, cin_ref, gx_ref, dng_ref, rin_ref, ssem, rsem = refs[npc:]
        step = pl.program_id(0)
        copies = lambda: _chip_exchange(cin_ref, rin_ref, (ssem, rsem))

        @pl.when(step == 0)
        def _():
            dng_ref[...] = jnp.zeros_like(dng_ref)
            for cp in copies():
                cp.start()

        dh = jnp.zeros((TM, D_MODEL), F32)
        for (lo, width), p_ref in zip(_PIECES, refs[:npc]):
            dh = dh + _dot(p_ref[...].astype(BF16), w_ref[lo:lo + width, :])
        xb = x_ref[...]
        r = lax.rsqrt(jnp.mean(xb * xb, axis=-1, keepdims=True) + EPS)
        n = xb * r
        dn = dh * ng_ref[...]
        gx_ref[...] = dout_ref[...] + r * (dn - n * jnp.mean(dn * n, axis=-1, keepdims=True))
        dng_ref[...] = dng_ref[...] + jnp.sum(dh * n, axis=0, keepdims=True)

        @pl.when(step == steps - 1)
        def _():
            for cp in copies():
                cp.wait_recv()
            for cp in copies():
                cp.wait_send()

    rows = lambda w: pl.BlockSpec((TM, w), lambda i: (i, 0))
    hbm = pl.BlockSpec(memory_space=pl.ANY)
    return pl.pallas_call(
        body, name="x_grad", grid=(steps,),
        in_specs=[rows(D_MODEL), rows(D_MODEL)] + [rows(w) for _, w in _PIECES]
        + [_const_spec((IN_WIDTH, D_MODEL)), _const_spec((1, D_MODEL)), hbm],
        out_specs=[rows(D_MODEL), _const_spec((1, D_MODEL)), hbm],
        out_shape=[jax.ShapeDtypeStruct((T, D_MODEL), F32), jax.ShapeDtypeStruct((1, D_MODEL), F32),
                   jax.ShapeDtypeStruct((3,) + cin4.shape[1:], cin4.dtype)],
        scratch_shapes=[pltpu.SemaphoreType.DMA((3,)), pltpu.SemaphoreType.DMA((3,))],
        compiler_params=_params(1),
    )(x2, dout, *pieces, w_bf, ng, cin4)


HBM = pl.BlockSpec(memory_space=pl.ANY)


def _place():
    x, y, c = lax.axis_index("x"), lax.axis_index("y"), lax.axis_index("c")
    chips = [(1 - x, y), (x, 1 - y), (1 - x, 1 - y)]
    return x, y, c, chips


def _remote(src, dst, sems, k, to):
    return pltpu.make_async_remote_copy(src_ref=src, dst_ref=dst, send_sem=sems[0].at[k], recv_sem=sems[1].at[k],
                                        device_id=to, device_id_type=MESH)


def _gather_start(src, dst, sems):
    x, y, c, chips = _place()
    for j, (cx, cy) in enumerate(chips):
        _remote(src.at[c], dst.at[2 * x + y, c], sems, j, (cx, cy, c)).start()


def _gather_finish(src, dst, sems):
    x, y, c, chips = _place()
    sib = (x, y, 1 - c)
    fwds = []
    for j, (cx, cy) in enumerate(chips):
        k = 2 * cx + cy
        _remote(src.at[c], dst.at[k, c], sems, j, sib).wait_recv()
        fwds.append(_remote(dst.at[k, c], dst.at[k, c], sems, 3 + j, sib))
        fwds[-1].start()
    for j, (cx, cy) in enumerate(chips):
        _remote(src.at[c], dst.at[2 * cx + cy, 1 - c], sems, 3 + j, sib).wait_recv()
    for j, (cx, cy) in enumerate(chips):
        _remote(src.at[c], dst.at[2 * x + y, c], sems, j, (cx, cy, c)).wait_send()
    for cp in fwds:
        cp.wait_send()


def _own_slot(gathered, shard):
    me = 2 * lax.axis_index("x") + lax.axis_index("y")
    return lax.dynamic_update_slice(gathered, shard[None], (me, 0, 0, 0))


def _pair_exchange(g4):
    def body(src, dst, ssem, rsem):
        x, y, c, _ = _place()
        cp = _remote(src.at[:, pl.ds(1 - c, 1)], dst, (ssem, rsem), 0, (x, y, 1 - c))
        cp.start()
        cp.wait()

    return pl.pallas_call(
        body, name=f"pair_exchange_{g4.shape[2]}", in_specs=[HBM], out_specs=HBM,
        out_shape=jax.ShapeDtypeStruct((4, 1) + g4.shape[2:], g4.dtype),
        scratch_shapes=[pltpu.SemaphoreType.DMA((1,)), pltpu.SemaphoreType.DMA((1,))],
    )(g4)


def _share_halves(hin, hout, small):
    def body(hin_ref, hout_ref, small_ref, oin_ref, oout_ref, sall_ref, ssem, rsem, lsem):
        x, y, c, _ = _place()
        sems = (ssem, rsem)
        sib = (x, y, 1 - c)
        me = 4 * x + 2 * y + c
        own = pltpu.make_async_copy(small_ref, sall_ref.at[me], lsem.at[0])
        own.start()
        sent = [_remote(hin_ref, oin_ref, sems, 0, sib), _remote(hout_ref, oout_ref, sems, 1, sib)]
        flips = [(fx, fy, fc) for fx in (0, 1) for fy in (0, 1) for fc in (0, 1)][1:]
        for k, (fx, fy, fc) in enumerate(flips):
            sent.append(_remote(small_ref, sall_ref.at[me], sems, 2 + k, (x ^ fx, y ^ fy, c ^ fc)))
        for cp in sent:
            cp.start()
        _remote(hin_ref, oin_ref, sems, 0, sib).wait_recv()
        _remote(hout_ref, oout_ref, sems, 1, sib).wait_recv()
        for k, (fx, fy, fc) in enumerate(flips):
            src = 4 * (x ^ fx) + 2 * (y ^ fy) + (c ^ fc)
            _remote(small_ref, sall_ref.at[src], sems, 2 + k, sib).wait_recv()
        for cp in sent:
            cp.wait_send()
        own.wait()

    return pl.pallas_call(
        body, name="share_halves", in_specs=[HBM, HBM, HBM], out_specs=[HBM, HBM, HBM],
        out_shape=[jax.ShapeDtypeStruct(hin.shape, F32), jax.ShapeDtypeStruct(hout.shape, F32),
                   jax.ShapeDtypeStruct((8,) + small.shape, F32)],
        scratch_shapes=[pltpu.SemaphoreType.DMA((9,)), pltpu.SemaphoreType.DMA((9,)), pltpu.SemaphoreType.DMA((1,))],
    )(hin, hout, small)


def _add_half(cidx, full4, recv4):
    _, _, rows, width = full4.shape

    def body(c_ref, a_ref, b_ref, o_ref):
        o_ref[0] = (a_ref[0, 0] + b_ref[0, 0]).astype(BF16)

    return pl.pallas_call(
        body, name=f"add_half_{rows}",
        grid_spec=pltpu.PrefetchScalarGridSpec(
            num_scalar_prefetch=1, grid=(4,),
            in_specs=[pl.BlockSpec((1, 1, rows, width), lambda k, c: (k, c[0], 0, 0)),
                      pl.BlockSpec((1, 1, rows, width), lambda k, c: (k, 0, 0, 0))],
            out_specs=pl.BlockSpec((1, rows, width), lambda k, c: (k, 0, 0))),
        out_shape=jax.ShapeDtypeStruct((4, rows, width), BF16),
        compiler_params=_params(1),
    )(cidx, full4, recv4)


def _sum_chips(chip, own4, recv3):
    _, rows, width = recv3.shape
    rb = rows // 2

    def body(k_ref, a_ref, r_ref, o_ref):
        acc = a_ref[0].astype(F32)
        for s in range(3):
            acc = acc + r_ref[s].astype(F32)
        o_ref[...] = acc

    return pl.pallas_call(
        body, name=f"sum_chips_{rows}",
        grid_spec=pltpu.PrefetchScalarGridSpec(
            num_scalar_prefetch=1, grid=(rows // rb,),
            in_specs=[pl.BlockSpec((1, rb, width), lambda i, k: (k[0], i, 0)),
                      pl.BlockSpec((3, rb, width), lambda i, k: (0, i, 0))],
            out_specs=pl.BlockSpec((rb, width), lambda i, k: (i, 0))),
        out_shape=jax.ShapeDtypeStruct((rows, width), F32),
        compiler_params=_params(1),
    )(chip, own4, recv3)


def _adam_math(w, g, m, v):
    c1 = 1.0 - ADAM_B1 ** ADAM_STEP
    c2 = 1.0 - ADAM_B2 ** ADAM_STEP
    nm = ADAM_B1 * m + (1.0 - ADAM_B1) * g
    nv = ADAM_B2 * v + (1.0 - ADAM_B2) * (g * g)
    return -ADAM_LR * ((nm / c1) / (jnp.sqrt(nv / c2) + ADAM_EPS) + ADAM_WD * w), nm, nv


def _small_update(small_all, params):
    n = len(params)
    flat = [a for p in params for a in p]

    def body(s_ref, *refs):
        ins, loss_ref, outs = refs[:3 * n], refs[3 * n], refs[3 * n + 1:]
        total = s_ref[0]
        for d in range(1, 8):
            total = total + s_ref[d]
        loss_ref[...] = total[n:n + 1, 0:1]
        for r in range(n):
            w_ref, m_ref, v_ref = ins[3 * r:3 * r + 3]
            g = total[r:r + 1, 0:w_ref.shape[1]]
            outs[4 * r][...] = g
            outs[4 * r + 1][...], outs[4 * r + 2][...], outs[4 * r + 3][...] = _adam_math(w_ref[...], g, m_ref[...], v_ref[...])

    whole = lambda a: pl.BlockSpec(a.shape, lambda i: (0,) * a.ndim)
    out_shape = [jax.ShapeDtypeStruct((1, 1), F32)] + [jax.ShapeDtypeStruct(p[0].shape, F32) for p in params for _ in range(4)]
    res = pl.pallas_call(
        body, name="small_update", grid=(1,),
        in_specs=[whole(small_all)] + [whole(a) for a in flat], out_specs=[whole(s) for s in out_shape],
        out_shape=out_shape, compiler_params=_params(1),
    )(small_all, *flat)
    return res[0], [tuple(res[1 + 4 * r:5 + 4 * r]) for r in range(n)]


def _adamw_halves(cidx, w, own, recv, m, v):
    rows, width = w.shape
    rb = rows // 4

    def body(c_ref, w_ref, own_ref, recv_ref, m_ref, v_ref, g_ref, d_ref, nm_ref, nv_ref):
        mine = (pl.program_id(0) // 2) == c_ref[0]
        g = jnp.where(mine, own_ref[...], recv_ref[...])
        g_ref[...] = g
        d_ref[...], nm_ref[...], nv_ref[...] = _adam_math(w_ref[...], g, m_ref[...], v_ref[...])

    full = pl.BlockSpec((rb, width), lambda i, c: (i, 0))
    half = pl.BlockSpec((rb, width), lambda i, c: (i % 2, 0))
    return pl.pallas_call(
        body, name=f"adamw_halves_{rows}",
        grid_spec=pltpu.PrefetchScalarGridSpec(
            num_scalar_prefetch=1, grid=(4,),
            in_specs=[full, half, half, full, full], out_specs=[full] * 4),
        out_shape=[jax.ShapeDtypeStruct((rows, width), F32)] * 4,
        compiler_params=_params(1),
    )(cidx, w, own, recv, m, v)


def _constants():
    idx = jnp.arange(512)
    g512 = jnp.where(idx[:, None] // HEAD_DIM == idx[None, :] // HEAD_DIM, 1.0 / HEAD_DIM, 0.0).astype(BF16)
    j = jnp.arange(SB_T)
    ublk = jnp.where(j[:, None] >= j[None, :], 1.0, 0.0).astype(BF16)
    pblk = jnp.where(j[:, None] < j[None, :], 1.0, 0.0).astype(BF16)
    return g512, jnp.concatenate([ublk, ublk], axis=0), pblk


def _pad_rows(v, width):
    return jnp.pad(v, ((0, 0), (0, width - v.shape[1])))


def _pair_sum(cidx, partial, rows):
    g4 = partial.reshape(4, 2, rows, D_MODEL)
    return _add_half(cidx, g4, _pair_exchange(g4))


def _step(x2, tgt, positions, norm_gain, q_norm_gain, k_norm_gain, sinks, win_shard, wout_shard, nbatch, seq):
    g512, ublk, pblk = _constants()
    qg512 = jnp.tile(q_norm_gain, (1, 8))
    kg128 = jnp.tile(k_norm_gain, (1, 2))
    sink1 = sinks.reshape(8)
    cidx = lax.axis_index("c").astype(jnp.int32).reshape(1)
    chip = (2 * lax.axis_index("x") + lax.axis_index("y")).astype(jnp.int32).reshape(1)

    h, (cos, sa, sb), w_bf = _prologue(x2, norm_gain, positions.reshape(-1, 1), win_shard)
    (qraw, kraw, qrot, k2, v2, ga, qb, kb, vb, gb), wout_bf = _norm_proj(
        h, w_bf, qg512, kg128, g512, cos, sa, sb, wout_shard)
    oa = _swa_fwd(sink1, qrot, k2, v2, nbatch, seq)
    ob, wst, ost = _sb_fwd(qb, kb, vb, ublk, nbatch, seq)
    dout, doa, dob, dga, dgb, dwout, loss_acc = _out_proj(oa, ob, ga, gb, x2, tgt, wout_bf)

    cout4 = _pair_sum(cidx, dwout, SHARD_OUT // 2)
    dqrot, dk2, dv2, dsink, rout3 = _swa_bwd(sink1, qrot, k2, v2, doa, cout4, nbatch, seq)
    dqb, dkb, dvb = _sb_bwd(qb, kb, vb, dob, wst, ost, pblk, nbatch, seq)
    dk2, dv2 = _unshift(dk2, nbatch, seq), _unshift(dv2, nbatch, seq)
    dqa, dkv, dqg, dkg = _qk_grad(qraw, kraw, dqrot, dk2, dv2, qg512, kg128, g512, cos, sa, sb)
    pieces = (dqa, dkv, dga, dqb, dkb, dvb, dgb)
    cin4 = _pair_sum(cidx, _w_in_grad(h, pieces), SHARD_IN // 2)
    gx, dng, rin3 = _x_grad(x2, dout, pieces, w_bf, norm_gain, cin4)
    own_in, own_out = _sum_chips(chip, cin4, rin3), _sum_chips(chip, cout4, rout3)

    dqg64 = dqg.reshape(8, HEAD_DIM).sum(axis=0, keepdims=True)
    dkg64 = dkg.reshape(2, HEAD_DIM).sum(axis=0, keepdims=True)
    small = jnp.concatenate([dng, _pad_rows(dqg64, D_MODEL), _pad_rows(dkg64, D_MODEL),
                             _pad_rows(dsink[:, 0].reshape(1, 8), D_MODEL), _pad_rows(loss_acc[0:1, 0:1], D_MODEL),
                             jnp.zeros((3, D_MODEL), F32)], axis=0)
    sib_in, sib_out, small_all = _share_halves(own_in, own_out, small)
    return gx, cidx, (own_in, sib_in), (own_out, sib_out), small_all


def kernel(x, positions, norm_gain, w_in, q_norm_gain, k_norm_gain, sinks, w_out, loss_target, m_norm_gain, m_w_in, m_q_norm_gain, m_k_norm_gain, m_sinks, m_w_out, v_norm_gain, v_w_in, v_q_norm_gain, v_k_norm_gain, v_sinks, v_w_out):
    nbatch, seq, _ = x.shape
    T = nbatch * seq
    x2 = x.reshape(T, D_MODEL)
    tgt = loss_target.reshape(T, D_MODEL)
    tr = lambda a: jnp.swapaxes(a[0], 0, 1)
    win_t, m_win_t, v_win_t = tr(w_in), tr(m_w_in), tr(v_w_in)

    win_shard = win_t.astype(BF16).reshape(2, SHARD_IN // 2, D_MODEL)
    wout_shard = w_out[0].astype(BF16).reshape(2, SHARD_OUT // 2, D_MODEL)
    gx, cidx, g_in_halves, g_out_halves, small_all = _step(
        x2, tgt, positions, norm_gain, q_norm_gain, k_norm_gain, sinks, win_shard, wout_shard, nbatch, seq)

    g_in, d_in, nm_in, nv_in = [jnp.swapaxes(a, 0, 1) for a in
                                _adamw_halves(cidx, win_t, *g_in_halves, m_win_t, v_win_t)]
    g_out, d_out, nm_out, nv_out = _adamw_halves(cidx, w_out[0], *g_out_halves, m_w_out[0], v_w_out[0])
    loss, small = _small_update(small_all, [(norm_gain, m_norm_gain, v_norm_gain), (q_norm_gain, m_q_norm_gain, v_q_norm_gain),
                                            (k_norm_gain, m_k_norm_gain, v_k_norm_gain), (sinks, m_sinks, v_sinks)])
    (g_ng, d_ng, m_ng, v_ng), (g_qg, d_qg, m_qg, v_qg), (g_kg, d_kg, m_kg, v_kg), (g_sk, d_sk, m_sk, v_sk) = small
    return (loss.reshape(()), gx.reshape(nbatch, seq, D_MODEL),
            g_ng, g_in[None], g_qg, g_kg, g_sk, g_out[None],
            d_ng, d_in[None], d_qg, d_kg, d_sk, d_out[None],
            m_ng, nm_in[None], m_qg, m_kg, m_sk, nm_out[None],
            v_ng, nv_in[None], v_qg, v_kg, v_sk, nv_out[None])
```

```python
import functools
import math

import jax
import jax.numpy as jnp
from jax import lax
from jax.experimental import pallas as pl
from jax.experimental.pallas import tpu as pltpu

F32 = jnp.float32
BF16 = jnp.bfloat16

D_MODEL = 1024
HEAD_DIM = 64
BLOCK = 128
ROPE_THETA = 10000.0
EPS = 1e-6
QA, KA, VA, GA, QB, KB, VB, GB = 0, 512, 640, 768, 1280, 1792, 2304, 2816
IN_WIDTH = 3328
SHARD_IN = IN_WIDTH // 4
SHARD_OUT = D_MODEL // 4
SCALE = 1.0 / math.sqrt(HEAD_DIM)
NEG = -1e30

ADAM_LR, ADAM_B1, ADAM_B2, ADAM_EPS, ADAM_WD, ADAM_STEP = 0.001, 0.9, 0.999, 1e-08, 0.01, 10

TM = 512
VMEM_LIMIT = 56 * 1024 * 1024
MESH = pl.DeviceIdType.MESH


def _dot(a, b):
    return jnp.dot(a, b, preferred_element_type=F32)


def _dot_nt(a, b):
    return lax.dot_general(a, b, (((1,), (1,)), ((), ())), preferred_element_type=F32)


def _dot_tn(a, b):
    return lax.dot_general(a, b, (((0,), (0,)), ((), ())), preferred_element_type=F32)


def _dot_b(a, m):
    return _dot(a.astype(BF16), m)


def _params(n_axes=None, vmem=VMEM_LIMIT):
    sem = None if n_axes is None else ("arbitrary",) * n_axes
    return pltpu.CompilerParams(dimension_semantics=sem, vmem_limit_bytes=vmem)


def _const_spec(shape):
    nd = len(shape)
    return pl.BlockSpec(shape, lambda *_: (0,) * nd)


def _rope_fwd(x, cos, sa, sb):
    return x * cos + pltpu.roll(x, 96, 1) * sa + pltpu.roll(x, 32, 1) * sb


def _rope_bwd(d, cos, sa, sb):
    return d * cos - pltpu.roll(d, 96, 1) * sa - pltpu.roll(d, 32, 1) * sb


def _lane_mask(rows, dtype=F32):
    lane = lax.broadcasted_iota(jnp.int32, (rows, 128), 1)
    return jnp.where(lane < HEAD_DIM, 1.0, 0.0).astype(dtype)


def _prologue(x2, ng, pos, win_shard):
    T = x2.shape[0]
    steps = T // TM
    half = HEAD_DIM // 2
    inv_freq = jnp.tile(ROPE_THETA ** (-jnp.arange(half, dtype=F32) * 2.0 / HEAD_DIM), 4).reshape(1, 128)

    def body(x_ref, ng_ref, pos_ref, freq_ref, wi_ref, h_ref, cos_ref, sa_ref, sb_ref, ai_ref, ssem, rsem):
        @pl.when(pl.program_id(0) == 0)
        def _():
            _gather_start(wi_ref, ai_ref, (ssem, rsem))

        xb = x_ref[...]
        r = lax.rsqrt(jnp.mean(xb * xb, axis=-1, keepdims=True) + EPS)
        h_ref[...] = (xb * r * ng_ref[...]).astype(BF16)
        ang = pos_ref[...].astype(F32) * freq_ref[...]
        first = (lax.broadcasted_iota(jnp.int32, (TM, 128), 1) & (HEAD_DIM - 1)) < half
        sn = jnp.sin(ang)
        cos_ref[...] = jnp.cos(ang)
        sa_ref[...] = jnp.where(first, -sn, 0.0)
        sb_ref[...] = jnp.where(first, 0.0, sn)

        @pl.when(pl.program_id(0) == steps - 1)
        def _():
            _gather_finish(wi_ref, ai_ref, (ssem, rsem))

    rows = lambda w: pl.BlockSpec((TM, w), lambda i: (i, 0))
    hbm = pl.BlockSpec(memory_space=pl.ANY)
    h, cos, sa, sb, gathered = pl.pallas_call(
        body, name="prologue", grid=(steps,),
        in_specs=[rows(D_MODEL), _const_spec((1, D_MODEL)), rows(1), _const_spec((1, 128)), hbm],
        out_specs=[rows(D_MODEL), rows(128), rows(128), rows(128), hbm],
        out_shape=[jax.ShapeDtypeStruct((T, D_MODEL), BF16)] + [jax.ShapeDtypeStruct((T, 128), F32)] * 3
        + [jax.ShapeDtypeStruct((4,) + win_shard.shape, win_shard.dtype)],
        scratch_shapes=[pltpu.SemaphoreType.DMA((9,)), pltpu.SemaphoreType.DMA((9,))],
        compiler_params=_params(1),
    )(x2, ng, pos, inv_freq, win_shard)
    return h, (cos, sa, sb), _own_slot(gathered, win_shard).reshape(IN_WIDTH, D_MODEL)


def _norm_proj(h, w_bf, qg512, kg128, g512, cos, sa, sb, wout_shard):
    T = h.shape[0]
    steps = T // TM

    def body(h_ref, w_ref, qg_ref, kg_ref, g_ref, cos_ref, sa_ref, sb_ref, wo_ref,
             qraw_ref, kraw_ref, qrot_ref, k2_ref, v2_ref, ga_ref, qb_ref, kb_ref, vb_ref, gb_ref, ao_ref,
             ssem, rsem):
        @pl.when(pl.program_id(0) == 0)
        def _():
            _gather_start(wo_ref, ao_ref, (ssem, rsem))

        h = h_ref[...]
        cosv, sav, sbv = cos_ref[...], sa_ref[...], sb_ref[...]
        m0 = _lane_mask(TM)

        def dup(v):
            v0 = v * m0
            v1 = v - v0
            return v0 + pltpu.roll(v0, 64, 1), v1 + pltpu.roll(v1, 64, 1)

        qa = _dot_nt(h, w_ref[QA:KA, :])
        qraw_ref[...] = qa
        qn = qa * lax.rsqrt(_dot_b(qa * qa, g_ref[...]) + EPS) * qg_ref[...]
        for s in range(4):
            qs = _rope_fwd(qn[:, s * 128:(s + 1) * 128], cosv, sav, sbv)
            qrot_ref[:, s * 128:(s + 1) * 128] = (qs * SCALE).astype(BF16)
        ka = _dot_nt(h, w_ref[KA:VA, :])
        kraw_ref[...] = ka
        kn = ka * lax.rsqrt(_dot_b(ka * ka, g_ref[0:128, 0:128]) + EPS) * kg_ref[...]
        k0, k1 = dup(_rope_fwd(kn, cosv, sav, sbv))
        k2_ref[:, 0:128] = k0.astype(BF16)
        k2_ref[:, 128:256] = k1.astype(BF16)
        v0, v1 = dup(_dot_nt(h, w_ref[VA:GA, :]))
        v2_ref[:, 0:128] = v0.astype(BF16)
        v2_ref[:, 128:256] = v1.astype(BF16)
        ga_ref[...] = _dot_nt(h, w_ref[GA:QB, :])
        qb_ref[...] = (_dot_nt(h, w_ref[QB:KB, :]) * SCALE).astype(BF16)
        kb_ref[...] = _dot_nt(h, w_ref[KB:VB, :]).astype(BF16)
        vb_ref[...] = _dot_nt(h, w_ref[VB:GB, :]).astype(BF16)
        gb_ref[...] = _dot_nt(h, w_ref[GB:IN_WIDTH, :])

        @pl.when(pl.program_id(0) == steps - 1)
        def _():
            _gather_finish(wo_ref, ao_ref, (ssem, rsem))

    def rows(w):
        return pl.BlockSpec((TM, w), lambda i: (i, 0))

    outs = [(512, F32), (128, F32), (512, BF16), (256, BF16), (256, BF16), (512, F32),
            (512, BF16), (512, BF16), (512, BF16), (512, F32)]
    hbm = pl.BlockSpec(memory_space=pl.ANY)
    res = pl.pallas_call(
        body, name="norm_proj", grid=(steps,),
        in_specs=[rows(D_MODEL), _const_spec((IN_WIDTH, D_MODEL)), _const_spec((1, 512)),
                  _const_spec((1, 128)), _const_spec((512, 512)), rows(128), rows(128), rows(128), hbm],
        out_specs=[rows(w) for w, _ in outs] + [hbm],
        out_shape=[jax.ShapeDtypeStruct((T, w), dt) for w, dt in outs]
        + [jax.ShapeDtypeStruct((4,) + wout_shard.shape, wout_shard.dtype)],
        scratch_shapes=[pltpu.SemaphoreType.DMA((9,)), pltpu.SemaphoreType.DMA((9,))],
        compiler_params=_params(1),
    )(h, w_bf, qg512, kg128, g512, cos, sa, sb, wout_shard)
    return res[:-1], _own_slot(res[-1], wout_shard).reshape(D_MODEL, D_MODEL)


SWA_Q = 512
SWA_SUB = SWA_Q // BLOCK


def _swa_scores(q, kst, mask, sink_ref, kv):
    s_all = _dot_nt(q, kst)
    first = lax.broadcasted_iota(jnp.int32, (256, 1), 0) < 128
    probs, stats = [], []
    for hh in range(2):
        sink = jnp.where(first, sink_ref[kv * 4 + hh], sink_ref[kv * 4 + 2 + hh])
        s = jnp.where(mask, s_all[:, hh * 256:(hh + 1) * 256], NEG)
        m = jnp.maximum(jnp.max(s, axis=1, keepdims=True), sink)
        pe = jnp.exp(s - m)
        inv = 1.0 / (jnp.sum(pe, axis=1, keepdims=True) + jnp.exp(sink - m))
        probs.append(pe * inv)
        stats.append(jnp.exp(sink - m) * inv)
    return probs, stats


def _swa_mask(has_prev):
    r = lax.broadcasted_iota(jnp.int32, (256, 256), 0) & 127
    c = lax.broadcasted_iota(jnp.int32, (256, 256), 1)
    band = (c > r) & (c <= r + 128)
    return band if has_prev is True else band & ((c >= 128) | has_prev)


def _stack_pairs(ref, rows, kv):
    return jnp.concatenate([ref[rows, (2 * kv) * 128:(2 * kv + 1) * 128], ref[rows, (2 * kv + 1) * 128:(2 * kv + 2) * 128]], axis=0)


def _swa_keys(prev_ref, main_ref, s, kv, m0b):
    cols = slice(kv * 128, (kv + 1) * 128)
    prev = prev_ref[:, cols] if s == 0 else main_ref[(s - 1) * 128:s * 128, cols]
    kc = jnp.concatenate([prev, main_ref[s * 128:(s + 1) * 128, cols]], axis=0)
    k0 = kc * m0b
    return jnp.concatenate([k0, kc - k0], axis=0)


def _swa_fwd(sinks, qrot, k2, v2, nbatch, seq):
    ni = seq // SWA_Q
    T = nbatch * seq

    def body(sink_ref, q_ref, kp_ref, km_ref, vp_ref, vm_ref, o_ref):
        i = pl.program_id(1)
        m0b = _lane_mask(256, BF16)
        for s in range(SWA_SUB):
            mask = _swa_mask(True if s else i > 0)
            rows = slice(s * 128, (s + 1) * 128)
            for kv in range(2):
                kst = _swa_keys(kp_ref, km_ref, s, kv, m0b)
                vst = _swa_keys(vp_ref, vm_ref, s, kv, m0b)
                probs, _ = _swa_scores(_stack_pairs(q_ref, rows, kv), kst, mask, sink_ref, kv)
                o2 = _dot(jnp.concatenate(probs, axis=1).astype(BF16), vst)
                o_ref[rows, kv * 256:kv * 256 + 128] = o2[0:128].astype(BF16)
                o_ref[rows, kv * 256 + 128:(kv + 1) * 256] = o2[128:256].astype(BF16)

    main = lambda b, i: (b * ni + i, 0)
    prev = lambda b, i: ((b * ni + i) * SWA_SUB - jnp.where(i > 0, 1, 0), 0)
    return pl.pallas_call(
        body, name="swa_fwd", grid=(nbatch, ni),
        in_specs=[pl.BlockSpec(memory_space=pltpu.SMEM), pl.BlockSpec((SWA_Q, 512), main),
                  pl.BlockSpec((128, 256), prev), pl.BlockSpec((SWA_Q, 256), main),
                  pl.BlockSpec((128, 256), prev), pl.BlockSpec((SWA_Q, 256), main)],
        out_specs=pl.BlockSpec((SWA_Q, 512), main),
        out_shape=jax.ShapeDtypeStruct((T, 512), BF16),
        compiler_params=_params(2),
    )(sinks, qrot, k2, k2, v2, v2)


def _swa_bwd(sinks, qrot, k2, v2, doa, cout4, nbatch, seq):
    ni = seq // SWA_Q
    T = nbatch * seq

    def body(sink_ref, q_ref, kp_ref, km_ref, vp_ref, vm_ref, do_ref, cout_ref,
             dq_ref, dk_ref, dv_ref, ds_ref, rout_ref, dkc, dvc, ssem, rsem):
        b, i = pl.program_id(0), pl.program_id(1)
        copies = lambda: _chip_exchange(cout_ref, rout_ref, (ssem, rsem))

        @pl.when((b == 0) & (i == 0))
        def _():
            ds_ref[...] = jnp.zeros_like(ds_ref)
            for cp in copies():
                cp.start()

        @pl.when((b == nbatch - 1) & (i == ni))
        def _():
            for cp in copies():
                cp.wait_recv()
            for cp in copies():
                cp.wait_send()

        @pl.when(i == 0)
        def _():
            dkc[...] = jnp.zeros_like(dkc)
            dvc[...] = jnp.zeros_like(dvc)

        @pl.when(i < ni)
        def _():
            m0b = _lane_mask(256, BF16)
            m0 = _lane_mask(128) > 0.5
            for kv in range(2):
                kcols = slice(kv * 128, (kv + 1) * 128)
                dk_own, dv_own = dkc[:, kcols], dvc[:, kcols]
                for s in range(SWA_SUB):
                    mask = _swa_mask(True if s else i > 0)
                    rows = slice(s * 128, (s + 1) * 128)
                    kst = _swa_keys(kp_ref, km_ref, s, kv, m0b)
                    vst = _swa_keys(vp_ref, vm_ref, s, kv, m0b)
                    q, do = _stack_pairs(q_ref, rows, kv), _stack_pairs(do_ref, rows, kv)
                    probs, psink = _swa_scores(q, kst, mask, sink_ref, kv)
                    dp_all = _dot_nt(do, vst)
                    ds_parts = []
                    for hh in range(2):
                        dp = dp_all[:, hh * 256:(hh + 1) * 256]
                        delta = jnp.sum(probs[hh] * dp, axis=1, keepdims=True)
                        ds_parts.append(probs[hh] * (dp - delta))
                        dsink = psink[hh] * delta
                        for pr in range(2):
                            h = kv * 4 + pr * 2 + hh
                            ds_ref[h:h + 1, :] = ds_ref[h:h + 1, :] - jnp.sum(dsink[pr * 128:(pr + 1) * 128])
                    ds_all = jnp.concatenate(ds_parts, axis=1).astype(BF16)
                    p_all = jnp.concatenate(probs, axis=1).astype(BF16)
                    dq2 = _dot(ds_all, kst) * SCALE
                    dq_ref[rows, kv * 256:kv * 256 + 128] = dq2[0:128]
                    dq_ref[rows, kv * 256 + 128:(kv + 1) * 256] = dq2[128:256]
                    dkst = _dot_tn(ds_all, q)
                    dvst = _dot_tn(p_all, do)
                    dk_ref[rows, kcols] = dk_own + jnp.where(m0, dkst[0:128], dkst[256:384])
                    dv_ref[rows, kcols] = dv_own + jnp.where(m0, dvst[0:128], dvst[256:384])
                    dk_own = jnp.where(m0, dkst[128:256], dkst[384:512])
                    dv_own = jnp.where(m0, dvst[128:256], dvst[384:512])
                dkc[:, kcols] = dk_own
                dvc[:, kcols] = dv_own

        @pl.when(i == ni)
        def _():
            dk_ref[...] = jnp.zeros_like(dk_ref)
            dv_ref[...] = jnp.zeros_like(dv_ref)
            dk_ref[0:128, :] = dkc[...]
            dv_ref[0:128, :] = dvc[...]

    main = lambda b, i: (b * ni + jnp.minimum(i, ni - 1), 0)
    prev = lambda b, i: ((b * ni + jnp.minimum(i, ni - 1)) * SWA_SUB - jnp.where(jnp.minimum(i, ni - 1) > 0, 1, 0), 0)
    shifted = lambda b, i: (b * (ni + 1) + i, 0)
    tpad = nbatch * (ni + 1) * SWA_Q
    return pl.pallas_call(
        body, name="swa_bwd", grid=(nbatch, ni + 1),
        in_specs=[pl.BlockSpec(memory_space=pltpu.SMEM), pl.BlockSpec((SWA_Q, 512), main),
                  pl.BlockSpec((128, 256), prev), pl.BlockSpec((SWA_Q, 256), main),
                  pl.BlockSpec((128, 256), prev), pl.BlockSpec((SWA_Q, 256), main),
                  pl.BlockSpec((SWA_Q, 512), main), pl.BlockSpec(memory_space=pl.ANY)],
        out_specs=[pl.BlockSpec((SWA_Q, 512), main), pl.BlockSpec((SWA_Q, 256), shifted),
                   pl.BlockSpec((SWA_Q, 256), shifted), _const_spec((8, 128)), pl.BlockSpec(memory_space=pl.ANY)],
        out_shape=[jax.ShapeDtypeStruct((T, 512), F32), jax.ShapeDtypeStruct((tpad, 256), F32),
                   jax.ShapeDtypeStruct((tpad, 256), F32), jax.ShapeDtypeStruct((8, 128), F32),
                   jax.ShapeDtypeStruct((3,) + cout4.shape[1:], cout4.dtype)],
        scratch_shapes=[pltpu.VMEM((128, 256), F32), pltpu.VMEM((128, 256), F32),
                        pltpu.SemaphoreType.DMA((3,)), pltpu.SemaphoreType.DMA((3,))],
        compiler_params=_params(2),
    )(sinks, qrot, k2, k2, v2, v2, doa, cout4)


def _unshift(dkpad, nbatch, seq):
    return dkpad.reshape(nbatch, seq + SWA_Q, 256)[:, BLOCK:BLOCK + seq].reshape(nbatch * seq, 256)


SB_T = 256
SB_TQ = 2 * SB_T


def _sb_mask(kind):
    if kind == "full":
        return None
    rows = SB_T if kind == "B" else SB_TQ
    r = lax.broadcasted_iota(jnp.int32, (rows, 2 * SB_T), 0)
    c = lax.broadcasted_iota(jnp.int32, (rows, 2 * SB_T), 1)
    causal = (c & (SB_T - 1)) < r
    return causal | (r >= SB_T) if kind == "A" else causal


def _sb_rows(kind):
    return slice(SB_T, SB_TQ) if kind == "B" else slice(0, SB_TQ)


def _lower(x):
    return jnp.concatenate([jnp.zeros_like(x), x], axis=0)


def _sb_logits(neg_q, kst):
    nz = _dot_nt(neg_q, kst)
    sign = jnp.uint32(0x80000000)
    neg_abs = lax.bitcast_convert_type(lax.bitcast_convert_type(nz, jnp.uint32) | sign, F32)
    return nz, jnp.minimum(nz, 0.0) - jnp.log(1.0 + jnp.exp(neg_abs))


SB_NP = 4


def _pair_rows(ref, j, cols, m0b):
    kj = ref[pl.ds(pl.multiple_of(j * SB_T, SB_T), SB_T), cols]
    k0 = kj * m0b
    return jnp.concatenate([k0, kj - k0], axis=0)


def _bcast2(c0, c1):
    rows = c0.shape[0]
    return jnp.concatenate([jnp.broadcast_to(c0, (rows, SB_T)), jnp.broadcast_to(c1, (rows, SB_T))], axis=1)


def _rowsum2(x):
    return jnp.sum(x[:, 0:SB_T], axis=1, keepdims=True), jnp.sum(x[:, SB_T:2 * SB_T], axis=1, keepdims=True)


def _scan2(x, tri2):
    outs = []
    for h in range(2):
        xh = x[:, h * SB_T:(h + 1) * SB_T]
        hi = xh.astype(BF16)
        lo = (xh - hi.astype(F32)).astype(BF16)
        outs.append(_dot(jnp.concatenate([hi, lo], axis=1), tri2))
    return jnp.concatenate(outs, axis=1)


def _scan1(x, tri):
    xb = x.astype(BF16)
    return jnp.concatenate([_dot(xb[:, h * SB_T:(h + 1) * SB_T], tri) for h in range(2)], axis=1)


def _sb_fwd(qb, kb, vb, ublk, nbatch, seq):
    nq, nk = seq // SB_TQ, seq // SB_T
    T = nbatch * seq

    def body(q_ref, k_ref, v_ref, u_ref, o_ref, wst_ref, ost_ref, wbuf, obuf, sems):
        b, i = pl.program_id(0), pl.program_id(2)
        m0b = _lane_mask(SB_T, BF16)
        u = u_ref[...]
        pairs = [slice(pp * 128, (pp + 1) * 128) for pp in range(SB_NP)]
        neg_qs = [-q_ref[:, cols] for cols in pairs]

        def stores(j, slot):
            tix = (b * nq + i) * nk + j
            return (pltpu.make_async_copy(wbuf.at[slot], wst_ref.at[tix], sems.at[0, slot]),
                    pltpu.make_async_copy(obuf.at[slot], ost_ref.at[tix], sems.at[1, slot]))

        def tile(j, carries, kind, slot, wait):
            rows, mask = _sb_rows(kind), _sb_mask(kind)
            if wait:
                for cp in stores(j, slot):
                    cp.wait()
            out = []
            for pp, (cols, neg_q, (c0, c1, acc)) in enumerate(zip(pairs, neg_qs, carries)):
                kst = _pair_rows(k_ref, j, cols, m0b)
                vst = _pair_rows(v_ref, j, cols, m0b)
                nz, lb = _sb_logits(neg_q[rows], kst)
                if mask is not None:
                    lb = jnp.where(mask, lb, 0.0)
                incl = _scan2(lb, u)
                w = jnp.exp(incl - nz if kind == "B" else incl + _bcast2(c0, c1) - nz)
                if mask is not None:
                    w = jnp.where(mask, w, 0.0)
                wb = w.astype(BF16)
                wbuf[slot, pp, rows, :] = wb
                obuf[slot, pp, rows, :] = lb.astype(BF16)
                if kind == "B":
                    wbuf[slot, pp, 0:SB_T, :] = jnp.zeros((SB_T, 2 * SB_T), BF16)
                    obuf[slot, pp, 0:SB_T, :] = jnp.zeros((SB_T, 2 * SB_T), BF16)
                d0, d1, da = incl[:, 0:1], incl[:, SB_T:SB_T + 1], _dot(wb, vst)
                if kind == "B":
                    d0, d1, da = _lower(d0), _lower(d1), _lower(da)
                out.append((c0 + d0, c1 + d1, acc + da))
            for cp in stores(j, slot):
                cp.start()
            return tuple(out)

        zc = jnp.zeros((SB_TQ, 1), F32)
        carries = tile(2 * i + 1, ((zc, zc, jnp.zeros((SB_TQ, 128), F32)),) * SB_NP, "B", 0, False)
        carries = tile(2 * i, carries, "A", 1, False)

        def two(jj, cr):
            cr = tile(2 * i - 1 - 2 * jj, cr, "full", 0, True)
            return tile(2 * i - 2 - 2 * jj, cr, "full", 1, True)

        carries = lax.fori_loop(0, i, two, carries)
        for cols, carry in zip(pairs, carries):
            o_ref[:, cols] = carry[2].astype(BF16)
        for slot in range(2):
            for cp in stores(0, slot):
                cp.wait()

    wide = 128 * SB_NP
    blk = lambda b, g, i: (b * nq + i, g)
    full = lambda b, g, i: (b, g)
    hbm = pl.BlockSpec(memory_space=pl.ANY)
    tiles = jax.ShapeDtypeStruct((nbatch * nq * nk, SB_NP, SB_TQ, 2 * SB_T), BF16)
    return pl.pallas_call(
        body, name="sb_fwd", grid=(nbatch, 4 // SB_NP, nq),
        in_specs=[pl.BlockSpec((SB_TQ, wide), blk), pl.BlockSpec((seq, wide), full), pl.BlockSpec((seq, wide), full),
                  _const_spec((2 * SB_T, SB_T))],
        out_specs=[pl.BlockSpec((SB_TQ, wide), blk), hbm, hbm],
        out_shape=[jax.ShapeDtypeStruct((T, 512), BF16), tiles, tiles],
        scratch_shapes=[pltpu.VMEM((2, SB_NP, SB_TQ, 2 * SB_T), BF16), pltpu.VMEM((2, SB_NP, SB_TQ, 2 * SB_T), BF16),
                        pltpu.SemaphoreType.DMA((2, 2))],
        compiler_params=_params(3),
    )(qb, kb, vb, ublk)


def _sb_bwd(qb, kb, vb, dob, wst, ost, pblk, nbatch, seq):
    nq, nk = seq // SB_TQ, seq // SB_T
    T = nbatch * seq

    def body(q_ref, k_ref, v_ref, do_ref, wst_ref, ost_ref, up_ref, dq_ref, dk_ref, dv_ref, wbuf, obuf, sems):
        b, i = pl.program_id(0), pl.program_id(2)

        @pl.when(i == 0)
        def _():
            dk_ref[...] = jnp.zeros_like(dk_ref)
            dv_ref[...] = jnp.zeros_like(dv_ref)

        m0b = _lane_mask(SB_T, BF16)
        m0 = _lane_mask(SB_T) > 0.5
        up = up_ref[...]
        pairs = [slice(pp * 128, (pp + 1) * 128) for pp in range(SB_NP)]

        def loads(j, slot):
            tix = (b * nq + i) * nk + j
            return (pltpu.make_async_copy(wst_ref.at[tix], wbuf.at[slot], sems.at[0, slot]),
                    pltpu.make_async_copy(ost_ref.at[tix], obuf.at[slot], sems.at[1, slot]))

        def tile(j, carries, kind, slot, fetch_next):
            rows, mask = _sb_rows(kind), _sb_mask(kind)
            if fetch_next:
                for cp in loads(j + 1, 1 - slot):
                    cp.start()
            for cp in loads(j, slot):
                cp.wait()
            out = []
            for pp, (cols, (s0, s1, dq)) in enumerate(zip(pairs, carries)):
                q, do = q_ref[rows, cols], do_ref[rows, cols]
                kst = _pair_rows(k_ref, j, cols, m0b)
                vst = _pair_rows(v_ref, j, cols, m0b)
                wb = wbuf[slot, pp, rows, :]
                e = _dot_nt(do, vst) * wb.astype(F32)
                dlb = _bcast2(s0[rows], s1[rows]) + _scan1(e, up)
                dz = (e + dlb) * jnp.exp(obuf[slot, pp, rows, :].astype(F32)) - dlb
                if mask is not None:
                    dz = jnp.where(mask, dz, 0.0)
                dzb = dz.astype(BF16)
                dkst = _dot_tn(dzb, q)
                dvst = _dot_tn(wb, do)
                keys = pl.ds(pl.multiple_of(j * SB_T, SB_T), SB_T)
                dk_ref[keys, cols] = dk_ref[keys, cols] + jnp.where(m0, dkst[0:SB_T], dkst[SB_T:2 * SB_T])
                dv_ref[keys, cols] = dv_ref[keys, cols] + jnp.where(m0, dvst[0:SB_T], dvst[SB_T:2 * SB_T])
                x0, x1 = _rowsum2(e)
                ddq = _dot(dzb, kst)
                if kind == "B":
                    x0, x1, ddq = _lower(x0), _lower(x1), _lower(ddq)
                out.append((s0 + x0, s1 + x1, dq + ddq))
            return tuple(out)

        for cp in loads(0, 0):
            cp.start()

        def two(jj, cr):
            cr = tile(2 * jj, cr, "full", 0, True)
            return tile(2 * jj + 1, cr, "full", 1, True)

        zc = jnp.zeros((SB_TQ, 1), F32)
        carries = lax.fori_loop(0, i, two, ((zc, zc, jnp.zeros((SB_TQ, 128), F32)),) * SB_NP)
        carries = tile(2 * i, carries, "A", 0, True)
        carries = tile(2 * i + 1, carries, "B", 1, False)
        for cols, carry in zip(pairs, carries):
            dq_ref[:, cols] = carry[2] * SCALE

    wide = 128 * SB_NP
    blk = lambda b, g, i: (b * nq + i, g)
    full = lambda b, g, i: (b, g)
    hbm = pl.BlockSpec(memory_space=pl.ANY)
    return pl.pallas_call(
        body, name="sb_bwd", grid=(nbatch, 4 // SB_NP, nq),
        in_specs=[pl.BlockSpec((SB_TQ, wide), blk), pl.BlockSpec((seq, wide), full), pl.BlockSpec((seq, wide), full),
                  pl.BlockSpec((SB_TQ, wide), blk), hbm, hbm, _const_spec((SB_T, SB_T))],
        out_specs=[pl.BlockSpec((SB_TQ, wide), blk), pl.BlockSpec((seq, wide), full), pl.BlockSpec((seq, wide), full)],
        out_shape=[jax.ShapeDtypeStruct((T, 512), F32)] * 3,
        scratch_shapes=[pltpu.VMEM((2, SB_NP, SB_TQ, 2 * SB_T), BF16), pltpu.VMEM((2, SB_NP, SB_TQ, 2 * SB_T), BF16),
                        pltpu.SemaphoreType.DMA((2, 2))],
        compiler_params=_params(3),
    )(qb, kb, vb, dob, wst, ost, pblk)


def _sigmoid(g):
    return 1.0 / (1.0 + jnp.exp(-g))


def _out_proj(oa, ob, ga, gb, x2, tgt, wout_bf):
    T = x2.shape[0]

    def body(oa_ref, ob_ref, ga_ref, gb_ref, x_ref, t_ref, w_ref,
             dout_ref, doa_ref, dob_ref, dga_ref, dgb_ref, dw_ref, loss_ref):
        @pl.when(pl.program_id(0) == 0)
        def _():
            loss_ref[...] = jnp.zeros_like(loss_ref)
            dw_ref[...] = jnp.zeros_like(dw_ref)

        halves = ((oa_ref, ga_ref, doa_ref, dga_ref, 0), (ob_ref, gb_ref, dob_ref, dgb_ref, 512))
        out = x_ref[...]
        gated = []
        for o_ref, g_ref, _, _, lo in halves:
            g = g_ref[...]
            sg = _sigmoid(g)
            y = (o_ref[...] * (g * sg)).astype(BF16)
            out = out + _dot(y, w_ref[lo:lo + 512, :])
            gated.append((y, g, sg))
        diff = out - t_ref[...]
        dout = diff * (1.0 / D_MODEL)
        dout_ref[...] = dout
        loss_ref[...] = loss_ref[...] + jnp.sum(diff * diff) * (0.5 / D_MODEL)
        db = dout.astype(BF16)
        for (o_ref, _, do_ref, dg_ref, lo), (y, g, sg) in zip(halves, gated):
            dw_ref[lo:lo + 512, :] = dw_ref[lo:lo + 512, :] + _dot_tn(y, db)
            dy = _dot_nt(db, w_ref[lo:lo + 512, :])
            do_ref[...] = (dy * (g * sg)).astype(BF16)
            dg_ref[...] = (dy * o_ref[...] * (sg * (1.0 + g * (1.0 - sg)))).astype(BF16)

    rows = lambda w: pl.BlockSpec((TM, w), lambda i: (i, 0))
    return pl.pallas_call(
        body, name="out_proj", grid=(T // TM,),
        in_specs=[rows(512), rows(512), rows(512), rows(512), rows(D_MODEL), rows(D_MODEL),
                  _const_spec((D_MODEL, D_MODEL))],
        out_specs=[rows(D_MODEL)] + [rows(512)] * 4 + [_const_spec((D_MODEL, D_MODEL)), _const_spec((8, 128))],
        out_shape=[jax.ShapeDtypeStruct((T, D_MODEL), F32)] + [jax.ShapeDtypeStruct((T, 512), BF16)] * 4
        + [jax.ShapeDtypeStruct((D_MODEL, D_MODEL), F32), jax.ShapeDtypeStruct((8, 128), F32)],
        compiler_params=_params(1),
    )(oa, ob, ga, gb, x2, tgt, wout_bf)


def _qk_grad(qraw, kraw, dqrot, dk2, dv2, qg512, kg128, g512, cos, sa, sb):
    T = qraw.shape[0]

    def body(qraw_ref, kraw_ref, dqrot_ref, dk2_ref, dv2_ref, qg_ref, kg_ref, g_ref, cos_ref, sa_ref, sb_ref,
             dqa_ref, dkv_ref, dqg_ref, dkg_ref):
        @pl.when(pl.program_id(0) == 0)
        def _():
            dqg_ref[...] = jnp.zeros_like(dqg_ref)
            dkg_ref[...] = jnp.zeros_like(dkg_ref)

        cosv, sav, sbv = cos_ref[...], sa_ref[...], sb_ref[...]
        m0 = _lane_mask(TM) > 0.5

        def head_norm_bwd(raw, dn_rot, gmat, gain):
            r = lax.rsqrt(_dot_b(raw * raw, gmat) + EPS)
            n = raw * r
            dn = dn_rot * gain
            return r * (dn - n * _dot_b(dn * n, gmat)), jnp.sum(dn_rot * n, axis=0, keepdims=True)

        def fold(ref):
            a, b = ref[:, 0:128], ref[:, 128:256]
            return jnp.where(m0, a + pltpu.roll(a, 64, 1), b + pltpu.roll(b, 64, 1))

        dqn = jnp.concatenate([_rope_bwd(dqrot_ref[:, s * 128:(s + 1) * 128], cosv, sav, sbv) for s in range(4)], axis=1)
        dqa, dqg = head_norm_bwd(qraw_ref[...], dqn, g_ref[...], qg_ref[...])
        dka, dkg = head_norm_bwd(kraw_ref[...], _rope_bwd(fold(dk2_ref), cosv, sav, sbv), g_ref[0:128, 0:128], kg_ref[...])
        dqa_ref[...] = dqa.astype(BF16)
        dkv_ref[:, 0:128] = dka.astype(BF16)
        dkv_ref[:, 128:256] = fold(dv2_ref).astype(BF16)
        dqg_ref[...] = dqg_ref[...] + dqg
        dkg_ref[...] = dkg_ref[...] + dkg

    rows = lambda w: pl.BlockSpec((TM, w), lambda i: (i, 0))
    return pl.pallas_call(
        body, name="qk_grad", grid=(T // TM,),
        in_specs=[rows(512), rows(128), rows(512), rows(256), rows(256), _const_spec((1, 512)), _const_spec((1, 128)),
                  _const_spec((512, 512)), rows(128), rows(128), rows(128)],
        out_specs=[rows(512), rows(256), _const_spec((1, 512)), _const_spec((1, 128))],
        out_shape=[jax.ShapeDtypeStruct((T, 512), BF16), jax.ShapeDtypeStruct((T, 256), BF16),
                   jax.ShapeDtypeStruct((1, 512), F32), jax.ShapeDtypeStruct((1, 128), F32)],
        compiler_params=_params(1),
    )(qraw, kraw, dqrot, dk2, dv2, qg512, kg128, g512, cos, sa, sb)


_PIECES = ((QA, 512), (KA, 256), (GA, 512), (QB, 512), (KB, 512), (VB, 512), (GB, 512))


def _w_in_grad(h, pieces):
    T = h.shape[0]

    def body(h_ref, *refs):
        dw_ref = refs[-1]

        @pl.when(pl.program_id(0) == 0)
        def _():
            dw_ref[...] = jnp.zeros_like(dw_ref)

        hb = h_ref[...]
        for (lo, width), p_ref in zip(_PIECES, refs[:-1]):
            dw_ref[lo:lo + width, :] = dw_ref[lo:lo + width, :] + _dot_tn(p_ref[...].astype(BF16), hb)

    rows = lambda w: pl.BlockSpec((TM, w), lambda i: (i, 0))
    return pl.pallas_call(
        body, name="w_in_grad", grid=(T // TM,),
        in_specs=[rows(D_MODEL)] + [rows(w) for _, w in _PIECES],
        out_specs=_const_spec((IN_WIDTH, D_MODEL)),
        out_shape=jax.ShapeDtypeStruct((IN_WIDTH, D_MODEL), F32),
        compiler_params=_params(1),
    )(h, *pieces)


def _chip_exchange(src_ref, dst_ref, sems):
    _, _, c, chips = _place()
    return [_remote(src_ref.at[2 * cx + cy], dst_ref.at[j], sems, j, (cx, cy, c)) for j, (cx, cy) in enumerate(chips)]


def _x_grad(x2, dout, pieces, w_bf, ng, cin4):
    T = x2.shape[0]
    npc = len(_PIECES)
    steps = T // TM

    def body(x_ref, dout_ref, *refs):
        w_ref, ng_ref, cin_ref, gx_ref, dng_ref, rin_ref, ssem, rsem = refs[npc:]
        step = pl.program_id(0)
        copies = lambda: _chip_exchange(cin_ref, rin_ref, (ssem, rsem))

        @pl.when(step == 0)
        def _():
            dng_ref[...] = jnp.zeros_like(dng_ref)
            for cp in copies():
                cp.start()

        dh = jnp.zeros((TM, D_MODEL), F32)
        for (lo, width), p_ref in zip(_PIECES, refs[:npc]):
            dh = dh + _dot(p_ref[...].astype(BF16), w_ref[lo:lo + width, :])
        xb = x_ref[...]
        r = lax.rsqrt(jnp.mean(xb * xb, axis=-1, keepdims=True) + EPS)
        n = xb * r
        dn = dh * ng_ref[...]
        gx_ref[...] = dout_ref[...] + r * (dn - n * jnp.mean(dn * n, axis=-1, keepdims=True))
        dng_ref[...] = dng_ref[...] + jnp.sum(dh * n, axis=0, keepdims=True)

        @pl.when(step == steps - 1)
        def _():
            for cp in copies():
                cp.wait_recv()
            for cp in copies():
                cp.wait_send()

    rows = lambda w: pl.BlockSpec((TM, w), lambda i: (i, 0))
    hbm = pl.BlockSpec(memory_space=pl.ANY)
    return pl.pallas_call(
        body, name="x_grad", grid=(steps,),
        in_specs=[rows(D_MODEL), rows(D_MODEL)] + [rows(w) for _, w in _PIECES]
        + [_const_spec((IN_WIDTH, D_MODEL)), _const_spec((1, D_MODEL)), hbm],
        out_specs=[rows(D_MODEL), _const_spec((1, D_MODEL)), hbm],
        out_shape=[jax.ShapeDtypeStruct((T, D_MODEL), F32), jax.ShapeDtypeStruct((1, D_MODEL), F32),
                   jax.ShapeDtypeStruct((3,) + cin4.shape[1:], cin4.dtype)],
        scratch_shapes=[pltpu.SemaphoreType.DMA((3,)), pltpu.SemaphoreType.DMA((3,))],
        compiler_params=_params(1),
    )(x2, dout, *pieces, w_bf, ng, cin4)


HBM = pl.BlockSpec(memory_space=pl.ANY)


def _place():
    x, y, c = lax.axis_index("x"), lax.axis_index("y"), lax.axis_index("c")
    chips = [(1 - x, y), (x, 1 - y), (1 - x, 1 - y)]
    return x, y, c, chips


def _remote(src, dst, sems, k, to):
    return pltpu.make_async_remote_copy(src_ref=src, dst_ref=dst, send_sem=sems[0].at[k], recv_sem=sems[1].at[k],
                                        device_id=to, device_id_type=MESH)


def _gather_plan(src, dst, sems):
    x, y, c, _ = _place()
    rows = src.shape[1] // 2
    parts = (pl.ds(0, rows), pl.ds(rows, rows))
    me, kx, ky, kd = 2 * x + y, 2 * (1 - x) + y, 2 * x + 1 - y, 2 * (1 - x) + 1 - y
    to_x, to_y, sib = (1 - x, y, c), (x, 1 - y, c), (x, y, 1 - c)
    direct = [_remote(src.at[c, parts[0]], dst.at[me, c, parts[0]], sems, 0, to_x),
              _remote(src.at[c, parts[1]], dst.at[me, c, parts[1]], sems, 1, to_x),
              _remote(src.at[c, parts[1]], dst.at[me, c, parts[1]], sems, 2, to_y),
              _remote(src.at[c, parts[0]], dst.at[me, c, parts[0]], sems, 3, to_y)]
    arrived = [dst.at[kx, c, parts[0]], dst.at[kx, c, parts[1]], dst.at[ky, c, parts[1]], dst.at[ky, c, parts[0]]]
    relays = [_remote(arrived[0], arrived[0], sems, 4, to_y), _remote(arrived[2], arrived[2], sems, 5, to_x)]
    relayed = [dst.at[kd, c, parts[0]], dst.at[kd, c, parts[1]]]
    forwards = [_remote(dst.at[k, c], dst.at[k, c], sems, 6 + n, sib) for n, k in enumerate((kx, ky, kd))]
    from_sib = [dst.at[k, 1 - c] for k in (kx, ky, kd)]
    return direct, arrived, relays, relayed, forwards, from_sib


def _gather_start(src, dst, sems):
    direct = _gather_plan(src, dst, sems)[0]
    for k in (0, 2, 1, 3):
        direct[k].start()


def _gather_finish(src, dst, sems):
    direct, arrived, relays, relayed, forwards, from_sib = _gather_plan(src, dst, sems)
    landed = lambda ref, k: _remote(ref, ref, sems, k, (0, 0, 0)).wait_recv()
    landed(arrived[0], 0)
    relays[0].start()
    landed(arrived[2], 2)
    relays[1].start()
    landed(arrived[1], 1)
    forwards[0].start()
    landed(arrived[3], 3)
    forwards[1].start()
    landed(relayed[0], 4)
    landed(relayed[1], 5)
    forwards[2].start()
    for n, ref in enumerate(from_sib):
        landed(ref, 6 + n)
    for cp in direct + relays + forwards:
        cp.wait_send()


def _own_slot(gathered, shard):
    me = 2 * lax.axis_index("x") + lax.axis_index("y")
    return lax.dynamic_update_slice(gathered, shard[None], (me, 0, 0, 0))


def _pair_exchange(g4):
    def body(src, dst, ssem, rsem):
        x, y, c, _ = _place()
        cp = _remote(src.at[:, pl.ds(1 - c, 1)], dst, (ssem, rsem), 0, (x, y, 1 - c))
        cp.start()
        cp.wait()

    return pl.pallas_call(
        body, name=f"pair_exchange_{g4.shape[2]}", in_specs=[HBM], out_specs=HBM,
        out_shape=jax.ShapeDtypeStruct((4, 1) + g4.shape[2:], g4.dtype),
        scratch_shapes=[pltpu.SemaphoreType.DMA((1,)), pltpu.SemaphoreType.DMA((1,))],
    )(g4)


def _share_halves(hin, hout, small):
    def body(hin_ref, hout_ref, small_ref, oin_ref, oout_ref, sall_ref, ssem, rsem, lsem):
        x, y, c, _ = _place()
        sems = (ssem, rsem)
        sib = (x, y, 1 - c)
        me = 4 * x + 2 * y + c
        own = pltpu.make_async_copy(small_ref, sall_ref.at[me], lsem.at[0])
        own.start()
        sent = [_remote(hin_ref, oin_ref, sems, 0, sib), _remote(hout_ref, oout_ref, sems, 1, sib)]
        flips = [(fx, fy, fc) for fx in (0, 1) for fy in (0, 1) for fc in (0, 1)][1:]
        for k, (fx, fy, fc) in enumerate(flips):
            sent.append(_remote(small_ref, sall_ref.at[me], sems, 2 + k, (x ^ fx, y ^ fy, c ^ fc)))
        for cp in sent:
            cp.start()
        _remote(hin_ref, oin_ref, sems, 0, sib).wait_recv()
        _remote(hout_ref, oout_ref, sems, 1, sib).wait_recv()
        for k, (fx, fy, fc) in enumerate(flips):
            src = 4 * (x ^ fx) + 2 * (y ^ fy) + (c ^ fc)
            _remote(small_ref, sall_ref.at[src], sems, 2 + k, sib).wait_recv()
        for cp in sent:
            cp.wait_send()
        own.wait()

    return pl.pallas_call(
        body, name="share_halves", in_specs=[HBM, HBM, HBM], out_specs=[HBM, HBM, HBM],
        out_shape=[jax.ShapeDtypeStruct(hin.shape, F32), jax.ShapeDtypeStruct(hout.shape, F32),
                   jax.ShapeDtypeStruct((8,) + small.shape, F32)],
        scratch_shapes=[pltpu.SemaphoreType.DMA((9,)), pltpu.SemaphoreType.DMA((9,)), pltpu.SemaphoreType.DMA((1,))],
    )(hin, hout, small)


def _add_half(cidx, full4, recv4):
    _, _, rows, width = full4.shape

    def body(c_ref, a_ref, b_ref, o_ref):
        o_ref[0] = (a_ref[0, 0] + b_ref[0, 0]).astype(BF16)

    return pl.pallas_call(
        body, name=f"add_half_{rows}",
        grid_spec=pltpu.PrefetchScalarGridSpec(
            num_scalar_prefetch=1, grid=(4,),
            in_specs=[pl.BlockSpec((1, 1, rows, width), lambda k, c: (k, c[0], 0, 0)),
                      pl.BlockSpec((1, 1, rows, width), lambda k, c: (k, 0, 0, 0))],
            out_specs=pl.BlockSpec((1, rows, width), lambda k, c: (k, 0, 0))),
        out_shape=jax.ShapeDtypeStruct((4, rows, width), BF16),
        compiler_params=_params(1),
    )(cidx, full4, recv4)


def _sum_chips(chip, own4, recv3):
    _, rows, width = recv3.shape
    rb = rows // 2

    def body(k_ref, a_ref, r_ref, o_ref):
        acc = a_ref[0].astype(F32)
        for s in range(3):
            acc = acc + r_ref[s].astype(F32)
        o_ref[...] = acc

    return pl.pallas_call(
        body, name=f"sum_chips_{rows}",
        grid_spec=pltpu.PrefetchScalarGridSpec(
            num_scalar_prefetch=1, grid=(rows // rb,),
            in_specs=[pl.BlockSpec((1, rb, width), lambda i, k: (k[0], i, 0)),
                      pl.BlockSpec((3, rb, width), lambda i, k: (0, i, 0))],
            out_specs=pl.BlockSpec((rb, width), lambda i, k: (i, 0))),
        out_shape=jax.ShapeDtypeStruct((rows, width), F32),
        compiler_params=_params(1),
    )(chip, own4, recv3)


def _adam_math(w, g, m, v):
    c1 = 1.0 - ADAM_B1 ** ADAM_STEP
    c2 = 1.0 - ADAM_B2 ** ADAM_STEP
    nm = ADAM_B1 * m + (1.0 - ADAM_B1) * g
    nv = ADAM_B2 * v + (1.0 - ADAM_B2) * (g * g)
    return -ADAM_LR * ((nm / c1) / (jnp.sqrt(nv / c2) + ADAM_EPS) + ADAM_WD * w), nm, nv


def _small_update(small_all, params):
    n = len(params)
    flat = [a for p in params for a in p]

    def body(s_ref, *refs):
        ins, loss_ref, outs = refs[:3 * n], refs[3 * n], refs[3 * n + 1:]
        total = s_ref[0]
        for d in range(1, 8):
            total = total + s_ref[d]
        loss_ref[...] = total[n:n + 1, 0:1]
        for r in range(n):
            w_ref, m_ref, v_ref = ins[3 * r:3 * r + 3]
            g = total[r:r + 1, 0:w_ref.shape[1]]
            outs[4 * r][...] = g
            outs[4 * r + 1][...], outs[4 * r + 2][...], outs[4 * r + 3][...] = _adam_math(w_ref[...], g, m_ref[...], v_ref[...])

    whole = lambda a: pl.BlockSpec(a.shape, lambda i: (0,) * a.ndim)
    out_shape = [jax.ShapeDtypeStruct((1, 1), F32)] + [jax.ShapeDtypeStruct(p[0].shape, F32) for p in params for _ in range(4)]
    res = pl.pallas_call(
        body, name="small_update", grid=(1,),
        in_specs=[whole(small_all)] + [whole(a) for a in flat], out_specs=[whole(s) for s in out_shape],
        out_shape=out_shape, compiler_params=_params(1),
    )(small_all, *flat)
    return res[0], [tuple(res[1 + 4 * r:5 + 4 * r]) for r in range(n)]


def _adamw_halves(cidx, w, own, recv, m, v):
    rows, width = w.shape
    rb = rows // 4

    def body(c_ref, w_ref, own_ref, recv_ref, m_ref, v_ref, g_ref, d_ref, nm_ref, nv_ref):
        mine = (pl.program_id(0) // 2) == c_ref[0]
        g = jnp.where(mine, own_ref[...], recv_ref[...])
        g_ref[...] = g
        d_ref[...], nm_ref[...], nv_ref[...] = _adam_math(w_ref[...], g, m_ref[...], v_ref[...])

    full = pl.BlockSpec((rb, width), lambda i, c: (i, 0))
    half = pl.BlockSpec((rb, width), lambda i, c: (i % 2, 0))
    return pl.pallas_call(
        body, name=f"adamw_halves_{rows}",
        grid_spec=pltpu.PrefetchScalarGridSpec(
            num_scalar_prefetch=1, grid=(4,),
            in_specs=[full, half, half, full, full], out_specs=[full] * 4),
        out_shape=[jax.ShapeDtypeStruct((rows, width), F32)] * 4,
        compiler_params=_params(1),
    )(cidx, w, own, recv, m, v)


def _constants():
    idx = jnp.arange(512)
    g512 = jnp.where(idx[:, None] // HEAD_DIM == idx[None, :] // HEAD_DIM, 1.0 / HEAD_DIM, 0.0).astype(BF16)
    j = jnp.arange(SB_T)
    ublk = jnp.where(j[:, None] >= j[None, :], 1.0, 0.0).astype(BF16)
    pblk = jnp.where(j[:, None] < j[None, :], 1.0, 0.0).astype(BF16)
    return g512, jnp.concatenate([ublk, ublk], axis=0), pblk


def _pad_rows(v, width):
    return jnp.pad(v, ((0, 0), (0, width - v.shape[1])))


def _pair_sum(cidx, partial, rows):
    g4 = partial.reshape(4, 2, rows, D_MODEL)
    return _add_half(cidx, g4, _pair_exchange(g4))


def _step(x2, tgt, positions, norm_gain, q_norm_gain, k_norm_gain, sinks, win_shard, wout_shard, nbatch, seq):
    g512, ublk, pblk = _constants()
    qg512 = jnp.tile(q_norm_gain, (1, 8))
    kg128 = jnp.tile(k_norm_gain, (1, 2))
    sink1 = sinks.reshape(8)
    cidx = lax.axis_index("c").astype(jnp.int32).reshape(1)
    chip = (2 * lax.axis_index("x") + lax.axis_index("y")).astype(jnp.int32).reshape(1)

    h, (cos, sa, sb), w_bf = _prologue(x2, norm_gain, positions.reshape(-1, 1), win_shard)
    (qraw, kraw, qrot, k2, v2, ga, qb, kb, vb, gb), wout_bf = _norm_proj(
        h, w_bf, qg512, kg128, g512, cos, sa, sb, wout_shard)
    oa = _swa_fwd(sink1, qrot, k2, v2, nbatch, seq)
    ob, wst, ost = _sb_fwd(qb, kb, vb, ublk, nbatch, seq)
    dout, doa, dob, dga, dgb, dwout, loss_acc = _out_proj(oa, ob, ga, gb, x2, tgt, wout_bf)

    cout4 = _pair_sum(cidx, dwout, SHARD_OUT // 2)
    dqrot, dk2, dv2, dsink, rout3 = _swa_bwd(sink1, qrot, k2, v2, doa, cout4, nbatch, seq)
    dqb, dkb, dvb = _sb_bwd(qb, kb, vb, dob, wst, ost, pblk, nbatch, seq)
    dk2, dv2 = _unshift(dk2, nbatch, seq), _unshift(dv2, nbatch, seq)
    dqa, dkv, dqg, dkg = _qk_grad(qraw, kraw, dqrot, dk2, dv2, qg512, kg128, g512, cos, sa, sb)
    pieces = (dqa, dkv, dga, dqb, dkb, dvb, dgb)
    cin4 = _pair_sum(cidx, _w_in_grad(h, pieces), SHARD_IN // 2)
    gx, dng, rin3 = _x_grad(x2, dout, pieces, w_bf, norm_gain, cin4)
    own_in, own_out = _sum_chips(chip, cin4, rin3), _sum_chips(chip, cout4, rout3)

    dqg64 = dqg.reshape(8, HEAD_DIM).sum(axis=0, keepdims=True)
    dkg64 = dkg.reshape(2, HEAD_DIM).sum(axis=0, keepdims=True)
    small = jnp.concatenate([dng, _pad_rows(dqg64, D_MODEL), _pad_rows(dkg64, D_MODEL),
                             _pad_rows(dsink[:, 0].reshape(1, 8), D_MODEL), _pad_rows(loss_acc[0:1, 0:1], D_MODEL),
                             jnp.zeros((3, D_MODEL), F32)], axis=0)
    sib_in, sib_out, small_all = _share_halves(own_in, own_out, small)
    return gx, cidx, (own_in, sib_in), (own_out, sib_out), small_all


def kernel(x, positions, norm_gain, w_in, q_norm_gain, k_norm_gain, sinks, w_out, loss_target, m_norm_gain, m_w_in, m_q_norm_gain, m_k_norm_gain, m_sinks, m_w_out, v_norm_gain, v_w_in, v_q_norm_gain, v_k_norm_gain, v_sinks, v_w_out):
    nbatch, seq, _ = x.shape
    T = nbatch * seq
    x2 = x.reshape(T, D_MODEL)
    tgt = loss_target.reshape(T, D_MODEL)
    tr = lambda a: jnp.swapaxes(a[0], 0, 1)
    win_t, m_win_t, v_win_t = tr(w_in), tr(m_w_in), tr(v_w_in)

    win_shard = win_t.astype(BF16).reshape(2, SHARD_IN // 2, D_MODEL)
    wout_shard = w_out[0].astype(BF16).reshape(2, SHARD_OUT // 2, D_MODEL)
    gx, cidx, g_in_halves, g_out_halves, small_all = _step(
        x2, tgt, positions, norm_gain, q_norm_gain, k_norm_gain, sinks, win_shard, wout_shard, nbatch, seq)

    g_in, d_in, nm_in, nv_in = [jnp.swapaxes(a, 0, 1) for a in
                                _adamw_halves(cidx, win_t, *g_in_halves, m_win_t, v_win_t)]
    g_out, d_out, nm_out, nv_out = _adamw_halves(cidx, w_out[0], *g_out_halves, m_w_out[0], v_w_out[0])
    loss, small = _small_update(small_all, [(norm_gain, m_norm_gain, v_norm_gain), (q_norm_gain, m_q_norm_gain, v_q_norm_gain),
                                            (k_norm_gain, m_k_norm_gain, v_k_norm_gain), (sinks, m_sinks, v_sinks)])
    (g_ng, d_ng, m_ng, v_ng), (g_qg, d_qg, m_qg, v_qg), (g_kg, d_kg, m_kg, v_kg), (g_sk, d_sk, m_sk, v_sk) = small
    return (loss.reshape(()), gx.reshape(nbatch, seq, D_MODEL),
            g_ng, g_in[None], g_qg, g_kg, g_sk, g_out[None],
            d_ng, d_in[None], d_qg, d_kg, d_sk, d_out[None],
            m_ng, nm_in[None], m_qg, m_kg, m_sk, nm_out[None],
            v_ng, nv_in[None], v_qg, v_kg, v_sk, nv_out[None])
```

```python
import functools
import math

import jax
import jax.numpy as jnp
from jax import lax
from jax.experimental import pallas as pl
from jax.experimental.pallas import tpu as pltpu

F32 = jnp.float32
BF16 = jnp.bfloat16

D_MODEL = 1024
HEAD_DIM = 64
BLOCK = 128
ROPE_THETA = 10000.0
EPS = 1e-6
QA, KA, VA, GA, QB, KB, VB, GB = 0, 512, 640, 768, 1280, 1792, 2304, 2816
IN_WIDTH = 3328
SHARD_IN = IN_WIDTH // 4
SHARD_OUT = D_MODEL // 4
SCALE = 1.0 / math.sqrt(HEAD_DIM)
NEG = -1e30

ADAM_LR, ADAM_B1, ADAM_B2, ADAM_EPS, ADAM_WD, ADAM_STEP = 0.001, 0.9, 0.999, 1e-08, 0.01, 10

TM = 512
VMEM_LIMIT = 56 * 1024 * 1024
MESH = pl.DeviceIdType.MESH


def _dot(a, b):
    return jnp.dot(a, b, preferred_element_type=F32)


def _dot_nt(a, b):
    return lax.dot_general(a, b, (((1,), (1,)), ((), ())), preferred_element_type=F32)


def _dot_tn(a, b):
    return lax.dot_general(a, b, (((0,), (0,)), ((), ())), preferred_element_type=F32)


def _dot_b(a, m):
    return _dot(a.astype(BF16), m)


def _params(n_axes=None, vmem=VMEM_LIMIT):
    sem = None if n_axes is None else ("arbitrary",) * n_axes
    return pltpu.CompilerParams(dimension_semantics=sem, vmem_limit_bytes=vmem)


def _const_spec(shape):
    nd = len(shape)
    return pl.BlockSpec(shape, lambda *_: (0,) * nd)


def _rope_fwd(x, cos, sa, sb):
    return x * cos + pltpu.roll(x, 96, 1) * sa + pltpu.roll(x, 32, 1) * sb


def _rope_bwd(d, cos, sa, sb):
    return d * cos - pltpu.roll(d, 96, 1) * sa - pltpu.roll(d, 32, 1) * sb


def _lane_mask(rows, dtype=F32):
    lane = lax.broadcasted_iota(jnp.int32, (rows, 128), 1)
    return jnp.where(lane < HEAD_DIM, 1.0, 0.0).astype(dtype)


def _prologue(x2, ng, pos, win_shard):
    T = x2.shape[0]
    steps = T // TM
    half = HEAD_DIM // 2
    inv_freq = jnp.tile(ROPE_THETA ** (-jnp.arange(half, dtype=F32) * 2.0 / HEAD_DIM), 4).reshape(1, 128)

    def body(x_ref, ng_ref, pos_ref, freq_ref, wi_ref, h_ref, cos_ref, sa_ref, sb_ref, ai_ref, ssem, rsem):
        @pl.when(pl.program_id(0) == 0)
        def _():
            _gather_start(wi_ref, ai_ref, (ssem, rsem), relay=True)

        xb = x_ref[...]
        r = lax.rsqrt(jnp.mean(xb * xb, axis=-1, keepdims=True) + EPS)
        h_ref[...] = (xb * r * ng_ref[...]).astype(BF16)
        ang = pos_ref[...].astype(F32) * freq_ref[...]
        first = (lax.broadcasted_iota(jnp.int32, (TM, 128), 1) & (HEAD_DIM - 1)) < half
        sn = jnp.sin(ang)
        cos_ref[...] = jnp.cos(ang)
        sa_ref[...] = jnp.where(first, -sn, 0.0)
        sb_ref[...] = jnp.where(first, 0.0, sn)

        @pl.when(pl.program_id(0) == steps - 1)
        def _():
            _gather_finish(wi_ref, ai_ref, (ssem, rsem), relay=True)

    rows = lambda w: pl.BlockSpec((TM, w), lambda i: (i, 0))
    hbm = pl.BlockSpec(memory_space=pl.ANY)
    h, cos, sa, sb, gathered = pl.pallas_call(
        body, name="prologue", grid=(steps,),
        in_specs=[rows(D_MODEL), _const_spec((1, D_MODEL)), rows(1), _const_spec((1, 128)), hbm],
        out_specs=[rows(D_MODEL), rows(128), rows(128), rows(128), hbm],
        out_shape=[jax.ShapeDtypeStruct((T, D_MODEL), BF16)] + [jax.ShapeDtypeStruct((T, 128), F32)] * 3
        + [jax.ShapeDtypeStruct((4,) + win_shard.shape, win_shard.dtype)],
        scratch_shapes=[pltpu.SemaphoreType.DMA((9,)), pltpu.SemaphoreType.DMA((9,))],
        compiler_params=_params(1),
    )(x2, ng, pos, inv_freq, win_shard)
    return h, (cos, sa, sb), _own_slot(gathered, win_shard).reshape(IN_WIDTH, D_MODEL)


def _norm_proj(h, w_bf, qg512, kg128, g512, cos, sa, sb, wout_shard):
    T = h.shape[0]
    steps = T // TM

    def body(h_ref, w_ref, qg_ref, kg_ref, g_ref, cos_ref, sa_ref, sb_ref, wo_ref,
             qraw_ref, kraw_ref, qrot_ref, k2_ref, v2_ref, ga_ref, qb_ref, kb_ref, vb_ref, gb_ref, ao_ref,
             ssem, rsem):
        @pl.when(pl.program_id(0) == 0)
        def _():
            _gather_start(wo_ref, ao_ref, (ssem, rsem), relay=False)

        h = h_ref[...]
        cosv, sav, sbv = cos_ref[...], sa_ref[...], sb_ref[...]
        m0 = _lane_mask(TM)

        def dup(v):
            v0 = v * m0
            v1 = v - v0
            return v0 + pltpu.roll(v0, 64, 1), v1 + pltpu.roll(v1, 64, 1)

        qa = _dot_nt(h, w_ref[QA:KA, :])
        qraw_ref[...] = qa
        qn = qa * lax.rsqrt(_dot_b(qa * qa, g_ref[...]) + EPS) * qg_ref[...]
        for s in range(4):
            qs = _rope_fwd(qn[:, s * 128:(s + 1) * 128], cosv, sav, sbv)
            qrot_ref[:, s * 128:(s + 1) * 128] = (qs * SCALE).astype(BF16)
        ka = _dot_nt(h, w_ref[KA:VA, :])
        kraw_ref[...] = ka
        kn = ka * lax.rsqrt(_dot_b(ka * ka, g_ref[0:128, 0:128]) + EPS) * kg_ref[...]
        k0, k1 = dup(_rope_fwd(kn, cosv, sav, sbv))
        k2_ref[:, 0:128] = k0.astype(BF16)
        k2_ref[:, 128:256] = k1.astype(BF16)
        v0, v1 = dup(_dot_nt(h, w_ref[VA:GA, :]))
        v2_ref[:, 0:128] = v0.astype(BF16)
        v2_ref[:, 128:256] = v1.astype(BF16)
        ga_ref[...] = _dot_nt(h, w_ref[GA:QB, :])
        qb_ref[...] = (_dot_nt(h, w_ref[QB:KB, :]) * SCALE).astype(BF16)
        kb_ref[...] = _dot_nt(h, w_ref[KB:VB, :]).astype(BF16)
        vb_ref[...] = _dot_nt(h, w_ref[VB:GB, :]).astype(BF16)
        gb_ref[...] = _dot_nt(h, w_ref[GB:IN_WIDTH, :])

        @pl.when(pl.program_id(0) == steps - 1)
        def _():
            _gather_finish(wo_ref, ao_ref, (ssem, rsem), relay=False)

    def rows(w):
        return pl.BlockSpec((TM, w), lambda i: (i, 0))

    outs = [(512, F32), (128, F32), (512, BF16), (256, BF16), (256, BF16), (512, F32),
            (512, BF16), (512, BF16), (512, BF16), (512, F32)]
    hbm = pl.BlockSpec(memory_space=pl.ANY)
    res = pl.pallas_call(
        body, name="norm_proj", grid=(steps,),
        in_specs=[rows(D_MODEL), _const_spec((IN_WIDTH, D_MODEL)), _const_spec((1, 512)),
                  _const_spec((1, 128)), _const_spec((512, 512)), rows(128), rows(128), rows(128), hbm],
        out_specs=[rows(w) for w, _ in outs] + [hbm],
        out_shape=[jax.ShapeDtypeStruct((T, w), dt) for w, dt in outs]
        + [jax.ShapeDtypeStruct((4,) + wout_shard.shape, wout_shard.dtype)],
        scratch_shapes=[pltpu.SemaphoreType.DMA((9,)), pltpu.SemaphoreType.DMA((9,))],
        compiler_params=_params(1),
    )(h, w_bf, qg512, kg128, g512, cos, sa, sb, wout_shard)
    return res[:-1], _own_slot(res[-1], wout_shard).reshape(D_MODEL, D_MODEL)


SWA_Q = 512
SWA_SUB = SWA_Q // BLOCK


def _swa_scores(q, kst, mask, sink_ref, kv):
    s_all = _dot_nt(q, kst)
    first = lax.broadcasted_iota(jnp.int32, (256, 1), 0) < 128
    probs, stats = [], []
    for hh in range(2):
        sink = jnp.where(first, sink_ref[kv * 4 + hh], sink_ref[kv * 4 + 2 + hh])
        s = jnp.where(mask, s_all[:, hh * 256:(hh + 1) * 256], NEG)
        m = jnp.maximum(jnp.max(s, axis=1, keepdims=True), sink)
        pe = jnp.exp(s - m)
        inv = 1.0 / (jnp.sum(pe, axis=1, keepdims=True) + jnp.exp(sink - m))
        probs.append(pe * inv)
        stats.append(jnp.exp(sink - m) * inv)
    return probs, stats


def _swa_mask(has_prev):
    r = lax.broadcasted_iota(jnp.int32, (256, 256), 0) & 127
    c = lax.broadcasted_iota(jnp.int32, (256, 256), 1)
    band = (c > r) & (c <= r + 128)
    return band if has_prev is True else band & ((c >= 128) | has_prev)


def _stack_pairs(ref, rows, kv):
    return jnp.concatenate([ref[rows, (2 * kv) * 128:(2 * kv + 1) * 128], ref[rows, (2 * kv + 1) * 128:(2 * kv + 2) * 128]], axis=0)


def _swa_keys(prev_ref, main_ref, s, kv, m0b):
    cols = slice(kv * 128, (kv + 1) * 128)
    prev = prev_ref[:, cols] if s == 0 else main_ref[(s - 1) * 128:s * 128, cols]
    kc = jnp.concatenate([prev, main_ref[s * 128:(s + 1) * 128, cols]], axis=0)
    k0 = kc * m0b
    return jnp.concatenate([k0, kc - k0], axis=0)


def _swa_fwd(sinks, qrot, k2, v2, nbatch, seq):
    ni = seq // SWA_Q
    T = nbatch * seq

    def body(sink_ref, q_ref, kp_ref, km_ref, vp_ref, vm_ref, o_ref):
        i = pl.program_id(1)
        m0b = _lane_mask(256, BF16)
        for s in range(SWA_SUB):
            mask = _swa_mask(True if s else i > 0)
            rows = slice(s * 128, (s + 1) * 128)
            for kv in range(2):
                kst = _swa_keys(kp_ref, km_ref, s, kv, m0b)
                vst = _swa_keys(vp_ref, vm_ref, s, kv, m0b)
                probs, _ = _swa_scores(_stack_pairs(q_ref, rows, kv), kst, mask, sink_ref, kv)
                o2 = _dot(jnp.concatenate(probs, axis=1).astype(BF16), vst)
                o_ref[rows, kv * 256:kv * 256 + 128] = o2[0:128].astype(BF16)
                o_ref[rows, kv * 256 + 128:(kv + 1) * 256] = o2[128:256].astype(BF16)

    main = lambda b, i: (b * ni + i, 0)
    prev = lambda b, i: ((b * ni + i) * SWA_SUB - jnp.where(i > 0, 1, 0), 0)
    return pl.pallas_call(
        body, name="swa_fwd", grid=(nbatch, ni),
        in_specs=[pl.BlockSpec(memory_space=pltpu.SMEM), pl.BlockSpec((SWA_Q, 512), main),
                  pl.BlockSpec((128, 256), prev), pl.BlockSpec((SWA_Q, 256), main),
                  pl.BlockSpec((128, 256), prev), pl.BlockSpec((SWA_Q, 256), main)],
        out_specs=pl.BlockSpec((SWA_Q, 512), main),
        out_shape=jax.ShapeDtypeStruct((T, 512), BF16),
        compiler_params=_params(2),
    )(sinks, qrot, k2, k2, v2, v2)


def _swa_bwd(sinks, qrot, k2, v2, doa, cout4, nbatch, seq):
    ni = seq // SWA_Q
    T = nbatch * seq

    def body(sink_ref, q_ref, kp_ref, km_ref, vp_ref, vm_ref, do_ref, cout_ref,
             dq_ref, dk_ref, dv_ref, ds_ref, rout_ref, dkc, dvc, ssem, rsem):
        b, i = pl.program_id(0), pl.program_id(1)
        copies = lambda: _chip_exchange(cout_ref, rout_ref, (ssem, rsem))

        @pl.when((b == 0) & (i == 0))
        def _():
            ds_ref[...] = jnp.zeros_like(ds_ref)
            for cp in copies():
                cp.start()

        @pl.when((b == nbatch - 1) & (i == ni))
        def _():
            for cp in copies():
                cp.wait_recv()
            for cp in copies():
                cp.wait_send()

        @pl.when(i == 0)
        def _():
            dkc[...] = jnp.zeros_like(dkc)
            dvc[...] = jnp.zeros_like(dvc)

        @pl.when(i < ni)
        def _():
            m0b = _lane_mask(256, BF16)
            m0 = _lane_mask(128) > 0.5
            for kv in range(2):
                kcols = slice(kv * 128, (kv + 1) * 128)
                dk_own, dv_own = dkc[:, kcols], dvc[:, kcols]
                for s in range(SWA_SUB):
                    mask = _swa_mask(True if s else i > 0)
                    rows = slice(s * 128, (s + 1) * 128)
                    kst = _swa_keys(kp_ref, km_ref, s, kv, m0b)
                    vst = _swa_keys(vp_ref, vm_ref, s, kv, m0b)
                    q, do = _stack_pairs(q_ref, rows, kv), _stack_pairs(do_ref, rows, kv)
                    probs, psink = _swa_scores(q, kst, mask, sink_ref, kv)
                    dp_all = _dot_nt(do, vst)
                    ds_parts = []
                    for hh in range(2):
                        dp = dp_all[:, hh * 256:(hh + 1) * 256]
                        delta = jnp.sum(probs[hh] * dp, axis=1, keepdims=True)
                        ds_parts.append(probs[hh] * (dp - delta))
                        dsink = psink[hh] * delta
                        for pr in range(2):
                            h = kv * 4 + pr * 2 + hh
                            ds_ref[h:h + 1, :] = ds_ref[h:h + 1, :] - jnp.sum(dsink[pr * 128:(pr + 1) * 128])
                    ds_all = jnp.concatenate(ds_parts, axis=1).astype(BF16)
                    p_all = jnp.concatenate(probs, axis=1).astype(BF16)
                    dq2 = _dot(ds_all, kst) * SCALE
                    dq_ref[rows, kv * 256:kv * 256 + 128] = dq2[0:128]
                    dq_ref[rows, kv * 256 + 128:(kv + 1) * 256] = dq2[128:256]
                    dkst = _dot_tn(ds_all, q)
                    dvst = _dot_tn(p_all, do)
                    dk_ref[rows, kcols] = dk_own + jnp.where(m0, dkst[0:128], dkst[256:384])
                    dv_ref[rows, kcols] = dv_own + jnp.where(m0, dvst[0:128], dvst[256:384])
                    dk_own = jnp.where(m0, dkst[128:256], dkst[384:512])
                    dv_own = jnp.where(m0, dvst[128:256], dvst[384:512])
                dkc[:, kcols] = dk_own
                dvc[:, kcols] = dv_own

        @pl.when(i == ni)
        def _():
            dk_ref[...] = jnp.zeros_like(dk_ref)
            dv_ref[...] = jnp.zeros_like(dv_ref)
            dk_ref[0:128, :] = dkc[...]
            dv_ref[0:128, :] = dvc[...]

    main = lambda b, i: (b * ni + jnp.minimum(i, ni - 1), 0)
    prev = lambda b, i: ((b * ni + jnp.minimum(i, ni - 1)) * SWA_SUB - jnp.where(jnp.minimum(i, ni - 1) > 0, 1, 0), 0)
    shifted = lambda b, i: (b * (ni + 1) + i, 0)
    tpad = nbatch * (ni + 1) * SWA_Q
    return pl.pallas_call(
        body, name="swa_bwd", grid=(nbatch, ni + 1),
        in_specs=[pl.BlockSpec(memory_space=pltpu.SMEM), pl.BlockSpec((SWA_Q, 512), main),
                  pl.BlockSpec((128, 256), prev), pl.BlockSpec((SWA_Q, 256), main),
                  pl.BlockSpec((128, 256), prev), pl.BlockSpec((SWA_Q, 256), main),
                  pl.BlockSpec((SWA_Q, 512), main), pl.BlockSpec(memory_space=pl.ANY)],
        out_specs=[pl.BlockSpec((SWA_Q, 512), main), pl.BlockSpec((SWA_Q, 256), shifted),
                   pl.BlockSpec((SWA_Q, 256), shifted), _const_spec((8, 128)), pl.BlockSpec(memory_space=pl.ANY)],
        out_shape=[jax.ShapeDtypeStruct((T, 512), F32), jax.ShapeDtypeStruct((tpad, 256), F32),
                   jax.ShapeDtypeStruct((tpad, 256), F32), jax.ShapeDtypeStruct((8, 128), F32),
                   jax.ShapeDtypeStruct((3,) + cout4.shape[1:], cout4.dtype)],
        scratch_shapes=[pltpu.VMEM((128, 256), F32), pltpu.VMEM((128, 256), F32),
                        pltpu.SemaphoreType.DMA((3,)), pltpu.SemaphoreType.DMA((3,))],
        compiler_params=_params(2),
    )(sinks, qrot, k2, k2, v2, v2, doa, cout4)


def _unshift(dkpad, nbatch, seq):
    return dkpad.reshape(nbatch, seq + SWA_Q, 256)[:, BLOCK:BLOCK + seq].reshape(nbatch * seq, 256)


SB_T = 256
SB_TQ = 2 * SB_T


def _sb_mask(kind):
    if kind == "full":
        return None
    rows = SB_T if kind == "B" else SB_TQ
    r = lax.broadcasted_iota(jnp.int32, (rows, 2 * SB_T), 0)
    c = lax.broadcasted_iota(jnp.int32, (rows, 2 * SB_T), 1)
    causal = (c & (SB_T - 1)) < r
    return causal | (r >= SB_T) if kind == "A" else causal


def _sb_rows(kind):
    return slice(SB_T, SB_TQ) if kind == "B" else slice(0, SB_TQ)


def _lower(x):
    return jnp.concatenate([jnp.zeros_like(x), x], axis=0)


def _sb_logits(neg_q, kst):
    nz = _dot_nt(neg_q, kst)
    sign = jnp.uint32(0x80000000)
    neg_abs = lax.bitcast_convert_type(lax.bitcast_convert_type(nz, jnp.uint32) | sign, F32)
    return nz, jnp.minimum(nz, 0.0) - jnp.log(1.0 + jnp.exp(neg_abs))


SB_NP = 4


def _pair_rows(ref, j, cols, m0b):
    kj = ref[pl.ds(pl.multiple_of(j * SB_T, SB_T), SB_T), cols]
    k0 = kj * m0b
    return jnp.concatenate([k0, kj - k0], axis=0)


def _bcast2(c0, c1):
    rows = c0.shape[0]
    return jnp.concatenate([jnp.broadcast_to(c0, (rows, SB_T)), jnp.broadcast_to(c1, (rows, SB_T))], axis=1)


def _rowsum2(x):
    return jnp.sum(x[:, 0:SB_T], axis=1, keepdims=True), jnp.sum(x[:, SB_T:2 * SB_T], axis=1, keepdims=True)


def _scan2(x, tri2):
    outs = []
    for h in range(2):
        xh = x[:, h * SB_T:(h + 1) * SB_T]
        hi = xh.astype(BF16)
        lo = (xh - hi.astype(F32)).astype(BF16)
        outs.append(_dot(jnp.concatenate([hi, lo], axis=1), tri2))
    return jnp.concatenate(outs, axis=1)


def _scan1(x, tri):
    xb = x.astype(BF16)
    return jnp.concatenate([_dot(xb[:, h * SB_T:(h + 1) * SB_T], tri) for h in range(2)], axis=1)


def _sb_fwd(qb, kb, vb, ublk, nbatch, seq):
    nq, nk = seq // SB_TQ, seq // SB_T
    T = nbatch * seq

    def body(q_ref, k_ref, v_ref, u_ref, o_ref, wst_ref, ost_ref, wbuf, obuf, sems):
        b, i = pl.program_id(0), pl.program_id(2)
        m0b = _lane_mask(SB_T, BF16)
        u = u_ref[...]
        pairs = [slice(pp * 128, (pp + 1) * 128) for pp in range(SB_NP)]
        neg_qs = [-q_ref[:, cols] for cols in pairs]

        def stores(j, slot):
            tix = (b * nq + i) * nk + j
            return (pltpu.make_async_copy(wbuf.at[slot], wst_ref.at[tix], sems.at[0, slot]),
                    pltpu.make_async_copy(obuf.at[slot], ost_ref.at[tix], sems.at[1, slot]))

        def tile(j, carries, kind, slot, wait):
            rows, mask = _sb_rows(kind), _sb_mask(kind)
            if wait:
                for cp in stores(j, slot):
                    cp.wait()
            out = []
            for pp, (cols, neg_q, (c0, c1, acc)) in enumerate(zip(pairs, neg_qs, carries)):
                kst = _pair_rows(k_ref, j, cols, m0b)
                vst = _pair_rows(v_ref, j, cols, m0b)
                nz, lb = _sb_logits(neg_q[rows], kst)
                if mask is not None:
                    lb = jnp.where(mask, lb, 0.0)
                incl = _scan2(lb, u)
                w = jnp.exp(incl - nz if kind == "B" else incl + _bcast2(c0, c1) - nz)
                if mask is not None:
                    w = jnp.where(mask, w, 0.0)
                wb = w.astype(BF16)
                wbuf[slot, pp, rows, :] = wb
                obuf[slot, pp, rows, :] = lb.astype(BF16)
                if kind == "B":
                    wbuf[slot, pp, 0:SB_T, :] = jnp.zeros((SB_T, 2 * SB_T), BF16)
                    obuf[slot, pp, 0:SB_T, :] = jnp.zeros((SB_T, 2 * SB_T), BF16)
                d0, d1, da = incl[:, 0:1], incl[:, SB_T:SB_T + 1], _dot(wb, vst)
                if kind == "B":
                    d0, d1, da = _lower(d0), _lower(d1), _lower(da)
                out.append((c0 + d0, c1 + d1, acc + da))
            for cp in stores(j, slot):
                cp.start()
            return tuple(out)

        zc = jnp.zeros((SB_TQ, 1), F32)
        carries = tile(2 * i + 1, ((zc, zc, jnp.zeros((SB_TQ, 128), F32)),) * SB_NP, "B", 0, False)
        carries = tile(2 * i, carries, "A", 1, False)

        def two(jj, cr):
            cr = tile(2 * i - 1 - 2 * jj, cr, "full", 0, True)
            return tile(2 * i - 2 - 2 * jj, cr, "full", 1, True)

        carries = lax.fori_loop(0, i, two, carries)
        for cols, carry in zip(pairs, carries):
            o_ref[:, cols] = carry[2].astype(BF16)
        for slot in range(2):
            for cp in stores(0, slot):
                cp.wait()

    wide = 128 * SB_NP
    blk = lambda b, g, i: (b * nq + i, g)
    full = lambda b, g, i: (b, g)
    hbm = pl.BlockSpec(memory_space=pl.ANY)
    tiles = jax.ShapeDtypeStruct((nbatch * nq * nk, SB_NP, SB_TQ, 2 * SB_T), BF16)
    return pl.pallas_call(
        body, name="sb_fwd", grid=(nbatch, 4 // SB_NP, nq),
        in_specs=[pl.BlockSpec((SB_TQ, wide), blk), pl.BlockSpec((seq, wide), full), pl.BlockSpec((seq, wide), full),
                  _const_spec((2 * SB_T, SB_T))],
        out_specs=[pl.BlockSpec((SB_TQ, wide), blk), hbm, hbm],
        out_shape=[jax.ShapeDtypeStruct((T, 512), BF16), tiles, tiles],
        scratch_shapes=[pltpu.VMEM((2, SB_NP, SB_TQ, 2 * SB_T), BF16), pltpu.VMEM((2, SB_NP, SB_TQ, 2 * SB_T), BF16),
                        pltpu.SemaphoreType.DMA((2, 2))],
        compiler_params=_params(3),
    )(qb, kb, vb, ublk)


def _sb_bwd(qb, kb, vb, dob, wst, ost, pblk, nbatch, seq):
    nq, nk = seq // SB_TQ, seq // SB_T
    T = nbatch * seq

    def body(q_ref, k_ref, v_ref, do_ref, wst_ref, ost_ref, up_ref, dq_ref, dk_ref, dv_ref, wbuf, obuf, sems):
        b, i = pl.program_id(0), pl.program_id(2)

        @pl.when(i == 0)
        def _():
            dk_ref[...] = jnp.zeros_like(dk_ref)
            dv_ref[...] = jnp.zeros_like(dv_ref)

        m0b = _lane_mask(SB_T, BF16)
        m0 = _lane_mask(SB_T) > 0.5
        up = up_ref[...]
        pairs = [slice(pp * 128, (pp + 1) * 128) for pp in range(SB_NP)]

        def loads(j, slot):
            tix = (b * nq + i) * nk + j
            return (pltpu.make_async_copy(wst_ref.at[tix], wbuf.at[slot], sems.at[0, slot]),
                    pltpu.make_async_copy(ost_ref.at[tix], obuf.at[slot], sems.at[1, slot]))

        def tile(j, carries, kind, slot, fetch_next):
            rows, mask = _sb_rows(kind), _sb_mask(kind)
            if fetch_next:
                for cp in loads(j + 1, 1 - slot):
                    cp.start()
            for cp in loads(j, slot):
                cp.wait()
            out = []
            for pp, (cols, (s0, s1, dq)) in enumerate(zip(pairs, carries)):
                q, do = q_ref[rows, cols], do_ref[rows, cols]
                kst = _pair_rows(k_ref, j, cols, m0b)
                vst = _pair_rows(v_ref, j, cols, m0b)
                wb = wbuf[slot, pp, rows, :]
                e = _dot_nt(do, vst) * wb.astype(F32)
                dlb = _bcast2(s0[rows], s1[rows]) + _scan1(e, up)
                dz = (e + dlb) * jnp.exp(obuf[slot, pp, rows, :].astype(F32)) - dlb
                if mask is not None:
                    dz = jnp.where(mask, dz, 0.0)
                dzb = dz.astype(BF16)
                dkst = _dot_tn(dzb, q)
                dvst = _dot_tn(wb, do)
                keys = pl.ds(pl.multiple_of(j * SB_T, SB_T), SB_T)
                dk_ref[keys, cols] = dk_ref[keys, cols] + jnp.where(m0, dkst[0:SB_T], dkst[SB_T:2 * SB_T])
                dv_ref[keys, cols] = dv_ref[keys, cols] + jnp.where(m0, dvst[0:SB_T], dvst[SB_T:2 * SB_T])
                x0, x1 = _rowsum2(e)
                ddq = _dot(dzb, kst)
                if kind == "B":
                    x0, x1, ddq = _lower(x0), _lower(x1), _lower(ddq)
                out.append((s0 + x0, s1 + x1, dq + ddq))
            return tuple(out)

        for cp in loads(0, 0):
            cp.start()

        def two(jj, cr):
            cr = tile(2 * jj, cr, "full", 0, True)
            return tile(2 * jj + 1, cr, "full", 1, True)

        zc = jnp.zeros((SB_TQ, 1), F32)
        carries = lax.fori_loop(0, i, two, ((zc, zc, jnp.zeros((SB_TQ, 128), F32)),) * SB_NP)
        carries = tile(2 * i, carries, "A", 0, True)
        carries = tile(2 * i + 1, carries, "B", 1, False)
        for cols, carry in zip(pairs, carries):
            dq_ref[:, cols] = carry[2] * SCALE

    wide = 128 * SB_NP
    blk = lambda b, g, i: (b * nq + i, g)
    full = lambda b, g, i: (b, g)
    hbm = pl.BlockSpec(memory_space=pl.ANY)
    return pl.pallas_call(
        body, name="sb_bwd", grid=(nbatch, 4 // SB_NP, nq),
        in_specs=[pl.BlockSpec((SB_TQ, wide), blk), pl.BlockSpec((seq, wide), full), pl.BlockSpec((seq, wide), full),
                  pl.BlockSpec((SB_TQ, wide), blk), hbm, hbm, _const_spec((SB_T, SB_T))],
        out_specs=[pl.BlockSpec((SB_TQ, wide), blk), pl.BlockSpec((seq, wide), full), pl.BlockSpec((seq, wide), full)],
        out_shape=[jax.ShapeDtypeStruct((T, 512), F32)] * 3,
        scratch_shapes=[pltpu.VMEM((2, SB_NP, SB_TQ, 2 * SB_T), BF16), pltpu.VMEM((2, SB_NP, SB_TQ, 2 * SB_T), BF16),
                        pltpu.SemaphoreType.DMA((2, 2))],
        compiler_params=_params(3),
    )(qb, kb, vb, dob, wst, ost, pblk)


def _sigmoid(g):
    return 1.0 / (1.0 + jnp.exp(-g))


def _out_proj(oa, ob, ga, gb, x2, tgt, wout_bf):
    T = x2.shape[0]

    def body(oa_ref, ob_ref, ga_ref, gb_ref, x_ref, t_ref, w_ref,
             dout_ref, doa_ref, dob_ref, dga_ref, dgb_ref, dw_ref, loss_ref):
        @pl.when(pl.program_id(0) == 0)
        def _():
            loss_ref[...] = jnp.zeros_like(loss_ref)
            dw_ref[...] = jnp.zeros_like(dw_ref)

        halves = ((oa_ref, ga_ref, doa_ref, dga_ref, 0), (ob_ref, gb_ref, dob_ref, dgb_ref, 512))
        out = x_ref[...]
        gated = []
        for o_ref, g_ref, _, _, lo in halves:
            g = g_ref[...]
            sg = _sigmoid(g)
            y = (o_ref[...] * (g * sg)).astype(BF16)
            out = out + _dot(y, w_ref[lo:lo + 512, :])
            gated.append((y, g, sg))
        diff = out - t_ref[...]
        dout = diff * (1.0 / D_MODEL)
        dout_ref[...] = dout
        loss_ref[...] = loss_ref[...] + jnp.sum(diff * diff) * (0.5 / D_MODEL)
        db = dout.astype(BF16)
        for (o_ref, _, do_ref, dg_ref, lo), (y, g, sg) in zip(halves, gated):
            dw_ref[lo:lo + 512, :] = dw_ref[lo:lo + 512, :] + _dot_tn(y, db)
            dy = _dot_nt(db, w_ref[lo:lo + 512, :])
            do_ref[...] = (dy * (g * sg)).astype(BF16)
            dg_ref[...] = (dy * o_ref[...] * (sg * (1.0 + g * (1.0 - sg)))).astype(BF16)

    rows = lambda w: pl.BlockSpec((TM, w), lambda i: (i, 0))
    return pl.pallas_call(
        body, name="out_proj", grid=(T // TM,),
        in_specs=[rows(512), rows(512), rows(512), rows(512), rows(D_MODEL), rows(D_MODEL),
                  _const_spec((D_MODEL, D_MODEL))],
        out_specs=[rows(D_MODEL)] + [rows(512)] * 4 + [_const_spec((D_MODEL, D_MODEL)), _const_spec((8, 128))],
        out_shape=[jax.ShapeDtypeStruct((T, D_MODEL), F32)] + [jax.ShapeDtypeStruct((T, 512), BF16)] * 4
        + [jax.ShapeDtypeStruct((D_MODEL, D_MODEL), F32), jax.ShapeDtypeStruct((8, 128), F32)],
        compiler_params=_params(1),
    )(oa, ob, ga, gb, x2, tgt, wout_bf)


def _qk_grad(qraw, kraw, dqrot, dk2, dv2, qg512, kg128, g512, cos, sa, sb):
    T = qraw.shape[0]

    def body(qraw_ref, kraw_ref, dqrot_ref, dk2_ref, dv2_ref, qg_ref, kg_ref, g_ref, cos_ref, sa_ref, sb_ref,
             dqa_ref, dkv_ref, dqg_ref, dkg_ref):
        @pl.when(pl.program_id(0) == 0)
        def _():
            dqg_ref[...] = jnp.zeros_like(dqg_ref)
            dkg_ref[...] = jnp.zeros_like(dkg_ref)

        cosv, sav, sbv = cos_ref[...], sa_ref[...], sb_ref[...]
        m0 = _lane_mask(TM) > 0.5

        def head_norm_bwd(raw, dn_rot, gmat, gain):
            r = lax.rsqrt(_dot_b(raw * raw, gmat) + EPS)
            n = raw * r
            dn = dn_rot * gain
            return r * (dn - n * _dot_b(dn * n, gmat)), jnp.sum(dn_rot * n, axis=0, keepdims=True)

        def fold(ref):
            a, b = ref[:, 0:128], ref[:, 128:256]
            return jnp.where(m0, a + pltpu.roll(a, 64, 1), b + pltpu.roll(b, 64, 1))

        dqn = jnp.concatenate([_rope_bwd(dqrot_ref[:, s * 128:(s + 1) * 128], cosv, sav, sbv) for s in range(4)], axis=1)
        dqa, dqg = head_norm_bwd(qraw_ref[...], dqn, g_ref[...], qg_ref[...])
        dka, dkg = head_norm_bwd(kraw_ref[...], _rope_bwd(fold(dk2_ref), cosv, sav, sbv), g_ref[0:128, 0:128], kg_ref[...])
        dqa_ref[...] = dqa.astype(BF16)
        dkv_ref[:, 0:128] = dka.astype(BF16)
        dkv_ref[:, 128:256] = fold(dv2_ref).astype(BF16)
        dqg_ref[...] = dqg_ref[...] + dqg
        dkg_ref[...] = dkg_ref[...] + dkg

    rows = lambda w: pl.BlockSpec((TM, w), lambda i: (i, 0))
    return pl.pallas_call(
        body, name="qk_grad", grid=(T // TM,),
        in_specs=[rows(512), rows(128), rows(512), rows(256), rows(256), _const_spec((1, 512)), _const_spec((1, 128)),
                  _const_spec((512, 512)), rows(128), rows(128), rows(128)],
        out_specs=[rows(512), rows(256), _const_spec((1, 512)), _const_spec((1, 128))],
        out_shape=[jax.ShapeDtypeStruct((T, 512), BF16), jax.ShapeDtypeStruct((T, 256), BF16),
                   jax.ShapeDtypeStruct((1, 512), F32), jax.ShapeDtypeStruct((1, 128), F32)],
        compiler_params=_params(1),
    )(qraw, kraw, dqrot, dk2, dv2, qg512, kg128, g512, cos, sa, sb)


_PIECES = ((QA, 512), (KA, 256), (GA, 512), (QB, 512), (KB, 512), (VB, 512), (GB, 512))


def _w_in_grad(h, pieces):
    T = h.shape[0]

    def body(h_ref, *refs):
        dw_ref = refs[-1]

        @pl.when(pl.program_id(0) == 0)
        def _():
            dw_ref[...] = jnp.zeros_like(dw_ref)

        hb = h_ref[...]
        for (lo, width), p_ref in zip(_PIECES, refs[:-1]):
            dw_ref[lo:lo + width, :] = dw_ref[lo:lo + width, :] + _dot_tn(p_ref[...].astype(BF16), hb)

    rows = lambda w: pl.BlockSpec((TM, w), lambda i: (i, 0))
    return pl.pallas_call(
        body, name="w_in_grad", grid=(T // TM,),
        in_specs=[rows(D_MODEL)] + [rows(w) for _, w in _PIECES],
        out_specs=_const_spec((IN_WIDTH, D_MODEL)),
        out_shape=jax.ShapeDtypeStruct((IN_WIDTH, D_MODEL), F32),
        compiler_params=_params(1),
    )(h, *pieces)


def _chip_exchange(src_ref, dst_ref, sems):
    _, _, c, chips = _place()
    return [_remote(src_ref.at[2 * cx + cy], dst_ref.at[j], sems, j, (cx, cy, c)) for j, (cx, cy) in enumerate(chips)]


def _x_grad(x2, dout, pieces, w_bf, ng, cin4):
    T = x2.shape[0]
    npc = len(_PIECES)
    steps = T // TM

    def body(x_ref, dout_ref, *refs):
        w_ref, ng_ref, cin_ref, gx_ref, dng_ref, rin_ref, ssem, rsem = refs[npc:]
        step = pl.program_id(0)
        copies = lambda: _chip_exchange(cin_ref, rin_ref, (ssem, rsem))

        @pl.when(step == 0)
        def _():
            dng_ref[...] = jnp.zeros_like(dng_ref)
            for cp in copies():
                cp.start()

        dh = jnp.zeros((TM, D_MODEL), F32)
        for (lo, width), p_ref in zip(_PIECES, refs[:npc]):
            dh = dh + _dot(p_ref[...].astype(BF16), w_ref[lo:lo + width, :])
        xb = x_ref[...]
        r = lax.rsqrt(jnp.mean(xb * xb, axis=-1, keepdims=True) + EPS)
        n = xb * r
        dn = dh * ng_ref[...]
        gx_ref[...] = dout_ref[...] + r * (dn - n * jnp.mean(dn * n, axis=-1, keepdims=True))
        dng_ref[...] = dng_ref[...] + jnp.sum(dh * n, axis=0, keepdims=True)

        @pl.when(step == steps - 1)
        def _():
            for cp in copies():
                cp.wait_recv()
            for cp in copies():
                cp.wait_send()

    rows = lambda w: pl.BlockSpec((TM, w), lambda i: (i, 0))
    hbm = pl.BlockSpec(memory_space=pl.ANY)
    return pl.pallas_call(
        body, name="x_grad", grid=(steps,),
        in_specs=[rows(D_MODEL), rows(D_MODEL)] + [rows(w) for _, w in _PIECES]
        + [_const_spec((IN_WIDTH, D_MODEL)), _const_spec((1, D_MODEL)), hbm],
        out_specs=[rows(D_MODEL), _const_spec((1, D_MODEL)), hbm],
        out_shape=[jax.ShapeDtypeStruct((T, D_MODEL), F32), jax.ShapeDtypeStruct((1, D_MODEL), F32),
                   jax.ShapeDtypeStruct((3,) + cin4.shape[1:], cin4.dtype)],
        scratch_shapes=[pltpu.SemaphoreType.DMA((3,)), pltpu.SemaphoreType.DMA((3,))],
        compiler_params=_params(1),
    )(x2, dout, *pieces, w_bf, ng, cin4)


HBM = pl.BlockSpec(memory_space=pl.ANY)


def _place():
    x, y, c = lax.axis_index("x"), lax.axis_index("y"), lax.axis_index("c")
    chips = [(1 - x, y), (x, 1 - y), (1 - x, 1 - y)]
    return x, y, c, chips


def _remote(src, dst, sems, k, to):
    return pltpu.make_async_remote_copy(src_ref=src, dst_ref=dst, send_sem=sems[0].at[k], recv_sem=sems[1].at[k],
                                        device_id=to, device_id_type=MESH)


def _gather_plan(src, dst, sems, relay):
    x, y, c, _ = _place()
    rows = src.shape[1] // 2
    parts = (pl.ds(0, rows), pl.ds(rows, rows))
    me, kx, ky, kd = 2 * x + y, 2 * (1 - x) + y, 2 * x + 1 - y, 2 * (1 - x) + 1 - y
    to_x, to_y, to_d, sib = (1 - x, y, c), (x, 1 - y, c), (1 - x, 1 - y, c), (x, y, 1 - c)
    mine = [(src.at[c, p], dst.at[me, c, p]) for p in parts]
    direct = [_remote(*mine[0], sems, 0, to_x), _remote(*mine[1], sems, 1, to_x),
              _remote(*mine[1], sems, 2, to_y), _remote(*mine[0], sems, 3, to_y)]
    arrived = [dst.at[kx, c, parts[0]], dst.at[kx, c, parts[1]], dst.at[ky, c, parts[1]], dst.at[ky, c, parts[0]]]
    if relay:
        relays = [_remote(arrived[0], arrived[0], sems, 4, to_y), _remote(arrived[2], arrived[2], sems, 5, to_x)]
    else:
        relays = []
        direct += [_remote(*mine[0], sems, 4, to_d), _remote(*mine[1], sems, 5, to_d)]
    relayed = [dst.at[kd, c, parts[0]], dst.at[kd, c, parts[1]]]
    forwards = [_remote(dst.at[k, c], dst.at[k, c], sems, 6 + n, sib) for n, k in enumerate((kx, ky, kd))]
    from_sib = [dst.at[k, 1 - c] for k in (kx, ky, kd)]
    return direct, arrived, relays, relayed, forwards, from_sib


def _gather_start(src, dst, sems, relay):
    direct = _gather_plan(src, dst, sems, relay)[0]
    for k in (0, 2, 1, 3):
        direct[k].start()
    for cp in direct[4:]:
        cp.start()


def _gather_finish(src, dst, sems, relay):
    direct, arrived, relays, relayed, forwards, from_sib = _gather_plan(src, dst, sems, relay)
    landed = lambda ref, k: _remote(ref, ref, sems, k, (0, 0, 0)).wait_recv()
    landed(arrived[0], 0)
    if relay:
        relays[0].start()
    landed(arrived[2], 2)
    if relay:
        relays[1].start()
    landed(arrived[1], 1)
    forwards[0].start()
    landed(arrived[3], 3)
    forwards[1].start()
    landed(relayed[0], 4)
    landed(relayed[1], 5)
    forwards[2].start()
    for n, ref in enumerate(from_sib):
        landed(ref, 6 + n)
    for cp in direct + relays + forwards:
        cp.wait_send()


def _own_slot(gathered, shard):
    me = 2 * lax.axis_index("x") + lax.axis_index("y")
    return lax.dynamic_update_slice(gathered, shard[None], (me, 0, 0, 0))


def _pair_exchange(g4):
    def body(src, dst, ssem, rsem):
        x, y, c, _ = _place()
        cp = _remote(src.at[:, pl.ds(1 - c, 1)], dst, (ssem, rsem), 0, (x, y, 1 - c))
        cp.start()
        cp.wait()

    return pl.pallas_call(
        body, name=f"pair_exchange_{g4.shape[2]}", in_specs=[HBM], out_specs=HBM,
        out_shape=jax.ShapeDtypeStruct((4, 1) + g4.shape[2:], g4.dtype),
        scratch_shapes=[pltpu.SemaphoreType.DMA((1,)), pltpu.SemaphoreType.DMA((1,))],
    )(g4)


def _share_halves(hin, hout, small):
    def body(hin_ref, hout_ref, small_ref, oin_ref, oout_ref, sall_ref, ssem, rsem, lsem):
        x, y, c, _ = _place()
        sems = (ssem, rsem)
        sib = (x, y, 1 - c)
        me = 4 * x + 2 * y + c
        own = pltpu.make_async_copy(small_ref, sall_ref.at[me], lsem.at[0])
        own.start()
        sent = [_remote(hin_ref, oin_ref, sems, 0, sib), _remote(hout_ref, oout_ref, sems, 1, sib)]
        flips = [(fx, fy, fc) for fx in (0, 1) for fy in (0, 1) for fc in (0, 1)][1:]
        for k, (fx, fy, fc) in enumerate(flips):
            sent.append(_remote(small_ref, sall_ref.at[me], sems, 2 + k, (x ^ fx, y ^ fy, c ^ fc)))
        for cp in sent:
            cp.start()
        _remote(hin_ref, oin_ref, sems, 0, sib).wait_recv()
        _remote(hout_ref, oout_ref, sems, 1, sib).wait_recv()
        for k, (fx, fy, fc) in enumerate(flips):
            src = 4 * (x ^ fx) + 2 * (y ^ fy) + (c ^ fc)
            _remote(small_ref, sall_ref.at[src], sems, 2 + k, sib).wait_recv()
        for cp in sent:
            cp.wait_send()
        own.wait()

    return pl.pallas_call(
        body, name="share_halves", in_specs=[HBM, HBM, HBM], out_specs=[HBM, HBM, HBM],
        out_shape=[jax.ShapeDtypeStruct(hin.shape, F32), jax.ShapeDtypeStruct(hout.shape, F32),
                   jax.ShapeDtypeStruct((8,) + small.shape, F32)],
        scratch_shapes=[pltpu.SemaphoreType.DMA((9,)), pltpu.SemaphoreType.DMA((9,)), pltpu.SemaphoreType.DMA((1,))],
    )(hin, hout, small)


def _add_half(cidx, full4, recv4):
    _, _, rows, width = full4.shape

    def body(c_ref, a_ref, b_ref, o_ref):
        o_ref[0] = (a_ref[0, 0] + b_ref[0, 0]).astype(BF16)

    return pl.pallas_call(
        body, name=f"add_half_{rows}",
        grid_spec=pltpu.PrefetchScalarGridSpec(
            num_scalar_prefetch=1, grid=(4,),
            in_specs=[pl.BlockSpec((1, 1, rows, width), lambda k, c: (k, c[0], 0, 0)),
                      pl.BlockSpec((1, 1, rows, width), lambda k, c: (k, 0, 0, 0))],
            out_specs=pl.BlockSpec((1, rows, width), lambda k, c: (k, 0, 0))),
        out_shape=jax.ShapeDtypeStruct((4, rows, width), BF16),
        compiler_params=_params(1),
    )(cidx, full4, recv4)


def _sum_chips(chip, own4, recv3):
    _, rows, width = recv3.shape
    rb = rows // 2

    def body(k_ref, a_ref, r_ref, o_ref):
        acc = a_ref[0].astype(F32)
        for s in range(3):
            acc = acc + r_ref[s].astype(F32)
        o_ref[...] = acc

    return pl.pallas_call(
        body, name=f"sum_chips_{rows}",
        grid_spec=pltpu.PrefetchScalarGridSpec(
            num_scalar_prefetch=1, grid=(rows // rb,),
            in_specs=[pl.BlockSpec((1, rb, width), lambda i, k: (k[0], i, 0)),
                      pl.BlockSpec((3, rb, width), lambda i, k: (0, i, 0))],
            out_specs=pl.BlockSpec((rb, width), lambda i, k: (i, 0))),
        out_shape=jax.ShapeDtypeStruct((rows, width), F32),
        compiler_params=_params(1),
    )(chip, own4, recv3)


def _adam_math(w, g, m, v):
    c1 = 1.0 - ADAM_B1 ** ADAM_STEP
    c2 = 1.0 - ADAM_B2 ** ADAM_STEP
    nm = ADAM_B1 * m + (1.0 - ADAM_B1) * g
    nv = ADAM_B2 * v + (1.0 - ADAM_B2) * (g * g)
    return -ADAM_LR * ((nm / c1) / (jnp.sqrt(nv / c2) + ADAM_EPS) + ADAM_WD * w), nm, nv


def _small_update(small_all, params):
    n = len(params)
    flat = [a for p in params for a in p]

    def body(s_ref, *refs):
        ins, loss_ref, outs = refs[:3 * n], refs[3 * n], refs[3 * n + 1:]
        total = s_ref[0]
        for d in range(1, 8):
            total = total + s_ref[d]
        loss_ref[...] = total[n:n + 1, 0:1]
        for r in range(n):
            w_ref, m_ref, v_ref = ins[3 * r:3 * r + 3]
            g = total[r:r + 1, 0:w_ref.shape[1]]
            outs[4 * r][...] = g
            outs[4 * r + 1][...], outs[4 * r + 2][...], outs[4 * r + 3][...] = _adam_math(w_ref[...], g, m_ref[...], v_ref[...])

    whole = lambda a: pl.BlockSpec(a.shape, lambda i: (0,) * a.ndim)
    out_shape = [jax.ShapeDtypeStruct((1, 1), F32)] + [jax.ShapeDtypeStruct(p[0].shape, F32) for p in params for _ in range(4)]
    res = pl.pallas_call(
        body, name="small_update", grid=(1,),
        in_specs=[whole(small_all)] + [whole(a) for a in flat], out_specs=[whole(s) for s in out_shape],
        out_shape=out_shape, compiler_params=_params(1),
    )(small_all, *flat)
    return res[0], [tuple(res[1 + 4 * r:5 + 4 * r]) for r in range(n)]


def _adamw_halves(cidx, w, own, recv, m, v):
    rows, width = w.shape
    rb = rows // 4

    def body(c_ref, w_ref, own_ref, recv_ref, m_ref, v_ref, g_ref, d_ref, nm_ref, nv_ref):
        mine = (pl.program_id(0) // 2) == c_ref[0]
        g = jnp.where(mine, own_ref[...], recv_ref[...])
        g_ref[...] = g
        d_ref[...], nm_ref[...], nv_ref[...] = _adam_math(w_ref[...], g, m_ref[...], v_ref[...])

    full = pl.BlockSpec((rb, width), lambda i, c: (i, 0))
    half = pl.BlockSpec((rb, width), lambda i, c: (i % 2, 0))
    return pl.pallas_call(
        body, name=f"adamw_halves_{rows}",
        grid_spec=pltpu.PrefetchScalarGridSpec(
            num_scalar_prefetch=1, grid=(4,),
            in_specs=[full, half, half, full, full], out_specs=[full] * 4),
        out_shape=[jax.ShapeDtypeStruct((rows, width), F32)] * 4,
        compiler_params=_params(1),
    )(cidx, w, own, recv, m, v)


def _constants():
    idx = jnp.arange(512)
    g512 = jnp.where(idx[:, None] // HEAD_DIM == idx[None, :] // HEAD_DIM, 1.0 / HEAD_DIM, 0.0).astype(BF16)
    j = jnp.arange(SB_T)
    ublk = jnp.where(j[:, None] >= j[None, :], 1.0, 0.0).astype(BF16)
    pblk = jnp.where(j[:, None] < j[None, :], 1.0, 0.0).astype(BF16)
    return g512, jnp.concatenate([ublk, ublk], axis=0), pblk


def _pad_rows(v, width):
    return jnp.pad(v, ((0, 0), (0, width - v.shape[1])))


def _pair_sum(cidx, partial, rows):
    g4 = partial.reshape(4, 2, rows, D_MODEL)
    return _add_half(cidx, g4, _pair_exchange(g4))


def _step(x2, tgt, positions, norm_gain, q_norm_gain, k_norm_gain, sinks, win_shard, wout_shard, nbatch, seq):
    g512, ublk, pblk = _constants()
    qg512 = jnp.tile(q_norm_gain, (1, 8))
    kg128 = jnp.tile(k_norm_gain, (1, 2))
    sink1 = sinks.reshape(8)
    cidx = lax.axis_index("c").astype(jnp.int32).reshape(1)
    chip = (2 * lax.axis_index("x") + lax.axis_index("y")).astype(jnp.int32).reshape(1)

    h, (cos, sa, sb), w_bf = _prologue(x2, norm_gain, positions.reshape(-1, 1), win_shard)
    (qraw, kraw, qrot, k2, v2, ga, qb, kb, vb, gb), wout_bf = _norm_proj(
        h, w_bf, qg512, kg128, g512, cos, sa, sb, wout_shard)
    oa = _swa_fwd(sink1, qrot, k2, v2, nbatch, seq)
    ob, wst, ost = _sb_fwd(qb, kb, vb, ublk, nbatch, seq)
    dout, doa, dob, dga, dgb, dwout, loss_acc = _out_proj(oa, ob, ga, gb, x2, tgt, wout_bf)

    cout4 = _pair_sum(cidx, dwout, SHARD_OUT // 2)
    dqrot, dk2, dv2, dsink, rout3 = _swa_bwd(sink1, qrot, k2, v2, doa, cout4, nbatch, seq)
    dqb, dkb, dvb = _sb_bwd(qb, kb, vb, dob, wst, ost, pblk, nbatch, seq)
    dk2, dv2 = _unshift(dk2, nbatch, seq), _unshift(dv2, nbatch, seq)
    dqa, dkv, dqg, dkg = _qk_grad(qraw, kraw, dqrot, dk2, dv2, qg512, kg128, g512, cos, sa, sb)
    pieces = (dqa, dkv, dga, dqb, dkb, dvb, dgb)
    cin4 = _pair_sum(cidx, _w_in_grad(h, pieces), SHARD_IN // 2)
    gx, dng, rin3 = _x_grad(x2, dout, pieces, w_bf, norm_gain, cin4)
    own_in, own_out = _sum_chips(chip, cin4, rin3), _sum_chips(chip, cout4, rout3)

    dqg64 = dqg.reshape(8, HEAD_DIM).sum(axis=0, keepdims=True)
    dkg64 = dkg.reshape(2, HEAD_DIM).sum(axis=0, keepdims=True)
    small = jnp.concatenate([dng, _pad_rows(dqg64, D_MODEL), _pad_rows(dkg64, D_MODEL),
                             _pad_rows(dsink[:, 0].reshape(1, 8), D_MODEL), _pad_rows(loss_acc[0:1, 0:1], D_MODEL),
                             jnp.zeros((3, D_MODEL), F32)], axis=0)
    sib_in, sib_out, small_all = _share_halves(own_in, own_out, small)
    return gx, cidx, (own_in, sib_in), (own_out, sib_out), small_all


def kernel(x, positions, norm_gain, w_in, q_norm_gain, k_norm_gain, sinks, w_out, loss_target, m_norm_gain, m_w_in, m_q_norm_gain, m_k_norm_gain, m_sinks, m_w_out, v_norm_gain, v_w_in, v_q_norm_gain, v_k_norm_gain, v_sinks, v_w_out):
    nbatch, seq, _ = x.shape
    T = nbatch * seq
    x2 = x.reshape(T, D_MODEL)
    tgt = loss_target.reshape(T, D_MODEL)
    tr = lambda a: jnp.swapaxes(a[0], 0, 1)
    win_t, m_win_t, v_win_t = tr(w_in), tr(m_w_in), tr(v_w_in)

    win_shard = win_t.astype(BF16).reshape(2, SHARD_IN // 2, D_MODEL)
    wout_shard = w_out[0].astype(BF16).reshape(2, SHARD_OUT // 2, D_MODEL)
    gx, cidx, g_in_halves, g_out_halves, small_all = _step(
        x2, tgt, positions, norm_gain, q_norm_gain, k_norm_gain, sinks, win_shard, wout_shard, nbatch, seq)

    g_in, d_in, nm_in, nv_in = [jnp.swapaxes(a, 0, 1) for a in
                                _adamw_halves(cidx, win_t, *g_in_halves, m_win_t, v_win_t)]
    g_out, d_out, nm_out, nv_out = _adamw_halves(cidx, w_out[0], *g_out_halves, m_w_out[0], v_w_out[0])
    loss, small = _small_update(small_all, [(norm_gain, m_norm_gain, v_norm_gain), (q_norm_gain, m_q_norm_gain, v_q_norm_gain),
                                            (k_norm_gain, m_k_norm_gain, v_k_norm_gain), (sinks, m_sinks, v_sinks)])
    (g_ng, d_ng, m_ng, v_ng), (g_qg, d_qg, m_qg, v_qg), (g_kg, d_kg, m_kg, v_kg), (g_sk, d_sk, m_sk, v_sk) = small
    return (loss.reshape(()), gx.reshape(nbatch, seq, D_MODEL),
            g_ng, g_in[None], g_qg, g_kg, g_sk, g_out[None],
            d_ng, d_in[None], d_qg, d_kg, d_sk, d_out[None],
            m_ng, nm_in[None], m_qg, m_kg, m_sk, nm_out[None],
            v_ng, nv_in[None], v_qg, v_kg, v_sk, nv_out[None])
```

```python
import functools
import math

import jax
import jax.numpy as jnp
from jax import lax
from jax.experimental import pallas as pl
from jax.experimental.pallas import tpu as pltpu

F32 = jnp.float32
BF16 = jnp.bfloat16

D_MODEL = 1024
HEAD_DIM = 64
BLOCK = 128
ROPE_THETA = 10000.0
EPS = 1e-6
QA, KA, VA, GA, QB, KB, VB, GB = 0, 512, 640, 768, 1280, 1792, 2304, 2816
IN_WIDTH = 3328
SHARD_IN = IN_WIDTH // 4
SHARD_OUT = D_MODEL // 4
SCALE = 1.0 / math.sqrt(HEAD_DIM)
NEG = -1e30

ADAM_LR, ADAM_B1, ADAM_B2, ADAM_EPS, ADAM_WD, ADAM_STEP = 0.001, 0.9, 0.999, 1e-08, 0.01, 10

TM = 512
VMEM_LIMIT = 56 * 1024 * 1024
MESH = pl.DeviceIdType.MESH


def _dot(a, b):
    return jnp.dot(a, b, preferred_element_type=F32)


def _dot_nt(a, b):
    return lax.dot_general(a, b, (((1,), (1,)), ((), ())), preferred_element_type=F32)


def _dot_tn(a, b):
    return lax.dot_general(a, b, (((0,), (0,)), ((), ())), preferred_element_type=F32)


def _dot_b(a, m):
    return _dot(a.astype(BF16), m)


def _params(n_axes=None, vmem=VMEM_LIMIT):
    sem = None if n_axes is None else ("arbitrary",) * n_axes
    return pltpu.CompilerParams(dimension_semantics=sem, vmem_limit_bytes=vmem)


def _const_spec(shape):
    nd = len(shape)
    return pl.BlockSpec(shape, lambda *_: (0,) * nd)


def _rope_fwd(x, cos, sa, sb):
    return x * cos + pltpu.roll(x, 96, 1) * sa + pltpu.roll(x, 32, 1) * sb


def _rope_bwd(d, cos, sa, sb):
    return d * cos - pltpu.roll(d, 96, 1) * sa - pltpu.roll(d, 32, 1) * sb


def _lane_mask(rows, dtype=F32):
    lane = lax.broadcasted_iota(jnp.int32, (rows, 128), 1)
    return jnp.where(lane < HEAD_DIM, 1.0, 0.0).astype(dtype)


def _prologue(x2, ng, pos, win_shard):
    T = x2.shape[0]
    steps = T // TM
    half = HEAD_DIM // 2
    inv_freq = jnp.tile(ROPE_THETA ** (-jnp.arange(half, dtype=F32) * 2.0 / HEAD_DIM), 4).reshape(1, 128)

    def body(x_ref, ng_ref, pos_ref, freq_ref, wi_ref, h_ref, cos_ref, sa_ref, sb_ref, ai_ref, ssem, rsem):
        @pl.when(pl.program_id(0) == 0)
        def _():
            _gather_start(wi_ref, ai_ref, (ssem, rsem), relay=True)

        xb = x_ref[...]
        r = lax.rsqrt(jnp.mean(xb * xb, axis=-1, keepdims=True) + EPS)
        h_ref[...] = (xb * r * ng_ref[...]).astype(BF16)
        ang = pos_ref[...].astype(F32) * freq_ref[...]
        first = (lax.broadcasted_iota(jnp.int32, (TM, 128), 1) & (HEAD_DIM - 1)) < half
        sn = jnp.sin(ang)
        cos_ref[...] = jnp.cos(ang)
        sa_ref[...] = jnp.where(first, -sn, 0.0)
        sb_ref[...] = jnp.where(first, 0.0, sn)

        @pl.when(pl.program_id(0) == steps - 1)
        def _():
            _gather_finish(wi_ref, ai_ref, (ssem, rsem), relay=True)

    rows = lambda w: pl.BlockSpec((TM, w), lambda i: (i, 0))
    hbm = pl.BlockSpec(memory_space=pl.ANY)
    h, cos, sa, sb, gathered = pl.pallas_call(
        body, name="prologue", grid=(steps,),
        in_specs=[rows(D_MODEL), _const_spec((1, D_MODEL)), rows(1), _const_spec((1, 128)), hbm],
        out_specs=[rows(D_MODEL), rows(128), rows(128), rows(128), hbm],
        out_shape=[jax.ShapeDtypeStruct((T, D_MODEL), BF16)] + [jax.ShapeDtypeStruct((T, 128), F32)] * 3
        + [jax.ShapeDtypeStruct((4,) + win_shard.shape, win_shard.dtype)],
        scratch_shapes=[pltpu.SemaphoreType.DMA((9,)), pltpu.SemaphoreType.DMA((9,))],
        compiler_params=_params(1),
    )(x2, ng, pos, inv_freq, win_shard)
    return h, (cos, sa, sb), _own_slot(gathered, win_shard).reshape(IN_WIDTH, D_MODEL)


def _norm_proj(h, w_bf, qg512, kg128, g512, cos, sa, sb, wout_shard):
    T = h.shape[0]
    steps = T // TM

    def body(h_ref, w_ref, qg_ref, kg_ref, g_ref, cos_ref, sa_ref, sb_ref, wo_ref,
             qraw_ref, kraw_ref, qrot_ref, k2_ref, v2_ref, ga_ref, qb_ref, kb_ref, vb_ref, gb_ref, ao_ref,
             ssem, rsem):
        @pl.when(pl.program_id(0) == 0)
        def _():
            _gather_start(wo_ref, ao_ref, (ssem, rsem), relay=False)

        h = h_ref[...]
        cosv, sav, sbv = cos_ref[...], sa_ref[...], sb_ref[...]
        m0 = _lane_mask(TM)

        def dup(v):
            v0 = v * m0
            v1 = v - v0
            return v0 + pltpu.roll(v0, 64, 1), v1 + pltpu.roll(v1, 64, 1)

        qa = _dot_nt(h, w_ref[QA:KA, :])
        qraw_ref[...] = qa
        qn = qa * lax.rsqrt(_dot_b(qa * qa, g_ref[...]) + EPS) * qg_ref[...]
        for s in range(4):
            qs = _rope_fwd(qn[:, s * 128:(s + 1) * 128], cosv, sav, sbv)
            qrot_ref[:, s * 128:(s + 1) * 128] = (qs * SCALE).astype(BF16)
        ka = _dot_nt(h, w_ref[KA:VA, :])
        kraw_ref[...] = ka
        kn = ka * lax.rsqrt(_dot_b(ka * ka, g_ref[0:128, 0:128]) + EPS) * kg_ref[...]
        k0, k1 = dup(_rope_fwd(kn, cosv, sav, sbv))
        k2_ref[:, 0:128] = k0.astype(BF16)
        k2_ref[:, 128:256] = k1.astype(BF16)
        v0, v1 = dup(_dot_nt(h, w_ref[VA:GA, :]))
        v2_ref[:, 0:128] = v0.astype(BF16)
        v2_ref[:, 128:256] = v1.astype(BF16)
        ga_ref[...] = _dot_nt(h, w_ref[GA:QB, :])
        qb_ref[...] = (_dot_nt(h, w_ref[QB:KB, :]) * SCALE).astype(BF16)
        kb_ref[...] = _dot_nt(h, w_ref[KB:VB, :]).astype(BF16)
        vb_ref[...] = _dot_nt(h, w_ref[VB:GB, :]).astype(BF16)
        gb_ref[...] = _dot_nt(h, w_ref[GB:IN_WIDTH, :])

        @pl.when(pl.program_id(0) == steps - 1)
        def _():
            _gather_finish(wo_ref, ao_ref, (ssem, rsem), relay=False)

    def rows(w):
        return pl.BlockSpec((TM, w), lambda i: (i, 0))

    outs = [(512, F32), (128, F32), (512, BF16), (256, BF16), (256, BF16), (512, F32),
            (512, BF16), (512, BF16), (512, BF16), (512, F32)]
    hbm = pl.BlockSpec(memory_space=pl.ANY)
    res = pl.pallas_call(
        body, name="norm_proj", grid=(steps,),
        in_specs=[rows(D_MODEL), _const_spec((IN_WIDTH, D_MODEL)), _const_spec((1, 512)),
                  _const_spec((1, 128)), _const_spec((512, 512)), rows(128), rows(128), rows(128), hbm],
        out_specs=[rows(w) for w, _ in outs] + [hbm],
        out_shape=[jax.ShapeDtypeStruct((T, w), dt) for w, dt in outs]
        + [jax.ShapeDtypeStruct((4,) + wout_shard.shape, wout_shard.dtype)],
        scratch_shapes=[pltpu.SemaphoreType.DMA((9,)), pltpu.SemaphoreType.DMA((9,))],
        compiler_params=_params(1),
    )(h, w_bf, qg512, kg128, g512, cos, sa, sb, wout_shard)
    return res[:-1], _own_slot(res[-1], wout_shard).reshape(D_MODEL, D_MODEL)


SWA_Q = 512
SWA_SUB = SWA_Q // BLOCK


def _swa_scores(q, kst, mask, sink_ref, kv):
    s_all = _dot_nt(q, kst)
    first = lax.broadcasted_iota(jnp.int32, (256, 1), 0) < 128
    probs, stats = [], []
    for hh in range(2):
        sink = jnp.where(first, sink_ref[kv * 4 + hh], sink_ref[kv * 4 + 2 + hh])
        s = jnp.where(mask, s_all[:, hh * 256:(hh + 1) * 256], NEG)
        m = jnp.maximum(jnp.max(s, axis=1, keepdims=True), sink)
        pe = jnp.exp(s - m)
        inv = 1.0 / (jnp.sum(pe, axis=1, keepdims=True) + jnp.exp(sink - m))
        probs.append(pe * inv)
        stats.append(jnp.exp(sink - m) * inv)
    return probs, stats


def _swa_mask(has_prev):
    r = lax.broadcasted_iota(jnp.int32, (256, 256), 0) & 127
    c = lax.broadcasted_iota(jnp.int32, (256, 256), 1)
    band = (c > r) & (c <= r + 128)
    return band if has_prev is True else band & ((c >= 128) | has_prev)


def _stack_pairs(ref, rows, kv):
    return jnp.concatenate([ref[rows, (2 * kv) * 128:(2 * kv + 1) * 128], ref[rows, (2 * kv + 1) * 128:(2 * kv + 2) * 128]], axis=0)


def _swa_keys(prev_ref, main_ref, s, kv, m0b):
    cols = slice(kv * 128, (kv + 1) * 128)
    prev = prev_ref[:, cols] if s == 0 else main_ref[(s - 1) * 128:s * 128, cols]
    kc = jnp.concatenate([prev, main_ref[s * 128:(s + 1) * 128, cols]], axis=0)
    k0 = kc * m0b
    return jnp.concatenate([k0, kc - k0], axis=0)


def _swa_fwd(sinks, qrot, k2, v2, nbatch, seq):
    ni = seq // SWA_Q
    T = nbatch * seq

    def body(sink_ref, q_ref, kp_ref, km_ref, vp_ref, vm_ref, o_ref):
        i = pl.program_id(1)
        m0b = _lane_mask(256, BF16)
        for s in range(SWA_SUB):
            mask = _swa_mask(True if s else i > 0)
            rows = slice(s * 128, (s + 1) * 128)
            for kv in range(2):
                kst = _swa_keys(kp_ref, km_ref, s, kv, m0b)
                vst = _swa_keys(vp_ref, vm_ref, s, kv, m0b)
                probs, _ = _swa_scores(_stack_pairs(q_ref, rows, kv), kst, mask, sink_ref, kv)
                o2 = _dot(jnp.concatenate(probs, axis=1).astype(BF16), vst)
                o_ref[rows, kv * 256:kv * 256 + 128] = o2[0:128].astype(BF16)
                o_ref[rows, kv * 256 + 128:(kv + 1) * 256] = o2[128:256].astype(BF16)

    main = lambda b, i: (b * ni + i, 0)
    prev = lambda b, i: ((b * ni + i) * SWA_SUB - jnp.where(i > 0, 1, 0), 0)
    return pl.pallas_call(
        body, name="swa_fwd", grid=(nbatch, ni),
        in_specs=[pl.BlockSpec(memory_space=pltpu.SMEM), pl.BlockSpec((SWA_Q, 512), main),
                  pl.BlockSpec((128, 256), prev), pl.BlockSpec((SWA_Q, 256), main),
                  pl.BlockSpec((128, 256), prev), pl.BlockSpec((SWA_Q, 256), main)],
        out_specs=pl.BlockSpec((SWA_Q, 512), main),
        out_shape=jax.ShapeDtypeStruct((T, 512), BF16),
        compiler_params=_params(2),
    )(sinks, qrot, k2, k2, v2, v2)


def _swa_bwd(sinks, qrot, k2, v2, doa, cout4, nbatch, seq):
    ni = seq // SWA_Q
    T = nbatch * seq

    def body(sink_ref, q_ref, kp_ref, km_ref, vp_ref, vm_ref, do_ref, cout_ref,
             dq_ref, dk_ref, dv_ref, ds_ref, rout_ref, dkc, dvc, ssem, rsem):
        b, i = pl.program_id(0), pl.program_id(1)
        copies = lambda: _chip_exchange(cout_ref, rout_ref, (ssem, rsem))

        @pl.when((b == 0) & (i == 0))
        def _():
            ds_ref[...] = jnp.zeros_like(ds_ref)
            for cp in copies():
                cp.start()

        @pl.when((b == nbatch - 1) & (i == ni))
        def _():
            for cp in copies():
                cp.wait_recv()
            for cp in copies():
                cp.wait_send()

        @pl.when(i == 0)
        def _():
            dkc[...] = jnp.zeros_like(dkc)
            dvc[...] = jnp.zeros_like(dvc)

        @pl.when(i < ni)
        def _():
            m0b = _lane_mask(256, BF16)
            m0 = _lane_mask(128) > 0.5
            for kv in range(2):
                kcols = slice(kv * 128, (kv + 1) * 128)
                dk_own, dv_own = dkc[:, kcols], dvc[:, kcols]
                for s in range(SWA_SUB):
                    mask = _swa_mask(True if s else i > 0)
                    rows = slice(s * 128, (s + 1) * 128)
                    kst = _swa_keys(kp_ref, km_ref, s, kv, m0b)
                    vst = _swa_keys(vp_ref, vm_ref, s, kv, m0b)
                    q, do = _stack_pairs(q_ref, rows, kv), _stack_pairs(do_ref, rows, kv)
                    probs, psink = _swa_scores(q, kst, mask, sink_ref, kv)
                    dp_all = _dot_nt(do, vst)
                    ds_parts = []
                    for hh in range(2):
                        dp = dp_all[:, hh * 256:(hh + 1) * 256]
                        delta = jnp.sum(probs[hh] * dp, axis=1, keepdims=True)
                        ds_parts.append(probs[hh] * (dp - delta))
                        dsink = psink[hh] * delta
                        for pr in range(2):
                            h = kv * 4 + pr * 2 + hh
                            ds_ref[h:h + 1, :] = ds_ref[h:h + 1, :] - jnp.sum(dsink[pr * 128:(pr + 1) * 128])
                    ds_all = jnp.concatenate(ds_parts, axis=1).astype(BF16)
                    p_all = jnp.concatenate(probs, axis=1).astype(BF16)
                    dq2 = _dot(ds_all, kst) * SCALE
                    dq_ref[rows, kv * 256:kv * 256 + 128] = dq2[0:128]
                    dq_ref[rows, kv * 256 + 128:(kv + 1) * 256] = dq2[128:256]
                    dkst = _dot_tn(ds_all, q)
                    dvst = _dot_tn(p_all, do)
                    dk_ref[rows, kcols] = dk_own + jnp.where(m0, dkst[0:128], dkst[256:384])
                    dv_ref[rows, kcols] = dv_own + jnp.where(m0, dvst[0:128], dvst[256:384])
                    dk_own = jnp.where(m0, dkst[128:256], dkst[384:512])
                    dv_own = jnp.where(m0, dvst[128:256], dvst[384:512])
                dkc[:, kcols] = dk_own
                dvc[:, kcols] = dv_own

        @pl.when(i == ni)
        def _():
            dk_ref[...] = jnp.zeros_like(dk_ref)
            dv_ref[...] = jnp.zeros_like(dv_ref)
            dk_ref[0:128, :] = dkc[...]
            dv_ref[0:128, :] = dvc[...]

    main = lambda b, i: (b * ni + jnp.minimum(i, ni - 1), 0)
    prev = lambda b, i: ((b * ni + jnp.minimum(i, ni - 1)) * SWA_SUB - jnp.where(jnp.minimum(i, ni - 1) > 0, 1, 0), 0)
    shifted = lambda b, i: (b * (ni + 1) + i, 0)
    tpad = nbatch * (ni + 1) * SWA_Q
    return pl.pallas_call(
        body, name="swa_bwd", grid=(nbatch, ni + 1),
        in_specs=[pl.BlockSpec(memory_space=pltpu.SMEM), pl.BlockSpec((SWA_Q, 512), main),
                  pl.BlockSpec((128, 256), prev), pl.BlockSpec((SWA_Q, 256), main),
                  pl.BlockSpec((128, 256), prev), pl.BlockSpec((SWA_Q, 256), main),
                  pl.BlockSpec((SWA_Q, 512), main), pl.BlockSpec(memory_space=pl.ANY)],
        out_specs=[pl.BlockSpec((SWA_Q, 512), main), pl.BlockSpec((SWA_Q, 256), shifted),
                   pl.BlockSpec((SWA_Q, 256), shifted), _const_spec((8, 128)), pl.BlockSpec(memory_space=pl.ANY)],
        out_shape=[jax.ShapeDtypeStruct((T, 512), F32), jax.ShapeDtypeStruct((tpad, 256), F32),
                   jax.ShapeDtypeStruct((tpad, 256), F32), jax.ShapeDtypeStruct((8, 128), F32),
                   jax.ShapeDtypeStruct((3,) + cout4.shape[1:], cout4.dtype)],
        scratch_shapes=[pltpu.VMEM((128, 256), F32), pltpu.VMEM((128, 256), F32),
                        pltpu.SemaphoreType.DMA((3,)), pltpu.SemaphoreType.DMA((3,))],
        compiler_params=_params(2),
    )(sinks, qrot, k2, k2, v2, v2, doa, cout4)


SB_T = 256
SB_TQ = 2 * SB_T


def _sb_mask(kind):
    if kind == "full":
        return None
    rows = SB_T if kind == "B" else SB_TQ
    r = lax.broadcasted_iota(jnp.int32, (rows, 2 * SB_T), 0)
    c = lax.broadcasted_iota(jnp.int32, (rows, 2 * SB_T), 1)
    causal = (c & (SB_T - 1)) < r
    return causal | (r >= SB_T) if kind == "A" else causal


def _sb_rows(kind):
    return slice(SB_T, SB_TQ) if kind == "B" else slice(0, SB_TQ)


def _lower(x):
    return jnp.concatenate([jnp.zeros_like(x), x], axis=0)


def _sb_logits(neg_q, kst):
    nz = _dot_nt(neg_q, kst)
    sign = jnp.uint32(0x80000000)
    neg_abs = lax.bitcast_convert_type(lax.bitcast_convert_type(nz, jnp.uint32) | sign, F32)
    return nz, jnp.minimum(nz, 0.0) - jnp.log(1.0 + jnp.exp(neg_abs))


SB_NP = 4


def _pair_rows(ref, j, cols, m0b):
    kj = ref[pl.ds(pl.multiple_of(j * SB_T, SB_T), SB_T), cols]
    k0 = kj * m0b
    return jnp.concatenate([k0, kj - k0], axis=0)


def _bcast2(c0, c1):
    rows = c0.shape[0]
    return jnp.concatenate([jnp.broadcast_to(c0, (rows, SB_T)), jnp.broadcast_to(c1, (rows, SB_T))], axis=1)


def _rowsum2(x):
    return jnp.sum(x[:, 0:SB_T], axis=1, keepdims=True), jnp.sum(x[:, SB_T:2 * SB_T], axis=1, keepdims=True)


def _scan2(x, tri2):
    outs = []
    for h in range(2):
        xh = x[:, h * SB_T:(h + 1) * SB_T]
        hi = xh.astype(BF16)
        lo = (xh - hi.astype(F32)).astype(BF16)
        outs.append(_dot(jnp.concatenate([hi, lo], axis=1), tri2))
    return jnp.concatenate(outs, axis=1)


def _scan1(x, tri):
    xb = x.astype(BF16)
    return jnp.concatenate([_dot(xb[:, h * SB_T:(h + 1) * SB_T], tri) for h in range(2)], axis=1)


def _sb_fwd(qb, kb, vb, ublk, nbatch, seq):
    nq, nk = seq // SB_TQ, seq // SB_T
    T = nbatch * seq

    def body(q_ref, k_ref, v_ref, u_ref, o_ref, wst_ref, ost_ref, wbuf, obuf, sems):
        b, i = pl.program_id(0), pl.program_id(2)
        m0b = _lane_mask(SB_T, BF16)
        u = u_ref[...]
        pairs = [slice(pp * 128, (pp + 1) * 128) for pp in range(SB_NP)]
        neg_qs = [-q_ref[:, cols] for cols in pairs]

        def stores(j, slot):
            tix = (b * nq + i) * nk + j
            return (pltpu.make_async_copy(wbuf.at[slot], wst_ref.at[tix], sems.at[0, slot]),
                    pltpu.make_async_copy(obuf.at[slot], ost_ref.at[tix], sems.at[1, slot]))

        def tile(j, carries, kind, slot, wait):
            rows, mask = _sb_rows(kind), _sb_mask(kind)
            if wait:
                for cp in stores(j, slot):
                    cp.wait()
            out = []
            for pp, (cols, neg_q, (c0, c1, acc)) in enumerate(zip(pairs, neg_qs, carries)):
                kst = _pair_rows(k_ref, j, cols, m0b)
                vst = _pair_rows(v_ref, j, cols, m0b)
                nz, lb = _sb_logits(neg_q[rows], kst)
                if mask is not None:
                    lb = jnp.where(mask, lb, 0.0)
                incl = _scan2(lb, u)
                w = jnp.exp(incl - nz if kind == "B" else incl + _bcast2(c0, c1) - nz)
                if mask is not None:
                    w = jnp.where(mask, w, 0.0)
                wb = w.astype(BF16)
                wbuf[slot, pp, rows, :] = wb
                obuf[slot, pp, rows, :] = lb.astype(BF16)
                if kind == "B":
                    wbuf[slot, pp, 0:SB_T, :] = jnp.zeros((SB_T, 2 * SB_T), BF16)
                    obuf[slot, pp, 0:SB_T, :] = jnp.zeros((SB_T, 2 * SB_T), BF16)
                d0, d1, da = incl[:, 0:1], incl[:, SB_T:SB_T + 1], _dot(wb, vst)
                if kind == "B":
                    d0, d1, da = _lower(d0), _lower(d1), _lower(da)
                out.append((c0 + d0, c1 + d1, acc + da))
            for cp in stores(j, slot):
                cp.start()
            return tuple(out)

        zc = jnp.zeros((SB_TQ, 1), F32)
        carries = tile(2 * i + 1, ((zc, zc, jnp.zeros((SB_TQ, 128), F32)),) * SB_NP, "B", 0, False)
        carries = tile(2 * i, carries, "A", 1, False)

        def two(jj, cr):
            cr = tile(2 * i - 1 - 2 * jj, cr, "full", 0, True)
            return tile(2 * i - 2 - 2 * jj, cr, "full", 1, True)

        carries = lax.fori_loop(0, i, two, carries)
        for cols, carry in zip(pairs, carries):
            o_ref[:, cols] = carry[2].astype(BF16)
        for slot in range(2):
            for cp in stores(0, slot):
                cp.wait()

    wide = 128 * SB_NP
    blk = lambda b, g, i: (b * nq + i, g)
    full = lambda b, g, i: (b, g)
    hbm = pl.BlockSpec(memory_space=pl.ANY)
    tiles = jax.ShapeDtypeStruct((nbatch * nq * nk, SB_NP, SB_TQ, 2 * SB_T), BF16)
    return pl.pallas_call(
        body, name="sb_fwd", grid=(nbatch, 4 // SB_NP, nq),
        in_specs=[pl.BlockSpec((SB_TQ, wide), blk), pl.BlockSpec((seq, wide), full), pl.BlockSpec((seq, wide), full),
                  _const_spec((2 * SB_T, SB_T))],
        out_specs=[pl.BlockSpec((SB_TQ, wide), blk), hbm, hbm],
        out_shape=[jax.ShapeDtypeStruct((T, 512), BF16), tiles, tiles],
        scratch_shapes=[pltpu.VMEM((2, SB_NP, SB_TQ, 2 * SB_T), BF16), pltpu.VMEM((2, SB_NP, SB_TQ, 2 * SB_T), BF16),
                        pltpu.SemaphoreType.DMA((2, 2))],
        compiler_params=_params(3),
    )(qb, kb, vb, ublk)


def _sb_bwd(qb, kb, vb, dob, wst, ost, pblk, nbatch, seq):
    nq, nk = seq // SB_TQ, seq // SB_T
    T = nbatch * seq

    def body(q_ref, k_ref, v_ref, do_ref, wst_ref, ost_ref, up_ref, dq_ref, dk_ref, dv_ref, wbuf, obuf, sems):
        b, i = pl.program_id(0), pl.program_id(2)

        @pl.when(i == 0)
        def _():
            dk_ref[...] = jnp.zeros_like(dk_ref)
            dv_ref[...] = jnp.zeros_like(dv_ref)

        m0b = _lane_mask(SB_T, BF16)
        m0 = _lane_mask(SB_T) > 0.5
        up = up_ref[...]
        pairs = [slice(pp * 128, (pp + 1) * 128) for pp in range(SB_NP)]

        def loads(j, slot):
            tix = (b * nq + i) * nk + j
            return (pltpu.make_async_copy(wst_ref.at[tix], wbuf.at[slot], sems.at[0, slot]),
                    pltpu.make_async_copy(ost_ref.at[tix], obuf.at[slot], sems.at[1, slot]))

        def tile(j, carries, kind, slot, fetch_next):
            rows, mask = _sb_rows(kind), _sb_mask(kind)
            if fetch_next:
                for cp in loads(j + 1, 1 - slot):
                    cp.start()
            for cp in loads(j, slot):
                cp.wait()
            out = []
            for pp, (cols, (s0, s1, dq)) in enumerate(zip(pairs, carries)):
                q, do = q_ref[rows, cols], do_ref[rows, cols]
                kst = _pair_rows(k_ref, j, cols, m0b)
                vst = _pair_rows(v_ref, j, cols, m0b)
                wb = wbuf[slot, pp, rows, :]
                e = _dot_nt(do, vst) * wb.astype(F32)
                dlb = _bcast2(s0[rows], s1[rows]) + _scan1(e, up)
                dz = (e + dlb) * jnp.exp(obuf[slot, pp, rows, :].astype(F32)) - dlb
                if mask is not None:
                    dz = jnp.where(mask, dz, 0.0)
                dzb = dz.astype(BF16)
                dkst = _dot_tn(dzb, q)
                dvst = _dot_tn(wb, do)
                keys = pl.ds(pl.multiple_of(j * SB_T, SB_T), SB_T)
                dk_ref[keys, cols] = dk_ref[keys, cols] + jnp.where(m0, dkst[0:SB_T], dkst[SB_T:2 * SB_T])
                dv_ref[keys, cols] = dv_ref[keys, cols] + jnp.where(m0, dvst[0:SB_T], dvst[SB_T:2 * SB_T])
                x0, x1 = _rowsum2(e)
                ddq = _dot(dzb, kst)
                if kind == "B":
                    x0, x1, ddq = _lower(x0), _lower(x1), _lower(ddq)
                out.append((s0 + x0, s1 + x1, dq + ddq))
            return tuple(out)

        for cp in loads(0, 0):
            cp.start()

        def two(jj, cr):
            cr = tile(2 * jj, cr, "full", 0, True)
            return tile(2 * jj + 1, cr, "full", 1, True)

        zc = jnp.zeros((SB_TQ, 1), F32)
        carries = lax.fori_loop(0, i, two, ((zc, zc, jnp.zeros((SB_TQ, 128), F32)),) * SB_NP)
        carries = tile(2 * i, carries, "A", 0, True)
        carries = tile(2 * i + 1, carries, "B", 1, False)
        for cols, carry in zip(pairs, carries):
            dq_ref[:, cols] = carry[2] * SCALE

    wide = 128 * SB_NP
    blk = lambda b, g, i: (b * nq + i, g)
    full = lambda b, g, i: (b, g)
    hbm = pl.BlockSpec(memory_space=pl.ANY)
    return pl.pallas_call(
        body, name="sb_bwd", grid=(nbatch, 4 // SB_NP, nq),
        in_specs=[pl.BlockSpec((SB_TQ, wide), blk), pl.BlockSpec((seq, wide), full), pl.BlockSpec((seq, wide), full),
                  pl.BlockSpec((SB_TQ, wide), blk), hbm, hbm, _const_spec((SB_T, SB_T))],
        out_specs=[pl.BlockSpec((SB_TQ, wide), blk), pl.BlockSpec((seq, wide), full), pl.BlockSpec((seq, wide), full)],
        out_shape=[jax.ShapeDtypeStruct((T, 512), F32)] * 3,
        scratch_shapes=[pltpu.VMEM((2, SB_NP, SB_TQ, 2 * SB_T), BF16), pltpu.VMEM((2, SB_NP, SB_TQ, 2 * SB_T), BF16),
                        pltpu.SemaphoreType.DMA((2, 2))],
        compiler_params=_params(3),
    )(qb, kb, vb, dob, wst, ost, pblk)


def _sigmoid(g):
    return 1.0 / (1.0 + jnp.exp(-g))


def _out_proj(oa, ob, ga, gb, x2, tgt, wout_bf):
    T = x2.shape[0]

    def body(oa_ref, ob_ref, ga_ref, gb_ref, x_ref, t_ref, w_ref,
             dout_ref, doa_ref, dob_ref, dga_ref, dgb_ref, dw_ref, loss_ref):
        @pl.when(pl.program_id(0) == 0)
        def _():
            loss_ref[...] = jnp.zeros_like(loss_ref)
            dw_ref[...] = jnp.zeros_like(dw_ref)

        halves = ((oa_ref, ga_ref, doa_ref, dga_ref, 0), (ob_ref, gb_ref, dob_ref, dgb_ref, 512))
        out = x_ref[...]
        gated = []
        for o_ref, g_ref, _, _, lo in halves:
            g = g_ref[...]
            sg = _sigmoid(g)
            y = (o_ref[...] * (g * sg)).astype(BF16)
            out = out + _dot(y, w_ref[lo:lo + 512, :])
            gated.append((y, g, sg))
        diff = out - t_ref[...]
        dout = diff * (1.0 / D_MODEL)
        dout_ref[...] = dout
        loss_ref[...] = loss_ref[...] + jnp.sum(diff * diff) * (0.5 / D_MODEL)
        db = dout.astype(BF16)
        for (o_ref, _, do_ref, dg_ref, lo), (y, g, sg) in zip(halves, gated):
            dw_ref[lo:lo + 512, :] = dw_ref[lo:lo + 512, :] + _dot_tn(y, db)
            dy = _dot_nt(db, w_ref[lo:lo + 512, :])
            do_ref[...] = (dy * (g * sg)).astype(BF16)
            dg_ref[...] = (dy * o_ref[...] * (sg * (1.0 + g * (1.0 - sg)))).astype(BF16)

    rows = lambda w: pl.BlockSpec((TM, w), lambda i: (i, 0))
    return pl.pallas_call(
        body, name="out_proj", grid=(T // TM,),
        in_specs=[rows(512), rows(512), rows(512), rows(512), rows(D_MODEL), rows(D_MODEL),
                  _const_spec((D_MODEL, D_MODEL))],
        out_specs=[rows(D_MODEL)] + [rows(512)] * 4 + [_const_spec((D_MODEL, D_MODEL)), _const_spec((8, 128))],
        out_shape=[jax.ShapeDtypeStruct((T, D_MODEL), F32)] + [jax.ShapeDtypeStruct((T, 512), BF16)] * 4
        + [jax.ShapeDtypeStruct((D_MODEL, D_MODEL), F32), jax.ShapeDtypeStruct((8, 128), F32)],
        compiler_params=_params(1),
    )(oa, ob, ga, gb, x2, tgt, wout_bf)


def _qk_grad(qraw, kraw, dqrot, dkpad, dvpad, qg512, kg128, g512, cos, sa, sb, seq):
    T = qraw.shape[0]
    assert TM == SWA_Q
    ni = seq // SWA_Q

    def body(qraw_ref, kraw_ref, dqrot_ref, dk0, dk1, dk2, dk3, dv0, dv1, dv2, dv3,
             qg_ref, kg_ref, g_ref, cos_ref, sa_ref, sb_ref, dqa_ref, dkv_ref, dqg_ref, dkg_ref):
        @pl.when(pl.program_id(0) == 0)
        def _():
            dqg_ref[...] = jnp.zeros_like(dqg_ref)
            dkg_ref[...] = jnp.zeros_like(dkg_ref)

        cosv, sav, sbv = cos_ref[...], sa_ref[...], sb_ref[...]
        m0 = _lane_mask(TM) > 0.5

        def head_norm_bwd(raw, dn_rot, gmat, gain):
            r = lax.rsqrt(_dot_b(raw * raw, gmat) + EPS)
            n = raw * r
            dn = dn_rot * gain
            return r * (dn - n * _dot_b(dn * n, gmat)), jnp.sum(dn_rot * n, axis=0, keepdims=True)

        def fold(refs):
            a = jnp.concatenate([r[:, 0:128] for r in refs], axis=0)
            b = jnp.concatenate([r[:, 128:256] for r in refs], axis=0)
            return jnp.where(m0, a + pltpu.roll(a, 64, 1), b + pltpu.roll(b, 64, 1))

        dqn = jnp.concatenate([_rope_bwd(dqrot_ref[:, s * 128:(s + 1) * 128], cosv, sav, sbv) for s in range(4)], axis=1)
        dqa, dqg = head_norm_bwd(qraw_ref[...], dqn, g_ref[...], qg_ref[...])
        dka, dkg = head_norm_bwd(kraw_ref[...], _rope_bwd(fold((dk0, dk1, dk2, dk3)), cosv, sav, sbv),
                                 g_ref[0:128, 0:128], kg_ref[...])
        dqa_ref[...] = dqa.astype(BF16)
        dkv_ref[:, 0:128] = dka.astype(BF16)
        dkv_ref[:, 128:256] = fold((dv0, dv1, dv2, dv3)).astype(BF16)
        dqg_ref[...] = dqg_ref[...] + dqg
        dkg_ref[...] = dkg_ref[...] + dkg

    rows = lambda w: pl.BlockSpec((TM, w), lambda i: (i, 0))
    sub = SWA_Q // BLOCK
    shifted = [pl.BlockSpec((BLOCK, 256), functools.partial(
        lambda i, j: ((i // ni) * (ni + 1) * sub + (i % ni) * sub + 1 + j, 0), j=j)) for j in range(sub)]
    return pl.pallas_call(
        body, name="qk_grad", grid=(T // TM,),
        in_specs=[rows(512), rows(128), rows(512)] + shifted + shifted + [
            _const_spec((1, 512)), _const_spec((1, 128)), _const_spec((512, 512)), rows(128), rows(128), rows(128)],
        out_specs=[rows(512), rows(256), _const_spec((1, 512)), _const_spec((1, 128))],
        out_shape=[jax.ShapeDtypeStruct((T, 512), BF16), jax.ShapeDtypeStruct((T, 256), BF16),
                   jax.ShapeDtypeStruct((1, 512), F32), jax.ShapeDtypeStruct((1, 128), F32)],
        compiler_params=_params(1),
    )(qraw, kraw, dqrot, *([dkpad] * sub), *([dvpad] * sub), qg512, kg128, g512, cos, sa, sb)


_PIECES = ((QA, 512), (KA, 256), (GA, 512), (QB, 512), (KB, 512), (VB, 512), (GB, 512))


def _w_in_grad(h, pieces):
    T = h.shape[0]

    def body(h_ref, *refs):
        dw_ref = refs[-1]

        @pl.when(pl.program_id(0) == 0)
        def _():
            dw_ref[...] = jnp.zeros_like(dw_ref)

        hb = h_ref[...]
        for (lo, width), p_ref in zip(_PIECES, refs[:-1]):
            dw_ref[lo:lo + width, :] = dw_ref[lo:lo + width, :] + _dot_tn(p_ref[...].astype(BF16), hb)

    rows = lambda w: pl.BlockSpec((TM, w), lambda i: (i, 0))
    return pl.pallas_call(
        body, name="w_in_grad", grid=(T // TM,),
        in_specs=[rows(D_MODEL)] + [rows(w) for _, w in _PIECES],
        out_specs=_const_spec((IN_WIDTH, D_MODEL)),
        out_shape=jax.ShapeDtypeStruct((IN_WIDTH, D_MODEL), F32),
        compiler_params=_params(1),
    )(h, *pieces)


def _chip_exchange(src_ref, dst_ref, sems):
    _, _, c, chips = _place()
    return [_remote(src_ref.at[2 * cx + cy], dst_ref.at[j], sems, j, (cx, cy, c)) for j, (cx, cy) in enumerate(chips)]


def _x_grad(x2, dout, pieces, w_bf, ng, cin4):
    T = x2.shape[0]
    npc = len(_PIECES)
    steps = T // TM

    def body(x_ref, dout_ref, *refs):
        w_ref, ng_ref, cin_ref, gx_ref, dng_ref, rin_ref, ssem, rsem = refs[npc:]
        step = pl.program_id(0)
        copies = lambda: _chip_exchange(cin_ref, rin_ref, (ssem, rsem))

        @pl.when(step == 0)
        def _():
            dng_ref[...] = jnp.zeros_like(dng_ref)
            for cp in copies():
                cp.start()

        dh = jnp.zeros((TM, D_MODEL), F32)
        for (lo, width), p_ref in zip(_PIECES, refs[:npc]):
            dh = dh + _dot(p_ref[...].astype(BF16), w_ref[lo:lo + width, :])
        xb = x_ref[...]
        r = lax.rsqrt(jnp.mean(xb * xb, axis=-1, keepdims=True) + EPS)
        n = xb * r
        dn = dh * ng_ref[...]
        gx_ref[...] = dout_ref[...] + r * (dn - n * jnp.mean(dn * n, axis=-1, keepdims=True))
        dng_ref[...] = dng_ref[...] + jnp.sum(dh * n, axis=0, keepdims=True)

        @pl.when(step == steps - 1)
        def _():
            for cp in copies():
                cp.wait_recv()
            for cp in copies():
                cp.wait_send()

    rows = lambda w: pl.BlockSpec((TM, w), lambda i: (i, 0))
    hbm = pl.BlockSpec(memory_space=pl.ANY)
    return pl.pallas_call(
        body, name="x_grad", grid=(steps,),
        in_specs=[rows(D_MODEL), rows(D_MODEL)] + [rows(w) for _, w in _PIECES]
        + [_const_spec((IN_WIDTH, D_MODEL)), _const_spec((1, D_MODEL)), hbm],
        out_specs=[rows(D_MODEL), _const_spec((1, D_MODEL)), hbm],
        out_shape=[jax.ShapeDtypeStruct((T, D_MODEL), F32), jax.ShapeDtypeStruct((1, D_MODEL), F32),
                   jax.ShapeDtypeStruct((3,) + cin4.shape[1:], cin4.dtype)],
        scratch_shapes=[pltpu.SemaphoreType.DMA((3,)), pltpu.SemaphoreType.DMA((3,))],
        compiler_params=_params(1),
    )(x2, dout, *pieces, w_bf, ng, cin4)


HBM = pl.BlockSpec(memory_space=pl.ANY)


def _place():
    x, y, c = lax.axis_index("x"), lax.axis_index("y"), lax.axis_index("c")
    chips = [(1 - x, y), (x, 1 - y), (1 - x, 1 - y)]
    return x, y, c, chips


def _remote(src, dst, sems, k, to):
    return pltpu.make_async_remote_copy(src_ref=src, dst_ref=dst, send_sem=sems[0].at[k], recv_sem=sems[1].at[k],
                                        device_id=to, device_id_type=MESH)


def _gather_plan(src, dst, sems, relay):
    x, y, c, _ = _place()
    rows = src.shape[1] // 2
    parts = (pl.ds(0, rows), pl.ds(rows, rows))
    me, kx, ky, kd = 2 * x + y, 2 * (1 - x) + y, 2 * x + 1 - y, 2 * (1 - x) + 1 - y
    to_x, to_y, to_d, sib = (1 - x, y, c), (x, 1 - y, c), (1 - x, 1 - y, c), (x, y, 1 - c)
    mine = [(src.at[c, p], dst.at[me, c, p]) for p in parts]
    direct = [_remote(*mine[0], sems, 0, to_x), _remote(*mine[1], sems, 1, to_x),
              _remote(*mine[1], sems, 2, to_y), _remote(*mine[0], sems, 3, to_y)]
    arrived = [dst.at[kx, c, parts[0]], dst.at[kx, c, parts[1]], dst.at[ky, c, parts[1]], dst.at[ky, c, parts[0]]]
    if relay:
        relays = [_remote(arrived[0], arrived[0], sems, 4, to_y), _remote(arrived[2], arrived[2], sems, 5, to_x)]
    else:
        relays = []
        direct += [_remote(*mine[0], sems, 4, to_d), _remote(*mine[1], sems, 5, to_d)]
    relayed = [dst.at[kd, c, parts[0]], dst.at[kd, c, parts[1]]]
    forwards = [_remote(dst.at[k, c], dst.at[k, c], sems, 6 + n, sib) for n, k in enumerate((kx, ky, kd))]
    from_sib = [dst.at[k, 1 - c] for k in (kx, ky, kd)]
    return direct, arrived, relays, relayed, forwards, from_sib


def _gather_start(src, dst, sems, relay):
    direct = _gather_plan(src, dst, sems, relay)[0]
    for k in (0, 2, 1, 3):
        direct[k].start()
    for cp in direct[4:]:
        cp.start()


def _gather_finish(src, dst, sems, relay):
    direct, arrived, relays, relayed, forwards, from_sib = _gather_plan(src, dst, sems, relay)
    landed = lambda ref, k: _remote(ref, ref, sems, k, (0, 0, 0)).wait_recv()
    landed(arrived[0], 0)
    if relay:
        relays[0].start()
    landed(arrived[2], 2)
    if relay:
        relays[1].start()
    landed(arrived[1], 1)
    forwards[0].start()
    landed(arrived[3], 3)
    forwards[1].start()
    landed(relayed[0], 4)
    landed(relayed[1], 5)
    forwards[2].start()
    for n, ref in enumerate(from_sib):
        landed(ref, 6 + n)
    for cp in direct + relays + forwards:
        cp.wait_send()


def _own_slot(gathered, shard):
    me = 2 * lax.axis_index("x") + lax.axis_index("y")
    return lax.dynamic_update_slice(gathered, shard[None], (me, 0, 0, 0))


def _pair_exchange(g4):
    def body(src, dst, ssem, rsem):
        x, y, c, _ = _place()
        cp = _remote(src.at[:, pl.ds(1 - c, 1)], dst, (ssem, rsem), 0, (x, y, 1 - c))
        cp.start()
        cp.wait()

    return pl.pallas_call(
        body, name=f"pair_exchange_{g4.shape[2]}", in_specs=[HBM], out_specs=HBM,
        out_shape=jax.ShapeDtypeStruct((4, 1) + g4.shape[2:], g4.dtype),
        scratch_shapes=[pltpu.SemaphoreType.DMA((1,)), pltpu.SemaphoreType.DMA((1,))],
    )(g4)


def _share_halves(hin, hout, small):
    def body(hin_ref, hout_ref, small_ref, oin_ref, oout_ref, sall_ref, ssem, rsem, lsem):
        x, y, c, _ = _place()
        sems = (ssem, rsem)
        sib = (x, y, 1 - c)
        me = 4 * x + 2 * y + c
        own = pltpu.make_async_copy(small_ref, sall_ref.at[me], lsem.at[0])
        own.start()
        sent = [_remote(hin_ref, oin_ref, sems, 0, sib), _remote(hout_ref, oout_ref, sems, 1, sib)]
        flips = [(fx, fy, fc) for fx in (0, 1) for fy in (0, 1) for fc in (0, 1)][1:]
        for k, (fx, fy, fc) in enumerate(flips):
            sent.append(_remote(small_ref, sall_ref.at[me], sems, 2 + k, (x ^ fx, y ^ fy, c ^ fc)))
        for cp in sent:
            cp.start()
        _remote(hin_ref, oin_ref, sems, 0, sib).wait_recv()
        _remote(hout_ref, oout_ref, sems, 1, sib).wait_recv()
        for k, (fx, fy, fc) in enumerate(flips):
            src = 4 * (x ^ fx) + 2 * (y ^ fy) + (c ^ fc)
            _remote(small_ref, sall_ref.at[src], sems, 2 + k, sib).wait_recv()
        for cp in sent:
            cp.wait_send()
        own.wait()

    return pl.pallas_call(
        body, name="share_halves", in_specs=[HBM, HBM, HBM], out_specs=[HBM, HBM, HBM],
        out_shape=[jax.ShapeDtypeStruct(hin.shape, F32), jax.ShapeDtypeStruct(hout.shape, F32),
                   jax.ShapeDtypeStruct((8,) + small.shape, F32)],
        scratch_shapes=[pltpu.SemaphoreType.DMA((9,)), pltpu.SemaphoreType.DMA((9,)), pltpu.SemaphoreType.DMA((1,))],
    )(hin, hout, small)


def _add_half(cidx, full4, recv4):
    _, _, rows, width = full4.shape

    def body(c_ref, a_ref, b_ref, o_ref):
        o_ref[0] = (a_ref[0, 0] + b_ref[0, 0]).astype(BF16)

    return pl.pallas_call(
        body, name=f"add_half_{rows}",
        grid_spec=pltpu.PrefetchScalarGridSpec(
            num_scalar_prefetch=1, grid=(4,),
            in_specs=[pl.BlockSpec((1, 1, rows, width), lambda k, c: (k, c[0], 0, 0)),
                      pl.BlockSpec((1, 1, rows, width), lambda k, c: (k, 0, 0, 0))],
            out_specs=pl.BlockSpec((1, rows, width), lambda k, c: (k, 0, 0))),
        out_shape=jax.ShapeDtypeStruct((4, rows, width), BF16),
        compiler_params=_params(1),
    )(cidx, full4, recv4)


def _sum_chips(chip, own4, recv3):
    _, rows, width = recv3.shape
    rb = rows // 2

    def body(k_ref, a_ref, r_ref, o_ref):
        acc = a_ref[0].astype(F32)
        for s in range(3):
            acc = acc + r_ref[s].astype(F32)
        o_ref[...] = acc

    return pl.pallas_call(
        body, name=f"sum_chips_{rows}",
        grid_spec=pltpu.PrefetchScalarGridSpec(
            num_scalar_prefetch=1, grid=(rows // rb,),
            in_specs=[pl.BlockSpec((1, rb, width), lambda i, k: (k[0], i, 0)),
                      pl.BlockSpec((3, rb, width), lambda i, k: (0, i, 0))],
            out_specs=pl.BlockSpec((rb, width), lambda i, k: (i, 0))),
        out_shape=jax.ShapeDtypeStruct((rows, width), F32),
        compiler_params=_params(1),
    )(chip, own4, recv3)


def _adam_math(w, g, m, v):
    c1 = 1.0 - ADAM_B1 ** ADAM_STEP
    c2 = 1.0 - ADAM_B2 ** ADAM_STEP
    nm = ADAM_B1 * m + (1.0 - ADAM_B1) * g
    nv = ADAM_B2 * v + (1.0 - ADAM_B2) * (g * g)
    return -ADAM_LR * ((nm / c1) / (jnp.sqrt(nv / c2) + ADAM_EPS) + ADAM_WD * w), nm, nv


def _small_update(small_all, params):
    n = len(params)
    flat = [a for p in params for a in p]

    def body(s_ref, *refs):
        ins, loss_ref, outs = refs[:3 * n], refs[3 * n], refs[3 * n + 1:]
        total = s_ref[0]
        for d in range(1, 8):
            total = total + s_ref[d]
        loss_ref[...] = total[n:n + 1, 0:1]
        for r in range(n):
            w_ref, m_ref, v_ref = ins[3 * r:3 * r + 3]
            g = total[r:r + 1, 0:w_ref.shape[1]]
            outs[4 * r][...] = g
            outs[4 * r + 1][...], outs[4 * r + 2][...], outs[4 * r + 3][...] = _adam_math(w_ref[...], g, m_ref[...], v_ref[...])

    whole = lambda a: pl.BlockSpec(a.shape, lambda i: (0,) * a.ndim)
    out_shape = [jax.ShapeDtypeStruct((1, 1), F32)] + [jax.ShapeDtypeStruct(p[0].shape, F32) for p in params for _ in range(4)]
    res = pl.pallas_call(
        body, name="small_update", grid=(1,),
        in_specs=[whole(small_all)] + [whole(a) for a in flat], out_specs=[whole(s) for s in out_shape],
        out_shape=out_shape, compiler_params=_params(1),
    )(small_all, *flat)
    return res[0], [tuple(res[1 + 4 * r:5 + 4 * r]) for r in range(n)]


def _adamw_halves(cidx, w, own, recv, m, v):
    rows, width = w.shape
    rb = rows // 4

    def body(c_ref, w_ref, own_ref, recv_ref, m_ref, v_ref, g_ref, d_ref, nm_ref, nv_ref):
        mine = (pl.program_id(0) // 2) == c_ref[0]
        g = jnp.where(mine, own_ref[...], recv_ref[...])
        g_ref[...] = g
        d_ref[...], nm_ref[...], nv_ref[...] = _adam_math(w_ref[...], g, m_ref[...], v_ref[...])

    full = pl.BlockSpec((rb, width), lambda i, c: (i, 0))
    half = pl.BlockSpec((rb, width), lambda i, c: (i % 2, 0))
    return pl.pallas_call(
        body, name=f"adamw_halves_{rows}",
        grid_spec=pltpu.PrefetchScalarGridSpec(
            num_scalar_prefetch=1, grid=(4,),
            in_specs=[full, half, half, full, full], out_specs=[full] * 4),
        out_shape=[jax.ShapeDtypeStruct((rows, width), F32)] * 4,
        compiler_params=_params(1),
    )(cidx, w, own, recv, m, v)


def _constants():
    idx = jnp.arange(512)
    g512 = jnp.where(idx[:, None] // HEAD_DIM == idx[None, :] // HEAD_DIM, 1.0 / HEAD_DIM, 0.0).astype(BF16)
    j = jnp.arange(SB_T)
    ublk = jnp.where(j[:, None] >= j[None, :], 1.0, 0.0).astype(BF16)
    pblk = jnp.where(j[:, None] < j[None, :], 1.0, 0.0).astype(BF16)
    return g512, jnp.concatenate([ublk, ublk], axis=0), pblk


def _pad_rows(v, width):
    return jnp.pad(v, ((0, 0), (0, width - v.shape[1])))


def _pair_sum(cidx, partial, rows):
    g4 = partial.reshape(4, 2, rows, D_MODEL)
    return _add_half(cidx, g4, _pair_exchange(g4))


def _step(x2, tgt, positions, norm_gain, q_norm_gain, k_norm_gain, sinks, win_shard, wout_shard, nbatch, seq):
    g512, ublk, pblk = _constants()
    qg512 = jnp.tile(q_norm_gain, (1, 8))
    kg128 = jnp.tile(k_norm_gain, (1, 2))
    sink1 = sinks.reshape(8)
    cidx = lax.axis_index("c").astype(jnp.int32).reshape(1)
    chip = (2 * lax.axis_index("x") + lax.axis_index("y")).astype(jnp.int32).reshape(1)

    h, (cos, sa, sb), w_bf = _prologue(x2, norm_gain, positions.reshape(-1, 1), win_shard)
    (qraw, kraw, qrot, k2, v2, ga, qb, kb, vb, gb), wout_bf = _norm_proj(
        h, w_bf, qg512, kg128, g512, cos, sa, sb, wout_shard)
    oa = _swa_fwd(sink1, qrot, k2, v2, nbatch, seq)
    ob, wst, ost = _sb_fwd(qb, kb, vb, ublk, nbatch, seq)
    dout, doa, dob, dga, dgb, dwout, loss_acc = _out_proj(oa, ob, ga, gb, x2, tgt, wout_bf)

    cout4 = _pair_sum(cidx, dwout, SHARD_OUT // 2)
    dqrot, dk2, dv2, dsink, rout3 = _swa_bwd(sink1, qrot, k2, v2, doa, cout4, nbatch, seq)
    dqb, dkb, dvb = _sb_bwd(qb, kb, vb, dob, wst, ost, pblk, nbatch, seq)
    dqa, dkv, dqg, dkg = _qk_grad(qraw, kraw, dqrot, dk2, dv2, qg512, kg128, g512, cos, sa, sb, seq)
    pieces = (dqa, dkv, dga, dqb, dkb, dvb, dgb)
    cin4 = _pair_sum(cidx, _w_in_grad(h, pieces), SHARD_IN // 2)
    gx, dng, rin3 = _x_grad(x2, dout, pieces, w_bf, norm_gain, cin4)
    own_in, own_out = _sum_chips(chip, cin4, rin3), _sum_chips(chip, cout4, rout3)

    dqg64 = dqg.reshape(8, HEAD_DIM).sum(axis=0, keepdims=True)
    dkg64 = dkg.reshape(2, HEAD_DIM).sum(axis=0, keepdims=True)
    small = jnp.concatenate([dng, _pad_rows(dqg64, D_MODEL), _pad_rows(dkg64, D_MODEL),
                             _pad_rows(dsink[:, 0].reshape(1, 8), D_MODEL), _pad_rows(loss_acc[0:1, 0:1], D_MODEL),
                             jnp.zeros((3, D_MODEL), F32)], axis=0)
    sib_in, sib_out, small_all = _share_halves(own_in, own_out, small)
    return gx, cidx, (own_in, sib_in), (own_out, sib_out), small_all


def kernel(x, positions, norm_gain, w_in, q_norm_gain, k_norm_gain, sinks, w_out, loss_target, m_norm_gain, m_w_in, m_q_norm_gain, m_k_norm_gain, m_sinks, m_w_out, v_norm_gain, v_w_in, v_q_norm_gain, v_k_norm_gain, v_sinks, v_w_out):
    nbatch, seq, _ = x.shape
    T = nbatch * seq
    x2 = x.reshape(T, D_MODEL)
    tgt = loss_target.reshape(T, D_MODEL)
    tr = lambda a: jnp.swapaxes(a[0], 0, 1)
    win_t, m_win_t, v_win_t = tr(w_in), tr(m_w_in), tr(v_w_in)

    win_shard = win_t.astype(BF16).reshape(2, SHARD_IN // 2, D_MODEL)
    wout_shard = w_out[0].astype(BF16).reshape(2, SHARD_OUT // 2, D_MODEL)
    gx, cidx, g_in_halves, g_out_halves, small_all = _step(
        x2, tgt, positions, norm_gain, q_norm_gain, k_norm_gain, sinks, win_shard, wout_shard, nbatch, seq)

    g_in, d_in, nm_in, nv_in = [jnp.swapaxes(a, 0, 1) for a in
                                _adamw_halves(cidx, win_t, *g_in_halves, m_win_t, v_win_t)]
    g_out, d_out, nm_out, nv_out = _adamw_halves(cidx, w_out[0], *g_out_halves, m_w_out[0], v_w_out[0])
    loss, small = _small_update(small_all, [(norm_gain, m_norm_gain, v_norm_gain), (q_norm_gain, m_q_norm_gain, v_q_norm_gain),
                                            (k_norm_gain, m_k_norm_gain, v_k_norm_gain), (sinks, m_sinks, v_sinks)])
    (g_ng, d_ng, m_ng, v_ng), (g_qg, d_qg, m_qg, v_qg), (g_kg, d_kg, m_kg, v_kg), (g_sk, d_sk, m_sk, v_sk) = small
    return (loss.reshape(()), gx.reshape(nbatch, seq, D_MODEL),
            g_ng, g_in[None], g_qg, g_kg, g_sk, g_out[None],
            d_ng, d_in[None], d_qg, d_kg, d_sk, d_out[None],
            m_ng, nm_in[None], m_qg, m_kg, m_sk, nm_out[None],
            v_ng, nv_in[None], v_qg, v_kg, v_sk, nv_out[None])
```

```python
import functools
import math

import jax
import jax.numpy as jnp
from jax import lax
from jax.experimental import pallas as pl
from jax.experimental.pallas import tpu as pltpu

F32 = jnp.float32
BF16 = jnp.bfloat16

D_MODEL = 1024
HEAD_DIM = 64
BLOCK = 128
ROPE_THETA = 10000.0
EPS = 1e-6
QA, KA, VA, GA, QB, KB, VB, GB = 0, 512, 640, 768, 1280, 1792, 2304, 2816
IN_WIDTH = 3328
SHARD_IN = IN_WIDTH // 4
SHARD_OUT = D_MODEL // 4
SCALE = 1.0 / math.sqrt(HEAD_DIM)
NEG = -1e30

ADAM_LR, ADAM_B1, ADAM_B2, ADAM_EPS, ADAM_WD, ADAM_STEP = 0.001, 0.9, 0.999, 1e-08, 0.01, 10

TM = 512
VMEM_LIMIT = 56 * 1024 * 1024
MESH = pl.DeviceIdType.MESH


def _dot(a, b):
    return jnp.dot(a, b, preferred_element_type=F32)


def _dot_nt(a, b):
    return lax.dot_general(a, b, (((1,), (1,)), ((), ())), preferred_element_type=F32)


def _dot_tn(a, b):
    return lax.dot_general(a, b, (((0,), (0,)), ((), ())), preferred_element_type=F32)


def _dot_b(a, m):
    return _dot(a.astype(BF16), m)


def _params(n_axes=None, vmem=VMEM_LIMIT):
    sem = None if n_axes is None else ("arbitrary",) * n_axes
    return pltpu.CompilerParams(dimension_semantics=sem, vmem_limit_bytes=vmem)


def _const_spec(shape):
    nd = len(shape)
    return pl.BlockSpec(shape, lambda *_: (0,) * nd)


def _rope_fwd(x, cos, sa, sb):
    return x * cos + pltpu.roll(x, 96, 1) * sa + pltpu.roll(x, 32, 1) * sb


def _rope_bwd(d, cos, sa, sb):
    return d * cos - pltpu.roll(d, 96, 1) * sa - pltpu.roll(d, 32, 1) * sb


def _lane_mask(rows, dtype=F32):
    lane = lax.broadcasted_iota(jnp.int32, (rows, 128), 1)
    return jnp.where(lane < HEAD_DIM, 1.0, 0.0).astype(dtype)


def _prologue(x2, ng, pos, win_shard):
    T = x2.shape[0]
    steps = T // TM
    half = HEAD_DIM // 2
    inv_freq = jnp.tile(ROPE_THETA ** (-jnp.arange(half, dtype=F32) * 2.0 / HEAD_DIM), 4).reshape(1, 128)

    def body(x_ref, ng_ref, pos_ref, freq_ref, wi_ref, h_ref, cos_ref, sa_ref, sb_ref, ai_ref, ssem, rsem):
        @pl.when(pl.program_id(0) == 0)
        def _():
            _gather_start(wi_ref, ai_ref, (ssem, rsem), relay=True)

        xb = x_ref[...]
        r = lax.rsqrt(jnp.mean(xb * xb, axis=-1, keepdims=True) + EPS)
        h_ref[...] = (xb * r * ng_ref[...]).astype(BF16)
        ang = pos_ref[...].astype(F32) * freq_ref[...]
        first = (lax.broadcasted_iota(jnp.int32, (TM, 128), 1) & (HEAD_DIM - 1)) < half
        sn = jnp.sin(ang)
        cos_ref[...] = jnp.cos(ang)
        sa_ref[...] = jnp.where(first, -sn, 0.0)
        sb_ref[...] = jnp.where(first, 0.0, sn)

        @pl.when(pl.program_id(0) == steps - 1)
        def _():
            _gather_finish(wi_ref, ai_ref, (ssem, rsem), relay=True)

    rows = lambda w: pl.BlockSpec((TM, w), lambda i: (i, 0))
    hbm = pl.BlockSpec(memory_space=pl.ANY)
    h, cos, sa, sb, gathered = pl.pallas_call(
        body, name="prologue", grid=(steps,),
        in_specs=[rows(D_MODEL), _const_spec((1, D_MODEL)), rows(1), _const_spec((1, 128)), hbm],
        out_specs=[rows(D_MODEL), rows(128), rows(128), rows(128), hbm],
        out_shape=[jax.ShapeDtypeStruct((T, D_MODEL), BF16)] + [jax.ShapeDtypeStruct((T, 128), F32)] * 3
        + [jax.ShapeDtypeStruct((4,) + win_shard.shape, win_shard.dtype)],
        scratch_shapes=[pltpu.SemaphoreType.DMA((9,)), pltpu.SemaphoreType.DMA((9,))],
        compiler_params=_params(1),
    )(x2, ng, pos, inv_freq, win_shard)
    return h, (cos, sa, sb), _own_slot(gathered, win_shard).reshape(IN_WIDTH, D_MODEL)


def _norm_proj(h, w_bf, qg512, kg128, g512, cos, sa, sb, wout_shard):
    T = h.shape[0]
    steps = T // TM

    def body(h_ref, w_ref, qg_ref, kg_ref, g_ref, cos_ref, sa_ref, sb_ref, wo_ref,
             qraw_ref, kraw_ref, qrot_ref, k2_ref, v2_ref, ga_ref, qb_ref, kb_ref, vb_ref, gb_ref, ao_ref,
             ssem, rsem):
        @pl.when(pl.program_id(0) == 0)
        def _():
            _gather_start(wo_ref, ao_ref, (ssem, rsem), relay=False)

        h = h_ref[...]
        cosv, sav, sbv = cos_ref[...], sa_ref[...], sb_ref[...]
        m0 = _lane_mask(TM)

        def dup(v):
            v0 = v * m0
            v1 = v - v0
            return v0 + pltpu.roll(v0, 64, 1), v1 + pltpu.roll(v1, 64, 1)

        qa = _dot_nt(h, w_ref[QA:KA, :])
        qraw_ref[...] = qa
        qn = qa * lax.rsqrt(_dot_b(qa * qa, g_ref[...]) + EPS) * qg_ref[...]
        for s in range(4):
            qs = _rope_fwd(qn[:, s * 128:(s + 1) * 128], cosv, sav, sbv)
            qrot_ref[:, s * 128:(s + 1) * 128] = (qs * SCALE).astype(BF16)
        ka = _dot_nt(h, w_ref[KA:VA, :])
        kraw_ref[...] = ka
        kn = ka * lax.rsqrt(_dot_b(ka * ka, g_ref[0:128, 0:128]) + EPS) * kg_ref[...]
        k0, k1 = dup(_rope_fwd(kn, cosv, sav, sbv))
        k2_ref[:, 0:128] = k0.astype(BF16)
        k2_ref[:, 128:256] = k1.astype(BF16)
        v0, v1 = dup(_dot_nt(h, w_ref[VA:GA, :]))
        v2_ref[:, 0:128] = v0.astype(BF16)
        v2_ref[:, 128:256] = v1.astype(BF16)
        ga_ref[...] = _dot_nt(h, w_ref[GA:QB, :])
        qb_ref[...] = (_dot_nt(h, w_ref[QB:KB, :]) * SCALE).astype(BF16)
        kb_ref[...] = _dot_nt(h, w_ref[KB:VB, :]).astype(BF16)
        vb_ref[...] = _dot_nt(h, w_ref[VB:GB, :]).astype(BF16)
        gb_ref[...] = _dot_nt(h, w_ref[GB:IN_WIDTH, :])

        @pl.when(pl.program_id(0) == steps - 1)
        def _():
            _gather_finish(wo_ref, ao_ref, (ssem, rsem), relay=False)

    def rows(w):
        return pl.BlockSpec((TM, w), lambda i: (i, 0))

    outs = [(512, F32), (128, F32), (512, BF16), (256, BF16), (256, BF16), (512, F32),
            (512, BF16), (512, BF16), (512, BF16), (512, F32)]
    hbm = pl.BlockSpec(memory_space=pl.ANY)
    res = pl.pallas_call(
        body, name="norm_proj", grid=(steps,),
        in_specs=[rows(D_MODEL), _const_spec((IN_WIDTH, D_MODEL)), _const_spec((1, 512)),
                  _const_spec((1, 128)), _const_spec((512, 512)), rows(128), rows(128), rows(128), hbm],
        out_specs=[rows(w) for w, _ in outs] + [hbm],
        out_shape=[jax.ShapeDtypeStruct((T, w), dt) for w, dt in outs]
        + [jax.ShapeDtypeStruct((4,) + wout_shard.shape, wout_shard.dtype)],
        scratch_shapes=[pltpu.SemaphoreType.DMA((9,)), pltpu.SemaphoreType.DMA((9,))],
        compiler_params=_params(1),
    )(h, w_bf, qg512, kg128, g512, cos, sa, sb, wout_shard)
    return res[:-1], _own_slot(res[-1], wout_shard).reshape(D_MODEL, D_MODEL)


SWA_Q = 512
SWA_SUB = SWA_Q // BLOCK


def _swa_scores(q, kst, mask, sink_ref, kv):
    s_all = _dot_nt(q, kst)
    first = lax.broadcasted_iota(jnp.int32, (256, 1), 0) < 128
    probs, stats = [], []
    for hh in range(2):
        sink = jnp.where(first, sink_ref[kv * 4 + hh], sink_ref[kv * 4 + 2 + hh])
        s = jnp.where(mask, s_all[:, hh * 256:(hh + 1) * 256], NEG)
        m = jnp.maximum(jnp.max(s, axis=1, keepdims=True), sink)
        pe = jnp.exp(s - m)
        inv = 1.0 / (jnp.sum(pe, axis=1, keepdims=True) + jnp.exp(sink - m))
        probs.append(pe * inv)
        stats.append(jnp.exp(sink - m) * inv)
    return probs, stats


def _swa_mask(has_prev):
    r = lax.broadcasted_iota(jnp.int32, (256, 256), 0) & 127
    c = lax.broadcasted_iota(jnp.int32, (256, 256), 1)
    band = (c > r) & (c <= r + 128)
    return band if has_prev is True else band & ((c >= 128) | has_prev)


def _stack_pairs(ref, rows, kv):
    return jnp.concatenate([ref[rows, (2 * kv) * 128:(2 * kv + 1) * 128], ref[rows, (2 * kv + 1) * 128:(2 * kv + 2) * 128]], axis=0)


def _swa_keys(prev_ref, main_ref, s, kv, m0b):
    cols = slice(kv * 128, (kv + 1) * 128)
    prev = prev_ref[:, cols] if s == 0 else main_ref[(s - 1) * 128:s * 128, cols]
    kc = jnp.concatenate([prev, main_ref[s * 128:(s + 1) * 128, cols]], axis=0)
    k0 = kc * m0b
    return jnp.concatenate([k0, kc - k0], axis=0)


def _swa_fwd(sinks, qrot, k2, v2, nbatch, seq):
    ni = seq // SWA_Q
    T = nbatch * seq

    def body(sink_ref, q_ref, kp_ref, km_ref, vp_ref, vm_ref, o_ref):
        i = pl.program_id(1)
        m0b = _lane_mask(256, BF16)
        for s in range(SWA_SUB):
            mask = _swa_mask(True if s else i > 0)
            rows = slice(s * 128, (s + 1) * 128)
            for kv in range(2):
                kst = _swa_keys(kp_ref, km_ref, s, kv, m0b)
                vst = _swa_keys(vp_ref, vm_ref, s, kv, m0b)
                probs, _ = _swa_scores(_stack_pairs(q_ref, rows, kv), kst, mask, sink_ref, kv)
                o2 = _dot(jnp.concatenate(probs, axis=1).astype(BF16), vst)
                o_ref[rows, kv * 256:kv * 256 + 128] = o2[0:128].astype(BF16)
                o_ref[rows, kv * 256 + 128:(kv + 1) * 256] = o2[128:256].astype(BF16)

    main = lambda b, i: (b * ni + i, 0)
    prev = lambda b, i: ((b * ni + i) * SWA_SUB - jnp.where(i > 0, 1, 0), 0)
    return pl.pallas_call(
        body, name="swa_fwd", grid=(nbatch, ni),
        in_specs=[pl.BlockSpec(memory_space=pltpu.SMEM), pl.BlockSpec((SWA_Q, 512), main),
                  pl.BlockSpec((128, 256), prev), pl.BlockSpec((SWA_Q, 256), main),
                  pl.BlockSpec((128, 256), prev), pl.BlockSpec((SWA_Q, 256), main)],
        out_specs=pl.BlockSpec((SWA_Q, 512), main),
        out_shape=jax.ShapeDtypeStruct((T, 512), BF16),
        compiler_params=_params(2),
    )(sinks, qrot, k2, k2, v2, v2)


def _swa_bwd(sinks, qrot, k2, v2, doa, cout4, nbatch, seq):
    ni = seq // SWA_Q
    T = nbatch * seq

    def body(sink_ref, q_ref, kp_ref, km_ref, vp_ref, vm_ref, do_ref, cout_ref,
             dq_ref, dk_ref, dv_ref, ds_ref, rout_ref, dkc, dvc, ssem, rsem):
        b, i = pl.program_id(0), pl.program_id(1)
        copies = lambda: _chip_exchange(cout_ref, rout_ref, (ssem, rsem))

        @pl.when((b == 0) & (i == 0))
        def _():
            ds_ref[...] = jnp.zeros_like(ds_ref)
            for cp in copies():
                cp.start()

        @pl.when((b == nbatch - 1) & (i == ni))
        def _():
            for cp in copies():
                cp.wait_recv()
            for cp in copies():
                cp.wait_send()

        @pl.when(i == 0)
        def _():
            dkc[...] = jnp.zeros_like(dkc)
            dvc[...] = jnp.zeros_like(dvc)

        @pl.when(i < ni)
        def _():
            m0b = _lane_mask(256, BF16)
            m0 = _lane_mask(128) > 0.5
            for kv in range(2):
                kcols = slice(kv * 128, (kv + 1) * 128)
                dk_own, dv_own = dkc[:, kcols], dvc[:, kcols]
                for s in range(SWA_SUB):
                    mask = _swa_mask(True if s else i > 0)
                    rows = slice(s * 128, (s + 1) * 128)
                    kst = _swa_keys(kp_ref, km_ref, s, kv, m0b)
                    vst = _swa_keys(vp_ref, vm_ref, s, kv, m0b)
                    q, do = _stack_pairs(q_ref, rows, kv), _stack_pairs(do_ref, rows, kv)
                    probs, psink = _swa_scores(q, kst, mask, sink_ref, kv)
                    dp_all = _dot_nt(do, vst)
                    ds_parts = []
                    for hh in range(2):
                        dp = dp_all[:, hh * 256:(hh + 1) * 256]
                        delta = jnp.sum(probs[hh] * dp, axis=1, keepdims=True)
                        ds_parts.append(probs[hh] * (dp - delta))
                        dsink = psink[hh] * delta
                        for pr in range(2):
                            h = kv * 4 + pr * 2 + hh
                            ds_ref[h:h + 1, :] = ds_ref[h:h + 1, :] - jnp.sum(dsink[pr * 128:(pr + 1) * 128])
                    ds_all = jnp.concatenate(ds_parts, axis=1).astype(BF16)
                    p_all = jnp.concatenate(probs, axis=1).astype(BF16)
                    dq2 = _dot(ds_all, kst) * SCALE
                    dq_ref[rows, kv * 256:kv * 256 + 128] = dq2[0:128]
                    dq_ref[rows, kv * 256 + 128:(kv + 1) * 256] = dq2[128:256]
                    dkst = _dot_tn(ds_all, q)
                    dvst = _dot_tn(p_all, do)
                    dk_ref[rows, kcols] = dk_own + jnp.where(m0, dkst[0:128], dkst[256:384])
                    dv_ref[rows, kcols] = dv_own + jnp.where(m0, dvst[0:128], dvst[256:384])
                    dk_own = jnp.where(m0, dkst[128:256], dkst[384:512])
                    dv_own = jnp.where(m0, dvst[128:256], dvst[384:512])
                dkc[:, kcols] = dk_own
                dvc[:, kcols] = dv_own

        @pl.when(i == ni)
        def _():
            dk_ref[...] = jnp.zeros_like(dk_ref)
            dv_ref[...] = jnp.zeros_like(dv_ref)
            dk_ref[0:128, :] = dkc[...]
            dv_ref[0:128, :] = dvc[...]

    main = lambda b, i: (b * ni + jnp.minimum(i, ni - 1), 0)
    prev = lambda b, i: ((b * ni + jnp.minimum(i, ni - 1)) * SWA_SUB - jnp.where(jnp.minimum(i, ni - 1) > 0, 1, 0), 0)
    shifted = lambda b, i: (b * (ni + 1) + i, 0)
    tpad = nbatch * (ni + 1) * SWA_Q
    return pl.pallas_call(
        body, name="swa_bwd", grid=(nbatch, ni + 1),
        in_specs=[pl.BlockSpec(memory_space=pltpu.SMEM), pl.BlockSpec((SWA_Q, 512), main),
                  pl.BlockSpec((128, 256), prev), pl.BlockSpec((SWA_Q, 256), main),
                  pl.BlockSpec((128, 256), prev), pl.BlockSpec((SWA_Q, 256), main),
                  pl.BlockSpec((SWA_Q, 512), main), pl.BlockSpec(memory_space=pl.ANY)],
        out_specs=[pl.BlockSpec((SWA_Q, 512), main), pl.BlockSpec((SWA_Q, 256), shifted),
                   pl.BlockSpec((SWA_Q, 256), shifted), _const_spec((8, 128)), pl.BlockSpec(memory_space=pl.ANY)],
        out_shape=[jax.ShapeDtypeStruct((T, 512), F32), jax.ShapeDtypeStruct((tpad, 256), F32),
                   jax.ShapeDtypeStruct((tpad, 256), F32), jax.ShapeDtypeStruct((8, 128), F32),
                   jax.ShapeDtypeStruct((3,) + cout4.shape[1:], cout4.dtype)],
        scratch_shapes=[pltpu.VMEM((128, 256), F32), pltpu.VMEM((128, 256), F32),
                        pltpu.SemaphoreType.DMA((3,)), pltpu.SemaphoreType.DMA((3,))],
        compiler_params=_params(2),
    )(sinks, qrot, k2, k2, v2, v2, doa, cout4)


SB_T = 256
SB_TQ = 2 * SB_T


def _sb_mask(kind):
    if kind == "full":
        return None
    rows = SB_T if kind == "B" else SB_TQ
    r = lax.broadcasted_iota(jnp.int32, (rows, 2 * SB_T), 0)
    c = lax.broadcasted_iota(jnp.int32, (rows, 2 * SB_T), 1)
    causal = (c & (SB_T - 1)) < r
    return causal | (r >= SB_T) if kind == "A" else causal


def _sb_rows(kind):
    return slice(SB_T, SB_TQ) if kind == "B" else slice(0, SB_TQ)


def _lower(x):
    return jnp.concatenate([jnp.zeros_like(x), x], axis=0)


def _sb_logits(neg_q, kst):
    nz = _dot_nt(neg_q, kst)
    sign = jnp.uint32(0x80000000)
    neg_abs = lax.bitcast_convert_type(lax.bitcast_convert_type(nz, jnp.uint32) | sign, F32)
    return nz, jnp.minimum(nz, 0.0) - jnp.log(1.0 + jnp.exp(neg_abs))


SB_NP = 4


def _pair_rows(ref, j, cols, m0b):
    kj = ref[pl.ds(pl.multiple_of(j * SB_T, SB_T), SB_T), cols]
    k0 = kj * m0b
    return jnp.concatenate([k0, kj - k0], axis=0)


def _bcast2(c0, c1):
    rows = c0.shape[0]
    return jnp.concatenate([jnp.broadcast_to(c0, (rows, SB_T)), jnp.broadcast_to(c1, (rows, SB_T))], axis=1)


def _rowsum2(x):
    return jnp.sum(x[:, 0:SB_T], axis=1, keepdims=True), jnp.sum(x[:, SB_T:2 * SB_T], axis=1, keepdims=True)


def _scan2(x, tri2):
    outs = []
    for h in range(2):
        xh = x[:, h * SB_T:(h + 1) * SB_T]
        hi = xh.astype(BF16)
        lo = (xh - hi.astype(F32)).astype(BF16)
        outs.append(_dot(jnp.concatenate([hi, lo], axis=1), tri2))
    return jnp.concatenate(outs, axis=1)


def _scan1(x, tri):
    xb = x.astype(BF16)
    return jnp.concatenate([_dot(xb[:, h * SB_T:(h + 1) * SB_T], tri) for h in range(2)], axis=1)


def _sb_fwd(qb, kb, vb, ublk, nbatch, seq):
    nq, nk = seq // SB_TQ, seq // SB_T
    T = nbatch * seq

    def body(q_ref, k_ref, v_ref, u_ref, o_ref, wst_ref, ost_ref, wbuf, obuf, sems):
        b, i = pl.program_id(0), pl.program_id(2)
        m0b = _lane_mask(SB_T, BF16)
        u = u_ref[...]
        pairs = [slice(pp * 128, (pp + 1) * 128) for pp in range(SB_NP)]
        neg_qs = [-q_ref[:, cols] for cols in pairs]

        def stores(j, slot):
            tix = (b * nq + i) * nk + j
            return (pltpu.make_async_copy(wbuf.at[slot], wst_ref.at[tix], sems.at[0, slot]),
                    pltpu.make_async_copy(obuf.at[slot], ost_ref.at[tix], sems.at[1, slot]))

        def tile(j, carries, kind, slot, wait):
            rows, mask = _sb_rows(kind), _sb_mask(kind)
            if wait:
                for cp in stores(j, slot):
                    cp.wait()
            out = []
            for pp, (cols, neg_q, (c0, c1, acc)) in enumerate(zip(pairs, neg_qs, carries)):
                kst = _pair_rows(k_ref, j, cols, m0b)
                vst = _pair_rows(v_ref, j, cols, m0b)
                nz, lb = _sb_logits(neg_q[rows], kst)
                if mask is not None:
                    lb = jnp.where(mask, lb, 0.0)
                incl = _scan2(lb, u)
                w = jnp.exp(incl - nz if kind == "B" else incl + _bcast2(c0, c1) - nz)
                if mask is not None:
                    w = jnp.where(mask, w, 0.0)
                wb = w.astype(BF16)
                wbuf[slot, pp, rows, :] = wb
                obuf[slot, pp, rows, :] = lb.astype(BF16)
                if kind == "B":
                    wbuf[slot, pp, 0:SB_T, :] = jnp.zeros((SB_T, 2 * SB_T), BF16)
                    obuf[slot, pp, 0:SB_T, :] = jnp.zeros((SB_T, 2 * SB_T), BF16)
                d0, d1, da = incl[:, 0:1], incl[:, SB_T:SB_T + 1], _dot(wb, vst)
                if kind == "B":
                    d0, d1, da = _lower(d0), _lower(d1), _lower(da)
                out.append((c0 + d0, c1 + d1, acc + da))
            for cp in stores(j, slot):
                cp.start()
            return tuple(out)

        zc = jnp.zeros((SB_TQ, 1), F32)
        carries = tile(2 * i + 1, ((zc, zc, jnp.zeros((SB_TQ, 128), F32)),) * SB_NP, "B", 0, False)
        carries = tile(2 * i, carries, "A", 1, False)

        def two(jj, cr):
            cr = tile(2 * i - 1 - 2 * jj, cr, "full", 0, True)
            return tile(2 * i - 2 - 2 * jj, cr, "full", 1, True)

        carries = lax.fori_loop(0, i, two, carries)
        for cols, carry in zip(pairs, carries):
            o_ref[:, cols] = carry[2].astype(BF16)
        for slot in range(2):
            for cp in stores(0, slot):
                cp.wait()

    wide = 128 * SB_NP
    blk = lambda b, g, i: (b * nq + i, g)
    full = lambda b, g, i: (b, g)
    hbm = pl.BlockSpec(memory_space=pl.ANY)
    tiles = jax.ShapeDtypeStruct((nbatch * nq * nk, SB_NP, SB_TQ, 2 * SB_T), BF16)
    return pl.pallas_call(
        body, name="sb_fwd", grid=(nbatch, 4 // SB_NP, nq),
        in_specs=[pl.BlockSpec((SB_TQ, wide), blk), pl.BlockSpec((seq, wide), full), pl.BlockSpec((seq, wide), full),
                  _const_spec((2 * SB_T, SB_T))],
        out_specs=[pl.BlockSpec((SB_TQ, wide), blk), hbm, hbm],
        out_shape=[jax.ShapeDtypeStruct((T, 512), BF16), tiles, tiles],
        scratch_shapes=[pltpu.VMEM((2, SB_NP, SB_TQ, 2 * SB_T), BF16), pltpu.VMEM((2, SB_NP, SB_TQ, 2 * SB_T), BF16),
                        pltpu.SemaphoreType.DMA((2, 2))],
        compiler_params=_params(3),
    )(qb, kb, vb, ublk)


def _sb_bwd(qb, kb, vb, dob, wst, ost, pblk, nbatch, seq):
    nq, nk = seq // SB_TQ, seq // SB_T
    T = nbatch * seq

    def body(q_ref, k_ref, v_ref, do_ref, wst_ref, ost_ref, up_ref, dq_ref, dk_ref, dv_ref, wbuf, obuf, sems):
        b, i = pl.program_id(0), pl.program_id(2)

        @pl.when(i == 0)
        def _():
            dk_ref[...] = jnp.zeros_like(dk_ref)
            dv_ref[...] = jnp.zeros_like(dv_ref)

        m0b = _lane_mask(SB_T, BF16)
        m0 = _lane_mask(SB_T) > 0.5
        up = up_ref[...]
        pairs = [slice(pp * 128, (pp + 1) * 128) for pp in range(SB_NP)]

        def loads(j, slot):
            tix = (b * nq + i) * nk + j
            return (pltpu.make_async_copy(wst_ref.at[tix], wbuf.at[slot], sems.at[0, slot]),
                    pltpu.make_async_copy(ost_ref.at[tix], obuf.at[slot], sems.at[1, slot]))

        def tile(j, carries, kind, slot, fetch_next):
            rows, mask = _sb_rows(kind), _sb_mask(kind)
            if fetch_next:
                for cp in loads(j + 1, 1 - slot):
                    cp.start()
            for cp in loads(j, slot):
                cp.wait()
            out = []
            for pp, (cols, (s0, s1, dq)) in enumerate(zip(pairs, carries)):
                q, do = q_ref[rows, cols], do_ref[rows, cols]
                kst = _pair_rows(k_ref, j, cols, m0b)
                vst = _pair_rows(v_ref, j, cols, m0b)
                wb = wbuf[slot, pp, rows, :]
                e = _dot_nt(do, vst) * wb.astype(F32)
                dlb = _bcast2(s0[rows], s1[rows]) + _scan1(e, up)
                dz = (e + dlb) * jnp.exp(obuf[slot, pp, rows, :].astype(F32)) - dlb
                if mask is not None:
                    dz = jnp.where(mask, dz, 0.0)
                dzb = dz.astype(BF16)
                dkst = _dot_tn(dzb, q)
                dvst = _dot_tn(wb, do)
                keys = pl.ds(pl.multiple_of(j * SB_T, SB_T), SB_T)
                dk_ref[keys, cols] = dk_ref[keys, cols] + jnp.where(m0, dkst[0:SB_T], dkst[SB_T:2 * SB_T])
                dv_ref[keys, cols] = dv_ref[keys, cols] + jnp.where(m0, dvst[0:SB_T], dvst[SB_T:2 * SB_T])
                x0, x1 = _rowsum2(e)
                ddq = _dot(dzb, kst)
                if kind == "B":
                    x0, x1, ddq = _lower(x0), _lower(x1), _lower(ddq)
                out.append((s0 + x0, s1 + x1, dq + ddq))
            return tuple(out)

        for cp in loads(0, 0):
            cp.start()

        def two(jj, cr):
            cr = tile(2 * jj, cr, "full", 0, True)
            return tile(2 * jj + 1, cr, "full", 1, True)

        zc = jnp.zeros((SB_TQ, 1), F32)
        carries = lax.fori_loop(0, i, two, ((zc, zc, jnp.zeros((SB_TQ, 128), F32)),) * SB_NP)
        carries = tile(2 * i, carries, "A", 0, True)
        carries = tile(2 * i + 1, carries, "B", 1, False)
        for cols, carry in zip(pairs, carries):
            dq_ref[:, cols] = carry[2] * SCALE

    wide = 128 * SB_NP
    blk = lambda b, g, i: (b * nq + i, g)
    full = lambda b, g, i: (b, g)
    hbm = pl.BlockSpec(memory_space=pl.ANY)
    return pl.pallas_call(
        body, name="sb_bwd", grid=(nbatch, 4 // SB_NP, nq),
        in_specs=[pl.BlockSpec((SB_TQ, wide), blk), pl.BlockSpec((seq, wide), full), pl.BlockSpec((seq, wide), full),
                  pl.BlockSpec((SB_TQ, wide), blk), hbm, hbm, _const_spec((SB_T, SB_T))],
        out_specs=[pl.BlockSpec((SB_TQ, wide), blk), pl.BlockSpec((seq, wide), full), pl.BlockSpec((seq, wide), full)],
        out_shape=[jax.ShapeDtypeStruct((T, 512), F32)] * 3,
        scratch_shapes=[pltpu.VMEM((2, SB_NP, SB_TQ, 2 * SB_T), BF16), pltpu.VMEM((2, SB_NP, SB_TQ, 2 * SB_T), BF16),
                        pltpu.SemaphoreType.DMA((2, 2))],
        compiler_params=_params(3),
    )(qb, kb, vb, dob, wst, ost, pblk)


def _sigmoid(g):
    return 1.0 / (1.0 + jnp.exp(-g))


def _out_proj(oa, ob, ga, gb, x2, tgt, wout_bf):
    T = x2.shape[0]

    def body(oa_ref, ob_ref, ga_ref, gb_ref, x_ref, t_ref, w_ref,
             dout_ref, doa_ref, dob_ref, dga_ref, dgb_ref, dw_ref, loss_ref):
        @pl.when(pl.program_id(0) == 0)
        def _():
            loss_ref[...] = jnp.zeros_like(loss_ref)
            dw_ref[...] = jnp.zeros_like(dw_ref)

        halves = ((oa_ref, ga_ref, doa_ref, dga_ref, 0), (ob_ref, gb_ref, dob_ref, dgb_ref, 512))
        out = x_ref[...]
        gated = []
        for o_ref, g_ref, _, _, lo in halves:
            g = g_ref[...]
            sg = _sigmoid(g)
            y = (o_ref[...] * (g * sg)).astype(BF16)
            out = out + _dot(y, w_ref[lo:lo + 512, :])
            gated.append((y, g, sg))
        diff = out - t_ref[...]
        dout = diff * (1.0 / D_MODEL)
        dout_ref[...] = dout
        loss_ref[...] = loss_ref[...] + jnp.sum(diff * diff) * (0.5 / D_MODEL)
        db = dout.astype(BF16)
        for (o_ref, _, do_ref, dg_ref, lo), (y, g, sg) in zip(halves, gated):
            dw_ref[lo:lo + 512, :] = dw_ref[lo:lo + 512, :] + _dot_tn(y, db)
            dy = _dot_nt(db, w_ref[lo:lo + 512, :])
            do_ref[...] = (dy * (g * sg)).astype(BF16)
            dg_ref[...] = (dy * o_ref[...] * (sg * (1.0 + g * (1.0 - sg)))).astype(BF16)

    rows = lambda w: pl.BlockSpec((TM, w), lambda i: (i, 0))
    return pl.pallas_call(
        body, name="out_proj", grid=(T // TM,),
        in_specs=[rows(512), rows(512), rows(512), rows(512), rows(D_MODEL), rows(D_MODEL),
                  _const_spec((D_MODEL, D_MODEL))],
        out_specs=[rows(D_MODEL)] + [rows(512)] * 4 + [_const_spec((D_MODEL, D_MODEL)), _const_spec((8, 128))],
        out_shape=[jax.ShapeDtypeStruct((T, D_MODEL), F32)] + [jax.ShapeDtypeStruct((T, 512), BF16)] * 4
        + [jax.ShapeDtypeStruct((D_MODEL, D_MODEL), F32), jax.ShapeDtypeStruct((8, 128), F32)],
        compiler_params=_params(1),
    )(oa, ob, ga, gb, x2, tgt, wout_bf)


def _qk_grad(qraw, kraw, dqrot, dkpad, dvpad, qg512, kg128, g512, cos, sa, sb, seq):
    T = qraw.shape[0]
    assert TM == SWA_Q
    ni = seq // SWA_Q

    def body(qraw_ref, kraw_ref, dqrot_ref, dk0, dk1, dk2, dk3, dv0, dv1, dv2, dv3,
             qg_ref, kg_ref, g_ref, cos_ref, sa_ref, sb_ref, dqa_ref, dkv_ref, dqg_ref, dkg_ref):
        @pl.when(pl.program_id(0) == 0)
        def _():
            dqg_ref[...] = jnp.zeros_like(dqg_ref)
            dkg_ref[...] = jnp.zeros_like(dkg_ref)

        cosv, sav, sbv = cos_ref[...], sa_ref[...], sb_ref[...]
        m0 = _lane_mask(TM) > 0.5

        def head_norm_bwd(raw, dn_rot, gmat, gain):
            r = lax.rsqrt(_dot_b(raw * raw, gmat) + EPS)
            n = raw * r
            dn = dn_rot * gain
            return r * (dn - n * _dot_b(dn * n, gmat)), jnp.sum(dn_rot * n, axis=0, keepdims=True)

        def fold(refs):
            a = jnp.concatenate([r[:, 0:128] for r in refs], axis=0)
            b = jnp.concatenate([r[:, 128:256] for r in refs], axis=0)
            return jnp.where(m0, a + pltpu.roll(a, 64, 1), b + pltpu.roll(b, 64, 1))

        dqn = jnp.concatenate([_rope_bwd(dqrot_ref[:, s * 128:(s + 1) * 128], cosv, sav, sbv) for s in range(4)], axis=1)
        dqa, dqg = head_norm_bwd(qraw_ref[...], dqn, g_ref[...], qg_ref[...])
        dka, dkg = head_norm_bwd(kraw_ref[...], _rope_bwd(fold((dk0, dk1, dk2, dk3)), cosv, sav, sbv),
                                 g_ref[0:128, 0:128], kg_ref[...])
        dqa_ref[...] = dqa.astype(BF16)
        dkv_ref[:, 0:128] = dka.astype(BF16)
        dkv_ref[:, 128:256] = fold((dv0, dv1, dv2, dv3)).astype(BF16)
        dqg_ref[...] = dqg_ref[...] + dqg
        dkg_ref[...] = dkg_ref[...] + dkg

    rows = lambda w: pl.BlockSpec((TM, w), lambda i: (i, 0))
    sub = SWA_Q // BLOCK
    shifted = [pl.BlockSpec((BLOCK, 256), functools.partial(
        lambda i, j: ((i // ni) * (ni + 1) * sub + (i % ni) * sub + 1 + j, 0), j=j)) for j in range(sub)]
    return pl.pallas_call(
        body, name="qk_grad", grid=(T // TM,),
        in_specs=[rows(512), rows(128), rows(512)] + shifted + shifted + [
            _const_spec((1, 512)), _const_spec((1, 128)), _const_spec((512, 512)), rows(128), rows(128), rows(128)],
        out_specs=[rows(512), rows(256), _const_spec((1, 512)), _const_spec((1, 128))],
        out_shape=[jax.ShapeDtypeStruct((T, 512), BF16), jax.ShapeDtypeStruct((T, 256), BF16),
                   jax.ShapeDtypeStruct((1, 512), F32), jax.ShapeDtypeStruct((1, 128), F32)],
        compiler_params=_params(1),
    )(qraw, kraw, dqrot, *([dkpad] * sub), *([dvpad] * sub), qg512, kg128, g512, cos, sa, sb)


_PIECES = ((QA, 512), (KA, 256), (GA, 512), (QB, 512), (KB, 512), (VB, 512), (GB, 512))


def _w_in_grad(h, pieces):
    T = h.shape[0]

    def body(h_ref, *refs):
        dw_ref = refs[-1]

        @pl.when(pl.program_id(0) == 0)
        def _():
            dw_ref[...] = jnp.zeros_like(dw_ref)

        hb = h_ref[...]
        for (lo, width), p_ref in zip(_PIECES, refs[:-1]):
            dw_ref[lo:lo + width, :] = dw_ref[lo:lo + width, :] + _dot_tn(p_ref[...].astype(BF16), hb)

    rows = lambda w: pl.BlockSpec((TM, w), lambda i: (i, 0))
    return pl.pallas_call(
        body, name="w_in_grad", grid=(T // TM,),
        in_specs=[rows(D_MODEL)] + [rows(w) for _, w in _PIECES],
        out_specs=_const_spec((IN_WIDTH, D_MODEL)),
        out_shape=jax.ShapeDtypeStruct((IN_WIDTH, D_MODEL), F32),
        compiler_params=_params(1),
    )(h, *pieces)


def _chip_exchange(src_ref, dst_ref, sems):
    _, _, c, chips = _place()
    return [_remote(src_ref.at[2 * cx + cy], dst_ref.at[j], sems, j, (cx, cy, c)) for j, (cx, cy) in enumerate(chips)]


def _x_grad(x2, dout, pieces, w_bf, ng, cin4):
    T = x2.shape[0]
    npc = len(_PIECES)
    steps = T // TM

    def body(x_ref, dout_ref, *refs):
        w_ref, ng_ref, cin_ref, gx_ref, dng_ref, rin_ref, ssem, rsem = refs[npc:]
        step = pl.program_id(0)
        copies = lambda: _chip_exchange(cin_ref, rin_ref, (ssem, rsem))

        @pl.when(step == 0)
        def _():
            dng_ref[...] = jnp.zeros_like(dng_ref)
            for cp in copies():
                cp.start()

        dh = jnp.zeros((TM, D_MODEL), F32)
        for (lo, width), p_ref in zip(_PIECES, refs[:npc]):
            dh = dh + _dot(p_ref[...].astype(BF16), w_ref[lo:lo + width, :])
        xb = x_ref[...]
        r = lax.rsqrt(jnp.mean(xb * xb, axis=-1, keepdims=True) + EPS)
        n = xb * r
        dn = dh * ng_ref[...]
        gx_ref[...] = dout_ref[...] + r * (dn - n * jnp.mean(dn * n, axis=-1, keepdims=True))
        dng_ref[...] = dng_ref[...] + jnp.sum(dh * n, axis=0, keepdims=True)

        @pl.when(step == steps - 1)
        def _():
            for cp in copies():
                cp.wait_recv()
            for cp in copies():
                cp.wait_send()

    rows = lambda w: pl.BlockSpec((TM, w), lambda i: (i, 0))
    hbm = pl.BlockSpec(memory_space=pl.ANY)
    return pl.pallas_call(
        body, name="x_grad", grid=(steps,),
        in_specs=[rows(D_MODEL), rows(D_MODEL)] + [rows(w) for _, w in _PIECES]
        + [_const_spec((IN_WIDTH, D_MODEL)), _const_spec((1, D_MODEL)), hbm],
        out_specs=[rows(D_MODEL), _const_spec((1, D_MODEL)), hbm],
        out_shape=[jax.ShapeDtypeStruct((T, D_MODEL), F32), jax.ShapeDtypeStruct((1, D_MODEL), F32),
                   jax.ShapeDtypeStruct((3,) + cin4.shape[1:], cin4.dtype)],
        scratch_shapes=[pltpu.SemaphoreType.DMA((3,)), pltpu.SemaphoreType.DMA((3,))],
        input_output_aliases={1: 0},
        compiler_params=_params(1),
    )(x2, dout, *pieces, w_bf, ng, cin4)


HBM = pl.BlockSpec(memory_space=pl.ANY)


def _place():
    x, y, c = lax.axis_index("x"), lax.axis_index("y"), lax.axis_index("c")
    chips = [(1 - x, y), (x, 1 - y), (1 - x, 1 - y)]
    return x, y, c, chips


def _remote(src, dst, sems, k, to):
    return pltpu.make_async_remote_copy(src_ref=src, dst_ref=dst, send_sem=sems[0].at[k], recv_sem=sems[1].at[k],
                                        device_id=to, device_id_type=MESH)


def _gather_plan(src, dst, sems, relay):
    x, y, c, _ = _place()
    rows = src.shape[1] // 2
    parts = (pl.ds(0, rows), pl.ds(rows, rows))
    me, kx, ky, kd = 2 * x + y, 2 * (1 - x) + y, 2 * x + 1 - y, 2 * (1 - x) + 1 - y
    to_x, to_y, to_d, sib = (1 - x, y, c), (x, 1 - y, c), (1 - x, 1 - y, c), (x, y, 1 - c)
    mine = [(src.at[c, p], dst.at[me, c, p]) for p in parts]
    direct = [_remote(*mine[0], sems, 0, to_x), _remote(*mine[1], sems, 1, to_x),
              _remote(*mine[1], sems, 2, to_y), _remote(*mine[0], sems, 3, to_y)]
    arrived = [dst.at[kx, c, parts[0]], dst.at[kx, c, parts[1]], dst.at[ky, c, parts[1]], dst.at[ky, c, parts[0]]]
    if relay:
        relays = [_remote(arrived[0], arrived[0], sems, 4, to_y), _remote(arrived[2], arrived[2], sems, 5, to_x)]
    else:
        relays = []
        direct += [_remote(*mine[0], sems, 4, to_d), _remote(*mine[1], sems, 5, to_d)]
    relayed = [dst.at[kd, c, parts[0]], dst.at[kd, c, parts[1]]]
    forwards = [_remote(dst.at[k, c], dst.at[k, c], sems, 6 + n, sib) for n, k in enumerate((kx, ky, kd))]
    from_sib = [dst.at[k, 1 - c] for k in (kx, ky, kd)]
    return direct, arrived, relays, relayed, forwards, from_sib


def _gather_start(src, dst, sems, relay):
    direct = _gather_plan(src, dst, sems, relay)[0]
    for k in (0, 2, 1, 3):
        direct[k].start()
    for cp in direct[4:]:
        cp.start()


def _gather_finish(src, dst, sems, relay):
    direct, arrived, relays, relayed, forwards, from_sib = _gather_plan(src, dst, sems, relay)
    landed = lambda ref, k: _remote(ref, ref, sems, k, (0, 0, 0)).wait_recv()
    landed(arrived[0], 0)
    if relay:
        relays[0].start()
    landed(arrived[2], 2)
    if relay:
        relays[1].start()
    landed(arrived[1], 1)
    forwards[0].start()
    landed(arrived[3], 3)
    forwards[1].start()
    landed(relayed[0], 4)
    landed(relayed[1], 5)
    forwards[2].start()
    for n, ref in enumerate(from_sib):
        landed(ref, 6 + n)
    for cp in direct + relays + forwards:
        cp.wait_send()


def _own_slot(gathered, shard):
    me = 2 * lax.axis_index("x") + lax.axis_index("y")
    return lax.dynamic_update_slice(gathered, shard[None], (me, 0, 0, 0))


def _pair_exchange(g4):
    def body(src, dst, ssem, rsem):
        x, y, c, _ = _place()
        cp = _remote(src.at[:, pl.ds(1 - c, 1)], dst, (ssem, rsem), 0, (x, y, 1 - c))
        cp.start()
        cp.wait()

    return pl.pallas_call(
        body, name=f"pair_exchange_{g4.shape[2]}", in_specs=[HBM], out_specs=HBM,
        out_shape=jax.ShapeDtypeStruct((4, 1) + g4.shape[2:], g4.dtype),
        scratch_shapes=[pltpu.SemaphoreType.DMA((1,)), pltpu.SemaphoreType.DMA((1,))],
    )(g4)


def _share_halves(hin, hout, small):
    def body(hin_ref, hout_ref, small_ref, oin_ref, oout_ref, sall_ref, ssem, rsem, lsem):
        x, y, c, _ = _place()
        sems = (ssem, rsem)
        sib = (x, y, 1 - c)
        me = 4 * x + 2 * y + c
        own = pltpu.make_async_copy(small_ref, sall_ref.at[me], lsem.at[0])
        own.start()
        sent = [_remote(hin_ref, oin_ref, sems, 0, sib), _remote(hout_ref, oout_ref, sems, 1, sib)]
        flips = [(fx, fy, fc) for fx in (0, 1) for fy in (0, 1) for fc in (0, 1)][1:]
        for k, (fx, fy, fc) in enumerate(flips):
            sent.append(_remote(small_ref, sall_ref.at[me], sems, 2 + k, (x ^ fx, y ^ fy, c ^ fc)))
        for cp in sent:
            cp.start()
        _remote(hin_ref, oin_ref, sems, 0, sib).wait_recv()
        _remote(hout_ref, oout_ref, sems, 1, sib).wait_recv()
        for k, (fx, fy, fc) in enumerate(flips):
            src = 4 * (x ^ fx) + 2 * (y ^ fy) + (c ^ fc)
            _remote(small_ref, sall_ref.at[src], sems, 2 + k, sib).wait_recv()
        for cp in sent:
            cp.wait_send()
        own.wait()

    return pl.pallas_call(
        body, name="share_halves", in_specs=[HBM, HBM, HBM], out_specs=[HBM, HBM, HBM],
        out_shape=[jax.ShapeDtypeStruct(hin.shape, F32), jax.ShapeDtypeStruct(hout.shape, F32),
                   jax.ShapeDtypeStruct((8,) + small.shape, F32)],
        scratch_shapes=[pltpu.SemaphoreType.DMA((9,)), pltpu.SemaphoreType.DMA((9,)), pltpu.SemaphoreType.DMA((1,))],
    )(hin, hout, small)


def _add_half(cidx, full4, recv4):
    _, _, rows, width = full4.shape

    def body(c_ref, a_ref, b_ref, o_ref):
        o_ref[0] = (a_ref[0, 0] + b_ref[0, 0]).astype(BF16)

    return pl.pallas_call(
        body, name=f"add_half_{rows}",
        grid_spec=pltpu.PrefetchScalarGridSpec(
            num_scalar_prefetch=1, grid=(4,),
            in_specs=[pl.BlockSpec((1, 1, rows, width), lambda k, c: (k, c[0], 0, 0)),
                      pl.BlockSpec((1, 1, rows, width), lambda k, c: (k, 0, 0, 0))],
            out_specs=pl.BlockSpec((1, rows, width), lambda k, c: (k, 0, 0))),
        out_shape=jax.ShapeDtypeStruct((4, rows, width), BF16),
        compiler_params=_params(1),
    )(cidx, full4, recv4)


def _sum_chips(chip, own4, recv3):
    _, rows, width = recv3.shape
    rb = rows // 2

    def body(k_ref, a_ref, r_ref, o_ref):
        acc = a_ref[0].astype(F32)
        for s in range(3):
            acc = acc + r_ref[s].astype(F32)
        o_ref[...] = acc

    return pl.pallas_call(
        body, name=f"sum_chips_{rows}",
        grid_spec=pltpu.PrefetchScalarGridSpec(
            num_scalar_prefetch=1, grid=(rows // rb,),
            in_specs=[pl.BlockSpec((1, rb, width), lambda i, k: (k[0], i, 0)),
                      pl.BlockSpec((3, rb, width), lambda i, k: (0, i, 0))],
            out_specs=pl.BlockSpec((rb, width), lambda i, k: (i, 0))),
        out_shape=jax.ShapeDtypeStruct((rows, width), F32),
        compiler_params=_params(1),
    )(chip, own4, recv3)


def _adam_math(w, g, m, v):
    c1 = 1.0 - ADAM_B1 ** ADAM_STEP
    c2 = 1.0 - ADAM_B2 ** ADAM_STEP
    nm = ADAM_B1 * m + (1.0 - ADAM_B1) * g
    nv = ADAM_B2 * v + (1.0 - ADAM_B2) * (g * g)
    return -ADAM_LR * ((nm / c1) / (jnp.sqrt(nv / c2) + ADAM_EPS) + ADAM_WD * w), nm, nv


def _small_update(small_all, params):
    n = len(params)
    flat = [a for p in params for a in p]

    def body(s_ref, *refs):
        ins, loss_ref, outs = refs[:3 * n], refs[3 * n], refs[3 * n + 1:]
        total = s_ref[0]
        for d in range(1, 8):
            total = total + s_ref[d]
        loss_ref[...] = total[n:n + 1, 0:1]
        for r in range(n):
            w_ref, m_ref, v_ref = ins[3 * r:3 * r + 3]
            g = total[r:r + 1, 0:w_ref.shape[1]]
            outs[4 * r][...] = g
            outs[4 * r + 1][...], outs[4 * r + 2][...], outs[4 * r + 3][...] = _adam_math(w_ref[...], g, m_ref[...], v_ref[...])

    whole = lambda a: pl.BlockSpec(a.shape, lambda i: (0,) * a.ndim)
    out_shape = [jax.ShapeDtypeStruct((1, 1), F32)] + [jax.ShapeDtypeStruct(p[0].shape, F32) for p in params for _ in range(4)]
    res = pl.pallas_call(
        body, name="small_update", grid=(1,),
        in_specs=[whole(small_all)] + [whole(a) for a in flat], out_specs=[whole(s) for s in out_shape],
        out_shape=out_shape, compiler_params=_params(1),
    )(small_all, *flat)
    return res[0], [tuple(res[1 + 4 * r:5 + 4 * r]) for r in range(n)]


def _adamw_halves(cidx, w, own, recv, m, v):
    rows, width = w.shape
    rb = rows // 4

    def body(c_ref, w_ref, own_ref, recv_ref, m_ref, v_ref, g_ref, d_ref, nm_ref, nv_ref):
        mine = (pl.program_id(0) // 2) == c_ref[0]
        g = jnp.where(mine, own_ref[...], recv_ref[...])
        g_ref[...] = g
        d_ref[...], nm_ref[...], nv_ref[...] = _adam_math(w_ref[...], g, m_ref[...], v_ref[...])

    full = pl.BlockSpec((rb, width), lambda i, c: (i, 0))
    half = pl.BlockSpec((rb, width), lambda i, c: (i % 2, 0))
    return pl.pallas_call(
        body, name=f"adamw_halves_{rows}",
        grid_spec=pltpu.PrefetchScalarGridSpec(
            num_scalar_prefetch=1, grid=(4,),
            in_specs=[full, half, half, full, full], out_specs=[full] * 4),
        out_shape=[jax.ShapeDtypeStruct((rows, width), F32)] * 4,
        compiler_params=_params(1),
    )(cidx, w, own, recv, m, v)


def _constants():
    idx = jnp.arange(512)
    g512 = jnp.where(idx[:, None] // HEAD_DIM == idx[None, :] // HEAD_DIM, 1.0 / HEAD_DIM, 0.0).astype(BF16)
    j = jnp.arange(SB_T)
    ublk = jnp.where(j[:, None] >= j[None, :], 1.0, 0.0).astype(BF16)
    pblk = jnp.where(j[:, None] < j[None, :], 1.0, 0.0).astype(BF16)
    return g512, jnp.concatenate([ublk, ublk], axis=0), pblk


def _pad_rows(v, width):
    return jnp.pad(v, ((0, 0), (0, width - v.shape[1])))


def _pair_sum(cidx, partial, rows):
    g4 = partial.reshape(4, 2, rows, D_MODEL)
    return _add_half(cidx, g4, _pair_exchange(g4))


def _step(x2, tgt, positions, norm_gain, q_norm_gain, k_norm_gain, sinks, win_shard, wout_shard, nbatch, seq):
    g512, ublk, pblk = _constants()
    qg512 = jnp.tile(q_norm_gain, (1, 8))
    kg128 = jnp.tile(k_norm_gain, (1, 2))
    sink1 = sinks.reshape(8)
    cidx = lax.axis_index("c").astype(jnp.int32).reshape(1)
    chip = (2 * lax.axis_index("x") + lax.axis_index("y")).astype(jnp.int32).reshape(1)

    h, (cos, sa, sb), w_bf = _prologue(x2, norm_gain, positions.reshape(-1, 1), win_shard)
    (qraw, kraw, qrot, k2, v2, ga, qb, kb, vb, gb), wout_bf = _norm_proj(
        h, w_bf, qg512, kg128, g512, cos, sa, sb, wout_shard)
    oa = _swa_fwd(sink1, qrot, k2, v2, nbatch, seq)
    ob, wst, ost = _sb_fwd(qb, kb, vb, ublk, nbatch, seq)
    dout, doa, dob, dga, dgb, dwout, loss_acc = _out_proj(oa, ob, ga, gb, x2, tgt, wout_bf)

    cout4 = _pair_sum(cidx, dwout, SHARD_OUT // 2)
    dqrot, dk2, dv2, dsink, rout3 = _swa_bwd(sink1, qrot, k2, v2, doa, cout4, nbatch, seq)
    dqb, dkb, dvb = _sb_bwd(qb, kb, vb, dob, wst, ost, pblk, nbatch, seq)
    dqa, dkv, dqg, dkg = _qk_grad(qraw, kraw, dqrot, dk2, dv2, qg512, kg128, g512, cos, sa, sb, seq)
    pieces = (dqa, dkv, dga, dqb, dkb, dvb, dgb)
    cin4 = _pair_sum(cidx, _w_in_grad(h, pieces), SHARD_IN // 2)
    gx, dng, rin3 = _x_grad(x2, dout, pieces, w_bf, norm_gain, cin4)
    own_in, own_out = _sum_chips(chip, cin4, rin3), _sum_chips(chip, cout4, rout3)

    dqg64 = dqg.reshape(8, HEAD_DIM).sum(axis=0, keepdims=True)
    dkg64 = dkg.reshape(2, HEAD_DIM).sum(axis=0, keepdims=True)
    small = jnp.concatenate([dng, _pad_rows(dqg64, D_MODEL), _pad_rows(dkg64, D_MODEL),
                             _pad_rows(dsink[:, 0].reshape(1, 8), D_MODEL), _pad_rows(loss_acc[0:1, 0:1], D_MODEL),
                             jnp.zeros((3, D_MODEL), F32)], axis=0)
    sib_in, sib_out, small_all = _share_halves(own_in, own_out, small)
    return gx, cidx, (own_in, sib_in), (own_out, sib_out), small_all


def kernel(x, positions, norm_gain, w_in, q_norm_gain, k_norm_gain, sinks, w_out, loss_target, m_norm_gain, m_w_in, m_q_norm_gain, m_k_norm_gain, m_sinks, m_w_out, v_norm_gain, v_w_in, v_q_norm_gain, v_k_norm_gain, v_sinks, v_w_out):
    nbatch, seq, _ = x.shape
    T = nbatch * seq
    x2 = x.reshape(T, D_MODEL)
    tgt = loss_target.reshape(T, D_MODEL)
    tr = lambda a: jnp.swapaxes(a[0], 0, 1)
    win_t, m_win_t, v_win_t = tr(w_in), tr(m_w_in), tr(v_w_in)

    win_shard = win_t.astype(BF16).reshape(2, SHARD_IN // 2, D_MODEL)
    wout_shard = w_out[0].astype(BF16).reshape(2, SHARD_OUT // 2, D_MODEL)
    gx, cidx, g_in_halves, g_out_halves, small_all = _step(
        x2, tgt, positions, norm_gain, q_norm_gain, k_norm_gain, sinks, win_shard, wout_shard, nbatch, seq)

    g_in, d_in, nm_in, nv_in = [jnp.swapaxes(a, 0, 1) for a in
                                _adamw_halves(cidx, win_t, *g_in_halves, m_win_t, v_win_t)]
    g_out, d_out, nm_out, nv_out = _adamw_halves(cidx, w_out[0], *g_out_halves, m_w_out[0], v_w_out[0])
    loss, small = _small_update(small_all, [(norm_gain, m_norm_gain, v_norm_gain), (q_norm_gain, m_q_norm_gain, v_q_norm_gain),
                                            (k_norm_gain, m_k_norm_gain, v_k_norm_gain), (sinks, m_sinks, v_sinks)])
    (g_ng, d_ng, m_ng, v_ng), (g_qg, d_qg, m_qg, v_qg), (g_kg, d_kg, m_kg, v_kg), (g_sk, d_sk, m_sk, v_sk) = small
    return (loss.reshape(()), gx.reshape(nbatch, seq, D_MODEL),
            g_ng, g_in[None], g_qg, g_kg, g_sk, g_out[None],
            d_ng, d_in[None], d_qg, d_kg, d_sk, d_out[None],
            m_ng, nm_in[None], m_qg, m_kg, m_sk, nm_out[None],
            v_ng, nv_in[None], v_qg, v_kg, v_sk, nv_out[None])
```

```python
import functools
import math

import jax
import jax.numpy as jnp
from jax import lax
from jax.experimental import pallas as pl
from jax.experimental.pallas import tpu as pltpu

F32 = jnp.float32
BF16 = jnp.bfloat16

D_MODEL = 1024
HEAD_DIM = 64
BLOCK = 128
ROPE_THETA = 10000.0
EPS = 1e-6
QA, KA, VA, GA, QB, KB, VB, GB = 0, 512, 640, 768, 1280, 1792, 2304, 2816
IN_WIDTH = 3328
SHARD_IN = IN_WIDTH // 4
SHARD_OUT = D_MODEL // 4
SCALE = 1.0 / math.sqrt(HEAD_DIM)
NEG = -1e30

ADAM_LR, ADAM_B1, ADAM_B2, ADAM_EPS, ADAM_WD, ADAM_STEP = 0.001, 0.9, 0.999, 1e-08, 0.01, 10

TM = 512
VMEM_LIMIT = 56 * 1024 * 1024
MESH = pl.DeviceIdType.MESH


def _dot(a, b):
    return jnp.dot(a, b, preferred_element_type=F32)


def _dot_nt(a, b):
    return lax.dot_general(a, b, (((1,), (1,)), ((), ())), preferred_element_type=F32)


def _dot_tn(a, b):
    return lax.dot_general(a, b, (((0,), (0,)), ((), ())), preferred_element_type=F32)


def _dot_b(a, m):
    return _dot(a.astype(BF16), m)


def _params(n_axes=None, vmem=VMEM_LIMIT):
    sem = None if n_axes is None else ("arbitrary",) * n_axes
    return pltpu.CompilerParams(dimension_semantics=sem, vmem_limit_bytes=vmem)


def _const_spec(shape):
    nd = len(shape)
    return pl.BlockSpec(shape, lambda *_: (0,) * nd)


def _rope_fwd(x, cos, sa, sb):
    return x * cos + pltpu.roll(x, 96, 1) * sa + pltpu.roll(x, 32, 1) * sb


def _rope_bwd(d, cos, sa, sb):
    return d * cos - pltpu.roll(d, 96, 1) * sa - pltpu.roll(d, 32, 1) * sb


def _lane_mask(rows, dtype=F32):
    lane = lax.broadcasted_iota(jnp.int32, (rows, 128), 1)
    return jnp.where(lane < HEAD_DIM, 1.0, 0.0).astype(dtype)


def _prologue(x2, ng, pos, win_shard):
    T = x2.shape[0]
    steps = T // TM
    half = HEAD_DIM // 2
    inv_freq = jnp.tile(ROPE_THETA ** (-jnp.arange(half, dtype=F32) * 2.0 / HEAD_DIM), 4).reshape(1, 128)

    def body(x_ref, ng_ref, pos_ref, freq_ref, wi_ref, h_ref, cos_ref, sa_ref, sb_ref, ai_ref, ssem, rsem):
        @pl.when(pl.program_id(0) == 0)
        def _():
            _gather_start(wi_ref, ai_ref, (ssem, rsem), relay=True)

        xb = x_ref[...]
        r = lax.rsqrt(jnp.mean(xb * xb, axis=-1, keepdims=True) + EPS)
        h_ref[...] = (xb * r * ng_ref[...]).astype(BF16)
        ang = pos_ref[...].astype(F32) * freq_ref[...]
        first = (lax.broadcasted_iota(jnp.int32, (TM, 128), 1) & (HEAD_DIM - 1)) < half
        sn = jnp.sin(ang)
        cos_ref[...] = jnp.cos(ang)
        sa_ref[...] = jnp.where(first, -sn, 0.0)
        sb_ref[...] = jnp.where(first, 0.0, sn)

        @pl.when(pl.program_id(0) == steps - 1)
        def _():
            _gather_finish(wi_ref, ai_ref, (ssem, rsem), relay=True)

    rows = lambda w: pl.BlockSpec((TM, w), lambda i: (i, 0))
    hbm = pl.BlockSpec(memory_space=pl.ANY)
    h, cos, sa, sb, gathered = pl.pallas_call(
        body, name="prologue", grid=(steps,),
        in_specs=[rows(D_MODEL), _const_spec((1, D_MODEL)), rows(1), _const_spec((1, 128)), hbm],
        out_specs=[rows(D_MODEL), rows(128), rows(128), rows(128), hbm],
        out_shape=[jax.ShapeDtypeStruct((T, D_MODEL), BF16)] + [jax.ShapeDtypeStruct((T, 128), F32)] * 3
        + [jax.ShapeDtypeStruct((4,) + win_shard.shape, win_shard.dtype)],
        scratch_shapes=[pltpu.SemaphoreType.DMA((9,)), pltpu.SemaphoreType.DMA((9,))],
        compiler_params=_params(1),
    )(x2, ng, pos, inv_freq, win_shard)
    return h, (cos, sa, sb), _own_slot(gathered, win_shard).reshape(IN_WIDTH, D_MODEL)


def _norm_proj(h, w_bf, qg512, kg128, g512, cos, sa, sb, wout_shard):
    T = h.shape[0]
    steps = T // TM

    def body(h_ref, w_ref, qg_ref, kg_ref, g_ref, cos_ref, sa_ref, sb_ref, wo_ref,
             qraw_ref, kraw_ref, qrot_ref, k2_ref, v2_ref, ga_ref, qb_ref, kb_ref, vb_ref, gb_ref, ao_ref,
             ssem, rsem):
        @pl.when(pl.program_id(0) == 0)
        def _():
            _gather_start(wo_ref, ao_ref, (ssem, rsem), relay=False)

        h = h_ref[...]
        cosv, sav, sbv = cos_ref[...], sa_ref[...], sb_ref[...]
        m0 = _lane_mask(TM)

        def dup(v):
            v0 = v * m0
            v1 = v - v0
            return v0 + pltpu.roll(v0, 64, 1), v1 + pltpu.roll(v1, 64, 1)

        qa = _dot_nt(h, w_ref[QA:KA, :])
        qraw_ref[...] = qa
        qn = qa * lax.rsqrt(_dot_b(qa * qa, g_ref[...]) + EPS) * qg_ref[...]
        for s in range(4):
            qs = _rope_fwd(qn[:, s * 128:(s + 1) * 128], cosv, sav, sbv)
            qrot_ref[:, s * 128:(s + 1) * 128] = (qs * SCALE).astype(BF16)
        ka = _dot_nt(h, w_ref[KA:VA, :])
        kraw_ref[...] = ka
        kn = ka * lax.rsqrt(_dot_b(ka * ka, g_ref[0:128, 0:128]) + EPS) * kg_ref[...]
        k0, k1 = dup(_rope_fwd(kn, cosv, sav, sbv))
        k2_ref[:, 0:128] = k0.astype(BF16)
        k2_ref[:, 128:256] = k1.astype(BF16)
        v0, v1 = dup(_dot_nt(h, w_ref[VA:GA, :]))
        v2_ref[:, 0:128] = v0.astype(BF16)
        v2_ref[:, 128:256] = v1.astype(BF16)
        ga_ref[...] = _dot_nt(h, w_ref[GA:QB, :])
        qb_ref[...] = (_dot_nt(h, w_ref[QB:KB, :]) * SCALE).astype(BF16)
        kb_ref[...] = _dot_nt(h, w_ref[KB:VB, :]).astype(BF16)
        vb_ref[...] = _dot_nt(h, w_ref[VB:GB, :]).astype(BF16)
        gb_ref[...] = _dot_nt(h, w_ref[GB:IN_WIDTH, :])

        @pl.when(pl.program_id(0) == steps - 1)
        def _():
            _gather_finish(wo_ref, ao_ref, (ssem, rsem), relay=False)

    def rows(w):
        return pl.BlockSpec((TM, w), lambda i: (i, 0))

    outs = [(512, F32), (128, F32), (512, BF16), (256, BF16), (256, BF16), (512, F32),
            (512, BF16), (512, BF16), (512, BF16), (512, F32)]
    hbm = pl.BlockSpec(memory_space=pl.ANY)
    res = pl.pallas_call(
        body, name="norm_proj", grid=(steps,),
        in_specs=[rows(D_MODEL), _const_spec((IN_WIDTH, D_MODEL)), _const_spec((1, 512)),
                  _const_spec((1, 128)), _const_spec((512, 512)), rows(128), rows(128), rows(128), hbm],
        out_specs=[rows(w) for w, _ in outs] + [hbm],
        out_shape=[jax.ShapeDtypeStruct((T, w), dt) for w, dt in outs]
        + [jax.ShapeDtypeStruct((4,) + wout_shard.shape, wout_shard.dtype)],
        scratch_shapes=[pltpu.SemaphoreType.DMA((9,)), pltpu.SemaphoreType.DMA((9,))],
        compiler_params=_params(1),
    )(h, w_bf, qg512, kg128, g512, cos, sa, sb, wout_shard)
    return res[:-1], _own_slot(res[-1], wout_shard).reshape(D_MODEL, D_MODEL)


SWA_Q = 512
SWA_SUB = SWA_Q // BLOCK


def _swa_scores(q, kst, mask, sink_ref, kv):
    s_all = _dot_nt(q, kst)
    first = lax.broadcasted_iota(jnp.int32, (256, 1), 0) < 128
    probs, stats = [], []
    for hh in range(2):
        sink = jnp.where(first, sink_ref[kv * 4 + hh], sink_ref[kv * 4 + 2 + hh])
        s = jnp.where(mask, s_all[:, hh * 256:(hh + 1) * 256], NEG)
        m = jnp.maximum(jnp.max(s, axis=1, keepdims=True), sink)
        pe = jnp.exp(s - m)
        inv = 1.0 / (jnp.sum(pe, axis=1, keepdims=True) + jnp.exp(sink - m))
        probs.append(pe * inv)
        stats.append(jnp.exp(sink - m) * inv)
    return probs, stats


def _swa_mask(has_prev):
    r = lax.broadcasted_iota(jnp.int32, (256, 256), 0) & 127
    c = lax.broadcasted_iota(jnp.int32, (256, 256), 1)
    band = (c > r) & (c <= r + 128)
    return band if has_prev is True else band & ((c >= 128) | has_prev)


def _stack_pairs(ref, rows, kv):
    return jnp.concatenate([ref[rows, (2 * kv) * 128:(2 * kv + 1) * 128], ref[rows, (2 * kv + 1) * 128:(2 * kv + 2) * 128]], axis=0)


def _swa_keys(prev_ref, main_ref, s, kv, m0b):
    cols = slice(kv * 128, (kv + 1) * 128)
    prev = prev_ref[:, cols] if s == 0 else main_ref[(s - 1) * 128:s * 128, cols]
    kc = jnp.concatenate([prev, main_ref[s * 128:(s + 1) * 128, cols]], axis=0)
    k0 = kc * m0b
    return jnp.concatenate([k0, kc - k0], axis=0)


def _swa_fwd(sinks, qrot, k2, v2, nbatch, seq):
    ni = seq // SWA_Q
    T = nbatch * seq

    def body(sink_ref, q_ref, kp_ref, km_ref, vp_ref, vm_ref, o_ref):
        i = pl.program_id(1)
        m0b = _lane_mask(256, BF16)
        for s in range(SWA_SUB):
            mask = _swa_mask(True if s else i > 0)
            rows = slice(s * 128, (s + 1) * 128)
            for kv in range(2):
                kst = _swa_keys(kp_ref, km_ref, s, kv, m0b)
                vst = _swa_keys(vp_ref, vm_ref, s, kv, m0b)
                probs, _ = _swa_scores(_stack_pairs(q_ref, rows, kv), kst, mask, sink_ref, kv)
                o2 = _dot(jnp.concatenate(probs, axis=1).astype(BF16), vst)
                o_ref[rows, kv * 256:kv * 256 + 128] = o2[0:128].astype(BF16)
                o_ref[rows, kv * 256 + 128:(kv + 1) * 256] = o2[128:256].astype(BF16)

    main = lambda b, i: (b * ni + i, 0)
    prev = lambda b, i: ((b * ni + i) * SWA_SUB - jnp.where(i > 0, 1, 0), 0)
    return pl.pallas_call(
        body, name="swa_fwd", grid=(nbatch, ni),
        in_specs=[pl.BlockSpec(memory_space=pltpu.SMEM), pl.BlockSpec((SWA_Q, 512), main),
                  pl.BlockSpec((128, 256), prev), pl.BlockSpec((SWA_Q, 256), main),
                  pl.BlockSpec((128, 256), prev), pl.BlockSpec((SWA_Q, 256), main)],
        out_specs=pl.BlockSpec((SWA_Q, 512), main),
        out_shape=jax.ShapeDtypeStruct((T, 512), BF16),
        compiler_params=_params(2),
    )(sinks, qrot, k2, k2, v2, v2)


def _swa_bwd(sinks, qrot, k2, v2, doa, gout4, nbatch, seq):
    ni = seq // SWA_Q
    T = nbatch * seq

    def body(sink_ref, q_ref, kp_ref, km_ref, vp_ref, vm_ref, do_ref, gout_ref,
             dq_ref, dk_ref, dv_ref, ds_ref, rout_ref, dkc, dvc, ssem, rsem):
        b, i = pl.program_id(0), pl.program_id(1)
        copies = lambda: [_pair_copy(gout_ref, rout_ref, (ssem, rsem))]

        @pl.when((b == 0) & (i == 0))
        def _():
            ds_ref[...] = jnp.zeros_like(ds_ref)
            for cp in copies():
                cp.start()

        @pl.when((b == nbatch - 1) & (i == ni))
        def _():
            for cp in copies():
                cp.wait_recv()
            for cp in copies():
                cp.wait_send()

        @pl.when(i == 0)
        def _():
            dkc[...] = jnp.zeros_like(dkc)
            dvc[...] = jnp.zeros_like(dvc)

        @pl.when(i < ni)
        def _():
            m0b = _lane_mask(256, BF16)
            m0 = _lane_mask(128) > 0.5
            for kv in range(2):
                kcols = slice(kv * 128, (kv + 1) * 128)
                dk_own, dv_own = dkc[:, kcols], dvc[:, kcols]
                for s in range(SWA_SUB):
                    mask = _swa_mask(True if s else i > 0)
                    rows = slice(s * 128, (s + 1) * 128)
                    kst = _swa_keys(kp_ref, km_ref, s, kv, m0b)
                    vst = _swa_keys(vp_ref, vm_ref, s, kv, m0b)
                    q, do = _stack_pairs(q_ref, rows, kv), _stack_pairs(do_ref, rows, kv)
                    probs, psink = _swa_scores(q, kst, mask, sink_ref, kv)
                    dp_all = _dot_nt(do, vst)
                    ds_parts = []
                    for hh in range(2):
                        dp = dp_all[:, hh * 256:(hh + 1) * 256]
                        delta = jnp.sum(probs[hh] * dp, axis=1, keepdims=True)
                        ds_parts.append(probs[hh] * (dp - delta))
                        dsink = psink[hh] * delta
                        for pr in range(2):
                            h = kv * 4 + pr * 2 + hh
                            ds_ref[h:h + 1, :] = ds_ref[h:h + 1, :] - jnp.sum(dsink[pr * 128:(pr + 1) * 128])
                    ds_all = jnp.concatenate(ds_parts, axis=1).astype(BF16)
                    p_all = jnp.concatenate(probs, axis=1).astype(BF16)
                    dq2 = _dot(ds_all, kst) * SCALE
                    dq_ref[rows, kv * 256:kv * 256 + 128] = dq2[0:128]
                    dq_ref[rows, kv * 256 + 128:(kv + 1) * 256] = dq2[128:256]
                    dkst = _dot_tn(ds_all, q)
                    dvst = _dot_tn(p_all, do)
                    dk_ref[rows, kcols] = dk_own + jnp.where(m0, dkst[0:128], dkst[256:384])
                    dv_ref[rows, kcols] = dv_own + jnp.where(m0, dvst[0:128], dvst[256:384])
                    dk_own = jnp.where(m0, dkst[128:256], dkst[384:512])
                    dv_own = jnp.where(m0, dvst[128:256], dvst[384:512])
                dkc[:, kcols] = dk_own
                dvc[:, kcols] = dv_own

        @pl.when(i == ni)
        def _():
            dk_ref[...] = jnp.zeros_like(dk_ref)
            dv_ref[...] = jnp.zeros_like(dv_ref)
            dk_ref[0:128, :] = dkc[...]
            dv_ref[0:128, :] = dvc[...]

    main = lambda b, i: (b * ni + jnp.minimum(i, ni - 1), 0)
    prev = lambda b, i: ((b * ni + jnp.minimum(i, ni - 1)) * SWA_SUB - jnp.where(jnp.minimum(i, ni - 1) > 0, 1, 0), 0)
    shifted = lambda b, i: (b * (ni + 1) + i, 0)
    tpad = nbatch * (ni + 1) * SWA_Q
    return pl.pallas_call(
        body, name="swa_bwd", grid=(nbatch, ni + 1),
        in_specs=[pl.BlockSpec(memory_space=pltpu.SMEM), pl.BlockSpec((SWA_Q, 512), main),
                  pl.BlockSpec((128, 256), prev), pl.BlockSpec((SWA_Q, 256), main),
                  pl.BlockSpec((128, 256), prev), pl.BlockSpec((SWA_Q, 256), main),
                  pl.BlockSpec((SWA_Q, 512), main), pl.BlockSpec(memory_space=pl.ANY)],
        out_specs=[pl.BlockSpec((SWA_Q, 512), main), pl.BlockSpec((SWA_Q, 256), shifted),
                   pl.BlockSpec((SWA_Q, 256), shifted), _const_spec((8, 128)), pl.BlockSpec(memory_space=pl.ANY)],
        out_shape=[jax.ShapeDtypeStruct((T, 512), F32), jax.ShapeDtypeStruct((tpad, 256), F32),
                   jax.ShapeDtypeStruct((tpad, 256), F32), jax.ShapeDtypeStruct((8, 128), F32),
                   jax.ShapeDtypeStruct((4, 1) + gout4.shape[2:], gout4.dtype)],
        scratch_shapes=[pltpu.VMEM((128, 256), F32), pltpu.VMEM((128, 256), F32),
                        pltpu.SemaphoreType.DMA((1,)), pltpu.SemaphoreType.DMA((1,))],
        compiler_params=_params(2),
    )(sinks, qrot, k2, k2, v2, v2, doa, gout4)


SB_T = 256
SB_TQ = 2 * SB_T


def _sb_mask(kind):
    if kind == "full":
        return None
    rows = SB_T if kind == "B" else SB_TQ
    r = lax.broadcasted_iota(jnp.int32, (rows, 2 * SB_T), 0)
    c = lax.broadcasted_iota(jnp.int32, (rows, 2 * SB_T), 1)
    causal = (c & (SB_T - 1)) < r
    return causal | (r >= SB_T) if kind == "A" else causal


def _sb_rows(kind):
    return slice(SB_T, SB_TQ) if kind == "B" else slice(0, SB_TQ)


def _lower(x):
    return jnp.concatenate([jnp.zeros_like(x), x], axis=0)


def _sb_logits(neg_q, kst):
    nz = _dot_nt(neg_q, kst)
    sign = jnp.uint32(0x80000000)
    neg_abs = lax.bitcast_convert_type(lax.bitcast_convert_type(nz, jnp.uint32) | sign, F32)
    return nz, jnp.minimum(nz, 0.0) - jnp.log(1.0 + jnp.exp(neg_abs))


SB_NP = 4


def _pair_rows(ref, j, cols, m0b):
    kj = ref[pl.ds(pl.multiple_of(j * SB_T, SB_T), SB_T), cols]
    k0 = kj * m0b
    return jnp.concatenate([k0, kj - k0], axis=0)


def _bcast2(c0, c1):
    rows = c0.shape[0]
    return jnp.concatenate([jnp.broadcast_to(c0, (rows, SB_T)), jnp.broadcast_to(c1, (rows, SB_T))], axis=1)


def _rowsum2(x):
    return jnp.sum(x[:, 0:SB_T], axis=1, keepdims=True), jnp.sum(x[:, SB_T:2 * SB_T], axis=1, keepdims=True)


def _scan2(x, tri2):
    outs = []
    for h in range(2):
        xh = x[:, h * SB_T:(h + 1) * SB_T]
        hi = xh.astype(BF16)
        lo = (xh - hi.astype(F32)).astype(BF16)
        outs.append(_dot(jnp.concatenate([hi, lo], axis=1), tri2))
    return jnp.concatenate(outs, axis=1)


def _scan1(x, tri):
    xb = x.astype(BF16)
    return jnp.concatenate([_dot(xb[:, h * SB_T:(h + 1) * SB_T], tri) for h in range(2)], axis=1)


def _sb_fwd(qb, kb, vb, ublk, nbatch, seq):
    nq, nk = seq // SB_TQ, seq // SB_T
    T = nbatch * seq

    def body(q_ref, k_ref, v_ref, u_ref, o_ref, wst_ref, ost_ref, wbuf, obuf, sems):
        b, i = pl.program_id(0), pl.program_id(2)
        m0b = _lane_mask(SB_T, BF16)
        u = u_ref[...]
        pairs = [slice(pp * 128, (pp + 1) * 128) for pp in range(SB_NP)]
        neg_qs = [-q_ref[:, cols] for cols in pairs]

        def stores(j, slot):
            tix = (b * nq + i) * nk + j
            return (pltpu.make_async_copy(wbuf.at[slot], wst_ref.at[tix], sems.at[0, slot]),
                    pltpu.make_async_copy(obuf.at[slot], ost_ref.at[tix], sems.at[1, slot]))

        def tile(j, carries, kind, slot, wait):
            rows, mask = _sb_rows(kind), _sb_mask(kind)
            if wait:
                for cp in stores(j, slot):
                    cp.wait()
            out = []
            for pp, (cols, neg_q, (c0, c1, acc)) in enumerate(zip(pairs, neg_qs, carries)):
                kst = _pair_rows(k_ref, j, cols, m0b)
                vst = _pair_rows(v_ref, j, cols, m0b)
                nz, lb = _sb_logits(neg_q[rows], kst)
                if mask is not None:
                    lb = jnp.where(mask, lb, 0.0)
                incl = _scan2(lb, u)
                w = jnp.exp(incl - nz if kind == "B" else incl + _bcast2(c0, c1) - nz)
                if mask is not None:
                    w = jnp.where(mask, w, 0.0)
                wb = w.astype(BF16)
                wbuf[slot, pp, rows, :] = wb
                obuf[slot, pp, rows, :] = lb.astype(BF16)
                if kind == "B":
                    wbuf[slot, pp, 0:SB_T, :] = jnp.zeros((SB_T, 2 * SB_T), BF16)
                    obuf[slot, pp, 0:SB_T, :] = jnp.zeros((SB_T, 2 * SB_T), BF16)
                d0, d1, da = incl[:, 0:1], incl[:, SB_T:SB_T + 1], _dot(wb, vst)
                if kind == "B":
                    d0, d1, da = _lower(d0), _lower(d1), _lower(da)
                out.append((c0 + d0, c1 + d1, acc + da))
            for cp in stores(j, slot):
                cp.start()
            return tuple(out)

        zc = jnp.zeros((SB_TQ, 1), F32)
        carries = tile(2 * i + 1, ((zc, zc, jnp.zeros((SB_TQ, 128), F32)),) * SB_NP, "B", 0, False)
        carries = tile(2 * i, carries, "A", 1, False)

        def two(jj, cr):
            cr = tile(2 * i - 1 - 2 * jj, cr, "full", 0, True)
            return tile(2 * i - 2 - 2 * jj, cr, "full", 1, True)

        carries = lax.fori_loop(0, i, two, carries)
        for cols, carry in zip(pairs, carries):
            o_ref[:, cols] = carry[2].astype(BF16)
        for slot in range(2):
            for cp in stores(0, slot):
                cp.wait()

    wide = 128 * SB_NP
    blk = lambda b, g, i: (b * nq + i, g)
    full = lambda b, g, i: (b, g)
    hbm = pl.BlockSpec(memory_space=pl.ANY)
    tiles = jax.ShapeDtypeStruct((nbatch * nq * nk, SB_NP, SB_TQ, 2 * SB_T), BF16)
    return pl.pallas_call(
        body, name="sb_fwd", grid=(nbatch, 4 // SB_NP, nq),
        in_specs=[pl.BlockSpec((SB_TQ, wide), blk), pl.BlockSpec((seq, wide), full), pl.BlockSpec((seq, wide), full),
                  _const_spec((2 * SB_T, SB_T))],
        out_specs=[pl.BlockSpec((SB_TQ, wide), blk), hbm, hbm],
        out_shape=[jax.ShapeDtypeStruct((T, 512), BF16), tiles, tiles],
        scratch_shapes=[pltpu.VMEM((2, SB_NP, SB_TQ, 2 * SB_T), BF16), pltpu.VMEM((2, SB_NP, SB_TQ, 2 * SB_T), BF16),
                        pltpu.SemaphoreType.DMA((2, 2))],
        compiler_params=_params(3),
    )(qb, kb, vb, ublk)


def _sb_bwd(qb, kb, vb, dob, wst, ost, pblk, cout4, nbatch, seq):
    nq, nk = seq // SB_TQ, seq // SB_T
    T = nbatch * seq
    ng = 4 // SB_NP

    def body(q_ref, k_ref, v_ref, do_ref, wst_ref, ost_ref, up_ref, cout_ref, dq_ref, dk_ref, dv_ref, rout_ref,
             wbuf, obuf, sems, ssem, rsem):
        b, g, i = pl.program_id(0), pl.program_id(1), pl.program_id(2)
        copies = lambda: _chip_exchange(cout_ref, rout_ref, (ssem, rsem))

        @pl.when((b == 0) & (g == 0) & (i == 0))
        def _():
            for cp in copies():
                cp.start()

        @pl.when(i == 0)
        def _():
            dk_ref[...] = jnp.zeros_like(dk_ref)
            dv_ref[...] = jnp.zeros_like(dv_ref)

        m0b = _lane_mask(SB_T, BF16)
        m0 = _lane_mask(SB_T) > 0.5
        up = up_ref[...]
        pairs = [slice(pp * 128, (pp + 1) * 128) for pp in range(SB_NP)]

        def loads(j, slot):
            tix = (b * nq + i) * nk + j
            return (pltpu.make_async_copy(wst_ref.at[tix], wbuf.at[slot], sems.at[0, slot]),
                    pltpu.make_async_copy(ost_ref.at[tix], obuf.at[slot], sems.at[1, slot]))

        def tile(j, carries, kind, slot, fetch_next):
            rows, mask = _sb_rows(kind), _sb_mask(kind)
            if fetch_next:
                for cp in loads(j + 1, 1 - slot):
                    cp.start()
            for cp in loads(j, slot):
                cp.wait()
            out = []
            for pp, (cols, (s0, s1, dq)) in enumerate(zip(pairs, carries)):
                q, do = q_ref[rows, cols], do_ref[rows, cols]
                kst = _pair_rows(k_ref, j, cols, m0b)
                vst = _pair_rows(v_ref, j, cols, m0b)
                wb = wbuf[slot, pp, rows, :]
                e = _dot_nt(do, vst) * wb.astype(F32)
                dlb = _bcast2(s0[rows], s1[rows]) + _scan1(e, up)
                dz = (e + dlb) * jnp.exp(obuf[slot, pp, rows, :].astype(F32)) - dlb
                if mask is not None:
                    dz = jnp.where(mask, dz, 0.0)
                dzb = dz.astype(BF16)
                dkst = _dot_tn(dzb, q)
                dvst = _dot_tn(wb, do)
                keys = pl.ds(pl.multiple_of(j * SB_T, SB_T), SB_T)
                dk_ref[keys, cols] = dk_ref[keys, cols] + jnp.where(m0, dkst[0:SB_T], dkst[SB_T:2 * SB_T])
                dv_ref[keys, cols] = dv_ref[keys, cols] + jnp.where(m0, dvst[0:SB_T], dvst[SB_T:2 * SB_T])
                x0, x1 = _rowsum2(e)
                ddq = _dot(dzb, kst)
                if kind == "B":
                    x0, x1, ddq = _lower(x0), _lower(x1), _lower(ddq)
                out.append((s0 + x0, s1 + x1, dq + ddq))
            return tuple(out)

        for cp in loads(0, 0):
            cp.start()

        def two(jj, cr):
            cr = tile(2 * jj, cr, "full", 0, True)
            return tile(2 * jj + 1, cr, "full", 1, True)

        zc = jnp.zeros((SB_TQ, 1), F32)
        carries = lax.fori_loop(0, i, two, ((zc, zc, jnp.zeros((SB_TQ, 128), F32)),) * SB_NP)
        carries = tile(2 * i, carries, "A", 0, True)
        carries = tile(2 * i + 1, carries, "B", 1, False)
        for cols, carry in zip(pairs, carries):
            dq_ref[:, cols] = carry[2] * SCALE

        @pl.when((b == nbatch - 1) & (g == ng - 1) & (i == nq - 1))
        def _():
            for cp in copies():
                cp.wait_recv()
            for cp in copies():
                cp.wait_send()

    wide = 128 * SB_NP
    blk = lambda b, g, i: (b * nq + i, g)
    full = lambda b, g, i: (b, g)
    hbm = pl.BlockSpec(memory_space=pl.ANY)
    return pl.pallas_call(
        body, name="sb_bwd", grid=(nbatch, ng, nq),
        in_specs=[pl.BlockSpec((SB_TQ, wide), blk), pl.BlockSpec((seq, wide), full), pl.BlockSpec((seq, wide), full),
                  pl.BlockSpec((SB_TQ, wide), blk), hbm, hbm, _const_spec((SB_T, SB_T)), hbm],
        out_specs=[pl.BlockSpec((SB_TQ, wide), blk), pl.BlockSpec((seq, wide), full), pl.BlockSpec((seq, wide), full),
                   hbm],
        out_shape=[jax.ShapeDtypeStruct((T, 512), F32)] * 3 + [jax.ShapeDtypeStruct((3,) + cout4.shape[1:], cout4.dtype)],
        scratch_shapes=[pltpu.VMEM((2, SB_NP, SB_TQ, 2 * SB_T), BF16), pltpu.VMEM((2, SB_NP, SB_TQ, 2 * SB_T), BF16),
                        pltpu.SemaphoreType.DMA((2, 2)), pltpu.SemaphoreType.DMA((3,)), pltpu.SemaphoreType.DMA((3,))],
        compiler_params=_params(3),
    )(qb, kb, vb, dob, wst, ost, pblk, cout4)


def _sigmoid(g):
    return 1.0 / (1.0 + jnp.exp(-g))


def _out_proj(oa, ob, ga, gb, x2, tgt, wout_bf):
    T = x2.shape[0]

    def body(oa_ref, ob_ref, ga_ref, gb_ref, x_ref, t_ref, w_ref,
             dout_ref, doa_ref, dob_ref, dga_ref, dgb_ref, dw_ref, loss_ref):
        @pl.when(pl.program_id(0) == 0)
        def _():
            loss_ref[...] = jnp.zeros_like(loss_ref)
            dw_ref[...] = jnp.zeros_like(dw_ref)

        halves = ((oa_ref, ga_ref, doa_ref, dga_ref, 0), (ob_ref, gb_ref, dob_ref, dgb_ref, 512))
        out = x_ref[...]
        gated = []
        for o_ref, g_ref, _, _, lo in halves:
            g = g_ref[...]
            sg = _sigmoid(g)
            y = (o_ref[...] * (g * sg)).astype(BF16)
            out = out + _dot(y, w_ref[lo:lo + 512, :])
            gated.append((y, g, sg))
        diff = out - t_ref[...]
        dout = diff * (1.0 / D_MODEL)
        dout_ref[...] = dout
        loss_ref[...] = loss_ref[...] + jnp.sum(diff * diff) * (0.5 / D_MODEL)
        db = dout.astype(BF16)
        for (o_ref, _, do_ref, dg_ref, lo), (y, g, sg) in zip(halves, gated):
            dw_ref[lo:lo + 512, :] = dw_ref[lo:lo + 512, :] + _dot_tn(y, db)
            dy = _dot_nt(db, w_ref[lo:lo + 512, :])
            do_ref[...] = (dy * (g * sg)).astype(BF16)
            dg_ref[...] = (dy * o_ref[...] * (sg * (1.0 + g * (1.0 - sg)))).astype(BF16)

    rows = lambda w: pl.BlockSpec((TM, w), lambda i: (i, 0))
    return pl.pallas_call(
        body, name="out_proj", grid=(T // TM,),
        in_specs=[rows(512), rows(512), rows(512), rows(512), rows(D_MODEL), rows(D_MODEL),
                  _const_spec((D_MODEL, D_MODEL))],
        out_specs=[rows(D_MODEL)] + [rows(512)] * 4 + [_const_spec((D_MODEL, D_MODEL)), _const_spec((8, 128))],
        out_shape=[jax.ShapeDtypeStruct((T, D_MODEL), F32)] + [jax.ShapeDtypeStruct((T, 512), BF16)] * 4
        + [jax.ShapeDtypeStruct((D_MODEL, D_MODEL), F32), jax.ShapeDtypeStruct((8, 128), F32)],
        compiler_params=_params(1),
    )(oa, ob, ga, gb, x2, tgt, wout_bf)


def _qk_grad(qraw, kraw, dqrot, dkpad, dvpad, qg512, kg128, g512, cos, sa, sb, seq):
    T = qraw.shape[0]
    assert TM == SWA_Q
    ni = seq // SWA_Q

    def body(qraw_ref, kraw_ref, dqrot_ref, dk0, dk1, dk2, dk3, dv0, dv1, dv2, dv3,
             qg_ref, kg_ref, g_ref, cos_ref, sa_ref, sb_ref, dqa_ref, dkv_ref, dqg_ref, dkg_ref):
        @pl.when(pl.program_id(0) == 0)
        def _():
            dqg_ref[...] = jnp.zeros_like(dqg_ref)
            dkg_ref[...] = jnp.zeros_like(dkg_ref)

        cosv, sav, sbv = cos_ref[...], sa_ref[...], sb_ref[...]
        m0 = _lane_mask(TM) > 0.5

        def head_norm_bwd(raw, dn_rot, gmat, gain):
            r = lax.rsqrt(_dot_b(raw * raw, gmat) + EPS)
            n = raw * r
            dn = dn_rot * gain
            return r * (dn - n * _dot_b(dn * n, gmat)), jnp.sum(dn_rot * n, axis=0, keepdims=True)

        def fold(refs):
            a = jnp.concatenate([r[:, 0:128] for r in refs], axis=0)
            b = jnp.concatenate([r[:, 128:256] for r in refs], axis=0)
            return jnp.where(m0, a + pltpu.roll(a, 64, 1), b + pltpu.roll(b, 64, 1))

        dqn = jnp.concatenate([_rope_bwd(dqrot_ref[:, s * 128:(s + 1) * 128], cosv, sav, sbv) for s in range(4)], axis=1)
        dqa, dqg = head_norm_bwd(qraw_ref[...], dqn, g_ref[...], qg_ref[...])
        dka, dkg = head_norm_bwd(kraw_ref[...], _rope_bwd(fold((dk0, dk1, dk2, dk3)), cosv, sav, sbv),
                                 g_ref[0:128, 0:128], kg_ref[...])
        dqa_ref[...] = dqa.astype(BF16)
        dkv_ref[:, 0:128] = dka.astype(BF16)
        dkv_ref[:, 128:256] = fold((dv0, dv1, dv2, dv3)).astype(BF16)
        dqg_ref[...] = dqg_ref[...] + dqg
        dkg_ref[...] = dkg_ref[...] + dkg

    rows = lambda w: pl.BlockSpec((TM, w), lambda i: (i, 0))
    sub = SWA_Q // BLOCK
    shifted = [pl.BlockSpec((BLOCK, 256), functools.partial(
        lambda i, j: ((i // ni) * (ni + 1) * sub + (i % ni) * sub + 1 + j, 0), j=j)) for j in range(sub)]
    return pl.pallas_call(
        body, name="qk_grad", grid=(T // TM,),
        in_specs=[rows(512), rows(128), rows(512)] + shifted + shifted + [
            _const_spec((1, 512)), _const_spec((1, 128)), _const_spec((512, 512)), rows(128), rows(128), rows(128)],
        out_specs=[rows(512), rows(256), _const_spec((1, 512)), _const_spec((1, 128))],
        out_shape=[jax.ShapeDtypeStruct((T, 512), BF16), jax.ShapeDtypeStruct((T, 256), BF16),
                   jax.ShapeDtypeStruct((1, 512), F32), jax.ShapeDtypeStruct((1, 128), F32)],
        compiler_params=_params(1),
    )(qraw, kraw, dqrot, *([dkpad] * sub), *([dvpad] * sub), qg512, kg128, g512, cos, sa, sb)


_PIECES = ((QA, 512), (KA, 256), (GA, 512), (QB, 512), (KB, 512), (VB, 512), (GB, 512))


def _w_in_grad(h, pieces):
    T = h.shape[0]

    def body(h_ref, *refs):
        dw_ref = refs[-1]

        @pl.when(pl.program_id(0) == 0)
        def _():
            dw_ref[...] = jnp.zeros_like(dw_ref)

        hb = h_ref[...]
        for (lo, width), p_ref in zip(_PIECES, refs[:-1]):
            dw_ref[lo:lo + width, :] = dw_ref[lo:lo + width, :] + _dot_tn(p_ref[...].astype(BF16), hb)

    rows = lambda w: pl.BlockSpec((TM, w), lambda i: (i, 0))
    return pl.pallas_call(
        body, name="w_in_grad", grid=(T // TM,),
        in_specs=[rows(D_MODEL)] + [rows(w) for _, w in _PIECES],
        out_specs=_const_spec((IN_WIDTH, D_MODEL)),
        out_shape=jax.ShapeDtypeStruct((IN_WIDTH, D_MODEL), F32),
        compiler_params=_params(1),
    )(h, *pieces)


def _chip_exchange(src_ref, dst_ref, sems):
    _, _, c, chips = _place()
    return [_remote(src_ref.at[2 * cx + cy], dst_ref.at[j], sems, j, (cx, cy, c)) for j, (cx, cy) in enumerate(chips)]


def _x_grad(x2, dout, pieces, w_bf, ng, cin4):
    T = x2.shape[0]
    npc = len(_PIECES)
    steps = T // TM

    def body(x_ref, dout_ref, *refs):
        w_ref, ng_ref, cin_ref, gx_ref, dng_ref, rin_ref, ssem, rsem = refs[npc:]
        step = pl.program_id(0)
        copies = lambda: _chip_exchange(cin_ref, rin_ref, (ssem, rsem))

        @pl.when(step == 0)
        def _():
            dng_ref[...] = jnp.zeros_like(dng_ref)
            for cp in copies():
                cp.start()

        dh = jnp.zeros((TM, D_MODEL), F32)
        for (lo, width), p_ref in zip(_PIECES, refs[:npc]):
            dh = dh + _dot(p_ref[...].astype(BF16), w_ref[lo:lo + width, :])
        xb = x_ref[...]
        r = lax.rsqrt(jnp.mean(xb * xb, axis=-1, keepdims=True) + EPS)
        n = xb * r
        dn = dh * ng_ref[...]
        gx_ref[...] = dout_ref[...] + r * (dn - n * jnp.mean(dn * n, axis=-1, keepdims=True))
        dng_ref[...] = dng_ref[...] + jnp.sum(dh * n, axis=0, keepdims=True)

        @pl.when(step == steps - 1)
        def _():
            for cp in copies():
                cp.wait_recv()
            for cp in copies():
                cp.wait_send()

    rows = lambda w: pl.BlockSpec((TM, w), lambda i: (i, 0))
    hbm = pl.BlockSpec(memory_space=pl.ANY)
    return pl.pallas_call(
        body, name="x_grad", grid=(steps,),
        in_specs=[rows(D_MODEL), rows(D_MODEL)] + [rows(w) for _, w in _PIECES]
        + [_const_spec((IN_WIDTH, D_MODEL)), _const_spec((1, D_MODEL)), hbm],
        out_specs=[rows(D_MODEL), _const_spec((1, D_MODEL)), hbm],
        out_shape=[jax.ShapeDtypeStruct((T, D_MODEL), F32), jax.ShapeDtypeStruct((1, D_MODEL), F32),
                   jax.ShapeDtypeStruct((3,) + cin4.shape[1:], cin4.dtype)],
        scratch_shapes=[pltpu.SemaphoreType.DMA((3,)), pltpu.SemaphoreType.DMA((3,))],
        compiler_params=_params(1),
    )(x2, dout, *pieces, w_bf, ng, cin4)


HBM = pl.BlockSpec(memory_space=pl.ANY)


def _place():
    x, y, c = lax.axis_index("x"), lax.axis_index("y"), lax.axis_index("c")
    chips = [(1 - x, y), (x, 1 - y), (1 - x, 1 - y)]
    return x, y, c, chips


def _remote(src, dst, sems, k, to):
    return pltpu.make_async_remote_copy(src_ref=src, dst_ref=dst, send_sem=sems[0].at[k], recv_sem=sems[1].at[k],
                                        device_id=to, device_id_type=MESH)


def _gather_plan(src, dst, sems, relay):
    x, y, c, _ = _place()
    rows = src.shape[1] // 2
    parts = (pl.ds(0, rows), pl.ds(rows, rows))
    me, kx, ky, kd = 2 * x + y, 2 * (1 - x) + y, 2 * x + 1 - y, 2 * (1 - x) + 1 - y
    to_x, to_y, to_d, sib = (1 - x, y, c), (x, 1 - y, c), (1 - x, 1 - y, c), (x, y, 1 - c)
    mine = [(src.at[c, p], dst.at[me, c, p]) for p in parts]
    direct = [_remote(*mine[0], sems, 0, to_x), _remote(*mine[1], sems, 1, to_x),
              _remote(*mine[1], sems, 2, to_y), _remote(*mine[0], sems, 3, to_y)]
    arrived = [dst.at[kx, c, parts[0]], dst.at[kx, c, parts[1]], dst.at[ky, c, parts[1]], dst.at[ky, c, parts[0]]]
    if relay:
        relays = [_remote(arrived[0], arrived[0], sems, 4, to_y), _remote(arrived[2], arrived[2], sems, 5, to_x)]
    else:
        relays = []
        direct += [_remote(*mine[0], sems, 4, to_d), _remote(*mine[1], sems, 5, to_d)]
    relayed = [dst.at[kd, c, parts[0]], dst.at[kd, c, parts[1]]]
    forwards = [_remote(dst.at[k, c], dst.at[k, c], sems, 6 + n, sib) for n, k in enumerate((kx, ky, kd))]
    from_sib = [dst.at[k, 1 - c] for k in (kx, ky, kd)]
    return direct, arrived, relays, relayed, forwards, from_sib


def _gather_start(src, dst, sems, relay):
    direct = _gather_plan(src, dst, sems, relay)[0]
    for k in (0, 2, 1, 3):
        direct[k].start()
    for cp in direct[4:]:
        cp.start()


def _gather_finish(src, dst, sems, relay):
    direct, arrived, relays, relayed, forwards, from_sib = _gather_plan(src, dst, sems, relay)
    landed = lambda ref, k: _remote(ref, ref, sems, k, (0, 0, 0)).wait_recv()
    landed(arrived[0], 0)
    if relay:
        relays[0].start()
    landed(arrived[2], 2)
    if relay:
        relays[1].start()
    landed(arrived[1], 1)
    forwards[0].start()
    landed(arrived[3], 3)
    forwards[1].start()
    landed(relayed[0], 4)
    landed(relayed[1], 5)
    forwards[2].start()
    for n, ref in enumerate(from_sib):
        landed(ref, 6 + n)
    for cp in direct + relays + forwards:
        cp.wait_send()


def _own_slot(gathered, shard):
    me = 2 * lax.axis_index("x") + lax.axis_index("y")
    return lax.dynamic_update_slice(gathered, shard[None], (me, 0, 0, 0))


def _pair_copy(src, dst, sems):
    x, y, c, _ = _place()
    return _remote(src.at[:, pl.ds(1 - c, 1)], dst, sems, 0, (x, y, 1 - c))


def _pair_exchange(g4):
    def body(src, dst, ssem, rsem):
        cp = _pair_copy(src, dst, (ssem, rsem))
        cp.start()
        cp.wait()

    return pl.pallas_call(
        body, name=f"pair_exchange_{g4.shape[2]}", in_specs=[HBM], out_specs=HBM,
        out_shape=jax.ShapeDtypeStruct((4, 1) + g4.shape[2:], g4.dtype),
        scratch_shapes=[pltpu.SemaphoreType.DMA((1,)), pltpu.SemaphoreType.DMA((1,))],
    )(g4)


def _share_halves(hin, hout, small):
    def body(hin_ref, hout_ref, small_ref, oin_ref, oout_ref, sall_ref, ssem, rsem, lsem):
        x, y, c, _ = _place()
        sems = (ssem, rsem)
        sib = (x, y, 1 - c)
        me = 4 * x + 2 * y + c
        own = pltpu.make_async_copy(small_ref, sall_ref.at[me], lsem.at[0])
        own.start()
        sent = [_remote(hin_ref, oin_ref, sems, 0, sib), _remote(hout_ref, oout_ref, sems, 1, sib)]
        flips = [(fx, fy, fc) for fx in (0, 1) for fy in (0, 1) for fc in (0, 1)][1:]
        for k, (fx, fy, fc) in enumerate(flips):
            sent.append(_remote(small_ref, sall_ref.at[me], sems, 2 + k, (x ^ fx, y ^ fy, c ^ fc)))
        for cp in sent:
            cp.start()
        _remote(hin_ref, oin_ref, sems, 0, sib).wait_recv()
        _remote(hout_ref, oout_ref, sems, 1, sib).wait_recv()
        for k, (fx, fy, fc) in enumerate(flips):
            src = 4 * (x ^ fx) + 2 * (y ^ fy) + (c ^ fc)
            _remote(small_ref, sall_ref.at[src], sems, 2 + k, sib).wait_recv()
        for cp in sent:
            cp.wait_send()
        own.wait()

    return pl.pallas_call(
        body, name="share_halves", in_specs=[HBM, HBM, HBM], out_specs=[HBM, HBM, HBM],
        out_shape=[jax.ShapeDtypeStruct(hin.shape, F32), jax.ShapeDtypeStruct(hout.shape, F32),
                   jax.ShapeDtypeStruct((8,) + small.shape, F32)],
        scratch_shapes=[pltpu.SemaphoreType.DMA((9,)), pltpu.SemaphoreType.DMA((9,)), pltpu.SemaphoreType.DMA((1,))],
    )(hin, hout, small)


def _add_half(cidx, full4, recv4):
    _, _, rows, width = full4.shape

    def body(c_ref, a_ref, b_ref, o_ref):
        o_ref[0] = (a_ref[0, 0] + b_ref[0, 0]).astype(BF16)

    return pl.pallas_call(
        body, name=f"add_half_{rows}",
        grid_spec=pltpu.PrefetchScalarGridSpec(
            num_scalar_prefetch=1, grid=(4,),
            in_specs=[pl.BlockSpec((1, 1, rows, width), lambda k, c: (k, c[0], 0, 0)),
                      pl.BlockSpec((1, 1, rows, width), lambda k, c: (k, 0, 0, 0))],
            out_specs=pl.BlockSpec((1, rows, width), lambda k, c: (k, 0, 0))),
        out_shape=jax.ShapeDtypeStruct((4, rows, width), BF16),
        compiler_params=_params(1),
    )(cidx, full4, recv4)


def _sum_chips(chip, own4, recv3):
    _, rows, width = recv3.shape
    rb = rows // 2

    def body(k_ref, a_ref, r_ref, o_ref):
        acc = a_ref[0].astype(F32)
        for s in range(3):
            acc = acc + r_ref[s].astype(F32)
        o_ref[...] = acc

    return pl.pallas_call(
        body, name=f"sum_chips_{rows}",
        grid_spec=pltpu.PrefetchScalarGridSpec(
            num_scalar_prefetch=1, grid=(rows // rb,),
            in_specs=[pl.BlockSpec((1, rb, width), lambda i, k: (k[0], i, 0)),
                      pl.BlockSpec((3, rb, width), lambda i, k: (0, i, 0))],
            out_specs=pl.BlockSpec((rb, width), lambda i, k: (i, 0))),
        out_shape=jax.ShapeDtypeStruct((rows, width), F32),
        compiler_params=_params(1),
    )(chip, own4, recv3)


def _adam_math(w, g, m, v):
    c1 = 1.0 - ADAM_B1 ** ADAM_STEP
    c2 = 1.0 - ADAM_B2 ** ADAM_STEP
    nm = ADAM_B1 * m + (1.0 - ADAM_B1) * g
    nv = ADAM_B2 * v + (1.0 - ADAM_B2) * (g * g)
    return -ADAM_LR * ((nm / c1) / (jnp.sqrt(nv / c2) + ADAM_EPS) + ADAM_WD * w), nm, nv


def _small_update(small_all, params):
    n = len(params)
    flat = [a for p in params for a in p]

    def body(s_ref, *refs):
        ins, loss_ref, outs = refs[:3 * n], refs[3 * n], refs[3 * n + 1:]
        total = s_ref[0]
        for d in range(1, 8):
            total = total + s_ref[d]
        loss_ref[...] = total[n:n + 1, 0:1]
        for r in range(n):
            w_ref, m_ref, v_ref = ins[3 * r:3 * r + 3]
            g = total[r:r + 1, 0:w_ref.shape[1]]
            outs[4 * r][...] = g
            outs[4 * r + 1][...], outs[4 * r + 2][...], outs[4 * r + 3][...] = _adam_math(w_ref[...], g, m_ref[...], v_ref[...])

    whole = lambda a: pl.BlockSpec(a.shape, lambda i: (0,) * a.ndim)
    out_shape = [jax.ShapeDtypeStruct((1, 1), F32)] + [jax.ShapeDtypeStruct(p[0].shape, F32) for p in params for _ in range(4)]
    res = pl.pallas_call(
        body, name="small_update", grid=(1,),
        in_specs=[whole(small_all)] + [whole(a) for a in flat], out_specs=[whole(s) for s in out_shape],
        out_shape=out_shape, compiler_params=_params(1),
    )(small_all, *flat)
    return res[0], [tuple(res[1 + 4 * r:5 + 4 * r]) for r in range(n)]


def _adamw_halves(cidx, w, own, recv, m, v):
    rows, width = w.shape
    rb = rows // 4

    def body(c_ref, w_ref, own_ref, recv_ref, m_ref, v_ref, g_ref, d_ref, nm_ref, nv_ref):
        mine = (pl.program_id(0) // 2) == c_ref[0]
        g = jnp.where(mine, own_ref[...], recv_ref[...])
        g_ref[...] = g
        d_ref[...], nm_ref[...], nv_ref[...] = _adam_math(w_ref[...], g, m_ref[...], v_ref[...])

    full = pl.BlockSpec((rb, width), lambda i, c: (i, 0))
    half = pl.BlockSpec((rb, width), lambda i, c: (i % 2, 0))
    return pl.pallas_call(
        body, name=f"adamw_halves_{rows}",
        grid_spec=pltpu.PrefetchScalarGridSpec(
            num_scalar_prefetch=1, grid=(4,),
            in_specs=[full, half, half, full, full], out_specs=[full] * 4),
        out_shape=[jax.ShapeDtypeStruct((rows, width), F32)] * 4,
        compiler_params=_params(1),
    )(cidx, w, own, recv, m, v)


def _constants():
    idx = jnp.arange(512)
    g512 = jnp.where(idx[:, None] // HEAD_DIM == idx[None, :] // HEAD_DIM, 1.0 / HEAD_DIM, 0.0).astype(BF16)
    j = jnp.arange(SB_T)
    ublk = jnp.where(j[:, None] >= j[None, :], 1.0, 0.0).astype(BF16)
    pblk = jnp.where(j[:, None] < j[None, :], 1.0, 0.0).astype(BF16)
    return g512, jnp.concatenate([ublk, ublk], axis=0), pblk


def _pad_rows(v, width):
    return jnp.pad(v, ((0, 0), (0, width - v.shape[1])))


def _pair_sum(cidx, partial, rows):
    g4 = partial.reshape(4, 2, rows, D_MODEL)
    return _add_half(cidx, g4, _pair_exchange(g4))


def _step(x2, tgt, positions, norm_gain, q_norm_gain, k_norm_gain, sinks, win_shard, wout_shard, nbatch, seq):
    g512, ublk, pblk = _constants()
    qg512 = jnp.tile(q_norm_gain, (1, 8))
    kg128 = jnp.tile(k_norm_gain, (1, 2))
    sink1 = sinks.reshape(8)
    cidx = lax.axis_index("c").astype(jnp.int32).reshape(1)
    chip = (2 * lax.axis_index("x") + lax.axis_index("y")).astype(jnp.int32).reshape(1)

    h, (cos, sa, sb), w_bf = _prologue(x2, norm_gain, positions.reshape(-1, 1), win_shard)
    (qraw, kraw, qrot, k2, v2, ga, qb, kb, vb, gb), wout_bf = _norm_proj(
        h, w_bf, qg512, kg128, g512, cos, sa, sb, wout_shard)
    oa = _swa_fwd(sink1, qrot, k2, v2, nbatch, seq)
    ob, wst, ost = _sb_fwd(qb, kb, vb, ublk, nbatch, seq)
    dout, doa, dob, dga, dgb, dwout, loss_acc = _out_proj(oa, ob, ga, gb, x2, tgt, wout_bf)

    gout4 = dwout.reshape(4, 2, SHARD_OUT // 2, D_MODEL)
    dqrot, dk2, dv2, dsink, pout4 = _swa_bwd(sink1, qrot, k2, v2, doa, gout4, nbatch, seq)
    cout4 = _add_half(cidx, gout4, pout4)
    dqb, dkb, dvb, rout3 = _sb_bwd(qb, kb, vb, dob, wst, ost, pblk, cout4, nbatch, seq)
    dqa, dkv, dqg, dkg = _qk_grad(qraw, kraw, dqrot, dk2, dv2, qg512, kg128, g512, cos, sa, sb, seq)
    pieces = (dqa, dkv, dga, dqb, dkb, dvb, dgb)
    cin4 = _pair_sum(cidx, _w_in_grad(h, pieces), SHARD_IN // 2)
    gx, dng, rin3 = _x_grad(x2, dout, pieces, w_bf, norm_gain, cin4)
    own_in, own_out = _sum_chips(chip, cin4, rin3), _sum_chips(chip, cout4, rout3)

    dqg64 = dqg.reshape(8, HEAD_DIM).sum(axis=0, keepdims=True)
    dkg64 = dkg.reshape(2, HEAD_DIM).sum(axis=0, keepdims=True)
    small = jnp.concatenate([dng, _pad_rows(dqg64, D_MODEL), _pad_rows(dkg64, D_MODEL),
                             _pad_rows(dsink[:, 0].reshape(1, 8), D_MODEL), _pad_rows(loss_acc[0:1, 0:1], D_MODEL),
                             jnp.zeros((3, D_MODEL), F32)], axis=0)
    sib_in, sib_out, small_all = _share_halves(own_in, own_out, small)
    return gx, cidx, (own_in, sib_in), (own_out, sib_out), small_all


def kernel(x, positions, norm_gain, w_in, q_norm_gain, k_norm_gain, sinks, w_out, loss_target, m_norm_gain, m_w_in, m_q_norm_gain, m_k_norm_gain, m_sinks, m_w_out, v_norm_gain, v_w_in, v_q_norm_gain, v_k_norm_gain, v_sinks, v_w_out):
    nbatch, seq, _ = x.shape
    T = nbatch * seq
    x2 = x.reshape(T, D_MODEL)
    tgt = loss_target.reshape(T, D_MODEL)
    tr = lambda a: jnp.swapaxes(a[0], 0, 1)
    win_t, m_win_t, v_win_t = tr(w_in), tr(m_w_in), tr(v_w_in)

    win_shard = win_t.astype(BF16).reshape(2, SHARD_IN // 2, D_MODEL)
    wout_shard = w_out[0].astype(BF16).reshape(2, SHARD_OUT // 2, D_MODEL)
    gx, cidx, g_in_halves, g_out_halves, small_all = _step(
        x2, tgt, positions, norm_gain, q_norm_gain, k_norm_gain, sinks, win_shard, wout_shard, nbatch, seq)

    g_in, d_in, nm_in, nv_in = [jnp.swapaxes(a, 0, 1) for a in
                                _adamw_halves(cidx, win_t, *g_in_halves, m_win_t, v_win_t)]
    g_out, d_out, nm_out, nv_out = _adamw_halves(cidx, w_out[0], *g_out_halves, m_w_out[0], v_w_out[0])
    loss, small = _small_update(small_all, [(norm_gain, m_norm_gain, v_norm_gain), (q_norm_gain, m_q_norm_gain, v_q_norm_gain),
                                            (k_norm_gain, m_k_norm_gain, v_k_norm_gain), (sinks, m_sinks, v_sinks)])
    (g_ng, d_ng, m_ng, v_ng), (g_qg, d_qg, m_qg, v_qg), (g_kg, d_kg, m_kg, v_kg), (g_sk, d_sk, m_sk, v_sk) = small
    return (loss.reshape(()), gx.reshape(nbatch, seq, D_MODEL),
            g_ng, g_in[None], g_qg, g_kg, g_sk, g_out[None],
            d_ng, d_in[None], d_qg, d_kg, d_sk, d_out[None],
            m_ng, nm_in[None], m_qg, m_kg, m_sk, nm_out[None],
            v_ng, nv_in[None], v_qg, v_kg, v_sk, nv_out[None])
```

```python
import functools
import math

import jax
import jax.numpy as jnp
from jax import lax
from jax.experimental import pallas as pl
from jax.experimental.pallas import tpu as pltpu

F32 = jnp.float32
BF16 = jnp.bfloat16

D_MODEL = 1024
HEAD_DIM = 64
BLOCK = 128
ROPE_THETA = 10000.0
EPS = 1e-6
QA, KA, VA, GA, QB, KB, VB, GB = 0, 512, 640, 768, 1280, 1792, 2304, 2816
IN_WIDTH = 3328
SHARD_IN = IN_WIDTH // 4
SHARD_OUT = D_MODEL // 4
SCALE = 1.0 / math.sqrt(HEAD_DIM)
NEG = -1e30

ADAM_LR, ADAM_B1, ADAM_B2, ADAM_EPS, ADAM_WD, ADAM_STEP = 0.001, 0.9, 0.999, 1e-08, 0.01, 10

TM = 512
VMEM_LIMIT = 56 * 1024 * 1024
MESH = pl.DeviceIdType.MESH


def _dot(a, b):
    return jnp.dot(a, b, preferred_element_type=F32)


def _dot_nt(a, b):
    return lax.dot_general(a, b, (((1,), (1,)), ((), ())), preferred_element_type=F32)


def _dot_tn(a, b):
    return lax.dot_general(a, b, (((0,), (0,)), ((), ())), preferred_element_type=F32)


def _dot_b(a, m):
    return _dot(a.astype(BF16), m)


def _params(n_axes=None, vmem=VMEM_LIMIT):
    sem = None if n_axes is None else ("arbitrary",) * n_axes
    return pltpu.CompilerParams(dimension_semantics=sem, vmem_limit_bytes=vmem)


def _const_spec(shape):
    nd = len(shape)
    return pl.BlockSpec(shape, lambda *_: (0,) * nd)


def _rope_fwd(x, cos, sa, sb):
    return x * cos + pltpu.roll(x, 96, 1) * sa + pltpu.roll(x, 32, 1) * sb


def _rope_bwd(d, cos, sa, sb):
    return d * cos - pltpu.roll(d, 96, 1) * sa - pltpu.roll(d, 32, 1) * sb


def _lane_mask(rows, dtype=F32):
    lane = lax.broadcasted_iota(jnp.int32, (rows, 128), 1)
    return jnp.where(lane < HEAD_DIM, 1.0, 0.0).astype(dtype)


def _prologue(x2, ng, pos, win_shard):
    T = x2.shape[0]
    steps = T // TM
    half = HEAD_DIM // 2
    inv_freq = jnp.tile(ROPE_THETA ** (-jnp.arange(half, dtype=F32) * 2.0 / HEAD_DIM), 4).reshape(1, 128)

    def body(x_ref, ng_ref, pos_ref, freq_ref, wi_ref, h_ref, cos_ref, sa_ref, sb_ref, ai_ref, ssem, rsem):
        @pl.when(pl.program_id(0) == 0)
        def _():
            _gather_start(wi_ref, ai_ref, (ssem, rsem), relay=True)

        xb = x_ref[...]
        r = lax.rsqrt(jnp.mean(xb * xb, axis=-1, keepdims=True) + EPS)
        h_ref[...] = (xb * r * ng_ref[...]).astype(BF16)
        ang = pos_ref[...].astype(F32) * freq_ref[...]
        first = (lax.broadcasted_iota(jnp.int32, (TM, 128), 1) & (HEAD_DIM - 1)) < half
        sn = jnp.sin(ang)
        cos_ref[...] = jnp.cos(ang)
        sa_ref[...] = jnp.where(first, -sn, 0.0)
        sb_ref[...] = jnp.where(first, 0.0, sn)

        @pl.when(pl.program_id(0) == steps - 1)
        def _():
            _gather_finish(wi_ref, ai_ref, (ssem, rsem), relay=True)

    rows = lambda w: pl.BlockSpec((TM, w), lambda i: (i, 0))
    hbm = pl.BlockSpec(memory_space=pl.ANY)
    h, cos, sa, sb, gathered = pl.pallas_call(
        body, name="prologue", grid=(steps,),
        in_specs=[rows(D_MODEL), _const_spec((1, D_MODEL)), rows(1), _const_spec((1, 128)), hbm],
        out_specs=[rows(D_MODEL), rows(128), rows(128), rows(128), hbm],
        out_shape=[jax.ShapeDtypeStruct((T, D_MODEL), BF16)] + [jax.ShapeDtypeStruct((T, 128), F32)] * 3
        + [jax.ShapeDtypeStruct((4,) + win_shard.shape, win_shard.dtype)],
        scratch_shapes=[pltpu.SemaphoreType.DMA((9,)), pltpu.SemaphoreType.DMA((9,))],
        compiler_params=_params(1),
    )(x2, ng, pos, inv_freq, win_shard)
    return h, (cos, sa, sb), _own_slot(gathered, win_shard).reshape(IN_WIDTH, D_MODEL)


def _norm_proj(h, w_bf, qg512, kg128, g512, cos, sa, sb, wout_shard):
    T = h.shape[0]
    steps = T // TM

    def body(h_ref, w_ref, qg_ref, kg_ref, g_ref, cos_ref, sa_ref, sb_ref, wo_ref,
             qraw_ref, kraw_ref, qrot_ref, k2_ref, v2_ref, ga_ref, qb_ref, kb_ref, vb_ref, gb_ref, ao_ref,
             ssem, rsem):
        @pl.when(pl.program_id(0) == 0)
        def _():
            _gather_start(wo_ref, ao_ref, (ssem, rsem), relay=False)

        h = h_ref[...]
        cosv, sav, sbv = cos_ref[...], sa_ref[...], sb_ref[...]
        m0 = _lane_mask(TM)

        def dup(v):
            v0 = v * m0
            v1 = v - v0
            return v0 + pltpu.roll(v0, 64, 1), v1 + pltpu.roll(v1, 64, 1)

        qa = _dot_nt(h, w_ref[QA:KA, :])
        qraw_ref[...] = qa
        qn = qa * lax.rsqrt(_dot_b(qa * qa, g_ref[...]) + EPS) * qg_ref[...]
        for s in range(4):
            qs = _rope_fwd(qn[:, s * 128:(s + 1) * 128], cosv, sav, sbv)
            qrot_ref[:, s * 128:(s + 1) * 128] = (qs * SCALE).astype(BF16)
        ka = _dot_nt(h, w_ref[KA:VA, :])
        kraw_ref[...] = ka
        kn = ka * lax.rsqrt(_dot_b(ka * ka, g_ref[0:128, 0:128]) + EPS) * kg_ref[...]
        k0, k1 = dup(_rope_fwd(kn, cosv, sav, sbv))
        k2_ref[:, 0:128] = k0.astype(BF16)
        k2_ref[:, 128:256] = k1.astype(BF16)
        v0, v1 = dup(_dot_nt(h, w_ref[VA:GA, :]))
        v2_ref[:, 0:128] = v0.astype(BF16)
        v2_ref[:, 128:256] = v1.astype(BF16)
        ga_ref[...] = _dot_nt(h, w_ref[GA:QB, :])
        qb_ref[...] = (_dot_nt(h, w_ref[QB:KB, :]) * SCALE).astype(BF16)
        kb_ref[...] = _dot_nt(h, w_ref[KB:VB, :]).astype(BF16)
        vb_ref[...] = _dot_nt(h, w_ref[VB:GB, :]).astype(BF16)
        gb_ref[...] = _dot_nt(h, w_ref[GB:IN_WIDTH, :])

        @pl.when(pl.program_id(0) == steps - 1)
        def _():
            _gather_finish(wo_ref, ao_ref, (ssem, rsem), relay=False)

    def rows(w):
        return pl.BlockSpec((TM, w), lambda i: (i, 0))

    outs = [(512, F32), (128, F32), (512, BF16), (256, BF16), (256, BF16), (512, F32),
            (512, BF16), (512, BF16), (512, BF16), (512, F32)]
    hbm = pl.BlockSpec(memory_space=pl.ANY)
    res = pl.pallas_call(
        body, name="norm_proj", grid=(steps,),
        in_specs=[rows(D_MODEL), _const_spec((IN_WIDTH, D_MODEL)), _const_spec((1, 512)),
                  _const_spec((1, 128)), _const_spec((512, 512)), rows(128), rows(128), rows(128), hbm],
        out_specs=[rows(w) for w, _ in outs] + [hbm],
        out_shape=[jax.ShapeDtypeStruct((T, w), dt) for w, dt in outs]
        + [jax.ShapeDtypeStruct((4,) + wout_shard.shape, wout_shard.dtype)],
        scratch_shapes=[pltpu.SemaphoreType.DMA((9,)), pltpu.SemaphoreType.DMA((9,))],
        compiler_params=_params(1),
    )(h, w_bf, qg512, kg128, g512, cos, sa, sb, wout_shard)
    return res[:-1], _own_slot(res[-1], wout_shard).reshape(D_MODEL, D_MODEL)


SWA_Q = 512
SWA_SUB = SWA_Q // BLOCK


def _swa_scores(q, kst, mask, sink_ref, kv):
    s_all = _dot_nt(q, kst)
    first = lax.broadcasted_iota(jnp.int32, (256, 1), 0) < 128
    probs, stats = [], []
    for hh in range(2):
        sink = jnp.where(first, sink_ref[kv * 4 + hh], sink_ref[kv * 4 + 2 + hh])
        s = jnp.where(mask, s_all[:, hh * 256:(hh + 1) * 256], NEG)
        m = jnp.maximum(jnp.max(s, axis=1, keepdims=True), sink)
        pe = jnp.exp(s - m)
        inv = 1.0 / (jnp.sum(pe, axis=1, keepdims=True) + jnp.exp(sink - m))
        probs.append(pe * inv)
        stats.append(jnp.exp(sink - m) * inv)
    return probs, stats


def _swa_mask(has_prev):
    r = lax.broadcasted_iota(jnp.int32, (256, 256), 0) & 127
    c = lax.broadcasted_iota(jnp.int32, (256, 256), 1)
    band = (c > r) & (c <= r + 128)
    return band if has_prev is True else band & ((c >= 128) | has_prev)


def _stack_pairs(ref, rows, kv):
    return jnp.concatenate([ref[rows, (2 * kv) * 128:(2 * kv + 1) * 128], ref[rows, (2 * kv + 1) * 128:(2 * kv + 2) * 128]], axis=0)


def _swa_keys(prev_ref, main_ref, s, kv, m0b):
    cols = slice(kv * 128, (kv + 1) * 128)
    prev = prev_ref[:, cols] if s == 0 else main_ref[(s - 1) * 128:s * 128, cols]
    kc = jnp.concatenate([prev, main_ref[s * 128:(s + 1) * 128, cols]], axis=0)
    k0 = kc * m0b
    return jnp.concatenate([k0, kc - k0], axis=0)


def _swa_fwd(sinks, qrot, k2, v2, nbatch, seq):
    ni = seq // SWA_Q
    T = nbatch * seq

    def body(sink_ref, q_ref, kp_ref, km_ref, vp_ref, vm_ref, o_ref):
        i = pl.program_id(1)
        m0b = _lane_mask(256, BF16)
        for s in range(SWA_SUB):
            mask = _swa_mask(True if s else i > 0)
            rows = slice(s * 128, (s + 1) * 128)
            for kv in range(2):
                kst = _swa_keys(kp_ref, km_ref, s, kv, m0b)
                vst = _swa_keys(vp_ref, vm_ref, s, kv, m0b)
                probs, _ = _swa_scores(_stack_pairs(q_ref, rows, kv), kst, mask, sink_ref, kv)
                o2 = _dot(jnp.concatenate(probs, axis=1).astype(BF16), vst)
                o_ref[rows, kv * 256:kv * 256 + 128] = o2[0:128].astype(BF16)
                o_ref[rows, kv * 256 + 128:(kv + 1) * 256] = o2[128:256].astype(BF16)

    main = lambda b, i: (b * ni + i, 0)
    prev = lambda b, i: ((b * ni + i) * SWA_SUB - jnp.where(i > 0, 1, 0), 0)
    return pl.pallas_call(
        body, name="swa_fwd", grid=(nbatch, ni),
        in_specs=[pl.BlockSpec(memory_space=pltpu.SMEM), pl.BlockSpec((SWA_Q, 512), main),
                  pl.BlockSpec((128, 256), prev), pl.BlockSpec((SWA_Q, 256), main),
                  pl.BlockSpec((128, 256), prev), pl.BlockSpec((SWA_Q, 256), main)],
        out_specs=pl.BlockSpec((SWA_Q, 512), main),
        out_shape=jax.ShapeDtypeStruct((T, 512), BF16),
        compiler_params=_params(2),
    )(sinks, qrot, k2, k2, v2, v2)


def _swa_bwd(sinks, qrot, k2, v2, doa, gout4, nbatch, seq):
    ni = seq // SWA_Q
    T = nbatch * seq

    def body(sink_ref, q_ref, kp_ref, km_ref, vp_ref, vm_ref, do_ref, gout_ref,
             dq_ref, dk_ref, dv_ref, ds_ref, rout_ref, dkc, dvc, ssem, rsem):
        b, i = pl.program_id(0), pl.program_id(1)
        copies = lambda: [_pair_copy(gout_ref, rout_ref, (ssem, rsem))]

        @pl.when((b == 0) & (i == 0))
        def _():
            ds_ref[...] = jnp.zeros_like(ds_ref)
            for cp in copies():
                cp.start()

        @pl.when((b == nbatch - 1) & (i == ni))
        def _():
            for cp in copies():
                cp.wait_recv()
            for cp in copies():
                cp.wait_send()

        @pl.when(i == 0)
        def _():
            dkc[...] = jnp.zeros_like(dkc)
            dvc[...] = jnp.zeros_like(dvc)

        @pl.when(i < ni)
        def _():
            m0b = _lane_mask(256, BF16)
            m0 = _lane_mask(128) > 0.5
            for kv in range(2):
                kcols = slice(kv * 128, (kv + 1) * 128)
                dk_own, dv_own = dkc[:, kcols], dvc[:, kcols]
                for s in range(SWA_SUB):
                    mask = _swa_mask(True if s else i > 0)
                    rows = slice(s * 128, (s + 1) * 128)
                    kst = _swa_keys(kp_ref, km_ref, s, kv, m0b)
                    vst = _swa_keys(vp_ref, vm_ref, s, kv, m0b)
                    q, do = _stack_pairs(q_ref, rows, kv), _stack_pairs(do_ref, rows, kv)
                    probs, psink = _swa_scores(q, kst, mask, sink_ref, kv)
                    dp_all = _dot_nt(do, vst)
                    ds_parts = []
                    for hh in range(2):
                        dp = dp_all[:, hh * 256:(hh + 1) * 256]
                        delta = jnp.sum(probs[hh] * dp, axis=1, keepdims=True)
                        ds_parts.append(probs[hh] * (dp - delta))
                        dsink = psink[hh] * delta
                        for pr in range(2):
                            h = kv * 4 + pr * 2 + hh
                            ds_ref[h:h + 1, :] = ds_ref[h:h + 1, :] - jnp.sum(dsink[pr * 128:(pr + 1) * 128])
                    ds_all = jnp.concatenate(ds_parts, axis=1).astype(BF16)
                    p_all = jnp.concatenate(probs, axis=1).astype(BF16)
                    dq2 = _dot(ds_all, kst) * SCALE
                    dq_ref[rows, kv * 256:kv * 256 + 128] = dq2[0:128]
                    dq_ref[rows, kv * 256 + 128:(kv + 1) * 256] = dq2[128:256]
                    dkst = _dot_tn(ds_all, q)
                    dvst = _dot_tn(p_all, do)
                    dk_ref[rows, kcols] = dk_own + jnp.where(m0, dkst[0:128], dkst[256:384])
                    dv_ref[rows, kcols] = dv_own + jnp.where(m0, dvst[0:128], dvst[256:384])
                    dk_own = jnp.where(m0, dkst[128:256], dkst[384:512])
                    dv_own = jnp.where(m0, dvst[128:256], dvst[384:512])
                dkc[:, kcols] = dk_own
                dvc[:, kcols] = dv_own

        @pl.when(i == ni)
        def _():
            dk_ref[...] = jnp.zeros_like(dk_ref)
            dv_ref[...] = jnp.zeros_like(dv_ref)
            dk_ref[0:128, :] = dkc[...]
            dv_ref[0:128, :] = dvc[...]

    main = lambda b, i: (b * ni + jnp.minimum(i, ni - 1), 0)
    prev = lambda b, i: ((b * ni + jnp.minimum(i, ni - 1)) * SWA_SUB - jnp.where(jnp.minimum(i, ni - 1) > 0, 1, 0), 0)
    shifted = lambda b, i: (b * (ni + 1) + i, 0)
    tpad = nbatch * (ni + 1) * SWA_Q
    return pl.pallas_call(
        body, name="swa_bwd", grid=(nbatch, ni + 1),
        in_specs=[pl.BlockSpec(memory_space=pltpu.SMEM), pl.BlockSpec((SWA_Q, 512), main),
                  pl.BlockSpec((128, 256), prev), pl.BlockSpec((SWA_Q, 256), main),
                  pl.BlockSpec((128, 256), prev), pl.BlockSpec((SWA_Q, 256), main),
                  pl.BlockSpec((SWA_Q, 512), main), pl.BlockSpec(memory_space=pl.ANY)],
        out_specs=[pl.BlockSpec((SWA_Q, 512), main), pl.BlockSpec((SWA_Q, 256), shifted),
                   pl.BlockSpec((SWA_Q, 256), shifted), _const_spec((8, 128)), pl.BlockSpec(memory_space=pl.ANY)],
        out_shape=[jax.ShapeDtypeStruct((T, 512), F32), jax.ShapeDtypeStruct((tpad, 256), F32),
                   jax.ShapeDtypeStruct((tpad, 256), F32), jax.ShapeDtypeStruct((8, 128), F32),
                   jax.ShapeDtypeStruct((4, 1) + gout4.shape[2:], gout4.dtype)],
        scratch_shapes=[pltpu.VMEM((128, 256), F32), pltpu.VMEM((128, 256), F32),
                        pltpu.SemaphoreType.DMA((1,)), pltpu.SemaphoreType.DMA((1,))],
        compiler_params=_params(2),
    )(sinks, qrot, k2, k2, v2, v2, doa, gout4)


SB_T = 256
SB_TQ = 2 * SB_T


def _sb_mask(kind):
    if kind == "full":
        return None
    rows = SB_T if kind == "B" else SB_TQ
    r = lax.broadcasted_iota(jnp.int32, (rows, 2 * SB_T), 0)
    c = lax.broadcasted_iota(jnp.int32, (rows, 2 * SB_T), 1)
    causal = (c & (SB_T - 1)) < r
    return causal | (r >= SB_T) if kind == "A" else causal


def _sb_rows(kind):
    return slice(SB_T, SB_TQ) if kind == "B" else slice(0, SB_TQ)


def _lower(x):
    return jnp.concatenate([jnp.zeros_like(x), x], axis=0)


def _sb_logits(neg_q, kst):
    nz = _dot_nt(neg_q, kst)
    sign = jnp.uint32(0x80000000)
    neg_abs = lax.bitcast_convert_type(lax.bitcast_convert_type(nz, jnp.uint32) | sign, F32)
    return nz, jnp.minimum(nz, 0.0) - jnp.log(1.0 + jnp.exp(neg_abs))


SB_NP = 4


def _pair_rows(ref, j, cols, m0b):
    kj = ref[pl.ds(pl.multiple_of(j * SB_T, SB_T), SB_T), cols]
    k0 = kj * m0b
    return jnp.concatenate([k0, kj - k0], axis=0)


def _bcast2(c0, c1):
    rows = c0.shape[0]
    return jnp.concatenate([jnp.broadcast_to(c0, (rows, SB_T)), jnp.broadcast_to(c1, (rows, SB_T))], axis=1)


def _rowsum2(x):
    return jnp.sum(x[:, 0:SB_T], axis=1, keepdims=True), jnp.sum(x[:, SB_T:2 * SB_T], axis=1, keepdims=True)


def _scan2(x, tri2):
    outs = []
    for h in range(2):
        xh = x[:, h * SB_T:(h + 1) * SB_T]
        hi = xh.astype(BF16)
        lo = (xh - hi.astype(F32)).astype(BF16)
        outs.append(_dot(jnp.concatenate([hi, lo], axis=1), tri2))
    return jnp.concatenate(outs, axis=1)


def _scan1(x, tri):
    xb = x.astype(BF16)
    return jnp.concatenate([_dot(xb[:, h * SB_T:(h + 1) * SB_T], tri) for h in range(2)], axis=1)


def _sb_fwd(qb, kb, vb, ublk, nbatch, seq):
    nq, nk = seq // SB_TQ, seq // SB_T
    T = nbatch * seq

    def body(q_ref, k_ref, v_ref, u_ref, o_ref, wst_ref, ost_ref, wbuf, obuf, sems):
        b, i = pl.program_id(0), pl.program_id(2)
        m0b = _lane_mask(SB_T, BF16)
        u = u_ref[...]
        pairs = [slice(pp * 128, (pp + 1) * 128) for pp in range(SB_NP)]
        neg_qs = [-q_ref[:, cols] for cols in pairs]

        def stores(j, slot):
            tix = (b * nq + i) * nk + j
            return (pltpu.make_async_copy(wbuf.at[slot], wst_ref.at[tix], sems.at[0, slot]),
                    pltpu.make_async_copy(obuf.at[slot], ost_ref.at[tix], sems.at[1, slot]))

        def tile(j, carries, kind, slot, wait):
            rows, mask = _sb_rows(kind), _sb_mask(kind)
            if wait:
                for cp in stores(j, slot):
                    cp.wait()
            out = []
            for pp, (cols, neg_q, (c0, c1, acc)) in enumerate(zip(pairs, neg_qs, carries)):
                kst = _pair_rows(k_ref, j, cols, m0b)
                vst = _pair_rows(v_ref, j, cols, m0b)
                nz, lb = _sb_logits(neg_q[rows], kst)
                if mask is not None:
                    lb = jnp.where(mask, lb, 0.0)
                incl = _scan2(lb, u)
                w = jnp.exp(incl - nz if kind == "B" else incl + _bcast2(c0, c1) - nz)
                if mask is not None:
                    w = jnp.where(mask, w, 0.0)
                wb = w.astype(BF16)
                wbuf[slot, pp, rows, :] = wb
                obuf[slot, pp, rows, :] = lb.astype(BF16)
                if kind == "B":
                    wbuf[slot, pp, 0:SB_T, :] = jnp.zeros((SB_T, 2 * SB_T), BF16)
                    obuf[slot, pp, 0:SB_T, :] = jnp.zeros((SB_T, 2 * SB_T), BF16)
                d0, d1, da = incl[:, 0:1], incl[:, SB_T:SB_T + 1], _dot(wb, vst)
                if kind == "B":
                    d0, d1, da = _lower(d0), _lower(d1), _lower(da)
                out.append((c0 + d0, c1 + d1, acc + da))
            for thread, cp in enumerate(stores(j, slot)):
                cp.start(priority=thread)
            return tuple(out)

        zc = jnp.zeros((SB_TQ, 1), F32)
        carries = tile(2 * i + 1, ((zc, zc, jnp.zeros((SB_TQ, 128), F32)),) * SB_NP, "B", 0, False)
        carries = tile(2 * i, carries, "A", 1, False)

        def two(jj, cr):
            cr = tile(2 * i - 1 - 2 * jj, cr, "full", 0, True)
            return tile(2 * i - 2 - 2 * jj, cr, "full", 1, True)

        carries = lax.fori_loop(0, i, two, carries)
        for cols, carry in zip(pairs, carries):
            o_ref[:, cols] = carry[2].astype(BF16)
        for slot in range(2):
            for cp in stores(0, slot):
                cp.wait()

    wide = 128 * SB_NP
    blk = lambda b, g, i: (b * nq + i, g)
    full = lambda b, g, i: (b, g)
    hbm = pl.BlockSpec(memory_space=pl.ANY)
    tiles = jax.ShapeDtypeStruct((nbatch * nq * nk, SB_NP, SB_TQ, 2 * SB_T), BF16)
    return pl.pallas_call(
        body, name="sb_fwd", grid=(nbatch, 4 // SB_NP, nq),
        in_specs=[pl.BlockSpec((SB_TQ, wide), blk), pl.BlockSpec((seq, wide), full), pl.BlockSpec((seq, wide), full),
                  _const_spec((2 * SB_T, SB_T))],
        out_specs=[pl.BlockSpec((SB_TQ, wide), blk), hbm, hbm],
        out_shape=[jax.ShapeDtypeStruct((T, 512), BF16), tiles, tiles],
        scratch_shapes=[pltpu.VMEM((2, SB_NP, SB_TQ, 2 * SB_T), BF16), pltpu.VMEM((2, SB_NP, SB_TQ, 2 * SB_T), BF16),
                        pltpu.SemaphoreType.DMA((2, 2))],
        compiler_params=_params(3),
    )(qb, kb, vb, ublk)


def _sb_bwd(qb, kb, vb, dob, wst, ost, pblk, cout4, nbatch, seq):
    nq, nk = seq // SB_TQ, seq // SB_T
    T = nbatch * seq
    ng = 4 // SB_NP

    def body(q_ref, k_ref, v_ref, do_ref, wst_ref, ost_ref, up_ref, cout_ref, dq_ref, dk_ref, dv_ref, rout_ref,
             wbuf, obuf, sems, ssem, rsem):
        b, g, i = pl.program_id(0), pl.program_id(1), pl.program_id(2)
        copies = lambda: _chip_exchange(cout_ref, rout_ref, (ssem, rsem))

        @pl.when((b == 0) & (g == 0) & (i == 0))
        def _():
            for cp in copies():
                cp.start()

        @pl.when(i == 0)
        def _():
            dk_ref[...] = jnp.zeros_like(dk_ref)
            dv_ref[...] = jnp.zeros_like(dv_ref)

        m0b = _lane_mask(SB_T, BF16)
        m0 = _lane_mask(SB_T) > 0.5
        up = up_ref[...]
        pairs = [slice(pp * 128, (pp + 1) * 128) for pp in range(SB_NP)]

        def loads(j, slot):
            tix = (b * nq + i) * nk + j
            return (pltpu.make_async_copy(wst_ref.at[tix], wbuf.at[slot], sems.at[0, slot]),
                    pltpu.make_async_copy(ost_ref.at[tix], obuf.at[slot], sems.at[1, slot]))

        def tile(j, carries, kind, slot, fetch_next):
            rows, mask = _sb_rows(kind), _sb_mask(kind)
            if fetch_next:
                for thread, cp in enumerate(loads(j + 1, 1 - slot)):
                    cp.start(priority=thread)
            for cp in loads(j, slot):
                cp.wait()
            out = []
            for pp, (cols, (s0, s1, dq)) in enumerate(zip(pairs, carries)):
                q, do = q_ref[rows, cols], do_ref[rows, cols]
                kst = _pair_rows(k_ref, j, cols, m0b)
                vst = _pair_rows(v_ref, j, cols, m0b)
                wb = wbuf[slot, pp, rows, :]
                e = _dot_nt(do, vst) * wb.astype(F32)
                dlb = _bcast2(s0[rows], s1[rows]) + _scan1(e, up)
                dz = (e + dlb) * jnp.exp(obuf[slot, pp, rows, :].astype(F32)) - dlb
                if mask is not None:
                    dz = jnp.where(mask, dz, 0.0)
                dzb = dz.astype(BF16)
                dkst = _dot_tn(dzb, q)
                dvst = _dot_tn(wb, do)
                keys = pl.ds(pl.multiple_of(j * SB_T, SB_T), SB_T)
                dk_ref[keys, cols] = dk_ref[keys, cols] + jnp.where(m0, dkst[0:SB_T], dkst[SB_T:2 * SB_T])
                dv_ref[keys, cols] = dv_ref[keys, cols] + jnp.where(m0, dvst[0:SB_T], dvst[SB_T:2 * SB_T])
                x0, x1 = _rowsum2(e)
                ddq = _dot(dzb, kst)
                if kind == "B":
                    x0, x1, ddq = _lower(x0), _lower(x1), _lower(ddq)
                out.append((s0 + x0, s1 + x1, dq + ddq))
            return tuple(out)

        for thread, cp in enumerate(loads(0, 0)):
            cp.start(priority=thread)

        def two(jj, cr):
            cr = tile(2 * jj, cr, "full", 0, True)
            return tile(2 * jj + 1, cr, "full", 1, True)

        zc = jnp.zeros((SB_TQ, 1), F32)
        carries = lax.fori_loop(0, i, two, ((zc, zc, jnp.zeros((SB_TQ, 128), F32)),) * SB_NP)
        carries = tile(2 * i, carries, "A", 0, True)
        carries = tile(2 * i + 1, carries, "B", 1, False)
        for cols, carry in zip(pairs, carries):
            dq_ref[:, cols] = carry[2] * SCALE

        @pl.when((b == nbatch - 1) & (g == ng - 1) & (i == nq - 1))
        def _():
            for cp in copies():
                cp.wait_recv()
            for cp in copies():
                cp.wait_send()

    wide = 128 * SB_NP
    blk = lambda b, g, i: (b * nq + i, g)
    full = lambda b, g, i: (b, g)
    hbm = pl.BlockSpec(memory_space=pl.ANY)
    return pl.pallas_call(
        body, name="sb_bwd", grid=(nbatch, ng, nq),
        in_specs=[pl.BlockSpec((SB_TQ, wide), blk), pl.BlockSpec((seq, wide), full), pl.BlockSpec((seq, wide), full),
                  pl.BlockSpec((SB_TQ, wide), blk), hbm, hbm, _const_spec((SB_T, SB_T)), hbm],
        out_specs=[pl.BlockSpec((SB_TQ, wide), blk), pl.BlockSpec((seq, wide), full), pl.BlockSpec((seq, wide), full),
                   hbm],
        out_shape=[jax.ShapeDtypeStruct((T, 512), F32)] * 3 + [jax.ShapeDtypeStruct((3,) + cout4.shape[1:], cout4.dtype)],
        scratch_shapes=[pltpu.VMEM((2, SB_NP, SB_TQ, 2 * SB_T), BF16), pltpu.VMEM((2, SB_NP, SB_TQ, 2 * SB_T), BF16),
                        pltpu.SemaphoreType.DMA((2, 2)), pltpu.SemaphoreType.DMA((3,)), pltpu.SemaphoreType.DMA((3,))],
        compiler_params=_params(3),
    )(qb, kb, vb, dob, wst, ost, pblk, cout4)


def _sigmoid(g):
    return 1.0 / (1.0 + jnp.exp(-g))


def _out_proj(oa, ob, ga, gb, x2, tgt, wout_bf):
    T = x2.shape[0]

    def body(oa_ref, ob_ref, ga_ref, gb_ref, x_ref, t_ref, w_ref,
             dout_ref, doa_ref, dob_ref, dga_ref, dgb_ref, dw_ref, loss_ref):
        @pl.when(pl.program_id(0) == 0)
        def _():
            loss_ref[...] = jnp.zeros_like(loss_ref)
            dw_ref[...] = jnp.zeros_like(dw_ref)

        halves = ((oa_ref, ga_ref, doa_ref, dga_ref, 0), (ob_ref, gb_ref, dob_ref, dgb_ref, 512))
        out = x_ref[...]
        gated = []
        for o_ref, g_ref, _, _, lo in halves:
            g = g_ref[...]
            sg = _sigmoid(g)
            y = (o_ref[...] * (g * sg)).astype(BF16)
            out = out + _dot(y, w_ref[lo:lo + 512, :])
            gated.append((y, g, sg))
        diff = out - t_ref[...]
        dout = diff * (1.0 / D_MODEL)
        dout_ref[...] = dout
        loss_ref[...] = loss_ref[...] + jnp.sum(diff * diff) * (0.5 / D_MODEL)
        db = dout.astype(BF16)
        for (o_ref, _, do_ref, dg_ref, lo), (y, g, sg) in zip(halves, gated):
            dw_ref[lo:lo + 512, :] = dw_ref[lo:lo + 512, :] + _dot_tn(y, db)
            dy = _dot_nt(db, w_ref[lo:lo + 512, :])
            do_ref[...] = (dy * (g * sg)).astype(BF16)
            dg_ref[...] = (dy * o_ref[...] * (sg * (1.0 + g * (1.0 - sg)))).astype(BF16)

    rows = lambda w: pl.BlockSpec((TM, w), lambda i: (i, 0))
    return pl.pallas_call(
        body, name="out_proj", grid=(T // TM,),
        in_specs=[rows(512), rows(512), rows(512), rows(512), rows(D_MODEL), rows(D_MODEL),
                  _const_spec((D_MODEL, D_MODEL))],
        out_specs=[rows(D_MODEL)] + [rows(512)] * 4 + [_const_spec((D_MODEL, D_MODEL)), _const_spec((8, 128))],
        out_shape=[jax.ShapeDtypeStruct((T, D_MODEL), F32)] + [jax.ShapeDtypeStruct((T, 512), BF16)] * 4
        + [jax.ShapeDtypeStruct((D_MODEL, D_MODEL), F32), jax.ShapeDtypeStruct((8, 128), F32)],
        compiler_params=_params(1),
    )(oa, ob, ga, gb, x2, tgt, wout_bf)


def _qk_grad(qraw, kraw, dqrot, dkpad, dvpad, qg512, kg128, g512, cos, sa, sb, seq):
    T = qraw.shape[0]
    assert TM == SWA_Q
    ni = seq // SWA_Q

    def body(qraw_ref, kraw_ref, dqrot_ref, dk0, dk1, dk2, dk3, dv0, dv1, dv2, dv3,
             qg_ref, kg_ref, g_ref, cos_ref, sa_ref, sb_ref, dqa_ref, dkv_ref, dqg_ref, dkg_ref):
        @pl.when(pl.program_id(0) == 0)
        def _():
            dqg_ref[...] = jnp.zeros_like(dqg_ref)
            dkg_ref[...] = jnp.zeros_like(dkg_ref)

        cosv, sav, sbv = cos_ref[...], sa_ref[...], sb_ref[...]
        m0 = _lane_mask(TM) > 0.5

        def head_norm_bwd(raw, dn_rot, gmat, gain):
            r = lax.rsqrt(_dot_b(raw * raw, gmat) + EPS)
            n = raw * r
            dn = dn_rot * gain
            return r * (dn - n * _dot_b(dn * n, gmat)), jnp.sum(dn_rot * n, axis=0, keepdims=True)

        def fold(refs):
            a = jnp.concatenate([r[:, 0:128] for r in refs], axis=0)
            b = jnp.concatenate([r[:, 128:256] for r in refs], axis=0)
            return jnp.where(m0, a + pltpu.roll(a, 64, 1), b + pltpu.roll(b, 64, 1))

        dqn = jnp.concatenate([_rope_bwd(dqrot_ref[:, s * 128:(s + 1) * 128], cosv, sav, sbv) for s in range(4)], axis=1)
        dqa, dqg = head_norm_bwd(qraw_ref[...], dqn, g_ref[...], qg_ref[...])
        dka, dkg = head_norm_bwd(kraw_ref[...], _rope_bwd(fold((dk0, dk1, dk2, dk3)), cosv, sav, sbv),
                                 g_ref[0:128, 0:128], kg_ref[...])
        dqa_ref[...] = dqa.astype(BF16)
        dkv_ref[:, 0:128] = dka.astype(BF16)
        dkv_ref[:, 128:256] = fold((dv0, dv1, dv2, dv3)).astype(BF16)
        dqg_ref[...] = dqg_ref[...] + dqg
        dkg_ref[...] = dkg_ref[...] + dkg

    rows = lambda w: pl.BlockSpec((TM, w), lambda i: (i, 0))
    sub = SWA_Q // BLOCK
    shifted = [pl.BlockSpec((BLOCK, 256), functools.partial(
        lambda i, j: ((i // ni) * (ni + 1) * sub + (i % ni) * sub + 1 + j, 0), j=j)) for j in range(sub)]
    return pl.pallas_call(
        body, name="qk_grad", grid=(T // TM,),
        in_specs=[rows(512), rows(128), rows(512)] + shifted + shifted + [
            _const_spec((1, 512)), _const_spec((1, 128)), _const_spec((512, 512)), rows(128), rows(128), rows(128)],
        out_specs=[rows(512), rows(256), _const_spec((1, 512)), _const_spec((1, 128))],
        out_shape=[jax.ShapeDtypeStruct((T, 512), BF16), jax.ShapeDtypeStruct((T, 256), BF16),
                   jax.ShapeDtypeStruct((1, 512), F32), jax.ShapeDtypeStruct((1, 128), F32)],
        compiler_params=_params(1),
    )(qraw, kraw, dqrot, *([dkpad] * sub), *([dvpad] * sub), qg512, kg128, g512, cos, sa, sb)


_PIECES = ((QA, 512), (KA, 256), (GA, 512), (QB, 512), (KB, 512), (VB, 512), (GB, 512))


def _w_in_grad(h, pieces):
    T = h.shape[0]

    def body(h_ref, *refs):
        dw_ref = refs[-1]

        @pl.when(pl.program_id(0) == 0)
        def _():
            dw_ref[...] = jnp.zeros_like(dw_ref)

        hb = h_ref[...]
        for (lo, width), p_ref in zip(_PIECES, refs[:-1]):
            dw_ref[lo:lo + width, :] = dw_ref[lo:lo + width, :] + _dot_tn(p_ref[...].astype(BF16), hb)

    rows = lambda w: pl.BlockSpec((TM, w), lambda i: (i, 0))
    return pl.pallas_call(
        body, name="w_in_grad", grid=(T // TM,),
        in_specs=[rows(D_MODEL)] + [rows(w) for _, w in _PIECES],
        out_specs=_const_spec((IN_WIDTH, D_MODEL)),
        out_shape=jax.ShapeDtypeStruct((IN_WIDTH, D_MODEL), F32),
        compiler_params=_params(1),
    )(h, *pieces)


def _chip_exchange(src_ref, dst_ref, sems):
    _, _, c, chips = _place()
    return [_remote(src_ref.at[2 * cx + cy], dst_ref.at[j], sems, j, (cx, cy, c)) for j, (cx, cy) in enumerate(chips)]


def _x_grad(x2, dout, pieces, w_bf, ng, cin4):
    T = x2.shape[0]
    npc = len(_PIECES)
    steps = T // TM

    def body(x_ref, dout_ref, *refs):
        w_ref, ng_ref, cin_ref, gx_ref, dng_ref, rin_ref, ssem, rsem = refs[npc:]
        step = pl.program_id(0)
        copies = lambda: _chip_exchange(cin_ref, rin_ref, (ssem, rsem))

        @pl.when(step == 0)
        def _():
            dng_ref[...] = jnp.zeros_like(dng_ref)
            for cp in copies():
                cp.start()

        dh = jnp.zeros((TM, D_MODEL), F32)
        for (lo, width), p_ref in zip(_PIECES, refs[:npc]):
            dh = dh + _dot(p_ref[...].astype(BF16), w_ref[lo:lo + width, :])
        xb = x_ref[...]
        r = lax.rsqrt(jnp.mean(xb * xb, axis=-1, keepdims=True) + EPS)
        n = xb * r
        dn = dh * ng_ref[...]
        gx_ref[...] = dout_ref[...] + r * (dn - n * jnp.mean(dn * n, axis=-1, keepdims=True))
        dng_ref[...] = dng_ref[...] + jnp.sum(dh * n, axis=0, keepdims=True)

        @pl.when(step == steps - 1)
        def _():
            for cp in copies():
                cp.wait_recv()
            for cp in copies():
                cp.wait_send()

    rows = lambda w: pl.BlockSpec((TM, w), lambda i: (i, 0))
    hbm = pl.BlockSpec(memory_space=pl.ANY)
    return pl.pallas_call(
        body, name="x_grad", grid=(steps,),
        in_specs=[rows(D_MODEL), rows(D_MODEL)] + [rows(w) for _, w in _PIECES]
        + [_const_spec((IN_WIDTH, D_MODEL)), _const_spec((1, D_MODEL)), hbm],
        out_specs=[rows(D_MODEL), _const_spec((1, D_MODEL)), hbm],
        out_shape=[jax.ShapeDtypeStruct((T, D_MODEL), F32), jax.ShapeDtypeStruct((1, D_MODEL), F32),
                   jax.ShapeDtypeStruct((3,) + cin4.shape[1:], cin4.dtype)],
        scratch_shapes=[pltpu.SemaphoreType.DMA((3,)), pltpu.SemaphoreType.DMA((3,))],
        compiler_params=_params(1),
    )(x2, dout, *pieces, w_bf, ng, cin4)


HBM = pl.BlockSpec(memory_space=pl.ANY)


def _place():
    x, y, c = lax.axis_index("x"), lax.axis_index("y"), lax.axis_index("c")
    chips = [(1 - x, y), (x, 1 - y), (1 - x, 1 - y)]
    return x, y, c, chips


def _remote(src, dst, sems, k, to):
    return pltpu.make_async_remote_copy(src_ref=src, dst_ref=dst, send_sem=sems[0].at[k], recv_sem=sems[1].at[k],
                                        device_id=to, device_id_type=MESH)


def _gather_plan(src, dst, sems, relay):
    x, y, c, _ = _place()
    rows = src.shape[1] // 2
    parts = (pl.ds(0, rows), pl.ds(rows, rows))
    me, kx, ky, kd = 2 * x + y, 2 * (1 - x) + y, 2 * x + 1 - y, 2 * (1 - x) + 1 - y
    to_x, to_y, to_d, sib = (1 - x, y, c), (x, 1 - y, c), (1 - x, 1 - y, c), (x, y, 1 - c)
    mine = [(src.at[c, p], dst.at[me, c, p]) for p in parts]
    direct = [_remote(*mine[0], sems, 0, to_x), _remote(*mine[1], sems, 1, to_x),
              _remote(*mine[1], sems, 2, to_y), _remote(*mine[0], sems, 3, to_y)]
    arrived = [dst.at[kx, c, parts[0]], dst.at[kx, c, parts[1]], dst.at[ky, c, parts[1]], dst.at[ky, c, parts[0]]]
    if relay:
        relays = [_remote(arrived[0], arrived[0], sems, 4, to_y), _remote(arrived[2], arrived[2], sems, 5, to_x)]
    else:
        relays = []
        direct += [_remote(*mine[0], sems, 4, to_d), _remote(*mine[1], sems, 5, to_d)]
    relayed = [dst.at[kd, c, parts[0]], dst.at[kd, c, parts[1]]]
    forwards = [_remote(dst.at[k, c], dst.at[k, c], sems, 6 + n, sib) for n, k in enumerate((kx, ky, kd))]
    from_sib = [dst.at[k, 1 - c] for k in (kx, ky, kd)]
    return direct, arrived, relays, relayed, forwards, from_sib


def _gather_start(src, dst, sems, relay):
    direct = _gather_plan(src, dst, sems, relay)[0]
    for k in (0, 2, 1, 3):
        direct[k].start()
    for cp in direct[4:]:
        cp.start()


def _gather_finish(src, dst, sems, relay):
    direct, arrived, relays, relayed, forwards, from_sib = _gather_plan(src, dst, sems, relay)
    landed = lambda ref, k: _remote(ref, ref, sems, k, (0, 0, 0)).wait_recv()
    landed(arrived[0], 0)
    if relay:
        relays[0].start()
    landed(arrived[2], 2)
    if relay:
        relays[1].start()
    landed(arrived[1], 1)
    forwards[0].start()
    landed(arrived[3], 3)
    forwards[1].start()
    landed(relayed[0], 4)
    landed(relayed[1], 5)
    forwards[2].start()
    for n, ref in enumerate(from_sib):
        landed(ref, 6 + n)
    for cp in direct + relays + forwards:
        cp.wait_send()


def _own_slot(gathered, shard):
    me = 2 * lax.axis_index("x") + lax.axis_index("y")
    return lax.dynamic_update_slice(gathered, shard[None], (me, 0, 0, 0))


def _pair_copy(src, dst, sems):
    x, y, c, _ = _place()
    return _remote(src.at[:, pl.ds(1 - c, 1)], dst, sems, 0, (x, y, 1 - c))


def _pair_exchange(g4):
    def body(src, dst, ssem, rsem):
        cp = _pair_copy(src, dst, (ssem, rsem))
        cp.start()
        cp.wait()

    return pl.pallas_call(
        body, name=f"pair_exchange_{g4.shape[2]}", in_specs=[HBM], out_specs=HBM,
        out_shape=jax.ShapeDtypeStruct((4, 1) + g4.shape[2:], g4.dtype),
        scratch_shapes=[pltpu.SemaphoreType.DMA((1,)), pltpu.SemaphoreType.DMA((1,))],
    )(g4)


def _share_halves(hin, hout, small):
    def body(hin_ref, hout_ref, small_ref, oin_ref, oout_ref, sall_ref, ssem, rsem, lsem):
        x, y, c, _ = _place()
        sems = (ssem, rsem)
        sib = (x, y, 1 - c)
        me = 4 * x + 2 * y + c
        own = pltpu.make_async_copy(small_ref, sall_ref.at[me], lsem.at[0])
        own.start()
        sent = [_remote(hin_ref, oin_ref, sems, 0, sib), _remote(hout_ref, oout_ref, sems, 1, sib)]
        flips = [(fx, fy, fc) for fx in (0, 1) for fy in (0, 1) for fc in (0, 1)][1:]
        for k, (fx, fy, fc) in enumerate(flips):
            sent.append(_remote(small_ref, sall_ref.at[me], sems, 2 + k, (x ^ fx, y ^ fy, c ^ fc)))
        for cp in sent:
            cp.start()
        _remote(hin_ref, oin_ref, sems, 0, sib).wait_recv()
        _remote(hout_ref, oout_ref, sems, 1, sib).wait_recv()
        for k, (fx, fy, fc) in enumerate(flips):
            src = 4 * (x ^ fx) + 2 * (y ^ fy) + (c ^ fc)
            _remote(small_ref, sall_ref.at[src], sems, 2 + k, sib).wait_recv()
        for cp in sent:
            cp.wait_send()
        own.wait()

    return pl.pallas_call(
        body, name="share_halves", in_specs=[HBM, HBM, HBM], out_specs=[HBM, HBM, HBM],
        out_shape=[jax.ShapeDtypeStruct(hin.shape, F32), jax.ShapeDtypeStruct(hout.shape, F32),
                   jax.ShapeDtypeStruct((8,) + small.shape, F32)],
        scratch_shapes=[pltpu.SemaphoreType.DMA((9,)), pltpu.SemaphoreType.DMA((9,)), pltpu.SemaphoreType.DMA((1,))],
    )(hin, hout, small)


def _add_half(cidx, full4, recv4):
    _, _, rows, width = full4.shape

    def body(c_ref, a_ref, b_ref, o_ref):
        o_ref[0] = (a_ref[0, 0] + b_ref[0, 0]).astype(BF16)

    return pl.pallas_call(
        body, name=f"add_half_{rows}",
        grid_spec=pltpu.PrefetchScalarGridSpec(
            num_scalar_prefetch=1, grid=(4,),
            in_specs=[pl.BlockSpec((1, 1, rows, width), lambda k, c: (k, c[0], 0, 0)),
                      pl.BlockSpec((1, 1, rows, width), lambda k, c: (k, 0, 0, 0))],
            out_specs=pl.BlockSpec((1, rows, width), lambda k, c: (k, 0, 0))),
        out_shape=jax.ShapeDtypeStruct((4, rows, width), BF16),
        compiler_params=_params(1),
    )(cidx, full4, recv4)


def _sum_chips(chip, own4, recv3):
    _, rows, width = recv3.shape
    rb = rows // 2

    def body(k_ref, a_ref, r_ref, o_ref):
        acc = a_ref[0].astype(F32)
        for s in range(3):
            acc = acc + r_ref[s].astype(F32)
        o_ref[...] = acc

    return pl.pallas_call(
        body, name=f"sum_chips_{rows}",
        grid_spec=pltpu.PrefetchScalarGridSpec(
            num_scalar_prefetch=1, grid=(rows // rb,),
            in_specs=[pl.BlockSpec((1, rb, width), lambda i, k: (k[0], i, 0)),
                      pl.BlockSpec((3, rb, width), lambda i, k: (0, i, 0))],
            out_specs=pl.BlockSpec((rb, width), lambda i, k: (i, 0))),
        out_shape=jax.ShapeDtypeStruct((rows, width), F32),
        compiler_params=_params(1),
    )(chip, own4, recv3)


def _adam_math(w, g, m, v):
    c1 = 1.0 - ADAM_B1 ** ADAM_STEP
    c2 = 1.0 - ADAM_B2 ** ADAM_STEP
    nm = ADAM_B1 * m + (1.0 - ADAM_B1) * g
    nv = ADAM_B2 * v + (1.0 - ADAM_B2) * (g * g)
    return -ADAM_LR * ((nm / c1) / (jnp.sqrt(nv / c2) + ADAM_EPS) + ADAM_WD * w), nm, nv


def _small_update(small_all, params):
    n = len(params)
    flat = [a for p in params for a in p]

    def body(s_ref, *refs):
        ins, loss_ref, outs = refs[:3 * n], refs[3 * n], refs[3 * n + 1:]
        total = s_ref[0]
        for d in range(1, 8):
            total = total + s_ref[d]
        loss_ref[...] = total[n:n + 1, 0:1]
        for r in range(n):
            w_ref, m_ref, v_ref = ins[3 * r:3 * r + 3]
            g = total[r:r + 1, 0:w_ref.shape[1]]
            outs[4 * r][...] = g
            outs[4 * r + 1][...], outs[4 * r + 2][...], outs[4 * r + 3][...] = _adam_math(w_ref[...], g, m_ref[...], v_ref[...])

    whole = lambda a: pl.BlockSpec(a.shape, lambda i: (0,) * a.ndim)
    out_shape = [jax.ShapeDtypeStruct((1, 1), F32)] + [jax.ShapeDtypeStruct(p[0].shape, F32) for p in params for _ in range(4)]
    res = pl.pallas_call(
        body, name="small_update", grid=(1,),
        in_specs=[whole(small_all)] + [whole(a) for a in flat], out_specs=[whole(s) for s in out_shape],
        out_shape=out_shape, compiler_params=_params(1),
    )(small_all, *flat)
    return res[0], [tuple(res[1 + 4 * r:5 + 4 * r]) for r in range(n)]


def _adamw_halves(cidx, w, own, recv, m, v):
    rows, width = w.shape
    rb = rows // 4

    def body(c_ref, w_ref, own_ref, recv_ref, m_ref, v_ref, g_ref, d_ref, nm_ref, nv_ref):
        mine = (pl.program_id(0) // 2) == c_ref[0]
        g = jnp.where(mine, own_ref[...], recv_ref[...])
        g_ref[...] = g
        d_ref[...], nm_ref[...], nv_ref[...] = _adam_math(w_ref[...], g, m_ref[...], v_ref[...])

    full = pl.BlockSpec((rb, width), lambda i, c: (i, 0))
    half = pl.BlockSpec((rb, width), lambda i, c: (i % 2, 0))
    return pl.pallas_call(
        body, name=f"adamw_halves_{rows}",
        grid_spec=pltpu.PrefetchScalarGridSpec(
            num_scalar_prefetch=1, grid=(4,),
            in_specs=[full, half, half, full, full], out_specs=[full] * 4),
        out_shape=[jax.ShapeDtypeStruct((rows, width), F32)] * 4,
        compiler_params=_params(1),
    )(cidx, w, own, recv, m, v)


def _constants():
    idx = jnp.arange(512)
    g512 = jnp.where(idx[:, None] // HEAD_DIM == idx[None, :] // HEAD_DIM, 1.0 / HEAD_DIM, 0.0).astype(BF16)
    j = jnp.arange(SB_T)
    ublk = jnp.where(j[:, None] >= j[None, :], 1.0, 0.0).astype(BF16)
    pblk = jnp.where(j[:, None] < j[None, :], 1.0, 0.0).astype(BF16)
    return g512, jnp.concatenate([ublk, ublk], axis=0), pblk


def _pad_rows(v, width):
    return jnp.pad(v, ((0, 0), (0, width - v.shape[1])))


def _pair_sum(cidx, partial, rows):
    g4 = partial.reshape(4, 2, rows, D_MODEL)
    return _add_half(cidx, g4, _pair_exchange(g4))


def _step(x2, tgt, positions, norm_gain, q_norm_gain, k_norm_gain, sinks, win_shard, wout_shard, nbatch, seq):
    g512, ublk, pblk = _constants()
    qg512 = jnp.tile(q_norm_gain, (1, 8))
    kg128 = jnp.tile(k_norm_gain, (1, 2))
    sink1 = sinks.reshape(8)
    cidx = lax.axis_index("c").astype(jnp.int32).reshape(1)
    chip = (2 * lax.axis_index("x") + lax.axis_index("y")).astype(jnp.int32).reshape(1)

    h, (cos, sa, sb), w_bf = _prologue(x2, norm_gain, positions.reshape(-1, 1), win_shard)
    (qraw, kraw, qrot, k2, v2, ga, qb, kb, vb, gb), wout_bf = _norm_proj(
        h, w_bf, qg512, kg128, g512, cos, sa, sb, wout_shard)
    oa = _swa_fwd(sink1, qrot, k2, v2, nbatch, seq)
    ob, wst, ost = _sb_fwd(qb, kb, vb, ublk, nbatch, seq)
    dout, doa, dob, dga, dgb, dwout, loss_acc = _out_proj(oa, ob, ga, gb, x2, tgt, wout_bf)

    gout4 = dwout.reshape(4, 2, SHARD_OUT // 2, D_MODEL)
    dqrot, dk2, dv2, dsink, pout4 = _swa_bwd(sink1, qrot, k2, v2, doa, gout4, nbatch, seq)
    cout4 = _add_half(cidx, gout4, pout4)
    dqb, dkb, dvb, rout3 = _sb_bwd(qb, kb, vb, dob, wst, ost, pblk, cout4, nbatch, seq)
    dqa, dkv, dqg, dkg = _qk_grad(qraw, kraw, dqrot, dk2, dv2, qg512, kg128, g512, cos, sa, sb, seq)
    pieces = (dqa, dkv, dga, dqb, dkb, dvb, dgb)
    cin4 = _pair_sum(cidx, _w_in_grad(h, pieces), SHARD_IN // 2)
    gx, dng, rin3 = _x_grad(x2, dout, pieces, w_bf, norm_gain, cin4)
    own_in, own_out = _sum_chips(chip, cin4, rin3), _sum_chips(chip, cout4, rout3)

    dqg64 = dqg.reshape(8, HEAD_DIM).sum(axis=0, keepdims=True)
    dkg64 = dkg.reshape(2, HEAD_DIM).sum(axis=0, keepdims=True)
    small = jnp.concatenate([dng, _pad_rows(dqg64, D_MODEL), _pad_rows(dkg64, D_MODEL),
                             _pad_rows(dsink[:, 0].reshape(1, 8), D_MODEL), _pad_rows(loss_acc[0:1, 0:1], D_MODEL),
                             jnp.zeros((3, D_MODEL), F32)], axis=0)
    sib_in, sib_out, small_all = _share_halves(own_in, own_out, small)
    return gx, cidx, (own_in, sib_in), (own_out, sib_out), small_all


def kernel(x, positions, norm_gain, w_in, q_norm_gain, k_norm_gain, sinks, w_out, loss_target, m_norm_gain, m_w_in, m_q_norm_gain, m_k_norm_gain, m_sinks, m_w_out, v_norm_gain, v_w_in, v_q_norm_gain, v_k_norm_gain, v_sinks, v_w_out):
    nbatch, seq, _ = x.shape
    T = nbatch * seq
    x2 = x.reshape(T, D_MODEL)
    tgt = loss_target.reshape(T, D_MODEL)
    tr = lambda a: jnp.swapaxes(a[0], 0, 1)
    win_t, m_win_t, v_win_t = tr(w_in), tr(m_w_in), tr(v_w_in)

    win_shard = win_t.astype(BF16).reshape(2, SHARD_IN // 2, D_MODEL)
    wout_shard = w_out[0].astype(BF16).reshape(2, SHARD_OUT // 2, D_MODEL)
    gx, cidx, g_in_halves, g_out_halves, small_all = _step(
        x2, tgt, positions, norm_gain, q_norm_gain, k_norm_gain, sinks, win_shard, wout_shard, nbatch, seq)

    g_in, d_in, nm_in, nv_in = [jnp.swapaxes(a, 0, 1) for a in
                                _adamw_halves(cidx, win_t, *g_in_halves, m_win_t, v_win_t)]
    g_out, d_out, nm_out, nv_out = _adamw_halves(cidx, w_out[0], *g_out_halves, m_w_out[0], v_w_out[0])
    loss, small = _small_update(small_all, [(norm_gain, m_norm_gain, v_norm_gain), (q_norm_gain, m_q_norm_gain, v_q_norm_gain),
                                            (k_norm_gain, m_k_norm_gain, v_k_norm_gain), (sinks, m_sinks, v_sinks)])
    (g_ng, d_ng, m_ng, v_ng), (g_qg, d_qg, m_qg, v_qg), (g_kg, d_kg, m_kg, v_kg), (g_sk, d_sk, m_sk, v_sk) = small
    return (loss.reshape(()), gx.reshape(nbatch, seq, D_MODEL),
            g_ng, g_in[None], g_qg, g_kg, g_sk, g_out[None],
            d_ng, d_in[None], d_qg, d_kg, d_sk, d_out[None],
            m_ng, nm_in[None], m_qg, m_kg, m_sk, nm_out[None],
            v_ng, nv_in[None], v_qg, v_kg, v_sk, nv_out[None])
```
